```python
import math
import jax, jax.numpy as jnp
from jax import lax
import numpy as np

D_MODEL = 1024
BATCH = 4
SEQ = 4096
DEPTH = 1

DA_HEADS = 4
DA_QK_DIM = 64
DA_V_DIM = 2 * DA_QK_DIM
DA_WIDTH = DA_HEADS * DA_V_DIM
HG_HEADS = 4
HG_DK = 128
HG_DV = 128
HG_WIDTH = HG_HEADS * HG_DV
MIX_WIDTH = DA_WIDTH + HG_WIDTH
DA_Q_COLS = DA_HEADS * 2 * DA_QK_DIM
DA_K_COLS = DA_HEADS * 2 * DA_QK_DIM
DA_V_COLS = DA_WIDTH
HG_Q_COLS = HG_HEADS * HG_DK
HG_F_COLS = HG_HEADS * HG_DK
HG_I_COLS = HG_WIDTH
HG_G_COLS = HG_WIDTH
IN_WIDTH = DA_Q_COLS + DA_K_COLS + DA_V_COLS + HG_Q_COLS + HG_F_COLS + HG_I_COLS + HG_G_COLS
IN_SPLITS = (DA_Q_COLS,
             DA_Q_COLS + DA_K_COLS,
             DA_Q_COLS + DA_K_COLS + DA_V_COLS,
             DA_Q_COLS + DA_K_COLS + DA_V_COLS + HG_Q_COLS,
             DA_Q_COLS + DA_K_COLS + DA_V_COLS + HG_Q_COLS + HG_F_COLS,
             DA_Q_COLS + DA_K_COLS + DA_V_COLS + HG_Q_COLS + HG_F_COLS + HG_I_COLS)
ROPE_THETA = 500000.0
ROT_DIM = DA_QK_DIM // 4
Q_BLOCK = 128
HG_CHUNK = 64
N_GROUPS = 4
EXPERTS_PER_GROUP = 8
N_EXPERTS = N_GROUPS * EXPERTS_PER_GROUP
TOP_K = 2
EXPERT_FF = 256
MOE_BLOCK = 128
EPS = 1e-6

kernel_name = "hymba_diffattn_hgrn2_hmoe_adaln"


def rmsnorm(x, w):
    xf = x.astype(jnp.float32)
    y = xf * lax.rsqrt(jnp.mean(xf * xf, axis=-1, keepdims=True) + EPS)
    return (y * w.astype(jnp.float32)).astype(x.dtype)


def rope_partial(t, positions):
    half = ROT_DIM // 2
    inv_freq = ROPE_THETA ** (-jnp.arange(half, dtype=jnp.float32) / half)
    ang = positions.astype(jnp.float32)[..., None] * inv_freq
    cos = jnp.cos(ang)[:, :, None, :]
    sin = jnp.sin(ang)[:, :, None, :]
    tr = t[..., :ROT_DIM].astype(jnp.float32)
    t1, t2 = tr[..., :half], tr[..., half:]
    rot = jnp.concatenate([t1 * cos - t2 * sin, t2 * cos + t1 * sin], axis=-1).astype(t.dtype)
    return jnp.concatenate([rot, t[..., ROT_DIM:]], axis=-1)


def diff_attention(q, k, v, lam_q1, lam_k1, lam_q2, lam_k2, subln_w, layer_idx):
    B, S, H, _, Dk = q.shape
    lam_init = 0.8 - 0.6 * math.exp(-0.3 * layer_idx)
    lam = (jnp.exp(jnp.sum(lam_q1.astype(jnp.float32) * lam_k1.astype(jnp.float32)))
           - jnp.exp(jnp.sum(lam_q2.astype(jnp.float32) * lam_k2.astype(jnp.float32)))
           + lam_init)
    scale = Dk ** -0.5
    nb = S // Q_BLOCK
    q_blocks = q.reshape(B, nb, Q_BLOCK, H, 2, Dk).transpose(1, 0, 2, 3, 4, 5)
    key_idx = jnp.arange(S)

    def one_block(args):
        qb, bi = args
        s = jnp.einsum('bqhmd,bkhmd->bhmqk', qb, k,
                       preferred_element_type=jnp.float32) * scale
        q_idx = bi * Q_BLOCK + jnp.arange(Q_BLOCK)
        causal = key_idx[None, :] <= q_idx[:, None]
        p = jax.nn.softmax(jnp.where(causal, s, -jnp.inf), axis=-1)
        a = p[:, :, 0] - lam * p[:, :, 1]
        return jnp.einsum('bhqk,bkhd->bqhd', a.astype(v.dtype), v)

    out = lax.map(one_block, (q_blocks, jnp.arange(nb)))
    out = out.transpose(1, 0, 2, 3, 4).reshape(B, S, H, v.shape[-1])
    return rmsnorm(out, subln_w) * (1.0 - lam_init)


def hgrn2_chunkwise(q, f_raw, i, g, lb, norm_w):
    B, S, H, DK = q.shape
    DV = i.shape[-1]
    nc = S // HG_CHUNK
    f32 = jnp.float32
    qf = jax.nn.silu(q.astype(f32))
    f = lb + (1.0 - lb) * jax.nn.sigmoid(f_raw.astype(f32))
    log_f = jnp.log(f)
    kf = 1.0 - f

    def to_chunks(t):
        return t.reshape(B, nc, HG_CHUNK, H, t.shape[-1]).transpose(1, 0, 3, 2, 4)

    tri = jnp.arange(HG_CHUNK)[:, None] >= jnp.arange(HG_CHUNK)[None, :]

    def step(state, inp):
        qc, kc, vc, lfc = inp
        b = jnp.cumsum(lfc, axis=2)
        b_last = b[:, :, -1:, :]
        o_inter = jnp.einsum('bhtk,bhkv->bhtv', qc * jnp.exp(b), state)
        rel = jnp.where(tri[:, :, None], b[:, :, :, None, :] - b[:, :, None, :, :], -jnp.inf)
        scores = jnp.einsum('bhtk,bhsk,bhtsk->bhts', qc, kc, jnp.exp(rel))
        o = o_inter + jnp.einsum('bhts,bhsv->bhtv', scores, vc)
        state = (jnp.exp(b_last[:, :, 0, :])[..., None] * state
                 + jnp.einsum('bhsk,bhsv->bhkv', kc * jnp.exp(b_last - b), vc))
        return state, o

    s0 = jnp.zeros((B, H, DK, DV), f32)
    _, o = lax.scan(step, s0, (to_chunks(qf), to_chunks(kf), to_chunks(i.astype(f32)), to_chunks(log_f)))
    o = o.transpose(1, 0, 3, 2, 4).reshape(B, S, H, DV)
    o = rmsnorm(o, norm_w) * jax.nn.silu(g.astype(f32))
    return o.astype(q.dtype)


def hier_moe(h, w_group, b_group, w_router, b_router, w_gate, w_up, w_down):
    B, S, D = h.shape
    T = B * S
    t = h.reshape(T, D)
    g_logits = jnp.einsum('td,dg->tg', t, w_group, preferred_element_type=jnp.float32) + b_group
    g_prob = jax.nn.softmax(g_logits, axis=-1)
    g_idx = jnp.argmax(g_logits, axis=-1)
    g_w = jnp.take_along_axis(g_prob, g_idx[:, None], axis=1)[:, 0]
    e_logits = (jnp.einsum('td,de->te', t, w_router, preferred_element_type=jnp.float32)
                + b_router).reshape(T, N_GROUPS, EXPERTS_PER_GROUP)
    e_logits = jnp.take_along_axis(e_logits, g_idx[:, None, None], axis=1)[:, 0]
    e_prob = jax.nn.softmax(e_logits, axis=-1)
    top_p, top_i = lax.top_k(e_prob, TOP_K)
    top_p = top_p / jnp.sum(top_p, axis=-1, keepdims=True)
    expert_ids = g_idx[:, None] * EXPERTS_PER_GROUP + top_i
    weights = top_p * g_w[:, None]
    combine = jnp.sum(jax.nn.one_hot(expert_ids, N_EXPERTS, dtype=jnp.float32)
                      * weights[..., None], axis=1)

    nb = T // MOE_BLOCK

    def one_block(args):
        tb, cb = args
        hg = jnp.einsum('td,edf->tef', tb, w_gate)
        hu = jnp.einsum('td,edf->tef', tb, w_up)
        act = jax.nn.silu(hg) * hu * cb[..., None].astype(tb.dtype)
        return jnp.einsum('tef,efd->td', act, w_down)

    y = lax.map(one_block, (t.reshape(nb, MOE_BLOCK, D), combine.reshape(nb, MOE_BLOCK, N_EXPERTS)))
    return y.reshape(B, S, D)


def setup_inputs(seed: int = 0) -> dict:
    key = jax.random.key(seed)
    ks = jax.random.split(key, 24)
    f32 = jnp.float32
    nrm = lambda k, shape, s: jax.random.normal(k, shape, f32) * s
    return {
        "x": nrm(ks[0], (BATCH, SEQ, D_MODEL), 1.0),
        "c": nrm(ks[1], (BATCH, D_MODEL), 1.0),
        "positions": jnp.broadcast_to(jnp.arange(SEQ, dtype=jnp.int32), (BATCH, SEQ)),
        "norm1_w": 1.0 + nrm(ks[2], (DEPTH, D_MODEL), 0.02),
        "norm2_w": 1.0 + nrm(ks[3], (DEPTH, D_MODEL), 0.02),
        "final_norm_w": 1.0 + nrm(ks[4], (D_MODEL,), 0.02),
        "ada_w": nrm(ks[5], (DEPTH, D_MODEL, 6 * D_MODEL), 0.5 * D_MODEL ** -0.5),
        "ada_b": nrm(ks[6], (DEPTH, 6 * D_MODEL), 0.02),
        "w_in": nrm(ks[7], (DEPTH, D_MODEL, IN_WIDTH), D_MODEL ** -0.5),
        "w_out": nrm(ks[8], (DEPTH, MIX_WIDTH, D_MODEL), MIX_WIDTH ** -0.5),
        "da_lambda_q1": nrm(ks[9], (DEPTH, DA_QK_DIM), 0.1),
        "da_lambda_k1": nrm(ks[10], (DEPTH, DA_QK_DIM), 0.1),
        "da_lambda_q2": nrm(ks[11], (DEPTH, DA_QK_DIM), 0.1),
        "da_lambda_k2": nrm(ks[12], (DEPTH, DA_QK_DIM), 0.1),
        "da_subln_w": 1.0 + nrm(ks[13], (DEPTH, DA_V_DIM), 0.02),
        "hg_lower_bound": nrm(ks[14], (DEPTH + 1, HG_HEADS * HG_DK), 0.1),
        "hg_norm_w": 1.0 + nrm(ks[15], (DEPTH, HG_DV), 0.02),
        "moe_w_group": nrm(ks[16], (DEPTH, D_MODEL, N_GROUPS), D_MODEL ** -0.5),
        "moe_b_group": nrm(ks[17], (DEPTH, N_GROUPS), 0.01),
        "moe_w_router": nrm(ks[18], (DEPTH, D_MODEL, N_EXPERTS), D_MODEL ** -0.5),
        "moe_b_router": nrm(ks[19], (DEPTH, N_EXPERTS), 0.01),
        "moe_w_gate": nrm(ks[20], (DEPTH, N_EXPERTS, D_MODEL, EXPERT_FF), D_MODEL ** -0.5),
        "moe_w_up": nrm(ks[21], (DEPTH, N_EXPERTS, D_MODEL, EXPERT_FF), D_MODEL ** -0.5),
        "moe_w_down": nrm(ks[22], (DEPTH, N_EXPERTS, EXPERT_FF, D_MODEL), EXPERT_FF ** -0.5),
    }


def reference(x, c, positions, norm1_w, norm2_w, final_norm_w, ada_w, ada_b, w_in, w_out,
              da_lambda_q1, da_lambda_k1, da_lambda_q2, da_lambda_k2, da_subln_w,
              hg_lower_bound, hg_norm_w, moe_w_group, moe_b_group, moe_w_router, moe_b_router,
              moe_w_gate, moe_w_up, moe_w_down):
    B, S, D = x.shape
    lower_bounds = jnp.cumsum(jax.nn.softmax(hg_lower_bound.astype(jnp.float32), axis=0), axis=0)
    c_act = jax.nn.silu(c)
    for l in range(DEPTH):
        mod = jnp.einsum('bd,de->be', c_act, ada_w[l]) + ada_b[l]
        shift1, scale1, gate1, shift2, scale2, gate2 = jnp.split(mod, 6, axis=-1)

        h = rmsnorm(x, norm1_w[l]) * (1.0 + scale1[:, None]) + shift1[:, None]
        proj = jnp.einsum('bsd,de->bse', h, w_in[l])
        dq, dk, dv, gq, gf, gi, gg = jnp.split(proj, IN_SPLITS, axis=-1)
        dq = rope_partial(dq.reshape(B, S, DA_HEADS * 2, DA_QK_DIM), positions)
        dk = rope_partial(dk.reshape(B, S, DA_HEADS * 2, DA_QK_DIM), positions)
        da_out = diff_attention(dq.reshape(B, S, DA_HEADS, 2, DA_QK_DIM),
                                dk.reshape(B, S, DA_HEADS, 2, DA_QK_DIM),
                                dv.reshape(B, S, DA_HEADS, DA_V_DIM),
                                da_lambda_q1[l], da_lambda_k1[l], da_lambda_q2[l], da_lambda_k2[l],
                                da_subln_w[l], l)
        hg_out = hgrn2_chunkwise(gq.reshape(B, S, HG_HEADS, HG_DK),
                                 gf.reshape(B, S, HG_HEADS, HG_DK),
                                 gi.reshape(B, S, HG_HEADS, HG_DV),
                                 gg.reshape(B, S, HG_HEADS, HG_DV),
                                 lower_bounds[l].reshape(HG_HEADS, HG_DK), hg_norm_w[l])
        mixed = jnp.concatenate([da_out.reshape(B, S, DA_WIDTH).astype(x.dtype),
                                 hg_out.reshape(B, S, HG_WIDTH).astype(x.dtype)], axis=-1)
        x = x + gate1[:, None] * jnp.einsum('bse,ed->bsd', mixed, w_out[l])

        h2 = rmsnorm(x, norm2_w[l]) * (1.0 + scale2[:, None]) + shift2[:, None]
        y = hier_moe(h2, moe_w_group[l], moe_b_group[l], moe_w_router[l], moe_b_router[l],
                     moe_w_gate[l], moe_w_up[l], moe_w_down[l])
        x = x + gate2[:, None] * y
    return rmsnorm(x, final_norm_w)
```

```python
import functools
import math

import numpy as np
import jax
import jax.numpy as jnp
from jax import lax
from jax.experimental import pallas as pl
from jax.experimental.pallas import tpu as pltpu

F32 = jnp.float32
BF16 = jnp.bfloat16
HIGHEST = lax.Precision.HIGHEST

LANES = 128
SUBLANES = 8
D_MODEL = 1024
HEADS = 4
HEAD_W = 128
STREAM_W = HEADS * HEAD_W
N_STREAMS = 7
DA_QK_DIM = 64
ROPE_THETA = 500000.0
ROT_DIM = DA_QK_DIM // 4
ROT_HALF = ROT_DIM // 2
N_GROUPS = 4
EXPERTS_PER_GROUP = 8
N_EXPERTS = N_GROUPS * EXPERTS_PER_GROUP
EXPERT_FF = 256
EPS = 1e-6
LAM_INIT = 0.8 - 0.6 * math.exp(-0.3 * 0)
NEG_BIG = -1e30

ROW_TILE = 256
ATTN_TILE = 512
HGRN_BLOCK = 256
MOE_TILE = 256
GATHER_TILE = 256

NT_DIMS = (((1,), (1,)), ((), ()))
TN_DIMS = (((0,), (0,)), ((), ()))


def _sigmoid(x):
    return 1.0 / (1.0 + jnp.exp(-x))


def _adaln_body(c_ref, w_ref, b_ref, o_ref):
    c = c_ref[...]
    ca = c * _sigmoid(c)
    o_ref[...] = jnp.dot(ca, w_ref[...], preferred_element_type=F32, precision=HIGHEST) + b_ref[...]


def _adaln(c, ada_w, ada_b):
    bsz, d = c.shape
    n = ada_w.shape[1]
    tn = 1024
    return pl.pallas_call(
        _adaln_body,
        grid=(n // tn,),
        in_specs=[pl.BlockSpec((bsz, d), lambda j: (0, 0)),
                  pl.BlockSpec((d, tn), lambda j: (0, j)),
                  pl.BlockSpec((1, tn), lambda j: (0, j))],
        out_specs=pl.BlockSpec((bsz, tn), lambda j: (0, j)),
        out_shape=jax.ShapeDtypeStruct((bsz, n), F32),
        name="adaln",
    )(c, ada_w, ada_b)


def _inproj_body(x_ref, mod_ref, nw_ref, w_ref, pos_ref, invf_ref, lbr_ref,
                 q_ref, k_ref, v_ref, hq_ref, lf_ref, kf_ref, gi_ref, sg_ref):
    x = x_ref[...]
    ms = jnp.mean(x * x, axis=-1, keepdims=True)
    y = x * lax.rsqrt(ms + EPS) * nw_ref[...]
    shift = mod_ref[0, 0:1, :]
    scale = mod_ref[0, 1:2, :]
    h = (y * (1.0 + scale) + shift).astype(BF16)

    ang = pos_ref[...] * invf_ref[...]
    cosv = jnp.cos(ang)
    sinv = jnp.sin(ang)
    lane = lax.broadcasted_iota(jnp.int32, (1, LANES), 1) % DA_QK_DIM
    sin_lo = jnp.where(lane < ROT_HALF, -sinv, 0.0)
    sin_hi = jnp.where((lane >= ROT_HALF) & (lane < ROT_DIM), sinv, 0.0)

    def proj(j):
        return jnp.dot(h, w_ref[:, j * STREAM_W:(j + 1) * STREAM_W], preferred_element_type=F32)

    def rope(t):
        outs = []
        for hb in range(HEADS):
            tc = t[:, hb * HEAD_W:(hb + 1) * HEAD_W]
            outs.append(tc * cosv
                        + pltpu.roll(tc, LANES - ROT_HALF, 1) * sin_lo
                        + pltpu.roll(tc, ROT_HALF, 1) * sin_hi)
        return jnp.concatenate(outs, axis=1)

    q_ref[...] = (rope(proj(0)) * (DA_QK_DIM ** -0.5)).astype(BF16)
    k_ref[...] = rope(proj(1)).astype(BF16)
    v_ref[...] = proj(2).astype(BF16)

    gq = proj(3)
    hq_ref[...] = gq * _sigmoid(gq)

    a = lbr_ref[...]
    amax = jnp.max(a, axis=0, keepdims=True)
    ea = jnp.exp(a - amax)
    lb = ea[0:1, :] / jnp.sum(ea, axis=0, keepdims=True)
    gf = proj(4)
    f = lb + (1.0 - lb) * _sigmoid(gf)
    lf_ref[...] = jnp.log(f)
    kf_ref[...] = 1.0 - f

    gi_ref[...] = proj(5).astype(BF16)
    gg = proj(6)
    sg_ref[...] = (gg * _sigmoid(gg)).astype(BF16)


def _inproj(x2, mod3, norm_w, w_bf, pos_col, invf, lb_raw, seq):
    t, d = x2.shape
    tm = ROW_TILE
    per_b = seq // tm
    row = lambda i: (i, 0)
    full = lambda i: (0, 0)
    out_bf = jax.ShapeDtypeStruct((t, STREAM_W), BF16)
    out_f = jax.ShapeDtypeStruct((t, STREAM_W), F32)
    stream = pl.BlockSpec((tm, STREAM_W), row)
    return pl.pallas_call(
        _inproj_body,
        grid=(t // tm,),
        in_specs=[pl.BlockSpec((tm, d), row),
                  pl.BlockSpec((1, 6, d), lambda i: (i // per_b, 0, 0)),
                  pl.BlockSpec((1, d), full),
                  pl.BlockSpec((d, N_STREAMS * STREAM_W), full),
                  pl.BlockSpec((tm, 1), row),
                  pl.BlockSpec((1, LANES), full),
                  pl.BlockSpec(lb_raw.shape, full)],
        out_specs=[stream] * 8,
        out_shape=[out_bf, out_bf, out_bf, out_f, out_f, out_f, out_bf, out_bf],
        compiler_params=pltpu.CompilerParams(vmem_limit_bytes=56 * 1024 * 1024),
        name="inproj",
    )(x2, mod3, norm_w, w_bf, pos_col, invf, lb_raw)


def _attn_body(q_ref, k_ref, v_ref, lam_ref, sw_ref, o_ref, m_scr, l_scr, a_scr, *, tile):
    qi = pl.program_id(2)
    q = q_ref[0]
    lane = lax.broadcasted_iota(jnp.int32, (1, HEAD_W), 1)
    zero = jnp.zeros_like(q)
    qmaps = (jnp.where(lane < DA_QK_DIM, q, zero), jnp.where(lane >= DA_QK_DIM, q, zero))

    m_scr[...] = jnp.full(m_scr.shape, NEG_BIG, F32)
    l_scr[...] = jnp.zeros(l_scr.shape, F32)
    a_scr[...] = jnp.zeros(a_scr.shape, F32)

    def block(ki, masked):
        start = pl.multiple_of(ki * tile, tile)
        kb = k_ref[0, pl.ds(start, tile), :]
        vb = v_ref[0, pl.ds(start, tile), :]
        for mp in range(2):
            s = lax.dot_general(qmaps[mp], kb, NT_DIMS, preferred_element_type=F32)
            if masked:
                r = lax.broadcasted_iota(jnp.int32, s.shape, 0)
                c = lax.broadcasted_iota(jnp.int32, s.shape, 1)
                s = jnp.where(c <= r, s, NEG_BIG)
            m_old = m_scr[mp]
            m_new = jnp.maximum(m_old, jnp.max(s, axis=-1, keepdims=True))
            p = jnp.exp(s - m_new)
            alpha = jnp.exp(m_old - m_new)
            l_scr[mp] = alpha * l_scr[mp] + jnp.sum(p, axis=-1, keepdims=True)
            a_scr[mp] = alpha * a_scr[mp] + jnp.dot(p.astype(BF16), vb, preferred_element_type=F32)
            m_scr[mp] = m_new

    def body(ki, carry):
        block(ki, False)
        return carry

    lax.fori_loop(0, qi, body, 0)
    block(qi, True)

    lp = lam_ref[...]
    lam = (jnp.exp(jnp.sum(lp[0:1] * lp[1:2], axis=-1, keepdims=True))
           - jnp.exp(jnp.sum(lp[2:3] * lp[3:4], axis=-1, keepdims=True)) + LAM_INIT)
    o = a_scr[0] / l_scr[0] - lam * (a_scr[1] / l_scr[1])
    o = o * lax.rsqrt(jnp.mean(o * o, axis=-1, keepdims=True) + EPS) * sw_ref[...]
    o_ref[0] = (o * (1.0 - LAM_INIT)).astype(BF16)


def _attn(q3, k3, v3, lam_p, subln_w):
    bsz, seq, _ = q3.shape
    tile = ATTN_TILE
    return pl.pallas_call(
        functools.partial(_attn_body, tile=tile),
        grid=(bsz, HEADS, seq // tile),
        in_specs=[pl.BlockSpec((1, tile, HEAD_W), lambda b, h, i: (b, i, h)),
                  pl.BlockSpec((1, seq, HEAD_W), lambda b, h, i: (b, 0, h)),
                  pl.BlockSpec((1, seq, HEAD_W), lambda b, h, i: (b, 0, h)),
                  pl.BlockSpec(lam_p.shape, lambda b, h, i: (0, 0)),
                  pl.BlockSpec((1, HEAD_W), lambda b, h, i: (0, 0))],
        out_specs=pl.BlockSpec((1, tile, HEAD_W), lambda b, h, i: (b, i, h)),
        out_shape=jax.ShapeDtypeStruct((bsz, seq, STREAM_W), BF16),
        scratch_shapes=[pltpu.VMEM((2, tile, 1), F32),
                        pltpu.VMEM((2, tile, 1), F32),
                        pltpu.VMEM((2, tile, HEAD_W), F32)],
        name="diff_attn",
    )(q3, k3, v3, lam_p, subln_w)


def _hgrn_levels(block):
    return [block >> (i + 1) for i in range(block.bit_length() - 1)]


def _hgrn_constants(block):
    t = np.arange(block)[:, None]
    s = np.arange(block)[None, :]
    tril = (s <= t).astype(np.float32)
    lv = np.full((block, block), -1, np.int32)
    halves = _hgrn_levels(block)
    for li, m in enumerate(halves):
        same = (t // (2 * m)) == (s // (2 * m))
        lv[same & ((t & m) != 0) & ((s & m) == 0)] = li
    lv[np.arange(block), np.arange(block)] = len(halves)
    return jnp.asarray(tril, BF16), jnp.asarray(lv)


def _level_ref(b_ref, h, m, block):
    def bcast(r, n):
        return jnp.broadcast_to(b_ref[h, pl.ds(r, 1), :], (n, HEAD_W))

    if 2 * m >= SUBLANES:
        return jnp.concatenate([bcast(s0 + m - 1, 2 * m) for s0 in range(0, block, 2 * m)], axis=0)
    sub = lax.broadcasted_iota(jnp.int32, (SUBLANES, HEAD_W), 0)
    pieces = []
    for s0 in range(0, block, SUBLANES):
        piece = bcast(s0 + m - 1, SUBLANES)
        for j in range(1, SUBLANES // (2 * m)):
            piece = jnp.where(sub >= 2 * m * j, bcast(s0 + 2 * m * j + m - 1, SUBLANES), piece)
        pieces.append(piece)
    return jnp.concatenate(pieces, axis=0)


def _hgrn_body(hq_ref, lf_ref, kf_ref, gi_ref, sg_ref, nw_ref, tril_ref, lv_ref, o_ref,
               st_scr, b_scr, *, block):
    @pl.when(pl.program_id(1) == 0)
    def _():
        st_scr[...] = jnp.zeros(st_scr.shape, F32)

    tril = tril_ref[...]
    lv = lv_ref[...]
    row = lax.broadcasted_iota(jnp.int32, (block, 1), 0)
    halves = _hgrn_levels(block)

    for h in range(HEADS):
        cs = slice(h * HEAD_W, (h + 1) * HEAD_W)
        q = hq_ref[0, :, cs]
        lf = lf_ref[0, :, cs]
        k = kf_ref[0, :, cs]
        v = gi_ref[0, :, cs]

        hi = lf.astype(BF16)
        r1 = lf - hi.astype(F32)
        mid = r1.astype(BF16)
        lo = (r1 - mid.astype(F32)).astype(BF16)
        b = (jnp.dot(tril, hi, preferred_element_type=F32)
             + jnp.dot(tril, mid, preferred_element_type=F32)
             + jnp.dot(tril, lo, preferred_element_type=F32))
        b_scr[h] = b

        qb = q.astype(BF16)
        kb = k.astype(BF16)
        scores = jnp.where(lv == len(halves),
                           lax.dot_general(qb, kb, NT_DIMS, preferred_element_type=F32), 0.0)
        for li, m in enumerate(halves):
            ref = _level_ref(b_scr, h, m, block)
            e = jnp.exp(-jnp.abs(b - ref))
            xl = (jnp.where((row & m) != 0, q, k) * e).astype(BF16)
            p = lax.dot_general(xl, xl, NT_DIMS, preferred_element_type=F32)
            scores = jnp.where(lv == li, p, scores)
        o_intra = jnp.dot(scores.astype(BF16), v, preferred_element_type=F32)

        st = st_scr[h]
        o_inter = lax.dot_general((q * jnp.exp(b)).astype(BF16), st.astype(BF16), NT_DIMS,
                                  preferred_element_type=F32)
        b_last = b[block - 1:block, :]
        kdec = (k * jnp.exp(b_last - b)).astype(BF16)
        st_scr[h] = st * jnp.exp(b_last) + lax.dot_general(v, kdec, TN_DIMS,
                                                            preferred_element_type=F32)

        o = o_inter + o_intra
        o = o * lax.rsqrt(jnp.mean(o * o, axis=-1, keepdims=True) + EPS) * nw_ref[...]
        o_ref[0, :, cs] = (o * sg_ref[0, :, cs].astype(F32)).astype(BF16)


def _hgrn(hq3, lf3, kf3, gi3, sg3, norm_w):
    bsz, seq, _ = hq3.shape
    block = HGRN_BLOCK
    tril, lv = _hgrn_constants(block)
    blk = pl.BlockSpec((1, block, STREAM_W), lambda b, g: (b, g, 0))
    const = lambda b, g: (0, 0)
    return pl.pallas_call(
        functools.partial(_hgrn_body, block=block),
        grid=(bsz, seq // block),
        in_specs=[blk, blk, blk, blk, blk,
                  pl.BlockSpec((1, HEAD_W), const),
                  pl.BlockSpec((block, block), const),
                  pl.BlockSpec((block, block), const)],
        out_specs=blk,
        out_shape=jax.ShapeDtypeStruct((bsz, seq, STREAM_W), BF16),
        scratch_shapes=[pltpu.VMEM((HEADS, HEAD_W, HEAD_W), F32),
                        pltpu.VMEM((HEADS, block, HEAD_W), F32)],
        name="hgrn2",
    )(hq3, lf3, kf3, gi3, sg3, norm_w, tril, lv)


META_E0, META_E1, META_R0, META_R1, META_W0, META_W1 = range(6)
GROUP_LANE0 = N_EXPERTS


def _outproj_body(da_ref, hg_ref, x_ref, mod_ref, wo_ref, nw_ref, wr_ref, br_ref, stril_ref,
                  x1_ref, h2_ref, meta_ref, cnt_ref, carry_scr):
    @pl.when(pl.program_id(0) == 0)
    def _():
        carry_scr[...] = jnp.zeros(carry_scr.shape, F32)

    attn = (jnp.dot(da_ref[...], wo_ref[0:STREAM_W, :], preferred_element_type=F32)
            + jnp.dot(hg_ref[...], wo_ref[STREAM_W:, :], preferred_element_type=F32))
    gate1 = mod_ref[0, 2:3, :]
    shift2 = mod_ref[0, 3:4, :]
    scale2 = mod_ref[0, 4:5, :]
    x1 = x_ref[...] + gate1 * attn
    x1_ref[...] = x1
    h2 = (x1 * lax.rsqrt(jnp.mean(x1 * x1, axis=-1, keepdims=True) + EPS) * nw_ref[...]
          * (1.0 + scale2) + shift2)
    h2_ref[...] = h2

    logits = jnp.dot(h2, wr_ref[...], preferred_element_type=F32, precision=HIGHEST) + br_ref[...]
    lane = lax.broadcasted_iota(jnp.int32, logits.shape, 1)
    far = jnp.int32(LANES)

    def first_max(vals):
        mx = jnp.max(vals, axis=-1, keepdims=True)
        return mx, jnp.min(jnp.where(vals == mx, lane, far), axis=-1, keepdims=True)

    is_g = (lane >= GROUP_LANE0) & (lane < GROUP_LANE0 + N_GROUPS)
    gmax, glane = first_max(jnp.where(is_g, logits, NEG_BIG))
    g_w = 1.0 / jnp.sum(jnp.where(is_g, jnp.exp(logits - gmax), 0.0), axis=-1, keepdims=True)
    gidx = glane - GROUP_LANE0
    in_grp = (lane < N_EXPERTS) & ((lane // EXPERTS_PER_GROUP) == gidx)
    el = jnp.where(in_grp, logits, NEG_BIG)
    m1, i1 = first_max(el)
    m2, i2 = first_max(jnp.where(lane == i1, NEG_BIG, el))
    r = jnp.exp(m2 - m1)
    w0 = g_w / (1.0 + r)
    w1 = g_w * r / (1.0 + r)

    hot0 = lane == i1
    hot1 = lane == i2
    multi = jnp.where(hot0 | hot1, 1.0, 0.0)
    before = jnp.dot(stril_ref[...], multi.astype(BF16), preferred_element_type=F32) + carry_scr[...]
    rank0 = jnp.sum(jnp.where(hot0, before, 0.0), axis=-1, keepdims=True)
    rank1 = jnp.sum(jnp.where(hot1, before, 0.0), axis=-1, keepdims=True)
    carry = carry_scr[...] + jnp.sum(multi, axis=0, keepdims=True)
    carry_scr[...] = carry
    cnt_ref[...] = carry

    meta = jnp.zeros(logits.shape, F32)
    for idx, val in ((META_E0, i1.astype(F32)), (META_E1, i2.astype(F32)),
                     (META_R0, rank0), (META_R1, rank1), (META_W0, w0), (META_W1, w1)):
        meta = jnp.where(lane == idx, val, meta)
    meta_ref[...] = meta


def _outproj(da2, hg2, x2, mod3, wo_bf, norm_w, w_route, b_route, seq):
    t, d = x2.shape
    tm = ROW_TILE
    per_b = seq // tm
    row = lambda i: (i, 0)
    full = lambda i: (0, 0)
    stril = jnp.asarray(np.tril(np.ones((tm, tm), np.float32), -1), BF16)
    return pl.pallas_call(
        _outproj_body,
        grid=(t // tm,),
        in_specs=[pl.BlockSpec((tm, STREAM_W), row),
                  pl.BlockSpec((tm, STREAM_W), row),
                  pl.BlockSpec((tm, d), row),
                  pl.BlockSpec((1, 6, d), lambda i: (i // per_b, 0, 0)),
                  pl.BlockSpec((2 * STREAM_W, d), full),
                  pl.BlockSpec((1, d), full),
                  pl.BlockSpec((d, LANES), full),
                  pl.BlockSpec((1, LANES), full),
                  pl.BlockSpec((tm, tm), full)],
        out_specs=[pl.BlockSpec((tm, d), row),
                   pl.BlockSpec((tm, d), row),
                   pl.BlockSpec((tm, LANES), row),
                   pl.BlockSpec((1, LANES), full)],
        out_shape=[jax.ShapeDtypeStruct((t, d), F32),
                   jax.ShapeDtypeStruct((t, d), F32),
                   jax.ShapeDtypeStruct((t, LANES), F32),
                   jax.ShapeDtypeStruct((1, LANES), F32)],
        scratch_shapes=[pltpu.VMEM((1, LANES), F32)],
        compiler_params=pltpu.CompilerParams(dimension_semantics=("arbitrary",)),
        name="outproj_route",
    )(da2, hg2, x2, mod3, wo_bf, norm_w, w_route, b_route, stril)


def _scatter_body(pos_ref, h2_hbm, xs_in, xs_hbm, sem, *, tile):
    del xs_in
    base = pl.program_id(0) * tile

    def row_copy(t, j):
        return pltpu.make_async_copy(h2_hbm.at[pl.ds(base + t, 1)],
                                     xs_hbm.at[pl.ds(pos_ref[0, 0, 2 * t + j], 1)], sem)

    def issue(t, carry):
        row_copy(t, 0).start()
        row_copy(t, 1).start()
        return carry

    def drain(t, carry):
        row_copy(t, 0).wait()
        row_copy(t, 1).wait()
        return carry

    lax.fori_loop(0, tile, issue, 0)
    lax.fori_loop(0, tile, drain, 0)


def _scatter(pos3, h2, xs_zero):
    t, d = h2.shape
    tile = GATHER_TILE
    return pl.pallas_call(
        functools.partial(_scatter_body, tile=tile),
        grid=(t // tile,),
        in_specs=[pl.BlockSpec((1, 1, 2 * tile), lambda i: (i, 0, 0), memory_space=pltpu.SMEM),
                  pl.BlockSpec(memory_space=pl.ANY),
                  pl.BlockSpec(memory_space=pl.ANY)],
        out_specs=pl.BlockSpec(memory_space=pl.ANY),
        out_shape=jax.ShapeDtypeStruct(xs_zero.shape, F32),
        scratch_shapes=[pltpu.SemaphoreType.DMA],
        input_output_aliases={2: 0},
        compiler_params=pltpu.CompilerParams(dimension_semantics=("arbitrary",)),
        name="moe_scatter",
    )(pos3, h2, xs_zero)


def _experts_body(te_ref, nv_ref, x_ref, wg_ref, wu_ref, wd_ref, y_ref, wgu_scr, wd_scr):
    i = pl.program_id(0)
    e = te_ref[i]
    prev = te_ref[jnp.maximum(i - 1, 0)]

    @pl.when((i == 0) | (e != prev))
    def _():
        wgu_scr[:, 0:EXPERT_FF] = wg_ref[0].astype(BF16)
        wgu_scr[:, EXPERT_FF:] = wu_ref[0].astype(BF16)
        wd_scr[...] = wd_ref[0].astype(BF16)

    @pl.when(i < nv_ref[0])
    def _():
        gu = jnp.dot(x_ref[...].astype(BF16), wgu_scr[...], preferred_element_type=F32)
        g = gu[:, 0:EXPERT_FF]
        act = (g * _sigmoid(g) * gu[:, EXPERT_FF:]).astype(BF16)
        y_ref[...] = jnp.dot(act, wd_scr[...], preferred_element_type=F32)

    @pl.when(i >= nv_ref[0])
    def _():
        y_ref[...] = jnp.zeros(y_ref.shape, F32)


def _experts(tile_expert, n_valid, xs, w_gate, w_up, w_down):
    n_slots, d = xs.shape
    tm = MOE_TILE
    live = lambda i, te, nv: (jnp.minimum(i, nv[0] - 1), 0)
    grid_spec = pltpu.PrefetchScalarGridSpec(
        num_scalar_prefetch=2,
        grid=(n_slots // tm,),
        in_specs=[pl.BlockSpec((tm, d), live),
                  pl.BlockSpec((1, d, EXPERT_FF), lambda i, te, nv: (te[i], 0, 0)),
                  pl.BlockSpec((1, d, EXPERT_FF), lambda i, te, nv: (te[i], 0, 0)),
                  pl.BlockSpec((1, EXPERT_FF, d), lambda i, te, nv: (te[i], 0, 0))],
        out_specs=pl.BlockSpec((tm, d), lambda i, te, nv: (i, 0)),
        scratch_shapes=[pltpu.VMEM((d, 2 * EXPERT_FF), BF16),
                        pltpu.VMEM((EXPERT_FF, d), BF16)],
    )
    return pl.pallas_call(
        _experts_body,
        grid_spec=grid_spec,
        out_shape=jax.ShapeDtypeStruct((n_slots, d), F32),
        compiler_params=pltpu.CompilerParams(dimension_semantics=("arbitrary",)),
        name="moe_experts",
    )(tile_expert, n_valid, xs, w_gate, w_up, w_down)


def _combine_body(pos_ref, x1_ref, meta_ref, mod_ref, nw_ref, ys_hbm, o_ref, rows_scr, sem, *, tile):
    def row_copy(t, j):
        return pltpu.make_async_copy(ys_hbm.at[pl.ds(pos_ref[0, 0, 2 * t + j], 1)],
                                     rows_scr.at[j, pl.ds(t, 1)], sem)

    def issue(t, carry):
        row_copy(t, 0).start()
        row_copy(t, 1).start()
        return carry

    def drain(t, carry):
        row_copy(t, 0).wait()
        row_copy(t, 1).wait()
        return carry

    lax.fori_loop(0, tile, issue, 0)
    lax.fori_loop(0, tile, drain, 0)

    meta = meta_ref[...]
    w0 = meta[:, META_W0:META_W0 + 1]
    w1 = meta[:, META_W1:META_W1 + 1]
    y = w0 * rows_scr[0] + w1 * rows_scr[1]
    x2 = x1_ref[...] + mod_ref[0, 5:6, :] * y
    o_ref[...] = x2 * lax.rsqrt(jnp.mean(x2 * x2, axis=-1, keepdims=True) + EPS) * nw_ref[...]


def _combine(pos3, x1, meta, mod3, norm_w, ys, seq):
    t, d = x1.shape
    tile = GATHER_TILE
    per_b = seq // tile
    row = lambda i: (i, 0)
    return pl.pallas_call(
        functools.partial(_combine_body, tile=tile),
        grid=(t // tile,),
        in_specs=[pl.BlockSpec((1, 1, 2 * tile), lambda i: (i, 0, 0), memory_space=pltpu.SMEM),
                  pl.BlockSpec((tile, d), row),
                  pl.BlockSpec((tile, LANES), row),
                  pl.BlockSpec((1, 6, d), lambda i: (i // per_b, 0, 0)),
                  pl.BlockSpec((1, d), lambda i: (0, 0)),
                  pl.BlockSpec(memory_space=pl.ANY)],
        out_specs=pl.BlockSpec((tile, d), row),
        out_shape=jax.ShapeDtypeStruct((t, d), F32),
        scratch_shapes=[pltpu.VMEM((2, tile, d), F32), pltpu.SemaphoreType.DMA],
        compiler_params=pltpu.CompilerParams(dimension_semantics=("arbitrary",)),
        name="moe_combine",
    )(pos3, x1, meta, mod3, norm_w, ys)


def _rope_inv_freq_lanes():
    inv_freq = ROPE_THETA ** (-jnp.arange(ROT_HALF, dtype=F32) / ROT_HALF)
    lane = np.arange(LANES) % DA_QK_DIM
    table = jnp.where(jnp.asarray(lane < ROT_DIM), inv_freq[lane % ROT_HALF], 0.0)
    return table.reshape(1, LANES).astype(F32)


def kernel(x, c, positions, norm1_w, norm2_w, final_norm_w, ada_w, ada_b, w_in, w_out, da_lambda_q1, da_lambda_k1, da_lambda_q2, da_lambda_k2, da_subln_w, hg_lower_bound, hg_norm_w, moe_w_group, moe_b_group, moe_w_router, moe_b_router, moe_w_gate, moe_w_up, moe_w_down):
    bsz, seq, d = x.shape
    assert d == D_MODEL and norm1_w.shape[0] == 1, "single-layer model of width 1024 only"
    assert seq % ATTN_TILE == 0 and seq % HGRN_BLOCK == 0 and seq % ROW_TILE == 0
    t = bsz * seq
    x2 = x.reshape(t, d)

    mod3 = _adaln(c, ada_w[0], ada_b).reshape(bsz, 6, d)

    pos_col = positions.astype(F32).reshape(t, 1)
    q, k, v, hq, lf, kf, gi, sg = _inproj(x2, mod3, norm1_w, w_in[0].astype(BF16), pos_col,
                                          _rope_inv_freq_lanes(), hg_lower_bound, seq)

    as3 = lambda a: a.reshape(bsz, seq, STREAM_W)
    lam_p = jnp.concatenate([da_lambda_q1, da_lambda_k1, da_lambda_q2, da_lambda_k2], axis=0)
    da = _attn(as3(q), as3(k), as3(v), lam_p, da_subln_w)
    hg = _hgrn(as3(hq), as3(lf), as3(kf), as3(gi), as3(sg), hg_norm_w)

    pad = jnp.zeros((d, LANES - N_EXPERTS - N_GROUPS), F32)
    w_route = jnp.concatenate([moe_w_router[0], moe_w_group[0], pad], axis=1)
    b_route = jnp.concatenate([moe_b_router[0], moe_b_group[0], pad[0]]).reshape(1, LANES)
    x1, h2, meta, cnt = _outproj(da.reshape(t, STREAM_W), hg.reshape(t, STREAM_W), x2, mod3,
                                 w_out[0].astype(BF16), norm2_w, w_route, b_route, seq)

    counts = cnt[0, :N_EXPERTS].astype(jnp.int32)
    tiles_e = (counts + MOE_TILE - 1) // MOE_TILE
    tile_end = jnp.cumsum(tiles_e)
    offs = (tile_end - tiles_e) * MOE_TILE
    ids = meta[:, META_E0:META_E1 + 1].astype(jnp.int32)
    ranks = meta[:, META_R0:META_R1 + 1].astype(jnp.int32)
    pos = offs[ids] + ranks
    n_tiles = (2 * t) // MOE_TILE + N_EXPERTS
    n_valid = tile_end[-1:]
    tile_ids = jnp.minimum(jnp.arange(n_tiles, dtype=jnp.int32), n_valid - 1)
    tile_expert = jnp.sum(tile_ids[:, None] >= tile_end[None, :], axis=1).astype(jnp.int32)
    pos3 = pos.reshape(t // GATHER_TILE, 1, 2 * GATHER_TILE)

    xs = _scatter(pos3, h2, jnp.zeros((n_tiles * MOE_TILE, d), F32))
    ys = _experts(tile_expert, n_valid.astype(jnp.int32), xs, moe_w_gate[0], moe_w_up[0], moe_w_down[0])
    out = _combine(pos3, x1, meta, mod3, final_norm_w.reshape(1, d), ys, seq)
    return out.reshape(bsz, seq, d)
```

```python
import functools
import math

import numpy as np
import jax
import jax.numpy as jnp
from jax import lax
from jax.experimental import pallas as pl
from jax.experimental.pallas import tpu as pltpu

F32 = jnp.float32
BF16 = jnp.bfloat16
HIGHEST = lax.Precision.HIGHEST

LANES = 128
SUBLANES = 8
D_MODEL = 1024
HEADS = 4
HEAD_W = 128
STREAM_W = HEADS * HEAD_W
N_STREAMS = 7
DA_QK_DIM = 64
ROPE_THETA = 500000.0
ROT_DIM = DA_QK_DIM // 4
ROT_HALF = ROT_DIM // 2
N_GROUPS = 4
EXPERTS_PER_GROUP = 8
N_EXPERTS = N_GROUPS * EXPERTS_PER_GROUP
EXPERT_FF = 256
EPS = 1e-6
LAM_INIT = 0.8 - 0.6 * math.exp(-0.3 * 0)
NEG_BIG = -1e30

ROW_TILE = 256
ATTN_TILE = 512
HGRN_BLOCK = 256
MOE_TILE = 256
GATHER_TILE = 256

NT_DIMS = (((1,), (1,)), ((), ()))
TN_DIMS = (((0,), (0,)), ((), ()))


def _sigmoid(x):
    return 1.0 / (1.0 + jnp.exp(-x))


def _adaln_body(c_ref, w_ref, b_ref, o_ref):
    c = c_ref[...]
    ca = c * _sigmoid(c)
    o_ref[...] = jnp.dot(ca, w_ref[...], preferred_element_type=F32, precision=HIGHEST) + b_ref[...]


def _adaln(c, ada_w, ada_b):
    bsz, d = c.shape
    n = ada_w.shape[1]
    tn = 1024
    return pl.pallas_call(
        _adaln_body,
        grid=(n // tn,),
        in_specs=[pl.BlockSpec((bsz, d), lambda j: (0, 0)),
                  pl.BlockSpec((d, tn), lambda j: (0, j)),
                  pl.BlockSpec((1, tn), lambda j: (0, j))],
        out_specs=pl.BlockSpec((bsz, tn), lambda j: (0, j)),
        out_shape=jax.ShapeDtypeStruct((bsz, n), F32),
        name="adaln",
    )(c, ada_w, ada_b)


def _inproj_body(x_ref, mod_ref, nw_ref, w_ref, pos_ref, invf_ref, lbr_ref,
                 q_ref, k_ref, v_ref, hq_ref, lf_ref, kf_ref, gi_ref, sg_ref):
    x = x_ref[...]
    ms = jnp.mean(x * x, axis=-1, keepdims=True)
    y = x * lax.rsqrt(ms + EPS) * nw_ref[...]
    shift = mod_ref[0, 0:1, :]
    scale = mod_ref[0, 1:2, :]
    h = (y * (1.0 + scale) + shift).astype(BF16)

    ang = pos_ref[...] * invf_ref[...]
    cosv = jnp.cos(ang)
    sinv = jnp.sin(ang)
    lane = lax.broadcasted_iota(jnp.int32, (1, LANES), 1) % DA_QK_DIM
    sin_lo = jnp.where(lane < ROT_HALF, -sinv, 0.0)
    sin_hi = jnp.where((lane >= ROT_HALF) & (lane < ROT_DIM), sinv, 0.0)

    def proj(j):
        return jnp.dot(h, w_ref[:, j * STREAM_W:(j + 1) * STREAM_W], preferred_element_type=F32)

    def rope(t):
        outs = []
        for hb in range(HEADS):
            tc = t[:, hb * HEAD_W:(hb + 1) * HEAD_W]
            outs.append(tc * cosv
                        + pltpu.roll(tc, LANES - ROT_HALF, 1) * sin_lo
                        + pltpu.roll(tc, ROT_HALF, 1) * sin_hi)
        return jnp.concatenate(outs, axis=1)

    q_ref[0] = (rope(proj(0)) * (DA_QK_DIM ** -0.5)).T.astype(BF16)
    k_ref[...] = rope(proj(1)).astype(BF16)
    v_ref[0] = proj(2).astype(BF16).T

    gq = proj(3)
    hq_ref[...] = gq * _sigmoid(gq)

    a = lbr_ref[...]
    amax = jnp.max(a, axis=0, keepdims=True)
    ea = jnp.exp(a - amax)
    lb = ea[0:1, :] / jnp.sum(ea, axis=0, keepdims=True)
    gf = proj(4)
    f = lb + (1.0 - lb) * _sigmoid(gf)
    lf_ref[...] = jnp.log(f)
    kf_ref[...] = 1.0 - f

    gi_ref[...] = proj(5).astype(BF16)
    gg = proj(6)
    sg_ref[...] = (gg * _sigmoid(gg)).astype(BF16)


def _inproj(x2, mod3, norm_w, w_bf, pos_col, invf, lb_raw, seq):
    t, d = x2.shape
    tm = ROW_TILE
    per_b = seq // tm
    row = lambda i: (i, 0)
    full = lambda i: (0, 0)
    out_bf = jax.ShapeDtypeStruct((t, STREAM_W), BF16)
    out_f = jax.ShapeDtypeStruct((t, STREAM_W), F32)
    out_t = jax.ShapeDtypeStruct((t // seq, STREAM_W, seq), BF16)
    stream = pl.BlockSpec((tm, STREAM_W), row)
    stream_t = pl.BlockSpec((1, STREAM_W, tm), lambda i: (i // per_b, 0, i % per_b))
    return pl.pallas_call(
        _inproj_body,
        grid=(t // tm,),
        in_specs=[pl.BlockSpec((tm, d), row),
                  pl.BlockSpec((1, 6, d), lambda i: (i // per_b, 0, 0)),
                  pl.BlockSpec((1, d), full),
                  pl.BlockSpec((d, N_STREAMS * STREAM_W), full),
                  pl.BlockSpec((tm, 1), row),
                  pl.BlockSpec((1, LANES), full),
                  pl.BlockSpec(lb_raw.shape, full)],
        out_specs=[stream_t, stream, stream_t] + [stream] * 5,
        out_shape=[out_t, out_bf, out_t, out_f, out_f, out_f, out_bf, out_bf],
        compiler_params=pltpu.CompilerParams(vmem_limit_bytes=56 * 1024 * 1024),
        name="inproj",
    )(x2, mod3, norm_w, w_bf, pos_col, invf, lb_raw)


ONES_ROWS = 16


def _attn_body(qt_ref, k_ref, vt_ref, lam_ref, sw_ref, o_ref, m_scr, a_scr, *, tile):
    qi = pl.program_id(2)
    qt = qt_ref[0]
    feat = lax.broadcasted_iota(jnp.int32, (HEAD_W, 1), 0)
    zero = jnp.zeros_like(qt)
    qmaps = (jnp.where(feat < DA_QK_DIM, qt, zero), jnp.where(feat >= DA_QK_DIM, qt, zero))
    ones = jnp.ones((ONES_ROWS, tile), BF16)

    m_scr[...] = jnp.full(m_scr.shape, NEG_BIG, F32)
    a_scr[...] = jnp.zeros(a_scr.shape, F32)

    def block(ki, masked):
        start = pl.multiple_of(ki * tile, tile)
        kb = k_ref[0, pl.ds(start, tile), :]
        vb = jnp.concatenate([vt_ref[0, :, pl.ds(start, tile)], ones], axis=0)
        for mp in range(2):
            s = jnp.dot(kb, qmaps[mp], preferred_element_type=F32)
            if masked:
                key = lax.broadcasted_iota(jnp.int32, s.shape, 0)
                qry = lax.broadcasted_iota(jnp.int32, s.shape, 1)
                s = jnp.where(key <= qry, s, NEG_BIG)
            m_old = m_scr[mp]
            m_new = jnp.maximum(m_old, jnp.max(s, axis=0, keepdims=True))
            p = jnp.exp(s - m_new).astype(BF16)
            alpha = jnp.exp(m_old - m_new)
            a_scr[mp] = alpha * a_scr[mp] + jnp.dot(vb, p, preferred_element_type=F32)
            m_scr[mp] = m_new

    def body(ki, carry):
        block(ki, False)
        return carry

    lax.fori_loop(0, qi, body, 0)
    block(qi, True)

    lp = lam_ref[...]
    lam = (jnp.exp(jnp.sum(lp[0:1] * lp[1:2], axis=-1, keepdims=True))
           - jnp.exp(jnp.sum(lp[2:3] * lp[3:4], axis=-1, keepdims=True)) + LAM_INIT)
    o = (a_scr[0, 0:HEAD_W, :] / a_scr[0, HEAD_W:HEAD_W + 1, :]
         - lam * (a_scr[1, 0:HEAD_W, :] / a_scr[1, HEAD_W:HEAD_W + 1, :]))
    o = o * lax.rsqrt(jnp.mean(o * o, axis=0, keepdims=True) + EPS) * sw_ref[...]
    o_ref[0] = (o * (1.0 - LAM_INIT)).T.astype(BF16)


def _attn(qt3, k3, vt3, lam_p, subln_col):
    bsz, seq, _ = k3.shape
    tile = ATTN_TILE
    return pl.pallas_call(
        functools.partial(_attn_body, tile=tile),
        grid=(bsz, HEADS, seq // tile),
        in_specs=[pl.BlockSpec((1, HEAD_W, tile), lambda b, h, i: (b, h, i)),
                  pl.BlockSpec((1, seq, HEAD_W), lambda b, h, i: (b, 0, h)),
                  pl.BlockSpec((1, HEAD_W, seq), lambda b, h, i: (b, h, 0)),
                  pl.BlockSpec(lam_p.shape, lambda b, h, i: (0, 0)),
                  pl.BlockSpec((HEAD_W, 1), lambda b, h, i: (0, 0))],
        out_specs=pl.BlockSpec((1, tile, HEAD_W), lambda b, h, i: (b, i, h)),
        out_shape=jax.ShapeDtypeStruct((bsz, seq, STREAM_W), BF16),
        scratch_shapes=[pltpu.VMEM((2, 1, tile), F32),
                        pltpu.VMEM((2, HEAD_W + ONES_ROWS, tile), F32)],
        name="diff_attn",
    )(qt3, k3, vt3, lam_p, subln_col)


def _hgrn_levels(block):
    return [block >> (i + 1) for i in range(block.bit_length() - 1)]


def _hgrn_constants(block):
    t = np.arange(block)[:, None]
    s = np.arange(block)[None, :]
    tril = (s <= t).astype(np.float32)
    lv = np.full((block, block), -1, np.int32)
    halves = _hgrn_levels(block)
    for li, m in enumerate(halves):
        same = (t // (2 * m)) == (s // (2 * m))
        lv[same & ((t & m) != 0) & ((s & m) == 0)] = li
    lv[np.arange(block), np.arange(block)] = len(halves)
    return jnp.asarray(tril, BF16), jnp.asarray(lv)


def _level_ref(b_ref, h, m, block):
    def bcast(r, n):
        return jnp.broadcast_to(b_ref[h, pl.ds(r, 1), :], (n, HEAD_W))

    if 2 * m >= SUBLANES:
        return jnp.concatenate([bcast(s0 + m - 1, 2 * m) for s0 in range(0, block, 2 * m)], axis=0)
    sub = lax.broadcasted_iota(jnp.int32, (SUBLANES, HEAD_W), 0)
    pieces = []
    for s0 in range(0, block, SUBLANES):
        piece = bcast(s0 + m - 1, SUBLANES)
        for j in range(1, SUBLANES // (2 * m)):
            piece = jnp.where(sub >= 2 * m * j, bcast(s0 + 2 * m * j + m - 1, SUBLANES), piece)
        pieces.append(piece)
    return jnp.concatenate(pieces, axis=0)


def _hgrn_body(hq_ref, lf_ref, kf_ref, gi_ref, sg_ref, nw_ref, tril_ref, lv_ref, o_ref,
               st_scr, b_scr, *, block):
    @pl.when(pl.program_id(1) == 0)
    def _():
        st_scr[...] = jnp.zeros(st_scr.shape, F32)

    tril = tril_ref[...]
    lv = lv_ref[...]
    row = lax.broadcasted_iota(jnp.int32, (block, 1), 0)
    halves = _hgrn_levels(block)

    for h in range(HEADS):
        cs = slice(h * HEAD_W, (h + 1) * HEAD_W)
        q = hq_ref[0, :, cs]
        lf = lf_ref[0, :, cs]
        k = kf_ref[0, :, cs]
        v = gi_ref[0, :, cs]

        hi = lf.astype(BF16)
        r1 = lf - hi.astype(F32)
        mid = r1.astype(BF16)
        lo = (r1 - mid.astype(F32)).astype(BF16)
        b = (jnp.dot(tril, hi, preferred_element_type=F32)
             + jnp.dot(tril, mid, preferred_element_type=F32)
             + jnp.dot(tril, lo, preferred_element_type=F32))
        b_scr[h] = b

        qb = q.astype(BF16)
        kb = k.astype(BF16)
        scores = jnp.where(lv == len(halves),
                           lax.dot_general(qb, kb, NT_DIMS, preferred_element_type=F32), 0.0)
        for li, m in enumerate(halves):
            ref = _level_ref(b_scr, h, m, block)
            e = jnp.exp(-jnp.abs(b - ref))
            xl = (jnp.where((row & m) != 0, q, k) * e).astype(BF16)
            p = lax.dot_general(xl, xl, NT_DIMS, preferred_element_type=F32)
            scores = jnp.where(lv == li, p, scores)
        o_intra = jnp.dot(scores.astype(BF16), v, preferred_element_type=F32)

        st = st_scr[h]
        o_inter = lax.dot_general((q * jnp.exp(b)).astype(BF16), st.astype(BF16), NT_DIMS,
                                  preferred_element_type=F32)
        b_last = b[block - 1:block, :]
        kdec = (k * jnp.exp(b_last - b)).astype(BF16)
        st_scr[h] = st * jnp.exp(b_last) + lax.dot_general(v, kdec, TN_DIMS,
                                                            preferred_element_type=F32)

        o = o_inter + o_intra
        o = o * lax.rsqrt(jnp.mean(o * o, axis=-1, keepdims=True) + EPS) * nw_ref[...]
        o_ref[0, :, cs] = (o * sg_ref[0, :, cs].astype(F32)).astype(BF16)


def _hgrn(hq3, lf3, kf3, gi3, sg3, norm_w):
    bsz, seq, _ = hq3.shape
    block = HGRN_BLOCK
    tril, lv = _hgrn_constants(block)
    blk = pl.BlockSpec((1, block, STREAM_W), lambda b, g: (b, g, 0))
    const = lambda b, g: (0, 0)
    return pl.pallas_call(
        functools.partial(_hgrn_body, block=block),
        grid=(bsz, seq // block),
        in_specs=[blk, blk, blk, blk, blk,
                  pl.BlockSpec((1, HEAD_W), const),
                  pl.BlockSpec((block, block), const),
                  pl.BlockSpec((block, block), const)],
        out_specs=blk,
        out_shape=jax.ShapeDtypeStruct((bsz, seq, STREAM_W), BF16),
        scratch_shapes=[pltpu.VMEM((HEADS, HEAD_W, HEAD_W), F32),
                        pltpu.VMEM((HEADS, block, HEAD_W), F32)],
        name="hgrn2",
    )(hq3, lf3, kf3, gi3, sg3, norm_w, tril, lv)


META_E0, META_E1, META_R0, META_R1, META_W0, META_W1 = range(6)
GROUP_LANE0 = N_EXPERTS


def _outproj_body(da_ref, hg_ref, x_ref, mod_ref, wo_ref, nw_ref, wr_ref, br_ref, stril_ref,
                  x1_ref, h2_ref, meta_ref, cnt_ref, carry_scr):
    @pl.when(pl.program_id(0) == 0)
    def _():
        carry_scr[...] = jnp.zeros(carry_scr.shape, F32)

    attn = (jnp.dot(da_ref[...], wo_ref[0:STREAM_W, :], preferred_element_type=F32)
            + jnp.dot(hg_ref[...], wo_ref[STREAM_W:, :], preferred_element_type=F32))
    gate1 = mod_ref[0, 2:3, :]
    shift2 = mod_ref[0, 3:4, :]
    scale2 = mod_ref[0, 4:5, :]
    x1 = x_ref[...] + gate1 * attn
    x1_ref[...] = x1
    h2 = (x1 * lax.rsqrt(jnp.mean(x1 * x1, axis=-1, keepdims=True) + EPS) * nw_ref[...]
          * (1.0 + scale2) + shift2)
    h2_ref[...] = h2

    logits = jnp.dot(h2, wr_ref[...], preferred_element_type=F32, precision=HIGHEST) + br_ref[...]
    lane = lax.broadcasted_iota(jnp.int32, logits.shape, 1)
    far = jnp.int32(LANES)

    def first_max(vals):
        mx = jnp.max(vals, axis=-1, keepdims=True)
        return mx, jnp.min(jnp.where(vals == mx, lane, far), axis=-1, keepdims=True)

    is_g = (lane >= GROUP_LANE0) & (lane < GROUP_LANE0 + N_GROUPS)
    gmax, glane = first_max(jnp.where(is_g, logits, NEG_BIG))
    g_w = 1.0 / jnp.sum(jnp.where(is_g, jnp.exp(logits - gmax), 0.0), axis=-1, keepdims=True)
    gidx = glane - GROUP_LANE0
    in_grp = (lane < N_EXPERTS) & ((lane // EXPERTS_PER_GROUP) == gidx)
    el = jnp.where(in_grp, logits, NEG_BIG)
    m1, i1 = first_max(el)
    m2, i2 = first_max(jnp.where(lane == i1, NEG_BIG, el))
    r = jnp.exp(m2 - m1)
    w0 = g_w / (1.0 + r)
    w1 = g_w * r / (1.0 + r)

    hot0 = lane == i1
    hot1 = lane == i2
    multi = jnp.where(hot0 | hot1, 1.0, 0.0)
    before = jnp.dot(stril_ref[...], multi.astype(BF16), preferred_element_type=F32) + carry_scr[...]
    rank0 = jnp.sum(jnp.where(hot0, before, 0.0), axis=-1, keepdims=True)
    rank1 = jnp.sum(jnp.where(hot1, before, 0.0), axis=-1, keepdims=True)
    carry = carry_scr[...] + jnp.sum(multi, axis=0, keepdims=True)
    carry_scr[...] = carry
    cnt_ref[...] = carry

    meta = jnp.zeros(logits.shape, F32)
    for idx, val in ((META_E0, i1.astype(F32)), (META_E1, i2.astype(F32)),
                     (META_R0, rank0), (META_R1, rank1), (META_W0, w0), (META_W1, w1)):
        meta = jnp.where(lane == idx, val, meta)
    meta_ref[...] = meta


def _outproj(da2, hg2, x2, mod3, wo_bf, norm_w, w_route, b_route, seq):
    t, d = x2.shape
    tm = ROW_TILE
    per_b = seq // tm
    row = lambda i: (i, 0)
    full = lambda i: (0, 0)
    stril = jnp.asarray(np.tril(np.ones((tm, tm), np.float32), -1), BF16)
    return pl.pallas_call(
        _outproj_body,
        grid=(t // tm,),
        in_specs=[pl.BlockSpec((tm, STREAM_W), row),
                  pl.BlockSpec((tm, STREAM_W), row),
                  pl.BlockSpec((tm, d), row),
                  pl.BlockSpec((1, 6, d), lambda i: (i // per_b, 0, 0)),
                  pl.BlockSpec((2 * STREAM_W, d), full),
                  pl.BlockSpec((1, d), full),
                  pl.BlockSpec((d, LANES), full),
                  pl.BlockSpec((1, LANES), full),
                  pl.BlockSpec((tm, tm), full)],
        out_specs=[pl.BlockSpec((tm, d), row),
                   pl.BlockSpec((tm, d), row),
                   pl.BlockSpec((tm, LANES), row),
                   pl.BlockSpec((1, LANES), full)],
        out_shape=[jax.ShapeDtypeStruct((t, d), F32),
                   jax.ShapeDtypeStruct((t, d), F32),
                   jax.ShapeDtypeStruct((t, LANES), F32),
                   jax.ShapeDtypeStruct((1, LANES), F32)],
        scratch_shapes=[pltpu.VMEM((1, LANES), F32)],
        compiler_params=pltpu.CompilerParams(dimension_semantics=("arbitrary",)),
        name="outproj_route",
    )(da2, hg2, x2, mod3, wo_bf, norm_w, w_route, b_route, stril)


def _scatter_body(pos_ref, h2_ref, xs_in, xs_hbm, sem, *, tile):
    del xs_in

    def row_copy(t, j):
        return pltpu.make_async_copy(h2_ref.at[pl.ds(t, 1)],
                                     xs_hbm.at[pl.ds(pos_ref[0, 0, 2 * t + j], 1)], sem)

    def issue(t, carry):
        row_copy(t, 0).start()
        row_copy(t, 1).start()
        return carry

    def drain(t, carry):
        row_copy(t, 0).wait()
        row_copy(t, 1).wait()
        return carry

    lax.fori_loop(0, tile, issue, 0)
    lax.fori_loop(0, tile, drain, 0)


def _scatter(pos3, h2, xs_zero):
    t, d = h2.shape
    tile = GATHER_TILE
    return pl.pallas_call(
        functools.partial(_scatter_body, tile=tile),
        grid=(t // tile,),
        in_specs=[pl.BlockSpec((1, 1, 2 * tile), lambda i: (i, 0, 0), memory_space=pltpu.SMEM),
                  pl.BlockSpec((tile, d), lambda i: (i, 0)),
                  pl.BlockSpec(memory_space=pl.ANY)],
        out_specs=pl.BlockSpec(memory_space=pl.ANY),
        out_shape=jax.ShapeDtypeStruct(xs_zero.shape, F32),
        scratch_shapes=[pltpu.SemaphoreType.DMA],
        input_output_aliases={2: 0},
        compiler_params=pltpu.CompilerParams(dimension_semantics=("arbitrary",)),
        name="moe_scatter",
    )(pos3, h2, xs_zero)


def _experts_body(te_ref, nv_ref, x_ref, wg_ref, wu_ref, wd_ref, y_ref, wgu_scr, wd_scr):
    i = pl.program_id(0)
    e = te_ref[i]
    prev = te_ref[jnp.maximum(i - 1, 0)]

    @pl.when((i == 0) | (e != prev))
    def _():
        wgu_scr[:, 0:EXPERT_FF] = wg_ref[0].astype(BF16)
        wgu_scr[:, EXPERT_FF:] = wu_ref[0].astype(BF16)
        wd_scr[...] = wd_ref[0].astype(BF16)

    @pl.when(i < nv_ref[0])
    def _():
        gu = jnp.dot(x_ref[...].astype(BF16), wgu_scr[...], preferred_element_type=F32)
        g = gu[:, 0:EXPERT_FF]
        act = (g * _sigmoid(g) * gu[:, EXPERT_FF:]).astype(BF16)
        y_ref[...] = jnp.dot(act, wd_scr[...], preferred_element_type=F32)

    @pl.when(i >= nv_ref[0])
    def _():
        y_ref[...] = jnp.zeros(y_ref.shape, F32)


def _experts(tile_expert, n_valid, xs, w_gate, w_up, w_down):
    n_slots, d = xs.shape
    tm = MOE_TILE
    live = lambda i, te, nv: (jnp.minimum(i, nv[0] - 1), 0)
    grid_spec = pltpu.PrefetchScalarGridSpec(
        num_scalar_prefetch=2,
        grid=(n_slots // tm,),
        in_specs=[pl.BlockSpec((tm, d), live),
                  pl.BlockSpec((1, d, EXPERT_FF), lambda i, te, nv: (te[i], 0, 0)),
                  pl.BlockSpec((1, d, EXPERT_FF), lambda i, te, nv: (te[i], 0, 0)),
                  pl.BlockSpec((1, EXPERT_FF, d), lambda i, te, nv: (te[i], 0, 0))],
        out_specs=pl.BlockSpec((tm, d), lambda i, te, nv: (i, 0)),
        scratch_shapes=[pltpu.VMEM((d, 2 * EXPERT_FF), BF16),
                        pltpu.VMEM((EXPERT_FF, d), BF16)],
    )
    return pl.pallas_call(
        _experts_body,
        grid_spec=grid_spec,
        out_shape=jax.ShapeDtypeStruct((n_slots, d), F32),
        compiler_params=pltpu.CompilerParams(dimension_semantics=("arbitrary",)),
        name="moe_experts",
    )(tile_expert, n_valid, xs, w_gate, w_up, w_down)


def _combine_body(pos_ref, x1_ref, meta_ref, mod_ref, nw_ref, ys_hbm, o_ref, rows_scr, sem, *, tile):
    def row_copy(t, j):
        return pltpu.make_async_copy(ys_hbm.at[pl.ds(pos_ref[0, 0, 2 * t + j], 1)],
                                     rows_scr.at[j, pl.ds(t, 1)], sem)

    def issue(t, carry):
        row_copy(t, 0).start()
        row_copy(t, 1).start()
        return carry

    def drain(t, carry):
        row_copy(t, 0).wait()
        row_copy(t, 1).wait()
        return carry

    lax.fori_loop(0, tile, issue, 0)
    lax.fori_loop(0, tile, drain, 0)

    meta = meta_ref[...]
    w0 = meta[:, META_W0:META_W0 + 1]
    w1 = meta[:, META_W1:META_W1 + 1]
    y = w0 * rows_scr[0] + w1 * rows_scr[1]
    x2 = x1_ref[...] + mod_ref[0, 5:6, :] * y
    o_ref[...] = x2 * lax.rsqrt(jnp.mean(x2 * x2, axis=-1, keepdims=True) + EPS) * nw_ref[...]


def _combine(pos3, x1, meta, mod3, norm_w, ys, seq):
    t, d = x1.shape
    tile = GATHER_TILE
    per_b = seq // tile
    row = lambda i: (i, 0)
    return pl.pallas_call(
        functools.partial(_combine_body, tile=tile),
        grid=(t // tile,),
        in_specs=[pl.BlockSpec((1, 1, 2 * tile), lambda i: (i, 0, 0), memory_space=pltpu.SMEM),
                  pl.BlockSpec((tile, d), row),
                  pl.BlockSpec((tile, LANES), row),
                  pl.BlockSpec((1, 6, d), lambda i: (i // per_b, 0, 0)),
                  pl.BlockSpec((1, d), lambda i: (0, 0)),
                  pl.BlockSpec(memory_space=pl.ANY)],
        out_specs=pl.BlockSpec((tile, d), row),
        out_shape=jax.ShapeDtypeStruct((t, d), F32),
        scratch_shapes=[pltpu.VMEM((2, tile, d), F32), pltpu.SemaphoreType.DMA],
        compiler_params=pltpu.CompilerParams(dimension_semantics=("arbitrary",)),
        name="moe_combine",
    )(pos3, x1, meta, mod3, norm_w, ys)


def _rope_inv_freq_lanes():
    inv_freq = ROPE_THETA ** (-jnp.arange(ROT_HALF, dtype=F32) / ROT_HALF)
    lane = np.arange(LANES) % DA_QK_DIM
    table = jnp.where(jnp.asarray(lane < ROT_DIM), inv_freq[lane % ROT_HALF], 0.0)
    return table.reshape(1, LANES).astype(F32)


def kernel(x, c, positions, norm1_w, norm2_w, final_norm_w, ada_w, ada_b, w_in, w_out, da_lambda_q1, da_lambda_k1, da_lambda_q2, da_lambda_k2, da_subln_w, hg_lower_bound, hg_norm_w, moe_w_group, moe_b_group, moe_w_router, moe_b_router, moe_w_gate, moe_w_up, moe_w_down):
    bsz, seq, d = x.shape
    assert d == D_MODEL and norm1_w.shape[0] == 1, "single-layer model of width 1024 only"
    assert seq % ATTN_TILE == 0 and seq % HGRN_BLOCK == 0 and seq % ROW_TILE == 0
    t = bsz * seq
    x2 = x.reshape(t, d)

    mod3 = _adaln(c, ada_w[0], ada_b).reshape(bsz, 6, d)

    pos_col = positions.astype(F32).reshape(t, 1)
    qt, k, vt, hq, lf, kf, gi, sg = _inproj(x2, mod3, norm1_w, w_in[0].astype(BF16), pos_col,
                                          _rope_inv_freq_lanes(), hg_lower_bound, seq)

    as3 = lambda a: a.reshape(bsz, seq, STREAM_W)
    lam_p = jnp.concatenate([da_lambda_q1, da_lambda_k1, da_lambda_q2, da_lambda_k2], axis=0)
    da = _attn(qt, as3(k), vt, lam_p, da_subln_w.reshape(HEAD_W, 1))
    hg = _hgrn(as3(hq), as3(lf), as3(kf), as3(gi), as3(sg), hg_norm_w)

    pad = jnp.zeros((d, LANES - N_EXPERTS - N_GROUPS), F32)
    w_route = jnp.concatenate([moe_w_router[0], moe_w_group[0], pad], axis=1)
    b_route = jnp.concatenate([moe_b_router[0], moe_b_group[0], pad[0]]).reshape(1, LANES)
    x1, h2, meta, cnt = _outproj(da.reshape(t, STREAM_W), hg.reshape(t, STREAM_W), x2, mod3,
                                 w_out[0].astype(BF16), norm2_w, w_route, b_route, seq)

    counts = cnt[0, :N_EXPERTS].astype(jnp.int32)
    tiles_e = (counts + MOE_TILE - 1) // MOE_TILE
    tile_end = jnp.cumsum(tiles_e)
    offs = (tile_end - tiles_e) * MOE_TILE
    ids = meta[:, META_E0:META_E1 + 1].astype(jnp.int32)
    ranks = meta[:, META_R0:META_R1 + 1].astype(jnp.int32)
    pos = offs[ids] + ranks
    n_tiles = (2 * t) // MOE_TILE + N_EXPERTS
    n_valid = tile_end[-1:]
    tile_ids = jnp.minimum(jnp.arange(n_tiles, dtype=jnp.int32), n_valid - 1)
    tile_expert = jnp.sum(tile_ids[:, None] >= tile_end[None, :], axis=1).astype(jnp.int32)
    pos3 = pos.reshape(t // GATHER_TILE, 1, 2 * GATHER_TILE)

    xs = _scatter(pos3, h2, jnp.zeros((n_tiles * MOE_TILE, d), F32))
    ys = _experts(tile_expert, n_valid.astype(jnp.int32), xs, moe_w_gate[0], moe_w_up[0], moe_w_down[0])
    out = _combine(pos3, x1, meta, mod3, final_norm_w.reshape(1, d), ys, seq)
    return out.reshape(bsz, seq, d)
```

```python
import functools
import math

import numpy as np
import jax
import jax.numpy as jnp
from jax import lax
from jax.experimental import pallas as pl
from jax.experimental.pallas import tpu as pltpu

F32 = jnp.float32
BF16 = jnp.bfloat16
HIGHEST = lax.Precision.HIGHEST

LANES = 128
SUBLANES = 8
D_MODEL = 1024
HEADS = 4
HEAD_W = 128
STREAM_W = HEADS * HEAD_W
N_STREAMS = 7
DA_QK_DIM = 64
ROPE_THETA = 500000.0
ROT_DIM = DA_QK_DIM // 4
ROT_HALF = ROT_DIM // 2
N_GROUPS = 4
EXPERTS_PER_GROUP = 8
N_EXPERTS = N_GROUPS * EXPERTS_PER_GROUP
EXPERT_FF = 256
EPS = 1e-6
LAM_INIT = 0.8 - 0.6 * math.exp(-0.3 * 0)
NEG_BIG = -1e30

ROW_TILE = 256
ATTN_TILE = 1024
ATTN_KEY_TILE = 512
HGRN_BLOCK = 256
MOE_TILE = 256
GATHER_TILE = 256

NT_DIMS = (((1,), (1,)), ((), ()))
TN_DIMS = (((0,), (0,)), ((), ()))


def _sigmoid(x):
    return 1.0 / (1.0 + jnp.exp(-x))


def _adaln_body(c_ref, w_ref, b_ref, o_ref):
    c = c_ref[...]
    ca = c * _sigmoid(c)
    o_ref[...] = jnp.dot(ca, w_ref[...], preferred_element_type=F32, precision=HIGHEST) + b_ref[...]


def _adaln(c, ada_w, ada_b):
    bsz, d = c.shape
    n = ada_w.shape[1]
    tn = 1024
    return pl.pallas_call(
        _adaln_body,
        grid=(n // tn,),
        in_specs=[pl.BlockSpec((bsz, d), lambda j: (0, 0)),
                  pl.BlockSpec((d, tn), lambda j: (0, j)),
                  pl.BlockSpec((1, tn), lambda j: (0, j))],
        out_specs=pl.BlockSpec((bsz, tn), lambda j: (0, j)),
        out_shape=jax.ShapeDtypeStruct((bsz, n), F32),
        name="adaln",
    )(c, ada_w, ada_b)


def _inproj_body(x_ref, mod_ref, nw_ref, w_ref, pos_ref, invf_ref, lbr_ref,
                 q_ref, k_ref, v_ref, hq_ref, lf_ref, kf_ref, gi_ref, sg_ref):
    x = x_ref[...]
    ms = jnp.mean(x * x, axis=-1, keepdims=True)
    y = x * lax.rsqrt(ms + EPS) * nw_ref[...]
    shift = mod_ref[0, 0:1, :]
    scale = mod_ref[0, 1:2, :]
    h = (y * (1.0 + scale) + shift).astype(BF16)

    ang = pos_ref[...] * invf_ref[...]
    cosv = jnp.cos(ang)
    sinv = jnp.sin(ang)
    lane = lax.broadcasted_iota(jnp.int32, (1, LANES), 1) % DA_QK_DIM
    sin_lo = jnp.where(lane < ROT_HALF, -sinv, 0.0)
    sin_hi = jnp.where((lane >= ROT_HALF) & (lane < ROT_DIM), sinv, 0.0)

    def proj(j):
        return jnp.dot(h, w_ref[:, j * STREAM_W:(j + 1) * STREAM_W], preferred_element_type=F32)

    def rope(t):
        outs = []
        for hb in range(HEADS):
            tc = t[:, hb * HEAD_W:(hb + 1) * HEAD_W]
            outs.append(tc * cosv
                        + pltpu.roll(tc, LANES - ROT_HALF, 1) * sin_lo
                        + pltpu.roll(tc, ROT_HALF, 1) * sin_hi)
        return jnp.concatenate(outs, axis=1)

    q_ref[0] = (rope(proj(0)) * (DA_QK_DIM ** -0.5 * math.log2(math.e))).T.astype(BF16)
    k_ref[...] = rope(proj(1)).astype(BF16)
    v_ref[0] = proj(2).astype(BF16).T

    gq = proj(3)
    hq_ref[...] = gq * _sigmoid(gq)

    a = lbr_ref[...]
    amax = jnp.max(a, axis=0, keepdims=True)
    ea = jnp.exp(a - amax)
    lb = ea[0:1, :] / jnp.sum(ea, axis=0, keepdims=True)
    gf = proj(4)
    f = lb + (1.0 - lb) * _sigmoid(gf)
    lf_ref[...] = jnp.log(f)
    kf_ref[...] = 1.0 - f

    gi_ref[...] = proj(5).astype(BF16)
    gg = proj(6)
    sg_ref[...] = (gg * _sigmoid(gg)).astype(BF16)


def _inproj(x2, mod3, norm_w, w_bf, pos_col, invf, lb_raw, seq):
    t, d = x2.shape
    tm = ROW_TILE
    per_b = seq // tm
    row = lambda i: (i, 0)
    full = lambda i: (0, 0)
    out_bf = jax.ShapeDtypeStruct((t, STREAM_W), BF16)
    out_f = jax.ShapeDtypeStruct((t, STREAM_W), F32)
    out_t = jax.ShapeDtypeStruct((t // seq, STREAM_W, seq), BF16)
    stream = pl.BlockSpec((tm, STREAM_W), row)
    stream_t = pl.BlockSpec((1, STREAM_W, tm), lambda i: (i // per_b, 0, i % per_b))
    return pl.pallas_call(
        _inproj_body,
        grid=(t // tm,),
        in_specs=[pl.BlockSpec((tm, d), row),
                  pl.BlockSpec((1, 6, d), lambda i: (i // per_b, 0, 0)),
                  pl.BlockSpec((1, d), full),
                  pl.BlockSpec((d, N_STREAMS * STREAM_W), full),
                  pl.BlockSpec((tm, 1), row),
                  pl.BlockSpec((1, LANES), full),
                  pl.BlockSpec(lb_raw.shape, full)],
        out_specs=[stream_t, stream, stream_t] + [stream] * 5,
        out_shape=[out_t, out_bf, out_t, out_f, out_f, out_f, out_bf, out_bf],
        compiler_params=pltpu.CompilerParams(vmem_limit_bytes=56 * 1024 * 1024),
        name="inproj",
    )(x2, mod3, norm_w, w_bf, pos_col, invf, lb_raw)


ONES_ROWS = 16


def _attn_body(qt_ref, k_ref, vt_ref, lam_ref, sw_ref, o_ref, s_scr, m_scr, a_scr, *, tile, ktile):
    qi = pl.program_id(2)
    qt = qt_ref[0]
    feat = lax.broadcasted_iota(jnp.int32, (HEAD_W, 1), 0)
    zero = jnp.zeros_like(qt)
    qmaps = (jnp.where(feat < DA_QK_DIM, qt, zero), jnp.where(feat >= DA_QK_DIM, qt, zero))
    ones = jnp.ones((ONES_ROWS, ktile), BF16)

    m_scr[...] = jnp.full(m_scr.shape, NEG_BIG, F32)
    a_scr[...] = jnp.zeros(a_scr.shape, F32)

    ratio = tile // ktile
    n_full = qi * ratio

    def score_block(ki, diag_offset):
        start = pl.multiple_of(ki * ktile, ktile)
        kb = k_ref[0, pl.ds(start, ktile), :]
        for mp in range(2):
            s = jnp.dot(kb, qmaps[mp], preferred_element_type=F32)
            if diag_offset is not None:
                key = lax.broadcasted_iota(jnp.int32, s.shape, 0) + diag_offset
                qry = lax.broadcasted_iota(jnp.int32, s.shape, 1)
                s = jnp.where(key <= qry, s, NEG_BIG)
            grouped = s.reshape(ktile // SUBLANES, SUBLANES, tile)
            m_scr[mp] = jnp.maximum(m_scr[mp], jnp.max(grouped, axis=0))
            s_scr[mp, pl.ds(start, ktile), :] = s

    def value_block(ki, col_max):
        start = pl.multiple_of(ki * ktile, ktile)
        vb = jnp.concatenate([vt_ref[0, :, pl.ds(start, ktile)], ones], axis=0)
        for mp in range(2):
            p = jnp.exp2(s_scr[mp, pl.ds(start, ktile), :] - col_max[mp]).astype(BF16)
            a_scr[mp] += jnp.dot(vb, p, preferred_element_type=F32)

    def score_body(ki, carry):
        score_block(ki, None)
        return carry

    lax.fori_loop(0, n_full, score_body, 0)
    for j in range(ratio):
        score_block(n_full + j, j * ktile)
    col_max = [jnp.max(m_scr[mp], axis=0, keepdims=True) for mp in range(2)]

    def value_body(ki, carry):
        value_block(ki, col_max)
        return carry

    lax.fori_loop(0, n_full + ratio, value_body, 0)

    lp = lam_ref[...]
    lam = (jnp.exp(jnp.sum(lp[0:1] * lp[1:2], axis=-1, keepdims=True))
           - jnp.exp(jnp.sum(lp[2:3] * lp[3:4], axis=-1, keepdims=True)) + LAM_INIT)
    o = (a_scr[0, 0:HEAD_W, :] / a_scr[0, HEAD_W:HEAD_W + 1, :]
         - lam * (a_scr[1, 0:HEAD_W, :] / a_scr[1, HEAD_W:HEAD_W + 1, :]))
    o = o * lax.rsqrt(jnp.mean(o * o, axis=0, keepdims=True) + EPS) * sw_ref[...]
    o_ref[0] = (o * (1.0 - LAM_INIT)).T.astype(BF16)


def _attn(qt3, k3, vt3, lam_p, subln_col):
    bsz, seq, _ = k3.shape
    tile = ATTN_TILE
    return pl.pallas_call(
        functools.partial(_attn_body, tile=tile, ktile=ATTN_KEY_TILE),
        grid=(bsz, HEADS, seq // tile),
        in_specs=[pl.BlockSpec((1, HEAD_W, tile), lambda b, h, i: (b, h, i)),
                  pl.BlockSpec((1, seq, HEAD_W), lambda b, h, i: (b, 0, h)),
                  pl.BlockSpec((1, HEAD_W, seq), lambda b, h, i: (b, h, 0)),
                  pl.BlockSpec(lam_p.shape, lambda b, h, i: (0, 0)),
                  pl.BlockSpec((HEAD_W, 1), lambda b, h, i: (0, 0))],
        out_specs=pl.BlockSpec((1, tile, HEAD_W), lambda b, h, i: (b, i, h)),
        out_shape=jax.ShapeDtypeStruct((bsz, seq, STREAM_W), BF16),
        scratch_shapes=[pltpu.VMEM((2, seq, tile), F32),
                        pltpu.VMEM((2, SUBLANES, tile), F32),
                        pltpu.VMEM((2, HEAD_W + ONES_ROWS, tile), F32)],
        compiler_params=pltpu.CompilerParams(vmem_limit_bytes=56 * 1024 * 1024),
        name="diff_attn",
    )(qt3, k3, vt3, lam_p, subln_col)


def _hgrn_levels(block):
    return [block >> (i + 1) for i in range(block.bit_length() - 1)]


def _hgrn_constants(block):
    t = np.arange(block)[:, None]
    s = np.arange(block)[None, :]
    tril = (s <= t).astype(np.float32)
    lv = np.full((block, block), -1, np.int32)
    halves = _hgrn_levels(block)
    for li, m in enumerate(halves):
        same = (t // (2 * m)) == (s // (2 * m))
        lv[same & ((t & m) != 0) & ((s & m) == 0)] = li
    lv[np.arange(block), np.arange(block)] = len(halves)
    return jnp.asarray(tril, BF16), jnp.asarray(lv)


def _level_ref(b_ref, h, m, block):
    def bcast(r, n):
        return jnp.broadcast_to(b_ref[h, pl.ds(r, 1), :], (n, HEAD_W))

    if 2 * m >= SUBLANES:
        return jnp.concatenate([bcast(s0 + m - 1, 2 * m) for s0 in range(0, block, 2 * m)], axis=0)
    sub = lax.broadcasted_iota(jnp.int32, (SUBLANES, HEAD_W), 0)
    pieces = []
    for s0 in range(0, block, SUBLANES):
        piece = bcast(s0 + m - 1, SUBLANES)
        for j in range(1, SUBLANES // (2 * m)):
            piece = jnp.where(sub >= 2 * m * j, bcast(s0 + 2 * m * j + m - 1, SUBLANES), piece)
        pieces.append(piece)
    return jnp.concatenate(pieces, axis=0)


def _hgrn_body(hq_ref, lf_ref, kf_ref, gi_ref, sg_ref, nw_ref, tril_ref, lv_ref, o_ref,
               st_scr, b_scr, *, block):
    @pl.when(pl.program_id(1) == 0)
    def _():
        st_scr[...] = jnp.zeros(st_scr.shape, F32)

    tril = tril_ref[...]
    lv = lv_ref[...]
    row = lax.broadcasted_iota(jnp.int32, (block, 1), 0)
    halves = _hgrn_levels(block)

    for h in range(HEADS):
        cs = slice(h * HEAD_W, (h + 1) * HEAD_W)
        q = hq_ref[0, :, cs]
        lf = lf_ref[0, :, cs]
        k = kf_ref[0, :, cs]
        v = gi_ref[0, :, cs]

        hi = lf.astype(BF16)
        r1 = lf - hi.astype(F32)
        mid = r1.astype(BF16)
        lo = (r1 - mid.astype(F32)).astype(BF16)
        b = (jnp.dot(tril, hi, preferred_element_type=F32)
             + jnp.dot(tril, mid, preferred_element_type=F32)
             + jnp.dot(tril, lo, preferred_element_type=F32))
        b_scr[h] = b

        qb = q.astype(BF16)
        kb = k.astype(BF16)
        scores = jnp.where(lv == len(halves),
                           lax.dot_general(qb, kb, NT_DIMS, preferred_element_type=F32), 0.0)
        for li, m in enumerate(halves):
            ref = _level_ref(b_scr, h, m, block)
            e = jnp.exp(-jnp.abs(b - ref))
            xl = (jnp.where((row & m) != 0, q, k) * e).astype(BF16)
            p = lax.dot_general(xl, xl, NT_DIMS, preferred_element_type=F32)
            scores = jnp.where(lv == li, p, scores)
        o_intra = jnp.dot(scores.astype(BF16), v, preferred_element_type=F32)

        st = st_scr[h]
        o_inter = lax.dot_general((q * jnp.exp(b)).astype(BF16), st.astype(BF16), NT_DIMS,
                                  preferred_element_type=F32)
        b_last = b[block - 1:block, :]
        kdec = (k * jnp.exp(b_last - b)).astype(BF16)
        st_scr[h] = st * jnp.exp(b_last) + lax.dot_general(v, kdec, TN_DIMS,
                                                            preferred_element_type=F32)

        o = o_inter + o_intra
        o = o * lax.rsqrt(jnp.mean(o * o, axis=-1, keepdims=True) + EPS) * nw_ref[...]
        o_ref[0, :, cs] = (o * sg_ref[0, :, cs].astype(F32)).astype(BF16)


def _hgrn(hq3, lf3, kf3, gi3, sg3, norm_w):
    bsz, seq, _ = hq3.shape
    block = HGRN_BLOCK
    tril, lv = _hgrn_constants(block)
    blk = pl.BlockSpec((1, block, STREAM_W), lambda b, g: (b, g, 0))
    const = lambda b, g: (0, 0)
    return pl.pallas_call(
        functools.partial(_hgrn_body, block=block),
        grid=(bsz, seq // block),
        in_specs=[blk, blk, blk, blk, blk,
                  pl.BlockSpec((1, HEAD_W), const),
                  pl.BlockSpec((block, block), const),
                  pl.BlockSpec((block, block), const)],
        out_specs=blk,
        out_shape=jax.ShapeDtypeStruct((bsz, seq, STREAM_W), BF16),
        scratch_shapes=[pltpu.VMEM((HEADS, HEAD_W, HEAD_W), F32),
                        pltpu.VMEM((HEADS, block, HEAD_W), F32)],
        name="hgrn2",
    )(hq3, lf3, kf3, gi3, sg3, norm_w, tril, lv)


META_E0, META_E1, META_R0, META_R1, META_W0, META_W1 = range(6)
GROUP_LANE0 = N_EXPERTS


def _outproj_body(da_ref, hg_ref, x_ref, mod_ref, wo_ref, nw_ref, wr_ref, br_ref, stril_ref,
                  x1_ref, h2_ref, meta_ref, cnt_ref, carry_scr):
    @pl.when(pl.program_id(0) == 0)
    def _():
        carry_scr[...] = jnp.zeros(carry_scr.shape, F32)

    attn = (jnp.dot(da_ref[...], wo_ref[0:STREAM_W, :], preferred_element_type=F32)
            + jnp.dot(hg_ref[...], wo_ref[STREAM_W:, :], preferred_element_type=F32))
    gate1 = mod_ref[0, 2:3, :]
    shift2 = mod_ref[0, 3:4, :]
    scale2 = mod_ref[0, 4:5, :]
    x1 = x_ref[...] + gate1 * attn
    x1_ref[...] = x1
    h2 = (x1 * lax.rsqrt(jnp.mean(x1 * x1, axis=-1, keepdims=True) + EPS) * nw_ref[...]
          * (1.0 + scale2) + shift2)
    h2_ref[...] = h2

    logits = jnp.dot(h2, wr_ref[...], preferred_element_type=F32, precision=HIGHEST) + br_ref[...]
    lane = lax.broadcasted_iota(jnp.int32, logits.shape, 1)
    far = jnp.int32(LANES)

    def first_max(vals):
        mx = jnp.max(vals, axis=-1, keepdims=True)
        return mx, jnp.min(jnp.where(vals == mx, lane, far), axis=-1, keepdims=True)

    is_g = (lane >= GROUP_LANE0) & (lane < GROUP_LANE0 + N_GROUPS)
    gmax, glane = first_max(jnp.where(is_g, logits, NEG_BIG))
    g_w = 1.0 / jnp.sum(jnp.where(is_g, jnp.exp(logits - gmax), 0.0), axis=-1, keepdims=True)
    gidx = glane - GROUP_LANE0
    in_grp = (lane < N_EXPERTS) & ((lane // EXPERTS_PER_GROUP) == gidx)
    el = jnp.where(in_grp, logits, NEG_BIG)
    m1, i1 = first_max(el)
    m2, i2 = first_max(jnp.where(lane == i1, NEG_BIG, el))
    r = jnp.exp(m2 - m1)
    w0 = g_w / (1.0 + r)
    w1 = g_w * r / (1.0 + r)

    hot0 = lane == i1
    hot1 = lane == i2
    multi = jnp.where(hot0 | hot1, 1.0, 0.0)
    before = jnp.dot(stril_ref[...], multi.astype(BF16), preferred_element_type=F32) + carry_scr[...]
    rank0 = jnp.sum(jnp.where(hot0, before, 0.0), axis=-1, keepdims=True)
    rank1 = jnp.sum(jnp.where(hot1, before, 0.0), axis=-1, keepdims=True)
    carry = carry_scr[...] + jnp.sum(multi, axis=0, keepdims=True)
    carry_scr[...] = carry
    cnt_ref[...] = carry

    meta = jnp.zeros(logits.shape, F32)
    for idx, val in ((META_E0, i1.astype(F32)), (META_E1, i2.astype(F32)),
                     (META_R0, rank0), (META_R1, rank1), (META_W0, w0), (META_W1, w1)):
        meta = jnp.where(lane == idx, val, meta)
    meta_ref[...] = meta


def _outproj(da2, hg2, x2, mod3, wo_bf, norm_w, w_route, b_route, seq):
    t, d = x2.shape
    tm = ROW_TILE
    per_b = seq // tm
    row = lambda i: (i, 0)
    full = lambda i: (0, 0)
    stril = jnp.asarray(np.tril(np.ones((tm, tm), np.float32), -1), BF16)
    return pl.pallas_call(
        _outproj_body,
        grid=(t // tm,),
        in_specs=[pl.BlockSpec((tm, STREAM_W), row),
                  pl.BlockSpec((tm, STREAM_W), row),
                  pl.BlockSpec((tm, d), row),
                  pl.BlockSpec((1, 6, d), lambda i: (i // per_b, 0, 0)),
                  pl.BlockSpec((2 * STREAM_W, d), full),
                  pl.BlockSpec((1, d), full),
                  pl.BlockSpec((d, LANES), full),
                  pl.BlockSpec((1, LANES), full),
                  pl.BlockSpec((tm, tm), full)],
        out_specs=[pl.BlockSpec((tm, d), row),
                   pl.BlockSpec((tm, d), row),
                   pl.BlockSpec((tm, LANES), row),
                   pl.BlockSpec((1, LANES), full)],
        out_shape=[jax.ShapeDtypeStruct((t, d), F32),
                   jax.ShapeDtypeStruct((t, d), F32),
                   jax.ShapeDtypeStruct((t, LANES), F32),
                   jax.ShapeDtypeStruct((1, LANES), F32)],
        scratch_shapes=[pltpu.VMEM((1, LANES), F32)],
        compiler_params=pltpu.CompilerParams(dimension_semantics=("arbitrary",)),
        name="outproj_route",
    )(da2, hg2, x2, mod3, wo_bf, norm_w, w_route, b_route, stril)


def _scatter_body(pos_ref, h2_ref, xs_in, xs_hbm, sem, *, tile):
    del xs_in

    def row_copy(t, j):
        return pltpu.make_async_copy(h2_ref.at[pl.ds(t, 1)],
                                     xs_hbm.at[pl.ds(pos_ref[0, 0, 2 * t + j], 1)], sem)

    def issue(t, carry):
        row_copy(t, 0).start()
        row_copy(t, 1).start()
        return carry

    def drain(t, carry):
        row_copy(t, 0).wait()
        row_copy(t, 1).wait()
        return carry

    lax.fori_loop(0, tile, issue, 0)
    lax.fori_loop(0, tile, drain, 0)


def _scatter(pos3, h2, xs_zero):
    t, d = h2.shape
    tile = GATHER_TILE
    return pl.pallas_call(
        functools.partial(_scatter_body, tile=tile),
        grid=(t // tile,),
        in_specs=[pl.BlockSpec((1, 1, 2 * tile), lambda i: (i, 0, 0), memory_space=pltpu.SMEM),
                  pl.BlockSpec((tile, d), lambda i: (i, 0)),
                  pl.BlockSpec(memory_space=pl.ANY)],
        out_specs=pl.BlockSpec(memory_space=pl.ANY),
        out_shape=jax.ShapeDtypeStruct(xs_zero.shape, F32),
        scratch_shapes=[pltpu.SemaphoreType.DMA],
        input_output_aliases={2: 0},
        compiler_params=pltpu.CompilerParams(dimension_semantics=("arbitrary",)),
        name="moe_scatter",
    )(pos3, h2, xs_zero)


def _experts_body(te_ref, nv_ref, x_ref, wg_ref, wu_ref, wd_ref, y_ref, wgu_scr, wd_scr):
    i = pl.program_id(0)
    e = te_ref[i]
    prev = te_ref[jnp.maximum(i - 1, 0)]

    @pl.when((i == 0) | (e != prev))
    def _():
        wgu_scr[:, 0:EXPERT_FF] = wg_ref[0].astype(BF16)
        wgu_scr[:, EXPERT_FF:] = wu_ref[0].astype(BF16)
        wd_scr[...] = wd_ref[0].astype(BF16)

    @pl.when(i < nv_ref[0])
    def _():
        gu = jnp.dot(x_ref[...].astype(BF16), wgu_scr[...], preferred_element_type=F32)
        g = gu[:, 0:EXPERT_FF]
        act = (g * _sigmoid(g) * gu[:, EXPERT_FF:]).astype(BF16)
        y_ref[...] = jnp.dot(act, wd_scr[...], preferred_element_type=F32)

    @pl.when(i >= nv_ref[0])
    def _():
        y_ref[...] = jnp.zeros(y_ref.shape, F32)


def _experts(tile_expert, n_valid, xs, w_gate, w_up, w_down):
    n_slots, d = xs.shape
    tm = MOE_TILE
    live = lambda i, te, nv: (jnp.minimum(i, nv[0] - 1), 0)
    grid_spec = pltpu.PrefetchScalarGridSpec(
        num_scalar_prefetch=2,
        grid=(n_slots // tm,),
        in_specs=[pl.BlockSpec((tm, d), live),
                  pl.BlockSpec((1, d, EXPERT_FF), lambda i, te, nv: (te[i], 0, 0)),
                  pl.BlockSpec((1, d, EXPERT_FF), lambda i, te, nv: (te[i], 0, 0)),
                  pl.BlockSpec((1, EXPERT_FF, d), lambda i, te, nv: (te[i], 0, 0))],
        out_specs=pl.BlockSpec((tm, d), lambda i, te, nv: (i, 0)),
        scratch_shapes=[pltpu.VMEM((d, 2 * EXPERT_FF), BF16),
                        pltpu.VMEM((EXPERT_FF, d), BF16)],
    )
    return pl.pallas_call(
        _experts_body,
        grid_spec=grid_spec,
        out_shape=jax.ShapeDtypeStruct((n_slots, d), F32),
        compiler_params=pltpu.CompilerParams(dimension_semantics=("arbitrary",)),
        name="moe_experts",
    )(tile_expert, n_valid, xs, w_gate, w_up, w_down)


def _combine_body(pos_ref, x1_ref, meta_ref, mod_ref, nw_ref, ys_hbm, o_ref, rows_scr, sem, *, tile):
    def row_copy(t, j):
        return pltpu.make_async_copy(ys_hbm.at[pl.ds(pos_ref[0, 0, 2 * t + j], 1)],
                                     rows_scr.at[j, pl.ds(t, 1)], sem)

    def issue(t, carry):
        row_copy(t, 0).start()
        row_copy(t, 1).start()
        return carry

    def drain(t, carry):
        row_copy(t, 0).wait()
        row_copy(t, 1).wait()
        return carry

    lax.fori_loop(0, tile, issue, 0)
    lax.fori_loop(0, tile, drain, 0)

    meta = meta_ref[...]
    w0 = meta[:, META_W0:META_W0 + 1]
    w1 = meta[:, META_W1:META_W1 + 1]
    y = w0 * rows_scr[0] + w1 * rows_scr[1]
    x2 = x1_ref[...] + mod_ref[0, 5:6, :] * y
    o_ref[...] = x2 * lax.rsqrt(jnp.mean(x2 * x2, axis=-1, keepdims=True) + EPS) * nw_ref[...]


def _combine(pos3, x1, meta, mod3, norm_w, ys, seq):
    t, d = x1.shape
    tile = GATHER_TILE
    per_b = seq // tile
    row = lambda i: (i, 0)
    return pl.pallas_call(
        functools.partial(_combine_body, tile=tile),
        grid=(t // tile,),
        in_specs=[pl.BlockSpec((1, 1, 2 * tile), lambda i: (i, 0, 0), memory_space=pltpu.SMEM),
                  pl.BlockSpec((tile, d), row),
                  pl.BlockSpec((tile, LANES), row),
                  pl.BlockSpec((1, 6, d), lambda i: (i // per_b, 0, 0)),
                  pl.BlockSpec((1, d), lambda i: (0, 0)),
                  pl.BlockSpec(memory_space=pl.ANY)],
        out_specs=pl.BlockSpec((tile, d), row),
        out_shape=jax.ShapeDtypeStruct((t, d), F32),
        scratch_shapes=[pltpu.VMEM((2, tile, d), F32), pltpu.SemaphoreType.DMA],
        compiler_params=pltpu.CompilerParams(dimension_semantics=("arbitrary",)),
        name="moe_combine",
    )(pos3, x1, meta, mod3, norm_w, ys)


def _rope_inv_freq_lanes():
    inv_freq = ROPE_THETA ** (-jnp.arange(ROT_HALF, dtype=F32) / ROT_HALF)
    lane = np.arange(LANES) % DA_QK_DIM
    table = jnp.where(jnp.asarray(lane < ROT_DIM), inv_freq[lane % ROT_HALF], 0.0)
    return table.reshape(1, LANES).astype(F32)


def kernel(x, c, positions, norm1_w, norm2_w, final_norm_w, ada_w, ada_b, w_in, w_out, da_lambda_q1, da_lambda_k1, da_lambda_q2, da_lambda_k2, da_subln_w, hg_lower_bound, hg_norm_w, moe_w_group, moe_b_group, moe_w_router, moe_b_router, moe_w_gate, moe_w_up, moe_w_down):
    bsz, seq, d = x.shape
    assert d == D_MODEL and norm1_w.shape[0] == 1, "single-layer model of width 1024 only"
    assert seq % ATTN_TILE == 0 and seq % HGRN_BLOCK == 0 and seq % ROW_TILE == 0
    t = bsz * seq
    x2 = x.reshape(t, d)

    mod3 = _adaln(c, ada_w[0], ada_b).reshape(bsz, 6, d)

    pos_col = positions.astype(F32).reshape(t, 1)
    qt, k, vt, hq, lf, kf, gi, sg = _inproj(x2, mod3, norm1_w, w_in[0].astype(BF16), pos_col,
                                          _rope_inv_freq_lanes(), hg_lower_bound, seq)

    as3 = lambda a: a.reshape(bsz, seq, STREAM_W)
    lam_p = jnp.concatenate([da_lambda_q1, da_lambda_k1, da_lambda_q2, da_lambda_k2], axis=0)
    da = _attn(qt, as3(k), vt, lam_p, da_subln_w.reshape(HEAD_W, 1))
    hg = _hgrn(as3(hq), as3(lf), as3(kf), as3(gi), as3(sg), hg_norm_w)

    pad = jnp.zeros((d, LANES - N_EXPERTS - N_GROUPS), F32)
    w_route = jnp.concatenate([moe_w_router[0], moe_w_group[0], pad], axis=1)
    b_route = jnp.concatenate([moe_b_router[0], moe_b_group[0], pad[0]]).reshape(1, LANES)
    x1, h2, meta, cnt = _outproj(da.reshape(t, STREAM_W), hg.reshape(t, STREAM_W), x2, mod3,
                                 w_out[0].astype(BF16), norm2_w, w_route, b_route, seq)

    counts = cnt[0, :N_EXPERTS].astype(jnp.int32)
    tiles_e = (counts + MOE_TILE - 1) // MOE_TILE
    tile_end = jnp.cumsum(tiles_e)
    offs = (tile_end - tiles_e) * MOE_TILE
    ids = meta[:, META_E0:META_E1 + 1].astype(jnp.int32)
    ranks = meta[:, META_R0:META_R1 + 1].astype(jnp.int32)
    pos = offs[ids] + ranks
    n_tiles = (2 * t) // MOE_TILE + N_EXPERTS
    n_valid = tile_end[-1:]
    tile_ids = jnp.minimum(jnp.arange(n_tiles, dtype=jnp.int32), n_valid - 1)
    tile_expert = jnp.sum(tile_ids[:, None] >= tile_end[None, :], axis=1).astype(jnp.int32)
    pos3 = pos.reshape(t // GATHER_TILE, 1, 2 * GATHER_TILE)

    xs = _scatter(pos3, h2, jnp.zeros((n_tiles * MOE_TILE, d), F32))
    ys = _experts(tile_expert, n_valid.astype(jnp.int32), xs, moe_w_gate[0], moe_w_up[0], moe_w_down[0])
    out = _combine(pos3, x1, meta, mod3, final_norm_w.reshape(1, d), ys, seq)
    return out.reshape(bsz, seq, d)
```

```python
import functools
import math

import numpy as np
import jax
import jax.numpy as jnp
from jax import lax
from jax.experimental import pallas as pl
from jax.experimental.pallas import tpu as pltpu

F32 = jnp.float32
BF16 = jnp.bfloat16
HIGHEST = lax.Precision.HIGHEST

LANES = 128
SUBLANES = 8
D_MODEL = 1024
HEADS = 4
HEAD_W = 128
STREAM_W = HEADS * HEAD_W
N_STREAMS = 7
DA_QK_DIM = 64
ROPE_THETA = 500000.0
ROT_DIM = DA_QK_DIM // 4
ROT_HALF = ROT_DIM // 2
N_GROUPS = 4
EXPERTS_PER_GROUP = 8
N_EXPERTS = N_GROUPS * EXPERTS_PER_GROUP
EXPERT_FF = 256
EPS = 1e-6
LAM_INIT = 0.8 - 0.6 * math.exp(-0.3 * 0)
NEG_BIG = -1e30

ROW_TILE = 256
ATTN_TILE = 1024
ATTN_KEY_TILE = 512
HGRN_BLOCK = 256
MOE_TILE = 256
GATHER_TILE = 256

NT_DIMS = (((1,), (1,)), ((), ()))
TN_DIMS = (((0,), (0,)), ((), ()))


def _sigmoid(x):
    return 1.0 / (1.0 + jnp.exp(-x))


SLAB = D_MODEL // LANES


def _slab_store(ref, val):
    rows = val.shape[0]
    for c in range(SLAB):
        ref[pl.ds(c, rows, stride=SLAB), :] = val[:, c * LANES:(c + 1) * LANES]


def _slab_load(ref, rows):
    return jnp.concatenate([ref[pl.ds(c, rows, stride=SLAB), :] for c in range(SLAB)], axis=1)


def _adaln_body(c_ref, w_ref, b_ref, o_ref):
    c = c_ref[...]
    ca = c * _sigmoid(c)
    o_ref[...] = jnp.dot(ca, w_ref[...], preferred_element_type=F32, precision=HIGHEST) + b_ref[...]


def _adaln(c, ada_w, ada_b):
    bsz, d = c.shape
    n = ada_w.shape[1]
    tn = 1024
    return pl.pallas_call(
        _adaln_body,
        grid=(n // tn,),
        in_specs=[pl.BlockSpec((bsz, d), lambda j: (0, 0)),
                  pl.BlockSpec((d, tn), lambda j: (0, j)),
                  pl.BlockSpec((1, tn), lambda j: (0, j))],
        out_specs=pl.BlockSpec((bsz, tn), lambda j: (0, j)),
        out_shape=jax.ShapeDtypeStruct((bsz, n), F32),
        name="adaln",
    )(c, ada_w, ada_b)


def _inproj_body(x_ref, mod_ref, nw_ref, w_ref, pos_ref, invf_ref, lbr_ref,
                 q_ref, k_ref, v_ref, hq_ref, lf_ref, kf_ref, gi_ref, sg_ref):
    x = x_ref[...]
    ms = jnp.mean(x * x, axis=-1, keepdims=True)
    y = x * lax.rsqrt(ms + EPS) * nw_ref[...]
    shift = mod_ref[0, 0:1, :]
    scale = mod_ref[0, 1:2, :]
    h = (y * (1.0 + scale) + shift).astype(BF16)

    ang = pos_ref[...] * invf_ref[...]
    cosv = jnp.cos(ang)
    sinv = jnp.sin(ang)
    lane = lax.broadcasted_iota(jnp.int32, (1, LANES), 1) % DA_QK_DIM
    sin_lo = jnp.where(lane < ROT_HALF, -sinv, 0.0)
    sin_hi = jnp.where((lane >= ROT_HALF) & (lane < ROT_DIM), sinv, 0.0)

    def proj(j):
        return jnp.dot(h, w_ref[:, j * STREAM_W:(j + 1) * STREAM_W], preferred_element_type=F32)

    def rope(t):
        outs = []
        for hb in range(HEADS):
            tc = t[:, hb * HEAD_W:(hb + 1) * HEAD_W]
            outs.append(tc * cosv
                        + pltpu.roll(tc, LANES - ROT_HALF, 1) * sin_lo
                        + pltpu.roll(tc, ROT_HALF, 1) * sin_hi)
        return jnp.concatenate(outs, axis=1)

    q_ref[0] = (rope(proj(0)) * (DA_QK_DIM ** -0.5 * math.log2(math.e))).T.astype(BF16)
    k_ref[...] = rope(proj(1)).astype(BF16)
    v_ref[0] = proj(2).astype(BF16).T

    gq = proj(3)
    hq_ref[...] = gq * _sigmoid(gq)

    a = lbr_ref[...]
    amax = jnp.max(a, axis=0, keepdims=True)
    ea = jnp.exp(a - amax)
    lb = ea[0:1, :] / jnp.sum(ea, axis=0, keepdims=True)
    gf = proj(4)
    f = lb + (1.0 - lb) * _sigmoid(gf)
    lf_ref[...] = jnp.log(f)
    kf_ref[...] = 1.0 - f

    gi_ref[...] = proj(5).astype(BF16)
    gg = proj(6)
    sg_ref[...] = (gg * _sigmoid(gg)).astype(BF16)


def _inproj(x2, mod3, norm_w, w_bf, pos_col, invf, lb_raw, seq):
    t, d = x2.shape
    tm = ROW_TILE
    per_b = seq // tm
    row = lambda i: (i, 0)
    full = lambda i: (0, 0)
    out_bf = jax.ShapeDtypeStruct((t, STREAM_W), BF16)
    out_f = jax.ShapeDtypeStruct((t, STREAM_W), F32)
    out_t = jax.ShapeDtypeStruct((t // seq, STREAM_W, seq), BF16)
    stream = pl.BlockSpec((tm, STREAM_W), row)
    stream_t = pl.BlockSpec((1, STREAM_W, tm), lambda i: (i // per_b, 0, i % per_b))
    return pl.pallas_call(
        _inproj_body,
        grid=(t // tm,),
        in_specs=[pl.BlockSpec((tm, d), row),
                  pl.BlockSpec((1, 6, d), lambda i: (i // per_b, 0, 0)),
                  pl.BlockSpec((1, d), full),
                  pl.BlockSpec((d, N_STREAMS * STREAM_W), full),
                  pl.BlockSpec((tm, 1), row),
                  pl.BlockSpec((1, LANES), full),
                  pl.BlockSpec(lb_raw.shape, full)],
        out_specs=[stream_t, stream, stream_t] + [stream] * 5,
        out_shape=[out_t, out_bf, out_t, out_f, out_f, out_f, out_bf, out_bf],
        compiler_params=pltpu.CompilerParams(vmem_limit_bytes=56 * 1024 * 1024),
        name="inproj",
    )(x2, mod3, norm_w, w_bf, pos_col, invf, lb_raw)


ONES_ROWS = 16


def _attn_body(qt_ref, k_ref, vt_ref, lam_ref, sw_ref, o_ref, s_scr, m_scr, a_scr, *, tile, ktile):
    qi = pl.program_id(2)
    qt = qt_ref[0]
    feat = lax.broadcasted_iota(jnp.int32, (HEAD_W, 1), 0)
    zero = jnp.zeros_like(qt)
    qmaps = (jnp.where(feat < DA_QK_DIM, qt, zero), jnp.where(feat >= DA_QK_DIM, qt, zero))
    ones = jnp.ones((ONES_ROWS, ktile), BF16)

    m_scr[...] = jnp.full(m_scr.shape, NEG_BIG, F32)
    a_scr[...] = jnp.zeros(a_scr.shape, F32)

    ratio = tile // ktile
    n_full = qi * ratio

    def score_block(ki, diag_offset):
        start = pl.multiple_of(ki * ktile, ktile)
        kb = k_ref[0, pl.ds(start, ktile), :]
        for mp in range(2):
            s = jnp.dot(kb, qmaps[mp], preferred_element_type=F32)
            if diag_offset is not None:
                key = lax.broadcasted_iota(jnp.int32, s.shape, 0) + diag_offset
                qry = lax.broadcasted_iota(jnp.int32, s.shape, 1)
                s = jnp.where(key <= qry, s, NEG_BIG)
            grouped = s.reshape(ktile // SUBLANES, SUBLANES, tile)
            m_scr[mp] = jnp.maximum(m_scr[mp], jnp.max(grouped, axis=0))
            s_scr[mp, pl.ds(start, ktile), :] = s

    def value_block(ki, col_max):
        start = pl.multiple_of(ki * ktile, ktile)
        vb = jnp.concatenate([vt_ref[0, :, pl.ds(start, ktile)], ones], axis=0)
        for mp in range(2):
            p = jnp.exp2(s_scr[mp, pl.ds(start, ktile), :] - col_max[mp]).astype(BF16)
            a_scr[mp] += jnp.dot(vb, p, preferred_element_type=F32)

    def score_body(ki, carry):
        score_block(ki, None)
        return carry

    lax.fori_loop(0, n_full, score_body, 0)
    for j in range(ratio):
        score_block(n_full + j, j * ktile)
    col_max = [jnp.max(m_scr[mp], axis=0, keepdims=True) for mp in range(2)]

    def value_body(ki, carry):
        value_block(ki, col_max)
        return carry

    lax.fori_loop(0, n_full + ratio, value_body, 0)

    lp = lam_ref[...]
    lam = (jnp.exp(jnp.sum(lp[0:1] * lp[1:2], axis=-1, keepdims=True))
           - jnp.exp(jnp.sum(lp[2:3] * lp[3:4], axis=-1, keepdims=True)) + LAM_INIT)
    o = (a_scr[0, 0:HEAD_W, :] / a_scr[0, HEAD_W:HEAD_W + 1, :]
         - lam * (a_scr[1, 0:HEAD_W, :] / a_scr[1, HEAD_W:HEAD_W + 1, :]))
    o = o * lax.rsqrt(jnp.mean(o * o, axis=0, keepdims=True) + EPS) * sw_ref[...]
    o_ref[0] = (o * (1.0 - LAM_INIT)).T.astype(BF16)


def _attn(qt3, k3, vt3, lam_p, subln_col):
    bsz, seq, _ = k3.shape
    tile = ATTN_TILE
    return pl.pallas_call(
        functools.partial(_attn_body, tile=tile, ktile=ATTN_KEY_TILE),
        grid=(bsz, HEADS, seq // tile),
        in_specs=[pl.BlockSpec((1, HEAD_W, tile), lambda b, h, i: (b, h, i)),
                  pl.BlockSpec((1, seq, HEAD_W), lambda b, h, i: (b, 0, h)),
                  pl.BlockSpec((1, HEAD_W, seq), lambda b, h, i: (b, h, 0)),
                  pl.BlockSpec(lam_p.shape, lambda b, h, i: (0, 0)),
                  pl.BlockSpec((HEAD_W, 1), lambda b, h, i: (0, 0))],
        out_specs=pl.BlockSpec((1, tile, HEAD_W), lambda b, h, i: (b, i, h)),
        out_shape=jax.ShapeDtypeStruct((bsz, seq, STREAM_W), BF16),
        scratch_shapes=[pltpu.VMEM((2, seq, tile), F32),
                        pltpu.VMEM((2, SUBLANES, tile), F32),
                        pltpu.VMEM((2, HEAD_W + ONES_ROWS, tile), F32)],
        compiler_params=pltpu.CompilerParams(vmem_limit_bytes=56 * 1024 * 1024),
        name="diff_attn",
    )(qt3, k3, vt3, lam_p, subln_col)


def _hgrn_levels(block):
    return [block >> (i + 1) for i in range(block.bit_length() - 1)]


def _hgrn_constants(block):
    t = np.arange(block)[:, None]
    s = np.arange(block)[None, :]
    tril = (s <= t).astype(np.float32)
    lv = np.full((block, block), -1, np.int32)
    halves = _hgrn_levels(block)
    for li, m in enumerate(halves):
        same = (t // (2 * m)) == (s // (2 * m))
        lv[same & ((t & m) != 0) & ((s & m) == 0)] = li
    lv[np.arange(block), np.arange(block)] = len(halves)
    return jnp.asarray(tril, BF16), jnp.asarray(lv)


def _level_ref(b_ref, h, m, block):
    def bcast(r, n):
        return jnp.broadcast_to(b_ref[h, pl.ds(r, 1), :], (n, HEAD_W))

    if 2 * m >= SUBLANES:
        return jnp.concatenate([bcast(s0 + m - 1, 2 * m) for s0 in range(0, block, 2 * m)], axis=0)
    sub = lax.broadcasted_iota(jnp.int32, (SUBLANES, HEAD_W), 0)
    pieces = []
    for s0 in range(0, block, SUBLANES):
        piece = bcast(s0 + m - 1, SUBLANES)
        for j in range(1, SUBLANES // (2 * m)):
            piece = jnp.where(sub >= 2 * m * j, bcast(s0 + 2 * m * j + m - 1, SUBLANES), piece)
        pieces.append(piece)
    return jnp.concatenate(pieces, axis=0)


def _hgrn_body(hq_ref, lf_ref, kf_ref, gi_ref, sg_ref, nw_ref, tril_ref, lv_ref, o_ref,
               st_scr, b_scr, *, block):
    @pl.when(pl.program_id(1) == 0)
    def _():
        st_scr[...] = jnp.zeros(st_scr.shape, F32)

    tril = tril_ref[...]
    lv = lv_ref[...]
    row = lax.broadcasted_iota(jnp.int32, (block, 1), 0)
    halves = _hgrn_levels(block)

    for h in range(HEADS):
        cs = slice(h * HEAD_W, (h + 1) * HEAD_W)
        q = hq_ref[0, :, cs]
        lf = lf_ref[0, :, cs]
        k = kf_ref[0, :, cs]
        v = gi_ref[0, :, cs]

        hi = lf.astype(BF16)
        r1 = lf - hi.astype(F32)
        mid = r1.astype(BF16)
        lo = (r1 - mid.astype(F32)).astype(BF16)
        b = (jnp.dot(tril, hi, preferred_element_type=F32)
             + jnp.dot(tril, mid, preferred_element_type=F32)
             + jnp.dot(tril, lo, preferred_element_type=F32))
        b_scr[h] = b

        qb = q.astype(BF16)
        kb = k.astype(BF16)
        scores = jnp.where(lv == len(halves),
                           lax.dot_general(qb, kb, NT_DIMS, preferred_element_type=F32), 0.0)
        for li, m in enumerate(halves):
            ref = _level_ref(b_scr, h, m, block)
            e = jnp.exp(-jnp.abs(b - ref))
            xl = (jnp.where((row & m) != 0, q, k) * e).astype(BF16)
            p = lax.dot_general(xl, xl, NT_DIMS, preferred_element_type=F32)
            scores = jnp.where(lv == li, p, scores)
        o_intra = jnp.dot(scores.astype(BF16), v, preferred_element_type=F32)

        st = st_scr[h]
        o_inter = lax.dot_general((q * jnp.exp(b)).astype(BF16), st.astype(BF16), NT_DIMS,
                                  preferred_element_type=F32)
        b_last = b[block - 1:block, :]
        kdec = (k * jnp.exp(b_last - b)).astype(BF16)
        st_scr[h] = st * jnp.exp(b_last) + lax.dot_general(v, kdec, TN_DIMS,
                                                            preferred_element_type=F32)

        o = o_inter + o_intra
        o = o * lax.rsqrt(jnp.mean(o * o, axis=-1, keepdims=True) + EPS) * nw_ref[...]
        o_ref[0, :, cs] = (o * sg_ref[0, :, cs].astype(F32)).astype(BF16)


def _hgrn(hq3, lf3, kf3, gi3, sg3, norm_w):
    bsz, seq, _ = hq3.shape
    block = HGRN_BLOCK
    tril, lv = _hgrn_constants(block)
    blk = pl.BlockSpec((1, block, STREAM_W), lambda b, g: (b, g, 0))
    const = lambda b, g: (0, 0)
    return pl.pallas_call(
        functools.partial(_hgrn_body, block=block),
        grid=(bsz, seq // block),
        in_specs=[blk, blk, blk, blk, blk,
                  pl.BlockSpec((1, HEAD_W), const),
                  pl.BlockSpec((block, block), const),
                  pl.BlockSpec((block, block), const)],
        out_specs=blk,
        out_shape=jax.ShapeDtypeStruct((bsz, seq, STREAM_W), BF16),
        scratch_shapes=[pltpu.VMEM((HEADS, HEAD_W, HEAD_W), F32),
                        pltpu.VMEM((HEADS, block, HEAD_W), F32)],
        name="hgrn2",
    )(hq3, lf3, kf3, gi3, sg3, norm_w, tril, lv)


META_E0, META_E1, META_R0, META_R1, META_W0, META_W1 = range(6)
GROUP_LANE0 = N_EXPERTS


def _outproj_body(da_ref, hg_ref, x_ref, mod_ref, wo_ref, nw_ref, wr_ref, br_ref, stril_ref,
                  x1_ref, h2_ref, meta_ref, cnt_ref, carry_scr):
    @pl.when(pl.program_id(0) == 0)
    def _():
        carry_scr[...] = jnp.zeros(carry_scr.shape, F32)

    attn = (jnp.dot(da_ref[...], wo_ref[0:STREAM_W, :], preferred_element_type=F32)
            + jnp.dot(hg_ref[...], wo_ref[STREAM_W:, :], preferred_element_type=F32))
    gate1 = mod_ref[0, 2:3, :]
    shift2 = mod_ref[0, 3:4, :]
    scale2 = mod_ref[0, 4:5, :]
    x1 = x_ref[...] + gate1 * attn
    x1_ref[...] = x1
    h2 = (x1 * lax.rsqrt(jnp.mean(x1 * x1, axis=-1, keepdims=True) + EPS) * nw_ref[...]
          * (1.0 + scale2) + shift2)
    _slab_store(h2_ref, h2)

    logits = jnp.dot(h2, wr_ref[...], preferred_element_type=F32, precision=HIGHEST) + br_ref[...]
    lane = lax.broadcasted_iota(jnp.int32, logits.shape, 1)
    far = jnp.int32(LANES)

    def first_max(vals):
        mx = jnp.max(vals, axis=-1, keepdims=True)
        return mx, jnp.min(jnp.where(vals == mx, lane, far), axis=-1, keepdims=True)

    is_g = (lane >= GROUP_LANE0) & (lane < GROUP_LANE0 + N_GROUPS)
    gmax, glane = first_max(jnp.where(is_g, logits, NEG_BIG))
    g_w = 1.0 / jnp.sum(jnp.where(is_g, jnp.exp(logits - gmax), 0.0), axis=-1, keepdims=True)
    gidx = glane - GROUP_LANE0
    in_grp = (lane < N_EXPERTS) & ((lane // EXPERTS_PER_GROUP) == gidx)
    el = jnp.where(in_grp, logits, NEG_BIG)
    m1, i1 = first_max(el)
    m2, i2 = first_max(jnp.where(lane == i1, NEG_BIG, el))
    r = jnp.exp(m2 - m1)
    w0 = g_w / (1.0 + r)
    w1 = g_w * r / (1.0 + r)

    hot0 = lane == i1
    hot1 = lane == i2
    multi = jnp.where(hot0 | hot1, 1.0, 0.0)
    before = jnp.dot(stril_ref[...], multi.astype(BF16), preferred_element_type=F32) + carry_scr[...]
    rank0 = jnp.sum(jnp.where(hot0, before, 0.0), axis=-1, keepdims=True)
    rank1 = jnp.sum(jnp.where(hot1, before, 0.0), axis=-1, keepdims=True)
    carry = carry_scr[...] + jnp.sum(multi, axis=0, keepdims=True)
    carry_scr[...] = carry
    cnt_ref[...] = carry

    meta = jnp.zeros(logits.shape, F32)
    for idx, val in ((META_E0, i1.astype(F32)), (META_E1, i2.astype(F32)),
                     (META_R0, rank0), (META_R1, rank1), (META_W0, w0), (META_W1, w1)):
        meta = jnp.where(lane == idx, val, meta)
    meta_ref[...] = meta


def _outproj(da2, hg2, x2, mod3, wo_bf, norm_w, w_route, b_route, seq):
    t, d = x2.shape
    tm = ROW_TILE
    per_b = seq // tm
    row = lambda i: (i, 0)
    full = lambda i: (0, 0)
    stril = jnp.asarray(np.tril(np.ones((tm, tm), np.float32), -1), BF16)
    return pl.pallas_call(
        _outproj_body,
        grid=(t // tm,),
        in_specs=[pl.BlockSpec((tm, STREAM_W), row),
                  pl.BlockSpec((tm, STREAM_W), row),
                  pl.BlockSpec((tm, d), row),
                  pl.BlockSpec((1, 6, d), lambda i: (i // per_b, 0, 0)),
                  pl.BlockSpec((2 * STREAM_W, d), full),
                  pl.BlockSpec((1, d), full),
                  pl.BlockSpec((d, LANES), full),
                  pl.BlockSpec((1, LANES), full),
                  pl.BlockSpec((tm, tm), full)],
        out_specs=[pl.BlockSpec((tm, d), row),
                   pl.BlockSpec((tm * SLAB, LANES), row),
                   pl.BlockSpec((tm, LANES), row),
                   pl.BlockSpec((1, LANES), full)],
        out_shape=[jax.ShapeDtypeStruct((t, d), F32),
                   jax.ShapeDtypeStruct((t * SLAB, LANES), F32),
                   jax.ShapeDtypeStruct((t, LANES), F32),
                   jax.ShapeDtypeStruct((1, LANES), F32)],
        scratch_shapes=[pltpu.VMEM((1, LANES), F32)],
        compiler_params=pltpu.CompilerParams(dimension_semantics=("arbitrary",)),
        name="outproj_route",
    )(da2, hg2, x2, mod3, wo_bf, norm_w, w_route, b_route, stril)


DMA_UNROLL = 8


def _scatter_body(last_ref, pos_ref, h2_ref, xs_hbm, zero_scr, sem, zsem, *, tile):
    rows = tile * SLAB

    @pl.when(pl.program_id(0) == 0)
    def _():
        zero_scr[...] = jnp.zeros(zero_scr.shape, F32)

        def clear(tile_idx):
            start = pl.multiple_of(tile_idx * (MOE_TILE * SLAB), SLAB)
            return pltpu.make_async_copy(zero_scr, xs_hbm.at[pl.ds(start, MOE_TILE * SLAB)], zsem)

        for e in range(N_EXPERTS):
            @pl.when(last_ref[e] >= 0)
            def _():
                clear(last_ref[e]).start()
        for e in range(N_EXPERTS):
            @pl.when(last_ref[e] >= 0)
            def _():
                clear(last_ref[e]).wait()

        def clear_unused(tile_idx, carry):
            clear(tile_idx).start()
            clear(tile_idx).wait()
            return carry

        lax.fori_loop(last_ref[N_EXPERTS], xs_hbm.shape[0] // (MOE_TILE * SLAB), clear_unused, 0)

    def issue(t, carry):
        src = h2_ref.at[pl.ds(pl.multiple_of(t * SLAB, SLAB), SLAB)]
        for j in range(2):
            slot = pl.multiple_of(pos_ref[0, 0, 2 * t + j] * SLAB, SLAB)
            pltpu.make_async_copy(src, xs_hbm.at[pl.ds(slot, SLAB)], sem).start()
        return carry

    lax.fori_loop(0, tile, issue, 0, unroll=DMA_UNROLL)
    whole = pltpu.make_async_copy(h2_ref, xs_hbm.at[pl.ds(0, rows)], sem)
    whole.wait()
    whole.wait()


def _scatter(last_tile, pos3, h2s, n_slots):
    tile = GATHER_TILE
    rows = tile * SLAB
    grid_spec = pltpu.PrefetchScalarGridSpec(
        num_scalar_prefetch=1,
        grid=(h2s.shape[0] // rows,),
        in_specs=[pl.BlockSpec((1, 1, 2 * tile), lambda i, lt: (i, 0, 0), memory_space=pltpu.SMEM),
                  pl.BlockSpec((rows, LANES), lambda i, lt: (i, 0))],
        out_specs=pl.BlockSpec(memory_space=pl.ANY),
        scratch_shapes=[pltpu.VMEM((MOE_TILE * SLAB, LANES), F32),
                        pltpu.SemaphoreType.DMA, pltpu.SemaphoreType.DMA],
    )
    return pl.pallas_call(
        functools.partial(_scatter_body, tile=tile),
        grid_spec=grid_spec,
        out_shape=jax.ShapeDtypeStruct((n_slots * SLAB, LANES), F32),
        compiler_params=pltpu.CompilerParams(dimension_semantics=("arbitrary",)),
        name="moe_scatter",
    )(last_tile, pos3, h2s)


def _experts_body(te_ref, nv_ref, x_ref, wg_ref, wu_ref, wd_ref, y_ref, wgu_scr, wd_scr, *, tm):
    i = pl.program_id(0)
    e = te_ref[i]
    prev = te_ref[jnp.maximum(i - 1, 0)]

    @pl.when((i == 0) | (e != prev))
    def _():
        wgu_scr[:, 0:EXPERT_FF] = wg_ref[0].astype(BF16)
        wgu_scr[:, EXPERT_FF:] = wu_ref[0].astype(BF16)
        wd_scr[...] = wd_ref[0].astype(BF16)

    @pl.when(i < nv_ref[0])
    def _():
        x = _slab_load(x_ref, tm).astype(BF16)
        gu = jnp.dot(x, wgu_scr[...], preferred_element_type=F32)
        g = gu[:, 0:EXPERT_FF]
        act = (g * _sigmoid(g) * gu[:, EXPERT_FF:]).astype(BF16)
        _slab_store(y_ref, jnp.dot(act, wd_scr[...], preferred_element_type=F32))

    @pl.when(i >= nv_ref[0])
    def _():
        y_ref[...] = jnp.zeros(y_ref.shape, F32)


def _experts(tile_expert, n_valid, xs, w_gate, w_up, w_down):
    tm = MOE_TILE
    rows = tm * SLAB
    d = w_gate.shape[1]
    live = lambda i, te, nv: (jnp.minimum(i, nv[0] - 1), 0)
    grid_spec = pltpu.PrefetchScalarGridSpec(
        num_scalar_prefetch=2,
        grid=(xs.shape[0] // rows,),
        in_specs=[pl.BlockSpec((rows, LANES), live),
                  pl.BlockSpec((1, d, EXPERT_FF), lambda i, te, nv: (te[i], 0, 0)),
                  pl.BlockSpec((1, d, EXPERT_FF), lambda i, te, nv: (te[i], 0, 0)),
                  pl.BlockSpec((1, EXPERT_FF, d), lambda i, te, nv: (te[i], 0, 0))],
        out_specs=pl.BlockSpec((rows, LANES), lambda i, te, nv: (i, 0)),
        scratch_shapes=[pltpu.VMEM((d, 2 * EXPERT_FF), BF16),
                        pltpu.VMEM((EXPERT_FF, d), BF16)],
    )
    return pl.pallas_call(
        functools.partial(_experts_body, tm=tm),
        grid_spec=grid_spec,
        out_shape=jax.ShapeDtypeStruct(xs.shape, F32),
        compiler_params=pltpu.CompilerParams(dimension_semantics=("arbitrary",)),
        name="moe_experts",
    )(tile_expert, n_valid, xs, w_gate, w_up, w_down)


def _combine_body(pos_ref, nxt_ref, x1_ref, meta_ref, mod_ref, nw_ref, ys_hbm, o_ref, rows_scr, sems,
                  *, tile):
    i = pl.program_id(0)
    n = pl.num_programs(0)
    rows = tile * SLAB

    def fetch(idx_ref, buf):
        def issue(t, carry):
            dst = pl.ds(pl.multiple_of(t * SLAB, SLAB), SLAB)
            for j in range(2):
                slot = pl.multiple_of(idx_ref[0, 0, 2 * t + j] * SLAB, SLAB)
                pltpu.make_async_copy(ys_hbm.at[pl.ds(slot, SLAB)], rows_scr.at[buf, j, dst],
                                      sems.at[buf]).start()
            return carry
        lax.fori_loop(0, tile, issue, 0, unroll=DMA_UNROLL)

    @pl.when(i == 0)
    def _():
        fetch(pos_ref, 0)

    for cur in range(2):
        @pl.when((i % 2 == cur) & (i + 1 < n))
        def _():
            fetch(nxt_ref, 1 - cur)

    for cur in range(2):
        @pl.when(i % 2 == cur)
        def _():
            for j in range(2):
                pltpu.make_async_copy(ys_hbm.at[pl.ds(0, rows)], rows_scr.at[cur, j],
                                      sems.at[cur]).wait()
            meta = meta_ref[...]
            w0 = meta[:, META_W0:META_W0 + 1]
            w1 = meta[:, META_W1:META_W1 + 1]
            y = (w0 * _slab_load(rows_scr.at[cur, 0], tile)
                 + w1 * _slab_load(rows_scr.at[cur, 1], tile))
            x2 = x1_ref[...] + mod_ref[0, 5:6, :] * y
            o_ref[...] = (x2 * lax.rsqrt(jnp.mean(x2 * x2, axis=-1, keepdims=True) + EPS)
                          * nw_ref[...])


def _combine(pos3, x1, meta, mod3, norm_w, ys, seq):
    t, d = x1.shape
    tile = GATHER_TILE
    per_b = seq // tile
    n = t // tile
    row = lambda i: (i, 0)
    return pl.pallas_call(
        functools.partial(_combine_body, tile=tile),
        grid=(n,),
        in_specs=[pl.BlockSpec((1, 1, 2 * tile), lambda i: (i, 0, 0), memory_space=pltpu.SMEM),
                  pl.BlockSpec((1, 1, 2 * tile), lambda i: (jnp.minimum(i + 1, n - 1), 0, 0),
                               memory_space=pltpu.SMEM),
                  pl.BlockSpec((tile, d), row),
                  pl.BlockSpec((tile, LANES), row),
                  pl.BlockSpec((1, 6, d), lambda i: (i // per_b, 0, 0)),
                  pl.BlockSpec((1, d), lambda i: (0, 0)),
                  pl.BlockSpec(memory_space=pl.ANY)],
        out_specs=pl.BlockSpec((tile, d), row),
        out_shape=jax.ShapeDtypeStruct((t, d), F32),
        scratch_shapes=[pltpu.VMEM((2, 2, tile * SLAB, LANES), F32), pltpu.SemaphoreType.DMA((2,))],
        compiler_params=pltpu.CompilerParams(dimension_semantics=("arbitrary",)),
        name="moe_combine",
    )(pos3, pos3, x1, meta, mod3, norm_w, ys)


def _rope_inv_freq_lanes():
    inv_freq = ROPE_THETA ** (-jnp.arange(ROT_HALF, dtype=F32) / ROT_HALF)
    lane = np.arange(LANES) % DA_QK_DIM
    table = jnp.where(jnp.asarray(lane < ROT_DIM), inv_freq[lane % ROT_HALF], 0.0)
    return table.reshape(1, LANES).astype(F32)


def kernel(x, c, positions, norm1_w, norm2_w, final_norm_w, ada_w, ada_b, w_in, w_out, da_lambda_q1, da_lambda_k1, da_lambda_q2, da_lambda_k2, da_subln_w, hg_lower_bound, hg_norm_w, moe_w_group, moe_b_group, moe_w_router, moe_b_router, moe_w_gate, moe_w_up, moe_w_down):
    bsz, seq, d = x.shape
    assert d == D_MODEL and norm1_w.shape[0] == 1, "single-layer model of width 1024 only"
    assert seq % ATTN_TILE == 0 and seq % HGRN_BLOCK == 0 and seq % ROW_TILE == 0
    t = bsz * seq
    x2 = x.reshape(t, d)

    mod3 = _adaln(c, ada_w[0], ada_b).reshape(bsz, 6, d)

    pos_col = positions.astype(F32).reshape(t, 1)
    qt, k, vt, hq, lf, kf, gi, sg = _inproj(x2, mod3, norm1_w, w_in[0].astype(BF16), pos_col,
                                          _rope_inv_freq_lanes(), hg_lower_bound, seq)

    as3 = lambda a: a.reshape(bsz, seq, STREAM_W)
    lam_p = jnp.concatenate([da_lambda_q1, da_lambda_k1, da_lambda_q2, da_lambda_k2], axis=0)
    da = _attn(qt, as3(k), vt, lam_p, da_subln_w.reshape(HEAD_W, 1))
    hg = _hgrn(as3(hq), as3(lf), as3(kf), as3(gi), as3(sg), hg_norm_w)

    pad = jnp.zeros((d, LANES - N_EXPERTS - N_GROUPS), F32)
    w_route = jnp.concatenate([moe_w_router[0], moe_w_group[0], pad], axis=1)
    b_route = jnp.concatenate([moe_b_router[0], moe_b_group[0], pad[0]]).reshape(1, LANES)
    x1, h2, meta, cnt = _outproj(da.reshape(t, STREAM_W), hg.reshape(t, STREAM_W), x2, mod3,
                                 w_out[0].astype(BF16), norm2_w, w_route, b_route, seq)

    counts = cnt[0, :N_EXPERTS].astype(jnp.int32)
    tiles_e = (counts + MOE_TILE - 1) // MOE_TILE
    tile_end = jnp.cumsum(tiles_e)
    offs = (tile_end - tiles_e) * MOE_TILE
    ids = meta[:, META_E0:META_E1 + 1].astype(jnp.int32)
    ranks = meta[:, META_R0:META_R1 + 1].astype(jnp.int32)
    expert_iota = jnp.arange(N_EXPERTS, dtype=jnp.int32)
    pos = jnp.sum(jnp.where(ids[..., None] == expert_iota, offs, 0), axis=-1) + ranks
    n_tiles = (2 * t) // MOE_TILE + N_EXPERTS
    n_valid = tile_end[-1:]
    tile_ids = jnp.minimum(jnp.arange(n_tiles, dtype=jnp.int32), n_valid - 1)
    tile_expert = jnp.sum(tile_ids[:, None] >= tile_end[None, :], axis=1).astype(jnp.int32)
    pos3 = pos.reshape(t // GATHER_TILE, 1, 2 * GATHER_TILE)
    last_tile = jnp.concatenate([jnp.where(tiles_e > 0, tile_end - 1, -1), n_valid]).astype(jnp.int32)

    xs = _scatter(last_tile, pos3, h2, n_tiles * MOE_TILE)
    ys = _experts(tile_expert, n_valid.astype(jnp.int32), xs, moe_w_gate[0], moe_w_up[0], moe_w_down[0])
    out = _combine(pos3, x1, meta, mod3, final_norm_w.reshape(1, d), ys, seq)
    return out.reshape(bsz, seq, d)
```

```python
import functools
import math

import numpy as np
import jax
import jax.numpy as jnp
from jax import lax
from jax.experimental import pallas as pl
from jax.experimental.pallas import tpu as pltpu

F32 = jnp.float32
BF16 = jnp.bfloat16
HIGHEST = lax.Precision.HIGHEST

LANES = 128
SUBLANES = 8
D_MODEL = 1024
HEADS = 4
HEAD_W = 128
STREAM_W = HEADS * HEAD_W
N_STREAMS = 7
DA_QK_DIM = 64
ROPE_THETA = 500000.0
ROT_DIM = DA_QK_DIM // 4
ROT_HALF = ROT_DIM // 2
N_GROUPS = 4
EXPERTS_PER_GROUP = 8
N_EXPERTS = N_GROUPS * EXPERTS_PER_GROUP
EXPERT_FF = 256
EPS = 1e-6
LAM_INIT = 0.8 - 0.6 * math.exp(-0.3 * 0)
NEG_BIG = -1e30

ROW_TILE = 256
ATTN_TILE = 1024
ATTN_KEY_TILE = 512
HGRN_BLOCK = 256
MOE_TILE = 256
GATHER_TILE = 256

NT_DIMS = (((1,), (1,)), ((), ()))
TN_DIMS = (((0,), (0,)), ((), ()))


def _sigmoid(x):
    return 1.0 / (1.0 + jnp.exp(-x))


SLAB = D_MODEL // LANES


def _slab_store(ref, val):
    rows = val.shape[0]
    for c in range(SLAB):
        ref[pl.ds(c, rows, stride=SLAB), :] = val[:, c * LANES:(c + 1) * LANES]


def _slab_load(ref, rows):
    return jnp.concatenate([ref[pl.ds(c, rows, stride=SLAB), :] for c in range(SLAB)], axis=1)


def _adaln_body(c_ref, w_ref, b_ref, o_ref):
    c = c_ref[...]
    ca = c * _sigmoid(c)
    o_ref[...] = jnp.dot(ca, w_ref[...], preferred_element_type=F32, precision=HIGHEST) + b_ref[...]


def _adaln(c, ada_w, ada_b):
    bsz, d = c.shape
    n = ada_w.shape[1]
    tn = 1024
    return pl.pallas_call(
        _adaln_body,
        grid=(n // tn,),
        in_specs=[pl.BlockSpec((bsz, d), lambda j: (0, 0)),
                  pl.BlockSpec((d, tn), lambda j: (0, j)),
                  pl.BlockSpec((1, tn), lambda j: (0, j))],
        out_specs=pl.BlockSpec((bsz, tn), lambda j: (0, j)),
        out_shape=jax.ShapeDtypeStruct((bsz, n), F32),
        name="adaln",
    )(c, ada_w, ada_b)


def _inproj_body(x_ref, mod_ref, nw_ref, w_ref, pos_ref, invf_ref, lbr_ref,
                 q_ref, k_ref, v_ref, hq_ref, lf_ref, kf_ref, gi_ref, sg_ref):
    x = x_ref[...]
    ms = jnp.mean(x * x, axis=-1, keepdims=True)
    y = x * lax.rsqrt(ms + EPS) * nw_ref[...]
    shift = mod_ref[0, 0:1, :]
    scale = mod_ref[0, 1:2, :]
    h = (y * (1.0 + scale) + shift).astype(BF16)

    ang = pos_ref[...] * invf_ref[...]
    cosv = jnp.cos(ang)
    sinv = jnp.sin(ang)
    lane = lax.broadcasted_iota(jnp.int32, (1, LANES), 1) % DA_QK_DIM
    sin_lo = jnp.where(lane < ROT_HALF, -sinv, 0.0)
    sin_hi = jnp.where((lane >= ROT_HALF) & (lane < ROT_DIM), sinv, 0.0)

    def proj(j):
        return jnp.dot(h, w_ref[:, j * STREAM_W:(j + 1) * STREAM_W], preferred_element_type=F32)

    def rope(t):
        outs = []
        for hb in range(HEADS):
            tc = t[:, hb * HEAD_W:(hb + 1) * HEAD_W]
            outs.append(tc * cosv
                        + pltpu.roll(tc, LANES - ROT_HALF, 1) * sin_lo
                        + pltpu.roll(tc, ROT_HALF, 1) * sin_hi)
        return jnp.concatenate(outs, axis=1)

    q_ref[0] = (rope(proj(0)) * (DA_QK_DIM ** -0.5 * math.log2(math.e))).T.astype(BF16)
    k_ref[...] = rope(proj(1)).astype(BF16)
    v_ref[0] = proj(2).astype(BF16).T

    gq = proj(3)
    hq_ref[...] = gq * _sigmoid(gq)

    a = lbr_ref[...]
    amax = jnp.max(a, axis=0, keepdims=True)
    ea = jnp.exp(a - amax)
    lb = ea[0:1, :] / jnp.sum(ea, axis=0, keepdims=True)
    gf = proj(4)
    f = lb + (1.0 - lb) * _sigmoid(gf)
    lf_ref[...] = jnp.log(f)
    kf_ref[...] = 1.0 - f

    gi_ref[...] = proj(5).astype(BF16)
    gg = proj(6)
    sg_ref[...] = (gg * _sigmoid(gg)).astype(BF16)


def _inproj(x2, mod3, norm_w, w_bf, pos_col, invf, lb_raw, seq):
    t, d = x2.shape
    tm = ROW_TILE
    per_b = seq // tm
    row = lambda i: (i, 0)
    full = lambda i: (0, 0)
    out_bf = jax.ShapeDtypeStruct((t, STREAM_W), BF16)
    out_f = jax.ShapeDtypeStruct((t, STREAM_W), F32)
    out_t = jax.ShapeDtypeStruct((t // seq, STREAM_W, seq), BF16)
    stream = pl.BlockSpec((tm, STREAM_W), row)
    stream_t = pl.BlockSpec((1, STREAM_W, tm), lambda i: (i // per_b, 0, i % per_b))
    return pl.pallas_call(
        _inproj_body,
        grid=(t // tm,),
        in_specs=[pl.BlockSpec((tm, d), row),
                  pl.BlockSpec((1, 6, d), lambda i: (i // per_b, 0, 0)),
                  pl.BlockSpec((1, d), full),
                  pl.BlockSpec((d, N_STREAMS * STREAM_W), full),
                  pl.BlockSpec((tm, 1), row),
                  pl.BlockSpec((1, LANES), full),
                  pl.BlockSpec(lb_raw.shape, full)],
        out_specs=[stream_t, stream, stream_t] + [stream] * 5,
        out_shape=[out_t, out_bf, out_t, out_f, out_f, out_f, out_bf, out_bf],
        compiler_params=pltpu.CompilerParams(vmem_limit_bytes=56 * 1024 * 1024),
        name="inproj",
    )(x2, mod3, norm_w, w_bf, pos_col, invf, lb_raw)


ONES_ROWS = 16


def _attn_body(qt_ref, k_ref, vt_ref, lam_ref, sw_ref, o_ref, s_scr, m_scr, a_scr, *, tile, ktile):
    qi = pl.program_id(2)
    qt = qt_ref[0]
    feat = lax.broadcasted_iota(jnp.int32, (HEAD_W, 1), 0)
    zero = jnp.zeros_like(qt)
    qmaps = (jnp.where(feat < DA_QK_DIM, qt, zero), jnp.where(feat >= DA_QK_DIM, qt, zero))
    ones = jnp.ones((ONES_ROWS, ktile), BF16)

    m_scr[...] = jnp.full(m_scr.shape, NEG_BIG, F32)
    a_scr[...] = jnp.zeros(a_scr.shape, F32)

    ratio = tile // ktile
    n_full = qi * ratio

    def score_block(ki, diag_offset):
        start = pl.multiple_of(ki * ktile, ktile)
        kb = k_ref[0, pl.ds(start, ktile), :]
        for mp in range(2):
            s = jnp.dot(kb, qmaps[mp], preferred_element_type=F32)
            if diag_offset is not None:
                key = lax.broadcasted_iota(jnp.int32, s.shape, 0) + diag_offset
                qry = lax.broadcasted_iota(jnp.int32, s.shape, 1)
                s = jnp.where(key <= qry, s, NEG_BIG)
            grouped = s.reshape(ktile // SUBLANES, SUBLANES, tile)
            m_scr[mp] = jnp.maximum(m_scr[mp], jnp.max(grouped, axis=0))
            s_scr[mp, pl.ds(start, ktile), :] = s

    def value_block(ki, col_max):
        start = pl.multiple_of(ki * ktile, ktile)
        vb = jnp.concatenate([vt_ref[0, :, pl.ds(start, ktile)], ones], axis=0)
        for mp in range(2):
            p = jnp.exp2(s_scr[mp, pl.ds(start, ktile), :] - col_max[mp]).astype(BF16)
            a_scr[mp] += jnp.dot(vb, p, preferred_element_type=F32)

    def score_body(ki, carry):
        score_block(ki, None)
        return carry

    lax.fori_loop(0, n_full, score_body, 0)
    for j in range(ratio):
        score_block(n_full + j, j * ktile)
    col_max = [jnp.max(m_scr[mp], axis=0, keepdims=True) for mp in range(2)]

    def value_body(ki, carry):
        value_block(ki, col_max)
        return carry

    lax.fori_loop(0, n_full + ratio, value_body, 0)

    lp = lam_ref[...]
    lam = (jnp.exp(jnp.sum(lp[0:1] * lp[1:2], axis=-1, keepdims=True))
           - jnp.exp(jnp.sum(lp[2:3] * lp[3:4], axis=-1, keepdims=True)) + LAM_INIT)
    o = (a_scr[0, 0:HEAD_W, :] / a_scr[0, HEAD_W:HEAD_W + 1, :]
         - lam * (a_scr[1, 0:HEAD_W, :] / a_scr[1, HEAD_W:HEAD_W + 1, :]))
    o = o * lax.rsqrt(jnp.mean(o * o, axis=0, keepdims=True) + EPS) * sw_ref[...]
    o_ref[0] = (o * (1.0 - LAM_INIT)).T.astype(BF16)


def _attn(qt3, k3, vt3, lam_p, subln_col):
    bsz, seq, _ = k3.shape
    tile = ATTN_TILE
    return pl.pallas_call(
        functools.partial(_attn_body, tile=tile, ktile=ATTN_KEY_TILE),
        grid=(bsz, HEADS, seq // tile),
        in_specs=[pl.BlockSpec((1, HEAD_W, tile), lambda b, h, i: (b, h, i)),
                  pl.BlockSpec((1, seq, HEAD_W), lambda b, h, i: (b, 0, h)),
                  pl.BlockSpec((1, HEAD_W, seq), lambda b, h, i: (b, h, 0)),
                  pl.BlockSpec(lam_p.shape, lambda b, h, i: (0, 0)),
                  pl.BlockSpec((HEAD_W, 1), lambda b, h, i: (0, 0))],
        out_specs=pl.BlockSpec((1, tile, HEAD_W), lambda b, h, i: (b, i, h)),
        out_shape=jax.ShapeDtypeStruct((bsz, seq, STREAM_W), BF16),
        scratch_shapes=[pltpu.VMEM((2, seq, tile), F32),
                        pltpu.VMEM((2, SUBLANES, tile), F32),
                        pltpu.VMEM((2, HEAD_W + ONES_ROWS, tile), F32)],
        compiler_params=pltpu.CompilerParams(vmem_limit_bytes=56 * 1024 * 1024),
        name="diff_attn",
    )(qt3, k3, vt3, lam_p, subln_col)


def _hgrn_levels(block):
    return [block >> (i + 1) for i in range(block.bit_length() - 1)]


def _hgrn_constants(block):
    t = np.arange(block)[:, None]
    s = np.arange(block)[None, :]
    tril = (s <= t).astype(np.float32)
    lv = np.full((block, block), -1, np.int32)
    halves = _hgrn_levels(block)
    for li, m in enumerate(halves):
        same = (t // (2 * m)) == (s // (2 * m))
        lv[same & ((t & m) != 0) & ((s & m) == 0)] = li
    lv[np.arange(block), np.arange(block)] = len(halves)
    return jnp.asarray(tril, BF16), jnp.asarray(lv)


def _level_ref(b_ref, h, m, block):
    def bcast(r, n):
        return jnp.broadcast_to(b_ref[h, pl.ds(r, 1), :], (n, HEAD_W))

    if 2 * m >= SUBLANES:
        return jnp.concatenate([bcast(s0 + m - 1, 2 * m) for s0 in range(0, block, 2 * m)], axis=0)
    sub = lax.broadcasted_iota(jnp.int32, (SUBLANES, HEAD_W), 0)
    pieces = []
    for s0 in range(0, block, SUBLANES):
        piece = bcast(s0 + m - 1, SUBLANES)
        for j in range(1, SUBLANES // (2 * m)):
            piece = jnp.where(sub >= 2 * m * j, bcast(s0 + 2 * m * j + m - 1, SUBLANES), piece)
        pieces.append(piece)
    return jnp.concatenate(pieces, axis=0)


def _hgrn_body(hq_ref, lf_ref, kf_ref, gi_ref, sg_ref, nw_ref, tril_ref, lv_ref, o_ref,
               st_scr, b_scr, *, block):
    @pl.when(pl.program_id(1) == 0)
    def _():
        st_scr[...] = jnp.zeros(st_scr.shape, F32)

    tril = tril_ref[...]
    lv = lv_ref[...]
    row = lax.broadcasted_iota(jnp.int32, (block, 1), 0)
    halves = _hgrn_levels(block)

    for h in range(HEADS):
        cs = slice(h * HEAD_W, (h + 1) * HEAD_W)
        q = hq_ref[0, :, cs]
        lf = lf_ref[0, :, cs]
        k = kf_ref[0, :, cs]
        v = gi_ref[0, :, cs]

        hi = lf.astype(BF16)
        r1 = lf - hi.astype(F32)
        mid = r1.astype(BF16)
        lo = (r1 - mid.astype(F32)).astype(BF16)
        b = (jnp.dot(tril, hi, preferred_element_type=F32)
             + jnp.dot(tril, mid, preferred_element_type=F32)
             + jnp.dot(tril, lo, preferred_element_type=F32))
        b_scr[h] = b

        qb = q.astype(BF16)
        kb = k.astype(BF16)
        scores = jnp.where(lv == len(halves),
                           lax.dot_general(qb, kb, NT_DIMS, preferred_element_type=F32), 0.0)
        for li, m in enumerate(halves):
            ref = _level_ref(b_scr, h, m, block)
            e = jnp.exp(-jnp.abs(b - ref))
            xl = (jnp.where((row & m) != 0, q, k) * e).astype(BF16)
            p = lax.dot_general(xl, xl, NT_DIMS, preferred_element_type=F32)
            scores = jnp.where(lv == li, p, scores)
        o_intra = jnp.dot(scores.astype(BF16), v, preferred_element_type=F32)

        st = st_scr[h]
        o_inter = lax.dot_general((q * jnp.exp(b)).astype(BF16), st.astype(BF16), NT_DIMS,
                                  preferred_element_type=F32)
        b_last = b[block - 1:block, :]
        kdec = (k * jnp.exp(b_last - b)).astype(BF16)
        st_scr[h] = st * jnp.exp(b_last) + lax.dot_general(v, kdec, TN_DIMS,
                                                            preferred_element_type=F32)

        o = o_inter + o_intra
        o = o * lax.rsqrt(jnp.mean(o * o, axis=-1, keepdims=True) + EPS) * nw_ref[...]
        o_ref[0, :, cs] = (o * sg_ref[0, :, cs].astype(F32)).astype(BF16)


def _hgrn(hq3, lf3, kf3, gi3, sg3, norm_w):
    bsz, seq, _ = hq3.shape
    block = HGRN_BLOCK
    tril, lv = _hgrn_constants(block)
    blk = pl.BlockSpec((1, block, STREAM_W), lambda b, g: (b, g, 0))
    const = lambda b, g: (0, 0)
    return pl.pallas_call(
        functools.partial(_hgrn_body, block=block),
        grid=(bsz, seq // block),
        in_specs=[blk, blk, blk, blk, blk,
                  pl.BlockSpec((1, HEAD_W), const),
                  pl.BlockSpec((block, block), const),
                  pl.BlockSpec((block, block), const)],
        out_specs=blk,
        out_shape=jax.ShapeDtypeStruct((bsz, seq, STREAM_W), BF16),
        scratch_shapes=[pltpu.VMEM((HEADS, HEAD_W, HEAD_W), F32),
                        pltpu.VMEM((HEADS, block, HEAD_W), F32)],
        name="hgrn2",
    )(hq3, lf3, kf3, gi3, sg3, norm_w, tril, lv)


META_E0, META_E1, META_R0, META_R1, META_W0, META_W1 = range(6)
GROUP_LANE0 = N_EXPERTS


def _outproj_body(da_ref, hg_ref, x_ref, mod_ref, wo_ref, nw_ref, wr_ref, br_ref, stril_ref,
                  x1_ref, h2_ref, meta_ref, cnt_ref, carry_scr):
    @pl.when(pl.program_id(0) == 0)
    def _():
        carry_scr[...] = jnp.zeros(carry_scr.shape, F32)

    attn = (jnp.dot(da_ref[...], wo_ref[0:STREAM_W, :], preferred_element_type=F32)
            + jnp.dot(hg_ref[...], wo_ref[STREAM_W:, :], preferred_element_type=F32))
    gate1 = mod_ref[0, 2:3, :]
    shift2 = mod_ref[0, 3:4, :]
    scale2 = mod_ref[0, 4:5, :]
    x1 = x_ref[...] + gate1 * attn
    x1_ref[...] = x1
    h2 = (x1 * lax.rsqrt(jnp.mean(x1 * x1, axis=-1, keepdims=True) + EPS) * nw_ref[...]
          * (1.0 + scale2) + shift2)
    _slab_store(h2_ref, h2)

    logits = jnp.dot(h2, wr_ref[...], preferred_element_type=F32, precision=HIGHEST) + br_ref[...]
    lane = lax.broadcasted_iota(jnp.int32, logits.shape, 1)
    far = jnp.int32(LANES)

    def first_max(vals):
        mx = jnp.max(vals, axis=-1, keepdims=True)
        return mx, jnp.min(jnp.where(vals == mx, lane, far), axis=-1, keepdims=True)

    is_g = (lane >= GROUP_LANE0) & (lane < GROUP_LANE0 + N_GROUPS)
    gmax, glane = first_max(jnp.where(is_g, logits, NEG_BIG))
    g_w = 1.0 / jnp.sum(jnp.where(is_g, jnp.exp(logits - gmax), 0.0), axis=-1, keepdims=True)
    gidx = glane - GROUP_LANE0
    in_grp = (lane < N_EXPERTS) & ((lane // EXPERTS_PER_GROUP) == gidx)
    el = jnp.where(in_grp, logits, NEG_BIG)
    m1, i1 = first_max(el)
    m2, i2 = first_max(jnp.where(lane == i1, NEG_BIG, el))
    r = jnp.exp(m2 - m1)
    w0 = g_w / (1.0 + r)
    w1 = g_w * r / (1.0 + r)

    hot0 = lane == i1
    hot1 = lane == i2
    multi = jnp.where(hot0 | hot1, 1.0, 0.0)
    before = jnp.dot(stril_ref[...], multi.astype(BF16), preferred_element_type=F32) + carry_scr[...]
    rank0 = jnp.sum(jnp.where(hot0, before, 0.0), axis=-1, keepdims=True)
    rank1 = jnp.sum(jnp.where(hot1, before, 0.0), axis=-1, keepdims=True)
    carry = carry_scr[...] + jnp.sum(multi, axis=0, keepdims=True)
    carry_scr[...] = carry
    cnt_ref[...] = carry

    meta = jnp.zeros(logits.shape, F32)
    for idx, val in ((META_E0, i1.astype(F32)), (META_E1, i2.astype(F32)),
                     (META_R0, rank0), (META_R1, rank1), (META_W0, w0), (META_W1, w1)):
        meta = jnp.where(lane == idx, val, meta)
    meta_ref[...] = meta


def _outproj(da2, hg2, x2, mod3, wo_bf, norm_w, w_route, b_route, seq):
    t, d = x2.shape
    tm = ROW_TILE
    per_b = seq // tm
    row = lambda i: (i, 0)
    full = lambda i: (0, 0)
    stril = jnp.asarray(np.tril(np.ones((tm, tm), np.float32), -1), BF16)
    return pl.pallas_call(
        _outproj_body,
        grid=(t // tm,),
        in_specs=[pl.BlockSpec((tm, STREAM_W), row),
                  pl.BlockSpec((tm, STREAM_W), row),
                  pl.BlockSpec((tm, d), row),
                  pl.BlockSpec((1, 6, d), lambda i: (i // per_b, 0, 0)),
                  pl.BlockSpec((2 * STREAM_W, d), full),
                  pl.BlockSpec((1, d), full),
                  pl.BlockSpec((d, LANES), full),
                  pl.BlockSpec((1, LANES), full),
                  pl.BlockSpec((tm, tm), full)],
        out_specs=[pl.BlockSpec((tm, d), row),
                   pl.BlockSpec((tm * SLAB, LANES), row),
                   pl.BlockSpec((tm, LANES), row),
                   pl.BlockSpec((1, LANES), full)],
        out_shape=[jax.ShapeDtypeStruct((t, d), F32),
                   jax.ShapeDtypeStruct((t * SLAB, LANES), F32),
                   jax.ShapeDtypeStruct((t, LANES), F32),
                   jax.ShapeDtypeStruct((1, LANES), F32)],
        scratch_shapes=[pltpu.VMEM((1, LANES), F32)],
        compiler_params=pltpu.CompilerParams(dimension_semantics=("arbitrary",)),
        name="outproj_route",
    )(da2, hg2, x2, mod3, wo_bf, norm_w, w_route, b_route, stril)


DMA_UNROLL = 8


def _scatter_body(last_ref, pos_ref, h2_ref, xs_hbm, zero_scr, sem, zsem, *, tile):
    rows = tile * SLAB

    @pl.when(pl.program_id(0) == 0)
    def _():
        zero_scr[...] = jnp.zeros(zero_scr.shape, F32)

        def clear(tile_idx):
            start = pl.multiple_of(tile_idx * (MOE_TILE * SLAB), SLAB)
            return pltpu.make_async_copy(zero_scr, xs_hbm.at[pl.ds(start, MOE_TILE * SLAB)], zsem)

        for e in range(N_EXPERTS):
            @pl.when(last_ref[e] >= 0)
            def _():
                clear(last_ref[e]).start()
        for e in range(N_EXPERTS):
            @pl.when(last_ref[e] >= 0)
            def _():
                clear(last_ref[e]).wait()

        def clear_unused(tile_idx, carry):
            clear(tile_idx).start()
            clear(tile_idx).wait()
            return carry

        lax.fori_loop(last_ref[N_EXPERTS], xs_hbm.shape[0] // (MOE_TILE * SLAB), clear_unused, 0)

    def issue(t, carry):
        src = h2_ref.at[pl.ds(pl.multiple_of(t * SLAB, SLAB), SLAB)]
        for j in range(2):
            slot = pl.multiple_of(pos_ref[0, 0, 2 * t + j] * SLAB, SLAB)
            pltpu.make_async_copy(src, xs_hbm.at[pl.ds(slot, SLAB)], sem).start(priority=j)
        return carry

    lax.fori_loop(0, tile, issue, 0, unroll=DMA_UNROLL)
    whole = pltpu.make_async_copy(h2_ref, xs_hbm.at[pl.ds(0, rows)], sem)
    whole.wait()
    whole.wait()


def _scatter(last_tile, pos3, h2s, n_slots):
    tile = GATHER_TILE
    rows = tile * SLAB
    grid_spec = pltpu.PrefetchScalarGridSpec(
        num_scalar_prefetch=1,
        grid=(h2s.shape[0] // rows,),
        in_specs=[pl.BlockSpec((1, 1, 2 * tile), lambda i, lt: (i, 0, 0), memory_space=pltpu.SMEM),
                  pl.BlockSpec((rows, LANES), lambda i, lt: (i, 0))],
        out_specs=pl.BlockSpec(memory_space=pl.ANY),
        scratch_shapes=[pltpu.VMEM((MOE_TILE * SLAB, LANES), F32),
                        pltpu.SemaphoreType.DMA, pltpu.SemaphoreType.DMA],
    )
    return pl.pallas_call(
        functools.partial(_scatter_body, tile=tile),
        grid_spec=grid_spec,
        out_shape=jax.ShapeDtypeStruct((n_slots * SLAB, LANES), F32),
        compiler_params=pltpu.CompilerParams(dimension_semantics=("arbitrary",)),
        name="moe_scatter",
    )(last_tile, pos3, h2s)


def _experts_body(te_ref, nv_ref, x_ref, wg_ref, wu_ref, wd_ref, y_ref, wgu_scr, wd_scr, *, tm):
    i = pl.program_id(0)
    e = te_ref[i]
    prev = te_ref[jnp.maximum(i - 1, 0)]

    @pl.when((i == 0) | (e != prev))
    def _():
        wgu_scr[:, 0:EXPERT_FF] = wg_ref[0].astype(BF16)
        wgu_scr[:, EXPERT_FF:] = wu_ref[0].astype(BF16)
        wd_scr[...] = wd_ref[0].astype(BF16)

    @pl.when(i < nv_ref[0])
    def _():
        x = _slab_load(x_ref, tm).astype(BF16)
        gu = jnp.dot(x, wgu_scr[...], preferred_element_type=F32)
        g = gu[:, 0:EXPERT_FF]
        act = (g * _sigmoid(g) * gu[:, EXPERT_FF:]).astype(BF16)
        _slab_store(y_ref, jnp.dot(act, wd_scr[...], preferred_element_type=F32))

    @pl.when(i >= nv_ref[0])
    def _():
        y_ref[...] = jnp.zeros(y_ref.shape, F32)


def _experts(tile_expert, n_valid, xs, w_gate, w_up, w_down):
    tm = MOE_TILE
    rows = tm * SLAB
    d = w_gate.shape[1]
    live = lambda i, te, nv: (jnp.minimum(i, nv[0] - 1), 0)
    grid_spec = pltpu.PrefetchScalarGridSpec(
        num_scalar_prefetch=2,
        grid=(xs.shape[0] // rows,),
        in_specs=[pl.BlockSpec((rows, LANES), live),
                  pl.BlockSpec((1, d, EXPERT_FF), lambda i, te, nv: (te[i], 0, 0)),
                  pl.BlockSpec((1, d, EXPERT_FF), lambda i, te, nv: (te[i], 0, 0)),
                  pl.BlockSpec((1, EXPERT_FF, d), lambda i, te, nv: (te[i], 0, 0))],
        out_specs=pl.BlockSpec((rows, LANES), lambda i, te, nv: (i, 0)),
        scratch_shapes=[pltpu.VMEM((d, 2 * EXPERT_FF), BF16),
                        pltpu.VMEM((EXPERT_FF, d), BF16)],
    )
    return pl.pallas_call(
        functools.partial(_experts_body, tm=tm),
        grid_spec=grid_spec,
        out_shape=jax.ShapeDtypeStruct(xs.shape, F32),
        compiler_params=pltpu.CompilerParams(dimension_semantics=("arbitrary",)),
        name="moe_experts",
    )(tile_expert, n_valid, xs, w_gate, w_up, w_down)


def _combine_body(pos_ref, nxt_ref, x1_ref, meta_ref, mod_ref, nw_ref, ys_hbm, o_ref, rows_scr, sems,
                  *, tile):
    i = pl.program_id(0)
    n = pl.num_programs(0)
    rows = tile * SLAB

    def fetch(idx_ref, buf):
        def issue(t, carry):
            dst = pl.ds(pl.multiple_of(t * SLAB, SLAB), SLAB)
            for j in range(2):
                slot = pl.multiple_of(idx_ref[0, 0, 2 * t + j] * SLAB, SLAB)
                pltpu.make_async_copy(ys_hbm.at[pl.ds(slot, SLAB)], rows_scr.at[buf, j, dst],
                                      sems.at[buf]).start(priority=j)
            return carry
        lax.fori_loop(0, tile, issue, 0, unroll=DMA_UNROLL)

    @pl.when(i == 0)
    def _():
        fetch(pos_ref, 0)

    for cur in range(2):
        @pl.when((i % 2 == cur) & (i + 1 < n))
        def _():
            fetch(nxt_ref, 1 - cur)

    for cur in range(2):
        @pl.when(i % 2 == cur)
        def _():
            for j in range(2):
                pltpu.make_async_copy(ys_hbm.at[pl.ds(0, rows)], rows_scr.at[cur, j],
                                      sems.at[cur]).wait()
            meta = meta_ref[...]
            w0 = meta[:, META_W0:META_W0 + 1]
            w1 = meta[:, META_W1:META_W1 + 1]
            y = (w0 * _slab_load(rows_scr.at[cur, 0], tile)
                 + w1 * _slab_load(rows_scr.at[cur, 1], tile))
            x2 = x1_ref[...] + mod_ref[0, 5:6, :] * y
            o_ref[...] = (x2 * lax.rsqrt(jnp.mean(x2 * x2, axis=-1, keepdims=True) + EPS)
                          * nw_ref[...])


def _combine(pos3, x1, meta, mod3, norm_w, ys, seq):
    t, d = x1.shape
    tile = GATHER_TILE
    per_b = seq // tile
    n = t // tile
    row = lambda i: (i, 0)
    return pl.pallas_call(
        functools.partial(_combine_body, tile=tile),
        grid=(n,),
        in_specs=[pl.BlockSpec((1, 1, 2 * tile), lambda i: (i, 0, 0), memory_space=pltpu.SMEM),
                  pl.BlockSpec((1, 1, 2 * tile), lambda i: (jnp.minimum(i + 1, n - 1), 0, 0),
                               memory_space=pltpu.SMEM),
                  pl.BlockSpec((tile, d), row),
                  pl.BlockSpec((tile, LANES), row),
                  pl.BlockSpec((1, 6, d), lambda i: (i // per_b, 0, 0)),
                  pl.BlockSpec((1, d), lambda i: (0, 0)),
                  pl.BlockSpec(memory_space=pl.ANY)],
        out_specs=pl.BlockSpec((tile, d), row),
        out_shape=jax.ShapeDtypeStruct((t, d), F32),
        scratch_shapes=[pltpu.VMEM((2, 2, tile * SLAB, LANES), F32), pltpu.SemaphoreType.DMA((2,))],
        compiler_params=pltpu.CompilerParams(dimension_semantics=("arbitrary",)),
        name="moe_combine",
    )(pos3, pos3, x1, meta, mod3, norm_w, ys)


def _rope_inv_freq_lanes():
    inv_freq = ROPE_THETA ** (-jnp.arange(ROT_HALF, dtype=F32) / ROT_HALF)
    lane = np.arange(LANES) % DA_QK_DIM
    table = jnp.where(jnp.asarray(lane < ROT_DIM), inv_freq[lane % ROT_HALF], 0.0)
    return table.reshape(1, LANES).astype(F32)


def kernel(x, c, positions, norm1_w, norm2_w, final_norm_w, ada_w, ada_b, w_in, w_out, da_lambda_q1, da_lambda_k1, da_lambda_q2, da_lambda_k2, da_subln_w, hg_lower_bound, hg_norm_w, moe_w_group, moe_b_group, moe_w_router, moe_b_router, moe_w_gate, moe_w_up, moe_w_down):
    bsz, seq, d = x.shape
    assert d == D_MODEL and norm1_w.shape[0] == 1, "single-layer model of width 1024 only"
    assert seq % ATTN_TILE == 0 and seq % HGRN_BLOCK == 0 and seq % ROW_TILE == 0
    t = bsz * seq
    x2 = x.reshape(t, d)

    mod3 = _adaln(c, ada_w[0], ada_b).reshape(bsz, 6, d)

    pos_col = positions.astype(F32).reshape(t, 1)
    qt, k, vt, hq, lf, kf, gi, sg = _inproj(x2, mod3, norm1_w, w_in[0].astype(BF16), pos_col,
                                          _rope_inv_freq_lanes(), hg_lower_bound, seq)

    as3 = lambda a: a.reshape(bsz, seq, STREAM_W)
    lam_p = jnp.concatenate([da_lambda_q1, da_lambda_k1, da_lambda_q2, da_lambda_k2], axis=0)
    da = _attn(qt, as3(k), vt, lam_p, da_subln_w.reshape(HEAD_W, 1))
    hg = _hgrn(as3(hq), as3(lf), as3(kf), as3(gi), as3(sg), hg_norm_w)

    pad = jnp.zeros((d, LANES - N_EXPERTS - N_GROUPS), F32)
    w_route = jnp.concatenate([moe_w_router[0], moe_w_group[0], pad], axis=1)
    b_route = jnp.concatenate([moe_b_router[0], moe_b_group[0], pad[0]]).reshape(1, LANES)
    x1, h2, meta, cnt = _outproj(da.reshape(t, STREAM_W), hg.reshape(t, STREAM_W), x2, mod3,
                                 w_out[0].astype(BF16), norm2_w, w_route, b_route, seq)

    counts = cnt[0, :N_EXPERTS].astype(jnp.int32)
    tiles_e = (counts + MOE_TILE - 1) // MOE_TILE
    tile_end = jnp.cumsum(tiles_e)
    offs = (tile_end - tiles_e) * MOE_TILE
    ids = meta[:, META_E0:META_E1 + 1].astype(jnp.int32)
    ranks = meta[:, META_R0:META_R1 + 1].astype(jnp.int32)
    expert_iota = jnp.arange(N_EXPERTS, dtype=jnp.int32)
    pos = jnp.sum(jnp.where(ids[..., None] == expert_iota, offs, 0), axis=-1) + ranks
    n_tiles = (2 * t) // MOE_TILE + N_EXPERTS
    n_valid = tile_end[-1:]
    tile_ids = jnp.minimum(jnp.arange(n_tiles, dtype=jnp.int32), n_valid - 1)
    tile_expert = jnp.sum(tile_ids[:, None] >= tile_end[None, :], axis=1).astype(jnp.int32)
    pos3 = pos.reshape(t // GATHER_TILE, 1, 2 * GATHER_TILE)
    last_tile = jnp.concatenate([jnp.where(tiles_e > 0, tile_end - 1, -1), n_valid]).astype(jnp.int32)

    xs = _scatter(last_tile, pos3, h2, n_tiles * MOE_TILE)
    ys = _experts(tile_expert, n_valid.astype(jnp.int32), xs, moe_w_gate[0], moe_w_up[0], moe_w_down[0])
    out = _combine(pos3, x1, meta, mod3, final_norm_w.reshape(1, d), ys, seq)
    return out.reshape(bsz, seq, d)
```

```python
import functools
import math

import numpy as np
import jax
import jax.numpy as jnp
from jax import lax
from jax.experimental import pallas as pl
from jax.experimental.pallas import tpu as pltpu

F32 = jnp.float32
BF16 = jnp.bfloat16
HIGHEST = lax.Precision.HIGHEST

LANES = 128
SUBLANES = 8
D_MODEL = 1024
HEADS = 4
HEAD_W = 128
STREAM_W = HEADS * HEAD_W
N_STREAMS = 7
DA_QK_DIM = 64
ROPE_THETA = 500000.0
ROT_DIM = DA_QK_DIM // 4
ROT_HALF = ROT_DIM // 2
N_GROUPS = 4
EXPERTS_PER_GROUP = 8
N_EXPERTS = N_GROUPS * EXPERTS_PER_GROUP
EXPERT_FF = 256
EPS = 1e-6
LAM_INIT = 0.8 - 0.6 * math.exp(-0.3 * 0)
NEG_BIG = -1e30

ROW_TILE = 256
OUT_TILE = 512
ATTN_TILE = 1024
ATTN_KEY_TILE = 512
HGRN_BLOCK = 256
MOE_TILE = 256
GATHER_TILE = 256

NT_DIMS = (((1,), (1,)), ((), ()))
TN_DIMS = (((0,), (0,)), ((), ()))


def _sigmoid(x):
    return 1.0 / (1.0 + jnp.exp(-x))


SLAB = D_MODEL // LANES


def _slab_store(ref, val):
    rows = val.shape[0]
    for c in range(SLAB):
        ref[pl.ds(c, rows, stride=SLAB), :] = val[:, c * LANES:(c + 1) * LANES]


def _slab_load(ref, rows):
    return jnp.concatenate([ref[pl.ds(c, rows, stride=SLAB), :] for c in range(SLAB)], axis=1)


def _adaln_body(c_ref, w_ref, b_ref, o_ref):
    c = c_ref[...]
    ca = c * _sigmoid(c)
    o_ref[...] = jnp.dot(ca, w_ref[...], preferred_element_type=F32, precision=HIGHEST) + b_ref[...]


def _adaln(c, ada_w, ada_b):
    bsz, d = c.shape
    n = ada_w.shape[1]
    tn = 1024
    return pl.pallas_call(
        _adaln_body,
        grid=(n // tn,),
        in_specs=[pl.BlockSpec((bsz, d), lambda j: (0, 0)),
                  pl.BlockSpec((d, tn), lambda j: (0, j)),
                  pl.BlockSpec((1, tn), lambda j: (0, j))],
        out_specs=pl.BlockSpec((bsz, tn), lambda j: (0, j)),
        out_shape=jax.ShapeDtypeStruct((bsz, n), F32),
        name="adaln",
    )(c, ada_w, ada_b)


def _inproj_body(x_ref, mod_ref, nw_ref, w_ref, pos_ref, invf_ref, lbr_ref,
                 q_ref, k_ref, v_ref, hq_ref, lf_ref, kf_ref, gi_ref, sg_ref):
    x = x_ref[...]
    ms = jnp.mean(x * x, axis=-1, keepdims=True)
    y = x * lax.rsqrt(ms + EPS) * nw_ref[...]
    shift = mod_ref[0, 0:1, :]
    scale = mod_ref[0, 1:2, :]
    h = (y * (1.0 + scale) + shift).astype(BF16)

    ang = pos_ref[...] * invf_ref[...]
    cosv = jnp.cos(ang)
    sinv = jnp.sin(ang)
    lane = lax.broadcasted_iota(jnp.int32, (1, LANES), 1) % DA_QK_DIM
    sin_lo = jnp.where(lane < ROT_HALF, -sinv, 0.0)
    sin_hi = jnp.where((lane >= ROT_HALF) & (lane < ROT_DIM), sinv, 0.0)

    def proj(j):
        return jnp.dot(h, w_ref[:, j * STREAM_W:(j + 1) * STREAM_W], preferred_element_type=F32)

    def rope(t):
        outs = []
        for hb in range(HEADS):
            tc = t[:, hb * HEAD_W:(hb + 1) * HEAD_W]
            outs.append(tc * cosv
                        + pltpu.roll(tc, LANES - ROT_HALF, 1) * sin_lo
                        + pltpu.roll(tc, ROT_HALF, 1) * sin_hi)
        return jnp.concatenate(outs, axis=1)

    q_ref[0] = (rope(proj(0)) * (DA_QK_DIM ** -0.5 * math.log2(math.e))).T.astype(BF16)
    k_ref[...] = rope(proj(1)).astype(BF16)
    v_ref[0] = proj(2).astype(BF16).T

    gq = proj(3)
    hq_ref[...] = gq * _sigmoid(gq)

    a = lbr_ref[...]
    amax = jnp.max(a, axis=0, keepdims=True)
    ea = jnp.exp(a - amax)
    lb = ea[0:1, :] / jnp.sum(ea, axis=0, keepdims=True)
    gf = proj(4)
    f = lb + (1.0 - lb) * _sigmoid(gf)
    lf_ref[...] = jnp.log(f)
    kf_ref[...] = 1.0 - f

    gi_ref[...] = proj(5).astype(BF16)
    gg = proj(6)
    sg_ref[...] = (gg * _sigmoid(gg)).astype(BF16)


def _inproj(x2, mod3, norm_w, w_bf, pos_col, invf, lb_raw, seq):
    t, d = x2.shape
    tm = ROW_TILE
    per_b = seq // tm
    row = lambda i: (i, 0)
    full = lambda i: (0, 0)
    out_bf = jax.ShapeDtypeStruct((t, STREAM_W), BF16)
    out_f = jax.ShapeDtypeStruct((t, STREAM_W), F32)
    out_t = jax.ShapeDtypeStruct((t // seq, STREAM_W, seq), BF16)
    stream = pl.BlockSpec((tm, STREAM_W), row)
    stream_t = pl.BlockSpec((1, STREAM_W, tm), lambda i: (i // per_b, 0, i % per_b))
    return pl.pallas_call(
        _inproj_body,
        grid=(t // tm,),
        in_specs=[pl.BlockSpec((tm, d), row),
                  pl.BlockSpec((1, 6, d), lambda i: (i // per_b, 0, 0)),
                  pl.BlockSpec((1, d), full),
                  pl.BlockSpec((d, N_STREAMS * STREAM_W), full),
                  pl.BlockSpec((tm, 1), row),
                  pl.BlockSpec((1, LANES), full),
                  pl.BlockSpec(lb_raw.shape, full)],
        out_specs=[stream_t, stream, stream_t] + [stream] * 5,
        out_shape=[out_t, out_bf, out_t, out_f, out_f, out_f, out_bf, out_bf],
        compiler_params=pltpu.CompilerParams(vmem_limit_bytes=56 * 1024 * 1024),
        name="inproj",
    )(x2, mod3, norm_w, w_bf, pos_col, invf, lb_raw)


ONES_ROWS = 16


def _attn_body(qt_ref, k_ref, vt_ref, lam_ref, sw_ref, o_ref, s_scr, m_scr, a_scr, *, tile, ktile):
    qi = pl.program_id(2)
    qt = qt_ref[0]
    feat = lax.broadcasted_iota(jnp.int32, (HEAD_W, 1), 0)
    zero = jnp.zeros_like(qt)
    qmaps = (jnp.where(feat < DA_QK_DIM, qt, zero), jnp.where(feat >= DA_QK_DIM, qt, zero))
    ones = jnp.ones((ONES_ROWS, ktile), BF16)

    m_scr[...] = jnp.full(m_scr.shape, NEG_BIG, F32)
    a_scr[...] = jnp.zeros(a_scr.shape, F32)

    ratio = tile // ktile
    n_full = qi * ratio

    def score_block(ki, diag_offset):
        start = pl.multiple_of(ki * ktile, ktile)
        kb = k_ref[0, pl.ds(start, ktile), :]
        for mp in range(2):
            s = jnp.dot(kb, qmaps[mp], preferred_element_type=F32)
            if diag_offset is not None:
                key = lax.broadcasted_iota(jnp.int32, s.shape, 0) + diag_offset
                qry = lax.broadcasted_iota(jnp.int32, s.shape, 1)
                s = jnp.where(key <= qry, s, NEG_BIG)
            grouped = s.reshape(ktile // SUBLANES, SUBLANES, tile)
            m_scr[mp] = jnp.maximum(m_scr[mp], jnp.max(grouped, axis=0))
            s_scr[mp, pl.ds(start, ktile), :] = s

    def value_block(ki, col_max):
        start = pl.multiple_of(ki * ktile, ktile)
        vb = jnp.concatenate([vt_ref[0, :, pl.ds(start, ktile)], ones], axis=0)
        for mp in range(2):
            p = jnp.exp2(s_scr[mp, pl.ds(start, ktile), :] - col_max[mp]).astype(BF16)
            a_scr[mp] += jnp.dot(vb, p, preferred_element_type=F32)

    def score_body(ki, carry):
        score_block(ki, None)
        return carry

    lax.fori_loop(0, n_full, score_body, 0)
    for j in range(ratio):
        score_block(n_full + j, j * ktile)
    col_max = [jnp.max(m_scr[mp], axis=0, keepdims=True) for mp in range(2)]

    def value_body(ki, carry):
        value_block(ki, col_max)
        return carry

    lax.fori_loop(0, n_full + ratio, value_body, 0)

    lp = lam_ref[...]
    lam = (jnp.exp(jnp.sum(lp[0:1] * lp[1:2], axis=-1, keepdims=True))
           - jnp.exp(jnp.sum(lp[2:3] * lp[3:4], axis=-1, keepdims=True)) + LAM_INIT)
    o = (a_scr[0, 0:HEAD_W, :] / a_scr[0, HEAD_W:HEAD_W + 1, :]
         - lam * (a_scr[1, 0:HEAD_W, :] / a_scr[1, HEAD_W:HEAD_W + 1, :]))
    o = o * lax.rsqrt(jnp.mean(o * o, axis=0, keepdims=True) + EPS) * sw_ref[...]
    o_ref[0] = (o * (1.0 - LAM_INIT)).T.astype(BF16)


def _attn(qt3, k3, vt3, lam_p, subln_col):
    bsz, seq, _ = k3.shape
    tile = ATTN_TILE
    return pl.pallas_call(
        functools.partial(_attn_body, tile=tile, ktile=ATTN_KEY_TILE),
        grid=(bsz, HEADS, seq // tile),
        in_specs=[pl.BlockSpec((1, HEAD_W, tile), lambda b, h, i: (b, h, i)),
                  pl.BlockSpec((1, seq, HEAD_W), lambda b, h, i: (b, 0, h)),
                  pl.BlockSpec((1, HEAD_W, seq), lambda b, h, i: (b, h, 0)),
                  pl.BlockSpec(lam_p.shape, lambda b, h, i: (0, 0)),
                  pl.BlockSpec((HEAD_W, 1), lambda b, h, i: (0, 0))],
        out_specs=pl.BlockSpec((1, tile, HEAD_W), lambda b, h, i: (b, i, h)),
        out_shape=jax.ShapeDtypeStruct((bsz, seq, STREAM_W), BF16),
        scratch_shapes=[pltpu.VMEM((2, seq, tile), F32),
                        pltpu.VMEM((2, SUBLANES, tile), F32),
                        pltpu.VMEM((2, HEAD_W + ONES_ROWS, tile), F32)],
        compiler_params=pltpu.CompilerParams(vmem_limit_bytes=56 * 1024 * 1024),
        name="diff_attn",
    )(qt3, k3, vt3, lam_p, subln_col)


def _hgrn_levels(block):
    return [block >> (i + 1) for i in range(block.bit_length() - 1)]


def _hgrn_constants(block):
    t = np.arange(block)[:, None]
    s = np.arange(block)[None, :]
    tril = (s <= t).astype(np.float32)
    lv = np.full((block, block), -1, np.int32)
    halves = _hgrn_levels(block)
    for li, m in enumerate(halves):
        same = (t // (2 * m)) == (s // (2 * m))
        lv[same & ((t & m) != 0) & ((s & m) == 0)] = li
    lv[np.arange(block), np.arange(block)] = len(halves)
    return jnp.asarray(tril, BF16), jnp.asarray(lv)


def _level_ref(b_ref, h, m, block):
    def bcast(r, n):
        return jnp.broadcast_to(b_ref[h, pl.ds(r, 1), :], (n, HEAD_W))

    if 2 * m >= SUBLANES:
        return jnp.concatenate([bcast(s0 + m - 1, 2 * m) for s0 in range(0, block, 2 * m)], axis=0)
    sub = lax.broadcasted_iota(jnp.int32, (SUBLANES, HEAD_W), 0)
    pieces = []
    for s0 in range(0, block, SUBLANES):
        piece = bcast(s0 + m - 1, SUBLANES)
        for j in range(1, SUBLANES // (2 * m)):
            piece = jnp.where(sub >= 2 * m * j, bcast(s0 + 2 * m * j + m - 1, SUBLANES), piece)
        pieces.append(piece)
    return jnp.concatenate(pieces, axis=0)


def _hgrn_body(hq_ref, lf_ref, kf_ref, gi_ref, sg_ref, nw_ref, tril_ref, lv_ref, o_ref,
               st_scr, b_scr, *, block):
    @pl.when(pl.program_id(1) == 0)
    def _():
        st_scr[...] = jnp.zeros(st_scr.shape, F32)

    tril = tril_ref[...]
    lv = lv_ref[...]
    row = lax.broadcasted_iota(jnp.int32, (block, 1), 0)
    halves = _hgrn_levels(block)

    for h in range(HEADS):
        cs = slice(h * HEAD_W, (h + 1) * HEAD_W)
        q = hq_ref[0, :, cs]
        lf = lf_ref[0, :, cs]
        k = kf_ref[0, :, cs]
        v = gi_ref[0, :, cs]

        hi = lf.astype(BF16)
        r1 = lf - hi.astype(F32)
        mid = r1.astype(BF16)
        lo = (r1 - mid.astype(F32)).astype(BF16)
        b = (jnp.dot(tril, hi, preferred_element_type=F32)
             + jnp.dot(tril, mid, preferred_element_type=F32)
             + jnp.dot(tril, lo, preferred_element_type=F32))
        b_scr[h] = b

        qb = q.astype(BF16)
        kb = k.astype(BF16)
        scores = jnp.where(lv == len(halves),
                           lax.dot_general(qb, kb, NT_DIMS, preferred_element_type=F32), 0.0)
        for li, m in enumerate(halves):
            ref = _level_ref(b_scr, h, m, block)
            e = jnp.exp(-jnp.abs(b - ref))
            xl = (jnp.where((row & m) != 0, q, k) * e).astype(BF16)
            p = lax.dot_general(xl, xl, NT_DIMS, preferred_element_type=F32)
            scores = jnp.where(lv == li, p, scores)
        o_intra = jnp.dot(scores.astype(BF16), v, preferred_element_type=F32)

        st = st_scr[h]
        o_inter = lax.dot_general((q * jnp.exp(b)).astype(BF16), st.astype(BF16), NT_DIMS,
                                  preferred_element_type=F32)
        b_last = b[block - 1:block, :]
        kdec = (k * jnp.exp(b_last - b)).astype(BF16)
        st_scr[h] = st * jnp.exp(b_last) + lax.dot_general(v, kdec, TN_DIMS,
                                                            preferred_element_type=F32)

        o = o_inter + o_intra
        o = o * lax.rsqrt(jnp.mean(o * o, axis=-1, keepdims=True) + EPS) * nw_ref[...]
        o_ref[0, :, cs] = (o * sg_ref[0, :, cs].astype(F32)).astype(BF16)


def _hgrn(hq3, lf3, kf3, gi3, sg3, norm_w):
    bsz, seq, _ = hq3.shape
    block = HGRN_BLOCK
    tril, lv = _hgrn_constants(block)
    blk = pl.BlockSpec((1, block, STREAM_W), lambda b, g: (b, g, 0))
    const = lambda b, g: (0, 0)
    return pl.pallas_call(
        functools.partial(_hgrn_body, block=block),
        grid=(bsz, seq // block),
        in_specs=[blk, blk, blk, blk, blk,
                  pl.BlockSpec((1, HEAD_W), const),
                  pl.BlockSpec((block, block), const),
                  pl.BlockSpec((block, block), const)],
        out_specs=blk,
        out_shape=jax.ShapeDtypeStruct((bsz, seq, STREAM_W), BF16),
        scratch_shapes=[pltpu.VMEM((HEADS, HEAD_W, HEAD_W), F32),
                        pltpu.VMEM((HEADS, block, HEAD_W), F32)],
        name="hgrn2",
    )(hq3, lf3, kf3, gi3, sg3, norm_w, tril, lv)


META_E0, META_E1, META_R0, META_R1, META_W0, META_W1 = range(6)
GROUP_LANE0 = N_EXPERTS


def _outproj_body(da_ref, hg_ref, x_ref, mod_ref, wo_ref, nw_ref, wr_ref, br_ref, stril_ref,
                  x1_ref, h2_ref, meta_ref, cnt_ref, carry_scr):
    @pl.when(pl.program_id(0) == 0)
    def _():
        carry_scr[...] = jnp.zeros(carry_scr.shape, F32)

    attn = (jnp.dot(da_ref[...], wo_ref[0:STREAM_W, :], preferred_element_type=F32)
            + jnp.dot(hg_ref[...], wo_ref[STREAM_W:, :], preferred_element_type=F32))
    gate1 = mod_ref[0, 2:3, :]
    shift2 = mod_ref[0, 3:4, :]
    scale2 = mod_ref[0, 4:5, :]
    x1 = x_ref[...] + gate1 * attn
    x1_ref[...] = x1
    h2 = (x1 * lax.rsqrt(jnp.mean(x1 * x1, axis=-1, keepdims=True) + EPS) * nw_ref[...]
          * (1.0 + scale2) + shift2)
    _slab_store(h2_ref, h2)

    rows = h2.shape[0]
    h_hi = h2.astype(BF16)
    h_lo = (h2 - h_hi.astype(F32)).astype(BF16)
    parts = jnp.dot(jnp.concatenate([h_hi, h_lo], axis=0), wr_ref[...], preferred_element_type=F32)
    logits = ((parts[:rows, :LANES] + parts[:rows, LANES:])
              + (parts[rows:, :LANES] + parts[rows:, LANES:]) + br_ref[...])
    lane = lax.broadcasted_iota(jnp.int32, logits.shape, 1)
    far = jnp.int32(LANES)

    def first_max(vals):
        mx = jnp.max(vals, axis=-1, keepdims=True)
        return mx, jnp.min(jnp.where(vals == mx, lane, far), axis=-1, keepdims=True)

    is_g = (lane >= GROUP_LANE0) & (lane < GROUP_LANE0 + N_GROUPS)
    gmax, glane = first_max(jnp.where(is_g, logits, NEG_BIG))
    g_w = 1.0 / jnp.sum(jnp.where(is_g, jnp.exp(logits - gmax), 0.0), axis=-1, keepdims=True)
    gidx = glane - GROUP_LANE0
    in_grp = (lane < N_EXPERTS) & ((lane // EXPERTS_PER_GROUP) == gidx)
    el = jnp.where(in_grp, logits, NEG_BIG)
    m1, i1 = first_max(el)
    m2, i2 = first_max(jnp.where(lane == i1, NEG_BIG, el))
    r = jnp.exp(m2 - m1)
    w0 = g_w / (1.0 + r)
    w1 = g_w * r / (1.0 + r)

    hot0 = lane == i1
    hot1 = lane == i2
    multi = jnp.where(hot0 | hot1, 1.0, 0.0)
    before = jnp.dot(stril_ref[...], multi.astype(BF16), preferred_element_type=F32) + carry_scr[...]
    rank0 = jnp.sum(jnp.where(hot0, before, 0.0), axis=-1, keepdims=True)
    rank1 = jnp.sum(jnp.where(hot1, before, 0.0), axis=-1, keepdims=True)
    carry = carry_scr[...] + jnp.sum(multi, axis=0, keepdims=True)
    carry_scr[...] = carry
    cnt_ref[...] = carry

    meta = jnp.zeros(logits.shape, F32)
    for idx, val in ((META_E0, i1.astype(F32)), (META_E1, i2.astype(F32)),
                     (META_R0, rank0), (META_R1, rank1), (META_W0, w0), (META_W1, w1)):
        meta = jnp.where(lane == idx, val, meta)
    meta_ref[...] = meta


def _outproj(da2, hg2, x2, mod3, wo_bf, norm_w, w_route, b_route, seq):
    t, d = x2.shape
    tm = OUT_TILE
    per_b = seq // tm
    row = lambda i: (i, 0)
    full = lambda i: (0, 0)
    stril =jnp.asarray(np.tril(np.ones((tm, tm), np.float32), -1), BF16)
    return pl.pallas_call(
        _outproj_body,
        grid=(t // tm,),
        in_specs=[pl.BlockSpec((tm, STREAM_W), row),
                  pl.BlockSpec((tm, STREAM_W), row),
                  pl.BlockSpec((tm, d), row),
                  pl.BlockSpec((1, 6, d), lambda i: (i // per_b, 0, 0)),
                  pl.BlockSpec((2 * STREAM_W, d), full),
                  pl.BlockSpec((1, d), full),
                  pl.BlockSpec((d, 2 * LANES), full),
                  pl.BlockSpec((1, LANES), full),
                  pl.BlockSpec((tm, tm), full)],
        out_specs=[pl.BlockSpec((tm, d), row),
                   pl.BlockSpec((tm * SLAB, LANES), row),
                   pl.BlockSpec((tm, LANES), row),
                   pl.BlockSpec((1, LANES), full)],
        out_shape=[jax.ShapeDtypeStruct((t, d), F32),
                   jax.ShapeDtypeStruct((t * SLAB, LANES), F32),
                   jax.ShapeDtypeStruct((t, LANES), F32),
                   jax.ShapeDtypeStruct((1, LANES), F32)],
        scratch_shapes=[pltpu.VMEM((1, LANES), F32)],
        compiler_params=pltpu.CompilerParams(dimension_semantics=("arbitrary",)),
        name="outproj_route",
    )(da2, hg2, x2, mod3, wo_bf, norm_w, w_route, b_route, stril)


DMA_UNROLL = 8


def _scatter_body(last_ref, pos_ref, h2_ref, xs_hbm, zero_scr, sem, zsem, *, tile):
    rows = tile * SLAB

    @pl.when(pl.program_id(0) == 0)
    def _():
        zero_scr[...] = jnp.zeros(zero_scr.shape, F32)

        def clear(tile_idx):
            start = pl.multiple_of(tile_idx * (MOE_TILE * SLAB), SLAB)
            return pltpu.make_async_copy(zero_scr, xs_hbm.at[pl.ds(start, MOE_TILE * SLAB)], zsem)

        for e in range(N_EXPERTS):
            @pl.when(last_ref[e] >= 0)
            def _():
                clear(last_ref[e]).start()
        for e in range(N_EXPERTS):
            @pl.when(last_ref[e] >= 0)
            def _():
                clear(last_ref[e]).wait()

        def clear_unused(tile_idx, carry):
            clear(tile_idx).start()
            clear(tile_idx).wait()
            return carry

        lax.fori_loop(last_ref[N_EXPERTS], xs_hbm.shape[0] // (MOE_TILE * SLAB), clear_unused, 0)

    def issue(t, carry):
        src = h2_ref.at[pl.ds(pl.multiple_of(t * SLAB, SLAB), SLAB)]
        for j in range(2):
            slot = pl.multiple_of(pos_ref[0, 0, 2 * t + j] * SLAB, SLAB)
            pltpu.make_async_copy(src, xs_hbm.at[pl.ds(slot, SLAB)], sem).start(priority=j)
        return carry

    lax.fori_loop(0, tile, issue, 0, unroll=DMA_UNROLL)
    whole = pltpu.make_async_copy(h2_ref, xs_hbm.at[pl.ds(0, rows)], sem)
    whole.wait()
    whole.wait()


def _scatter(last_tile, pos3, h2s, n_slots):
    tile = GATHER_TILE
    rows = tile * SLAB
    grid_spec = pltpu.PrefetchScalarGridSpec(
        num_scalar_prefetch=1,
        grid=(h2s.shape[0] // rows,),
        in_specs=[pl.BlockSpec((1, 1, 2 * tile), lambda i, lt: (i, 0, 0), memory_space=pltpu.SMEM),
                  pl.BlockSpec((rows, LANES), lambda i, lt: (i, 0))],
        out_specs=pl.BlockSpec(memory_space=pl.ANY),
        scratch_shapes=[pltpu.VMEM((MOE_TILE * SLAB, LANES), F32),
                        pltpu.SemaphoreType.DMA, pltpu.SemaphoreType.DMA],
    )
    return pl.pallas_call(
        functools.partial(_scatter_body, tile=tile),
        grid_spec=grid_spec,
        out_shape=jax.ShapeDtypeStruct((n_slots * SLAB, LANES), F32),
        compiler_params=pltpu.CompilerParams(dimension_semantics=("arbitrary",)),
        name="moe_scatter",
    )(last_tile, pos3, h2s)


def _experts_body(te_ref, nv_ref, x_ref, wg_ref, wu_ref, wd_ref, y_ref, wgu_scr, wd_scr, *, tm):
    i = pl.program_id(0)
    e = te_ref[i]
    prev = te_ref[jnp.maximum(i - 1, 0)]

    @pl.when((i == 0) | (e != prev))
    def _():
        wgu_scr[:, 0:EXPERT_FF] = wg_ref[0].astype(BF16)
        wgu_scr[:, EXPERT_FF:] = wu_ref[0].astype(BF16)
        wd_scr[...] = wd_ref[0].astype(BF16)

    @pl.when(i < nv_ref[0])
    def _():
        x = _slab_load(x_ref, tm).astype(BF16)
        gu = jnp.dot(x, wgu_scr[...], preferred_element_type=F32)
        g = gu[:, 0:EXPERT_FF]
        act = (g * _sigmoid(g) * gu[:, EXPERT_FF:]).astype(BF16)
        _slab_store(y_ref, jnp.dot(act, wd_scr[...], preferred_element_type=F32))

    @pl.when(i >= nv_ref[0])
    def _():
        y_ref[...] = jnp.zeros(y_ref.shape, F32)


def _experts(tile_expert, n_valid, xs, w_gate, w_up, w_down):
    tm = MOE_TILE
    rows = tm * SLAB
    d = w_gate.shape[1]
    live = lambda i, te, nv: (jnp.minimum(i, nv[0] - 1), 0)
    grid_spec = pltpu.PrefetchScalarGridSpec(
        num_scalar_prefetch=2,
        grid=(xs.shape[0] // rows,),
        in_specs=[pl.BlockSpec((rows, LANES), live),
                  pl.BlockSpec((1, d, EXPERT_FF), lambda i, te, nv: (te[i], 0, 0)),
                  pl.BlockSpec((1, d, EXPERT_FF), lambda i, te, nv: (te[i], 0, 0)),
                  pl.BlockSpec((1, EXPERT_FF, d), lambda i, te, nv: (te[i], 0, 0))],
        out_specs=pl.BlockSpec((rows, LANES), lambda i, te, nv: (i, 0)),
        scratch_shapes=[pltpu.VMEM((d, 2 * EXPERT_FF), BF16),
                        pltpu.VMEM((EXPERT_FF, d), BF16)],
    )
    return pl.pallas_call(
        functools.partial(_experts_body, tm=tm),
        grid_spec=grid_spec,
        out_shape=jax.ShapeDtypeStruct(xs.shape, F32),
        compiler_params=pltpu.CompilerParams(dimension_semantics=("arbitrary",)),
        name="moe_experts",
    )(tile_expert, n_valid, xs, w_gate, w_up, w_down)


def _combine_body(pos_ref, nxt_ref, x1_ref, meta_ref, mod_ref, nw_ref, ys_hbm, o_ref, rows_scr, sems,
                  *, tile):
    i = pl.program_id(0)
    n = pl.num_programs(0)
    rows = tile * SLAB

    def fetch(idx_ref, buf):
        def issue(t, carry):
            dst = pl.ds(pl.multiple_of(t * SLAB, SLAB), SLAB)
            for j in range(2):
                slot = pl.multiple_of(idx_ref[0, 0, 2 * t + j] * SLAB, SLAB)
                pltpu.make_async_copy(ys_hbm.at[pl.ds(slot, SLAB)], rows_scr.at[buf, j, dst],
                                      sems.at[buf]).start(priority=j)
            return carry
        lax.fori_loop(0, tile, issue, 0, unroll=DMA_UNROLL)

    @pl.when(i == 0)
    def _():
        fetch(pos_ref, 0)

    for cur in range(2):
        @pl.when((i % 2 == cur) & (i + 1 < n))
        def _():
            fetch(nxt_ref, 1 - cur)

    for cur in range(2):
        @pl.when(i % 2 == cur)
        def _():
            for j in range(2):
                pltpu.make_async_copy(ys_hbm.at[pl.ds(0, rows)], rows_scr.at[cur, j],
                                      sems.at[cur]).wait()
            meta = meta_ref[...]
            w0 = meta[:, META_W0:META_W0 + 1]
            w1 = meta[:, META_W1:META_W1 + 1]
            y = (w0 * _slab_load(rows_scr.at[cur, 0], tile)
                 + w1 * _slab_load(rows_scr.at[cur, 1], tile))
            x2 = x1_ref[...] + mod_ref[0, 5:6, :] * y
            o_ref[...] = (x2 * lax.rsqrt(jnp.mean(x2 * x2, axis=-1, keepdims=True) + EPS)
                          * nw_ref[...])


def _combine(pos3, x1, meta, mod3, norm_w, ys, seq):
    t, d = x1.shape
    tile = GATHER_TILE
    per_b = seq // tile
    n = t // tile
    row = lambda i: (i, 0)
    return pl.pallas_call(
        functools.partial(_combine_body, tile=tile),
        grid=(n,),
        in_specs=[pl.BlockSpec((1, 1, 2 * tile), lambda i: (i, 0, 0), memory_space=pltpu.SMEM),
                  pl.BlockSpec((1, 1, 2 * tile), lambda i: (jnp.minimum(i + 1, n - 1), 0, 0),
                               memory_space=pltpu.SMEM),
                  pl.BlockSpec((tile, d), row),
                  pl.BlockSpec((tile, LANES), row),
                  pl.BlockSpec((1, 6, d), lambda i: (i // per_b, 0, 0)),
                  pl.BlockSpec((1, d), lambda i: (0, 0)),
                  pl.BlockSpec(memory_space=pl.ANY)],
        out_specs=pl.BlockSpec((tile, d), row),
        out_shape=jax.ShapeDtypeStruct((t, d), F32),
        scratch_shapes=[pltpu.VMEM((2, 2, tile * SLAB, LANES), F32), pltpu.SemaphoreType.DMA((2,))],
        compiler_params=pltpu.CompilerParams(dimension_semantics=("arbitrary",)),
        name="moe_combine",
    )(pos3, pos3, x1, meta, mod3, norm_w, ys)


def _rope_inv_freq_lanes():
    inv_freq = ROPE_THETA ** (-jnp.arange(ROT_HALF, dtype=F32) / ROT_HALF)
    lane = np.arange(LANES) % DA_QK_DIM
    table = jnp.where(jnp.asarray(lane < ROT_DIM), inv_freq[lane % ROT_HALF], 0.0)
    return table.reshape(1, LANES).astype(F32)


def kernel(x, c, positions, norm1_w, norm2_w, final_norm_w, ada_w, ada_b, w_in, w_out, da_lambda_q1, da_lambda_k1, da_lambda_q2, da_lambda_k2, da_subln_w, hg_lower_bound, hg_norm_w, moe_w_group, moe_b_group, moe_w_router, moe_b_router, moe_w_gate, moe_w_up, moe_w_down):
    bsz, seq, d = x.shape
    assert d == D_MODEL and norm1_w.shape[0] == 1, "single-layer model of width 1024 only"
    assert seq % ATTN_TILE == 0 and seq % HGRN_BLOCK == 0 and seq % ROW_TILE == 0
    t = bsz * seq
    x2 = x.reshape(t, d)

    mod3 = _adaln(c, ada_w[0], ada_b).reshape(bsz, 6, d)

    pos_col = positions.astype(F32).reshape(t, 1)
    qt, k, vt, hq, lf, kf, gi, sg = _inproj(x2, mod3, norm1_w, w_in[0].astype(BF16), pos_col,
                                          _rope_inv_freq_lanes(), hg_lower_bound, seq)

    as3 = lambda a: a.reshape(bsz, seq, STREAM_W)
    lam_p = jnp.concatenate([da_lambda_q1, da_lambda_k1, da_lambda_q2, da_lambda_k2], axis=0)
    da = _attn(qt, as3(k), vt, lam_p, da_subln_w.reshape(HEAD_W, 1))
    hg = _hgrn(as3(hq), as3(lf), as3(kf), as3(gi), as3(sg), hg_norm_w)

    pad = jnp.zeros((d, LANES - N_EXPERTS - N_GROUPS), F32)
    w_route = jnp.concatenate([moe_w_router[0], moe_w_group[0], pad], axis=1)
    b_route = jnp.concatenate([moe_b_router[0], moe_b_group[0], pad[0]]).reshape(1, LANES)
    w_route_hi = w_route.astype(BF16)
    w_route_lo = (w_route - w_route_hi.astype(F32)).astype(BF16)
    x1, h2, meta, cnt = _outproj(da.reshape(t, STREAM_W), hg.reshape(t, STREAM_W), x2, mod3,
                                 w_out[0].astype(BF16), norm2_w,
                                 jnp.concatenate([w_route_hi, w_route_lo], axis=1), b_route, seq)

    counts = cnt[0, :N_EXPERTS].astype(jnp.int32)
    tiles_e = (counts + MOE_TILE - 1) // MOE_TILE
    tile_end = jnp.cumsum(tiles_e)
    offs = (tile_end - tiles_e) * MOE_TILE
    ids = meta[:, META_E0:META_E1 + 1].astype(jnp.int32)
    ranks = meta[:, META_R0:META_R1 + 1].astype(jnp.int32)
    expert_iota = jnp.arange(N_EXPERTS, dtype=jnp.int32)
    pos = jnp.sum(jnp.where(ids[..., None] == expert_iota, offs, 0), axis=-1) + ranks
    n_tiles = (2 * t) // MOE_TILE + N_EXPERTS
    n_valid = tile_end[-1:]
    tile_ids = jnp.minimum(jnp.arange(n_tiles, dtype=jnp.int32), n_valid - 1)
    tile_expert = jnp.sum(tile_ids[:, None] >= tile_end[None, :], axis=1).astype(jnp.int32)
    pos3 = pos.reshape(t // GATHER_TILE, 1, 2 * GATHER_TILE)
    last_tile = jnp.concatenate([jnp.where(tiles_e > 0, tile_end - 1, -1), n_valid]).astype(jnp.int32)

    xs = _scatter(last_tile, pos3, h2, n_tiles * MOE_TILE)
    ys = _experts(tile_expert, n_valid.astype(jnp.int32), xs, moe_w_gate[0], moe_w_up[0], moe_w_down[0])
    out = _combine(pos3, x1, meta, mod3, final_norm_w.reshape(1, d), ys, seq)
    return out.reshape(bsz, seq, d)
```

```python
import functools
import math

import numpy as np
import jax
import jax.numpy as jnp
from jax import lax
from jax.experimental import pallas as pl
from jax.experimental.pallas import tpu as pltpu

F32 = jnp.float32
BF16 = jnp.bfloat16
HIGHEST = lax.Precision.HIGHEST

LANES = 128
SUBLANES = 8
D_MODEL = 1024
HEADS = 4
HEAD_W = 128
STREAM_W = HEADS * HEAD_W
N_STREAMS = 7
DA_QK_DIM = 64
ROPE_THETA = 500000.0
ROT_DIM = DA_QK_DIM // 4
ROT_HALF = ROT_DIM // 2
N_GROUPS = 4
EXPERTS_PER_GROUP = 8
N_EXPERTS = N_GROUPS * EXPERTS_PER_GROUP
EXPERT_FF = 256
EPS = 1e-6
LAM_INIT = 0.8 - 0.6 * math.exp(-0.3 * 0)
NEG_BIG = -1e30

ROW_TILE = 256
OUT_TILE = 512
ATTN_TILE = 1024
ATTN_KEY_TILE = 512
HGRN_BLOCK = 128
HGRN_STEP = 256
MOE_TILE = 256
GATHER_TILE = 256

NT_DIMS = (((1,), (1,)), ((), ()))
TN_DIMS = (((0,), (0,)), ((), ()))


def _sigmoid(x):
    return 1.0 / (1.0 + jnp.exp(-x))


SLAB = D_MODEL // LANES


def _slab_store(ref, val):
    rows = val.shape[0]
    for c in range(SLAB):
        ref[pl.ds(c, rows, stride=SLAB), :] = val[:, c * LANES:(c + 1) * LANES]


def _slab_load(ref, rows):
    return jnp.concatenate([ref[pl.ds(c, rows, stride=SLAB), :] for c in range(SLAB)], axis=1)


def _adaln_body(c_ref, w_ref, b_ref, o_ref):
    c = c_ref[...]
    ca = c * _sigmoid(c)
    o_ref[...] = jnp.dot(ca, w_ref[...], preferred_element_type=F32, precision=HIGHEST) + b_ref[...]


def _adaln(c, ada_w, ada_b):
    bsz, d = c.shape
    n = ada_w.shape[1]
    tn = 1024
    return pl.pallas_call(
        _adaln_body,
        grid=(n // tn,),
        in_specs=[pl.BlockSpec((bsz, d), lambda j: (0, 0)),
                  pl.BlockSpec((d, tn), lambda j: (0, j)),
                  pl.BlockSpec((1, tn), lambda j: (0, j))],
        out_specs=pl.BlockSpec((bsz, tn), lambda j: (0, j)),
        out_shape=jax.ShapeDtypeStruct((bsz, n), F32),
        name="adaln",
    )(c, ada_w, ada_b)


def _inproj_body(x_ref, mod_ref, nw_ref, w_ref, pos_ref, invf_ref, lbr_ref,
                 q_ref, k_ref, v_ref, hq_ref, lf_ref, kf_ref, gi_ref, sg_ref):
    x = x_ref[...]
    ms = jnp.mean(x * x, axis=-1, keepdims=True)
    y = x * lax.rsqrt(ms + EPS) * nw_ref[...]
    shift = mod_ref[0, 0:1, :]
    scale = mod_ref[0, 1:2, :]
    h = (y * (1.0 + scale) + shift).astype(BF16)

    ang = pos_ref[...] * invf_ref[...]
    cosv = jnp.cos(ang)
    sinv = jnp.sin(ang)
    lane = lax.broadcasted_iota(jnp.int32, (1, LANES), 1) % DA_QK_DIM
    sin_lo = jnp.where(lane < ROT_HALF, -sinv, 0.0)
    sin_hi = jnp.where((lane >= ROT_HALF) & (lane < ROT_DIM), sinv, 0.0)

    def proj(j):
        return jnp.dot(h, w_ref[:, j * STREAM_W:(j + 1) * STREAM_W], preferred_element_type=F32)

    def rope(t):
        outs = []
        for hb in range(HEADS):
            tc = t[:, hb * HEAD_W:(hb + 1) * HEAD_W]
            outs.append(tc * cosv
                        + pltpu.roll(tc, LANES - ROT_HALF, 1) * sin_lo
                        + pltpu.roll(tc, ROT_HALF, 1) * sin_hi)
        return jnp.concatenate(outs, axis=1)

    q_ref[0] = (rope(proj(0)) * (DA_QK_DIM ** -0.5 * math.log2(math.e))).T.astype(BF16)
    k_ref[...] = rope(proj(1)).astype(BF16)
    v_ref[0] = proj(2).astype(BF16).T

    gq = proj(3)
    hq_ref[...] = gq * _sigmoid(gq)

    a = lbr_ref[...]
    amax = jnp.max(a, axis=0, keepdims=True)
    ea = jnp.exp(a - amax)
    lb = ea[0:1, :] / jnp.sum(ea, axis=0, keepdims=True)
    gf = proj(4)
    f = lb + (1.0 - lb) * _sigmoid(gf)
    lf_ref[...] = jnp.log(f) * math.log2(math.e)
    kf_ref[...] = 1.0 - f

    gi_ref[...] = proj(5).astype(BF16)
    gg = proj(6)
    sg_ref[...] = (gg * _sigmoid(gg)).astype(BF16)


def _inproj(x2, mod3, norm_w, w_bf, pos_col, invf, lb_raw, seq):
    t, d = x2.shape
    tm = ROW_TILE
    per_b = seq // tm
    row = lambda i: (i, 0)
    full = lambda i: (0, 0)
    out_bf = jax.ShapeDtypeStruct((t, STREAM_W), BF16)
    out_f = jax.ShapeDtypeStruct((t, STREAM_W), F32)
    out_t = jax.ShapeDtypeStruct((t // seq, STREAM_W, seq), BF16)
    stream = pl.BlockSpec((tm, STREAM_W), row)
    stream_t = pl.BlockSpec((1, STREAM_W, tm), lambda i: (i // per_b, 0, i % per_b))
    return pl.pallas_call(
        _inproj_body,
        grid=(t // tm,),
        in_specs=[pl.BlockSpec((tm, d), row),
                  pl.BlockSpec((1, 6, d), lambda i: (i // per_b, 0, 0)),
                  pl.BlockSpec((1, d), full),
                  pl.BlockSpec((d, N_STREAMS * STREAM_W), full),
                  pl.BlockSpec((tm, 1), row),
                  pl.BlockSpec((1, LANES), full),
                  pl.BlockSpec(lb_raw.shape, full)],
        out_specs=[stream_t, stream, stream_t] + [stream] * 5,
        out_shape=[out_t, out_bf, out_t, out_f, out_f, out_f, out_bf, out_bf],
        compiler_params=pltpu.CompilerParams(vmem_limit_bytes=56 * 1024 * 1024),
        name="inproj",
    )(x2, mod3, norm_w, w_bf, pos_col, invf, lb_raw)


ONES_ROWS = 16


def _attn_body(qt_ref, k_ref, vt_ref, lam_ref, sw_ref, o_ref, s_scr, m_scr, a_scr, *, tile, ktile):
    qi = pl.program_id(2)
    qt = qt_ref[0]
    feat = lax.broadcasted_iota(jnp.int32, (HEAD_W, 1), 0)
    zero = jnp.zeros_like(qt)
    qmaps = (jnp.where(feat < DA_QK_DIM, qt, zero), jnp.where(feat >= DA_QK_DIM, qt, zero))
    ones = jnp.ones((ONES_ROWS, ktile), BF16)

    m_scr[...] = jnp.full(m_scr.shape, NEG_BIG, F32)
    a_scr[...] = jnp.zeros(a_scr.shape, F32)

    ratio = tile // ktile
    n_full = qi * ratio

    def score_block(ki, diag_offset):
        start = pl.multiple_of(ki * ktile, ktile)
        kb = k_ref[0, pl.ds(start, ktile), :]
        for mp in range(2):
            s = jnp.dot(kb, qmaps[mp], preferred_element_type=F32)
            if diag_offset is not None:
                key = lax.broadcasted_iota(jnp.int32, s.shape, 0) + diag_offset
                qry = lax.broadcasted_iota(jnp.int32, s.shape, 1)
                s = jnp.where(key <= qry, s, NEG_BIG)
            grouped = s.reshape(ktile // SUBLANES, SUBLANES, tile)
            m_scr[mp] = jnp.maximum(m_scr[mp], jnp.max(grouped, axis=0))
            s_scr[mp, pl.ds(start, ktile), :] = s

    def value_block(ki, col_max):
        start = pl.multiple_of(ki * ktile, ktile)
        vb = jnp.concatenate([vt_ref[0, :, pl.ds(start, ktile)], ones], axis=0)
        for mp in range(2):
            p = jnp.exp2(s_scr[mp, pl.ds(start, ktile), :] - col_max[mp]).astype(BF16)
            a_scr[mp] += jnp.dot(vb, p, preferred_element_type=F32)

    def score_body(ki, carry):
        score_block(ki, None)
        return carry

    lax.fori_loop(0, n_full, score_body, 0)
    for j in range(ratio):
        score_block(n_full + j, j * ktile)
    col_max = [jnp.max(m_scr[mp], axis=0, keepdims=True) for mp in range(2)]

    def value_body(ki, carry):
        value_block(ki, col_max)
        return carry

    lax.fori_loop(0, n_full + ratio, value_body, 0)

    lp = lam_ref[...]
    lam = (jnp.exp(jnp.sum(lp[0:1] * lp[1:2], axis=-1, keepdims=True))
           - jnp.exp(jnp.sum(lp[2:3] * lp[3:4], axis=-1, keepdims=True)) + LAM_INIT)
    o = (a_scr[0, 0:HEAD_W, :] / a_scr[0, HEAD_W:HEAD_W + 1, :]
         - lam * (a_scr[1, 0:HEAD_W, :] / a_scr[1, HEAD_W:HEAD_W + 1, :]))
    o = o * lax.rsqrt(jnp.mean(o * o, axis=0, keepdims=True) + EPS) * sw_ref[...]
    o_ref[0] = (o * (1.0 - LAM_INIT)).T.astype(BF16)


def _attn(qt3, k3, vt3, lam_p, subln_col):
    bsz, seq, _ = k3.shape
    tile = ATTN_TILE
    return pl.pallas_call(
        functools.partial(_attn_body, tile=tile, ktile=ATTN_KEY_TILE),
        grid=(bsz, HEADS, seq // tile),
        in_specs=[pl.BlockSpec((1, HEAD_W, tile), lambda b, h, i: (b, h, i)),
                  pl.BlockSpec((1, seq, HEAD_W), lambda b, h, i: (b, 0, h)),
                  pl.BlockSpec((1, HEAD_W, seq), lambda b, h, i: (b, h, 0)),
                  pl.BlockSpec(lam_p.shape, lambda b, h, i: (0, 0)),
                  pl.BlockSpec((HEAD_W, 1), lambda b, h, i: (0, 0))],
        out_specs=pl.BlockSpec((1, tile, HEAD_W), lambda b, h, i: (b, i, h)),
        out_shape=jax.ShapeDtypeStruct((bsz, seq, STREAM_W), BF16),
        scratch_shapes=[pltpu.VMEM((2, seq, tile), F32),
                        pltpu.VMEM((2, SUBLANES, tile), F32),
                        pltpu.VMEM((2, HEAD_W + ONES_ROWS, tile), F32)],
        compiler_params=pltpu.CompilerParams(vmem_limit_bytes=56 * 1024 * 1024),
        name="diff_attn",
    )(qt3, k3, vt3, lam_p, subln_col)


def _hgrn_levels(block):
    return [block >> (i + 1) for i in range(block.bit_length() - 1)]


def _hgrn_constants(block):
    t = np.arange(block)[:, None]
    s = np.arange(block)[None, :]
    tril = (s <= t).astype(np.float32)
    lv = np.full((block, block), -1, np.int32)
    halves = _hgrn_levels(block)
    for li, m in enumerate(halves):
        same = (t // (2 * m)) == (s // (2 * m))
        lv[same & ((t & m) != 0) & ((s & m) == 0)] = li
    lv[np.arange(block), np.arange(block)] = len(halves)
    return jnp.asarray(tril, BF16), jnp.asarray(lv)


def _level_operand(b_ref, h, b, q, k, m, block):
    def ref_rows(r, n):
        return jnp.broadcast_to(b_ref[h, pl.ds(r, 1), :], (n, HEAD_W))

    if m >= SUBLANES:
        pieces = []
        for s0 in range(0, block, 2 * m):
            ref = ref_rows(s0 + m - 1, m)
            lo = slice(s0, s0 + m)
            up = slice(s0 + m, s0 + 2 * m)
            pieces.append(k[lo] * jnp.exp2(ref - b[lo]))
            pieces.append(q[up] * jnp.exp2(b[up] - ref))
        return jnp.concatenate(pieces, axis=0)

    sub = lax.broadcasted_iota(jnp.int32, (SUBLANES, HEAD_W), 0)
    refs = []
    for s0 in range(0, block, SUBLANES):
        piece = ref_rows(s0 + m - 1, SUBLANES)
        for j in range(1, SUBLANES // (2 * m)):
            piece = jnp.where(sub >= 2 * m * j, ref_rows(s0 + 2 * m * j + m - 1, SUBLANES), piece)
        refs.append(piece)
    d = b - jnp.concatenate(refs, axis=0)
    row = lax.broadcasted_iota(jnp.int32, (block, 1), 0)
    return jnp.where((row & m) != 0, q, k) * jnp.exp2(jnp.minimum(d, -d))


def _hgrn_body(hq_ref, lf_ref, kf_ref, gi_ref, sg_ref, nw_ref, tril_ref, lv_ref, o_ref,
               st_scr, b_scr, *, block, step):
    @pl.when(pl.program_id(1) == 0)
    def _():
        st_scr[...] = jnp.zeros(st_scr.shape, F32)

    tril = tril_ref[...]
    lv = lv_ref[...]
    halves = _hgrn_levels(block)

    units = [(u, r0, h) for u, (r0, h) in enumerate(
        (r0, h) for r0 in range(0, step, block) for h in range(HEADS))]

    def cols(h):
        return slice(h * HEAD_W, (h + 1) * HEAD_W)

    q, k, v, b, scores = {}, {}, {}, {}, {}
    for u, r0, h in units:
        rows = slice(r0, r0 + block)
        q[u] = hq_ref[0, rows, cols(h)]
        k[u] = kf_ref[0, rows, cols(h)]
        v[u] = gi_ref[0, rows, cols(h)]
        lf = lf_ref[0, rows, cols(h)]
        hi = lf.astype(BF16)
        r1 = lf - hi.astype(F32)
        mid = r1.astype(BF16)
        lo = (r1 - mid.astype(F32)).astype(BF16)
        b[u] = (jnp.dot(tril, hi, preferred_element_type=F32)
                + jnp.dot(tril, mid, preferred_element_type=F32)
                + jnp.dot(tril, lo, preferred_element_type=F32))
        b_scr[u] = b[u]
        scores[u] = jnp.where(lv == len(halves),
                              lax.dot_general(q[u].astype(BF16), k[u].astype(BF16), NT_DIMS,
                                              preferred_element_type=F32), 0.0)

    for li, m in enumerate(halves):
        for u, r0, h in units:
            xl = _level_operand(b_scr, u, b[u], q[u], k[u], m, block).astype(BF16)
            p = lax.dot_general(xl, xl, NT_DIMS, preferred_element_type=F32)
            scores[u] = jnp.where(lv == li, p, scores[u])

    for u, r0, h in units:
        rows = slice(r0, r0 + block)
        o_intra = jnp.dot(scores[u].astype(BF16), v[u], preferred_element_type=F32)
        st = st_scr[h]
        o_inter = lax.dot_general((q[u] * jnp.exp2(b[u])).astype(BF16), st.astype(BF16), NT_DIMS,
                                  preferred_element_type=F32)
        b_last = b[u][block - 1:block, :]
        kdec = (k[u] * jnp.exp2(b_last - b[u])).astype(BF16)
        st_scr[h] = st * jnp.exp2(b_last) + lax.dot_general(v[u], kdec, TN_DIMS,
                                                             preferred_element_type=F32)
        o = o_inter + o_intra
        o = o * lax.rsqrt(jnp.mean(o * o, axis=-1, keepdims=True) + EPS) * nw_ref[...]
        o_ref[0, rows, cols(h)] = (o * sg_ref[0, rows, cols(h)].astype(F32)).astype(BF16)


def _hgrn(hq3, lf3, kf3, gi3, sg3, norm_w):
    bsz, seq, _ = hq3.shape
    block = HGRN_BLOCK
    step = HGRN_STEP
    tril, lv = _hgrn_constants(block)
    blk = pl.BlockSpec((1, step, STREAM_W), lambda b, g: (b, g, 0))
    const = lambda b, g: (0, 0)
    return pl.pallas_call(
        functools.partial(_hgrn_body, block=block, step=step),
        grid=(bsz, seq // step),
        in_specs=[blk, blk, blk, blk, blk,
                  pl.BlockSpec((1, HEAD_W), const),
                  pl.BlockSpec((block, block), const),
                  pl.BlockSpec((block, block), const)],
        out_specs=blk,
        out_shape=jax.ShapeDtypeStruct((bsz, seq, STREAM_W), BF16),
        scratch_shapes=[pltpu.VMEM((HEADS, HEAD_W, HEAD_W), F32),
                        pltpu.VMEM((HEADS * step // block, block, HEAD_W), F32)],
        name="hgrn2",
    )(hq3, lf3, kf3, gi3, sg3, norm_w, tril, lv)


META_E0, META_E1, META_R0, META_R1, META_W0, META_W1 = range(6)
GROUP_LANE0 = N_EXPERTS


def _outproj_body(da_ref, hg_ref, x_ref, mod_ref, wo_ref, nw_ref, wr_ref, br_ref, stril_ref,
                  x1_ref, h2_ref, meta_ref, cnt_ref, carry_scr):
    @pl.when(pl.program_id(0) == 0)
    def _():
        carry_scr[...] = jnp.zeros(carry_scr.shape, F32)

    attn = (jnp.dot(da_ref[...], wo_ref[0:STREAM_W, :], preferred_element_type=F32)
            + jnp.dot(hg_ref[...], wo_ref[STREAM_W:, :], preferred_element_type=F32))
    gate1 = mod_ref[0, 2:3, :]
    shift2 = mod_ref[0, 3:4, :]
    scale2 = mod_ref[0, 4:5, :]
    x1 = x_ref[...] + gate1 * attn
    x1_ref[...] = x1
    h2 = (x1 * lax.rsqrt(jnp.mean(x1 * x1, axis=-1, keepdims=True) + EPS) * nw_ref[...]
          * (1.0 + scale2) + shift2)
    _slab_store(h2_ref, h2)

    rows = h2.shape[0]
    h_hi = h2.astype(BF16)
    h_lo = (h2 - h_hi.astype(F32)).astype(BF16)
    parts = jnp.dot(jnp.concatenate([h_hi, h_lo], axis=0), wr_ref[...], preferred_element_type=F32)
    logits = ((parts[:rows, :LANES] + parts[:rows, LANES:])
              + (parts[rows:, :LANES] + parts[rows:, LANES:]) + br_ref[...])
    lane = lax.broadcasted_iota(jnp.int32, logits.shape, 1)
    far = jnp.int32(LANES)

    def first_max(vals):
        mx = jnp.max(vals, axis=-1, keepdims=True)
        return mx, jnp.min(jnp.where(vals == mx, lane, far), axis=-1, keepdims=True)

    is_g = (lane >= GROUP_LANE0) & (lane < GROUP_LANE0 + N_GROUPS)
    gmax, glane = first_max(jnp.where(is_g, logits, NEG_BIG))
    g_w = 1.0 / jnp.sum(jnp.where(is_g, jnp.exp(logits - gmax), 0.0), axis=-1, keepdims=True)
    gidx = glane - GROUP_LANE0
    in_grp = (lane < N_EXPERTS) & ((lane // EXPERTS_PER_GROUP) == gidx)
    el = jnp.where(in_grp, logits, NEG_BIG)
    m1, i1 = first_max(el)
    m2, i2 = first_max(jnp.where(lane == i1, NEG_BIG, el))
    r = jnp.exp(m2 - m1)
    w0 = g_w / (1.0 + r)
    w1 = g_w * r / (1.0 + r)

    hot0 = lane == i1
    hot1 = lane == i2
    multi = jnp.where(hot0 | hot1, 1.0, 0.0)
    before = jnp.dot(stril_ref[...], multi.astype(BF16), preferred_element_type=F32) + carry_scr[...]
    rank0 = jnp.sum(jnp.where(hot0, before, 0.0), axis=-1, keepdims=True)
    rank1 = jnp.sum(jnp.where(hot1, before, 0.0), axis=-1, keepdims=True)
    carry = carry_scr[...] + jnp.sum(multi, axis=0, keepdims=True)
    carry_scr[...] = carry
    cnt_ref[...] = carry

    meta = jnp.zeros(logits.shape, F32)
    for idx, val in ((META_E0, i1.astype(F32)), (META_E1, i2.astype(F32)),
                     (META_R0, rank0), (META_R1, rank1), (META_W0, w0), (META_W1, w1)):
        meta = jnp.where(lane == idx, val, meta)
    meta_ref[...] = meta


def _outproj(da2, hg2, x2, mod3, wo_bf, norm_w, w_route, b_route, seq):
    t, d = x2.shape
    tm = OUT_TILE
    per_b = seq // tm
    row = lambda i: (i, 0)
    full = lambda i: (0, 0)
    stril =jnp.asarray(np.tril(np.ones((tm, tm), np.float32), -1), BF16)
    return pl.pallas_call(
        _outproj_body,
        grid=(t // tm,),
        in_specs=[pl.BlockSpec((tm, STREAM_W), row),
                  pl.BlockSpec((tm, STREAM_W), row),
                  pl.BlockSpec((tm, d), row),
                  pl.BlockSpec((1, 6, d), lambda i: (i // per_b, 0, 0)),
                  pl.BlockSpec((2 * STREAM_W, d), full),
                  pl.BlockSpec((1, d), full),
                  pl.BlockSpec((d, 2 * LANES), full),
                  pl.BlockSpec((1, LANES), full),
                  pl.BlockSpec((tm, tm), full)],
        out_specs=[pl.BlockSpec((tm, d), row),
                   pl.BlockSpec((tm * SLAB, LANES), row),
                   pl.BlockSpec((tm, LANES), row),
                   pl.BlockSpec((1, LANES), full)],
        out_shape=[jax.ShapeDtypeStruct((t, d), F32),
                   jax.ShapeDtypeStruct((t * SLAB, LANES), F32),
                   jax.ShapeDtypeStruct((t, LANES), F32),
                   jax.ShapeDtypeStruct((1, LANES), F32)],
        scratch_shapes=[pltpu.VMEM((1, LANES), F32)],
        compiler_params=pltpu.CompilerParams(dimension_semantics=("arbitrary",)),
        name="outproj_route",
    )(da2, hg2, x2, mod3, wo_bf, norm_w, w_route, b_route, stril)


DMA_UNROLL = 8


def _scatter_body(last_ref, pos_ref, h2_ref, xs_hbm, zero_scr, sem, zsem, *, tile):
    rows = tile * SLAB

    @pl.when(pl.program_id(0) == 0)
    def _():
        zero_scr[...] = jnp.zeros(zero_scr.shape, F32)

        def clear(tile_idx):
            start = pl.multiple_of(tile_idx * (MOE_TILE * SLAB), SLAB)
            return pltpu.make_async_copy(zero_scr, xs_hbm.at[pl.ds(start, MOE_TILE * SLAB)], zsem)

        for e in range(N_EXPERTS):
            @pl.when(last_ref[e] >= 0)
            def _():
                clear(last_ref[e]).start()
        for e in range(N_EXPERTS):
            @pl.when(last_ref[e] >= 0)
            def _():
                clear(last_ref[e]).wait()

        def clear_unused(tile_idx, carry):
            clear(tile_idx).start()
            clear(tile_idx).wait()
            return carry

        lax.fori_loop(last_ref[N_EXPERTS], xs_hbm.shape[0] // (MOE_TILE * SLAB), clear_unused, 0)

    def issue(t, carry):
        src = h2_ref.at[pl.ds(pl.multiple_of(t * SLAB, SLAB), SLAB)]
        for j in range(2):
            slot = pl.multiple_of(pos_ref[0, 0, 2 * t + j] * SLAB, SLAB)
            pltpu.make_async_copy(src, xs_hbm.at[pl.ds(slot, SLAB)], sem).start(priority=j)
        return carry

    lax.fori_loop(0, tile, issue, 0, unroll=DMA_UNROLL)
    whole = pltpu.make_async_copy(h2_ref, xs_hbm.at[pl.ds(0, rows)], sem)
    whole.wait()
    whole.wait()


def _scatter(last_tile, pos3, h2s, n_slots):
    tile = GATHER_TILE
    rows = tile * SLAB
    grid_spec = pltpu.PrefetchScalarGridSpec(
        num_scalar_prefetch=1,
        grid=(h2s.shape[0] // rows,),
        in_specs=[pl.BlockSpec((1, 1, 2 * tile), lambda i, lt: (i, 0, 0), memory_space=pltpu.SMEM),
                  pl.BlockSpec((rows, LANES), lambda i, lt: (i, 0))],
        out_specs=pl.BlockSpec(memory_space=pl.ANY),
        scratch_shapes=[pltpu.VMEM((MOE_TILE * SLAB, LANES), F32),
                        pltpu.SemaphoreType.DMA, pltpu.SemaphoreType.DMA],
    )
    return pl.pallas_call(
        functools.partial(_scatter_body, tile=tile),
        grid_spec=grid_spec,
        out_shape=jax.ShapeDtypeStruct((n_slots * SLAB, LANES), F32),
        compiler_params=pltpu.CompilerParams(dimension_semantics=("arbitrary",)),
        name="moe_scatter",
    )(last_tile, pos3, h2s)


def _experts_body(te_ref, nv_ref, x_ref, wg_ref, wu_ref, wd_ref, y_ref, wgu_scr, wd_scr, *, tm):
    i = pl.program_id(0)
    e = te_ref[i]
    prev = te_ref[jnp.maximum(i - 1, 0)]

    @pl.when((i == 0) | (e != prev))
    def _():
        wgu_scr[:, 0:EXPERT_FF] = wg_ref[0].astype(BF16)
        wgu_scr[:, EXPERT_FF:] = wu_ref[0].astype(BF16)
        wd_scr[...] = wd_ref[0].astype(BF16)

    @pl.when(i < nv_ref[0])
    def _():
        x = _slab_load(x_ref, tm).astype(BF16)
        gu = jnp.dot(x, wgu_scr[...], preferred_element_type=F32)
        g = gu[:, 0:EXPERT_FF]
        act = (g * _sigmoid(g) * gu[:, EXPERT_FF:]).astype(BF16)
        _slab_store(y_ref, jnp.dot(act, wd_scr[...], preferred_element_type=F32))

    @pl.when(i >= nv_ref[0])
    def _():
        y_ref[...] = jnp.zeros(y_ref.shape, F32)


def _experts(tile_expert, n_valid, xs, w_gate, w_up, w_down):
    tm = MOE_TILE
    rows = tm * SLAB
    d = w_gate.shape[1]
    live = lambda i, te, nv: (jnp.minimum(i, nv[0] - 1), 0)
    grid_spec = pltpu.PrefetchScalarGridSpec(
        num_scalar_prefetch=2,
        grid=(xs.shape[0] // rows,),
        in_specs=[pl.BlockSpec((rows, LANES), live),
                  pl.BlockSpec((1, d, EXPERT_FF), lambda i, te, nv: (te[i], 0, 0)),
                  pl.BlockSpec((1, d, EXPERT_FF), lambda i, te, nv: (te[i], 0, 0)),
                  pl.BlockSpec((1, EXPERT_FF, d), lambda i, te, nv: (te[i], 0, 0))],
        out_specs=pl.BlockSpec((rows, LANES), lambda i, te, nv: (i, 0)),
        scratch_shapes=[pltpu.VMEM((d, 2 * EXPERT_FF), BF16),
                        pltpu.VMEM((EXPERT_FF, d), BF16)],
    )
    return pl.pallas_call(
        functools.partial(_experts_body, tm=tm),
        grid_spec=grid_spec,
        out_shape=jax.ShapeDtypeStruct(xs.shape, F32),
        compiler_params=pltpu.CompilerParams(dimension_semantics=("arbitrary",)),
        name="moe_experts",
    )(tile_expert, n_valid, xs, w_gate, w_up, w_down)


def _combine_body(pos_ref, nxt_ref, x1_ref, meta_ref, mod_ref, nw_ref, ys_hbm, o_ref, rows_scr, sems,
                  *, tile):
    i = pl.program_id(0)
    n = pl.num_programs(0)
    rows = tile * SLAB

    def fetch(idx_ref, buf):
        def issue(t, carry):
            dst = pl.ds(pl.multiple_of(t * SLAB, SLAB), SLAB)
            for j in range(2):
                slot = pl.multiple_of(idx_ref[0, 0, 2 * t + j] * SLAB, SLAB)
                pltpu.make_async_copy(ys_hbm.at[pl.ds(slot, SLAB)], rows_scr.at[buf, j, dst],
                                      sems.at[buf]).start(priority=j)
            return carry
        lax.fori_loop(0, tile, issue, 0, unroll=DMA_UNROLL)

    @pl.when(i == 0)
    def _():
        fetch(pos_ref, 0)

    for cur in range(2):
        @pl.when((i % 2 == cur) & (i + 1 < n))
        def _():
            fetch(nxt_ref, 1 - cur)

    for cur in range(2):
        @pl.when(i % 2 == cur)
        def _():
            for j in range(2):
                pltpu.make_async_copy(ys_hbm.at[pl.ds(0, rows)], rows_scr.at[cur, j],
                                      sems.at[cur]).wait()
            meta = meta_ref[...]
            w0 = meta[:, META_W0:META_W0 + 1]
            w1 = meta[:, META_W1:META_W1 + 1]
            y = (w0 * _slab_load(rows_scr.at[cur, 0], tile)
                 + w1 * _slab_load(rows_scr.at[cur, 1], tile))
            x2 = x1_ref[...] + mod_ref[0, 5:6, :] * y
            o_ref[...] = (x2 * lax.rsqrt(jnp.mean(x2 * x2, axis=-1, keepdims=True) + EPS)
                          * nw_ref[...])


def _combine(pos3, x1, meta, mod3, norm_w, ys, seq):
    t, d = x1.shape
    tile = GATHER_TILE
    per_b = seq // tile
    n = t // tile
    row = lambda i: (i, 0)
    return pl.pallas_call(
        functools.partial(_combine_body, tile=tile),
        grid=(n,),
        in_specs=[pl.BlockSpec((1, 1, 2 * tile), lambda i: (i, 0, 0), memory_space=pltpu.SMEM),
                  pl.BlockSpec((1, 1, 2 * tile), lambda i: (jnp.minimum(i + 1, n - 1), 0, 0),
                               memory_space=pltpu.SMEM),
                  pl.BlockSpec((tile, d), row),
                  pl.BlockSpec((tile, LANES), row),
                  pl.BlockSpec((1, 6, d), lambda i: (i // per_b, 0, 0)),
                  pl.BlockSpec((1, d), lambda i: (0, 0)),
                  pl.BlockSpec(memory_space=pl.ANY)],
        out_specs=pl.BlockSpec((tile, d), row),
        out_shape=jax.ShapeDtypeStruct((t, d), F32),
        scratch_shapes=[pltpu.VMEM((2, 2, tile * SLAB, LANES), F32), pltpu.SemaphoreType.DMA((2,))],
        compiler_params=pltpu.CompilerParams(dimension_semantics=("arbitrary",)),
        name="moe_combine",
    )(pos3, pos3, x1, meta, mod3, norm_w, ys)


def _rope_inv_freq_lanes():
    inv_freq = ROPE_THETA ** (-jnp.arange(ROT_HALF, dtype=F32) / ROT_HALF)
    lane = np.arange(LANES) % DA_QK_DIM
    table = jnp.where(jnp.asarray(lane < ROT_DIM), inv_freq[lane % ROT_HALF], 0.0)
    return table.reshape(1, LANES).astype(F32)


def kernel(x, c, positions, norm1_w, norm2_w, final_norm_w, ada_w, ada_b, w_in, w_out, da_lambda_q1, da_lambda_k1, da_lambda_q2, da_lambda_k2, da_subln_w, hg_lower_bound, hg_norm_w, moe_w_group, moe_b_group, moe_w_router, moe_b_router, moe_w_gate, moe_w_up, moe_w_down):
    bsz, seq, d = x.shape
    assert d == D_MODEL and norm1_w.shape[0] == 1, "single-layer model of width 1024 only"
    assert seq % ATTN_TILE == 0 and seq % HGRN_STEP == 0 and seq % OUT_TILE == 0
    t = bsz * seq
    x2 = x.reshape(t, d)

    mod3 = _adaln(c, ada_w[0], ada_b).reshape(bsz, 6, d)

    pos_col = positions.astype(F32).reshape(t, 1)
    qt, k, vt, hq, lf, kf, gi, sg = _inproj(x2, mod3, norm1_w, w_in[0].astype(BF16), pos_col,
                                          _rope_inv_freq_lanes(), hg_lower_bound, seq)

    as3 = lambda a: a.reshape(bsz, seq, STREAM_W)
    lam_p = jnp.concatenate([da_lambda_q1, da_lambda_k1, da_lambda_q2, da_lambda_k2], axis=0)
    da = _attn(qt, as3(k), vt, lam_p, da_subln_w.reshape(HEAD_W, 1))
    hg = _hgrn(as3(hq), as3(lf), as3(kf), as3(gi), as3(sg), hg_norm_w)

    pad = jnp.zeros((d, LANES - N_EXPERTS - N_GROUPS), F32)
    w_route = jnp.concatenate([moe_w_router[0], moe_w_group[0], pad], axis=1)
    b_route = jnp.concatenate([moe_b_router[0], moe_b_group[0], pad[0]]).reshape(1, LANES)
    w_route_hi = w_route.astype(BF16)
    w_route_lo = (w_route - w_route_hi.astype(F32)).astype(BF16)
    x1, h2, meta, cnt = _outproj(da.reshape(t, STREAM_W), hg.reshape(t, STREAM_W), x2, mod3,
                                 w_out[0].astype(BF16), norm2_w,
                                 jnp.concatenate([w_route_hi, w_route_lo], axis=1), b_route, seq)

    counts = cnt[0, :N_EXPERTS].astype(jnp.int32)
    tiles_e = (counts + MOE_TILE - 1) // MOE_TILE
    tile_end = jnp.cumsum(tiles_e)
    offs = (tile_end - tiles_e) * MOE_TILE
    ids = meta[:, META_E0:META_E1 + 1].astype(jnp.int32)
    ranks = meta[:, META_R0:META_R1 + 1].astype(jnp.int32)
    expert_iota = jnp.arange(N_EXPERTS, dtype=jnp.int32)
    pos = jnp.sum(jnp.where(ids[..., None] == expert_iota, offs, 0), axis=-1) + ranks
    n_tiles = (2 * t) // MOE_TILE + N_EXPERTS
    n_valid = tile_end[-1:]
    tile_ids = jnp.minimum(jnp.arange(n_tiles, dtype=jnp.int32), n_valid - 1)
    tile_expert = jnp.sum(tile_ids[:, None] >= tile_end[None, :], axis=1).astype(jnp.int32)
    pos3 = pos.reshape(t // GATHER_TILE, 1, 2 * GATHER_TILE)
    last_tile = jnp.concatenate([jnp.where(tiles_e > 0, tile_end - 1, -1), n_valid]).astype(jnp.int32)

    xs = _scatter(last_tile, pos3, h2, n_tiles * MOE_TILE)
    ys = _experts(tile_expert, n_valid.astype(jnp.int32), xs, moe_w_gate[0], moe_w_up[0], moe_w_down[0])
    out = _combine(pos3, x1, meta, mod3, final_norm_w.reshape(1, d), ys, seq)
    return out.reshape(bsz, seq, d)
```

```python
import functools
import math

import numpy as np
import jax
import jax.numpy as jnp
from jax import lax
from jax.experimental import pallas as pl
from jax.experimental.pallas import tpu as pltpu

F32 = jnp.float32
BF16 = jnp.bfloat16
HIGHEST = lax.Precision.HIGHEST

LANES = 128
SUBLANES = 8
D_MODEL = 1024
HEADS = 4
HEAD_W = 128
STREAM_W = HEADS * HEAD_W
N_STREAMS = 7
DA_QK_DIM = 64
ROPE_THETA = 500000.0
ROT_DIM = DA_QK_DIM // 4
ROT_HALF = ROT_DIM // 2
N_GROUPS = 4
EXPERTS_PER_GROUP = 8
N_EXPERTS = N_GROUPS * EXPERTS_PER_GROUP
EXPERT_FF = 256
EPS = 1e-6
LAM_INIT = 0.8 - 0.6 * math.exp(-0.3 * 0)
NEG_BIG = -1e30

ROW_TILE = 256
OUT_TILE = 512
ATTN_TILE = 1024
ATTN_KEY_TILE = 512
HGRN_BLOCK = 128
HGRN_STEP = 256
MOE_TILE = 256
GATHER_TILE = 256

NT_DIMS = (((1,), (1,)), ((), ()))
TN_DIMS = (((0,), (0,)), ((), ()))


def _sigmoid(x):
    return 1.0 / (1.0 + jnp.exp(-x))


U32 = BF16
PIECES = D_MODEL // LANES
SLAB = PIECES


def _slab_store(ref, val, piece_scr):
    rows = val.shape[0]
    for c in range(PIECES):
        piece_scr[pl.ds(c, rows, stride=PIECES), :] = val[:, c * LANES:(c + 1) * LANES]
    ref[...] = piece_scr[...].astype(BF16)


def _slab_load(ref, rows, piece_scr):
    piece_scr[...] = ref[...].astype(F32)
    return jnp.concatenate([piece_scr[pl.ds(c, rows, stride=PIECES), :] for c in range(PIECES)],
                           axis=1)


def _adaln_body(c_ref, w_ref, b_ref, o_ref):
    c = c_ref[...]
    ca = c * _sigmoid(c)
    o_ref[...] = jnp.dot(ca, w_ref[...], preferred_element_type=F32, precision=HIGHEST) + b_ref[...]


def _adaln(c, ada_w, ada_b):
    bsz, d = c.shape
    n = ada_w.shape[1]
    tn = 1024
    return pl.pallas_call(
        _adaln_body,
        grid=(n // tn,),
        in_specs=[pl.BlockSpec((bsz, d), lambda j: (0, 0)),
                  pl.BlockSpec((d, tn), lambda j: (0, j)),
                  pl.BlockSpec((1, tn), lambda j: (0, j))],
        out_specs=pl.BlockSpec((bsz, tn), lambda j: (0, j)),
        out_shape=jax.ShapeDtypeStruct((bsz, n), F32),
        name="adaln",
    )(c, ada_w, ada_b)


def _inproj_body(x_ref, mod_ref, nw_ref, w_ref, pos_ref, invf_ref, lbr_ref,
                 q_ref, k_ref, v_ref, hq_ref, lf_ref, kf_ref, gi_ref, sg_ref):
    x = x_ref[...]
    ms = jnp.mean(x * x, axis=-1, keepdims=True)
    y = x * lax.rsqrt(ms + EPS) * nw_ref[...]
    shift = mod_ref[0, 0:1, :]
    scale = mod_ref[0, 1:2, :]
    h = (y * (1.0 + scale) + shift).astype(BF16)

    ang = pos_ref[...] * invf_ref[...]
    cosv = jnp.cos(ang)
    sinv = jnp.sin(ang)
    lane = lax.broadcasted_iota(jnp.int32, (1, LANES), 1) % DA_QK_DIM
    sin_lo = jnp.where(lane < ROT_HALF, -sinv, 0.0)
    sin_hi = jnp.where((lane >= ROT_HALF) & (lane < ROT_DIM), sinv, 0.0)

    def proj(j):
        return jnp.dot(h, w_ref[:, j * STREAM_W:(j + 1) * STREAM_W], preferred_element_type=F32)

    def rope(t):
        outs = []
        for hb in range(HEADS):
            tc = t[:, hb * HEAD_W:(hb + 1) * HEAD_W]
            outs.append(tc * cosv
                        + pltpu.roll(tc, LANES - ROT_HALF, 1) * sin_lo
                        + pltpu.roll(tc, ROT_HALF, 1) * sin_hi)
        return jnp.concatenate(outs, axis=1)

    q_ref[0] = (rope(proj(0)) * (DA_QK_DIM ** -0.5 * math.log2(math.e))).T.astype(BF16)
    k_ref[...] = rope(proj(1)).astype(BF16)
    v_ref[0] = proj(2).astype(BF16).T

    gq = proj(3)
    hq_ref[...] = gq * _sigmoid(gq)

    a = lbr_ref[...]
    amax = jnp.max(a, axis=0, keepdims=True)
    ea = jnp.exp(a - amax)
    lb = ea[0:1, :] / jnp.sum(ea, axis=0, keepdims=True)
    gf = proj(4)
    f = lb + (1.0 - lb) * _sigmoid(gf)
    lf_ref[...] = jnp.log(f) * math.log2(math.e)
    kf_ref[...] = 1.0 - f

    gi_ref[...] = proj(5).astype(BF16)
    gg = proj(6)
    sg_ref[...] = (gg * _sigmoid(gg)).astype(BF16)


def _inproj(x2, mod3, norm_w, w_bf, pos_col, invf, lb_raw, seq):
    t, d = x2.shape
    tm = ROW_TILE
    per_b = seq // tm
    row = lambda i: (i, 0)
    full = lambda i: (0, 0)
    out_bf = jax.ShapeDtypeStruct((t, STREAM_W), BF16)
    out_f = jax.ShapeDtypeStruct((t, STREAM_W), F32)
    out_t = jax.ShapeDtypeStruct((t // seq, STREAM_W, seq), BF16)
    stream = pl.BlockSpec((tm, STREAM_W), row)
    stream_t = pl.BlockSpec((1, STREAM_W, tm), lambda i: (i // per_b, 0, i % per_b))
    return pl.pallas_call(
        _inproj_body,
        grid=(t // tm,),
        in_specs=[pl.BlockSpec((tm, d), row),
                  pl.BlockSpec((1, 6, d), lambda i: (i // per_b, 0, 0)),
                  pl.BlockSpec((1, d), full),
                  pl.BlockSpec((d, N_STREAMS * STREAM_W), full),
                  pl.BlockSpec((tm, 1), row),
                  pl.BlockSpec((1, LANES), full),
                  pl.BlockSpec(lb_raw.shape, full)],
        out_specs=[stream_t, stream, stream_t] + [stream] * 5,
        out_shape=[out_t, out_bf, out_t, out_f, out_f, out_f, out_bf, out_bf],
        compiler_params=pltpu.CompilerParams(vmem_limit_bytes=56 * 1024 * 1024),
        name="inproj",
    )(x2, mod3, norm_w, w_bf, pos_col, invf, lb_raw)


ONES_ROWS = 16


def _attn_body(qt_ref, k_ref, vt_ref, lam_ref, sw_ref, o_ref, s_scr, m_scr, a_scr, *, tile, ktile):
    qi = pl.program_id(2)
    qt = qt_ref[0]
    feat = lax.broadcasted_iota(jnp.int32, (HEAD_W, 1), 0)
    zero = jnp.zeros_like(qt)
    qmaps = (jnp.where(feat < DA_QK_DIM, qt, zero), jnp.where(feat >= DA_QK_DIM, qt, zero))
    ones = jnp.ones((ONES_ROWS, ktile), BF16)

    m_scr[...] = jnp.full(m_scr.shape, NEG_BIG, F32)
    a_scr[...] = jnp.zeros(a_scr.shape, F32)

    ratio = tile // ktile
    n_full = qi * ratio

    def score_block(ki, diag_offset):
        start = pl.multiple_of(ki * ktile, ktile)
        kb = k_ref[0, pl.ds(start, ktile), :]
        for mp in range(2):
            s = jnp.dot(kb, qmaps[mp], preferred_element_type=F32)
            if diag_offset is not None:
                key = lax.broadcasted_iota(jnp.int32, s.shape, 0) + diag_offset
                qry = lax.broadcasted_iota(jnp.int32, s.shape, 1)
                s = jnp.where(key <= qry, s, NEG_BIG)
            grouped = s.reshape(ktile // SUBLANES, SUBLANES, tile)
            m_scr[mp] = jnp.maximum(m_scr[mp], jnp.max(grouped, axis=0))
            s_scr[mp, pl.ds(start, ktile), :] = s

    def value_block(ki, col_max):
        start = pl.multiple_of(ki * ktile, ktile)
        vb = jnp.concatenate([vt_ref[0, :, pl.ds(start, ktile)], ones], axis=0)
        for mp in range(2):
            p = jnp.exp2(s_scr[mp, pl.ds(start, ktile), :] - col_max[mp]).astype(BF16)
            a_scr[mp] += jnp.dot(vb, p, preferred_element_type=F32)

    def score_body(ki, carry):
        score_block(ki, None)
        return carry

    lax.fori_loop(0, n_full, score_body, 0)
    for j in range(ratio):
        score_block(n_full + j, j * ktile)
    col_max = [jnp.max(m_scr[mp], axis=0, keepdims=True) for mp in range(2)]

    def value_body(ki, carry):
        value_block(ki, col_max)
        return carry

    lax.fori_loop(0, n_full + ratio, value_body, 0)

    lp = lam_ref[...]
    lam = (jnp.exp(jnp.sum(lp[0:1] * lp[1:2], axis=-1, keepdims=True))
           - jnp.exp(jnp.sum(lp[2:3] * lp[3:4], axis=-1, keepdims=True)) + LAM_INIT)
    o = (a_scr[0, 0:HEAD_W, :] / a_scr[0, HEAD_W:HEAD_W + 1, :]
         - lam * (a_scr[1, 0:HEAD_W, :] / a_scr[1, HEAD_W:HEAD_W + 1, :]))
    o = o * lax.rsqrt(jnp.mean(o * o, axis=0, keepdims=True) + EPS) * sw_ref[...]
    o_ref[0] = (o * (1.0 - LAM_INIT)).T.astype(BF16)


def _attn(qt3, k3, vt3, lam_p, subln_col):
    bsz, seq, _ = k3.shape
    tile = ATTN_TILE
    return pl.pallas_call(
        functools.partial(_attn_body, tile=tile, ktile=ATTN_KEY_TILE),
        grid=(bsz, HEADS, seq // tile),
        in_specs=[pl.BlockSpec((1, HEAD_W, tile), lambda b, h, i: (b, h, i)),
                  pl.BlockSpec((1, seq, HEAD_W), lambda b, h, i: (b, 0, h)),
                  pl.BlockSpec((1, HEAD_W, seq), lambda b, h, i: (b, h, 0)),
                  pl.BlockSpec(lam_p.shape, lambda b, h, i: (0, 0)),
                  pl.BlockSpec((HEAD_W, 1), lambda b, h, i: (0, 0))],
        out_specs=pl.BlockSpec((1, tile, HEAD_W), lambda b, h, i: (b, i, h)),
        out_shape=jax.ShapeDtypeStruct((bsz, seq, STREAM_W), BF16),
        scratch_shapes=[pltpu.VMEM((2, seq, tile), F32),
                        pltpu.VMEM((2, SUBLANES, tile), F32),
                        pltpu.VMEM((2, HEAD_W + ONES_ROWS, tile), F32)],
        compiler_params=pltpu.CompilerParams(vmem_limit_bytes=56 * 1024 * 1024),
        name="diff_attn",
    )(qt3, k3, vt3, lam_p, subln_col)


def _hgrn_levels(block):
    return [block >> (i + 1) for i in range(block.bit_length() - 1)]


def _hgrn_constants(block):
    t = np.arange(block)[:, None]
    s = np.arange(block)[None, :]
    tril = (s <= t).astype(np.float32)
    lv = np.full((block, block), -1, np.int32)
    halves = _hgrn_levels(block)
    for li, m in enumerate(halves):
        same = (t // (2 * m)) == (s // (2 * m))
        lv[same & ((t & m) != 0) & ((s & m) == 0)] = li
    lv[np.arange(block), np.arange(block)] = len(halves)
    return jnp.asarray(tril, BF16), jnp.asarray(lv)


def _level_operand(b_ref, h, b, q, k, m, block):
    def ref_rows(r, n):
        return jnp.broadcast_to(b_ref[h, pl.ds(r, 1), :], (n, HEAD_W))

    if m >= SUBLANES:
        pieces = []
        for s0 in range(0, block, 2 * m):
            ref = ref_rows(s0 + m - 1, m)
            lo = slice(s0, s0 + m)
            up = slice(s0 + m, s0 + 2 * m)
            pieces.append(k[lo] * jnp.exp2(ref - b[lo]))
            pieces.append(q[up] * jnp.exp2(b[up] - ref))
        return jnp.concatenate(pieces, axis=0)

    sub = lax.broadcasted_iota(jnp.int32, (SUBLANES, HEAD_W), 0)
    refs = []
    for s0 in range(0, block, SUBLANES):
        piece = ref_rows(s0 + m - 1, SUBLANES)
        for j in range(1, SUBLANES // (2 * m)):
            piece = jnp.where(sub >= 2 * m * j, ref_rows(s0 + 2 * m * j + m - 1, SUBLANES), piece)
        refs.append(piece)
    d = b - jnp.concatenate(refs, axis=0)
    row = lax.broadcasted_iota(jnp.int32, (block, 1), 0)
    return jnp.where((row & m) != 0, q, k) * jnp.exp2(jnp.minimum(d, -d))


def _hgrn_body(hq_ref, lf_ref, kf_ref, gi_ref, sg_ref, nw_ref, tril_ref, lv_ref, o_ref,
               st_scr, b_scr, *, block, step):
    @pl.when(pl.program_id(1) == 0)
    def _():
        st_scr[...] = jnp.zeros(st_scr.shape, F32)

    tril = tril_ref[...]
    lv = lv_ref[...]
    halves = _hgrn_levels(block)

    units = [(u, r0, h) for u, (r0, h) in enumerate(
        (r0, h) for r0 in range(0, step, block) for h in range(HEADS))]

    def cols(h):
        return slice(h * HEAD_W, (h + 1) * HEAD_W)

    q, k, v, b, scores = {}, {}, {}, {}, {}
    for u, r0, h in units:
        rows = slice(r0, r0 + block)
        q[u] = hq_ref[0, rows, cols(h)]
        k[u] = kf_ref[0, rows, cols(h)]
        v[u] = gi_ref[0, rows, cols(h)]
        lf = lf_ref[0, rows, cols(h)]
        hi = lf.astype(BF16)
        r1 = lf - hi.astype(F32)
        mid = r1.astype(BF16)
        lo = (r1 - mid.astype(F32)).astype(BF16)
        b[u] = (jnp.dot(tril, hi, preferred_element_type=F32)
                + jnp.dot(tril, mid, preferred_element_type=F32)
                + jnp.dot(tril, lo, preferred_element_type=F32))
        b_scr[u] = b[u]
        scores[u] = jnp.where(lv == len(halves),
                              lax.dot_general(q[u].astype(BF16), k[u].astype(BF16), NT_DIMS,
                                              preferred_element_type=F32), 0.0)

    for li, m in enumerate(halves):
        for u, r0, h in units:
            xl = _level_operand(b_scr, u, b[u], q[u], k[u], m, block).astype(BF16)
            p = lax.dot_general(xl, xl, NT_DIMS, preferred_element_type=F32)
            scores[u] = jnp.where(lv == li, p, scores[u])

    for u, r0, h in units:
        rows = slice(r0, r0 + block)
        o_intra = jnp.dot(scores[u].astype(BF16), v[u], preferred_element_type=F32)
        st = st_scr[h]
        o_inter = lax.dot_general((q[u] * jnp.exp2(b[u])).astype(BF16), st.astype(BF16), NT_DIMS,
                                  preferred_element_type=F32)
        b_last = b[u][block - 1:block, :]
        kdec = (k[u] * jnp.exp2(b_last - b[u])).astype(BF16)
        st_scr[h] = st * jnp.exp2(b_last) + lax.dot_general(v[u], kdec, TN_DIMS,
                                                             preferred_element_type=F32)
        o = o_inter + o_intra
        o = o * lax.rsqrt(jnp.mean(o * o, axis=-1, keepdims=True) + EPS) * nw_ref[...]
        o_ref[0, rows, cols(h)] = (o * sg_ref[0, rows, cols(h)].astype(F32)).astype(BF16)


def _hgrn(hq3, lf3, kf3, gi3, sg3, norm_w):
    bsz, seq, _ = hq3.shape
    block = HGRN_BLOCK
    step = HGRN_STEP
    tril, lv = _hgrn_constants(block)
    blk = pl.BlockSpec((1, step, STREAM_W), lambda b, g: (b, g, 0))
    const = lambda b, g: (0, 0)
    return pl.pallas_call(
        functools.partial(_hgrn_body, block=block, step=step),
        grid=(bsz, seq // step),
        in_specs=[blk, blk, blk, blk, blk,
                  pl.BlockSpec((1, HEAD_W), const),
                  pl.BlockSpec((block, block), const),
                  pl.BlockSpec((block, block), const)],
        out_specs=blk,
        out_shape=jax.ShapeDtypeStruct((bsz, seq, STREAM_W), BF16),
        scratch_shapes=[pltpu.VMEM((HEADS, HEAD_W, HEAD_W), F32),
                        pltpu.VMEM((HEADS * step // block, block, HEAD_W), F32)],
        name="hgrn2",
    )(hq3, lf3, kf3, gi3, sg3, norm_w, tril, lv)


META_E0, META_E1, META_R0, META_R1, META_W0, META_W1 = range(6)
GROUP_LANE0 = N_EXPERTS


def _outproj_body(da_ref, hg_ref, x_ref, mod_ref, wo_ref, nw_ref, wr_ref, br_ref, stril_ref,
                  x1_ref, h2_ref, meta_ref, cnt_ref, carry_scr, piece_scr):
    @pl.when(pl.program_id(0) == 0)
    def _():
        carry_scr[...] = jnp.zeros(carry_scr.shape, F32)

    attn = (jnp.dot(da_ref[...], wo_ref[0:STREAM_W, :], preferred_element_type=F32)
            + jnp.dot(hg_ref[...], wo_ref[STREAM_W:, :], preferred_element_type=F32))
    gate1 = mod_ref[0, 2:3, :]
    shift2 = mod_ref[0, 3:4, :]
    scale2 = mod_ref[0, 4:5, :]
    x1 = x_ref[...] + gate1 * attn
    x1_ref[...] = x1
    h2 = (x1 * lax.rsqrt(jnp.mean(x1 * x1, axis=-1, keepdims=True) + EPS) * nw_ref[...]
          * (1.0 + scale2) + shift2)
    _slab_store(h2_ref, h2, piece_scr)

    rows = h2.shape[0]
    h_hi = h2.astype(BF16)
    h_lo = (h2 - h_hi.astype(F32)).astype(BF16)
    parts = jnp.dot(jnp.concatenate([h_hi, h_lo], axis=0), wr_ref[...], preferred_element_type=F32)
    logits = ((parts[:rows, :LANES] + parts[:rows, LANES:])
              + (parts[rows:, :LANES] + parts[rows:, LANES:]) + br_ref[...])
    lane = lax.broadcasted_iota(jnp.int32, logits.shape, 1)
    far = jnp.int32(LANES)

    def first_max(vals):
        mx = jnp.max(vals, axis=-1, keepdims=True)
        return mx, jnp.min(jnp.where(vals == mx, lane, far), axis=-1, keepdims=True)

    is_g = (lane >= GROUP_LANE0) & (lane < GROUP_LANE0 + N_GROUPS)
    gmax, glane = first_max(jnp.where(is_g, logits, NEG_BIG))
    g_w = 1.0 / jnp.sum(jnp.where(is_g, jnp.exp(logits - gmax), 0.0), axis=-1, keepdims=True)
    gidx = glane - GROUP_LANE0
    in_grp = (lane < N_EXPERTS) & ((lane // EXPERTS_PER_GROUP) == gidx)
    el = jnp.where(in_grp, logits, NEG_BIG)
    m1, i1 = first_max(el)
    m2, i2 = first_max(jnp.where(lane == i1, NEG_BIG, el))
    r = jnp.exp(m2 - m1)
    w0 = g_w / (1.0 + r)
    w1 = g_w * r / (1.0 + r)

    hot0 = lane == i1
    hot1 = lane == i2
    multi = jnp.where(hot0 | hot1, 1.0, 0.0)
    before = jnp.dot(stril_ref[...], multi.astype(BF16), preferred_element_type=F32) + carry_scr[...]
    rank0 = jnp.sum(jnp.where(hot0, before, 0.0), axis=-1, keepdims=True)
    rank1 = jnp.sum(jnp.where(hot1, before, 0.0), axis=-1, keepdims=True)
    carry = carry_scr[...] + jnp.sum(multi, axis=0, keepdims=True)
    carry_scr[...] = carry
    cnt_ref[...] = carry

    meta = jnp.zeros(logits.shape, F32)
    for idx, val in ((META_E0, i1.astype(F32)), (META_E1, i2.astype(F32)),
                     (META_R0, rank0), (META_R1, rank1), (META_W0, w0), (META_W1, w1)):
        meta = jnp.where(lane == idx, val, meta)
    meta_ref[...] = meta


def _outproj(da2, hg2, x2, mod3, wo_bf, norm_w, w_route, b_route, seq):
    t, d = x2.shape
    tm = OUT_TILE
    per_b = seq // tm
    row = lambda i: (i, 0)
    full = lambda i: (0, 0)
    stril =jnp.asarray(np.tril(np.ones((tm, tm), np.float32), -1), BF16)
    return pl.pallas_call(
        _outproj_body,
        grid=(t // tm,),
        in_specs=[pl.BlockSpec((tm, STREAM_W), row),
                  pl.BlockSpec((tm, STREAM_W), row),
                  pl.BlockSpec((tm, d), row),
                  pl.BlockSpec((1, 6, d), lambda i: (i // per_b, 0, 0)),
                  pl.BlockSpec((2 * STREAM_W, d), full),
                  pl.BlockSpec((1, d), full),
                  pl.BlockSpec((d, 2 * LANES), full),
                  pl.BlockSpec((1, LANES), full),
                  pl.BlockSpec((tm, tm), full)],
        out_specs=[pl.BlockSpec((tm, d), row),
                   pl.BlockSpec((tm * SLAB, LANES), row),
                   pl.BlockSpec((tm, LANES), row),
                   pl.BlockSpec((1, LANES), full)],
        out_shape=[jax.ShapeDtypeStruct((t, d), F32),
                   jax.ShapeDtypeStruct((t * SLAB, LANES), U32),
                   jax.ShapeDtypeStruct((t, LANES), F32),
                   jax.ShapeDtypeStruct((1, LANES), F32)],
        scratch_shapes=[pltpu.VMEM((1, LANES), F32), pltpu.VMEM((tm * PIECES, LANES), F32)],
        compiler_params=pltpu.CompilerParams(dimension_semantics=("arbitrary",)),
        name="outproj_route",
    )(da2, hg2, x2, mod3, wo_bf, norm_w, w_route, b_route, stril)


DMA_UNROLL = 8


def _scatter_body(last_ref, pos_ref, h2_ref, xs_hbm, zero_scr, sem, zsem, *, tile):
    rows = tile * SLAB

    @pl.when(pl.program_id(0) == 0)
    def _():
        zero_scr[...] = jnp.zeros(zero_scr.shape, U32)

        def clear(tile_idx):
            start = pl.multiple_of(tile_idx * (MOE_TILE * SLAB), SLAB)
            return pltpu.make_async_copy(zero_scr, xs_hbm.at[pl.ds(start, MOE_TILE * SLAB)], zsem)

        for e in range(N_EXPERTS):
            @pl.when(last_ref[e] >= 0)
            def _():
                clear(last_ref[e]).start()
        for e in range(N_EXPERTS):
            @pl.when(last_ref[e] >= 0)
            def _():
                clear(last_ref[e]).wait()

        def clear_unused(tile_idx, carry):
            clear(tile_idx).start()
            clear(tile_idx).wait()
            return carry

        lax.fori_loop(last_ref[N_EXPERTS], xs_hbm.shape[0] // (MOE_TILE * SLAB), clear_unused, 0)

    def issue(t, carry):
        src = h2_ref.at[pl.ds(pl.multiple_of(t * SLAB, SLAB), SLAB)]
        for j in range(2):
            slot = pl.multiple_of(pos_ref[0, 0, 2 * t + j] * SLAB, SLAB)
            pltpu.make_async_copy(src, xs_hbm.at[pl.ds(slot, SLAB)], sem).start(priority=j)
        return carry

    lax.fori_loop(0, tile, issue, 0, unroll=DMA_UNROLL)
    whole = pltpu.make_async_copy(h2_ref, xs_hbm.at[pl.ds(0, rows)], sem)
    whole.wait()
    whole.wait()


def _scatter(last_tile, pos3, h2s, n_slots):
    tile = GATHER_TILE
    rows = tile * SLAB
    grid_spec = pltpu.PrefetchScalarGridSpec(
        num_scalar_prefetch=1,
        grid=(h2s.shape[0] // rows,),
        in_specs=[pl.BlockSpec((1, 1, 2 * tile), lambda i, lt: (i, 0, 0), memory_space=pltpu.SMEM),
                  pl.BlockSpec((rows, LANES), lambda i, lt: (i, 0))],
        out_specs=pl.BlockSpec(memory_space=pl.ANY),
        scratch_shapes=[pltpu.VMEM((MOE_TILE * SLAB, LANES), U32),
                        pltpu.SemaphoreType.DMA, pltpu.SemaphoreType.DMA],
    )
    return pl.pallas_call(
        functools.partial(_scatter_body, tile=tile),
        grid_spec=grid_spec,
        out_shape=jax.ShapeDtypeStruct((n_slots * SLAB, LANES), U32),
        compiler_params=pltpu.CompilerParams(dimension_semantics=("arbitrary",)),
        name="moe_scatter",
    )(last_tile, pos3, h2s)


def _experts_body(te_ref, nv_ref, x_ref, wg_ref, wu_ref, wd_ref, y_ref, wgu_scr, wd_scr, piece_scr,
                  *, tm):
    i = pl.program_id(0)
    e = te_ref[i]
    prev = te_ref[jnp.maximum(i - 1, 0)]

    @pl.when((i == 0) | (e != prev))
    def _():
        wgu_scr[:, 0:EXPERT_FF] = wg_ref[0].astype(BF16)
        wgu_scr[:, EXPERT_FF:] = wu_ref[0].astype(BF16)
        wd_scr[...] = wd_ref[0].astype(BF16)

    @pl.when(i < nv_ref[0])
    def _():
        x = _slab_load(x_ref, tm, piece_scr).astype(BF16)
        gu = jnp.dot(x, wgu_scr[...], preferred_element_type=F32)
        g = gu[:, 0:EXPERT_FF]
        act = (g * _sigmoid(g) * gu[:, EXPERT_FF:]).astype(BF16)
        _slab_store(y_ref, jnp.dot(act, wd_scr[...], preferred_element_type=F32), piece_scr)

    @pl.when(i >= nv_ref[0])
    def _():
        y_ref[...] = jnp.zeros(y_ref.shape, U32)


def _experts(tile_expert, n_valid, xs, w_gate, w_up, w_down):
    tm = MOE_TILE
    rows = tm * SLAB
    d = w_gate.shape[1]
    live = lambda i, te, nv: (jnp.minimum(i, nv[0] - 1), 0)
    grid_spec = pltpu.PrefetchScalarGridSpec(
        num_scalar_prefetch=2,
        grid=(xs.shape[0] // rows,),
        in_specs=[pl.BlockSpec((rows, LANES), live),
                  pl.BlockSpec((1, d, EXPERT_FF), lambda i, te, nv: (te[i], 0, 0)),
                  pl.BlockSpec((1, d, EXPERT_FF), lambda i, te, nv: (te[i], 0, 0)),
                  pl.BlockSpec((1, EXPERT_FF, d), lambda i, te, nv: (te[i], 0, 0))],
        out_specs=pl.BlockSpec((rows, LANES), lambda i, te, nv: (i, 0)),
        scratch_shapes=[pltpu.VMEM((d, 2 * EXPERT_FF), BF16),
                        pltpu.VMEM((EXPERT_FF, d), BF16),
                        pltpu.VMEM((tm * PIECES, LANES), F32)],
    )
    return pl.pallas_call(
        functools.partial(_experts_body, tm=tm),
        grid_spec=grid_spec,
        out_shape=jax.ShapeDtypeStruct(xs.shape, U32),
        compiler_params=pltpu.CompilerParams(dimension_semantics=("arbitrary",)),
        name="moe_experts",
    )(tile_expert, n_valid, xs, w_gate, w_up, w_down)


def _combine_body(pos_ref, nxt_ref, x1_ref, meta_ref, mod_ref, nw_ref, ys_hbm, o_ref, rows_scr,
                  piece_scr, sems,
                  *, tile):
    i = pl.program_id(0)
    n = pl.num_programs(0)
    rows = tile * SLAB

    def fetch(idx_ref, buf):
        def issue(t, carry):
            dst = pl.ds(pl.multiple_of(t * SLAB, SLAB), SLAB)
            for j in range(2):
                slot = pl.multiple_of(idx_ref[0, 0, 2 * t + j] * SLAB, SLAB)
                pltpu.make_async_copy(ys_hbm.at[pl.ds(slot, SLAB)], rows_scr.at[buf, j, dst],
                                      sems.at[buf]).start(priority=j)
            return carry
        lax.fori_loop(0, tile, issue, 0, unroll=DMA_UNROLL)

    @pl.when(i == 0)
    def _():
        fetch(pos_ref, 0)

    for cur in range(2):
        @pl.when((i % 2 == cur) & (i + 1 < n))
        def _():
            fetch(nxt_ref, 1 - cur)

    for cur in range(2):
        @pl.when(i % 2 == cur)
        def _():
            for j in range(2):
                pltpu.make_async_copy(ys_hbm.at[pl.ds(0, rows)], rows_scr.at[cur, j],
                                      sems.at[cur]).wait()
            meta = meta_ref[...]
            w0 = meta[:, META_W0:META_W0 + 1]
            w1 = meta[:, META_W1:META_W1 + 1]
            y = w0 * _slab_load(rows_scr.at[cur, 0], tile, piece_scr.at[0])
            y = y + w1 * _slab_load(rows_scr.at[cur, 1], tile, piece_scr.at[1])
            x2 = x1_ref[...] + mod_ref[0, 5:6, :] * y
            o_ref[...] = (x2 * lax.rsqrt(jnp.mean(x2 * x2, axis=-1, keepdims=True) + EPS)
                          * nw_ref[...])


def _combine(pos3, x1, meta, mod3, norm_w, ys, seq):
    t, d = x1.shape
    tile = GATHER_TILE
    per_b = seq // tile
    n = t // tile
    row = lambda i: (i, 0)
    return pl.pallas_call(
        functools.partial(_combine_body, tile=tile),
        grid=(n,),
        in_specs=[pl.BlockSpec((1, 1, 2 * tile), lambda i: (i, 0, 0), memory_space=pltpu.SMEM),
                  pl.BlockSpec((1, 1, 2 * tile), lambda i: (jnp.minimum(i + 1, n - 1), 0, 0),
                               memory_space=pltpu.SMEM),
                  pl.BlockSpec((tile, d), row),
                  pl.BlockSpec((tile, LANES), row),
                  pl.BlockSpec((1, 6, d), lambda i: (i // per_b, 0, 0)),
                  pl.BlockSpec((1, d), lambda i: (0, 0)),
                  pl.BlockSpec(memory_space=pl.ANY)],
        out_specs=pl.BlockSpec((tile, d), row),
        out_shape=jax.ShapeDtypeStruct((t, d), F32),
        scratch_shapes=[pltpu.VMEM((2, 2, tile * SLAB, LANES), U32),
                        pltpu.VMEM((2, tile * PIECES, LANES), F32),
                        pltpu.SemaphoreType.DMA((2,))],
        compiler_params=pltpu.CompilerParams(dimension_semantics=("arbitrary",)),
        name="moe_combine",
    )(pos3, pos3, x1, meta, mod3, norm_w, ys)


def _rope_inv_freq_lanes():
    inv_freq = ROPE_THETA ** (-jnp.arange(ROT_HALF, dtype=F32) / ROT_HALF)
    lane = np.arange(LANES) % DA_QK_DIM
    table = jnp.where(jnp.asarray(lane < ROT_DIM), inv_freq[lane % ROT_HALF], 0.0)
    return table.reshape(1, LANES).astype(F32)


def kernel(x, c, positions, norm1_w, norm2_w, final_norm_w, ada_w, ada_b, w_in, w_out, da_lambda_q1, da_lambda_k1, da_lambda_q2, da_lambda_k2, da_subln_w, hg_lower_bound, hg_norm_w, moe_w_group, moe_b_group, moe_w_router, moe_b_router, moe_w_gate, moe_w_up, moe_w_down):
    bsz, seq, d = x.shape
    assert d == D_MODEL and norm1_w.shape[0] == 1, "single-layer model of width 1024 only"
    assert seq % ATTN_TILE == 0 and seq % HGRN_STEP == 0 and seq % OUT_TILE == 0
    t = bsz * seq
    x2 = x.reshape(t, d)

    mod3 = _adaln(c, ada_w[0], ada_b).reshape(bsz, 6, d)

    pos_col = positions.astype(F32).reshape(t, 1)
    qt, k, vt, hq, lf, kf, gi, sg = _inproj(x2, mod3, norm1_w, w_in[0].astype(BF16), pos_col,
                                          _rope_inv_freq_lanes(), hg_lower_bound, seq)

    as3 = lambda a: a.reshape(bsz, seq, STREAM_W)
    lam_p = jnp.concatenate([da_lambda_q1, da_lambda_k1, da_lambda_q2, da_lambda_k2], axis=0)
    da = _attn(qt, as3(k), vt, lam_p, da_subln_w.reshape(HEAD_W, 1))
    hg = _hgrn(as3(hq), as3(lf), as3(kf), as3(gi), as3(sg), hg_norm_w)

    pad = jnp.zeros((d, LANES - N_EXPERTS - N_GROUPS), F32)
    w_route = jnp.concatenate([moe_w_router[0], moe_w_group[0], pad], axis=1)
    b_route = jnp.concatenate([moe_b_router[0], moe_b_group[0], pad[0]]).reshape(1, LANES)
    w_route_hi = w_route.astype(BF16)
    w_route_lo = (w_route - w_route_hi.astype(F32)).astype(BF16)
    x1, h2, meta, cnt = _outproj(da.reshape(t, STREAM_W), hg.reshape(t, STREAM_W), x2, mod3,
                                 w_out[0].astype(BF16), norm2_w,
                                 jnp.concatenate([w_route_hi, w_route_lo], axis=1), b_route, seq)

    counts = cnt[0, :N_EXPERTS].astype(jnp.int32)
    tiles_e = (counts + MOE_TILE - 1) // MOE_TILE
    tile_end = jnp.cumsum(tiles_e)
    offs = (tile_end - tiles_e) * MOE_TILE
    ids = meta[:, META_E0:META_E1 + 1].astype(jnp.int32)
    ranks = meta[:, META_R0:META_R1 + 1].astype(jnp.int32)
    expert_iota = jnp.arange(N_EXPERTS, dtype=jnp.int32)
    pos = jnp.sum(jnp.where(ids[..., None] == expert_iota, offs, 0), axis=-1) + ranks
    n_tiles = (2 * t) // MOE_TILE + N_EXPERTS
    n_valid = tile_end[-1:]
    tile_ids = jnp.minimum(jnp.arange(n_tiles, dtype=jnp.int32), n_valid - 1)
    tile_expert = jnp.sum(tile_ids[:, None] >= tile_end[None, :], axis=1).astype(jnp.int32)
    pos3 = pos.reshape(t // GATHER_TILE, 1, 2 * GATHER_TILE)
    last_tile = jnp.concatenate([jnp.where(tiles_e > 0, tile_end - 1, -1), n_valid]).astype(jnp.int32)

    xs = _scatter(last_tile, pos3, h2, n_tiles * MOE_TILE)
    ys = _experts(tile_expert, n_valid.astype(jnp.int32), xs, moe_w_gate[0], moe_w_up[0], moe_w_down[0])
    out = _combine(pos3, x1, meta, mod3, final_norm_w.reshape(1, d), ys, seq)
    return out.reshape(bsz, seq, d)
```

```python
import functools
import math

import numpy as np
import jax
import jax.numpy as jnp
from jax import lax
from jax.experimental import pallas as pl
from jax.experimental.pallas import tpu as pltpu

F32 = jnp.float32
BF16 = jnp.bfloat16
HIGHEST = lax.Precision.HIGHEST

LANES = 128
SUBLANES = 8
D_MODEL = 1024
HEADS = 4
HEAD_W = 128
STREAM_W = HEADS * HEAD_W
N_STREAMS = 7
DA_QK_DIM = 64
ROPE_THETA = 500000.0
ROT_DIM = DA_QK_DIM // 4
ROT_HALF = ROT_DIM // 2
N_GROUPS = 4
EXPERTS_PER_GROUP = 8
N_EXPERTS = N_GROUPS * EXPERTS_PER_GROUP
EXPERT_FF = 256
EPS = 1e-6
LAM_INIT = 0.8 - 0.6 * math.exp(-0.3 * 0)
NEG_BIG = -1e30

ROW_TILE = 256
OUT_TILE = 512
ATTN_TILE = 1024
ATTN_KEY_TILE = 512
HGRN_BLOCK = 128
HGRN_STEP = 256
MOE_TILE = 256
GATHER_TILE = 256

NT_DIMS = (((1,), (1,)), ((), ()))
TN_DIMS = (((0,), (0,)), ((), ()))


def _sigmoid(x):
    return 1.0 / (1.0 + jnp.exp(-x))


SLAB = D_MODEL // LANES


def _slab_store(ref, val):
    rows = val.shape[0]
    for c in range(SLAB):
        ref[pl.ds(c, rows, stride=SLAB), :] = val[:, c * LANES:(c + 1) * LANES]


def _slab_load(ref, rows):
    return jnp.concatenate([ref[pl.ds(c, rows, stride=SLAB), :] for c in range(SLAB)], axis=1)


def _adaln_body(c_ref, w_ref, b_ref, o_ref):
    c = c_ref[...]
    ca = c * _sigmoid(c)
    o_ref[...] = jnp.dot(ca, w_ref[...], preferred_element_type=F32, precision=HIGHEST) + b_ref[...]


def _adaln(c, ada_w, ada_b):
    bsz, d = c.shape
    n = ada_w.shape[1]
    tn = 1024
    return pl.pallas_call(
        _adaln_body,
        grid=(n // tn,),
        in_specs=[pl.BlockSpec((bsz, d), lambda j: (0, 0)),
                  pl.BlockSpec((d, tn), lambda j: (0, j)),
                  pl.BlockSpec((1, tn), lambda j: (0, j))],
        out_specs=pl.BlockSpec((bsz, tn), lambda j: (0, j)),
        out_shape=jax.ShapeDtypeStruct((bsz, n), F32),
        name="adaln",
    )(c, ada_w, ada_b)


def _inproj_body(x_ref, mod_ref, nw_ref, w_ref, pos_ref, invf_ref, spread_ref, lbr_ref,
                 q_ref, k_ref, v_ref, hq_ref, lf_ref, kf_ref, gi_ref, sg_ref):
    x = x_ref[...]
    ms = jnp.mean(x * x, axis=-1, keepdims=True)
    y = x * lax.rsqrt(ms + EPS) * nw_ref[...]
    shift = mod_ref[0, 0:1, :]
    scale = mod_ref[0, 1:2, :]
    h = (y * (1.0 + scale) + shift).astype(BF16)

    ang_t = invf_ref[...] * pos_ref[0]

    def spread(table_t):
        hi = table_t.astype(BF16)
        lo = (table_t - hi.astype(F32)).astype(BF16)
        return lax.dot_general(jnp.concatenate([hi, lo], axis=0), spread_ref[...], TN_DIMS,
                               preferred_element_type=F32)

    lane = lax.broadcasted_iota(jnp.int32, (1, LANES), 1) % DA_QK_DIM
    cosv = spread(jnp.cos(ang_t)) + jnp.where(lane < ROT_DIM, 0.0, 1.0)
    sinv = spread(jnp.sin(ang_t))
    sin_lo = jnp.where(lane < ROT_HALF, -sinv, 0.0)
    sin_hi = jnp.where((lane >= ROT_HALF) & (lane < ROT_DIM), sinv, 0.0)

    streams = [jnp.dot(h, w_ref[:, j * STREAM_W:(j + 1) * STREAM_W], preferred_element_type=F32)
               for j in range(N_STREAMS)]

    def proj(j):
        return streams[j]

    def rope(t):
        outs = []
        for hb in range(HEADS):
            tc = t[:, hb * HEAD_W:(hb + 1) * HEAD_W]
            outs.append(tc * cosv
                        + pltpu.roll(tc, LANES - ROT_HALF, 1) * sin_lo
                        + pltpu.roll(tc, ROT_HALF, 1) * sin_hi)
        return jnp.concatenate(outs, axis=1)

    q_ref[0] = (rope(proj(0)) * (DA_QK_DIM ** -0.5 * math.log2(math.e))).T.astype(BF16)
    k_ref[...] = rope(proj(1)).astype(BF16)
    v_ref[0] = proj(2).astype(BF16).T

    gq = proj(3)
    hq_ref[...] = gq * _sigmoid(gq)

    a = lbr_ref[...]
    amax = jnp.max(a, axis=0, keepdims=True)
    ea = jnp.exp(a - amax)
    lb = ea[0:1, :] / jnp.sum(ea, axis=0, keepdims=True)
    gf = proj(4)
    f = lb + (1.0 - lb) * _sigmoid(gf)
    lf_ref[...] = jnp.log(f) * math.log2(math.e)
    kf_ref[...] = 1.0 - f

    gi_ref[...] = proj(5).astype(BF16)
    gg = proj(6)
    sg_ref[...] = (gg * _sigmoid(gg)).astype(BF16)


def _inproj(x2, mod3, norm_w, w_bf, pos_rows, invf, spread, lb_raw, seq):
    t, d = x2.shape
    tm = ROW_TILE
    per_b = seq // tm
    row = lambda i: (i, 0)
    full = lambda i: (0, 0)
    out_bf = jax.ShapeDtypeStruct((t, STREAM_W), BF16)
    out_f = jax.ShapeDtypeStruct((t, STREAM_W), F32)
    out_t = jax.ShapeDtypeStruct((t // seq, STREAM_W, seq), BF16)
    stream = pl.BlockSpec((tm, STREAM_W), row)
    stream_t = pl.BlockSpec((1, STREAM_W, tm), lambda i: (i // per_b, 0, i % per_b))
    return pl.pallas_call(
        _inproj_body,
        grid=(t // tm,),
        in_specs=[pl.BlockSpec((tm, d), row),
                  pl.BlockSpec((1, 6, d), lambda i: (i // per_b, 0, 0)),
                  pl.BlockSpec((1, d), full),
                  pl.BlockSpec((d, N_STREAMS * STREAM_W), full),
                  pl.BlockSpec((1, 1, tm), lambda i: (i, 0, 0)),
                  pl.BlockSpec((ROT_HALF, 1), full),
                  pl.BlockSpec((2 * ROT_HALF, LANES), full),
                  pl.BlockSpec(lb_raw.shape, full)],
        out_specs=[stream_t, stream, stream_t] + [stream] * 5,
        out_shape=[out_t, out_bf, out_t, out_f, out_f, out_f, out_bf, out_bf],
        compiler_params=pltpu.CompilerParams(vmem_limit_bytes=56 * 1024 * 1024),
        name="inproj",
    )(x2, mod3, norm_w, w_bf, pos_rows, invf, spread, lb_raw)


ONES_ROWS = 16


def _attn_body(qt_ref, k_ref, vt_ref, lam_ref, sw_ref, o_ref, s_scr, m_scr, a_scr, *, tile, ktile):
    qi = pl.program_id(2)
    qt = qt_ref[0]
    feat = lax.broadcasted_iota(jnp.int32, (HEAD_W, 1), 0)
    zero = jnp.zeros_like(qt)
    qmaps = (jnp.where(feat < DA_QK_DIM, qt, zero), jnp.where(feat >= DA_QK_DIM, qt, zero))
    ones = jnp.ones((ONES_ROWS, ktile), BF16)

    m_scr[...] = jnp.full(m_scr.shape, NEG_BIG, F32)
    a_scr[...] = jnp.zeros(a_scr.shape, F32)

    ratio = tile // ktile
    n_full = qi * ratio

    def score_block(ki, diag_offset):
        start = pl.multiple_of(ki * ktile, ktile)
        kb = k_ref[0, pl.ds(start, ktile), :]
        scores = [jnp.dot(kb, qmaps[mp], preferred_element_type=F32) for mp in range(2)]
        for mp, s in enumerate(scores):
            if diag_offset is not None:
                key = lax.broadcasted_iota(jnp.int32, s.shape, 0) + diag_offset
                qry = lax.broadcasted_iota(jnp.int32, s.shape, 1)
                s = jnp.where(key <= qry, s, NEG_BIG)
            grouped = s.reshape(ktile // SUBLANES, SUBLANES, tile)
            m_scr[mp] = jnp.maximum(m_scr[mp], jnp.max(grouped, axis=0))
            s_scr[mp, pl.ds(start, ktile), :] = s

    def value_block(ki, col_max):
        start = pl.multiple_of(ki * ktile, ktile)
        vb = jnp.concatenate([vt_ref[0, :, pl.ds(start, ktile)], ones], axis=0)
        probs = [jnp.exp2(s_scr[mp, pl.ds(start, ktile), :] - col_max[mp]).astype(BF16)
                 for mp in range(2)]
        for mp, p in enumerate(probs):
            a_scr[mp] += jnp.dot(vb, p, preferred_element_type=F32)

    def score_body(ki, carry):
        score_block(ki, None)
        return carry

    lax.fori_loop(0, n_full, score_body, 0)
    for j in range(ratio):
        score_block(n_full + j, j * ktile)
    col_max = [jnp.max(m_scr[mp], axis=0, keepdims=True) for mp in range(2)]

    def value_body(ki, carry):
        value_block(ki, col_max)
        return carry

    lax.fori_loop(0, n_full + ratio, value_body, 0)

    lp = lam_ref[...]
    lam = (jnp.exp(jnp.sum(lp[0:1] * lp[1:2], axis=-1, keepdims=True))
           - jnp.exp(jnp.sum(lp[2:3] * lp[3:4], axis=-1, keepdims=True)) + LAM_INIT)
    o = (a_scr[0, 0:HEAD_W, :] / a_scr[0, HEAD_W:HEAD_W + 1, :]
         - lam * (a_scr[1, 0:HEAD_W, :] / a_scr[1, HEAD_W:HEAD_W + 1, :]))
    o = o * lax.rsqrt(jnp.mean(o * o, axis=0, keepdims=True) + EPS) * sw_ref[...]
    o_ref[0] = (o * (1.0 - LAM_INIT)).T.astype(BF16)


def _attn(qt3, k3, vt3, lam_p, subln_col):
    bsz, seq, _ = k3.shape
    tile = ATTN_TILE
    return pl.pallas_call(
        functools.partial(_attn_body, tile=tile, ktile=ATTN_KEY_TILE),
        grid=(bsz, HEADS, seq // tile),
        in_specs=[pl.BlockSpec((1, HEAD_W, tile), lambda b, h, i: (b, h, i)),
                  pl.BlockSpec((1, seq, HEAD_W), lambda b, h, i: (b, 0, h)),
                  pl.BlockSpec((1, HEAD_W, seq), lambda b, h, i: (b, h, 0)),
                  pl.BlockSpec(lam_p.shape, lambda b, h, i: (0, 0)),
                  pl.BlockSpec((HEAD_W, 1), lambda b, h, i: (0, 0))],
        out_specs=pl.BlockSpec((1, tile, HEAD_W), lambda b, h, i: (b, i, h)),
        out_shape=jax.ShapeDtypeStruct((bsz, seq, STREAM_W), BF16),
        scratch_shapes=[pltpu.VMEM((2, seq, tile), F32),
                        pltpu.VMEM((2, SUBLANES, tile), F32),
                        pltpu.VMEM((2, HEAD_W + ONES_ROWS, tile), F32)],
        compiler_params=pltpu.CompilerParams(vmem_limit_bytes=56 * 1024 * 1024),
        name="diff_attn",
    )(qt3, k3, vt3, lam_p, subln_col)


def _hgrn_levels(block):
    return [block >> (i + 1) for i in range(block.bit_length() - 1)]


def _hgrn_constants(block):
    t = np.arange(block)[:, None]
    s = np.arange(block)[None, :]
    tril = (s <= t).astype(np.float32)
    lv = np.full((block, block), -1, np.int32)
    halves = _hgrn_levels(block)
    for li, m in enumerate(halves):
        same = (t // (2 * m)) == (s // (2 * m))
        lv[same & ((t & m) != 0) & ((s & m) == 0)] = li
    lv[np.arange(block), np.arange(block)] = len(halves)
    return jnp.asarray(tril, BF16), jnp.asarray(lv)


def _level_operand(b_ref, h, b, q, k, m, block):
    def ref_rows(r, n):
        return jnp.broadcast_to(b_ref[h, pl.ds(r, 1), :], (n, HEAD_W))

    if m >= SUBLANES:
        pieces = []
        for s0 in range(0, block, 2 * m):
            ref = ref_rows(s0 + m - 1, m)
            lo = slice(s0, s0 + m)
            up = slice(s0 + m, s0 + 2 * m)
            pieces.append(k[lo] * jnp.exp2(ref - b[lo]))
            pieces.append(q[up] * jnp.exp2(b[up] - ref))
        return jnp.concatenate(pieces, axis=0)

    sub = lax.broadcasted_iota(jnp.int32, (SUBLANES, HEAD_W), 0)
    refs = []
    for s0 in range(0, block, SUBLANES):
        piece = ref_rows(s0 + m - 1, SUBLANES)
        for j in range(1, SUBLANES // (2 * m)):
            piece = jnp.where(sub >= 2 * m * j, ref_rows(s0 + 2 * m * j + m - 1, SUBLANES), piece)
        refs.append(piece)
    d = b - jnp.concatenate(refs, axis=0)
    row = lax.broadcasted_iota(jnp.int32, (block, 1), 0)
    return jnp.where((row & m) != 0, q, k) * jnp.exp2(jnp.minimum(d, -d))


def _hgrn_body(hq_ref, lf_ref, kf_ref, gi_ref, sg_ref, nw_ref, tril_ref, lv_ref, o_ref,
               st_scr, b_scr, *, block, step):
    @pl.when(pl.program_id(1) == 0)
    def _():
        st_scr[...] = jnp.zeros(st_scr.shape, F32)

    tril = tril_ref[...]
    lv = lv_ref[...]
    halves = _hgrn_levels(block)

    units = [(u, r0, h) for u, (r0, h) in enumerate(
        (r0, h) for r0 in range(0, step, block) for h in range(HEADS))]

    def cols(h):
        return slice(h * HEAD_W, (h + 1) * HEAD_W)

    q, k, v, b, scores = {}, {}, {}, {}, {}
    for u, r0, h in units:
        rows = slice(r0, r0 + block)
        q[u] = hq_ref[0, rows, cols(h)]
        k[u] = kf_ref[0, rows, cols(h)]
        v[u] = gi_ref[0, rows, cols(h)]
        lf = lf_ref[0, rows, cols(h)]
        hi = lf.astype(BF16)
        r1 = lf - hi.astype(F32)
        mid = r1.astype(BF16)
        lo = (r1 - mid.astype(F32)).astype(BF16)
        b[u] = (jnp.dot(tril, hi, preferred_element_type=F32)
                + jnp.dot(tril, mid, preferred_element_type=F32)
                + jnp.dot(tril, lo, preferred_element_type=F32))
        b_scr[u] = b[u]
        scores[u] = jnp.where(lv == len(halves),
                              lax.dot_general(q[u].astype(BF16), k[u].astype(BF16), NT_DIMS,
                                              preferred_element_type=F32), 0.0)

    for li, m in enumerate(halves):
        for u, r0, h in units:
            xl = _level_operand(b_scr, u, b[u], q[u], k[u], m, block).astype(BF16)
            p = lax.dot_general(xl, xl, NT_DIMS, preferred_element_type=F32)
            scores[u] = jnp.where(lv == li, p, scores[u])

    for u, r0, h in units:
        rows = slice(r0, r0 + block)
        o_intra = jnp.dot(scores[u].astype(BF16), v[u], preferred_element_type=F32)
        st = st_scr[h]
        o_inter = lax.dot_general((q[u] * jnp.exp2(b[u])).astype(BF16), st.astype(BF16), NT_DIMS,
                                  preferred_element_type=F32)
        b_last = b[u][block - 1:block, :]
        kdec = (k[u] * jnp.exp2(b_last - b[u])).astype(BF16)
        st_scr[h] = st * jnp.exp2(b_last) + lax.dot_general(v[u], kdec, TN_DIMS,
                                                             preferred_element_type=F32)
        o = o_inter + o_intra
        o = o * lax.rsqrt(jnp.mean(o * o, axis=-1, keepdims=True) + EPS) * nw_ref[...]
        o_ref[0, rows, cols(h)] = (o * sg_ref[0, rows, cols(h)].astype(F32)).astype(BF16)


def _hgrn(hq3, lf3, kf3, gi3, sg3, norm_w):
    bsz, seq, _ = hq3.shape
    block = HGRN_BLOCK
    step = HGRN_STEP
    tril, lv = _hgrn_constants(block)
    blk = pl.BlockSpec((1, step, STREAM_W), lambda b, g: (b, g, 0))
    const = lambda b, g: (0, 0)
    return pl.pallas_call(
        functools.partial(_hgrn_body, block=block, step=step),
        grid=(bsz, seq // step),
        in_specs=[blk, blk, blk, blk, blk,
                  pl.BlockSpec((1, HEAD_W), const),
                  pl.BlockSpec((block, block), const),
                  pl.BlockSpec((block, block), const)],
        out_specs=blk,
        out_shape=jax.ShapeDtypeStruct((bsz, seq, STREAM_W), BF16),
        scratch_shapes=[pltpu.VMEM((HEADS, HEAD_W, HEAD_W), F32),
                        pltpu.VMEM((HEADS * step // block, block, HEAD_W), F32)],
        name="hgrn2",
    )(hq3, lf3, kf3, gi3, sg3, norm_w, tril, lv)


META_E0, META_E1, META_R0, META_R1, META_W0, META_W1 = range(6)
GROUP_LANE0 = N_EXPERTS


def _outproj_body(da_ref, hg_ref, x_ref, mod_ref, wo_ref, nw_ref, wr_ref, br_ref, stril_ref,
                  x1_ref, h2_ref, meta_ref, cnt_ref, carry_scr):
    @pl.when(pl.program_id(0) == 0)
    def _():
        carry_scr[...] = jnp.zeros(carry_scr.shape, F32)

    attn = (jnp.dot(da_ref[...], wo_ref[0:STREAM_W, :], preferred_element_type=F32)
            + jnp.dot(hg_ref[...], wo_ref[STREAM_W:, :], preferred_element_type=F32))
    gate1 = mod_ref[0, 2:3, :]
    shift2 = mod_ref[0, 3:4, :]
    scale2 = mod_ref[0, 4:5, :]
    x1 = x_ref[...] + gate1 * attn
    x1_ref[...] = x1
    h2 = (x1 * lax.rsqrt(jnp.mean(x1 * x1, axis=-1, keepdims=True) + EPS) * nw_ref[...]
          * (1.0 + scale2) + shift2)
    _slab_store(h2_ref, h2)

    rows = h2.shape[0]
    h_hi = h2.astype(BF16)
    h_lo = (h2 - h_hi.astype(F32)).astype(BF16)
    parts = jnp.dot(jnp.concatenate([h_hi, h_lo], axis=0), wr_ref[...], preferred_element_type=F32)
    logits = ((parts[:rows, :LANES] + parts[:rows, LANES:])
              + (parts[rows:, :LANES] + parts[rows:, LANES:]) + br_ref[...])
    lane = lax.broadcasted_iota(jnp.int32, logits.shape, 1)
    far = jnp.int32(LANES)

    def first_max(vals):
        mx = jnp.max(vals, axis=-1, keepdims=True)
        return mx, jnp.min(jnp.where(vals == mx, lane, far), axis=-1, keepdims=True)

    is_g = (lane >= GROUP_LANE0) & (lane < GROUP_LANE0 + N_GROUPS)
    gmax, glane = first_max(jnp.where(is_g, logits, NEG_BIG))
    g_w = 1.0 / jnp.sum(jnp.where(is_g, jnp.exp(logits - gmax), 0.0), axis=-1, keepdims=True)
    gidx = glane - GROUP_LANE0
    in_grp = (lane < N_EXPERTS) & ((lane // EXPERTS_PER_GROUP) == gidx)
    el = jnp.where(in_grp, logits, NEG_BIG)
    m1, i1 = first_max(el)
    m2, i2 = first_max(jnp.where(lane == i1, NEG_BIG, el))
    r = jnp.exp(m2 - m1)
    w0 = g_w / (1.0 + r)
    w1 = g_w * r / (1.0 + r)

    hot0 = lane == i1
    hot1 = lane == i2
    multi = jnp.where(hot0 | hot1, 1.0, 0.0)
    before = jnp.dot(stril_ref[...], multi.astype(BF16), preferred_element_type=F32) + carry_scr[...]
    rank0 = jnp.sum(jnp.where(hot0, before, 0.0), axis=-1, keepdims=True)
    rank1 = jnp.sum(jnp.where(hot1, before, 0.0), axis=-1, keepdims=True)
    carry = carry_scr[...] + jnp.sum(multi, axis=0, keepdims=True)
    carry_scr[...] = carry
    cnt_ref[...] = carry

    meta = jnp.zeros(logits.shape, F32)
    for idx, val in ((META_E0, i1.astype(F32)), (META_E1, i2.astype(F32)),
                     (META_R0, rank0), (META_R1, rank1), (META_W0, w0), (META_W1, w1)):
        meta = jnp.where(lane == idx, val, meta)
    meta_ref[...] = meta


def _outproj(da2, hg2, x2, mod3, wo_bf, norm_w, w_route, b_route, seq):
    t, d = x2.shape
    tm = OUT_TILE
    per_b = seq // tm
    row = lambda i: (i, 0)
    full = lambda i: (0, 0)
    stril =jnp.asarray(np.tril(np.ones((tm, tm), np.float32), -1), BF16)
    return pl.pallas_call(
        _outproj_body,
        grid=(t // tm,),
        in_specs=[pl.BlockSpec((tm, STREAM_W), row),
                  pl.BlockSpec((tm, STREAM_W), row),
                  pl.BlockSpec((tm, d), row),
                  pl.BlockSpec((1, 6, d), lambda i: (i // per_b, 0, 0)),
                  pl.BlockSpec((2 * STREAM_W, d), full),
                  pl.BlockSpec((1, d), full),
                  pl.BlockSpec((d, 2 * LANES), full),
                  pl.BlockSpec((1, LANES), full),
                  pl.BlockSpec((tm, tm), full)],
        out_specs=[pl.BlockSpec((tm, d), row),
                   pl.BlockSpec((tm * SLAB, LANES), row),
                   pl.BlockSpec((tm, LANES), row),
                   pl.BlockSpec((1, LANES), full)],
        out_shape=[jax.ShapeDtypeStruct((t, d), F32),
                   jax.ShapeDtypeStruct((t * SLAB, LANES), F32),
                   jax.ShapeDtypeStruct((t, LANES), F32),
                   jax.ShapeDtypeStruct((1, LANES), F32)],
        scratch_shapes=[pltpu.VMEM((1, LANES), F32)],
        compiler_params=pltpu.CompilerParams(dimension_semantics=("arbitrary",)),
        name="outproj_route",
    )(da2, hg2, x2, mod3, wo_bf, norm_w, w_route, b_route, stril)


DMA_UNROLL = 8


def _scatter_body(last_ref, pos_ref, h2_ref, xs_hbm, zero_scr, sem, zsem, *, tile):
    rows = tile * SLAB

    @pl.when(pl.program_id(0) == 0)
    def _():
        zero_scr[...] = jnp.zeros(zero_scr.shape, F32)

        def clear(tile_idx):
            start = pl.multiple_of(tile_idx * (MOE_TILE * SLAB), SLAB)
            return pltpu.make_async_copy(zero_scr, xs_hbm.at[pl.ds(start, MOE_TILE * SLAB)], zsem)

        for e in range(N_EXPERTS):
            @pl.when(last_ref[e] >= 0)
            def _():
                clear(last_ref[e]).start()
        for e in range(N_EXPERTS):
            @pl.when(last_ref[e] >= 0)
            def _():
                clear(last_ref[e]).wait()

        def clear_unused(tile_idx, carry):
            clear(tile_idx).start()
            clear(tile_idx).wait()
            return carry

        lax.fori_loop(last_ref[N_EXPERTS], xs_hbm.shape[0] // (MOE_TILE * SLAB), clear_unused, 0)

    def issue(t, carry):
        src = h2_ref.at[pl.ds(pl.multiple_of(t * SLAB, SLAB), SLAB)]
        for j in range(2):
            slot = pl.multiple_of(pos_ref[0, 0, 2 * t + j] * SLAB, SLAB)
            pltpu.make_async_copy(src, xs_hbm.at[pl.ds(slot, SLAB)], sem).start(priority=j)
        return carry

    lax.fori_loop(0, tile, issue, 0, unroll=DMA_UNROLL)
    whole = pltpu.make_async_copy(h2_ref, xs_hbm.at[pl.ds(0, rows)], sem)
    whole.wait()
    whole.wait()


def _scatter(last_tile, pos3, h2s, n_slots):
    tile = GATHER_TILE
    rows = tile * SLAB
    grid_spec = pltpu.PrefetchScalarGridSpec(
        num_scalar_prefetch=1,
        grid=(h2s.shape[0] // rows,),
        in_specs=[pl.BlockSpec((1, 1, 2 * tile), lambda i, lt: (i, 0, 0), memory_space=pltpu.SMEM),
                  pl.BlockSpec((rows, LANES), lambda i, lt: (i, 0))],
        out_specs=pl.BlockSpec(memory_space=pl.ANY),
        scratch_shapes=[pltpu.VMEM((MOE_TILE * SLAB, LANES), F32),
                        pltpu.SemaphoreType.DMA, pltpu.SemaphoreType.DMA],
    )
    return pl.pallas_call(
        functools.partial(_scatter_body, tile=tile),
        grid_spec=grid_spec,
        out_shape=jax.ShapeDtypeStruct((n_slots * SLAB, LANES), F32),
        compiler_params=pltpu.CompilerParams(dimension_semantics=("arbitrary",)),
        name="moe_scatter",
    )(last_tile, pos3, h2s)


def _experts_body(te_ref, nv_ref, x_ref, wg_ref, wu_ref, wd_ref, y_ref, wgu_scr, wd_scr, *, tm):
    i = pl.program_id(0)
    e = te_ref[i]
    prev = te_ref[jnp.maximum(i - 1, 0)]

    @pl.when((i == 0) | (e != prev))
    def _():
        wgu_scr[:, 0:EXPERT_FF] = wg_ref[0].astype(BF16)
        wgu_scr[:, EXPERT_FF:] = wu_ref[0].astype(BF16)
        wd_scr[...] = wd_ref[0].astype(BF16)

    @pl.when(i < nv_ref[0])
    def _():
        half = tm // 2
        xr = [x_ref.at[pl.ds(r * half * SLAB, half * SLAB)] for r in range(2)]
        yr = [y_ref.at[pl.ds(r * half * SLAB, half * SLAB)] for r in range(2)]
        xs = [_slab_load(ref, half).astype(BF16) for ref in xr]
        gus = [jnp.dot(x, wgu_scr[...], preferred_element_type=F32) for x in xs]
        acts = [(gu[:, 0:EXPERT_FF] * _sigmoid(gu[:, 0:EXPERT_FF]) * gu[:, EXPERT_FF:]).astype(BF16)
                for gu in gus]
        ys = [jnp.dot(act, wd_scr[...], preferred_element_type=F32) for act in acts]
        for ref, y in zip(yr, ys):
            _slab_store(ref, y)

    @pl.when(i >= nv_ref[0])
    def _():
        y_ref[...] = jnp.zeros(y_ref.shape, F32)


def _experts(tile_expert, n_valid, xs, w_gate, w_up, w_down):
    tm = MOE_TILE
    rows = tm * SLAB
    d = w_gate.shape[1]
    live = lambda i, te, nv: (jnp.minimum(i, nv[0] - 1), 0)
    grid_spec = pltpu.PrefetchScalarGridSpec(
        num_scalar_prefetch=2,
        grid=(xs.shape[0] // rows,),
        in_specs=[pl.BlockSpec((rows, LANES), live),
                  pl.BlockSpec((1, d, EXPERT_FF), lambda i, te, nv: (te[i], 0, 0)),
                  pl.BlockSpec((1, d, EXPERT_FF), lambda i, te, nv: (te[i], 0, 0)),
                  pl.BlockSpec((1, EXPERT_FF, d), lambda i, te, nv: (te[i], 0, 0))],
        out_specs=pl.BlockSpec((rows, LANES), lambda i, te, nv: (i, 0)),
        scratch_shapes=[pltpu.VMEM((d, 2 * EXPERT_FF), BF16),
                        pltpu.VMEM((EXPERT_FF, d), BF16)],
    )
    return pl.pallas_call(
        functools.partial(_experts_body, tm=tm),
        grid_spec=grid_spec,
        out_shape=jax.ShapeDtypeStruct(xs.shape, F32),
        compiler_params=pltpu.CompilerParams(dimension_semantics=("arbitrary",)),
        name="moe_experts",
    )(tile_expert, n_valid, xs, w_gate, w_up, w_down)


def _combine_body(pos_ref, nxt_ref, x1_ref, meta_ref, mod_ref, nw_ref, ys_hbm, o_ref, rows_scr, sems,
                  *, tile):
    i = pl.program_id(0)
    n = pl.num_programs(0)
    rows = tile * SLAB

    def fetch(idx_ref, buf):
        def issue(t, carry):
            dst = pl.ds(pl.multiple_of(t * SLAB, SLAB), SLAB)
            for j in range(2):
                slot = pl.multiple_of(idx_ref[0, 0, 2 * t + j] * SLAB, SLAB)
                pltpu.make_async_copy(ys_hbm.at[pl.ds(slot, SLAB)], rows_scr.at[buf, j, dst],
                                      sems.at[buf]).start(priority=j)
            return carry
        lax.fori_loop(0, tile, issue, 0, unroll=DMA_UNROLL)

    @pl.when(i == 0)
    def _():
        fetch(pos_ref, 0)

    for cur in range(2):
        @pl.when((i % 2 == cur) & (i + 1 < n))
        def _():
            fetch(nxt_ref, 1 - cur)

    for cur in range(2):
        @pl.when(i % 2 == cur)
        def _():
            for j in range(2):
                pltpu.make_async_copy(ys_hbm.at[pl.ds(0, rows)], rows_scr.at[cur, j],
                                      sems.at[cur]).wait()
            meta = meta_ref[...]
            w0 = meta[:, META_W0:META_W0 + 1]
            w1 = meta[:, META_W1:META_W1 + 1]
            y = (w0 * _slab_load(rows_scr.at[cur, 0], tile)
                 + w1 * _slab_load(rows_scr.at[cur, 1], tile))
            x2 = x1_ref[...] + mod_ref[0, 5:6, :] * y
            o_ref[...] = (x2 * lax.rsqrt(jnp.mean(x2 * x2, axis=-1, keepdims=True) + EPS)
                          * nw_ref[...])


def _combine(pos3, x1, meta, mod3, norm_w, ys, seq):
    t, d = x1.shape
    tile = GATHER_TILE
    per_b = seq // tile
    n = t // tile
    row = lambda i: (i, 0)
    return pl.pallas_call(
        functools.partial(_combine_body, tile=tile),
        grid=(n,),
        in_specs=[pl.BlockSpec((1, 1, 2 * tile), lambda i: (i, 0, 0), memory_space=pltpu.SMEM),
                  pl.BlockSpec((1, 1, 2 * tile), lambda i: (jnp.minimum(i + 1, n - 1), 0, 0),
                               memory_space=pltpu.SMEM),
                  pl.BlockSpec((tile, d), row),
                  pl.BlockSpec((tile, LANES), row),
                  pl.BlockSpec((1, 6, d), lambda i: (i // per_b, 0, 0)),
                  pl.BlockSpec((1, d), lambda i: (0, 0)),
                  pl.BlockSpec(memory_space=pl.ANY)],
        out_specs=pl.BlockSpec((tile, d), row),
        out_shape=jax.ShapeDtypeStruct((t, d), F32),
        scratch_shapes=[pltpu.VMEM((2, 2, tile * SLAB, LANES), F32), pltpu.SemaphoreType.DMA((2,))],
        compiler_params=pltpu.CompilerParams(dimension_semantics=("arbitrary",)),
        name="moe_combine",
    )(pos3, pos3, x1, meta, mod3, norm_w, ys)


def _rope_constants():
    inv_freq = ROPE_THETA ** (-jnp.arange(ROT_HALF, dtype=F32) / ROT_HALF)
    lane = np.arange(LANES) % DA_QK_DIM
    hit = (lane[None, :] < ROT_DIM) & (lane[None, :] % ROT_HALF == np.arange(ROT_HALF)[:, None])
    spread = np.concatenate([hit, hit], axis=0).astype(np.float32)
    return inv_freq.reshape(ROT_HALF, 1), jnp.asarray(spread, BF16)


def kernel(x, c, positions, norm1_w, norm2_w, final_norm_w, ada_w, ada_b, w_in, w_out, da_lambda_q1, da_lambda_k1, da_lambda_q2, da_lambda_k2, da_subln_w, hg_lower_bound, hg_norm_w, moe_w_group, moe_b_group, moe_w_router, moe_b_router, moe_w_gate, moe_w_up, moe_w_down):
    bsz, seq, d = x.shape
    assert d == D_MODEL and norm1_w.shape[0] == 1, "single-layer model of width 1024 only"
    assert seq % ATTN_TILE == 0 and seq % HGRN_STEP == 0 and seq % OUT_TILE == 0
    t = bsz * seq
    x2 = x.reshape(t, d)

    mod3 = _adaln(c, ada_w[0], ada_b).reshape(bsz, 6, d)

    pos_rows = positions.astype(F32).reshape(t // ROW_TILE, 1, ROW_TILE)
    inv_freq, spread = _rope_constants()
    qt, k, vt, hq, lf, kf, gi, sg = _inproj(x2, mod3, norm1_w, w_in[0].astype(BF16), pos_rows,
                                          inv_freq, spread, hg_lower_bound, seq)

    as3 = lambda a: a.reshape(bsz, seq, STREAM_W)
    lam_p = jnp.concatenate([da_lambda_q1, da_lambda_k1, da_lambda_q2, da_lambda_k2], axis=0)
    da = _attn(qt, as3(k), vt, lam_p, da_subln_w.reshape(HEAD_W, 1))
    hg = _hgrn(as3(hq), as3(lf), as3(kf), as3(gi), as3(sg), hg_norm_w)

    pad = jnp.zeros((d, LANES - N_EXPERTS - N_GROUPS), F32)
    w_route = jnp.concatenate([moe_w_router[0], moe_w_group[0], pad], axis=1)
    b_route = jnp.concatenate([moe_b_router[0], moe_b_group[0], pad[0]]).reshape(1, LANES)
    w_route_hi = w_route.astype(BF16)
    w_route_lo = (w_route - w_route_hi.astype(F32)).astype(BF16)
    x1, h2, meta, cnt = _outproj(da.reshape(t, STREAM_W), hg.reshape(t, STREAM_W), x2, mod3,
                                 w_out[0].astype(BF16), norm2_w,
                                 jnp.concatenate([w_route_hi, w_route_lo], axis=1), b_route, seq)

    counts = cnt[0, :N_EXPERTS].astype(jnp.int32)
    tiles_e = (counts + MOE_TILE - 1) // MOE_TILE
    tile_end = jnp.cumsum(tiles_e)
    offs = (tile_end - tiles_e) * MOE_TILE
    ids = meta[:, META_E0:META_E1 + 1].astype(jnp.int32)
    ranks = meta[:, META_R0:META_R1 + 1].astype(jnp.int32)
    expert_iota = jnp.arange(N_EXPERTS, dtype=jnp.int32)
    pos = jnp.sum(jnp.where(ids[..., None] == expert_iota, offs, 0), axis=-1) + ranks
    n_tiles = (2 * t) // MOE_TILE + N_EXPERTS
    n_valid = tile_end[-1:]
    tile_ids = jnp.minimum(jnp.arange(n_tiles, dtype=jnp.int32), n_valid - 1)
    tile_expert = jnp.sum(tile_ids[:, None] >= tile_end[None, :], axis=1).astype(jnp.int32)
    pos3 = pos.reshape(t // GATHER_TILE, 1, 2 * GATHER_TILE)
    last_tile = jnp.concatenate([jnp.where(tiles_e > 0, tile_end - 1, -1), n_valid]).astype(jnp.int32)

    xs = _scatter(last_tile, pos3, h2, n_tiles * MOE_TILE)
    ys = _experts(tile_expert, n_valid.astype(jnp.int32), xs, moe_w_gate[0], moe_w_up[0], moe_w_down[0])
    out = _combine(pos3, x1, meta, mod3, final_norm_w.reshape(1, d), ys, seq)
    return out.reshape(bsz, seq, d)
```

```python
import functools
import math

import numpy as np
import jax
import jax.numpy as jnp
from jax import lax
from jax.experimental import pallas as pl
from jax.experimental.pallas import tpu as pltpu

F32 = jnp.float32
BF16 = jnp.bfloat16
HIGHEST = lax.Precision.HIGHEST

LANES = 128
SUBLANES = 8
D_MODEL = 1024
HEADS = 4
HEAD_W = 128
STREAM_W = HEADS * HEAD_W
N_STREAMS = 7
DA_QK_DIM = 64
ROPE_THETA = 500000.0
ROT_DIM = DA_QK_DIM // 4
ROT_HALF = ROT_DIM // 2
N_GROUPS = 4
EXPERTS_PER_GROUP = 8
N_EXPERTS = N_GROUPS * EXPERTS_PER_GROUP
EXPERT_FF = 256
EPS = 1e-6
LAM_INIT = 0.8 - 0.6 * math.exp(-0.3 * 0)
NEG_BIG = -1e30

ROW_TILE = 256
OUT_TILE = 512
ATTN_TILE = 1024
ATTN_KEY_TILE = 512
HGRN_BLOCK = 128
HGRN_STEP = 256
MOE_TILE = 256
GATHER_TILE = 256

NT_DIMS = (((1,), (1,)), ((), ()))
TN_DIMS = (((0,), (0,)), ((), ()))


def _sigmoid(x):
    return 1.0 / (1.0 + jnp.exp(-x))


SLAB = D_MODEL // LANES


def _slab_store(ref, val):
    rows = val.shape[0]
    for c in range(SLAB):
        ref[pl.ds(c, rows, stride=SLAB), :] = val[:, c * LANES:(c + 1) * LANES]


def _slab_load(ref, rows):
    return jnp.concatenate([ref[pl.ds(c, rows, stride=SLAB), :] for c in range(SLAB)], axis=1)


def _adaln_body(c_ref, w_ref, b_ref, o_ref):
    c = c_ref[...]
    ca = c * _sigmoid(c)
    o_ref[...] = jnp.dot(ca, w_ref[...], preferred_element_type=F32, precision=HIGHEST) + b_ref[...]


def _adaln(c, ada_w, ada_b):
    bsz, d = c.shape
    n = ada_w.shape[1]
    tn = 1024
    return pl.pallas_call(
        _adaln_body,
        grid=(n // tn,),
        in_specs=[pl.BlockSpec((bsz, d), lambda j: (0, 0)),
                  pl.BlockSpec((d, tn), lambda j: (0, j)),
                  pl.BlockSpec((1, tn), lambda j: (0, j))],
        out_specs=pl.BlockSpec((bsz, tn), lambda j: (0, j)),
        out_shape=jax.ShapeDtypeStruct((bsz, n), F32),
        name="adaln",
    )(c, ada_w, ada_b)


def _inproj_body(x_ref, mod_ref, nw_ref, w_ref, pos_ref, invf_ref, spread_ref, lbr_ref,
                 q_ref, k_ref, v_ref, hq_ref, lf_ref, kf_ref, gi_ref, sg_ref):
    x = x_ref[...]
    ms = jnp.mean(x * x, axis=-1, keepdims=True)
    y = x * lax.rsqrt(ms + EPS) * nw_ref[...]
    shift = mod_ref[0, 0:1, :]
    scale = mod_ref[0, 1:2, :]
    h = (y * (1.0 + scale) + shift).astype(BF16)

    ang_t = invf_ref[...] * pos_ref[0]

    def spread(table_t):
        hi = table_t.astype(BF16)
        lo = (table_t - hi.astype(F32)).astype(BF16)
        return lax.dot_general(jnp.concatenate([hi, lo], axis=0), spread_ref[...], TN_DIMS,
                               preferred_element_type=F32)

    lane = lax.broadcasted_iota(jnp.int32, (1, LANES), 1) % DA_QK_DIM
    cosv = spread(jnp.cos(ang_t)) + jnp.where(lane < ROT_DIM, 0.0, 1.0)
    sinv = spread(jnp.sin(ang_t))
    sin_lo = jnp.where(lane < ROT_HALF, -sinv, 0.0)
    sin_hi = jnp.where((lane >= ROT_HALF) & (lane < ROT_DIM), sinv, 0.0)

    streams = [jnp.dot(h, w_ref[:, j * STREAM_W:(j + 1) * STREAM_W], preferred_element_type=F32)
               for j in range(N_STREAMS)]

    def proj(j):
        return streams[j]

    def rope(t):
        outs = []
        for hb in range(HEADS):
            tc = t[:, hb * HEAD_W:(hb + 1) * HEAD_W]
            outs.append(tc * cosv
                        + pltpu.roll(tc, LANES - ROT_HALF, 1) * sin_lo
                        + pltpu.roll(tc, ROT_HALF, 1) * sin_hi)
        return jnp.concatenate(outs, axis=1)

    q_ref[0] = (rope(proj(0)) * (DA_QK_DIM ** -0.5 * math.log2(math.e))).T.astype(BF16)
    k_ref[...] = rope(proj(1)).astype(BF16)
    v_ref[0] = proj(2).astype(BF16).T

    gq = proj(3)
    hq_ref[...] = gq * _sigmoid(gq)

    a = lbr_ref[...]
    amax = jnp.max(a, axis=0, keepdims=True)
    ea = jnp.exp(a - amax)
    lb = ea[0:1, :] / jnp.sum(ea, axis=0, keepdims=True)
    gf = proj(4)
    f = lb + (1.0 - lb) * _sigmoid(gf)
    lf_ref[...] = jnp.log(f) * math.log2(math.e)
    kf_ref[...] = 1.0 - f

    gi_ref[...] = proj(5).astype(BF16)
    gg = proj(6)
    sg_ref[...] = (gg * _sigmoid(gg)).astype(BF16)


def _inproj(x2, mod3, norm_w, w_bf, pos_rows, invf, spread, lb_raw, seq):
    t, d = x2.shape
    tm = ROW_TILE
    per_b = seq // tm
    row = lambda i: (i, 0)
    full = lambda i: (0, 0)
    out_bf = jax.ShapeDtypeStruct((t, STREAM_W), BF16)
    out_f = jax.ShapeDtypeStruct((t, STREAM_W), F32)
    out_t = jax.ShapeDtypeStruct((t // seq, STREAM_W, seq), BF16)
    stream = pl.BlockSpec((tm, STREAM_W), row)
    stream_t = pl.BlockSpec((1, STREAM_W, tm), lambda i: (i // per_b, 0, i % per_b))
    return pl.pallas_call(
        _inproj_body,
        grid=(t // tm,),
        in_specs=[pl.BlockSpec((tm, d), row),
                  pl.BlockSpec((1, 6, d), lambda i: (i // per_b, 0, 0)),
                  pl.BlockSpec((1, d), full),
                  pl.BlockSpec((d, N_STREAMS * STREAM_W), full),
                  pl.BlockSpec((1, 1, tm), lambda i: (i, 0, 0)),
                  pl.BlockSpec((ROT_HALF, 1), full),
                  pl.BlockSpec((2 * ROT_HALF, LANES), full),
                  pl.BlockSpec(lb_raw.shape, full)],
        out_specs=[stream_t, stream, stream_t] + [stream] * 5,
        out_shape=[out_t, out_bf, out_t, out_f, out_f, out_f, out_bf, out_bf],
        compiler_params=pltpu.CompilerParams(vmem_limit_bytes=56 * 1024 * 1024),
        name="inproj",
    )(x2, mod3, norm_w, w_bf, pos_rows, invf, spread, lb_raw)


ONES_ROWS = 16


def _attn_body(qt_ref, k_ref, vt_ref, lam_ref, sw_ref, o_ref, s_scr, m_scr, a_scr, *, tile, ktile):
    qi = pl.program_id(2)
    qt = qt_ref[0]
    feat = lax.broadcasted_iota(jnp.int32, (HEAD_W, 1), 0)
    zero = jnp.zeros_like(qt)
    qmaps = (jnp.where(feat < DA_QK_DIM, qt, zero), jnp.where(feat >= DA_QK_DIM, qt, zero))

    m_scr[...] = jnp.full(m_scr.shape, NEG_BIG, F32)
    a_scr[...] = jnp.zeros(a_scr.shape, F32)

    def score_block(start, nkeys, q_lo, masked):
        kb = k_ref[0, pl.ds(start, nkeys), :]
        scores = [jnp.dot(kb, qmaps[mp][:, q_lo:], preferred_element_type=F32) for mp in range(2)]
        for mp, s in enumerate(scores):
            if masked:
                key = lax.broadcasted_iota(jnp.int32, s.shape, 0)
                qry = lax.broadcasted_iota(jnp.int32, s.shape, 1)
                s = jnp.where(key <= qry, s, NEG_BIG)
            grouped = s.reshape(nkeys // SUBLANES, SUBLANES, tile - q_lo)
            m_scr[mp, :, q_lo:] = jnp.maximum(m_scr[mp, :, q_lo:], jnp.max(grouped, axis=0))
            s_scr[mp, pl.ds(start, nkeys), q_lo:] = s

    def value_block(start, nkeys, q_lo, col_max):
        vb = jnp.concatenate([vt_ref[0, :, pl.ds(start, nkeys)], jnp.ones((ONES_ROWS, nkeys), BF16)],
                             axis=0)
        probs = [jnp.exp2(s_scr[mp, pl.ds(start, nkeys), q_lo:] - col_max[mp][:, q_lo:]).astype(BF16)
                 for mp in range(2)]
        for mp, p in enumerate(probs):
            a_scr[mp, :, q_lo:] += jnp.dot(vb, p, preferred_element_type=F32)

    diag = pl.multiple_of(qi * tile, tile)
    pieces = [(diag + j * ktile, ktile, j * ktile) for j in range(tile // ktile)]

    def score_body(ki, carry):
        score_block(pl.multiple_of(ki * tile, tile), tile, 0, False)
        return carry

    lax.fori_loop(0, qi, score_body, 0)
    for start, nkeys, q_lo in pieces:
        score_block(start, nkeys, q_lo, True)
    col_max = [jnp.max(m_scr[mp], axis=0, keepdims=True) for mp in range(2)]

    def value_body(ki, carry):
        value_block(pl.multiple_of(ki * tile, tile), tile, 0, col_max)
        return carry

    lax.fori_loop(0, qi, value_body, 0)
    for start, nkeys, q_lo in pieces:
        value_block(start, nkeys, q_lo, col_max)

    lp = lam_ref[...]
    lam = (jnp.exp(jnp.sum(lp[0:1] * lp[1:2], axis=-1, keepdims=True))
           - jnp.exp(jnp.sum(lp[2:3] * lp[3:4], axis=-1, keepdims=True)) + LAM_INIT)
    o = (a_scr[0, 0:HEAD_W, :] / a_scr[0, HEAD_W:HEAD_W + 1, :]
         - lam * (a_scr[1, 0:HEAD_W, :] / a_scr[1, HEAD_W:HEAD_W + 1, :]))
    o = o * lax.rsqrt(jnp.mean(o * o, axis=0, keepdims=True) + EPS) * sw_ref[...]
    o_ref[0] = (o * (1.0 - LAM_INIT)).T.astype(BF16)


def _attn(qt3, k3, vt3, lam_p, subln_col):
    bsz, seq, _ = k3.shape
    tile = ATTN_TILE
    return pl.pallas_call(
        functools.partial(_attn_body, tile=tile, ktile=ATTN_KEY_TILE),
        grid=(bsz, HEADS, seq // tile),
        in_specs=[pl.BlockSpec((1, HEAD_W, tile), lambda b, h, i: (b, h, i)),
                  pl.BlockSpec((1, seq, HEAD_W), lambda b, h, i: (b, 0, h)),
                  pl.BlockSpec((1, HEAD_W, seq), lambda b, h, i: (b, h, 0)),
                  pl.BlockSpec(lam_p.shape, lambda b, h, i: (0, 0)),
                  pl.BlockSpec((HEAD_W, 1), lambda b, h, i: (0, 0))],
        out_specs=pl.BlockSpec((1, tile, HEAD_W), lambda b, h, i: (b, i, h)),
        out_shape=jax.ShapeDtypeStruct((bsz, seq, STREAM_W), BF16),
        scratch_shapes=[pltpu.VMEM((2, seq, tile), F32),
                        pltpu.VMEM((2, SUBLANES, tile), F32),
                        pltpu.VMEM((2, HEAD_W + ONES_ROWS, tile), F32)],
        compiler_params=pltpu.CompilerParams(vmem_limit_bytes=56 * 1024 * 1024),
        name="diff_attn",
    )(qt3, k3, vt3, lam_p, subln_col)


def _hgrn_levels(block):
    return [block >> (i + 1) for i in range(block.bit_length() - 1)]


def _hgrn_constants(block):
    t = np.arange(block)[:, None]
    s = np.arange(block)[None, :]
    tril = (s <= t).astype(np.float32)
    lv = np.full((block, block), -1, np.int32)
    halves = _hgrn_levels(block)
    for li, m in enumerate(halves):
        same = (t // (2 * m)) == (s // (2 * m))
        lv[same & ((t & m) != 0) & ((s & m) == 0)] = li
    lv[np.arange(block), np.arange(block)] = len(halves)
    return jnp.asarray(tril, BF16), jnp.asarray(lv)


def _level_operand(b_ref, h, b, q, k, m, block):
    def ref_rows(r, n):
        return jnp.broadcast_to(b_ref[h, pl.ds(r, 1), :], (n, HEAD_W))

    if m >= SUBLANES:
        pieces = []
        for s0 in range(0, block, 2 * m):
            ref = ref_rows(s0 + m - 1, m)
            lo = slice(s0, s0 + m)
            up = slice(s0 + m, s0 + 2 * m)
            pieces.append(k[lo] * jnp.exp2(ref - b[lo]))
            pieces.append(q[up] * jnp.exp2(b[up] - ref))
        return jnp.concatenate(pieces, axis=0)

    sub = lax.broadcasted_iota(jnp.int32, (SUBLANES, HEAD_W), 0)
    refs = []
    for s0 in range(0, block, SUBLANES):
        piece = ref_rows(s0 + m - 1, SUBLANES)
        for j in range(1, SUBLANES // (2 * m)):
            piece = jnp.where(sub >= 2 * m * j, ref_rows(s0 + 2 * m * j + m - 1, SUBLANES), piece)
        refs.append(piece)
    d = b - jnp.concatenate(refs, axis=0)
    row = lax.broadcasted_iota(jnp.int32, (block, 1), 0)
    return jnp.where((row & m) != 0, q, k) * jnp.exp2(jnp.minimum(d, -d))


def _hgrn_body(hq_ref, lf_ref, kf_ref, gi_ref, sg_ref, nw_ref, tril_ref, lv_ref, o_ref,
               st_scr, b_scr, *, block, step):
    @pl.when(pl.program_id(1) == 0)
    def _():
        st_scr[...] = jnp.zeros(st_scr.shape, F32)

    tril = tril_ref[...]
    lv = lv_ref[...]
    halves = _hgrn_levels(block)

    units = [(u, r0, h) for u, (r0, h) in enumerate(
        (r0, h) for r0 in range(0, step, block) for h in range(HEADS))]

    def cols(h):
        return slice(h * HEAD_W, (h + 1) * HEAD_W)

    q, k, v, b, scores = {}, {}, {}, {}, {}
    for u, r0, h in units:
        rows = slice(r0, r0 + block)
        q[u] = hq_ref[0, rows, cols(h)]
        k[u] = kf_ref[0, rows, cols(h)]
        v[u] = gi_ref[0, rows, cols(h)]
        lf = lf_ref[0, rows, cols(h)]
        hi = lf.astype(BF16)
        r1 = lf - hi.astype(F32)
        mid = r1.astype(BF16)
        lo = (r1 - mid.astype(F32)).astype(BF16)
        b[u] = (jnp.dot(tril, hi, preferred_element_type=F32)
                + jnp.dot(tril, mid, preferred_element_type=F32)
                + jnp.dot(tril, lo, preferred_element_type=F32))
        b_scr[u] = b[u]
        scores[u] = jnp.where(lv == len(halves),
                              lax.dot_general(q[u].astype(BF16), k[u].astype(BF16), NT_DIMS,
                                              preferred_element_type=F32), 0.0)

    for li, m in enumerate(halves):
        for u, r0, h in units:
            xl = _level_operand(b_scr, u, b[u], q[u], k[u], m, block).astype(BF16)
            p = lax.dot_general(xl, xl, NT_DIMS, preferred_element_type=F32)
            scores[u] = jnp.where(lv == li, p, scores[u])

    for u, r0, h in units:
        rows = slice(r0, r0 + block)
        o_intra = jnp.dot(scores[u].astype(BF16), v[u], preferred_element_type=F32)
        st = st_scr[h]
        o_inter = lax.dot_general((q[u] * jnp.exp2(b[u])).astype(BF16), st.astype(BF16), NT_DIMS,
                                  preferred_element_type=F32)
        b_last = b[u][block - 1:block, :]
        kdec = (k[u] * jnp.exp2(b_last - b[u])).astype(BF16)
        st_scr[h] = st * jnp.exp2(b_last) + lax.dot_general(v[u], kdec, TN_DIMS,
                                                             preferred_element_type=F32)
        o = o_inter + o_intra
        o = o * lax.rsqrt(jnp.mean(o * o, axis=-1, keepdims=True) + EPS) * nw_ref[...]
        o_ref[0, rows, cols(h)] = (o * sg_ref[0, rows, cols(h)].astype(F32)).astype(BF16)


def _hgrn(hq3, lf3, kf3, gi3, sg3, norm_w):
    bsz, seq, _ = hq3.shape
    block = HGRN_BLOCK
    step = HGRN_STEP
    tril, lv = _hgrn_constants(block)
    blk = pl.BlockSpec((1, step, STREAM_W), lambda b, g: (b, g, 0))
    const = lambda b, g: (0, 0)
    return pl.pallas_call(
        functools.partial(_hgrn_body, block=block, step=step),
        grid=(bsz, seq // step),
        in_specs=[blk, blk, blk, blk, blk,
                  pl.BlockSpec((1, HEAD_W), const),
                  pl.BlockSpec((block, block), const),
                  pl.BlockSpec((block, block), const)],
        out_specs=blk,
        out_shape=jax.ShapeDtypeStruct((bsz, seq, STREAM_W), BF16),
        scratch_shapes=[pltpu.VMEM((HEADS, HEAD_W, HEAD_W), F32),
                        pltpu.VMEM((HEADS * step // block, block, HEAD_W), F32)],
        name="hgrn2",
    )(hq3, lf3, kf3, gi3, sg3, norm_w, tril, lv)


META_E0, META_E1, META_R0, META_R1, META_W0, META_W1 = range(6)
GROUP_LANE0 = N_EXPERTS


def _outproj_body(da_ref, hg_ref, x_ref, mod_ref, wo_ref, nw_ref, wr_ref, br_ref, stril_ref,
                  x1_ref, h2_ref, meta_ref, cnt_ref, carry_scr):
    @pl.when(pl.program_id(0) == 0)
    def _():
        carry_scr[...] = jnp.zeros(carry_scr.shape, F32)

    attn = (jnp.dot(da_ref[...], wo_ref[0:STREAM_W, :], preferred_element_type=F32)
            + jnp.dot(hg_ref[...], wo_ref[STREAM_W:, :], preferred_element_type=F32))
    gate1 = mod_ref[0, 2:3, :]
    shift2 = mod_ref[0, 3:4, :]
    scale2 = mod_ref[0, 4:5, :]
    x1 = x_ref[...] + gate1 * attn
    x1_ref[...] = x1
    h2 = (x1 * lax.rsqrt(jnp.mean(x1 * x1, axis=-1, keepdims=True) + EPS) * nw_ref[...]
          * (1.0 + scale2) + shift2)
    _slab_store(h2_ref, h2)

    rows = h2.shape[0]
    h_hi = h2.astype(BF16)
    h_lo = (h2 - h_hi.astype(F32)).astype(BF16)
    parts = jnp.dot(jnp.concatenate([h_hi, h_lo], axis=0), wr_ref[...], preferred_element_type=F32)
    logits = ((parts[:rows, :LANES] + parts[:rows, LANES:])
              + (parts[rows:, :LANES] + parts[rows:, LANES:]) + br_ref[...])
    lane = lax.broadcasted_iota(jnp.int32, logits.shape, 1)
    far = jnp.int32(LANES)

    def first_max(vals):
        mx = jnp.max(vals, axis=-1, keepdims=True)
        return mx, jnp.min(jnp.where(vals == mx, lane, far), axis=-1, keepdims=True)

    is_g = (lane >= GROUP_LANE0) & (lane < GROUP_LANE0 + N_GROUPS)
    gmax, glane = first_max(jnp.where(is_g, logits, NEG_BIG))
    g_w = 1.0 / jnp.sum(jnp.where(is_g, jnp.exp(logits - gmax), 0.0), axis=-1, keepdims=True)
    gidx = glane - GROUP_LANE0
    in_grp = (lane < N_EXPERTS) & ((lane // EXPERTS_PER_GROUP) == gidx)
    el = jnp.where(in_grp, logits, NEG_BIG)
    m1, i1 = first_max(el)
    m2, i2 = first_max(jnp.where(lane == i1, NEG_BIG, el))
    r = jnp.exp(m2 - m1)
    w0 = g_w / (1.0 + r)
    w1 = g_w * r / (1.0 + r)

    hot0 = lane == i1
    hot1 = lane == i2
    multi = jnp.where(hot0 | hot1, 1.0, 0.0)
    before = jnp.dot(stril_ref[...], multi.astype(BF16), preferred_element_type=F32) + carry_scr[...]
    rank0 = jnp.sum(jnp.where(hot0, before, 0.0), axis=-1, keepdims=True)
    rank1 = jnp.sum(jnp.where(hot1, before, 0.0), axis=-1, keepdims=True)
    carry = carry_scr[...] + jnp.sum(multi, axis=0, keepdims=True)
    carry_scr[...] = carry
    cnt_ref[...] = carry

    meta = jnp.zeros(logits.shape, F32)
    for idx, val in ((META_E0, i1.astype(F32)), (META_E1, i2.astype(F32)),
                     (META_R0, rank0), (META_R1, rank1), (META_W0, w0), (META_W1, w1)):
        meta = jnp.where(lane == idx, val, meta)
    meta_ref[...] = meta


def _outproj(da2, hg2, x2, mod3, wo_bf, norm_w, w_route, b_route, seq):
    t, d = x2.shape
    tm = OUT_TILE
    per_b = seq // tm
    row = lambda i: (i, 0)
    full = lambda i: (0, 0)
    stril =jnp.asarray(np.tril(np.ones((tm, tm), np.float32), -1), BF16)
    return pl.pallas_call(
        _outproj_body,
        grid=(t // tm,),
        in_specs=[pl.BlockSpec((tm, STREAM_W), row),
                  pl.BlockSpec((tm, STREAM_W), row),
                  pl.BlockSpec((tm, d), row),
                  pl.BlockSpec((1, 6, d), lambda i: (i // per_b, 0, 0)),
                  pl.BlockSpec((2 * STREAM_W, d), full),
                  pl.BlockSpec((1, d), full),
                  pl.BlockSpec((d, 2 * LANES), full),
                  pl.BlockSpec((1, LANES), full),
                  pl.BlockSpec((tm, tm), full)],
        out_specs=[pl.BlockSpec((tm, d), row),
                   pl.BlockSpec((tm * SLAB, LANES), row),
                   pl.BlockSpec((tm, LANES), row),
                   pl.BlockSpec((1, LANES), full)],
        out_shape=[jax.ShapeDtypeStruct((t, d), F32),
                   jax.ShapeDtypeStruct((t * SLAB, LANES), F32),
                   jax.ShapeDtypeStruct((t, LANES), F32),
                   jax.ShapeDtypeStruct((1, LANES), F32)],
        scratch_shapes=[pltpu.VMEM((1, LANES), F32)],
        compiler_params=pltpu.CompilerParams(dimension_semantics=("arbitrary",)),
        name="outproj_route",
    )(da2, hg2, x2, mod3, wo_bf, norm_w, w_route, b_route, stril)


DMA_UNROLL = 8


def _scatter_body(last_ref, pos_ref, h2_ref, xs_hbm, zero_scr, sem, zsem, *, tile):
    rows = tile * SLAB

    @pl.when(pl.program_id(0) == 0)
    def _():
        zero_scr[...] = jnp.zeros(zero_scr.shape, F32)

        def clear(tile_idx):
            start = pl.multiple_of(tile_idx * (MOE_TILE * SLAB), SLAB)
            return pltpu.make_async_copy(zero_scr, xs_hbm.at[pl.ds(start, MOE_TILE * SLAB)], zsem)

        for e in range(N_EXPERTS):
            @pl.when(last_ref[e] >= 0)
            def _():
                clear(last_ref[e]).start()
        for e in range(N_EXPERTS):
            @pl.when(last_ref[e] >= 0)
            def _():
                clear(last_ref[e]).wait()

        def clear_unused(tile_idx, carry):
            clear(tile_idx).start()
            clear(tile_idx).wait()
            return carry

        lax.fori_loop(last_ref[N_EXPERTS], xs_hbm.shape[0] // (MOE_TILE * SLAB), clear_unused, 0)

    def issue(t, carry):
        src = h2_ref.at[pl.ds(pl.multiple_of(t * SLAB, SLAB), SLAB)]
        for j in range(2):
            slot = pl.multiple_of(pos_ref[0, 0, 2 * t + j] * SLAB, SLAB)
            pltpu.make_async_copy(src, xs_hbm.at[pl.ds(slot, SLAB)], sem).start(priority=j)
        return carry

    lax.fori_loop(0, tile, issue, 0, unroll=DMA_UNROLL)
    whole = pltpu.make_async_copy(h2_ref, xs_hbm.at[pl.ds(0, rows)], sem)
    whole.wait()
    whole.wait()


def _scatter(last_tile, pos3, h2s, n_slots):
    tile = GATHER_TILE
    rows = tile * SLAB
    grid_spec = pltpu.PrefetchScalarGridSpec(
        num_scalar_prefetch=1,
        grid=(h2s.shape[0] // rows,),
        in_specs=[pl.BlockSpec((1, 1, 2 * tile), lambda i, lt: (i, 0, 0), memory_space=pltpu.SMEM),
                  pl.BlockSpec((rows, LANES), lambda i, lt: (i, 0))],
        out_specs=pl.BlockSpec(memory_space=pl.ANY),
        scratch_shapes=[pltpu.VMEM((MOE_TILE * SLAB, LANES), F32),
                        pltpu.SemaphoreType.DMA, pltpu.SemaphoreType.DMA],
    )
    return pl.pallas_call(
        functools.partial(_scatter_body, tile=tile),
        grid_spec=grid_spec,
        out_shape=jax.ShapeDtypeStruct((n_slots * SLAB, LANES), F32),
        compiler_params=pltpu.CompilerParams(dimension_semantics=("arbitrary",)),
        name="moe_scatter",
    )(last_tile, pos3, h2s)


def _experts_body(te_ref, nv_ref, x_ref, wg_ref, wu_ref, wd_ref, y_ref, wgu_scr, wd_scr, *, tm):
    i = pl.program_id(0)
    e = te_ref[i]
    prev = te_ref[jnp.maximum(i - 1, 0)]

    @pl.when((i == 0) | (e != prev))
    def _():
        wgu_scr[:, 0:EXPERT_FF] = wg_ref[0].astype(BF16)
        wgu_scr[:, EXPERT_FF:] = wu_ref[0].astype(BF16)
        wd_scr[...] = wd_ref[0].astype(BF16)

    @pl.when(i < nv_ref[0])
    def _():
        half = tm // 2
        xr = [x_ref.at[pl.ds(r * half * SLAB, half * SLAB)] for r in range(2)]
        yr = [y_ref.at[pl.ds(r * half * SLAB, half * SLAB)] for r in range(2)]
        xs = [_slab_load(ref, half).astype(BF16) for ref in xr]
        gus = [jnp.dot(x, wgu_scr[...], preferred_element_type=F32) for x in xs]
        acts = [(gu[:, 0:EXPERT_FF] * _sigmoid(gu[:, 0:EXPERT_FF]) * gu[:, EXPERT_FF:]).astype(BF16)
                for gu in gus]
        ys = [jnp.dot(act, wd_scr[...], preferred_element_type=F32) for act in acts]
        for ref, y in zip(yr, ys):
            _slab_store(ref, y)

    @pl.when(i >= nv_ref[0])
    def _():
        y_ref[...] = jnp.zeros(y_ref.shape, F32)


def _experts(tile_expert, n_valid, xs, w_gate, w_up, w_down):
    tm = MOE_TILE
    rows = tm * SLAB
    d = w_gate.shape[1]
    live = lambda i, te, nv: (jnp.minimum(i, nv[0] - 1), 0)
    grid_spec = pltpu.PrefetchScalarGridSpec(
        num_scalar_prefetch=2,
        grid=(xs.shape[0] // rows,),
        in_specs=[pl.BlockSpec((rows, LANES), live),
                  pl.BlockSpec((1, d, EXPERT_FF), lambda i, te, nv: (te[i], 0, 0)),
                  pl.BlockSpec((1, d, EXPERT_FF), lambda i, te, nv: (te[i], 0, 0)),
                  pl.BlockSpec((1, EXPERT_FF, d), lambda i, te, nv: (te[i], 0, 0))],
        out_specs=pl.BlockSpec((rows, LANES), lambda i, te, nv: (i, 0)),
        scratch_shapes=[pltpu.VMEM((d, 2 * EXPERT_FF), BF16),
                        pltpu.VMEM((EXPERT_FF, d), BF16)],
    )
    return pl.pallas_call(
        functools.partial(_experts_body, tm=tm),
        grid_spec=grid_spec,
        out_shape=jax.ShapeDtypeStruct(xs.shape, F32),
        compiler_params=pltpu.CompilerParams(dimension_semantics=("arbitrary",)),
        name="moe_experts",
    )(tile_expert, n_valid, xs, w_gate, w_up, w_down)


def _combine_body(pos_ref, nxt_ref, x1_ref, meta_ref, mod_ref, nw_ref, ys_hbm, o_ref, rows_scr, sems,
                  *, tile):
    i = pl.program_id(0)
    n = pl.num_programs(0)
    rows = tile * SLAB

    def fetch(idx_ref, buf):
        def issue(t, carry):
            dst = pl.ds(pl.multiple_of(t * SLAB, SLAB), SLAB)
            for j in range(2):
                slot = pl.multiple_of(idx_ref[0, 0, 2 * t + j] * SLAB, SLAB)
                pltpu.make_async_copy(ys_hbm.at[pl.ds(slot, SLAB)], rows_scr.at[buf, j, dst],
                                      sems.at[buf]).start(priority=j)
            return carry
        lax.fori_loop(0, tile, issue, 0, unroll=DMA_UNROLL)

    @pl.when(i == 0)
    def _():
        fetch(pos_ref, 0)

    for cur in range(2):
        @pl.when((i % 2 == cur) & (i + 1 < n))
        def _():
            fetch(nxt_ref, 1 - cur)

    for cur in range(2):
        @pl.when(i % 2 == cur)
        def _():
            for j in range(2):
                pltpu.make_async_copy(ys_hbm.at[pl.ds(0, rows)], rows_scr.at[cur, j],
                                      sems.at[cur]).wait()
            meta = meta_ref[...]
            w0 = meta[:, META_W0:META_W0 + 1]
            w1 = meta[:, META_W1:META_W1 + 1]
            y = (w0 * _slab_load(rows_scr.at[cur, 0], tile)
                 + w1 * _slab_load(rows_scr.at[cur, 1], tile))
            x2 = x1_ref[...] + mod_ref[0, 5:6, :] * y
            o_ref[...] = (x2 * lax.rsqrt(jnp.mean(x2 * x2, axis=-1, keepdims=True) + EPS)
                          * nw_ref[...])


def _combine(pos3, x1, meta, mod3, norm_w, ys, seq):
    t, d = x1.shape
    tile = GATHER_TILE
    per_b = seq // tile
    n = t // tile
    row = lambda i: (i, 0)
    return pl.pallas_call(
        functools.partial(_combine_body, tile=tile),
        grid=(n,),
        in_specs=[pl.BlockSpec((1, 1, 2 * tile), lambda i: (i, 0, 0), memory_space=pltpu.SMEM),
                  pl.BlockSpec((1, 1, 2 * tile), lambda i: (jnp.minimum(i + 1, n - 1), 0, 0),
                               memory_space=pltpu.SMEM),
                  pl.BlockSpec((tile, d), row),
                  pl.BlockSpec((tile, LANES), row),
                  pl.BlockSpec((1, 6, d), lambda i: (i // per_b, 0, 0)),
                  pl.BlockSpec((1, d), lambda i: (0, 0)),
                  pl.BlockSpec(memory_space=pl.ANY)],
        out_specs=pl.BlockSpec((tile, d), row),
        out_shape=jax.ShapeDtypeStruct((t, d), F32),
        scratch_shapes=[pltpu.VMEM((2, 2, tile * SLAB, LANES), F32), pltpu.SemaphoreType.DMA((2,))],
        compiler_params=pltpu.CompilerParams(dimension_semantics=("arbitrary",)),
        name="moe_combine",
    )(pos3, pos3, x1, meta, mod3, norm_w, ys)


def _rope_constants():
    inv_freq = ROPE_THETA ** (-jnp.arange(ROT_HALF, dtype=F32) / ROT_HALF)
    lane = np.arange(LANES) % DA_QK_DIM
    hit = (lane[None, :] < ROT_DIM) & (lane[None, :] % ROT_HALF == np.arange(ROT_HALF)[:, None])
    spread = np.concatenate([hit, hit], axis=0).astype(np.float32)
    return inv_freq.reshape(ROT_HALF, 1), jnp.asarray(spread, BF16)


def kernel(x, c, positions, norm1_w, norm2_w, final_norm_w, ada_w, ada_b, w_in, w_out, da_lambda_q1, da_lambda_k1, da_lambda_q2, da_lambda_k2, da_subln_w, hg_lower_bound, hg_norm_w, moe_w_group, moe_b_group, moe_w_router, moe_b_router, moe_w_gate, moe_w_up, moe_w_down):
    bsz, seq, d = x.shape
    assert d == D_MODEL and norm1_w.shape[0] == 1, "single-layer model of width 1024 only"
    assert seq % ATTN_TILE == 0 and seq % HGRN_STEP == 0 and seq % OUT_TILE == 0
    t = bsz * seq
    x2 = x.reshape(t, d)

    mod3 = _adaln(c, ada_w[0], ada_b).reshape(bsz, 6, d)

    pos_rows = positions.astype(F32).reshape(t // ROW_TILE, 1, ROW_TILE)
    inv_freq, spread = _rope_constants()
    qt, k, vt, hq, lf, kf, gi, sg = _inproj(x2, mod3, norm1_w, w_in[0].astype(BF16), pos_rows,
                                          inv_freq, spread, hg_lower_bound, seq)

    as3 = lambda a: a.reshape(bsz, seq, STREAM_W)
    lam_p = jnp.concatenate([da_lambda_q1, da_lambda_k1, da_lambda_q2, da_lambda_k2], axis=0)
    da = _attn(qt, as3(k), vt, lam_p, da_subln_w.reshape(HEAD_W, 1))
    hg = _hgrn(as3(hq), as3(lf), as3(kf), as3(gi), as3(sg), hg_norm_w)

    pad = jnp.zeros((d, LANES - N_EXPERTS - N_GROUPS), F32)
    w_route = jnp.concatenate([moe_w_router[0], moe_w_group[0], pad], axis=1)
    b_route = jnp.concatenate([moe_b_router[0], moe_b_group[0], pad[0]]).reshape(1, LANES)
    w_route_hi = w_route.astype(BF16)
    w_route_lo = (w_route - w_route_hi.astype(F32)).astype(BF16)
    x1, h2, meta, cnt = _outproj(da.reshape(t, STREAM_W), hg.reshape(t, STREAM_W), x2, mod3,
                                 w_out[0].astype(BF16), norm2_w,
                                 jnp.concatenate([w_route_hi, w_route_lo], axis=1), b_route, seq)

    counts = cnt[0, :N_EXPERTS].astype(jnp.int32)
    tiles_e = (counts + MOE_TILE - 1) // MOE_TILE
    tile_end = jnp.cumsum(tiles_e)
    offs = (tile_end - tiles_e) * MOE_TILE
    ids = meta[:, META_E0:META_E1 + 1].astype(jnp.int32)
    ranks = meta[:, META_R0:META_R1 + 1].astype(jnp.int32)
    expert_iota = jnp.arange(N_EXPERTS, dtype=jnp.int32)
    pos = jnp.sum(jnp.where(ids[..., None] == expert_iota, offs, 0), axis=-1) + ranks
    n_tiles = (2 * t) // MOE_TILE + N_EXPERTS
    n_valid = tile_end[-1:]
    tile_ids = jnp.minimum(jnp.arange(n_tiles, dtype=jnp.int32), n_valid - 1)
    tile_expert = jnp.sum(tile_ids[:, None] >= tile_end[None, :], axis=1).astype(jnp.int32)
    pos3 = pos.reshape(t // GATHER_TILE, 1, 2 * GATHER_TILE)
    last_tile = jnp.concatenate([jnp.where(tiles_e > 0, tile_end - 1, -1), n_valid]).astype(jnp.int32)

    xs = _scatter(last_tile, pos3, h2, n_tiles * MOE_TILE)
    ys = _experts(tile_expert, n_valid.astype(jnp.int32), xs, moe_w_gate[0], moe_w_up[0], moe_w_down[0])
    out = _combine(pos3, x1, meta, mod3, final_norm_w.reshape(1, d), ys, seq)
    return out.reshape(bsz, seq, d)
```

```python
import functools
import math

import numpy as np
import jax
import jax.numpy as jnp
from jax import lax
from jax.experimental import pallas as pl
from jax.experimental.pallas import tpu as pltpu

F32 = jnp.float32
BF16 = jnp.bfloat16
HIGHEST = lax.Precision.HIGHEST

LANES = 128
SUBLANES = 8
D_MODEL = 1024
HEADS = 4
HEAD_W = 128
STREAM_W = HEADS * HEAD_W
N_STREAMS = 7
DA_QK_DIM = 64
ROPE_THETA = 500000.0
ROT_DIM = DA_QK_DIM // 4
ROT_HALF = ROT_DIM // 2
N_GROUPS = 4
EXPERTS_PER_GROUP = 8
N_EXPERTS = N_GROUPS * EXPERTS_PER_GROUP
EXPERT_FF = 256
EPS = 1e-6
LAM_INIT = 0.8 - 0.6 * math.exp(-0.3 * 0)
NEG_BIG = -1e30

ROW_TILE = 256
OUT_TILE = 512
ATTN_TILE = 1024
ATTN_KEY_TILE = 512
HGRN_BLOCK = 128
HGRN_STEP = 256
MOE_TILE = 512
GATHER_TILE = 256

NT_DIMS = (((1,), (1,)), ((), ()))
TN_DIMS = (((0,), (0,)), ((), ()))


def _sigmoid(x):
    return 1.0 / (1.0 + jnp.exp(-x))


SLAB = D_MODEL // LANES


def _slab_store(ref, val):
    rows = val.shape[0]
    for c in range(SLAB):
        ref[pl.ds(c, rows, stride=SLAB), :] = val[:, c * LANES:(c + 1) * LANES]


def _slab_load(ref, rows):
    return jnp.concatenate([ref[pl.ds(c, rows, stride=SLAB), :] for c in range(SLAB)], axis=1)


def _adaln_body(c_ref, w_ref, b_ref, o_ref):
    c = c_ref[...]
    ca = c * _sigmoid(c)
    o_ref[...] = jnp.dot(ca, w_ref[...], preferred_element_type=F32, precision=HIGHEST) + b_ref[...]


def _adaln(c, ada_w, ada_b):
    bsz, d = c.shape
    n = ada_w.shape[1]
    tn = 1024
    return pl.pallas_call(
        _adaln_body,
        grid=(n // tn,),
        in_specs=[pl.BlockSpec((bsz, d), lambda j: (0, 0)),
                  pl.BlockSpec((d, tn), lambda j: (0, j)),
                  pl.BlockSpec((1, tn), lambda j: (0, j))],
        out_specs=pl.BlockSpec((bsz, tn), lambda j: (0, j)),
        out_shape=jax.ShapeDtypeStruct((bsz, n), F32),
        name="adaln",
    )(c, ada_w, ada_b)


def _inproj_body(x_ref, mod_ref, nw_ref, w_ref, pos_ref, invf_ref, spread_ref, lbr_ref,
                 q_ref, k_ref, v_ref, hq_ref, lf_ref, kf_ref, gi_ref, sg_ref):
    x = x_ref[...]
    ms = jnp.mean(x * x, axis=-1, keepdims=True)
    y = x * lax.rsqrt(ms + EPS) * nw_ref[...]
    shift = mod_ref[0, 0:1, :]
    scale = mod_ref[0, 1:2, :]
    h = (y * (1.0 + scale) + shift).astype(BF16)

    ang_t = invf_ref[...] * pos_ref[0]

    def spread(table_t):
        hi = table_t.astype(BF16)
        lo = (table_t - hi.astype(F32)).astype(BF16)
        return lax.dot_general(jnp.concatenate([hi, lo], axis=0), spread_ref[...], TN_DIMS,
                               preferred_element_type=F32)

    lane = lax.broadcasted_iota(jnp.int32, (1, LANES), 1) % DA_QK_DIM
    cosv = spread(jnp.cos(ang_t)) + jnp.where(lane < ROT_DIM, 0.0, 1.0)
    sinv = spread(jnp.sin(ang_t))
    sin_lo = jnp.where(lane < ROT_HALF, -sinv, 0.0)
    sin_hi = jnp.where((lane >= ROT_HALF) & (lane < ROT_DIM), sinv, 0.0)

    streams = [jnp.dot(h, w_ref[:, j * STREAM_W:(j + 1) * STREAM_W], preferred_element_type=F32)
               for j in range(N_STREAMS)]

    def proj(j):
        return streams[j]

    def rope(t):
        outs = []
        for hb in range(HEADS):
            tc = t[:, hb * HEAD_W:(hb + 1) * HEAD_W]
            outs.append(tc * cosv
                        + pltpu.roll(tc, LANES - ROT_HALF, 1) * sin_lo
                        + pltpu.roll(tc, ROT_HALF, 1) * sin_hi)
        return jnp.concatenate(outs, axis=1)

    q_ref[0] = (rope(proj(0)) * (DA_QK_DIM ** -0.5 * math.log2(math.e))).T.astype(BF16)
    k_ref[...] = rope(proj(1)).astype(BF16)
    v_ref[0] = proj(2).astype(BF16).T

    gq = proj(3)
    hq_ref[...] = gq * _sigmoid(gq)

    a = lbr_ref[...]
    amax = jnp.max(a, axis=0, keepdims=True)
    ea = jnp.exp(a - amax)
    lb = ea[0:1, :] / jnp.sum(ea, axis=0, keepdims=True)
    gf = proj(4)
    f = lb + (1.0 - lb) * _sigmoid(gf)
    lf_ref[...] = jnp.log(f) * math.log2(math.e)
    kf_ref[...] = 1.0 - f

    gi_ref[...] = proj(5).astype(BF16)
    gg = proj(6)
    sg_ref[...] = (gg * _sigmoid(gg)).astype(BF16)


def _inproj(x2, mod3, norm_w, w_bf, pos_rows, invf, spread, lb_raw, seq):
    t, d = x2.shape
    tm = ROW_TILE
    per_b = seq // tm
    row = lambda i: (i, 0)
    full = lambda i: (0, 0)
    out_bf = jax.ShapeDtypeStruct((t, STREAM_W), BF16)
    out_f = jax.ShapeDtypeStruct((t, STREAM_W), F32)
    out_t = jax.ShapeDtypeStruct((t // seq, STREAM_W, seq), BF16)
    stream = pl.BlockSpec((tm, STREAM_W), row)
    stream_t = pl.BlockSpec((1, STREAM_W, tm), lambda i: (i // per_b, 0, i % per_b))
    return pl.pallas_call(
        _inproj_body,
        grid=(t // tm,),
        in_specs=[pl.BlockSpec((tm, d), row),
                  pl.BlockSpec((1, 6, d), lambda i: (i // per_b, 0, 0)),
                  pl.BlockSpec((1, d), full),
                  pl.BlockSpec((d, N_STREAMS * STREAM_W), full),
                  pl.BlockSpec((1, 1, tm), lambda i: (i, 0, 0)),
                  pl.BlockSpec((ROT_HALF, 1), full),
                  pl.BlockSpec((2 * ROT_HALF, LANES), full),
                  pl.BlockSpec(lb_raw.shape, full)],
        out_specs=[stream_t, stream, stream_t] + [stream] * 5,
        out_shape=[out_t, out_bf, out_t, out_f, out_f, out_f, out_bf, out_bf],
        compiler_params=pltpu.CompilerParams(vmem_limit_bytes=56 * 1024 * 1024),
        name="inproj",
    )(x2, mod3, norm_w, w_bf, pos_rows, invf, spread, lb_raw)


ONES_ROWS = 16


def _attn_body(qt_ref, k_ref, vt_ref, lam_ref, sw_ref, o_ref, s_scr, m_scr, a_scr, *, tile, ktile):
    qi = pl.program_id(2)
    qt = qt_ref[0]
    feat = lax.broadcasted_iota(jnp.int32, (HEAD_W, 1), 0)
    zero = jnp.zeros_like(qt)
    qmaps = (jnp.where(feat < DA_QK_DIM, qt, zero), jnp.where(feat >= DA_QK_DIM, qt, zero))

    m_scr[...] = jnp.full(m_scr.shape, NEG_BIG, F32)
    a_scr[...] = jnp.zeros(a_scr.shape, F32)

    def score_block(start, nkeys, q_lo, masked):
        kb = k_ref[0, pl.ds(start, nkeys), :]
        scores = [jnp.dot(kb, qmaps[mp][:, q_lo:], preferred_element_type=F32) for mp in range(2)]
        for mp, s in enumerate(scores):
            if masked:
                key = lax.broadcasted_iota(jnp.int32, s.shape, 0)
                qry = lax.broadcasted_iota(jnp.int32, s.shape, 1)
                s = jnp.where(key <= qry, s, NEG_BIG)
            grouped = s.reshape(nkeys // SUBLANES, SUBLANES, tile - q_lo)
            m_scr[mp, :, q_lo:] = jnp.maximum(m_scr[mp, :, q_lo:], jnp.max(grouped, axis=0))
            s_scr[mp, pl.ds(start, nkeys), q_lo:] = s

    def value_block(start, nkeys, q_lo, col_max):
        vb = jnp.concatenate([vt_ref[0, :, pl.ds(start, nkeys)], jnp.ones((ONES_ROWS, nkeys), BF16)],
                             axis=0)
        probs = [jnp.exp2(s_scr[mp, pl.ds(start, nkeys), q_lo:] - col_max[mp][:, q_lo:]).astype(BF16)
                 for mp in range(2)]
        for mp, p in enumerate(probs):
            a_scr[mp, :, q_lo:] += jnp.dot(vb, p, preferred_element_type=F32)

    diag = pl.multiple_of(qi * tile, tile)
    pieces = [(diag + j * ktile, ktile, j * ktile) for j in range(tile // ktile)]

    def score_body(ki, carry):
        score_block(pl.multiple_of(ki * tile, tile), tile, 0, False)
        return carry

    lax.fori_loop(0, qi, score_body, 0)
    for start, nkeys, q_lo in pieces:
        score_block(start, nkeys, q_lo, True)
    col_max = [jnp.max(m_scr[mp], axis=0, keepdims=True) for mp in range(2)]

    def value_body(ki, carry):
        value_block(pl.multiple_of(ki * tile, tile), tile, 0, col_max)
        return carry

    lax.fori_loop(0, qi, value_body, 0)
    for start, nkeys, q_lo in pieces:
        value_block(start, nkeys, q_lo, col_max)

    lp = lam_ref[...]
    lam = (jnp.exp(jnp.sum(lp[0:1] * lp[1:2], axis=-1, keepdims=True))
           - jnp.exp(jnp.sum(lp[2:3] * lp[3:4], axis=-1, keepdims=True)) + LAM_INIT)
    o = (a_scr[0, 0:HEAD_W, :] / a_scr[0, HEAD_W:HEAD_W + 1, :]
         - lam * (a_scr[1, 0:HEAD_W, :] / a_scr[1, HEAD_W:HEAD_W + 1, :]))
    o = o * lax.rsqrt(jnp.mean(o * o, axis=0, keepdims=True) + EPS) * sw_ref[...]
    o_ref[0] = (o * (1.0 - LAM_INIT)).T.astype(BF16)


def _attn(qt3, k3, vt3, lam_p, subln_col):
    bsz, seq, _ = k3.shape
    tile = ATTN_TILE
    return pl.pallas_call(
        functools.partial(_attn_body, tile=tile, ktile=ATTN_KEY_TILE),
        grid=(bsz, HEADS, seq // tile),
        in_specs=[pl.BlockSpec((1, HEAD_W, tile), lambda b, h, i: (b, h, i)),
                  pl.BlockSpec((1, seq, HEAD_W), lambda b, h, i: (b, 0, h)),
                  pl.BlockSpec((1, HEAD_W, seq), lambda b, h, i: (b, h, 0)),
                  pl.BlockSpec(lam_p.shape, lambda b, h, i: (0, 0)),
                  pl.BlockSpec((HEAD_W, 1), lambda b, h, i: (0, 0))],
        out_specs=pl.BlockSpec((1, tile, HEAD_W), lambda b, h, i: (b, i, h)),
        out_shape=jax.ShapeDtypeStruct((bsz, seq, STREAM_W), BF16),
        scratch_shapes=[pltpu.VMEM((2, seq, tile), F32),
                        pltpu.VMEM((2, SUBLANES, tile), F32),
                        pltpu.VMEM((2, HEAD_W + ONES_ROWS, tile), F32)],
        compiler_params=pltpu.CompilerParams(vmem_limit_bytes=56 * 1024 * 1024),
        name="diff_attn",
    )(qt3, k3, vt3, lam_p, subln_col)


def _hgrn_levels(block):
    return [block >> (i + 1) for i in range(block.bit_length() - 1)]


def _hgrn_constants(block):
    t = np.arange(block)[:, None]
    s = np.arange(block)[None, :]
    tril = (s <= t).astype(np.float32)
    lv = np.full((block, block), -1, np.int32)
    halves = _hgrn_levels(block)
    for li, m in enumerate(halves):
        same = (t // (2 * m)) == (s // (2 * m))
        lv[same & ((t & m) != 0) & ((s & m) == 0)] = li
    lv[np.arange(block), np.arange(block)] = len(halves)
    return jnp.asarray(tril, BF16), jnp.asarray(lv)


def _level_operand(b_ref, h, b, q, k, m, block):
    def ref_rows(r, n):
        return jnp.broadcast_to(b_ref[h, pl.ds(r, 1), :], (n, HEAD_W))

    if m >= SUBLANES:
        pieces = []
        for s0 in range(0, block, 2 * m):
            ref = ref_rows(s0 + m - 1, m)
            lo = slice(s0, s0 + m)
            up = slice(s0 + m, s0 + 2 * m)
            pieces.append(k[lo] * jnp.exp2(ref - b[lo]))
            pieces.append(q[up] * jnp.exp2(b[up] - ref))
        return jnp.concatenate(pieces, axis=0)

    sub = lax.broadcasted_iota(jnp.int32, (SUBLANES, HEAD_W), 0)
    refs = []
    for s0 in range(0, block, SUBLANES):
        piece = ref_rows(s0 + m - 1, SUBLANES)
        for j in range(1, SUBLANES // (2 * m)):
            piece = jnp.where(sub >= 2 * m * j, ref_rows(s0 + 2 * m * j + m - 1, SUBLANES), piece)
        refs.append(piece)
    d = b - jnp.concatenate(refs, axis=0)
    row = lax.broadcasted_iota(jnp.int32, (block, 1), 0)
    return jnp.where((row & m) != 0, q, k) * jnp.exp2(jnp.minimum(d, -d))


def _hgrn_body(hq_ref, lf_ref, kf_ref, gi_ref, sg_ref, nw_ref, tril_ref, lv_ref, o_ref,
               st_scr, b_scr, *, block, step):
    @pl.when(pl.program_id(1) == 0)
    def _():
        st_scr[...] = jnp.zeros(st_scr.shape, F32)

    tril = tril_ref[...]
    lv = lv_ref[...]
    halves = _hgrn_levels(block)

    units = [(u, r0, h) for u, (r0, h) in enumerate(
        (r0, h) for r0 in range(0, step, block) for h in range(HEADS))]

    def cols(h):
        return slice(h * HEAD_W, (h + 1) * HEAD_W)

    q, k, v, b, scores = {}, {}, {}, {}, {}
    for u, r0, h in units:
        rows = slice(r0, r0 + block)
        q[u] = hq_ref[0, rows, cols(h)]
        k[u] = kf_ref[0, rows, cols(h)]
        v[u] = gi_ref[0, rows, cols(h)]
        lf = lf_ref[0, rows, cols(h)]
        hi = lf.astype(BF16)
        r1 = lf - hi.astype(F32)
        mid = r1.astype(BF16)
        lo = (r1 - mid.astype(F32)).astype(BF16)
        b[u] = (jnp.dot(tril, hi, preferred_element_type=F32)
                + jnp.dot(tril, mid, preferred_element_type=F32)
                + jnp.dot(tril, lo, preferred_element_type=F32))
        b_scr[u] = b[u]
        scores[u] = jnp.where(lv == len(halves),
                              lax.dot_general(q[u].astype(BF16), k[u].astype(BF16), NT_DIMS,
                                              preferred_element_type=F32), 0.0)

    for li, m in enumerate(halves):
        for u, r0, h in units:
            xl = _level_operand(b_scr, u, b[u], q[u], k[u], m, block).astype(BF16)
            p = lax.dot_general(xl, xl, NT_DIMS, preferred_element_type=F32)
            scores[u] = jnp.where(lv == li, p, scores[u])

    for u, r0, h in units:
        rows = slice(r0, r0 + block)
        o_intra = jnp.dot(scores[u].astype(BF16), v[u], preferred_element_type=F32)
        st = st_scr[h]
        o_inter = lax.dot_general((q[u] * jnp.exp2(b[u])).astype(BF16), st.astype(BF16), NT_DIMS,
                                  preferred_element_type=F32)
        b_last = b[u][block - 1:block, :]
        kdec = (k[u] * jnp.exp2(b_last - b[u])).astype(BF16)
        st_scr[h] = st * jnp.exp2(b_last) + lax.dot_general(v[u], kdec, TN_DIMS,
                                                             preferred_element_type=F32)
        o = o_inter + o_intra
        o = o * lax.rsqrt(jnp.mean(o * o, axis=-1, keepdims=True) + EPS) * nw_ref[...]
        o_ref[0, rows, cols(h)] = (o * sg_ref[0, rows, cols(h)].astype(F32)).astype(BF16)


def _hgrn(hq3, lf3, kf3, gi3, sg3, norm_w):
    bsz, seq, _ = hq3.shape
    block = HGRN_BLOCK
    step = HGRN_STEP
    tril, lv = _hgrn_constants(block)
    blk = pl.BlockSpec((1, step, STREAM_W), lambda b, g: (b, g, 0))
    const = lambda b, g: (0, 0)
    return pl.pallas_call(
        functools.partial(_hgrn_body, block=block, step=step),
        grid=(bsz, seq // step),
        in_specs=[blk, blk, blk, blk, blk,
                  pl.BlockSpec((1, HEAD_W), const),
                  pl.BlockSpec((block, block), const),
                  pl.BlockSpec((block, block), const)],
        out_specs=blk,
        out_shape=jax.ShapeDtypeStruct((bsz, seq, STREAM_W), BF16),
        scratch_shapes=[pltpu.VMEM((HEADS, HEAD_W, HEAD_W), F32),
                        pltpu.VMEM((HEADS * step // block, block, HEAD_W), F32)],
        name="hgrn2",
    )(hq3, lf3, kf3, gi3, sg3, norm_w, tril, lv)


META_E0, META_E1, META_R0, META_R1, META_W0, META_W1 = range(6)
GROUP_LANE0 = N_EXPERTS


def _outproj_body(da_ref, hg_ref, x_ref, mod_ref, wo_ref, nw_ref, wr_ref, br_ref, stril_ref,
                  x1_ref, h2_ref, meta_ref, cnt_ref, carry_scr):
    @pl.when(pl.program_id(0) == 0)
    def _():
        carry_scr[...] = jnp.zeros(carry_scr.shape, F32)

    attn = (jnp.dot(da_ref[...], wo_ref[0:STREAM_W, :], preferred_element_type=F32)
            + jnp.dot(hg_ref[...], wo_ref[STREAM_W:, :], preferred_element_type=F32))
    gate1 = mod_ref[0, 2:3, :]
    shift2 = mod_ref[0, 3:4, :]
    scale2 = mod_ref[0, 4:5, :]
    x1 = x_ref[...] + gate1 * attn
    x1_ref[...] = x1
    h2 = (x1 * lax.rsqrt(jnp.mean(x1 * x1, axis=-1, keepdims=True) + EPS) * nw_ref[...]
          * (1.0 + scale2) + shift2)
    _slab_store(h2_ref, h2)

    rows = h2.shape[0]
    h_hi = h2.astype(BF16)
    h_lo = (h2 - h_hi.astype(F32)).astype(BF16)
    parts = jnp.dot(jnp.concatenate([h_hi, h_lo], axis=0), wr_ref[...], preferred_element_type=F32)
    logits = ((parts[:rows, :LANES] + parts[:rows, LANES:])
              + (parts[rows:, :LANES] + parts[rows:, LANES:]) + br_ref[...])
    lane = lax.broadcasted_iota(jnp.int32, logits.shape, 1)
    far = jnp.int32(LANES)

    def first_max(vals):
        mx = jnp.max(vals, axis=-1, keepdims=True)
        return mx, jnp.min(jnp.where(vals == mx, lane, far), axis=-1, keepdims=True)

    is_g = (lane >= GROUP_LANE0) & (lane < GROUP_LANE0 + N_GROUPS)
    gmax, glane = first_max(jnp.where(is_g, logits, NEG_BIG))
    g_w = 1.0 / jnp.sum(jnp.where(is_g, jnp.exp(logits - gmax), 0.0), axis=-1, keepdims=True)
    gidx = glane - GROUP_LANE0
    in_grp = (lane < N_EXPERTS) & ((lane // EXPERTS_PER_GROUP) == gidx)
    el = jnp.where(in_grp, logits, NEG_BIG)
    m1, i1 = first_max(el)
    m2, i2 = first_max(jnp.where(lane == i1, NEG_BIG, el))
    r = jnp.exp(m2 - m1)
    w0 = g_w / (1.0 + r)
    w1 = g_w * r / (1.0 + r)

    hot0 = lane == i1
    hot1 = lane == i2
    multi = jnp.where(hot0 | hot1, 1.0, 0.0)
    before = jnp.dot(stril_ref[...], multi.astype(BF16), preferred_element_type=F32) + carry_scr[...]
    rank0 = jnp.sum(jnp.where(hot0, before, 0.0), axis=-1, keepdims=True)
    rank1 = jnp.sum(jnp.where(hot1, before, 0.0), axis=-1, keepdims=True)
    carry = carry_scr[...] + jnp.sum(multi, axis=0, keepdims=True)
    carry_scr[...] = carry
    cnt_ref[...] = carry

    meta = jnp.zeros(logits.shape, F32)
    for idx, val in ((META_E0, i1.astype(F32)), (META_E1, i2.astype(F32)),
                     (META_R0, rank0), (META_R1, rank1), (META_W0, w0), (META_W1, w1)):
        meta = jnp.where(lane == idx, val, meta)
    meta_ref[...] = meta


def _outproj(da2, hg2, x2, mod3, wo_bf, norm_w, w_route, b_route, seq):
    t, d = x2.shape
    tm = OUT_TILE
    per_b = seq // tm
    row = lambda i: (i, 0)
    full = lambda i: (0, 0)
    stril =jnp.asarray(np.tril(np.ones((tm, tm), np.float32), -1), BF16)
    return pl.pallas_call(
        _outproj_body,
        grid=(t // tm,),
        in_specs=[pl.BlockSpec((tm, STREAM_W), row),
                  pl.BlockSpec((tm, STREAM_W), row),
                  pl.BlockSpec((tm, d), row),
                  pl.BlockSpec((1, 6, d), lambda i: (i // per_b, 0, 0)),
                  pl.BlockSpec((2 * STREAM_W, d), full),
                  pl.BlockSpec((1, d), full),
                  pl.BlockSpec((d, 2 * LANES), full),
                  pl.BlockSpec((1, LANES), full),
                  pl.BlockSpec((tm, tm), full)],
        out_specs=[pl.BlockSpec((tm, d), row),
                   pl.BlockSpec((tm * SLAB, LANES), row),
                   pl.BlockSpec((tm, LANES), row),
                   pl.BlockSpec((1, LANES), full)],
        out_shape=[jax.ShapeDtypeStruct((t, d), F32),
                   jax.ShapeDtypeStruct((t * SLAB, LANES), F32),
                   jax.ShapeDtypeStruct((t, LANES), F32),
                   jax.ShapeDtypeStruct((1, LANES), F32)],
        scratch_shapes=[pltpu.VMEM((1, LANES), F32)],
        compiler_params=pltpu.CompilerParams(dimension_semantics=("arbitrary",)),
        name="outproj_route",
    )(da2, hg2, x2, mod3, wo_bf, norm_w, w_route, b_route, stril)


DMA_UNROLL = 8


def _scatter_body(last_ref, pos_ref, h2_ref, xs_hbm, zero_scr, sem, zsem, *, tile):
    rows = tile * SLAB

    @pl.when(pl.program_id(0) == 0)
    def _():
        zero_scr[...] = jnp.zeros(zero_scr.shape, F32)

        def clear(tile_idx):
            start = pl.multiple_of(tile_idx * (MOE_TILE * SLAB), SLAB)
            return pltpu.make_async_copy(zero_scr, xs_hbm.at[pl.ds(start, MOE_TILE * SLAB)], zsem)

        for e in range(N_EXPERTS):
            @pl.when(last_ref[e] >= 0)
            def _():
                clear(last_ref[e]).start()
        for e in range(N_EXPERTS):
            @pl.when(last_ref[e] >= 0)
            def _():
                clear(last_ref[e]).wait()

        def clear_unused(tile_idx, carry):
            clear(tile_idx).start()
            clear(tile_idx).wait()
            return carry

        lax.fori_loop(last_ref[N_EXPERTS], xs_hbm.shape[0] // (MOE_TILE * SLAB), clear_unused, 0)

    def issue(t, carry):
        src = h2_ref.at[pl.ds(pl.multiple_of(t * SLAB, SLAB), SLAB)]
        for j in range(2):
            slot = pl.multiple_of(pos_ref[0, 0, 2 * t + j] * SLAB, SLAB)
            pltpu.make_async_copy(src, xs_hbm.at[pl.ds(slot, SLAB)], sem).start(priority=j)
        return carry

    lax.fori_loop(0, tile, issue, 0, unroll=DMA_UNROLL)
    whole = pltpu.make_async_copy(h2_ref, xs_hbm.at[pl.ds(0, rows)], sem)
    whole.wait()
    whole.wait()


def _scatter(last_tile, pos3, h2s, n_slots):
    tile = GATHER_TILE
    rows = tile * SLAB
    grid_spec = pltpu.PrefetchScalarGridSpec(
        num_scalar_prefetch=1,
        grid=(h2s.shape[0] // rows,),
        in_specs=[pl.BlockSpec((1, 1, 2 * tile), lambda i, lt: (i, 0, 0), memory_space=pltpu.SMEM),
                  pl.BlockSpec((rows, LANES), lambda i, lt: (i, 0))],
        out_specs=pl.BlockSpec(memory_space=pl.ANY),
        scratch_shapes=[pltpu.VMEM((MOE_TILE * SLAB, LANES), F32),
                        pltpu.SemaphoreType.DMA, pltpu.SemaphoreType.DMA],
    )
    return pl.pallas_call(
        functools.partial(_scatter_body, tile=tile),
        grid_spec=grid_spec,
        out_shape=jax.ShapeDtypeStruct((n_slots * SLAB, LANES), F32),
        compiler_params=pltpu.CompilerParams(dimension_semantics=("arbitrary",)),
        name="moe_scatter",
    )(last_tile, pos3, h2s)


def _experts_body(te_ref, nv_ref, x_ref, wg_ref, wu_ref, wd_ref, y_ref, wgu_scr, wd_scr, *, tm):
    i = pl.program_id(0)
    e = te_ref[i]
    prev = te_ref[jnp.maximum(i - 1, 0)]

    @pl.when((i == 0) | (e != prev))
    def _():
        wgu_scr[:, 0:EXPERT_FF] = wg_ref[0].astype(BF16)
        wgu_scr[:, EXPERT_FF:] = wu_ref[0].astype(BF16)
        wd_scr[...] = wd_ref[0].astype(BF16)

    @pl.when(i < nv_ref[0])
    def _():
        half = tm // 2
        xr = [x_ref.at[pl.ds(r * half * SLAB, half * SLAB)] for r in range(2)]
        yr = [y_ref.at[pl.ds(r * half * SLAB, half * SLAB)] for r in range(2)]
        xs = [_slab_load(ref, half).astype(BF16) for ref in xr]
        gus = [jnp.dot(x, wgu_scr[...], preferred_element_type=F32) for x in xs]
        acts = [(gu[:, 0:EXPERT_FF] * _sigmoid(gu[:, 0:EXPERT_FF]) * gu[:, EXPERT_FF:]).astype(BF16)
                for gu in gus]
        ys = [jnp.dot(act, wd_scr[...], preferred_element_type=F32) for act in acts]
        for ref, y in zip(yr, ys):
            _slab_store(ref, y)

    @pl.when(i >= nv_ref[0])
    def _():
        y_ref[...] = jnp.zeros(y_ref.shape, F32)


def _experts(tile_expert, n_valid, xs, w_gate, w_up, w_down):
    tm = MOE_TILE
    rows = tm * SLAB
    d = w_gate.shape[1]
    live = lambda i, te, nv: (jnp.minimum(i, nv[0] - 1), 0)
    grid_spec = pltpu.PrefetchScalarGridSpec(
        num_scalar_prefetch=2,
        grid=(xs.shape[0] // rows,),
        in_specs=[pl.BlockSpec((rows, LANES), live),
                  pl.BlockSpec((1, d, EXPERT_FF), lambda i, te, nv: (te[i], 0, 0)),
                  pl.BlockSpec((1, d, EXPERT_FF), lambda i, te, nv: (te[i], 0, 0)),
                  pl.BlockSpec((1, EXPERT_FF, d), lambda i, te, nv: (te[i], 0, 0))],
        out_specs=pl.BlockSpec((rows, LANES), lambda i, te, nv: (i, 0)),
        scratch_shapes=[pltpu.VMEM((d, 2 * EXPERT_FF), BF16),
                        pltpu.VMEM((EXPERT_FF, d), BF16)],
    )
    return pl.pallas_call(
        functools.partial(_experts_body, tm=tm),
        grid_spec=grid_spec,
        out_shape=jax.ShapeDtypeStruct(xs.shape, F32),
        compiler_params=pltpu.CompilerParams(dimension_semantics=("arbitrary",)),
        name="moe_experts",
    )(tile_expert, n_valid, xs, w_gate, w_up, w_down)


def _combine_body(pos_ref, nxt_ref, x1_ref, meta_ref, mod_ref, nw_ref, ys_hbm, o_ref, rows_scr, sems,
                  *, tile):
    i = pl.program_id(0)
    n = pl.num_programs(0)
    rows = tile * SLAB

    def fetch(idx_ref, buf):
        def issue(t, carry):
            dst = pl.ds(pl.multiple_of(t * SLAB, SLAB), SLAB)
            for j in range(2):
                slot = pl.multiple_of(idx_ref[0, 0, 2 * t + j] * SLAB, SLAB)
                pltpu.make_async_copy(ys_hbm.at[pl.ds(slot, SLAB)], rows_scr.at[buf, j, dst],
                                      sems.at[buf]).start(priority=j)
            return carry
        lax.fori_loop(0, tile, issue, 0, unroll=DMA_UNROLL)

    @pl.when(i == 0)
    def _():
        fetch(pos_ref, 0)

    for cur in range(2):
        @pl.when((i % 2 == cur) & (i + 1 < n))
        def _():
            fetch(nxt_ref, 1 - cur)

    for cur in range(2):
        @pl.when(i % 2 == cur)
        def _():
            for j in range(2):
                pltpu.make_async_copy(ys_hbm.at[pl.ds(0, rows)], rows_scr.at[cur, j],
                                      sems.at[cur]).wait()
            meta = meta_ref[...]
            w0 = meta[:, META_W0:META_W0 + 1]
            w1 = meta[:, META_W1:META_W1 + 1]
            y = (w0 * _slab_load(rows_scr.at[cur, 0], tile)
                 + w1 * _slab_load(rows_scr.at[cur, 1], tile))
            x2 = x1_ref[...] + mod_ref[0, 5:6, :] * y
            o_ref[...] = (x2 * lax.rsqrt(jnp.mean(x2 * x2, axis=-1, keepdims=True) + EPS)
                          * nw_ref[...])


def _combine(pos3, x1, meta, mod3, norm_w, ys, seq):
    t, d = x1.shape
    tile = GATHER_TILE
    per_b = seq // tile
    n = t // tile
    row = lambda i: (i, 0)
    return pl.pallas_call(
        functools.partial(_combine_body, tile=tile),
        grid=(n,),
        in_specs=[pl.BlockSpec((1, 1, 2 * tile), lambda i: (i, 0, 0), memory_space=pltpu.SMEM),
                  pl.BlockSpec((1, 1, 2 * tile), lambda i: (jnp.minimum(i + 1, n - 1), 0, 0),
                               memory_space=pltpu.SMEM),
                  pl.BlockSpec((tile, d), row),
                  pl.BlockSpec((tile, LANES), row),
                  pl.BlockSpec((1, 6, d), lambda i: (i // per_b, 0, 0)),
                  pl.BlockSpec((1, d), lambda i: (0, 0)),
                  pl.BlockSpec(memory_space=pl.ANY)],
        out_specs=pl.BlockSpec((tile, d), row),
        out_shape=jax.ShapeDtypeStruct((t, d), F32),
        scratch_shapes=[pltpu.VMEM((2, 2, tile * SLAB, LANES), F32), pltpu.SemaphoreType.DMA((2,))],
        compiler_params=pltpu.CompilerParams(dimension_semantics=("arbitrary",)),
        name="moe_combine",
    )(pos3, pos3, x1, meta, mod3, norm_w, ys)


def _rope_constants():
    inv_freq = ROPE_THETA ** (-jnp.arange(ROT_HALF, dtype=F32) / ROT_HALF)
    lane = np.arange(LANES) % DA_QK_DIM
    hit = (lane[None, :] < ROT_DIM) & (lane[None, :] % ROT_HALF == np.arange(ROT_HALF)[:, None])
    spread = np.concatenate([hit, hit], axis=0).astype(np.float32)
    return inv_freq.reshape(ROT_HALF, 1), jnp.asarray(spread, BF16)


def kernel(x, c, positions, norm1_w, norm2_w, final_norm_w, ada_w, ada_b, w_in, w_out, da_lambda_q1, da_lambda_k1, da_lambda_q2, da_lambda_k2, da_subln_w, hg_lower_bound, hg_norm_w, moe_w_group, moe_b_group, moe_w_router, moe_b_router, moe_w_gate, moe_w_up, moe_w_down):
    bsz, seq, d = x.shape
    assert d == D_MODEL and norm1_w.shape[0] == 1, "single-layer model of width 1024 only"
    assert seq % ATTN_TILE == 0 and seq % HGRN_STEP == 0 and seq % OUT_TILE == 0
    t = bsz * seq
    x2 = x.reshape(t, d)

    mod3 = _adaln(c, ada_w[0], ada_b).reshape(bsz, 6, d)

    pos_rows = positions.astype(F32).reshape(t // ROW_TILE, 1, ROW_TILE)
    inv_freq, spread = _rope_constants()
    qt, k, vt, hq, lf, kf, gi, sg = _inproj(x2, mod3, norm1_w, w_in[0].astype(BF16), pos_rows,
                                          inv_freq, spread, hg_lower_bound, seq)

    as3 = lambda a: a.reshape(bsz, seq, STREAM_W)
    lam_p = jnp.concatenate([da_lambda_q1, da_lambda_k1, da_lambda_q2, da_lambda_k2], axis=0)
    da = _attn(qt, as3(k), vt, lam_p, da_subln_w.reshape(HEAD_W, 1))
    hg = _hgrn(as3(hq), as3(lf), as3(kf), as3(gi), as3(sg), hg_norm_w)

    pad = jnp.zeros((d, LANES - N_EXPERTS - N_GROUPS), F32)
    w_route = jnp.concatenate([moe_w_router[0], moe_w_group[0], pad], axis=1)
    b_route = jnp.concatenate([moe_b_router[0], moe_b_group[0], pad[0]]).reshape(1, LANES)
    w_route_hi = w_route.astype(BF16)
    w_route_lo = (w_route - w_route_hi.astype(F32)).astype(BF16)
    x1, h2, meta, cnt = _outproj(da.reshape(t, STREAM_W), hg.reshape(t, STREAM_W), x2, mod3,
                                 w_out[0].astype(BF16), norm2_w,
                                 jnp.concatenate([w_route_hi, w_route_lo], axis=1), b_route, seq)

    counts = cnt[0, :N_EXPERTS].astype(jnp.int32)
    tiles_e = (counts + MOE_TILE - 1) // MOE_TILE
    tile_end = jnp.cumsum(tiles_e)
    offs = (tile_end - tiles_e) * MOE_TILE
    ids = meta[:, META_E0:META_E1 + 1].astype(jnp.int32)
    ranks = meta[:, META_R0:META_R1 + 1].astype(jnp.int32)
    expert_iota = jnp.arange(N_EXPERTS, dtype=jnp.int32)
    pos = jnp.sum(jnp.where(ids[..., None] == expert_iota, offs, 0), axis=-1) + ranks
    n_tiles = (2 * t) // MOE_TILE + N_EXPERTS
    n_valid = tile_end[-1:]
    tile_ids = jnp.minimum(jnp.arange(n_tiles, dtype=jnp.int32), n_valid - 1)
    tile_expert = jnp.sum(tile_ids[:, None] >= tile_end[None, :], axis=1).astype(jnp.int32)
    pos3 = pos.reshape(t // GATHER_TILE, 1, 2 * GATHER_TILE)
    last_tile = jnp.concatenate([jnp.where(tiles_e > 0, tile_end - 1, -1), n_valid]).astype(jnp.int32)

    xs = _scatter(last_tile, pos3, h2, n_tiles * MOE_TILE)
    ys = _experts(tile_expert, n_valid.astype(jnp.int32), xs, moe_w_gate[0], moe_w_up[0], moe_w_down[0])
    out = _combine(pos3, x1, meta, mod3, final_norm_w.reshape(1, d), ys, seq)
    return out.reshape(bsz, seq, d)
```

```python
import functools
import math

import numpy as np
import jax
import jax.numpy as jnp
from jax import lax
from jax.experimental import pallas as pl
from jax.experimental.pallas import tpu as pltpu

F32 = jnp.float32
BF16 = jnp.bfloat16
HIGHEST = lax.Precision.HIGHEST

LANES = 128
SUBLANES = 8
D_MODEL = 1024
HEADS = 4
HEAD_W = 128
STREAM_W = HEADS * HEAD_W
N_STREAMS = 7
DA_QK_DIM = 64
ROPE_THETA = 500000.0
ROT_DIM = DA_QK_DIM // 4
ROT_HALF = ROT_DIM // 2
N_GROUPS = 4
EXPERTS_PER_GROUP = 8
N_EXPERTS = N_GROUPS * EXPERTS_PER_GROUP
EXPERT_FF = 256
EPS = 1e-6
LAM_INIT = 0.8 - 0.6 * math.exp(-0.3 * 0)
NEG_BIG = -1e30

ROW_TILE = 512
OUT_TILE = 512
ATTN_TILE = 1024
ATTN_KEY_TILE = 512
HGRN_BLOCK = 128
HGRN_STEP = 512
MOE_TILE = 512
GATHER_TILE = 256

NT_DIMS = (((1,), (1,)), ((), ()))
TN_DIMS = (((0,), (0,)), ((), ()))


def _sigmoid(x):
    return 1.0 / (1.0 + jnp.exp(-x))


SLAB = D_MODEL // LANES


def _slab_store(ref, val):
    rows = val.shape[0]
    for c in range(SLAB):
        ref[pl.ds(c, rows, stride=SLAB), :] = val[:, c * LANES:(c + 1) * LANES]


def _slab_load(ref, rows):
    return jnp.concatenate([ref[pl.ds(c, rows, stride=SLAB), :] for c in range(SLAB)], axis=1)


def _adaln_body(c_ref, w_ref, b_ref, o_ref):
    c = c_ref[...]
    ca = c * _sigmoid(c)
    o_ref[...] = jnp.dot(ca, w_ref[...], preferred_element_type=F32, precision=HIGHEST) + b_ref[...]


def _adaln(c, ada_w, ada_b):
    bsz, d = c.shape
    n = ada_w.shape[1]
    tn = 1024
    return pl.pallas_call(
        _adaln_body,
        grid=(n // tn,),
        in_specs=[pl.BlockSpec((bsz, d), lambda j: (0, 0)),
                  pl.BlockSpec((d, tn), lambda j: (0, j)),
                  pl.BlockSpec((1, tn), lambda j: (0, j))],
        out_specs=pl.BlockSpec((bsz, tn), lambda j: (0, j)),
        out_shape=jax.ShapeDtypeStruct((bsz, n), F32),
        name="adaln",
    )(c, ada_w, ada_b)


def _inproj_body(x_ref, mod_ref, nw_ref, w_ref, pos_ref, invf_ref, spread_ref, lbr_ref,
                 q_ref, k_ref, v_ref, hq_ref, lf_ref, kf_ref, gi_ref, sg_ref):
    x = x_ref[...]
    ms = jnp.mean(x * x, axis=-1, keepdims=True)
    y = x * lax.rsqrt(ms + EPS) * nw_ref[...]
    shift = mod_ref[0, 0:1, :]
    scale = mod_ref[0, 1:2, :]
    h = (y * (1.0 + scale) + shift).astype(BF16)

    ang_t = invf_ref[...] * pos_ref[0]

    def spread(table_t):
        hi = table_t.astype(BF16)
        lo = (table_t - hi.astype(F32)).astype(BF16)
        return lax.dot_general(jnp.concatenate([hi, lo], axis=0), spread_ref[...], TN_DIMS,
                               preferred_element_type=F32)

    lane = lax.broadcasted_iota(jnp.int32, (1, LANES), 1) % DA_QK_DIM
    cosv = spread(jnp.cos(ang_t)) + jnp.where(lane < ROT_DIM, 0.0, 1.0)
    sinv = spread(jnp.sin(ang_t))
    sin_lo = jnp.where(lane < ROT_HALF, -sinv, 0.0)
    sin_hi = jnp.where((lane >= ROT_HALF) & (lane < ROT_DIM), sinv, 0.0)

    streams = [jnp.dot(h, w_ref[:, j * STREAM_W:(j + 1) * STREAM_W], preferred_element_type=F32)
               for j in range(N_STREAMS)]

    def proj(j):
        return streams[j]

    def rope(t):
        outs = []
        for hb in range(HEADS):
            tc = t[:, hb * HEAD_W:(hb + 1) * HEAD_W]
            outs.append(tc * cosv
                        + pltpu.roll(tc, LANES - ROT_HALF, 1) * sin_lo
                        + pltpu.roll(tc, ROT_HALF, 1) * sin_hi)
        return jnp.concatenate(outs, axis=1)

    q_ref[0] = (rope(proj(0)) * (DA_QK_DIM ** -0.5 * math.log2(math.e))).T.astype(BF16)
    k_ref[...] = rope(proj(1)).astype(BF16)
    v_ref[0] = proj(2).astype(BF16).T

    gq = proj(3)
    hq_ref[...] = gq * _sigmoid(gq)

    a = lbr_ref[...]
    amax = jnp.max(a, axis=0, keepdims=True)
    ea = jnp.exp(a - amax)
    lb = ea[0:1, :] / jnp.sum(ea, axis=0, keepdims=True)
    gf = proj(4)
    f = lb + (1.0 - lb) * _sigmoid(gf)
    lf_ref[...] = jnp.log(f) * math.log2(math.e)
    kf_ref[...] = 1.0 - f

    gi_ref[...] = proj(5).astype(BF16)
    gg = proj(6)
    sg_ref[...] = (gg * _sigmoid(gg)).astype(BF16)


def _inproj(x2, mod3, norm_w, w_bf, pos_rows, invf, spread, lb_raw, seq):
    t, d = x2.shape
    tm = ROW_TILE
    per_b = seq // tm
    row = lambda i: (i, 0)
    full = lambda i: (0, 0)
    out_bf = jax.ShapeDtypeStruct((t, STREAM_W), BF16)
    out_f = jax.ShapeDtypeStruct((t, STREAM_W), F32)
    out_t = jax.ShapeDtypeStruct((t // seq, STREAM_W, seq), BF16)
    stream = pl.BlockSpec((tm, STREAM_W), row)
    stream_t = pl.BlockSpec((1, STREAM_W, tm), lambda i: (i // per_b, 0, i % per_b))
    return pl.pallas_call(
        _inproj_body,
        grid=(t // tm,),
        in_specs=[pl.BlockSpec((tm, d), row),
                  pl.BlockSpec((1, 6, d), lambda i: (i // per_b, 0, 0)),
                  pl.BlockSpec((1, d), full),
                  pl.BlockSpec((d, N_STREAMS * STREAM_W), full),
                  pl.BlockSpec((1, 1, tm), lambda i: (i, 0, 0)),
                  pl.BlockSpec((ROT_HALF, 1), full),
                  pl.BlockSpec((2 * ROT_HALF, LANES), full),
                  pl.BlockSpec(lb_raw.shape, full)],
        out_specs=[stream_t, stream, stream_t] + [stream] * 5,
        out_shape=[out_t, out_bf, out_t, out_f, out_f, out_f, out_bf, out_bf],
        compiler_params=pltpu.CompilerParams(vmem_limit_bytes=56 * 1024 * 1024),
        name="inproj",
    )(x2, mod3, norm_w, w_bf, pos_rows, invf, spread, lb_raw)


ONES_ROWS = 16


def _attn_body(qt_ref, k_ref, vt_ref, lam_ref, sw_ref, o_ref, s_scr, m_scr, a_scr, *, tile, ktile):
    qi = pl.program_id(2)
    qt = qt_ref[0]
    feat = lax.broadcasted_iota(jnp.int32, (HEAD_W, 1), 0)
    zero = jnp.zeros_like(qt)
    qmaps = (jnp.where(feat < DA_QK_DIM, qt, zero), jnp.where(feat >= DA_QK_DIM, qt, zero))

    m_scr[...] = jnp.full(m_scr.shape, NEG_BIG, F32)
    a_scr[...] = jnp.zeros(a_scr.shape, F32)

    def score_block(start, nkeys, q_lo, masked):
        kb = k_ref[0, pl.ds(start, nkeys), :]
        scores = [jnp.dot(kb, qmaps[mp][:, q_lo:], preferred_element_type=F32) for mp in range(2)]
        for mp, s in enumerate(scores):
            if masked:
                key = lax.broadcasted_iota(jnp.int32, s.shape, 0)
                qry = lax.broadcasted_iota(jnp.int32, s.shape, 1)
                s = jnp.where(key <= qry, s, NEG_BIG)
            grouped = s.reshape(nkeys // SUBLANES, SUBLANES, tile - q_lo)
            m_scr[mp, :, q_lo:] = jnp.maximum(m_scr[mp, :, q_lo:], jnp.max(grouped, axis=0))
            s_scr[mp, pl.ds(start, nkeys), q_lo:] = s

    def value_block(start, nkeys, q_lo, col_max):
        vb = jnp.concatenate([vt_ref[0, :, pl.ds(start, nkeys)], jnp.ones((ONES_ROWS, nkeys), BF16)],
                             axis=0)
        probs = [jnp.exp2(s_scr[mp, pl.ds(start, nkeys), q_lo:] - col_max[mp][:, q_lo:]).astype(BF16)
                 for mp in range(2)]
        for mp, p in enumerate(probs):
            a_scr[mp, :, q_lo:] += jnp.dot(vb, p, preferred_element_type=F32)

    diag = pl.multiple_of(qi * tile, tile)
    pieces = [(diag + j * ktile, ktile, j * ktile) for j in range(tile // ktile)]

    def score_body(ki, carry):
        score_block(pl.multiple_of(ki * tile, tile), tile, 0, False)
        return carry

    lax.fori_loop(0, qi, score_body, 0)
    for start, nkeys, q_lo in pieces:
        score_block(start, nkeys, q_lo, True)
    col_max = [jnp.max(m_scr[mp], axis=0, keepdims=True) for mp in range(2)]

    def value_body(ki, carry):
        value_block(pl.multiple_of(ki * tile, tile), tile, 0, col_max)
        return carry

    lax.fori_loop(0, qi, value_body, 0)
    for start, nkeys, q_lo in pieces:
        value_block(start, nkeys, q_lo, col_max)

    lp = lam_ref[...]
    lam = (jnp.exp(jnp.sum(lp[0:1] * lp[1:2], axis=-1, keepdims=True))
           - jnp.exp(jnp.sum(lp[2:3] * lp[3:4], axis=-1, keepdims=True)) + LAM_INIT)
    o = (a_scr[0, 0:HEAD_W, :] / a_scr[0, HEAD_W:HEAD_W + 1, :]
         - lam * (a_scr[1, 0:HEAD_W, :] / a_scr[1, HEAD_W:HEAD_W + 1, :]))
    o = o * lax.rsqrt(jnp.mean(o * o, axis=0, keepdims=True) + EPS) * sw_ref[...]
    o_ref[0] = (o * (1.0 - LAM_INIT)).T.astype(BF16)


def _attn(qt3, k3, vt3, lam_p, subln_col):
    bsz, seq, _ = k3.shape
    tile = ATTN_TILE
    return pl.pallas_call(
        functools.partial(_attn_body, tile=tile, ktile=ATTN_KEY_TILE),
        grid=(bsz, HEADS, seq // tile),
        in_specs=[pl.BlockSpec((1, HEAD_W, tile), lambda b, h, i: (b, h, i)),
                  pl.BlockSpec((1, seq, HEAD_W), lambda b, h, i: (b, 0, h)),
                  pl.BlockSpec((1, HEAD_W, seq), lambda b, h, i: (b, h, 0)),
                  pl.BlockSpec(lam_p.shape, lambda b, h, i: (0, 0)),
                  pl.BlockSpec((HEAD_W, 1), lambda b, h, i: (0, 0))],
        out_specs=pl.BlockSpec((1, tile, HEAD_W), lambda b, h, i: (b, i, h)),
        out_shape=jax.ShapeDtypeStruct((bsz, seq, STREAM_W), BF16),
        scratch_shapes=[pltpu.VMEM((2, seq, tile), F32),
                        pltpu.VMEM((2, SUBLANES, tile), F32),
                        pltpu.VMEM((2, HEAD_W + ONES_ROWS, tile), F32)],
        compiler_params=pltpu.CompilerParams(vmem_limit_bytes=56 * 1024 * 1024),
        name="diff_attn",
    )(qt3, k3, vt3, lam_p, subln_col)


def _hgrn_levels(block):
    return [block >> (i + 1) for i in range(block.bit_length() - 1)]


def _hgrn_constants(block):
    t = np.arange(block)[:, None]
    s = np.arange(block)[None, :]
    tril = (s <= t).astype(np.float32)
    lv = np.full((block, block), -1, np.int32)
    halves = _hgrn_levels(block)
    for li, m in enumerate(halves):
        same = (t // (2 * m)) == (s // (2 * m))
        lv[same & ((t & m) != 0) & ((s & m) == 0)] = li
    lv[np.arange(block), np.arange(block)] = len(halves)
    return jnp.asarray(tril, BF16), jnp.asarray(lv)


def _level_operand(b_ref, h, b, q, k, m, block):
    def ref_rows(r, n):
        return jnp.broadcast_to(b_ref[h, pl.ds(r, 1), :], (n, HEAD_W))

    if m >= SUBLANES:
        pieces = []
        for s0 in range(0, block, 2 * m):
            ref = ref_rows(s0 + m - 1, m)
            lo = slice(s0, s0 + m)
            up = slice(s0 + m, s0 + 2 * m)
            pieces.append(k[lo] * jnp.exp2(ref - b[lo]))
            pieces.append(q[up] * jnp.exp2(b[up] - ref))
        return jnp.concatenate(pieces, axis=0)

    sub = lax.broadcasted_iota(jnp.int32, (SUBLANES, HEAD_W), 0)
    refs = []
    for s0 in range(0, block, SUBLANES):
        piece = ref_rows(s0 + m - 1, SUBLANES)
        for j in range(1, SUBLANES // (2 * m)):
            piece = jnp.where(sub >= 2 * m * j, ref_rows(s0 + 2 * m * j + m - 1, SUBLANES), piece)
        refs.append(piece)
    d = b - jnp.concatenate(refs, axis=0)
    row = lax.broadcasted_iota(jnp.int32, (block, 1), 0)
    return jnp.where((row & m) != 0, q, k) * jnp.exp2(jnp.minimum(d, -d))


def _hgrn_body(hq_ref, lf_ref, kf_ref, gi_ref, sg_ref, nw_ref, tril_ref, lv_ref, o_ref,
               st_scr, b_scr, *, block, step):
    @pl.when(pl.program_id(1) == 0)
    def _():
        st_scr[...] = jnp.zeros(st_scr.shape, F32)

    tril = tril_ref[...]
    lv = lv_ref[...]
    halves = _hgrn_levels(block)

    units = [(u, r0, h) for u, (r0, h) in enumerate(
        (r0, h) for r0 in range(0, step, block) for h in range(HEADS))]

    def cols(h):
        return slice(h * HEAD_W, (h + 1) * HEAD_W)

    q, k, v, b, scores = {}, {}, {}, {}, {}
    for u, r0, h in units:
        rows = slice(r0, r0 + block)
        q[u] = hq_ref[0, rows, cols(h)]
        k[u] = kf_ref[0, rows, cols(h)]
        v[u] = gi_ref[0, rows, cols(h)]
        lf = lf_ref[0, rows, cols(h)]
        hi = lf.astype(BF16)
        r1 = lf - hi.astype(F32)
        mid = r1.astype(BF16)
        lo = (r1 - mid.astype(F32)).astype(BF16)
        b[u] = (jnp.dot(tril, hi, preferred_element_type=F32)
                + jnp.dot(tril, mid, preferred_element_type=F32)
                + jnp.dot(tril, lo, preferred_element_type=F32))
        b_scr[u] = b[u]
        scores[u] = jnp.where(lv == len(halves),
                              lax.dot_general(q[u].astype(BF16), k[u].astype(BF16), NT_DIMS,
                                              preferred_element_type=F32), 0.0)

    for li, m in enumerate(halves):
        for u, r0, h in units:
            xl = _level_operand(b_scr, u, b[u], q[u], k[u], m, block).astype(BF16)
            p = lax.dot_general(xl, xl, NT_DIMS, preferred_element_type=F32)
            scores[u] = jnp.where(lv == li, p, scores[u])

    for u, r0, h in units:
        rows = slice(r0, r0 + block)
        o_intra = jnp.dot(scores[u].astype(BF16), v[u], preferred_element_type=F32)
        st = st_scr[h]
        o_inter = lax.dot_general((q[u] * jnp.exp2(b[u])).astype(BF16), st.astype(BF16), NT_DIMS,
                                  preferred_element_type=F32)
        b_last = b[u][block - 1:block, :]
        kdec = (k[u] * jnp.exp2(b_last - b[u])).astype(BF16)
        st_scr[h] = st * jnp.exp2(b_last) + lax.dot_general(v[u], kdec, TN_DIMS,
                                                             preferred_element_type=F32)
        o = o_inter + o_intra
        o = o * lax.rsqrt(jnp.mean(o * o, axis=-1, keepdims=True) + EPS) * nw_ref[...]
        o_ref[0, rows, cols(h)] = (o * sg_ref[0, rows, cols(h)].astype(F32)).astype(BF16)


def _hgrn(hq3, lf3, kf3, gi3, sg3, norm_w):
    bsz, seq, _ = hq3.shape
    block = HGRN_BLOCK
    step = HGRN_STEP
    tril, lv = _hgrn_constants(block)
    blk = pl.BlockSpec((1, step, STREAM_W), lambda b, g: (b, g, 0))
    const = lambda b, g: (0, 0)
    return pl.pallas_call(
        functools.partial(_hgrn_body, block=block, step=step),
        grid=(bsz, seq // step),
        in_specs=[blk, blk, blk, blk, blk,
                  pl.BlockSpec((1, HEAD_W), const),
                  pl.BlockSpec((block, block), const),
                  pl.BlockSpec((block, block), const)],
        out_specs=blk,
        out_shape=jax.ShapeDtypeStruct((bsz, seq, STREAM_W), BF16),
        scratch_shapes=[pltpu.VMEM((HEADS, HEAD_W, HEAD_W), F32),
                        pltpu.VMEM((HEADS * step // block, block, HEAD_W), F32)],
        name="hgrn2",
    )(hq3, lf3, kf3, gi3, sg3, norm_w, tril, lv)


META_E0, META_E1, META_R0, META_R1, META_W0, META_W1 = range(6)
GROUP_LANE0 = N_EXPERTS


OUT_PARTS = 4


def _route(logits, lane):
    far = jnp.int32(LANES)

    def first_max(vals):
        mx = jnp.max(vals, axis=-1, keepdims=True)
        return mx, jnp.min(jnp.where(vals == mx, lane, far), axis=-1, keepdims=True)

    is_g = (lane >= GROUP_LANE0) & (lane < GROUP_LANE0 + N_GROUPS)
    gmax, glane = first_max(jnp.where(is_g, logits, NEG_BIG))
    g_w = 1.0 / jnp.sum(jnp.where(is_g, jnp.exp(logits - gmax), 0.0), axis=-1, keepdims=True)
    gidx = glane - GROUP_LANE0
    in_grp = (lane < N_EXPERTS) & ((lane // EXPERTS_PER_GROUP) == gidx)
    el = jnp.where(in_grp, logits, NEG_BIG)
    m1, i1 = first_max(el)
    m2, i2 = first_max(jnp.where(lane == i1, NEG_BIG, el))
    r = jnp.exp(m2 - m1)
    return i1, i2, g_w / (1.0 + r), g_w * r / (1.0 + r)


def _outproj_body(da_ref, hg_ref, x_ref, mod_ref, wo_ref, nw_ref, wr_ref, br_ref, stril_ref,
                  x1_ref, h2_ref, meta_ref, cnt_ref, carry_scr):
    @pl.when(pl.program_id(0) == 0)
    def _():
        carry_scr[...] = jnp.zeros(carry_scr.shape, F32)

    rows = stril_ref.shape[0]
    parts = [slice(p * rows, (p + 1) * rows) for p in range(OUT_PARTS)]
    gate1 = mod_ref[0, 2:3, :]
    shift2 = mod_ref[0, 3:4, :]
    scale2 = mod_ref[0, 4:5, :]

    attn = [jnp.dot(da_ref[r, :], wo_ref[0:STREAM_W, :], preferred_element_type=F32)
            + jnp.dot(hg_ref[r, :], wo_ref[STREAM_W:, :], preferred_element_type=F32) for r in parts]

    h2 = []
    for r, a in zip(parts, attn):
        x1 = x_ref[r, :] + gate1 * a
        x1_ref[r, :] = x1
        h2.append(x1 * lax.rsqrt(jnp.mean(x1 * x1, axis=-1, keepdims=True) + EPS) * nw_ref[...]
                  * (1.0 + scale2) + shift2)

    logits = []
    for p, h in enumerate(h2):
        _slab_store(h2_ref.at[pl.ds(p * rows * SLAB, rows * SLAB)], h)
        h_hi = h.astype(BF16)
        h_lo = (h - h_hi.astype(F32)).astype(BF16)
        terms = jnp.dot(jnp.concatenate([h_hi, h_lo], axis=0), wr_ref[...],
                        preferred_element_type=F32)
        logits.append((terms[:rows, :LANES] + terms[:rows, LANES:])
                      + (terms[rows:, :LANES] + terms[rows:, LANES:]) + br_ref[...])

    lane = lax.broadcasted_iota(jnp.int32, (rows, LANES), 1)
    routed = [_route(lg, lane) for lg in logits]

    carry = carry_scr[...]
    for r, (i1, i2, w0, w1) in zip(parts, routed):
        hot0 = lane == i1
        hot1 = lane == i2
        multi = jnp.where(hot0 | hot1, 1.0, 0.0)
        before = jnp.dot(stril_ref[...], multi.astype(BF16), preferred_element_type=F32) + carry
        rank0 = jnp.sum(jnp.where(hot0, before, 0.0), axis=-1, keepdims=True)
        rank1 = jnp.sum(jnp.where(hot1, before, 0.0), axis=-1, keepdims=True)
        carry = carry + jnp.sum(multi, axis=0, keepdims=True)
        meta = jnp.zeros((rows, LANES), F32)
        for idx, val in ((META_E0, i1.astype(F32)), (META_E1, i2.astype(F32)),
                         (META_R0, rank0), (META_R1, rank1), (META_W0, w0), (META_W1, w1)):
            meta = jnp.where(lane == idx, val, meta)
        meta_ref[r, :] = meta
    carry_scr[...] = carry
    cnt_ref[...] = carry


def _outproj(da2, hg2, x2, mod3, wo_bf, norm_w, w_route, b_route, seq):
    t, d = x2.shape
    tm = OUT_TILE
    per_b = seq // tm
    row = lambda i: (i, 0)
    full = lambda i: (0, 0)
    part = tm // OUT_PARTS
    stril = jnp.asarray(np.tril(np.ones((part, part), np.float32), -1), BF16)
    return pl.pallas_call(
        _outproj_body,
        grid=(t // tm,),
        in_specs=[pl.BlockSpec((tm, STREAM_W), row),
                  pl.BlockSpec((tm, STREAM_W), row),
                  pl.BlockSpec((tm, d), row),
                  pl.BlockSpec((1, 6, d), lambda i: (i // per_b, 0, 0)),
                  pl.BlockSpec((2 * STREAM_W, d), full),
                  pl.BlockSpec((1, d), full),
                  pl.BlockSpec((d, 2 * LANES), full),
                  pl.BlockSpec((1, LANES), full),
                  pl.BlockSpec((part, part), full)],
        out_specs=[pl.BlockSpec((tm, d), row),
                   pl.BlockSpec((tm * SLAB, LANES), row),
                   pl.BlockSpec((tm, LANES), row),
                   pl.BlockSpec((1, LANES), full)],
        out_shape=[jax.ShapeDtypeStruct((t, d), F32),
                   jax.ShapeDtypeStruct((t * SLAB, LANES), F32),
                   jax.ShapeDtypeStruct((t, LANES), F32),
                   jax.ShapeDtypeStruct((1, LANES), F32)],
        scratch_shapes=[pltpu.VMEM((1, LANES), F32)],
        compiler_params=pltpu.CompilerParams(dimension_semantics=("arbitrary",)),
        name="outproj_route",
    )(da2, hg2, x2, mod3, wo_bf, norm_w, w_route, b_route, stril)


DMA_UNROLL = 8


def _scatter_body(last_ref, pos_ref, h2_ref, xs_hbm, zero_scr, sem, zsem, *, tile):
    rows = tile * SLAB

    @pl.when(pl.program_id(0) == 0)
    def _():
        zero_scr[...] = jnp.zeros(zero_scr.shape, F32)

        def clear(tile_idx):
            start = pl.multiple_of(tile_idx * (MOE_TILE * SLAB), SLAB)
            return pltpu.make_async_copy(zero_scr, xs_hbm.at[pl.ds(start, MOE_TILE * SLAB)], zsem)

        for e in range(N_EXPERTS):
            @pl.when(last_ref[e] >= 0)
            def _():
                clear(last_ref[e]).start()
        for e in range(N_EXPERTS):
            @pl.when(last_ref[e] >= 0)
            def _():
                clear(last_ref[e]).wait()

        def clear_unused(tile_idx, carry):
            clear(tile_idx).start()
            clear(tile_idx).wait()
            return carry

        lax.fori_loop(last_ref[N_EXPERTS], xs_hbm.shape[0] // (MOE_TILE * SLAB), clear_unused, 0)

    def issue(t, carry):
        src = h2_ref.at[pl.ds(pl.multiple_of(t * SLAB, SLAB), SLAB)]
        for j in range(2):
            slot = pl.multiple_of(pos_ref[0, 0, 2 * t + j] * SLAB, SLAB)
            pltpu.make_async_copy(src, xs_hbm.at[pl.ds(slot, SLAB)], sem).start(priority=j)
        return carry

    lax.fori_loop(0, tile, issue, 0, unroll=DMA_UNROLL)
    whole = pltpu.make_async_copy(h2_ref, xs_hbm.at[pl.ds(0, rows)], sem)
    whole.wait()
    whole.wait()


def _scatter(last_tile, pos3, h2s, n_slots):
    tile = GATHER_TILE
    rows = tile * SLAB
    grid_spec = pltpu.PrefetchScalarGridSpec(
        num_scalar_prefetch=1,
        grid=(h2s.shape[0] // rows,),
        in_specs=[pl.BlockSpec((1, 1, 2 * tile), lambda i, lt: (i, 0, 0), memory_space=pltpu.SMEM),
                  pl.BlockSpec((rows, LANES), lambda i, lt: (i, 0))],
        out_specs=pl.BlockSpec(memory_space=pl.ANY),
        scratch_shapes=[pltpu.VMEM((MOE_TILE * SLAB, LANES), F32),
                        pltpu.SemaphoreType.DMA, pltpu.SemaphoreType.DMA],
    )
    return pl.pallas_call(
        functools.partial(_scatter_body, tile=tile),
        grid_spec=grid_spec,
        out_shape=jax.ShapeDtypeStruct((n_slots * SLAB, LANES), F32),
        compiler_params=pltpu.CompilerParams(dimension_semantics=("arbitrary",)),
        name="moe_scatter",
    )(last_tile, pos3, h2s)


def _experts_body(te_ref, nv_ref, x_ref, wg_ref, wu_ref, wd_ref, y_ref, wgu_scr, wd_scr, *, tm):
    i = pl.program_id(0)
    e = te_ref[i]
    prev = te_ref[jnp.maximum(i - 1, 0)]

    @pl.when((i == 0) | (e != prev))
    def _():
        wgu_scr[:, 0:EXPERT_FF] = wg_ref[0].astype(BF16)
        wgu_scr[:, EXPERT_FF:] = wu_ref[0].astype(BF16)
        wd_scr[...] = wd_ref[0].astype(BF16)

    @pl.when(i < nv_ref[0])
    def _():
        half = tm // 2
        xr = [x_ref.at[pl.ds(r * half * SLAB, half * SLAB)] for r in range(2)]
        yr = [y_ref.at[pl.ds(r * half * SLAB, half * SLAB)] for r in range(2)]
        xs = [_slab_load(ref, half).astype(BF16) for ref in xr]
        gus = [jnp.dot(x, wgu_scr[...], preferred_element_type=F32) for x in xs]
        acts = [(gu[:, 0:EXPERT_FF] * _sigmoid(gu[:, 0:EXPERT_FF]) * gu[:, EXPERT_FF:]).astype(BF16)
                for gu in gus]
        ys = [jnp.dot(act, wd_scr[...], preferred_element_type=F32) for act in acts]
        for ref, y in zip(yr, ys):
            _slab_store(ref, y)

    @pl.when(i >= nv_ref[0])
    def _():
        y_ref[...] = jnp.zeros(y_ref.shape, F32)


def _experts(tile_expert, n_valid, xs, w_gate, w_up, w_down):
    tm = MOE_TILE
    rows = tm * SLAB
    d = w_gate.shape[1]
    live = lambda i, te, nv: (jnp.minimum(i, nv[0] - 1), 0)
    grid_spec = pltpu.PrefetchScalarGridSpec(
        num_scalar_prefetch=2,
        grid=(xs.shape[0] // rows,),
        in_specs=[pl.BlockSpec((rows, LANES), live),
                  pl.BlockSpec((1, d, EXPERT_FF), lambda i, te, nv: (te[i], 0, 0)),
                  pl.BlockSpec((1, d, EXPERT_FF), lambda i, te, nv: (te[i], 0, 0)),
                  pl.BlockSpec((1, EXPERT_FF, d), lambda i, te, nv: (te[i], 0, 0))],
        out_specs=pl.BlockSpec((rows, LANES), lambda i, te, nv: (i, 0)),
        scratch_shapes=[pltpu.VMEM((d, 2 * EXPERT_FF), BF16),
                        pltpu.VMEM((EXPERT_FF, d), BF16)],
    )
    return pl.pallas_call(
        functools.partial(_experts_body, tm=tm),
        grid_spec=grid_spec,
        out_shape=jax.ShapeDtypeStruct(xs.shape, F32),
        compiler_params=pltpu.CompilerParams(dimension_semantics=("arbitrary",)),
        name="moe_experts",
    )(tile_expert, n_valid, xs, w_gate, w_up, w_down)


def _combine_body(pos_ref, nxt_ref, x1_ref, meta_ref, mod_ref, nw_ref, ys_hbm, o_ref, rows_scr, sems,
                  *, tile):
    i = pl.program_id(0)
    n = pl.num_programs(0)
    rows = tile * SLAB

    def fetch(idx_ref, buf):
        def issue(t, carry):
            dst = pl.ds(pl.multiple_of(t * SLAB, SLAB), SLAB)
            for j in range(2):
                slot = pl.multiple_of(idx_ref[0, 0, 2 * t + j] * SLAB, SLAB)
                pltpu.make_async_copy(ys_hbm.at[pl.ds(slot, SLAB)], rows_scr.at[buf, j, dst],
                                      sems.at[buf]).start(priority=j)
            return carry
        lax.fori_loop(0, tile, issue, 0, unroll=DMA_UNROLL)

    @pl.when(i == 0)
    def _():
        fetch(pos_ref, 0)

    for cur in range(2):
        @pl.when((i % 2 == cur) & (i + 1 < n))
        def _():
            fetch(nxt_ref, 1 - cur)

    for cur in range(2):
        @pl.when(i % 2 == cur)
        def _():
            for j in range(2):
                pltpu.make_async_copy(ys_hbm.at[pl.ds(0, rows)], rows_scr.at[cur, j],
                                      sems.at[cur]).wait()
            meta = meta_ref[...]
            w0 = meta[:, META_W0:META_W0 + 1]
            w1 = meta[:, META_W1:META_W1 + 1]
            y = (w0 * _slab_load(rows_scr.at[cur, 0], tile)
                 + w1 * _slab_load(rows_scr.at[cur, 1], tile))
            x2 = x1_ref[...] + mod_ref[0, 5:6, :] * y
            o_ref[...] = (x2 * lax.rsqrt(jnp.mean(x2 * x2, axis=-1, keepdims=True) + EPS)
                          * nw_ref[...])


def _combine(pos3, x1, meta, mod3, norm_w, ys, seq):
    t, d = x1.shape
    tile = GATHER_TILE
    per_b = seq // tile
    n = t // tile
    row = lambda i: (i, 0)
    return pl.pallas_call(
        functools.partial(_combine_body, tile=tile),
        grid=(n,),
        in_specs=[pl.BlockSpec((1, 1, 2 * tile), lambda i: (i, 0, 0), memory_space=pltpu.SMEM),
                  pl.BlockSpec((1, 1, 2 * tile), lambda i: (jnp.minimum(i + 1, n - 1), 0, 0),
                               memory_space=pltpu.SMEM),
                  pl.BlockSpec((tile, d), row),
                  pl.BlockSpec((tile, LANES), row),
                  pl.BlockSpec((1, 6, d), lambda i: (i // per_b, 0, 0)),
                  pl.BlockSpec((1, d), lambda i: (0, 0)),
                  pl.BlockSpec(memory_space=pl.ANY)],
        out_specs=pl.BlockSpec((tile, d), row),
        out_shape=jax.ShapeDtypeStruct((t, d), F32),
        scratch_shapes=[pltpu.VMEM((2, 2, tile * SLAB, LANES), F32), pltpu.SemaphoreType.DMA((2,))],
        compiler_params=pltpu.CompilerParams(dimension_semantics=("arbitrary",)),
        name="moe_combine",
    )(pos3, pos3, x1, meta, mod3, norm_w, ys)


def _rope_constants():
    inv_freq = ROPE_THETA ** (-jnp.arange(ROT_HALF, dtype=F32) / ROT_HALF)
    lane = np.arange(LANES) % DA_QK_DIM
    hit = (lane[None, :] < ROT_DIM) & (lane[None, :] % ROT_HALF == np.arange(ROT_HALF)[:, None])
    spread = np.concatenate([hit, hit], axis=0).astype(np.float32)
    return inv_freq.reshape(ROT_HALF, 1), jnp.asarray(spread, BF16)


def kernel(x, c, positions, norm1_w, norm2_w, final_norm_w, ada_w, ada_b, w_in, w_out, da_lambda_q1, da_lambda_k1, da_lambda_q2, da_lambda_k2, da_subln_w, hg_lower_bound, hg_norm_w, moe_w_group, moe_b_group, moe_w_router, moe_b_router, moe_w_gate, moe_w_up, moe_w_down):
    bsz, seq, d = x.shape
    assert d == D_MODEL and norm1_w.shape[0] == 1, "single-layer model of width 1024 only"
    assert seq % ATTN_TILE == 0 and seq % HGRN_STEP == 0 and seq % OUT_TILE == 0
    t = bsz * seq
    x2 = x.reshape(t, d)

    mod3 = _adaln(c, ada_w[0], ada_b).reshape(bsz, 6, d)

    pos_rows = positions.astype(F32).reshape(t // ROW_TILE, 1, ROW_TILE)
    inv_freq, spread = _rope_constants()
    qt, k, vt, hq, lf, kf, gi, sg = _inproj(x2, mod3, norm1_w, w_in[0].astype(BF16), pos_rows,
                                          inv_freq, spread, hg_lower_bound, seq)

    as3 = lambda a: a.reshape(bsz, seq, STREAM_W)
    lam_p = jnp.concatenate([da_lambda_q1, da_lambda_k1, da_lambda_q2, da_lambda_k2], axis=0)
    da = _attn(qt, as3(k), vt, lam_p, da_subln_w.reshape(HEAD_W, 1))
    hg = _hgrn(as3(hq), as3(lf), as3(kf), as3(gi), as3(sg), hg_norm_w)

    pad = jnp.zeros((d, LANES - N_EXPERTS - N_GROUPS), F32)
    w_route = jnp.concatenate([moe_w_router[0], moe_w_group[0], pad], axis=1)
    b_route = jnp.concatenate([moe_b_router[0], moe_b_group[0], pad[0]]).reshape(1, LANES)
    w_route_hi = w_route.astype(BF16)
    w_route_lo = (w_route - w_route_hi.astype(F32)).astype(BF16)
    x1, h2, meta, cnt = _outproj(da.reshape(t, STREAM_W), hg.reshape(t, STREAM_W), x2, mod3,
                                 w_out[0].astype(BF16), norm2_w,
                                 jnp.concatenate([w_route_hi, w_route_lo], axis=1), b_route, seq)

    counts = cnt[0, :N_EXPERTS].astype(jnp.int32)
    tiles_e = (counts + MOE_TILE - 1) // MOE_TILE
    tile_end = jnp.cumsum(tiles_e)
    offs = (tile_end - tiles_e) * MOE_TILE
    ids = meta[:, META_E0:META_E1 + 1].astype(jnp.int32)
    ranks = meta[:, META_R0:META_R1 + 1].astype(jnp.int32)
    expert_iota = jnp.arange(N_EXPERTS, dtype=jnp.int32)
    pos = jnp.sum(jnp.where(ids[..., None] == expert_iota, offs, 0), axis=-1) + ranks
    n_tiles = (2 * t) // MOE_TILE + N_EXPERTS
    n_valid = tile_end[-1:]
    tile_ids = jnp.minimum(jnp.arange(n_tiles, dtype=jnp.int32), n_valid - 1)
    tile_expert = jnp.sum(tile_ids[:, None] >= tile_end[None, :], axis=1).astype(jnp.int32)
    pos3 = pos.reshape(t // GATHER_TILE, 1, 2 * GATHER_TILE)
    last_tile = jnp.concatenate([jnp.where(tiles_e > 0, tile_end - 1, -1), n_valid]).astype(jnp.int32)

    xs = _scatter(last_tile, pos3, h2, n_tiles * MOE_TILE)
    ys = _experts(tile_expert, n_valid.astype(jnp.int32), xs, moe_w_gate[0], moe_w_up[0], moe_w_down[0])
    out = _combine(pos3, x1, meta, mod3, final_norm_w.reshape(1, d), ys, seq)
    return out.reshape(bsz, seq, d)
```

```python
import functools
import math

import numpy as np
import jax
import jax.numpy as jnp
from jax import lax
from jax.experimental import pallas as pl
from jax.experimental.pallas import tpu as pltpu

F32 = jnp.float32
BF16 = jnp.bfloat16
HIGHEST = lax.Precision.HIGHEST

LANES = 128
SUBLANES = 8
D_MODEL = 1024
HEADS = 4
HEAD_W = 128
STREAM_W = HEADS * HEAD_W
N_STREAMS = 7
DA_QK_DIM = 64
ROPE_THETA = 500000.0
ROT_DIM = DA_QK_DIM // 4
ROT_HALF = ROT_DIM // 2
N_GROUPS = 4
EXPERTS_PER_GROUP = 8
N_EXPERTS = N_GROUPS * EXPERTS_PER_GROUP
EXPERT_FF = 256
EPS = 1e-6
LAM_INIT = 0.8 - 0.6 * math.exp(-0.3 * 0)
NEG_BIG = -1e30

ROW_TILE = 512
OUT_TILE = 512
ATTN_TILE = 1024
ATTN_KEY_TILE = 512
HGRN_BLOCK = 128
HGRN_STEP = 256
MOE_TILE = 512
GATHER_TILE = 256

NT_DIMS = (((1,), (1,)), ((), ()))
TN_DIMS = (((0,), (0,)), ((), ()))


def _sigmoid(x):
    return 1.0 / (1.0 + jnp.exp(-x))


SLAB = D_MODEL // LANES


def _slab_store(ref, val):
    rows = val.shape[0]
    for c in range(SLAB):
        ref[pl.ds(c, rows, stride=SLAB), :] = val[:, c * LANES:(c + 1) * LANES]


def _slab_load(ref, rows):
    return jnp.concatenate([ref[pl.ds(c, rows, stride=SLAB), :] for c in range(SLAB)], axis=1)


def _adaln_body(c_ref, w_ref, b_ref, o_ref):
    c = c_ref[...]
    ca = c * _sigmoid(c)
    o_ref[...] = jnp.dot(ca, w_ref[...], preferred_element_type=F32, precision=HIGHEST) + b_ref[...]


def _adaln(c, ada_w, ada_b):
    bsz, d = c.shape
    n = ada_w.shape[1]
    tn = 1024
    return pl.pallas_call(
        _adaln_body,
        grid=(n // tn,),
        in_specs=[pl.BlockSpec((bsz, d), lambda j: (0, 0)),
                  pl.BlockSpec((d, tn), lambda j: (0, j)),
                  pl.BlockSpec((1, tn), lambda j: (0, j))],
        out_specs=pl.BlockSpec((bsz, tn), lambda j: (0, j)),
        out_shape=jax.ShapeDtypeStruct((bsz, n), F32),
        name="adaln",
    )(c, ada_w, ada_b)


def _inproj_body(x_ref, mod_ref, nw_ref, w_ref, pos_ref, invf_ref, spread_ref, lbr_ref,
                 q_ref, k_ref, v_ref, hq_ref, lf_ref, kf_ref, gi_ref, sg_ref):
    x = x_ref[...]
    ms = jnp.mean(x * x, axis=-1, keepdims=True)
    y = x * lax.rsqrt(ms + EPS) * nw_ref[...]
    shift = mod_ref[0, 0:1, :]
    scale = mod_ref[0, 1:2, :]
    h = (y * (1.0 + scale) + shift).astype(BF16)

    ang_t = invf_ref[...] * pos_ref[0]

    def spread(table_t):
        hi = table_t.astype(BF16)
        lo = (table_t - hi.astype(F32)).astype(BF16)
        return lax.dot_general(jnp.concatenate([hi, lo], axis=0), spread_ref[...], TN_DIMS,
                               preferred_element_type=F32)

    lane = lax.broadcasted_iota(jnp.int32, (1, LANES), 1) % DA_QK_DIM
    cosv = spread(jnp.cos(ang_t)) + jnp.where(lane < ROT_DIM, 0.0, 1.0)
    sinv = spread(jnp.sin(ang_t))
    sin_lo = jnp.where(lane < ROT_HALF, -sinv, 0.0)
    sin_hi = jnp.where((lane >= ROT_HALF) & (lane < ROT_DIM), sinv, 0.0)

    streams = [jnp.dot(h, w_ref[:, j * STREAM_W:(j + 1) * STREAM_W], preferred_element_type=F32)
               for j in range(N_STREAMS)]

    def proj(j):
        return streams[j]

    def rope(t):
        outs = []
        for hb in range(HEADS):
            tc = t[:, hb * HEAD_W:(hb + 1) * HEAD_W]
            outs.append(tc * cosv
                        + pltpu.roll(tc, LANES - ROT_HALF, 1) * sin_lo
                        + pltpu.roll(tc, ROT_HALF, 1) * sin_hi)
        return jnp.concatenate(outs, axis=1)

    q_ref[0] = (rope(proj(0)) * (DA_QK_DIM ** -0.5 * math.log2(math.e))).T.astype(BF16)
    k_ref[...] = rope(proj(1)).astype(BF16)
    v_ref[0] = proj(2).astype(BF16).T

    gq = proj(3)
    hq_ref[...] = gq * _sigmoid(gq)

    a = lbr_ref[...]
    amax = jnp.max(a, axis=0, keepdims=True)
    ea = jnp.exp(a - amax)
    lb = ea[0:1, :] / jnp.sum(ea, axis=0, keepdims=True)
    gf = proj(4)
    f = lb + (1.0 - lb) * _sigmoid(gf)
    lf_ref[...] = jnp.log(f) * math.log2(math.e)
    kf_ref[...] = 1.0 - f

    gi_ref[...] = proj(5).astype(BF16)
    gg = proj(6)
    sg_ref[...] = (gg * _sigmoid(gg)).astype(BF16)


def _inproj(x2, mod3, norm_w, w_bf, pos_rows, invf, spread, lb_raw, seq):
    t, d = x2.shape
    tm = ROW_TILE
    per_b = seq // tm
    row = lambda i: (i, 0)
    full = lambda i: (0, 0)
    out_bf = jax.ShapeDtypeStruct((t, STREAM_W), BF16)
    out_f = jax.ShapeDtypeStruct((t, STREAM_W), F32)
    out_t = jax.ShapeDtypeStruct((t // seq, STREAM_W, seq), BF16)
    stream = pl.BlockSpec((tm, STREAM_W), row)
    stream_t = pl.BlockSpec((1, STREAM_W, tm), lambda i: (i // per_b, 0, i % per_b))
    return pl.pallas_call(
        _inproj_body,
        grid=(t // tm,),
        in_specs=[pl.BlockSpec((tm, d), row),
                  pl.BlockSpec((1, 6, d), lambda i: (i // per_b, 0, 0)),
                  pl.BlockSpec((1, d), full),
                  pl.BlockSpec((d, N_STREAMS * STREAM_W), full),
                  pl.BlockSpec((1, 1, tm), lambda i: (i, 0, 0)),
                  pl.BlockSpec((ROT_HALF, 1), full),
                  pl.BlockSpec((2 * ROT_HALF, LANES), full),
                  pl.BlockSpec(lb_raw.shape, full)],
        out_specs=[stream_t, stream, stream_t] + [stream] * 5,
        out_shape=[out_t, out_bf, out_t, out_f, out_f, out_f, out_bf, out_bf],
        compiler_params=pltpu.CompilerParams(vmem_limit_bytes=56 * 1024 * 1024),
        name="inproj",
    )(x2, mod3, norm_w, w_bf, pos_rows, invf, spread, lb_raw)


ONES_ROWS = 16


def _attn_body(qt_ref, k_ref, vt_ref, lam_ref, sw_ref, o_ref, s_scr, m_scr, a_scr, *, tile, ktile):
    qi = pl.program_id(2)
    qt = qt_ref[0]
    feat = lax.broadcasted_iota(jnp.int32, (HEAD_W, 1), 0)
    zero = jnp.zeros_like(qt)
    qmaps = (jnp.where(feat < DA_QK_DIM, qt, zero), jnp.where(feat >= DA_QK_DIM, qt, zero))

    m_scr[...] = jnp.full(m_scr.shape, NEG_BIG, F32)
    a_scr[...] = jnp.zeros(a_scr.shape, F32)

    def score_block(start, nkeys, q_lo, masked):
        kb = k_ref[0, pl.ds(start, nkeys), :]
        scores = [jnp.dot(kb, qmaps[mp][:, q_lo:], preferred_element_type=F32) for mp in range(2)]
        for mp, s in enumerate(scores):
            if masked:
                key = lax.broadcasted_iota(jnp.int32, s.shape, 0)
                qry = lax.broadcasted_iota(jnp.int32, s.shape, 1)
                s = jnp.where(key <= qry, s, NEG_BIG)
            grouped = s.reshape(nkeys // SUBLANES, SUBLANES, tile - q_lo)
            m_scr[mp, :, q_lo:] = jnp.maximum(m_scr[mp, :, q_lo:], jnp.max(grouped, axis=0))
            s_scr[mp, pl.ds(start, nkeys), q_lo:] = s

    def value_block(start, nkeys, q_lo, col_max):
        vb = jnp.concatenate([vt_ref[0, :, pl.ds(start, nkeys)], jnp.ones((ONES_ROWS, nkeys), BF16)],
                             axis=0)
        probs = [jnp.exp2(s_scr[mp, pl.ds(start, nkeys), q_lo:] - col_max[mp][:, q_lo:]).astype(BF16)
                 for mp in range(2)]
        for mp, p in enumerate(probs):
            a_scr[mp, :, q_lo:] += jnp.dot(vb, p, preferred_element_type=F32)

    diag = pl.multiple_of(qi * tile, tile)
    pieces = [(diag + j * ktile, ktile, j * ktile) for j in range(tile // ktile)]

    def score_body(ki, carry):
        score_block(pl.multiple_of(ki * tile, tile), tile, 0, False)
        return carry

    lax.fori_loop(0, qi, score_body, 0)
    for start, nkeys, q_lo in pieces:
        score_block(start, nkeys, q_lo, True)
    col_max = [jnp.max(m_scr[mp], axis=0, keepdims=True) for mp in range(2)]

    def value_body(ki, carry):
        value_block(pl.multiple_of(ki * tile, tile), tile, 0, col_max)
        return carry

    lax.fori_loop(0, qi, value_body, 0)
    for start, nkeys, q_lo in pieces:
        value_block(start, nkeys, q_lo, col_max)

    lp = lam_ref[...]
    lam = (jnp.exp(jnp.sum(lp[0:1] * lp[1:2], axis=-1, keepdims=True))
           - jnp.exp(jnp.sum(lp[2:3] * lp[3:4], axis=-1, keepdims=True)) + LAM_INIT)
    o = (a_scr[0, 0:HEAD_W, :] / a_scr[0, HEAD_W:HEAD_W + 1, :]
         - lam * (a_scr[1, 0:HEAD_W, :] / a_scr[1, HEAD_W:HEAD_W + 1, :]))
    o = o * lax.rsqrt(jnp.mean(o * o, axis=0, keepdims=True) + EPS) * sw_ref[...]
    o_ref[0] = (o * (1.0 - LAM_INIT)).T.astype(BF16)


def _attn(qt3, k3, vt3, lam_p, subln_col):
    bsz, seq, _ = k3.shape
    tile = ATTN_TILE
    return pl.pallas_call(
        functools.partial(_attn_body, tile=tile, ktile=ATTN_KEY_TILE),
        grid=(bsz, HEADS, seq // tile),
        in_specs=[pl.BlockSpec((1, HEAD_W, tile), lambda b, h, i: (b, h, i)),
                  pl.BlockSpec((1, seq, HEAD_W), lambda b, h, i: (b, 0, h)),
                  pl.BlockSpec((1, HEAD_W, seq), lambda b, h, i: (b, h, 0)),
                  pl.BlockSpec(lam_p.shape, lambda b, h, i: (0, 0)),
                  pl.BlockSpec((HEAD_W, 1), lambda b, h, i: (0, 0))],
        out_specs=pl.BlockSpec((1, tile, HEAD_W), lambda b, h, i: (b, i, h)),
        out_shape=jax.ShapeDtypeStruct((bsz, seq, STREAM_W), BF16),
        scratch_shapes=[pltpu.VMEM((2, seq, tile), F32),
                        pltpu.VMEM((2, SUBLANES, tile), F32),
                        pltpu.VMEM((2, HEAD_W + ONES_ROWS, tile), F32)],
        compiler_params=pltpu.CompilerParams(vmem_limit_bytes=56 * 1024 * 1024),
        name="diff_attn",
    )(qt3, k3, vt3, lam_p, subln_col)


def _hgrn_levels(block):
    return [block >> (i + 1) for i in range(block.bit_length() - 1)]


def _hgrn_constants(block):
    t = np.arange(block)[:, None]
    s = np.arange(block)[None, :]
    tril = (s <= t).astype(np.float32)
    lv = np.full((block, block), -1, np.int32)
    halves = _hgrn_levels(block)
    for li, m in enumerate(halves):
        same = (t // (2 * m)) == (s // (2 * m))
        lv[same & ((t & m) != 0) & ((s & m) == 0)] = li
    lv[np.arange(block), np.arange(block)] = len(halves)
    return jnp.asarray(tril, BF16), jnp.asarray(lv)


def _level_operand(b_ref, h, b, q, k, m, block):
    def ref_rows(r, n):
        return jnp.broadcast_to(b_ref[h, pl.ds(r, 1), :], (n, HEAD_W))

    if m >= SUBLANES:
        pieces = []
        for s0 in range(0, block, 2 * m):
            ref = ref_rows(s0 + m - 1, m)
            lo = slice(s0, s0 + m)
            up = slice(s0 + m, s0 + 2 * m)
            pieces.append(k[lo] * jnp.exp2(ref - b[lo]))
            pieces.append(q[up] * jnp.exp2(b[up] - ref))
        return jnp.concatenate(pieces, axis=0)

    sub = lax.broadcasted_iota(jnp.int32, (SUBLANES, HEAD_W), 0)
    refs = []
    for s0 in range(0, block, SUBLANES):
        piece = ref_rows(s0 + m - 1, SUBLANES)
        for j in range(1, SUBLANES // (2 * m)):
            piece = jnp.where(sub >= 2 * m * j, ref_rows(s0 + 2 * m * j + m - 1, SUBLANES), piece)
        refs.append(piece)
    d = b - jnp.concatenate(refs, axis=0)
    row = lax.broadcasted_iota(jnp.int32, (block, 1), 0)
    return jnp.where((row & m) != 0, q, k) * jnp.exp2(jnp.minimum(d, -d))


def _hgrn_body(hq_ref, lf_ref, kf_ref, gi_ref, sg_ref, nw_ref, tril_ref, lv_ref, o_ref,
               st_scr, b_scr, *, block, step):
    @pl.when(pl.program_id(1) == 0)
    def _():
        st_scr[...] = jnp.zeros(st_scr.shape, F32)

    tril = tril_ref[...]
    lv = lv_ref[...]
    halves = _hgrn_levels(block)

    units = [(u, r0, h) for u, (r0, h) in enumerate(
        (r0, h) for r0 in range(0, step, block) for h in range(HEADS))]

    def cols(h):
        return slice(h * HEAD_W, (h + 1) * HEAD_W)

    q, k, v, b, scores = {}, {}, {}, {}, {}
    for u, r0, h in units:
        rows = slice(r0, r0 + block)
        q[u] = hq_ref[0, rows, cols(h)]
        k[u] = kf_ref[0, rows, cols(h)]
        v[u] = gi_ref[0, rows, cols(h)]
        lf = lf_ref[0, rows, cols(h)]
        hi = lf.astype(BF16)
        r1 = lf - hi.astype(F32)
        mid = r1.astype(BF16)
        lo = (r1 - mid.astype(F32)).astype(BF16)
        b[u] = (jnp.dot(tril, hi, preferred_element_type=F32)
                + jnp.dot(tril, mid, preferred_element_type=F32)
                + jnp.dot(tril, lo, preferred_element_type=F32))
        b_scr[u] = b[u]
        scores[u] = jnp.where(lv == len(halves),
                              lax.dot_general(q[u].astype(BF16), k[u].astype(BF16), NT_DIMS,
                                              preferred_element_type=F32), 0.0)

    for li, m in enumerate(halves):
        for u, r0, h in units:
            xl = _level_operand(b_scr, u, b[u], q[u], k[u], m, block).astype(BF16)
            p = lax.dot_general(xl, xl, NT_DIMS, preferred_element_type=F32)
            scores[u] = jnp.where(lv == li, p, scores[u])

    for u, r0, h in units:
        rows = slice(r0, r0 + block)
        o_intra = jnp.dot(scores[u].astype(BF16), v[u], preferred_element_type=F32)
        st = st_scr[h]
        o_inter = lax.dot_general((q[u] * jnp.exp2(b[u])).astype(BF16), st.astype(BF16), NT_DIMS,
                                  preferred_element_type=F32)
        b_last = b[u][block - 1:block, :]
        kdec = (k[u] * jnp.exp2(b_last - b[u])).astype(BF16)
        st_scr[h] = st * jnp.exp2(b_last) + lax.dot_general(v[u], kdec, TN_DIMS,
                                                             preferred_element_type=F32)
        o = o_inter + o_intra
        o = o * lax.rsqrt(jnp.mean(o * o, axis=-1, keepdims=True) + EPS) * nw_ref[...]
        o_ref[0, rows, cols(h)] = (o * sg_ref[0, rows, cols(h)].astype(F32)).astype(BF16)


def _hgrn(hq3, lf3, kf3, gi3, sg3, norm_w):
    bsz, seq, _ = hq3.shape
    block = HGRN_BLOCK
    step = HGRN_STEP
    tril, lv = _hgrn_constants(block)
    blk = pl.BlockSpec((1, step, STREAM_W), lambda b, g: (b, g, 0))
    const = lambda b, g: (0, 0)
    return pl.pallas_call(
        functools.partial(_hgrn_body, block=block, step=step),
        grid=(bsz, seq // step),
        in_specs=[blk, blk, blk, blk, blk,
                  pl.BlockSpec((1, HEAD_W), const),
                  pl.BlockSpec((block, block), const),
                  pl.BlockSpec((block, block), const)],
        out_specs=blk,
        out_shape=jax.ShapeDtypeStruct((bsz, seq, STREAM_W), BF16),
        scratch_shapes=[pltpu.VMEM((HEADS, HEAD_W, HEAD_W), F32),
                        pltpu.VMEM((HEADS * step // block, block, HEAD_W), F32)],
        name="hgrn2",
    )(hq3, lf3, kf3, gi3, sg3, norm_w, tril, lv)


META_E0, META_E1, META_R0, META_R1, META_W0, META_W1 = range(6)
GROUP_LANE0 = N_EXPERTS


OUT_PARTS = 4


def _route(logits, lane):
    far = jnp.int32(LANES)

    def first_max(vals):
        mx = jnp.max(vals, axis=-1, keepdims=True)
        return mx, jnp.min(jnp.where(vals == mx, lane, far), axis=-1, keepdims=True)

    is_g = (lane >= GROUP_LANE0) & (lane < GROUP_LANE0 + N_GROUPS)
    gmax, glane = first_max(jnp.where(is_g, logits, NEG_BIG))
    g_w = 1.0 / jnp.sum(jnp.where(is_g, jnp.exp(logits - gmax), 0.0), axis=-1, keepdims=True)
    gidx = glane - GROUP_LANE0
    in_grp = (lane < N_EXPERTS) & ((lane // EXPERTS_PER_GROUP) == gidx)
    el = jnp.where(in_grp, logits, NEG_BIG)
    m1, i1 = first_max(el)
    m2, i2 = first_max(jnp.where(lane == i1, NEG_BIG, el))
    r = jnp.exp(m2 - m1)
    return i1, i2, g_w / (1.0 + r), g_w * r / (1.0 + r)


def _outproj_body(da_ref, hg_ref, x_ref, mod_ref, wo_ref, nw_ref, wr_ref, br_ref, stril_ref,
                  x1_ref, h2_ref, meta_ref, cnt_ref, carry_scr):
    @pl.when(pl.program_id(0) == 0)
    def _():
        carry_scr[...] = jnp.zeros(carry_scr.shape, F32)

    rows = stril_ref.shape[0]
    parts = [slice(p * rows, (p + 1) * rows) for p in range(OUT_PARTS)]
    gate1 = mod_ref[0, 2:3, :]
    shift2 = mod_ref[0, 3:4, :]
    scale2 = mod_ref[0, 4:5, :]

    attn = [jnp.dot(da_ref[r, :], wo_ref[0:STREAM_W, :], preferred_element_type=F32)
            + jnp.dot(hg_ref[r, :], wo_ref[STREAM_W:, :], preferred_element_type=F32) for r in parts]

    h2 = []
    for r, a in zip(parts, attn):
        x1 = x_ref[r, :] + gate1 * a
        x1_ref[r, :] = x1
        h2.append(x1 * lax.rsqrt(jnp.mean(x1 * x1, axis=-1, keepdims=True) + EPS) * nw_ref[...]
                  * (1.0 + scale2) + shift2)

    logits = []
    for p, h in enumerate(h2):
        _slab_store(h2_ref.at[pl.ds(p * rows * SLAB, rows * SLAB)], h)
        h_hi = h.astype(BF16)
        h_lo = (h - h_hi.astype(F32)).astype(BF16)
        terms = jnp.dot(jnp.concatenate([h_hi, h_lo], axis=0), wr_ref[...],
                        preferred_element_type=F32)
        logits.append((terms[:rows, :LANES] + terms[:rows, LANES:])
                      + (terms[rows:, :LANES] + terms[rows:, LANES:]) + br_ref[...])

    lane = lax.broadcasted_iota(jnp.int32, (rows, LANES), 1)
    routed = [_route(lg, lane) for lg in logits]

    carry = carry_scr[...]
    for r, (i1, i2, w0, w1) in zip(parts, routed):
        hot0 = lane == i1
        hot1 = lane == i2
        multi = jnp.where(hot0 | hot1, 1.0, 0.0)
        before = jnp.dot(stril_ref[...], multi.astype(BF16), preferred_element_type=F32) + carry
        rank0 = jnp.sum(jnp.where(hot0, before, 0.0), axis=-1, keepdims=True)
        rank1 = jnp.sum(jnp.where(hot1, before, 0.0), axis=-1, keepdims=True)
        carry = carry + jnp.sum(multi, axis=0, keepdims=True)
        meta = jnp.zeros((rows, LANES), F32)
        for idx, val in ((META_E0, i1.astype(F32)), (META_E1, i2.astype(F32)),
                         (META_R0, rank0), (META_R1, rank1), (META_W0, w0), (META_W1, w1)):
            meta = jnp.where(lane == idx, val, meta)
        meta_ref[r, :] = meta
    carry_scr[...] = carry
    cnt_ref[...] = carry


def _outproj(da2, hg2, x2, mod3, wo_bf, norm_w, w_route, b_route, seq):
    t, d = x2.shape
    tm = OUT_TILE
    per_b = seq // tm
    row = lambda i: (i, 0)
    full = lambda i: (0, 0)
    part = tm // OUT_PARTS
    stril = jnp.asarray(np.tril(np.ones((part, part), np.float32), -1), BF16)
    return pl.pallas_call(
        _outproj_body,
        grid=(t // tm,),
        in_specs=[pl.BlockSpec((tm, STREAM_W), row),
                  pl.BlockSpec((tm, STREAM_W), row),
                  pl.BlockSpec((tm, d), row),
                  pl.BlockSpec((1, 6, d), lambda i: (i // per_b, 0, 0)),
                  pl.BlockSpec((2 * STREAM_W, d), full),
                  pl.BlockSpec((1, d), full),
                  pl.BlockSpec((d, 2 * LANES), full),
                  pl.BlockSpec((1, LANES), full),
                  pl.BlockSpec((part, part), full)],
        out_specs=[pl.BlockSpec((tm, d), row),
                   pl.BlockSpec((tm * SLAB, LANES), row),
                   pl.BlockSpec((tm, LANES), row),
                   pl.BlockSpec((1, LANES), full)],
        out_shape=[jax.ShapeDtypeStruct((t, d), F32),
                   jax.ShapeDtypeStruct((t * SLAB, LANES), F32),
                   jax.ShapeDtypeStruct((t, LANES), F32),
                   jax.ShapeDtypeStruct((1, LANES), F32)],
        scratch_shapes=[pltpu.VMEM((1, LANES), F32)],
        compiler_params=pltpu.CompilerParams(dimension_semantics=("arbitrary",)),
        name="outproj_route",
    )(da2, hg2, x2, mod3, wo_bf, norm_w, w_route, b_route, stril)


DMA_UNROLL = 8


def _scatter_body(last_ref, pos_ref, h2_ref, xs_hbm, zero_scr, sem, zsem, *, tile):
    rows = tile * SLAB

    @pl.when(pl.program_id(0) == 0)
    def _():
        zero_scr[...] = jnp.zeros(zero_scr.shape, F32)

        def clear(tile_idx):
            start = pl.multiple_of(tile_idx * (MOE_TILE * SLAB), SLAB)
            return pltpu.make_async_copy(zero_scr, xs_hbm.at[pl.ds(start, MOE_TILE * SLAB)], zsem)

        for e in range(N_EXPERTS):
            @pl.when(last_ref[e] >= 0)
            def _():
                clear(last_ref[e]).start()
        for e in range(N_EXPERTS):
            @pl.when(last_ref[e] >= 0)
            def _():
                clear(last_ref[e]).wait()

        def clear_unused(tile_idx, carry):
            clear(tile_idx).start()
            clear(tile_idx).wait()
            return carry

        lax.fori_loop(last_ref[N_EXPERTS], xs_hbm.shape[0] // (MOE_TILE * SLAB), clear_unused, 0)

    def issue(t, carry):
        src = h2_ref.at[pl.ds(pl.multiple_of(t * SLAB, SLAB), SLAB)]
        for j in range(2):
            slot = pl.multiple_of(pos_ref[0, 0, 2 * t + j] * SLAB, SLAB)
            pltpu.make_async_copy(src, xs_hbm.at[pl.ds(slot, SLAB)], sem).start(priority=j)
        return carry

    lax.fori_loop(0, tile, issue, 0, unroll=DMA_UNROLL)
    whole = pltpu.make_async_copy(h2_ref, xs_hbm.at[pl.ds(0, rows)], sem)
    whole.wait()
    whole.wait()


def _scatter(last_tile, pos3, h2s, n_slots):
    tile = GATHER_TILE
    rows = tile * SLAB
    grid_spec = pltpu.PrefetchScalarGridSpec(
        num_scalar_prefetch=1,
        grid=(h2s.shape[0] // rows,),
        in_specs=[pl.BlockSpec((1, 1, 2 * tile), lambda i, lt: (i, 0, 0), memory_space=pltpu.SMEM),
                  pl.BlockSpec((rows, LANES), lambda i, lt: (i, 0))],
        out_specs=pl.BlockSpec(memory_space=pl.ANY),
        scratch_shapes=[pltpu.VMEM((MOE_TILE * SLAB, LANES), F32),
                        pltpu.SemaphoreType.DMA, pltpu.SemaphoreType.DMA],
    )
    return pl.pallas_call(
        functools.partial(_scatter_body, tile=tile),
        grid_spec=grid_spec,
        out_shape=jax.ShapeDtypeStruct((n_slots * SLAB, LANES), F32),
        compiler_params=pltpu.CompilerParams(dimension_semantics=("arbitrary",)),
        name="moe_scatter",
    )(last_tile, pos3, h2s)


def _experts_body(te_ref, nv_ref, x_ref, wg_ref, wu_ref, wd_ref, y_ref, wgu_scr, wd_scr, *, tm):
    i = pl.program_id(0)
    e = te_ref[i]
    prev = te_ref[jnp.maximum(i - 1, 0)]

    @pl.when((i == 0) | (e != prev))
    def _():
        wgu_scr[:, 0:EXPERT_FF] = wg_ref[0].astype(BF16)
        wgu_scr[:, EXPERT_FF:] = wu_ref[0].astype(BF16)
        wd_scr[...] = wd_ref[0].astype(BF16)

    @pl.when(i < nv_ref[0])
    def _():
        half = tm // 2
        xr = [x_ref.at[pl.ds(r * half * SLAB, half * SLAB)] for r in range(2)]
        xs = [_slab_load(ref, half).astype(BF16) for ref in xr]
        gus = [jnp.dot(x, wgu_scr[...], preferred_element_type=F32) for x in xs]
        acts = [(gu[:, 0:EXPERT_FF] * _sigmoid(gu[:, 0:EXPERT_FF]) * gu[:, EXPERT_FF:]).astype(BF16)
                for gu in gus]
        ys = [jnp.dot(act, wd_scr[...], preferred_element_type=F32) for act in acts]
        for r, y in enumerate(ys):
            y_ref[r * half:(r + 1) * half, :] = y

    @pl.when(i >= nv_ref[0])
    def _():
        y_ref[...] = jnp.zeros(y_ref.shape, F32)


def _experts(tile_expert, n_valid, xs, w_gate, w_up, w_down):
    tm = MOE_TILE
    rows = tm * SLAB
    d = w_gate.shape[1]
    live = lambda i, te, nv: (jnp.minimum(i, nv[0] - 1), 0)
    grid_spec = pltpu.PrefetchScalarGridSpec(
        num_scalar_prefetch=2,
        grid=(xs.shape[0] // rows,),
        in_specs=[pl.BlockSpec((rows, LANES), live),
                  pl.BlockSpec((1, d, EXPERT_FF), lambda i, te, nv: (te[i], 0, 0)),
                  pl.BlockSpec((1, d, EXPERT_FF), lambda i, te, nv: (te[i], 0, 0)),
                  pl.BlockSpec((1, EXPERT_FF, d), lambda i, te, nv: (te[i], 0, 0))],
        out_specs=pl.BlockSpec((tm, d), lambda i, te, nv: (i, 0)),
        scratch_shapes=[pltpu.VMEM((d, 2 * EXPERT_FF), BF16),
                        pltpu.VMEM((EXPERT_FF, d), BF16)],
    )
    return pl.pallas_call(
        functools.partial(_experts_body, tm=tm),
        grid_spec=grid_spec,
        out_shape=jax.ShapeDtypeStruct((xs.shape[0] // SLAB, d), F32),
        compiler_params=pltpu.CompilerParams(dimension_semantics=("arbitrary",)),
        name="moe_experts",
    )(tile_expert, n_valid, xs, w_gate, w_up, w_down)


def _combine_body(pos_ref, nxt_ref, x1_ref, meta_ref, mod_ref, nw_ref, ys_hbm, o_ref, rows_scr, sems,
                  *, tile):
    i = pl.program_id(0)
    n = pl.num_programs(0)

    def fetch(idx_ref, buf):
        def issue(t, carry):
            for j in range(2):
                slot = idx_ref[0, 0, 2 * t + j]
                pltpu.make_async_copy(ys_hbm.at[pl.ds(slot, 1)], rows_scr.at[buf, j, pl.ds(t, 1)],
                                      sems.at[buf]).start(priority=j)
            return carry
        lax.fori_loop(0, tile, issue, 0, unroll=DMA_UNROLL)

    @pl.when(i == 0)
    def _():
        fetch(pos_ref, 0)

    for cur in range(2):
        @pl.when((i % 2 == cur) & (i + 1 < n))
        def _():
            fetch(nxt_ref, 1 - cur)

    for cur in range(2):
        @pl.when(i % 2 == cur)
        def _():
            for j in range(2):
                pltpu.make_async_copy(ys_hbm.at[pl.ds(0, tile)], rows_scr.at[cur, j],
                                      sems.at[cur]).wait()
            meta = meta_ref[...]
            w0 = meta[:, META_W0:META_W0 + 1]
            w1 = meta[:, META_W1:META_W1 + 1]
            y = w0 * rows_scr[cur, 0] + w1 * rows_scr[cur, 1]
            x2 = x1_ref[...] + mod_ref[0, 5:6, :] * y
            o_ref[...] = (x2 * lax.rsqrt(jnp.mean(x2 * x2, axis=-1, keepdims=True) + EPS)
                          * nw_ref[...])


def _combine(pos3, x1, meta, mod3, norm_w, ys, seq):
    t, d = x1.shape
    tile = GATHER_TILE
    per_b = seq // tile
    n = t // tile
    row = lambda i: (i, 0)
    return pl.pallas_call(
        functools.partial(_combine_body, tile=tile),
        grid=(n,),
        in_specs=[pl.BlockSpec((1, 1, 2 * tile), lambda i: (i, 0, 0), memory_space=pltpu.SMEM),
                  pl.BlockSpec((1, 1, 2 * tile), lambda i: (jnp.minimum(i + 1, n - 1), 0, 0),
                               memory_space=pltpu.SMEM),
                  pl.BlockSpec((tile, d), row),
                  pl.BlockSpec((tile, LANES), row),
                  pl.BlockSpec((1, 6, d), lambda i: (i // per_b, 0, 0)),
                  pl.BlockSpec((1, d), lambda i: (0, 0)),
                  pl.BlockSpec(memory_space=pl.ANY)],
        out_specs=pl.BlockSpec((tile, d), row),
        out_shape=jax.ShapeDtypeStruct((t, d), F32),
        scratch_shapes=[pltpu.VMEM((2, 2, tile, d), F32), pltpu.SemaphoreType.DMA((2,))],
        compiler_params=pltpu.CompilerParams(dimension_semantics=("arbitrary",)),
        name="moe_combine",
    )(pos3, pos3, x1, meta, mod3, norm_w, ys)


def _rope_constants():
    inv_freq = ROPE_THETA ** (-jnp.arange(ROT_HALF, dtype=F32) / ROT_HALF)
    lane = np.arange(LANES) % DA_QK_DIM
    hit = (lane[None, :] < ROT_DIM) & (lane[None, :] % ROT_HALF == np.arange(ROT_HALF)[:, None])
    spread = np.concatenate([hit, hit], axis=0).astype(np.float32)
    return inv_freq.reshape(ROT_HALF, 1), jnp.asarray(spread, BF16)


def kernel(x, c, positions, norm1_w, norm2_w, final_norm_w, ada_w, ada_b, w_in, w_out, da_lambda_q1, da_lambda_k1, da_lambda_q2, da_lambda_k2, da_subln_w, hg_lower_bound, hg_norm_w, moe_w_group, moe_b_group, moe_w_router, moe_b_router, moe_w_gate, moe_w_up, moe_w_down):
    bsz, seq, d = x.shape
    assert d == D_MODEL and norm1_w.shape[0] == 1, "single-layer model of width 1024 only"
    assert seq % ATTN_TILE == 0 and seq % HGRN_STEP == 0 and seq % OUT_TILE == 0
    t = bsz * seq
    x2 = x.reshape(t, d)

    mod3 = _adaln(c, ada_w[0], ada_b).reshape(bsz, 6, d)

    pos_rows = positions.astype(F32).reshape(t // ROW_TILE, 1, ROW_TILE)
    inv_freq, spread = _rope_constants()
    qt, k, vt, hq, lf, kf, gi, sg = _inproj(x2, mod3, norm1_w, w_in[0].astype(BF16), pos_rows,
                                          inv_freq, spread, hg_lower_bound, seq)

    as3 = lambda a: a.reshape(bsz, seq, STREAM_W)
    lam_p = jnp.concatenate([da_lambda_q1, da_lambda_k1, da_lambda_q2, da_lambda_k2], axis=0)
    da = _attn(qt, as3(k), vt, lam_p, da_subln_w.reshape(HEAD_W, 1))
    hg = _hgrn(as3(hq), as3(lf), as3(kf), as3(gi), as3(sg), hg_norm_w)

    pad = jnp.zeros((d, LANES - N_EXPERTS - N_GROUPS), F32)
    w_route = jnp.concatenate([moe_w_router[0], moe_w_group[0], pad], axis=1)
    b_route = jnp.concatenate([moe_b_router[0], moe_b_group[0], pad[0]]).reshape(1, LANES)
    w_route_hi = w_route.astype(BF16)
    w_route_lo = (w_route - w_route_hi.astype(F32)).astype(BF16)
    x1, h2, meta, cnt = _outproj(da.reshape(t, STREAM_W), hg.reshape(t, STREAM_W), x2, mod3,
                                 w_out[0].astype(BF16), norm2_w,
                                 jnp.concatenate([w_route_hi, w_route_lo], axis=1), b_route, seq)

    counts = cnt[0, :N_EXPERTS].astype(jnp.int32)
    tiles_e = (counts + MOE_TILE - 1) // MOE_TILE
    tile_end = jnp.cumsum(tiles_e)
    offs = (tile_end - tiles_e) * MOE_TILE
    ids = meta[:, META_E0:META_E1 + 1].astype(jnp.int32)
    ranks = meta[:, META_R0:META_R1 + 1].astype(jnp.int32)
    expert_iota = jnp.arange(N_EXPERTS, dtype=jnp.int32)
    pos = jnp.sum(jnp.where(ids[..., None] == expert_iota, offs, 0), axis=-1) + ranks
    n_tiles = (2 * t) // MOE_TILE + N_EXPERTS
    n_valid = tile_end[-1:]
    tile_ids = jnp.minimum(jnp.arange(n_tiles, dtype=jnp.int32), n_valid - 1)
    tile_expert = jnp.sum(tile_ids[:, None] >= tile_end[None, :], axis=1).astype(jnp.int32)
    pos3 = pos.reshape(t // GATHER_TILE, 1, 2 * GATHER_TILE)
    last_tile = jnp.concatenate([jnp.where(tiles_e > 0, tile_end - 1, -1), n_valid]).astype(jnp.int32)

    xs = _scatter(last_tile, pos3, h2, n_tiles * MOE_TILE)
    ys = _experts(tile_expert, n_valid.astype(jnp.int32), xs, moe_w_gate[0], moe_w_up[0], moe_w_down[0])
    out = _combine(pos3, x1, meta, mod3, final_norm_w.reshape(1, d), ys, seq)
    return out.reshape(bsz, seq, d)
```

```python
import functools
import math

import numpy as np
import jax
import jax.numpy as jnp
from jax import lax
from jax.experimental import pallas as pl
from jax.experimental.pallas import tpu as pltpu

F32 = jnp.float32
BF16 = jnp.bfloat16
HIGHEST = lax.Precision.HIGHEST

LANES = 128
SUBLANES = 8
D_MODEL = 1024
HEADS = 4
HEAD_W = 128
STREAM_W = HEADS * HEAD_W
N_STREAMS = 7
DA_QK_DIM = 64
ROPE_THETA = 500000.0
ROT_DIM = DA_QK_DIM // 4
ROT_HALF = ROT_DIM // 2
N_GROUPS = 4
EXPERTS_PER_GROUP = 8
N_EXPERTS = N_GROUPS * EXPERTS_PER_GROUP
EXPERT_FF = 256
EPS = 1e-6
LAM_INIT = 0.8 - 0.6 * math.exp(-0.3 * 0)
NEG_BIG = -1e30

ROW_TILE = 512
OUT_TILE = 512
ATTN_TILE = 1024
ATTN_KEY_TILE = 512
HGRN_BLOCK = 128
HGRN_STEP = 256
MOE_TILE = 512
GATHER_TILE = 256

NT_DIMS = (((1,), (1,)), ((), ()))
TN_DIMS = (((0,), (0,)), ((), ()))


def _sigmoid(x):
    return 1.0 / (1.0 + jnp.exp(-x))


SLAB = D_MODEL // LANES


def _slab_store(ref, val):
    rows = val.shape[0]
    for c in range(SLAB):
        ref[pl.ds(c, rows, stride=SLAB), :] = val[:, c * LANES:(c + 1) * LANES]


def _slab_load(ref, rows):
    return jnp.concatenate([ref[pl.ds(c, rows, stride=SLAB), :] for c in range(SLAB)], axis=1)


def _adaln_body(c_ref, w_ref, b_ref, o_ref):
    c = c_ref[...]
    ca = c * _sigmoid(c)
    o_ref[...] = jnp.dot(ca, w_ref[...], preferred_element_type=F32, precision=HIGHEST) + b_ref[...]


def _adaln(c, ada_w, ada_b):
    bsz, d = c.shape
    n = ada_w.shape[1]
    tn = 1024
    return pl.pallas_call(
        _adaln_body,
        grid=(n // tn,),
        in_specs=[pl.BlockSpec((bsz, d), lambda j: (0, 0)),
                  pl.BlockSpec((d, tn), lambda j: (0, j)),
                  pl.BlockSpec((1, tn), lambda j: (0, j))],
        out_specs=pl.BlockSpec((bsz, tn), lambda j: (0, j)),
        out_shape=jax.ShapeDtypeStruct((bsz, n), F32),
        name="adaln",
    )(c, ada_w, ada_b)


def _inproj_body(x_ref, mod_ref, nw_ref, w32_ref, pos_ref, invf_ref, spread_ref, lbr_ref,
                 q_ref, k_ref, v_ref, hq_ref, lf_ref, kf_ref, gi_ref, sg_ref, w_ref):
    @pl.when(pl.program_id(0) == 0)
    def _():
        for j in range(N_STREAMS):
            cols = slice(j * STREAM_W, (j + 1) * STREAM_W)
            w_ref[:, cols] = w32_ref[:, cols].astype(BF16)

    x = x_ref[...]
    ms = jnp.mean(x * x, axis=-1, keepdims=True)
    y = x * lax.rsqrt(ms + EPS) * nw_ref[...]
    shift = mod_ref[0, 0:1, :]
    scale = mod_ref[0, 1:2, :]
    h = (y * (1.0 + scale) + shift).astype(BF16)

    ang_t = invf_ref[...] * pos_ref[0]

    def spread(table_t):
        hi = table_t.astype(BF16)
        lo = (table_t - hi.astype(F32)).astype(BF16)
        return lax.dot_general(jnp.concatenate([hi, lo], axis=0), spread_ref[...], TN_DIMS,
                               preferred_element_type=F32)

    lane = lax.broadcasted_iota(jnp.int32, (1, LANES), 1) % DA_QK_DIM
    cosv = spread(jnp.cos(ang_t)) + jnp.where(lane < ROT_DIM, 0.0, 1.0)
    sinv = spread(jnp.sin(ang_t))
    sin_lo = jnp.where(lane < ROT_HALF, -sinv, 0.0)
    sin_hi = jnp.where((lane >= ROT_HALF) & (lane < ROT_DIM), sinv, 0.0)

    streams = [jnp.dot(h, w_ref[:, j * STREAM_W:(j + 1) * STREAM_W], preferred_element_type=F32)
               for j in range(N_STREAMS)]

    def proj(j):
        return streams[j]

    def rope(t):
        outs = []
        for hb in range(HEADS):
            tc = t[:, hb * HEAD_W:(hb + 1) * HEAD_W]
            outs.append(tc * cosv
                        + pltpu.roll(tc, LANES - ROT_HALF, 1) * sin_lo
                        + pltpu.roll(tc, ROT_HALF, 1) * sin_hi)
        return jnp.concatenate(outs, axis=1)

    q_ref[0] = (rope(proj(0)) * (DA_QK_DIM ** -0.5 * math.log2(math.e))).T.astype(BF16)
    k_ref[...] = rope(proj(1)).astype(BF16)
    v_ref[0] = proj(2).astype(BF16).T

    gq = proj(3)
    hq_ref[...] = gq * _sigmoid(gq)

    a = lbr_ref[...]
    amax = jnp.max(a, axis=0, keepdims=True)
    ea = jnp.exp(a - amax)
    lb = ea[0:1, :] / jnp.sum(ea, axis=0, keepdims=True)
    gf = proj(4)
    f = lb + (1.0 - lb) * _sigmoid(gf)
    lf_ref[...] = jnp.log(f) * math.log2(math.e)
    kf_ref[...] = 1.0 - f

    gi_ref[...] = proj(5).astype(BF16)
    gg = proj(6)
    sg_ref[...] = (gg * _sigmoid(gg)).astype(BF16)


def _inproj(x2, mod3, norm_w, w_f32, pos_rows, invf, spread, lb_raw, seq):
    t, d = x2.shape
    tm = ROW_TILE
    per_b = seq // tm
    row = lambda i: (i, 0)
    full = lambda i: (0, 0)
    out_bf = jax.ShapeDtypeStruct((t, STREAM_W), BF16)
    out_f = jax.ShapeDtypeStruct((t, STREAM_W), F32)
    out_t = jax.ShapeDtypeStruct((t // seq, STREAM_W, seq), BF16)
    stream = pl.BlockSpec((tm, STREAM_W), row)
    stream_t = pl.BlockSpec((1, STREAM_W, tm), lambda i: (i // per_b, 0, i % per_b))
    return pl.pallas_call(
        _inproj_body,
        grid=(t // tm,),
        in_specs=[pl.BlockSpec((tm, d), row),
                  pl.BlockSpec((1, 6, d), lambda i: (i // per_b, 0, 0)),
                  pl.BlockSpec((1, d), full),
                  pl.BlockSpec((d, N_STREAMS * STREAM_W), full, pipeline_mode=pl.Buffered(1)),
                  pl.BlockSpec((1, 1, tm), lambda i: (i, 0, 0)),
                  pl.BlockSpec((ROT_HALF, 1), full),
                  pl.BlockSpec((2 * ROT_HALF, LANES), full),
                  pl.BlockSpec(lb_raw.shape, full)],
        out_specs=[stream_t, stream, stream_t] + [stream] * 5,
        out_shape=[out_t, out_bf, out_t, out_f, out_f, out_f, out_bf, out_bf],
        scratch_shapes=[pltpu.VMEM((d, N_STREAMS * STREAM_W), BF16)],
        compiler_params=pltpu.CompilerParams(dimension_semantics=("arbitrary",),
                                             vmem_limit_bytes=56 * 1024 * 1024),
        name="inproj",
    )(x2, mod3, norm_w, w_f32, pos_rows, invf, spread, lb_raw)


ONES_ROWS = 16


def _attn_body(qt_ref, k_ref, vt_ref, lam_ref, sw_ref, o_ref, s_scr, m_scr, a_scr, *, tile, ktile):
    qi = pl.program_id(2)
    qt = qt_ref[0]
    feat = lax.broadcasted_iota(jnp.int32, (HEAD_W, 1), 0)
    zero = jnp.zeros_like(qt)
    qmaps = (jnp.where(feat < DA_QK_DIM, qt, zero), jnp.where(feat >= DA_QK_DIM, qt, zero))

    m_scr[...] = jnp.full(m_scr.shape, NEG_BIG, F32)
    a_scr[...] = jnp.zeros(a_scr.shape, F32)

    def score_block(start, nkeys, q_lo, masked):
        kb = k_ref[0, pl.ds(start, nkeys), :]
        scores = [jnp.dot(kb, qmaps[mp][:, q_lo:], preferred_element_type=F32) for mp in range(2)]
        for mp, s in enumerate(scores):
            if masked:
                key = lax.broadcasted_iota(jnp.int32, s.shape, 0)
                qry = lax.broadcasted_iota(jnp.int32, s.shape, 1)
                s = jnp.where(key <= qry, s, NEG_BIG)
            grouped = s.reshape(nkeys // SUBLANES, SUBLANES, tile - q_lo)
            m_scr[mp, :, q_lo:] = jnp.maximum(m_scr[mp, :, q_lo:], jnp.max(grouped, axis=0))
            s_scr[mp, pl.ds(start, nkeys), q_lo:] = s

    def value_block(start, nkeys, q_lo, col_max):
        vb = jnp.concatenate([vt_ref[0, :, pl.ds(start, nkeys)], jnp.ones((ONES_ROWS, nkeys), BF16)],
                             axis=0)
        probs = [jnp.exp2(s_scr[mp, pl.ds(start, nkeys), q_lo:] - col_max[mp][:, q_lo:]).astype(BF16)
                 for mp in range(2)]
        for mp, p in enumerate(probs):
            a_scr[mp, :, q_lo:] += jnp.dot(vb, p, preferred_element_type=F32)

    diag = pl.multiple_of(qi * tile, tile)
    pieces = [(diag + j * ktile, ktile, j * ktile) for j in range(tile // ktile)]

    def score_body(ki, carry):
        score_block(pl.multiple_of(ki * tile, tile), tile, 0, False)
        return carry

    lax.fori_loop(0, qi, score_body, 0)
    for start, nkeys, q_lo in pieces:
        score_block(start, nkeys, q_lo, True)
    col_max = [jnp.max(m_scr[mp], axis=0, keepdims=True) for mp in range(2)]

    def value_body(ki, carry):
        value_block(pl.multiple_of(ki * tile, tile), tile, 0, col_max)
        return carry

    lax.fori_loop(0, qi, value_body, 0)
    for start, nkeys, q_lo in pieces:
        value_block(start, nkeys, q_lo, col_max)

    lp = lam_ref[...]
    lam = (jnp.exp(jnp.sum(lp[0:1] * lp[1:2], axis=-1, keepdims=True))
           - jnp.exp(jnp.sum(lp[2:3] * lp[3:4], axis=-1, keepdims=True)) + LAM_INIT)
    o = (a_scr[0, 0:HEAD_W, :] / a_scr[0, HEAD_W:HEAD_W + 1, :]
         - lam * (a_scr[1, 0:HEAD_W, :] / a_scr[1, HEAD_W:HEAD_W + 1, :]))
    o = o * lax.rsqrt(jnp.mean(o * o, axis=0, keepdims=True) + EPS) * sw_ref[...]
    o_ref[0] = (o * (1.0 - LAM_INIT)).T.astype(BF16)


def _attn(qt3, k3, vt3, lam_p, subln_col):
    bsz, seq, _ = k3.shape
    tile = ATTN_TILE
    return pl.pallas_call(
        functools.partial(_attn_body, tile=tile, ktile=ATTN_KEY_TILE),
        grid=(bsz, HEADS, seq // tile),
        in_specs=[pl.BlockSpec((1, HEAD_W, tile), lambda b, h, i: (b, h, i)),
                  pl.BlockSpec((1, seq, HEAD_W), lambda b, h, i: (b, 0, h)),
                  pl.BlockSpec((1, HEAD_W, seq), lambda b, h, i: (b, h, 0)),
                  pl.BlockSpec(lam_p.shape, lambda b, h, i: (0, 0)),
                  pl.BlockSpec((HEAD_W, 1), lambda b, h, i: (0, 0))],
        out_specs=pl.BlockSpec((1, tile, HEAD_W), lambda b, h, i: (b, i, h)),
        out_shape=jax.ShapeDtypeStruct((bsz, seq, STREAM_W), BF16),
        scratch_shapes=[pltpu.VMEM((2, seq, tile), F32),
                        pltpu.VMEM((2, SUBLANES, tile), F32),
                        pltpu.VMEM((2, HEAD_W + ONES_ROWS, tile), F32)],
        compiler_params=pltpu.CompilerParams(vmem_limit_bytes=56 * 1024 * 1024),
        name="diff_attn",
    )(qt3, k3, vt3, lam_p, subln_col)


def _hgrn_levels(block):
    return [block >> (i + 1) for i in range(block.bit_length() - 1)]


def _hgrn_constants(block):
    t = np.arange(block)[:, None]
    s = np.arange(block)[None, :]
    tril = (s <= t).astype(np.float32)
    lv = np.full((block, block), -1, np.int32)
    halves = _hgrn_levels(block)
    for li, m in enumerate(halves):
        same = (t // (2 * m)) == (s // (2 * m))
        lv[same & ((t & m) != 0) & ((s & m) == 0)] = li
    lv[np.arange(block), np.arange(block)] = len(halves)
    return jnp.asarray(tril, BF16), jnp.asarray(lv)


def _level_operand(b_ref, h, b, q, k, m, block):
    def ref_rows(r, n):
        return jnp.broadcast_to(b_ref[h, pl.ds(r, 1), :], (n, HEAD_W))

    if m >= SUBLANES:
        pieces = []
        for s0 in range(0, block, 2 * m):
            ref = ref_rows(s0 + m - 1, m)
            lo = slice(s0, s0 + m)
            up = slice(s0 + m, s0 + 2 * m)
            pieces.append(k[lo] * jnp.exp2(ref - b[lo]))
            pieces.append(q[up] * jnp.exp2(b[up] - ref))
        return jnp.concatenate(pieces, axis=0)

    sub = lax.broadcasted_iota(jnp.int32, (SUBLANES, HEAD_W), 0)
    refs = []
    for s0 in range(0, block, SUBLANES):
        piece = ref_rows(s0 + m - 1, SUBLANES)
        for j in range(1, SUBLANES // (2 * m)):
            piece = jnp.where(sub >= 2 * m * j, ref_rows(s0 + 2 * m * j + m - 1, SUBLANES), piece)
        refs.append(piece)
    d = b - jnp.concatenate(refs, axis=0)
    row = lax.broadcasted_iota(jnp.int32, (block, 1), 0)
    return jnp.where((row & m) != 0, q, k) * jnp.exp2(jnp.minimum(d, -d))


def _hgrn_body(hq_ref, lf_ref, kf_ref, gi_ref, sg_ref, nw_ref, tril_ref, lv_ref, o_ref,
               st_scr, b_scr, *, block, step):
    @pl.when(pl.program_id(1) == 0)
    def _():
        st_scr[...] = jnp.zeros(st_scr.shape, F32)

    tril = tril_ref[...]
    lv = lv_ref[...]
    halves = _hgrn_levels(block)

    units = [(u, r0, h) for u, (r0, h) in enumerate(
        (r0, h) for r0 in range(0, step, block) for h in range(HEADS))]

    def cols(h):
        return slice(h * HEAD_W, (h + 1) * HEAD_W)

    q, k, v, b, scores = {}, {}, {}, {}, {}
    for u, r0, h in units:
        rows = slice(r0, r0 + block)
        q[u] = hq_ref[0, rows, cols(h)]
        k[u] = kf_ref[0, rows, cols(h)]
        v[u] = gi_ref[0, rows, cols(h)]
        lf = lf_ref[0, rows, cols(h)]
        hi = lf.astype(BF16)
        r1 = lf - hi.astype(F32)
        mid = r1.astype(BF16)
        lo = (r1 - mid.astype(F32)).astype(BF16)
        b[u] = (jnp.dot(tril, hi, preferred_element_type=F32)
                + jnp.dot(tril, mid, preferred_element_type=F32)
                + jnp.dot(tril, lo, preferred_element_type=F32))
        b_scr[u] = b[u]
        scores[u] = jnp.where(lv == len(halves),
                              lax.dot_general(q[u].astype(BF16), k[u].astype(BF16), NT_DIMS,
                                              preferred_element_type=F32), 0.0)

    for li, m in enumerate(halves):
        for u, r0, h in units:
            xl = _level_operand(b_scr, u, b[u], q[u], k[u], m, block).astype(BF16)
            p = lax.dot_general(xl, xl, NT_DIMS, preferred_element_type=F32)
            scores[u] = jnp.where(lv == li, p, scores[u])

    for u, r0, h in units:
        rows = slice(r0, r0 + block)
        o_intra = jnp.dot(scores[u].astype(BF16), v[u], preferred_element_type=F32)
        st = st_scr[h]
        o_inter = lax.dot_general((q[u] * jnp.exp2(b[u])).astype(BF16), st.astype(BF16), NT_DIMS,
                                  preferred_element_type=F32)
        b_last = b[u][block - 1:block, :]
        kdec = (k[u] * jnp.exp2(b_last - b[u])).astype(BF16)
        st_scr[h] = st * jnp.exp2(b_last) + lax.dot_general(v[u], kdec, TN_DIMS,
                                                             preferred_element_type=F32)
        o = o_inter + o_intra
        o = o * lax.rsqrt(jnp.mean(o * o, axis=-1, keepdims=True) + EPS) * nw_ref[...]
        o_ref[0, rows, cols(h)] = (o * sg_ref[0, rows, cols(h)].astype(F32)).astype(BF16)


def _hgrn(hq3, lf3, kf3, gi3, sg3, norm_w):
    bsz, seq, _ = hq3.shape
    block = HGRN_BLOCK
    step = HGRN_STEP
    tril, lv = _hgrn_constants(block)
    blk = pl.BlockSpec((1, step, STREAM_W), lambda b, g: (b, g, 0))
    const = lambda b, g: (0, 0)
    return pl.pallas_call(
        functools.partial(_hgrn_body, block=block, step=step),
        grid=(bsz, seq // step),
        in_specs=[blk, blk, blk, blk, blk,
                  pl.BlockSpec((1, HEAD_W), const),
                  pl.BlockSpec((block, block), const),
                  pl.BlockSpec((block, block), const)],
        out_specs=blk,
        out_shape=jax.ShapeDtypeStruct((bsz, seq, STREAM_W), BF16),
        scratch_shapes=[pltpu.VMEM((HEADS, HEAD_W, HEAD_W), F32),
                        pltpu.VMEM((HEADS * step // block, block, HEAD_W), F32)],
        name="hgrn2",
    )(hq3, lf3, kf3, gi3, sg3, norm_w, tril, lv)


META_E0, META_E1, META_R0, META_R1, META_W0, META_W1 = range(6)
GROUP_LANE0 = N_EXPERTS


OUT_PARTS = 4


def _route(logits, lane):
    far = jnp.int32(LANES)

    def first_max(vals):
        mx = jnp.max(vals, axis=-1, keepdims=True)
        return mx, jnp.min(jnp.where(vals == mx, lane, far), axis=-1, keepdims=True)

    is_g = (lane >= GROUP_LANE0) & (lane < GROUP_LANE0 + N_GROUPS)
    gmax, glane = first_max(jnp.where(is_g, logits, NEG_BIG))
    g_w = 1.0 / jnp.sum(jnp.where(is_g, jnp.exp(logits - gmax), 0.0), axis=-1, keepdims=True)
    gidx = glane - GROUP_LANE0
    in_grp = (lane < N_EXPERTS) & ((lane // EXPERTS_PER_GROUP) == gidx)
    el = jnp.where(in_grp, logits, NEG_BIG)
    m1, i1 = first_max(el)
    m2, i2 = first_max(jnp.where(lane == i1, NEG_BIG, el))
    r = jnp.exp(m2 - m1)
    return i1, i2, g_w / (1.0 + r), g_w * r / (1.0 + r)


def _outproj_body(da_ref, hg_ref, x_ref, mod_ref, wo_ref, nw_ref, wr_ref, br_ref, stril_ref,
                  x1_ref, h2_ref, meta_ref, route_ref, cnt_ref, carry_scr):
    @pl.when(pl.program_id(0) == 0)
    def _():
        carry_scr[...] = jnp.zeros(carry_scr.shape, F32)

    rows = stril_ref.shape[0]
    parts = [slice(p * rows, (p + 1) * rows) for p in range(OUT_PARTS)]
    gate1 = mod_ref[0, 2:3, :]
    shift2 = mod_ref[0, 3:4, :]
    scale2 = mod_ref[0, 4:5, :]

    attn = [jnp.dot(da_ref[r, :], wo_ref[0:STREAM_W, :], preferred_element_type=F32)
            + jnp.dot(hg_ref[r, :], wo_ref[STREAM_W:, :], preferred_element_type=F32) for r in parts]

    h2 = []
    for r, a in zip(parts, attn):
        x1 = x_ref[r, :] + gate1 * a
        x1_ref[r, :] = x1
        h2.append(x1 * lax.rsqrt(jnp.mean(x1 * x1, axis=-1, keepdims=True) + EPS) * nw_ref[...]
                  * (1.0 + scale2) + shift2)

    logits = []
    for p, h in enumerate(h2):
        _slab_store(h2_ref.at[pl.ds(p * rows * SLAB, rows * SLAB)], h)
        h_hi = h.astype(BF16)
        h_lo = (h - h_hi.astype(F32)).astype(BF16)
        terms = jnp.dot(jnp.concatenate([h_hi, h_lo], axis=0), wr_ref[...],
                        preferred_element_type=F32)
        logits.append((terms[:rows, :LANES] + terms[:rows, LANES:])
                      + (terms[rows:, :LANES] + terms[rows:, LANES:]) + br_ref[...])

    lane = lax.broadcasted_iota(jnp.int32, (rows, LANES), 1)
    routed = [_route(lg, lane) for lg in logits]

    carry = carry_scr[...]
    for r, (i1, i2, w0, w1) in zip(parts, routed):
        hot0 = lane == i1
        hot1 = lane == i2
        multi = jnp.where(hot0 | hot1, 1.0, 0.0)
        before = jnp.dot(stril_ref[...], multi.astype(BF16), preferred_element_type=F32) + carry
        rank0 = jnp.sum(jnp.where(hot0, before, 0.0), axis=-1, keepdims=True)
        rank1 = jnp.sum(jnp.where(hot1, before, 0.0), axis=-1, keepdims=True)
        carry = carry + jnp.sum(multi, axis=0, keepdims=True)
        meta = jnp.zeros((rows, LANES), F32)
        for idx, val in ((META_E0, i1.astype(F32)), (META_E1, i2.astype(F32)),
                         (META_R0, rank0), (META_R1, rank1), (META_W0, w0), (META_W1, w1)):
            meta = jnp.where(lane == idx, val, meta)
        meta_ref[r, :] = meta
        route_ref[:, r] = meta.T[0:SUBLANES, :]
    carry_scr[...] = carry
    cnt_ref[...] = carry


def _outproj(da2, hg2, x2, mod3, wo_bf, norm_w, w_route, b_route, seq):
    t, d = x2.shape
    tm = OUT_TILE
    per_b = seq // tm
    row = lambda i: (i, 0)
    full = lambda i: (0, 0)
    part = tm // OUT_PARTS
    stril = jnp.asarray(np.tril(np.ones((part, part), np.float32), -1), BF16)
    return pl.pallas_call(
        _outproj_body,
        grid=(t // tm,),
        in_specs=[pl.BlockSpec((tm, STREAM_W), row),
                  pl.BlockSpec((tm, STREAM_W), row),
                  pl.BlockSpec((tm, d), row),
                  pl.BlockSpec((1, 6, d), lambda i: (i // per_b, 0, 0)),
                  pl.BlockSpec((2 * STREAM_W, d), full),
                  pl.BlockSpec((1, d), full),
                  pl.BlockSpec((d, 2 * LANES), full),
                  pl.BlockSpec((1, LANES), full),
                  pl.BlockSpec((part, part), full)],
        out_specs=[pl.BlockSpec((tm, d), row),
                   pl.BlockSpec((tm * SLAB, LANES), row),
                   pl.BlockSpec((tm, LANES), row),
                   pl.BlockSpec((SUBLANES, tm), lambda i: (0, i)),
                   pl.BlockSpec((1, LANES), full)],
        out_shape=[jax.ShapeDtypeStruct((t, d), F32),
                   jax.ShapeDtypeStruct((t * SLAB, LANES), F32),
                   jax.ShapeDtypeStruct((t, LANES), F32),
                   jax.ShapeDtypeStruct((SUBLANES, t), F32),
                   jax.ShapeDtypeStruct((1, LANES), F32)],
        scratch_shapes=[pltpu.VMEM((1, LANES), F32)],
        compiler_params=pltpu.CompilerParams(dimension_semantics=("arbitrary",)),
        name="outproj_route",
    )(da2, hg2, x2, mod3, wo_bf, norm_w, w_route, b_route, stril)


DMA_UNROLL = 8


def _scatter_body(last_ref, pos_ref, h2_ref, xs_hbm, zero_scr, sem, zsem, *, tile):
    rows = tile * SLAB

    @pl.when(pl.program_id(0) == 0)
    def _():
        zero_scr[...] = jnp.zeros(zero_scr.shape, F32)

        def clear(tile_idx):
            start = pl.multiple_of(tile_idx * (MOE_TILE * SLAB), SLAB)
            return pltpu.make_async_copy(zero_scr, xs_hbm.at[pl.ds(start, MOE_TILE * SLAB)], zsem)

        for e in range(N_EXPERTS):
            @pl.when(last_ref[e] >= 0)
            def _():
                clear(last_ref[e]).start()
        for e in range(N_EXPERTS):
            @pl.when(last_ref[e] >= 0)
            def _():
                clear(last_ref[e]).wait()

        def clear_unused(tile_idx, carry):
            clear(tile_idx).start()
            clear(tile_idx).wait()
            return carry

        lax.fori_loop(last_ref[N_EXPERTS], xs_hbm.shape[0] // (MOE_TILE * SLAB), clear_unused, 0)

    def issue(t, carry):
        src = h2_ref.at[pl.ds(pl.multiple_of(t * SLAB, SLAB), SLAB)]
        for j in range(2):
            slot = pl.multiple_of(pos_ref[0, j, t] * SLAB, SLAB)
            pltpu.make_async_copy(src, xs_hbm.at[pl.ds(slot, SLAB)], sem).start(priority=j)
        return carry

    lax.fori_loop(0, tile, issue, 0, unroll=DMA_UNROLL)
    whole = pltpu.make_async_copy(h2_ref, xs_hbm.at[pl.ds(0, rows)], sem)
    whole.wait()
    whole.wait()


def _scatter(last_tile, pos3, h2s, n_slots):
    tile = GATHER_TILE
    rows = tile * SLAB
    grid_spec = pltpu.PrefetchScalarGridSpec(
        num_scalar_prefetch=1,
        grid=(h2s.shape[0] // rows,),
        in_specs=[pl.BlockSpec((1, 2, tile), lambda i, lt: (i, 0, 0), memory_space=pltpu.SMEM),
                  pl.BlockSpec((rows, LANES), lambda i, lt: (i, 0))],
        out_specs=pl.BlockSpec(memory_space=pl.ANY),
        scratch_shapes=[pltpu.VMEM((MOE_TILE * SLAB, LANES), F32),
                        pltpu.SemaphoreType.DMA, pltpu.SemaphoreType.DMA],
    )
    return pl.pallas_call(
        functools.partial(_scatter_body, tile=tile),
        grid_spec=grid_spec,
        out_shape=jax.ShapeDtypeStruct((n_slots * SLAB, LANES), F32),
        compiler_params=pltpu.CompilerParams(dimension_semantics=("arbitrary",)),
        name="moe_scatter",
    )(last_tile, pos3, h2s)


def _experts_body(te_ref, nv_ref, x_ref, wg_ref, wu_ref, wd_ref, y_ref, wgu_scr, wd_scr, *, tm):
    i = pl.program_id(0)
    e = te_ref[i]
    prev = te_ref[jnp.maximum(i - 1, 0)]

    @pl.when((i == 0) | (e != prev))
    def _():
        wgu_scr[:, 0:EXPERT_FF] = wg_ref[0].astype(BF16)
        wgu_scr[:, EXPERT_FF:] = wu_ref[0].astype(BF16)
        wd_scr[...] = wd_ref[0].astype(BF16)

    @pl.when(i < nv_ref[0])
    def _():
        half = tm // 2
        xr = [x_ref.at[pl.ds(r * half * SLAB, half * SLAB)] for r in range(2)]
        yr = [y_ref.at[pl.ds(r * half * SLAB, half * SLAB)] for r in range(2)]
        xs = [_slab_load(ref, half).astype(BF16) for ref in xr]
        gus = [jnp.dot(x, wgu_scr[...], preferred_element_type=F32) for x in xs]
        acts = [(gu[:, 0:EXPERT_FF] * _sigmoid(gu[:, 0:EXPERT_FF]) * gu[:, EXPERT_FF:]).astype(BF16)
                for gu in gus]
        ys = [jnp.dot(act, wd_scr[...], preferred_element_type=F32) for act in acts]
        for ref, y in zip(yr, ys):
            _slab_store(ref, y)

    @pl.when(i >= nv_ref[0])
    def _():
        y_ref[...] = jnp.zeros(y_ref.shape, F32)


def _experts(tile_expert, n_valid, xs, w_gate, w_up, w_down):
    tm = MOE_TILE
    rows = tm * SLAB
    d = w_gate.shape[1]
    live = lambda i, te, nv: (jnp.minimum(i, nv[0] - 1), 0)
    grid_spec = pltpu.PrefetchScalarGridSpec(
        num_scalar_prefetch=2,
        grid=(xs.shape[0] // rows,),
        in_specs=[pl.BlockSpec((rows, LANES), live),
                  pl.BlockSpec((1, d, EXPERT_FF), lambda i, te, nv: (te[i], 0, 0)),
                  pl.BlockSpec((1, d, EXPERT_FF), lambda i, te, nv: (te[i], 0, 0)),
                  pl.BlockSpec((1, EXPERT_FF, d), lambda i, te, nv: (te[i], 0, 0))],
        out_specs=pl.BlockSpec((rows, LANES), lambda i, te, nv: (i, 0)),
        scratch_shapes=[pltpu.VMEM((d, 2 * EXPERT_FF), BF16),
                        pltpu.VMEM((EXPERT_FF, d), BF16)],
    )
    return pl.pallas_call(
        functools.partial(_experts_body, tm=tm),
        grid_spec=grid_spec,
        out_shape=jax.ShapeDtypeStruct(xs.shape, F32),
        compiler_params=pltpu.CompilerParams(dimension_semantics=("arbitrary",)),
        name="moe_experts",
    )(tile_expert, n_valid, xs, w_gate, w_up, w_down)


def _combine_body(pos_ref, nxt_ref, x1_ref, meta_ref, mod_ref, nw_ref, ys_hbm, o_ref, rows_scr, sems,
                  *, tile):
    i = pl.program_id(0)
    n = pl.num_programs(0)
    rows = tile * SLAB

    def fetch(idx_ref, buf):
        def issue(t, carry):
            dst = pl.ds(pl.multiple_of(t * SLAB, SLAB), SLAB)
            for j in range(2):
                slot = pl.multiple_of(idx_ref[0, j, t] * SLAB, SLAB)
                pltpu.make_async_copy(ys_hbm.at[pl.ds(slot, SLAB)], rows_scr.at[buf, j, dst],
                                      sems.at[buf]).start(priority=j)
            return carry
        lax.fori_loop(0, tile, issue, 0, unroll=DMA_UNROLL)

    @pl.when(i == 0)
    def _():
        fetch(pos_ref, 0)

    for cur in range(2):
        @pl.when((i % 2 == cur) & (i + 1 < n))
        def _():
            fetch(nxt_ref, 1 - cur)

    for cur in range(2):
        @pl.when(i % 2 == cur)
        def _():
            for j in range(2):
                pltpu.make_async_copy(ys_hbm.at[pl.ds(0, rows)], rows_scr.at[cur, j],
                                      sems.at[cur]).wait()
            meta = meta_ref[...]
            w0 = meta[:, META_W0:META_W0 + 1]
            w1 = meta[:, META_W1:META_W1 + 1]
            y = (w0 * _slab_load(rows_scr.at[cur, 0], tile)
                 + w1 * _slab_load(rows_scr.at[cur, 1], tile))
            x2 = x1_ref[...] + mod_ref[0, 5:6, :] * y
            o_ref[...] = (x2 * lax.rsqrt(jnp.mean(x2 * x2, axis=-1, keepdims=True) + EPS)
                          * nw_ref[...])


def _combine(pos3, x1, meta, mod3, norm_w, ys, seq):
    t, d = x1.shape
    tile = GATHER_TILE
    per_b = seq // tile
    n = t // tile
    row = lambda i: (i, 0)
    return pl.pallas_call(
        functools.partial(_combine_body, tile=tile),
        grid=(n,),
        in_specs=[pl.BlockSpec((1, 2, tile), lambda i: (i, 0, 0), memory_space=pltpu.SMEM),
                  pl.BlockSpec((1, 2, tile), lambda i: (jnp.minimum(i + 1, n - 1), 0, 0),
                               memory_space=pltpu.SMEM),
                  pl.BlockSpec((tile, d), row),
                  pl.BlockSpec((tile, LANES), row),
                  pl.BlockSpec((1, 6, d), lambda i: (i // per_b, 0, 0)),
                  pl.BlockSpec((1, d), lambda i: (0, 0)),
                  pl.BlockSpec(memory_space=pl.ANY)],
        out_specs=pl.BlockSpec((tile, d), row),
        out_shape=jax.ShapeDtypeStruct((t, d), F32),
        scratch_shapes=[pltpu.VMEM((2, 2, tile * SLAB, LANES), F32), pltpu.SemaphoreType.DMA((2,))],
        compiler_params=pltpu.CompilerParams(dimension_semantics=("arbitrary",)),
        name="moe_combine",
    )(pos3, pos3, x1, meta, mod3, norm_w, ys)


def _rope_constants():
    inv_freq = ROPE_THETA ** (-jnp.arange(ROT_HALF, dtype=F32) / ROT_HALF)
    lane = np.arange(LANES) % DA_QK_DIM
    hit = (lane[None, :] < ROT_DIM) & (lane[None, :] % ROT_HALF == np.arange(ROT_HALF)[:, None])
    spread = np.concatenate([hit, hit], axis=0).astype(np.float32)
    return inv_freq.reshape(ROT_HALF, 1), jnp.asarray(spread, BF16)


def kernel(x, c, positions, norm1_w, norm2_w, final_norm_w, ada_w, ada_b, w_in, w_out, da_lambda_q1, da_lambda_k1, da_lambda_q2, da_lambda_k2, da_subln_w, hg_lower_bound, hg_norm_w, moe_w_group, moe_b_group, moe_w_router, moe_b_router, moe_w_gate, moe_w_up, moe_w_down):
    bsz, seq, d = x.shape
    assert d == D_MODEL and norm1_w.shape[0] == 1, "single-layer model of width 1024 only"
    assert seq % ATTN_TILE == 0 and seq % HGRN_STEP == 0 and seq % OUT_TILE == 0
    t = bsz * seq
    x2 = x.reshape(t, d)

    mod3 = _adaln(c, ada_w[0], ada_b).reshape(bsz, 6, d)

    pos_rows = positions.astype(F32).reshape(t // ROW_TILE, 1, ROW_TILE)
    inv_freq, spread = _rope_constants()
    qt, k, vt, hq, lf, kf, gi, sg = _inproj(x2, mod3, norm1_w, w_in[0], pos_rows,
                                          inv_freq, spread, hg_lower_bound, seq)

    as3 = lambda a: a.reshape(bsz, seq, STREAM_W)
    lam_p = jnp.concatenate([da_lambda_q1, da_lambda_k1, da_lambda_q2, da_lambda_k2], axis=0)
    da = _attn(qt, as3(k), vt, lam_p, da_subln_w.reshape(HEAD_W, 1))
    hg = _hgrn(as3(hq), as3(lf), as3(kf), as3(gi), as3(sg), hg_norm_w)

    pad = jnp.zeros((d, LANES - N_EXPERTS - N_GROUPS), F32)
    w_route = jnp.concatenate([moe_w_router[0], moe_w_group[0], pad], axis=1)
    b_route = jnp.concatenate([moe_b_router[0], moe_b_group[0], pad[0]]).reshape(1, LANES)
    w_route_hi = w_route.astype(BF16)
    w_route_lo = (w_route - w_route_hi.astype(F32)).astype(BF16)
    x1, h2, meta, route, cnt = _outproj(da.reshape(t, STREAM_W), hg.reshape(t, STREAM_W), x2, mod3,
                                 w_out[0].astype(BF16), norm2_w,
                                 jnp.concatenate([w_route_hi, w_route_lo], axis=1), b_route, seq)

    counts = cnt[0, :N_EXPERTS].astype(jnp.int32)
    tiles_e = (counts + MOE_TILE - 1) // MOE_TILE
    tile_end = jnp.cumsum(tiles_e)
    offs = (tile_end - tiles_e) * MOE_TILE
    ids = route[META_E0:META_E1 + 1].astype(jnp.int32)
    ranks = route[META_R0:META_R1 + 1].astype(jnp.int32)
    expert_iota = jnp.arange(N_EXPERTS, dtype=jnp.int32)[:, None, None]
    pos = jnp.sum(jnp.where(ids[None] == expert_iota, offs[:, None, None], 0), axis=0) + ranks
    n_tiles = (2 * t) // MOE_TILE + N_EXPERTS
    n_valid = tile_end[-1:]
    tile_ids = jnp.minimum(jnp.arange(n_tiles, dtype=jnp.int32), n_valid - 1)
    tile_expert = jnp.sum(tile_ids[:, None] >= tile_end[None, :], axis=1).astype(jnp.int32)
    pos3 = pos.reshape(2, t // GATHER_TILE, GATHER_TILE).transpose(1, 0, 2)
    last_tile = jnp.concatenate([jnp.where(tiles_e > 0, tile_end - 1, -1), n_valid]).astype(jnp.int32)

    xs = _scatter(last_tile, pos3, h2, n_tiles * MOE_TILE)
    ys = _experts(tile_expert, n_valid.astype(jnp.int32), xs, moe_w_gate[0], moe_w_up[0], moe_w_down[0])
    out = _combine(pos3, x1, meta, mod3, final_norm_w.reshape(1, d), ys, seq)
    return out.reshape(bsz, seq, d)
```

```python
import functools
import math

import numpy as np
import jax
import jax.numpy as jnp
from jax import lax
from jax.experimental import pallas as pl
from jax.experimental.pallas import tpu as pltpu

F32 = jnp.float32
BF16 = jnp.bfloat16
HIGHEST = lax.Precision.HIGHEST

LANES = 128
SUBLANES = 8
D_MODEL = 1024
HEADS = 4
HEAD_W = 128
STREAM_W = HEADS * HEAD_W
N_STREAMS = 7
DA_QK_DIM = 64
ROPE_THETA = 500000.0
ROT_DIM = DA_QK_DIM // 4
ROT_HALF = ROT_DIM // 2
N_GROUPS = 4
EXPERTS_PER_GROUP = 8
N_EXPERTS = N_GROUPS * EXPERTS_PER_GROUP
EXPERT_FF = 256
EPS = 1e-6
LAM_INIT = 0.8 - 0.6 * math.exp(-0.3 * 0)
NEG_BIG = -1e30

ROW_TILE = 512
OUT_TILE = 512
ATTN_TILE = 1024
ATTN_KEY_TILE = 512
HGRN_BLOCK = 128
HGRN_STEP = 256
MOE_TILE = 512
GATHER_TILE = 256

NT_DIMS = (((1,), (1,)), ((), ()))
TN_DIMS = (((0,), (0,)), ((), ()))


def _sigmoid(x):
    return 1.0 / (1.0 + jnp.exp(-x))


SLAB = D_MODEL // LANES


def _slab_store(ref, val):
    rows = val.shape[0]
    for c in range(SLAB):
        ref[pl.ds(c, rows, stride=SLAB), :] = val[:, c * LANES:(c + 1) * LANES]


def _slab_load(ref, rows):
    return jnp.concatenate([ref[pl.ds(c, rows, stride=SLAB), :] for c in range(SLAB)], axis=1)


def _adaln_body(c_ref, w_ref, b_ref, o_ref):
    c = c_ref[...]
    ca = c * _sigmoid(c)
    o_ref[...] = jnp.dot(ca, w_ref[...], preferred_element_type=F32, precision=HIGHEST) + b_ref[...]


def _adaln(c, ada_w, ada_b):
    bsz, d = c.shape
    n = ada_w.shape[1]
    tn = 1024
    return pl.pallas_call(
        _adaln_body,
        grid=(n // tn,),
        in_specs=[pl.BlockSpec((bsz, d), lambda j: (0, 0)),
                  pl.BlockSpec((d, tn), lambda j: (0, j)),
                  pl.BlockSpec((1, tn), lambda j: (0, j))],
        out_specs=pl.BlockSpec((bsz, tn), lambda j: (0, j)),
        out_shape=jax.ShapeDtypeStruct((bsz, n), F32),
        name="adaln",
    )(c, ada_w, ada_b)


def _inproj_body(x_ref, mod_ref, nw_ref, w32_ref, pos_ref, invf_ref, spread_ref, lbr_ref,
                 q_ref, k_ref, v_ref, hq_ref, lf_ref, kf_ref, gi_ref, sg_ref, w_ref):
    @pl.when(pl.program_id(0) == 0)
    def _():
        for j in range(N_STREAMS):
            cols = slice(j * STREAM_W, (j + 1) * STREAM_W)
            w_ref[:, cols] = w32_ref[:, cols].astype(BF16)

    x = x_ref[...]
    ms = jnp.mean(x * x, axis=-1, keepdims=True)
    y = x * lax.rsqrt(ms + EPS) * nw_ref[...]
    shift = mod_ref[0, 0:1, :]
    scale = mod_ref[0, 1:2, :]
    h = (y * (1.0 + scale) + shift).astype(BF16)

    ang_t = invf_ref[...] * pos_ref[0]

    def spread(table_t):
        hi = table_t.astype(BF16)
        lo = (table_t - hi.astype(F32)).astype(BF16)
        return lax.dot_general(jnp.concatenate([hi, lo], axis=0), spread_ref[...], TN_DIMS,
                               preferred_element_type=F32)

    lane = lax.broadcasted_iota(jnp.int32, (1, LANES), 1) % DA_QK_DIM
    cosv = spread(jnp.cos(ang_t)) + jnp.where(lane < ROT_DIM, 0.0, 1.0)
    sinv = spread(jnp.sin(ang_t))
    sin_lo = jnp.where(lane < ROT_HALF, -sinv, 0.0)
    sin_hi = jnp.where((lane >= ROT_HALF) & (lane < ROT_DIM), sinv, 0.0)

    streams = [jnp.dot(h, w_ref[:, j * STREAM_W:(j + 1) * STREAM_W], preferred_element_type=F32)
               for j in range(N_STREAMS)]

    def proj(j):
        return streams[j]

    def rope(t):
        outs = []
        for hb in range(HEADS):
            tc = t[:, hb * HEAD_W:(hb + 1) * HEAD_W]
            outs.append(tc * cosv
                        + pltpu.roll(tc, LANES - ROT_HALF, 1) * sin_lo
                        + pltpu.roll(tc, ROT_HALF, 1) * sin_hi)
        return jnp.concatenate(outs, axis=1)

    q_ref[0] = (rope(proj(0)) * (DA_QK_DIM ** -0.5 * math.log2(math.e))).T.astype(BF16)
    k_ref[...] = rope(proj(1)).astype(BF16)
    v_ref[0] = proj(2).astype(BF16).T

    gq = proj(3)
    hq_ref[...] = gq * _sigmoid(gq)

    a = lbr_ref[...]
    amax = jnp.max(a, axis=0, keepdims=True)
    ea = jnp.exp(a - amax)
    lb = ea[0:1, :] / jnp.sum(ea, axis=0, keepdims=True)
    gf = proj(4)
    f = lb + (1.0 - lb) * _sigmoid(gf)
    lf_ref[...] = jnp.log(f) * math.log2(math.e)
    kf_ref[...] = 1.0 - f

    gi_ref[...] = proj(5).astype(BF16)
    gg = proj(6)
    sg_ref[...] = (gg * _sigmoid(gg)).astype(BF16)


def _inproj(x2, mod3, norm_w, w_f32, pos_rows, invf, spread, lb_raw, seq):
    t, d = x2.shape
    tm = ROW_TILE
    per_b = seq // tm
    row = lambda i: (i, 0)
    full = lambda i: (0, 0)
    out_bf = jax.ShapeDtypeStruct((t, STREAM_W), BF16)
    out_f = jax.ShapeDtypeStruct((t, STREAM_W), F32)
    out_t = jax.ShapeDtypeStruct((t // seq, STREAM_W, seq), BF16)
    stream = pl.BlockSpec((tm, STREAM_W), row)
    stream_t = pl.BlockSpec((1, STREAM_W, tm), lambda i: (i // per_b, 0, i % per_b))
    return pl.pallas_call(
        _inproj_body,
        grid=(t // tm,),
        in_specs=[pl.BlockSpec((tm, d), row),
                  pl.BlockSpec((1, 6, d), lambda i: (i // per_b, 0, 0)),
                  pl.BlockSpec((1, d), full),
                  pl.BlockSpec((d, N_STREAMS * STREAM_W), full, pipeline_mode=pl.Buffered(1)),
                  pl.BlockSpec((1, 1, tm), lambda i: (i, 0, 0)),
                  pl.BlockSpec((ROT_HALF, 1), full),
                  pl.BlockSpec((2 * ROT_HALF, LANES), full),
                  pl.BlockSpec(lb_raw.shape, full)],
        out_specs=[stream_t, stream, stream_t] + [stream] * 5,
        out_shape=[out_t, out_bf, out_t, out_f, out_f, out_f, out_bf, out_bf],
        scratch_shapes=[pltpu.VMEM((d, N_STREAMS * STREAM_W), BF16)],
        compiler_params=pltpu.CompilerParams(dimension_semantics=("arbitrary",),
                                             vmem_limit_bytes=56 * 1024 * 1024),
        name="inproj",
    )(x2, mod3, norm_w, w_f32, pos_rows, invf, spread, lb_raw)


ONES_ROWS = 16


def _attn_body(qt_ref, k_ref, vt_ref, lam_ref, sw_ref, o_ref, s_scr, m_scr, a_scr, *, tile, ktile):
    qi = pl.program_id(2)
    qt = qt_ref[0]
    feat = lax.broadcasted_iota(jnp.int32, (HEAD_W, 1), 0)
    zero = jnp.zeros_like(qt)
    qmaps = (jnp.where(feat < DA_QK_DIM, qt, zero), jnp.where(feat >= DA_QK_DIM, qt, zero))

    m_scr[...] = jnp.full(m_scr.shape, NEG_BIG, F32)
    a_scr[...] = jnp.zeros(a_scr.shape, F32)

    def score_block(start, nkeys, q_lo, masked):
        kb = k_ref[0, pl.ds(start, nkeys), :]
        scores = [jnp.dot(kb, qmaps[mp][:, q_lo:], preferred_element_type=F32) for mp in range(2)]
        for mp, s in enumerate(scores):
            if masked:
                key = lax.broadcasted_iota(jnp.int32, s.shape, 0)
                qry = lax.broadcasted_iota(jnp.int32, s.shape, 1)
                s = jnp.where(key <= qry, s, NEG_BIG)
            grouped = s.reshape(nkeys // SUBLANES, SUBLANES, tile - q_lo)
            m_scr[mp, :, q_lo:] = jnp.maximum(m_scr[mp, :, q_lo:], jnp.max(grouped, axis=0))
            s_scr[mp, pl.ds(start, nkeys), q_lo:] = s

    def value_block(start, nkeys, q_lo, col_max):
        vb = jnp.concatenate([vt_ref[0, :, pl.ds(start, nkeys)], jnp.ones((ONES_ROWS, nkeys), BF16)],
                             axis=0)
        probs = [jnp.exp2(s_scr[mp, pl.ds(start, nkeys), q_lo:] - col_max[mp][:, q_lo:]).astype(BF16)
                 for mp in range(2)]
        for mp, p in enumerate(probs):
            a_scr[mp, :, q_lo:] += jnp.dot(vb, p, preferred_element_type=F32)

    diag = pl.multiple_of(qi * tile, tile)
    pieces = [(diag + j * ktile, ktile, j * ktile) for j in range(tile // ktile)]

    def score_body(ki, carry):
        score_block(pl.multiple_of(ki * tile, tile), tile, 0, False)
        return carry

    lax.fori_loop(0, qi, score_body, 0)
    for start, nkeys, q_lo in pieces:
        score_block(start, nkeys, q_lo, True)
    col_max = [jnp.max(m_scr[mp], axis=0, keepdims=True) for mp in range(2)]

    def value_body(ki, carry):
        value_block(pl.multiple_of(ki * tile, tile), tile, 0, col_max)
        return carry

    lax.fori_loop(0, qi, value_body, 0)
    for start, nkeys, q_lo in pieces:
        value_block(start, nkeys, q_lo, col_max)

    lp = lam_ref[...]
    lam = (jnp.exp(jnp.sum(lp[0:1] * lp[1:2], axis=-1, keepdims=True))
           - jnp.exp(jnp.sum(lp[2:3] * lp[3:4], axis=-1, keepdims=True)) + LAM_INIT)
    o = (a_scr[0, 0:HEAD_W, :] / a_scr[0, HEAD_W:HEAD_W + 1, :]
         - lam * (a_scr[1, 0:HEAD_W, :] / a_scr[1, HEAD_W:HEAD_W + 1, :]))
    o = o * lax.rsqrt(jnp.mean(o * o, axis=0, keepdims=True) + EPS) * sw_ref[...]
    o_ref[0] = (o * (1.0 - LAM_INIT)).T.astype(BF16)


def _attn(qt3, k3, vt3, lam_p, subln_col):
    bsz, seq, _ = k3.shape
    tile = ATTN_TILE
    return pl.pallas_call(
        functools.partial(_attn_body, tile=tile, ktile=ATTN_KEY_TILE),
        grid=(bsz, HEADS, seq // tile),
        in_specs=[pl.BlockSpec((1, HEAD_W, tile), lambda b, h, i: (b, h, i)),
                  pl.BlockSpec((1, seq, HEAD_W), lambda b, h, i: (b, 0, h)),
                  pl.BlockSpec((1, HEAD_W, seq), lambda b, h, i: (b, h, 0)),
                  pl.BlockSpec(lam_p.shape, lambda b, h, i: (0, 0)),
                  pl.BlockSpec((HEAD_W, 1), lambda b, h, i: (0, 0))],
        out_specs=pl.BlockSpec((1, tile, HEAD_W), lambda b, h, i: (b, i, h)),
        out_shape=jax.ShapeDtypeStruct((bsz, seq, STREAM_W), BF16),
        scratch_shapes=[pltpu.VMEM((2, seq, tile), F32),
                        pltpu.VMEM((2, SUBLANES, tile), F32),
                        pltpu.VMEM((2, HEAD_W + ONES_ROWS, tile), F32)],
        compiler_params=pltpu.CompilerParams(vmem_limit_bytes=56 * 1024 * 1024),
        name="diff_attn",
    )(qt3, k3, vt3, lam_p, subln_col)


def _hgrn_levels(block):
    return [block >> (i + 1) for i in range(block.bit_length() - 1)]


def _hgrn_constants(block):
    t = np.arange(block)[:, None]
    s = np.arange(block)[None, :]
    tril = (s <= t).astype(np.float32)
    lv = np.full((block, block), -1, np.int32)
    halves = _hgrn_levels(block)
    for li, m in enumerate(halves):
        same = (t // (2 * m)) == (s // (2 * m))
        lv[same & ((t & m) != 0) & ((s & m) == 0)] = li
    lv[np.arange(block), np.arange(block)] = len(halves)
    return jnp.asarray(tril, BF16), jnp.asarray(lv)


def _level_operand(b_ref, h, b, q, k, m, block):
    def ref_rows(r, n):
        return jnp.broadcast_to(b_ref[h, pl.ds(r, 1), :], (n, HEAD_W))

    if m >= SUBLANES:
        pieces = []
        for s0 in range(0, block, 2 * m):
            ref = ref_rows(s0 + m - 1, m)
            lo = slice(s0, s0 + m)
            up = slice(s0 + m, s0 + 2 * m)
            pieces.append(k[lo] * jnp.exp2(ref - b[lo]))
            pieces.append(q[up] * jnp.exp2(b[up] - ref))
        return jnp.concatenate(pieces, axis=0)

    sub = lax.broadcasted_iota(jnp.int32, (SUBLANES, HEAD_W), 0)
    refs = []
    for s0 in range(0, block, SUBLANES):
        piece = ref_rows(s0 + m - 1, SUBLANES)
        for j in range(1, SUBLANES // (2 * m)):
            piece = jnp.where(sub >= 2 * m * j, ref_rows(s0 + 2 * m * j + m - 1, SUBLANES), piece)
        refs.append(piece)
    d = b - jnp.concatenate(refs, axis=0)
    row = lax.broadcasted_iota(jnp.int32, (block, 1), 0)
    return jnp.where((row & m) != 0, q, k) * jnp.exp2(jnp.minimum(d, -d))


def _hgrn_body(hq_ref, lf_ref, kf_ref, gi_ref, sg_ref, nw_ref, tril_ref, lv_ref, o_ref,
               st_scr, b_scr, *, block, step):
    @pl.when(pl.program_id(1) == 0)
    def _():
        st_scr[...] = jnp.zeros(st_scr.shape, F32)

    tril = tril_ref[...]
    lv = lv_ref[...]
    halves = _hgrn_levels(block)

    units = [(u, r0, h) for u, (r0, h) in enumerate(
        (r0, h) for r0 in range(0, step, block) for h in range(HEADS))]

    def cols(h):
        return slice(h * HEAD_W, (h + 1) * HEAD_W)

    q, k, v, b, scores = {}, {}, {}, {}, {}
    for u, r0, h in units:
        rows = slice(r0, r0 + block)
        q[u] = hq_ref[0, rows, cols(h)]
        k[u] = kf_ref[0, rows, cols(h)]
        v[u] = gi_ref[0, rows, cols(h)]
        lf = lf_ref[0, rows, cols(h)]
        hi = lf.astype(BF16)
        r1 = lf - hi.astype(F32)
        mid = r1.astype(BF16)
        lo = (r1 - mid.astype(F32)).astype(BF16)
        b[u] = (jnp.dot(tril, hi, preferred_element_type=F32)
                + jnp.dot(tril, mid, preferred_element_type=F32)
                + jnp.dot(tril, lo, preferred_element_type=F32))
        b_scr[u] = b[u]
        scores[u] = jnp.where(lv == len(halves),
                              lax.dot_general(q[u].astype(BF16), k[u].astype(BF16), NT_DIMS,
                                              preferred_element_type=F32), 0.0)

    for li, m in enumerate(halves):
        for u, r0, h in units:
            xl = _level_operand(b_scr, u, b[u], q[u], k[u], m, block).astype(BF16)
            p = lax.dot_general(xl, xl, NT_DIMS, preferred_element_type=F32)
            scores[u] = jnp.where(lv == li, p, scores[u])

    for u, r0, h in units:
        rows = slice(r0, r0 + block)
        o_intra = jnp.dot(scores[u].astype(BF16), v[u], preferred_element_type=F32)
        st = st_scr[h]
        o_inter = lax.dot_general((q[u] * jnp.exp2(b[u])).astype(BF16), st.astype(BF16), NT_DIMS,
                                  preferred_element_type=F32)
        b_last = b[u][block - 1:block, :]
        kdec = (k[u] * jnp.exp2(b_last - b[u])).astype(BF16)
        st_scr[h] = st * jnp.exp2(b_last) + lax.dot_general(v[u], kdec, TN_DIMS,
                                                             preferred_element_type=F32)
        o = o_inter + o_intra
        o = o * lax.rsqrt(jnp.mean(o * o, axis=-1, keepdims=True) + EPS) * nw_ref[...]
        o_ref[0, rows, cols(h)] = (o * sg_ref[0, rows, cols(h)].astype(F32)).astype(BF16)


def _hgrn(hq3, lf3, kf3, gi3, sg3, norm_w):
    bsz, seq, _ = hq3.shape
    block = HGRN_BLOCK
    step = HGRN_STEP
    tril, lv = _hgrn_constants(block)
    blk = pl.BlockSpec((1, step, STREAM_W), lambda b, g: (b, g, 0))
    const = lambda b, g: (0, 0)
    return pl.pallas_call(
        functools.partial(_hgrn_body, block=block, step=step),
        grid=(bsz, seq // step),
        in_specs=[blk, blk, blk, blk, blk,
                  pl.BlockSpec((1, HEAD_W), const),
                  pl.BlockSpec((block, block), const),
                  pl.BlockSpec((block, block), const)],
        out_specs=blk,
        out_shape=jax.ShapeDtypeStruct((bsz, seq, STREAM_W), BF16),
        scratch_shapes=[pltpu.VMEM((HEADS, HEAD_W, HEAD_W), F32),
                        pltpu.VMEM((HEADS * step // block, block, HEAD_W), F32)],
        name="hgrn2",
    )(hq3, lf3, kf3, gi3, sg3, norm_w, tril, lv)


META_E0, META_E1, META_R0, META_R1, META_W0, META_W1 = range(6)
GROUP_LANE0 = N_EXPERTS


OUT_PARTS = 4


def _route(logits, lane):
    far = jnp.int32(LANES)

    def first_max(vals):
        mx = jnp.max(vals, axis=-1, keepdims=True)
        return mx, jnp.min(jnp.where(vals == mx, lane, far), axis=-1, keepdims=True)

    is_g = (lane >= GROUP_LANE0) & (lane < GROUP_LANE0 + N_GROUPS)
    gmax, glane = first_max(jnp.where(is_g, logits, NEG_BIG))
    g_w = 1.0 / jnp.sum(jnp.where(is_g, jnp.exp(logits - gmax), 0.0), axis=-1, keepdims=True)
    gidx = glane - GROUP_LANE0
    in_grp = (lane < N_EXPERTS) & ((lane // EXPERTS_PER_GROUP) == gidx)
    el = jnp.where(in_grp, logits, NEG_BIG)
    m1, i1 = first_max(el)
    m2, i2 = first_max(jnp.where(lane == i1, NEG_BIG, el))
    r = jnp.exp(m2 - m1)
    return i1, i2, g_w / (1.0 + r), g_w * r / (1.0 + r)


def _outproj_body(da_ref, hg_ref, x_ref, mod_ref, wo_ref, nw_ref, wr_ref, br_ref, stril_ref,
                  x1_ref, h2_ref, meta_ref, route_ref, cnt_ref, carry_scr):
    @pl.when(pl.program_id(0) == 0)
    def _():
        carry_scr[...] = jnp.zeros(carry_scr.shape, F32)

    rows = stril_ref.shape[0]
    parts = [slice(p * rows, (p + 1) * rows) for p in range(OUT_PARTS)]
    gate1 = mod_ref[0, 2:3, :]
    shift2 = mod_ref[0, 3:4, :]
    scale2 = mod_ref[0, 4:5, :]

    attn = [jnp.dot(da_ref[r, :], wo_ref[0:STREAM_W, :], preferred_element_type=F32)
            + jnp.dot(hg_ref[r, :], wo_ref[STREAM_W:, :], preferred_element_type=F32) for r in parts]

    h2 = []
    for r, a in zip(parts, attn):
        x1 = x_ref[r, :] + gate1 * a
        x1_ref[r, :] = x1
        h2.append(x1 * lax.rsqrt(jnp.mean(x1 * x1, axis=-1, keepdims=True) + EPS) * nw_ref[...]
                  * (1.0 + scale2) + shift2)

    logits = []
    for p, h in enumerate(h2):
        _slab_store(h2_ref.at[pl.ds(p * rows * SLAB, rows * SLAB)], h)
        h_hi = h.astype(BF16)
        h_lo = (h - h_hi.astype(F32)).astype(BF16)
        terms = jnp.dot(jnp.concatenate([h_hi, h_lo], axis=0), wr_ref[...],
                        preferred_element_type=F32)
        logits.append((terms[:rows, :LANES] + terms[:rows, LANES:])
                      + (terms[rows:, :LANES] + terms[rows:, LANES:]) + br_ref[...])

    lane = lax.broadcasted_iota(jnp.int32, (rows, LANES), 1)
    routed = [_route(lg, lane) for lg in logits]

    carry = carry_scr[...]
    for r, (i1, i2, w0, w1) in zip(parts, routed):
        hot0 = lane == i1
        hot1 = lane == i2
        multi = jnp.where(hot0 | hot1, 1.0, 0.0)
        before = jnp.dot(stril_ref[...], multi.astype(BF16), preferred_element_type=F32) + carry
        rank0 = jnp.sum(jnp.where(hot0, before, 0.0), axis=-1, keepdims=True)
        rank1 = jnp.sum(jnp.where(hot1, before, 0.0), axis=-1, keepdims=True)
        carry = carry + jnp.sum(multi, axis=0, keepdims=True)
        meta = jnp.zeros((rows, LANES), F32)
        for idx, val in ((META_E0, i1.astype(F32)), (META_E1, i2.astype(F32)),
                         (META_R0, rank0), (META_R1, rank1), (META_W0, w0), (META_W1, w1)):
            meta = jnp.where(lane == idx, val, meta)
        meta_ref[r, :] = meta
        route_ref[:, r] = meta.T[0:SUBLANES, :]
    carry_scr[...] = carry
    cnt_ref[...] = carry


def _outproj(da2, hg2, x2, mod3, wo_bf, norm_w, w_route, b_route, seq):
    t, d = x2.shape
    tm = OUT_TILE
    per_b = seq // tm
    row = lambda i: (i, 0)
    full = lambda i: (0, 0)
    part = tm // OUT_PARTS
    stril = jnp.asarray(np.tril(np.ones((part, part), np.float32), -1), BF16)
    return pl.pallas_call(
        _outproj_body,
        grid=(t // tm,),
        in_specs=[pl.BlockSpec((tm, STREAM_W), row),
                  pl.BlockSpec((tm, STREAM_W), row),
                  pl.BlockSpec((tm, d), row),
                  pl.BlockSpec((1, 6, d), lambda i: (i // per_b, 0, 0)),
                  pl.BlockSpec((2 * STREAM_W, d), full),
                  pl.BlockSpec((1, d), full),
                  pl.BlockSpec((d, 2 * LANES), full),
                  pl.BlockSpec((1, LANES), full),
                  pl.BlockSpec((part, part), full)],
        out_specs=[pl.BlockSpec((tm, d), row),
                   pl.BlockSpec((tm * SLAB, LANES), row),
                   pl.BlockSpec((tm, LANES), row),
                   pl.BlockSpec((SUBLANES, tm), lambda i: (0, i)),
                   pl.BlockSpec((1, LANES), full)],
        out_shape=[jax.ShapeDtypeStruct((t, d), F32),
                   jax.ShapeDtypeStruct((t * SLAB, LANES), F32),
                   jax.ShapeDtypeStruct((t, LANES), F32),
                   jax.ShapeDtypeStruct((SUBLANES, t), F32),
                   jax.ShapeDtypeStruct((1, LANES), F32)],
        scratch_shapes=[pltpu.VMEM((1, LANES), F32)],
        compiler_params=pltpu.CompilerParams(dimension_semantics=("arbitrary",)),
        name="outproj_route",
    )(da2, hg2, x2, mod3, wo_bf, norm_w, w_route, b_route, stril)


DMA_UNROLL = 8


def _scatter_body(last_ref, pos_ref, h2_ref, xs_hbm, zero_scr, sem, zsem, *, tile):
    rows = tile * SLAB

    @pl.when(pl.program_id(0) == 0)
    def _():
        zero_scr[...] = jnp.zeros(zero_scr.shape, F32)

        def clear(tile_idx):
            start = pl.multiple_of(tile_idx * (MOE_TILE * SLAB), SLAB)
            return pltpu.make_async_copy(zero_scr, xs_hbm.at[pl.ds(start, MOE_TILE * SLAB)], zsem)

        for e in range(N_EXPERTS):
            @pl.when(last_ref[e] >= 0)
            def _():
                clear(last_ref[e]).start()
        for e in range(N_EXPERTS):
            @pl.when(last_ref[e] >= 0)
            def _():
                clear(last_ref[e]).wait()

        def clear_unused(tile_idx, carry):
            clear(tile_idx).start()
            clear(tile_idx).wait()
            return carry

        lax.fori_loop(last_ref[N_EXPERTS], xs_hbm.shape[0] // (MOE_TILE * SLAB), clear_unused, 0)

    def issue(t, carry):
        src = h2_ref.at[pl.ds(pl.multiple_of(t * SLAB, SLAB), SLAB)]
        for j in range(2):
            slot = pl.multiple_of(pos_ref[0, 0, j * tile + t] * SLAB, SLAB)
            pltpu.make_async_copy(src, xs_hbm.at[pl.ds(slot, SLAB)], sem).start(priority=j)
        return carry

    lax.fori_loop(0, tile, issue, 0, unroll=DMA_UNROLL)
    whole = pltpu.make_async_copy(h2_ref, xs_hbm.at[pl.ds(0, rows)], sem)
    whole.wait()
    whole.wait()


def _scatter(last_tile, pos3, h2s, n_slots):
    tile = GATHER_TILE
    rows = tile * SLAB
    grid_spec = pltpu.PrefetchScalarGridSpec(
        num_scalar_prefetch=1,
        grid=(h2s.shape[0] // rows,),
        in_specs=[pl.BlockSpec((1, 1, 2 * tile), lambda i, lt: (i, 0, 0), memory_space=pltpu.SMEM),
                  pl.BlockSpec((rows, LANES), lambda i, lt: (i, 0))],
        out_specs=pl.BlockSpec(memory_space=pl.ANY),
        scratch_shapes=[pltpu.VMEM((MOE_TILE * SLAB, LANES), F32),
                        pltpu.SemaphoreType.DMA, pltpu.SemaphoreType.DMA],
    )
    return pl.pallas_call(
        functools.partial(_scatter_body, tile=tile),
        grid_spec=grid_spec,
        out_shape=jax.ShapeDtypeStruct((n_slots * SLAB, LANES), F32),
        compiler_params=pltpu.CompilerParams(dimension_semantics=("arbitrary",)),
        name="moe_scatter",
    )(last_tile, pos3, h2s)


def _experts_body(te_ref, nv_ref, x_ref, wg_ref, wu_ref, wd_ref, y_ref, wgu_scr, wd_scr, *, tm):
    i = pl.program_id(0)
    e = te_ref[i]
    prev = te_ref[jnp.maximum(i - 1, 0)]

    @pl.when((i == 0) | (e != prev))
    def _():
        wgu_scr[:, 0:EXPERT_FF] = wg_ref[0].astype(BF16)
        wgu_scr[:, EXPERT_FF:] = wu_ref[0].astype(BF16)
        wd_scr[...] = wd_ref[0].astype(BF16)

    @pl.when(i < nv_ref[0])
    def _():
        half = tm // 2
        xr = [x_ref.at[pl.ds(r * half * SLAB, half * SLAB)] for r in range(2)]
        yr = [y_ref.at[pl.ds(r * half * SLAB, half * SLAB)] for r in range(2)]
        xs = [_slab_load(ref, half).astype(BF16) for ref in xr]
        gus = [jnp.dot(x, wgu_scr[...], preferred_element_type=F32) for x in xs]
        acts = [(gu[:, 0:EXPERT_FF] * _sigmoid(gu[:, 0:EXPERT_FF]) * gu[:, EXPERT_FF:]).astype(BF16)
                for gu in gus]
        ys = [jnp.dot(act, wd_scr[...], preferred_element_type=F32) for act in acts]
        for ref, y in zip(yr, ys):
            _slab_store(ref, y)

    @pl.when(i >= nv_ref[0])
    def _():
        y_ref[...] = jnp.zeros(y_ref.shape, F32)


def _experts(tile_expert, n_valid, xs, w_gate, w_up, w_down):
    tm = MOE_TILE
    rows = tm * SLAB
    d = w_gate.shape[1]
    live = lambda i, te, nv: (jnp.minimum(i, nv[0] - 1), 0)
    grid_spec = pltpu.PrefetchScalarGridSpec(
        num_scalar_prefetch=2,
        grid=(xs.shape[0] // rows,),
        in_specs=[pl.BlockSpec((rows, LANES), live),
                  pl.BlockSpec((1, d, EXPERT_FF), lambda i, te, nv: (te[i], 0, 0)),
                  pl.BlockSpec((1, d, EXPERT_FF), lambda i, te, nv: (te[i], 0, 0)),
                  pl.BlockSpec((1, EXPERT_FF, d), lambda i, te, nv: (te[i], 0, 0))],
        out_specs=pl.BlockSpec((rows, LANES), lambda i, te, nv: (i, 0)),
        scratch_shapes=[pltpu.VMEM((d, 2 * EXPERT_FF), BF16),
                        pltpu.VMEM((EXPERT_FF, d), BF16)],
    )
    return pl.pallas_call(
        functools.partial(_experts_body, tm=tm),
        grid_spec=grid_spec,
        out_shape=jax.ShapeDtypeStruct(xs.shape, F32),
        compiler_params=pltpu.CompilerParams(dimension_semantics=("arbitrary",)),
        name="moe_experts",
    )(tile_expert, n_valid, xs, w_gate, w_up, w_down)


def _combine_body(pos_ref, nxt_ref, x1_ref, meta_ref, mod_ref, nw_ref, ys_hbm, o_ref, rows_scr, sems,
                  *, tile):
    i = pl.program_id(0)
    n = pl.num_programs(0)
    rows = tile * SLAB

    def fetch(idx_ref, buf):
        def issue(t, carry):
            dst = pl.ds(pl.multiple_of(t * SLAB, SLAB), SLAB)
            for j in range(2):
                slot = pl.multiple_of(idx_ref[0, 0, j * tile + t] * SLAB, SLAB)
                pltpu.make_async_copy(ys_hbm.at[pl.ds(slot, SLAB)], rows_scr.at[buf, j, dst],
                                      sems.at[buf]).start(priority=j)
            return carry
        lax.fori_loop(0, tile, issue, 0, unroll=DMA_UNROLL)

    @pl.when(i == 0)
    def _():
        fetch(pos_ref, 0)

    for cur in range(2):
        @pl.when((i % 2 == cur) & (i + 1 < n))
        def _():
            fetch(nxt_ref, 1 - cur)

    for cur in range(2):
        @pl.when(i % 2 == cur)
        def _():
            for j in range(2):
                pltpu.make_async_copy(ys_hbm.at[pl.ds(0, rows)], rows_scr.at[cur, j],
                                      sems.at[cur]).wait()
            meta = meta_ref[...]
            w0 = meta[:, META_W0:META_W0 + 1]
            w1 = meta[:, META_W1:META_W1 + 1]
            y = (w0 * _slab_load(rows_scr.at[cur, 0], tile)
                 + w1 * _slab_load(rows_scr.at[cur, 1], tile))
            x2 = x1_ref[...] + mod_ref[0, 5:6, :] * y
            o_ref[...] = (x2 * lax.rsqrt(jnp.mean(x2 * x2, axis=-1, keepdims=True) + EPS)
                          * nw_ref[...])


def _combine(pos3, x1, meta, mod3, norm_w, ys, seq):
    t, d = x1.shape
    tile = GATHER_TILE
    per_b = seq // tile
    n = t // tile
    row = lambda i: (i, 0)
    return pl.pallas_call(
        functools.partial(_combine_body, tile=tile),
        grid=(n,),
        in_specs=[pl.BlockSpec((1, 1, 2 * tile), lambda i: (i, 0, 0), memory_space=pltpu.SMEM),
                  pl.BlockSpec((1, 1, 2 * tile), lambda i: (jnp.minimum(i + 1, n - 1), 0, 0),
                               memory_space=pltpu.SMEM),
                  pl.BlockSpec((tile, d), row),
                  pl.BlockSpec((tile, LANES), row),
                  pl.BlockSpec((1, 6, d), lambda i: (i // per_b, 0, 0)),
                  pl.BlockSpec((1, d), lambda i: (0, 0)),
                  pl.BlockSpec(memory_space=pl.ANY)],
        out_specs=pl.BlockSpec((tile, d), row),
        out_shape=jax.ShapeDtypeStruct((t, d), F32),
        scratch_shapes=[pltpu.VMEM((2, 2, tile * SLAB, LANES), F32), pltpu.SemaphoreType.DMA((2,))],
        compiler_params=pltpu.CompilerParams(dimension_semantics=("arbitrary",)),
        name="moe_combine",
    )(pos3, pos3, x1, meta, mod3, norm_w, ys)


def _rope_constants():
    inv_freq = ROPE_THETA ** (-jnp.arange(ROT_HALF, dtype=F32) / ROT_HALF)
    lane = np.arange(LANES) % DA_QK_DIM
    hit = (lane[None, :] < ROT_DIM) & (lane[None, :] % ROT_HALF == np.arange(ROT_HALF)[:, None])
    spread = np.concatenate([hit, hit], axis=0).astype(np.float32)
    return inv_freq.reshape(ROT_HALF, 1), jnp.asarray(spread, BF16)


def kernel(x, c, positions, norm1_w, norm2_w, final_norm_w, ada_w, ada_b, w_in, w_out, da_lambda_q1, da_lambda_k1, da_lambda_q2, da_lambda_k2, da_subln_w, hg_lower_bound, hg_norm_w, moe_w_group, moe_b_group, moe_w_router, moe_b_router, moe_w_gate, moe_w_up, moe_w_down):
    bsz, seq, d = x.shape
    assert d == D_MODEL and norm1_w.shape[0] == 1, "single-layer model of width 1024 only"
    assert seq % ATTN_TILE == 0 and seq % HGRN_STEP == 0 and seq % OUT_TILE == 0
    t = bsz * seq
    x2 = x.reshape(t, d)

    mod3 = _adaln(c, ada_w[0], ada_b).reshape(bsz, 6, d)

    pos_rows = positions.astype(F32).reshape(t // ROW_TILE, 1, ROW_TILE)
    inv_freq, spread = _rope_constants()
    qt, k, vt, hq, lf, kf, gi, sg = _inproj(x2, mod3, norm1_w, w_in[0], pos_rows,
                                          inv_freq, spread, hg_lower_bound, seq)

    as3 = lambda a: a.reshape(bsz, seq, STREAM_W)
    lam_p = jnp.concatenate([da_lambda_q1, da_lambda_k1, da_lambda_q2, da_lambda_k2], axis=0)
    da = _attn(qt, as3(k), vt, lam_p, da_subln_w.reshape(HEAD_W, 1))
    hg = _hgrn(as3(hq), as3(lf), as3(kf), as3(gi), as3(sg), hg_norm_w)

    pad = jnp.zeros((d, LANES - N_EXPERTS - N_GROUPS), F32)
    w_route = jnp.concatenate([moe_w_router[0], moe_w_group[0], pad], axis=1)
    b_route = jnp.concatenate([moe_b_router[0], moe_b_group[0], pad[0]]).reshape(1, LANES)
    w_route_hi = w_route.astype(BF16)
    w_route_lo = (w_route - w_route_hi.astype(F32)).astype(BF16)
    x1, h2, meta, route, cnt = _outproj(da.reshape(t, STREAM_W), hg.reshape(t, STREAM_W), x2, mod3,
                                 w_out[0].astype(BF16), norm2_w,
                                 jnp.concatenate([w_route_hi, w_route_lo], axis=1), b_route, seq)

    counts = cnt[0, :N_EXPERTS].astype(jnp.int32)
    tiles_e = (counts + MOE_TILE - 1) // MOE_TILE
    tile_end = jnp.cumsum(tiles_e)
    offs = (tile_end - tiles_e) * MOE_TILE
    ids = route[META_E0:META_E1 + 1].astype(jnp.int32)
    ranks = route[META_R0:META_R1 + 1].astype(jnp.int32)
    expert_iota = jnp.arange(N_EXPERTS, dtype=jnp.int32)[:, None, None]
    pos = jnp.sum(jnp.where(ids[None] == expert_iota, offs[:, None, None], 0), axis=0) + ranks
    n_tiles = (2 * t) // MOE_TILE + N_EXPERTS
    n_valid = tile_end[-1:]
    tile_ids = jnp.minimum(jnp.arange(n_tiles, dtype=jnp.int32), n_valid - 1)
    tile_expert = jnp.sum(tile_ids[:, None] >= tile_end[None, :], axis=1).astype(jnp.int32)
    pos3 = (pos.reshape(2, t // GATHER_TILE, GATHER_TILE).transpose(1, 0, 2)
            .reshape(t // GATHER_TILE, 1, 2 * GATHER_TILE))
    last_tile = jnp.concatenate([jnp.where(tiles_e > 0, tile_end - 1, -1), n_valid]).astype(jnp.int32)

    xs = _scatter(last_tile, pos3, h2, n_tiles * MOE_TILE)
    ys = _experts(tile_expert, n_valid.astype(jnp.int32), xs, moe_w_gate[0], moe_w_up[0], moe_w_down[0])
    out = _combine(pos3, x1, meta, mod3, final_norm_w.reshape(1, d), ys, seq)
    return out.reshape(bsz, seq, d)
```

```python
import functools
import math

import numpy as np
import jax
import jax.numpy as jnp
from jax import lax
from jax.experimental import pallas as pl
from jax.experimental.pallas import tpu as pltpu

F32 = jnp.float32
BF16 = jnp.bfloat16
HIGHEST = lax.Precision.HIGHEST

LANES = 128
SUBLANES = 8
D_MODEL = 1024
HEADS = 4
HEAD_W = 128
STREAM_W = HEADS * HEAD_W
N_STREAMS = 7
DA_QK_DIM = 64
ROPE_THETA = 500000.0
ROT_DIM = DA_QK_DIM // 4
ROT_HALF = ROT_DIM // 2
N_GROUPS = 4
EXPERTS_PER_GROUP = 8
N_EXPERTS = N_GROUPS * EXPERTS_PER_GROUP
EXPERT_FF = 256
EPS = 1e-6
LAM_INIT = 0.8 - 0.6 * math.exp(-0.3 * 0)
NEG_BIG = -1e30

ROW_TILE = 512
OUT_TILE = 512
ATTN_TILE = 1024
ATTN_KEY_TILE = 512
HGRN_BLOCK = 128
HGRN_STEP = 256
MOE_TILE = 512
GATHER_TILE = 256

NT_DIMS = (((1,), (1,)), ((), ()))
TN_DIMS = (((0,), (0,)), ((), ()))


def _sigmoid(x):
    return 1.0 / (1.0 + jnp.exp(-x))


SLAB = D_MODEL // LANES


def _slab_store(ref, val):
    rows = val.shape[0]
    for c in range(SLAB):
        ref[pl.ds(c, rows, stride=SLAB), :] = val[:, c * LANES:(c + 1) * LANES]


def _slab_load(ref, rows):
    return jnp.concatenate([ref[pl.ds(c, rows, stride=SLAB), :] for c in range(SLAB)], axis=1)


def _adaln_body(c_ref, w_ref, b_ref, o_ref):
    c = c_ref[...]
    ca = c * _sigmoid(c)
    o_ref[...] = jnp.dot(ca, w_ref[...], preferred_element_type=F32, precision=HIGHEST) + b_ref[...]


def _adaln(c, ada_w, ada_b):
    bsz, d = c.shape
    n = ada_w.shape[1]
    tn = 1024
    return pl.pallas_call(
        _adaln_body,
        grid=(n // tn,),
        in_specs=[pl.BlockSpec((bsz, d), lambda j: (0, 0)),
                  pl.BlockSpec((d, tn), lambda j: (0, j)),
                  pl.BlockSpec((1, tn), lambda j: (0, j))],
        out_specs=pl.BlockSpec((bsz, tn), lambda j: (0, j)),
        out_shape=jax.ShapeDtypeStruct((bsz, n), F32),
        name="adaln",
    )(c, ada_w, ada_b)


def _inproj_body(x_ref, mod_ref, nw_ref, w32_ref, pos_ref, invf_ref, spread_ref, lbr_ref,
                 q_ref, k_ref, v_ref, hq_ref, lf_ref, kf_ref, gi_ref, sg_ref, w_ref):
    @pl.when(pl.program_id(0) == 0)
    def _():
        for j in range(N_STREAMS):
            cols = slice(j * STREAM_W, (j + 1) * STREAM_W)
            w_ref[:, cols] = w32_ref[:, cols].astype(BF16)

    x = x_ref[...]
    ms = jnp.mean(x * x, axis=-1, keepdims=True)
    y = x * lax.rsqrt(ms + EPS) * nw_ref[...]
    shift = mod_ref[0, 0:1, :]
    scale = mod_ref[0, 1:2, :]
    h = (y * (1.0 + scale) + shift).astype(BF16)

    ang_t = invf_ref[...] * pos_ref[0]

    def spread(table_t):
        hi = table_t.astype(BF16)
        lo = (table_t - hi.astype(F32)).astype(BF16)
        return lax.dot_general(jnp.concatenate([hi, lo], axis=0), spread_ref[...], TN_DIMS,
                               preferred_element_type=F32)

    lane = lax.broadcasted_iota(jnp.int32, (1, LANES), 1) % DA_QK_DIM
    cosv = spread(jnp.cos(ang_t)) + jnp.where(lane < ROT_DIM, 0.0, 1.0)
    sinv = spread(jnp.sin(ang_t))
    sin_lo = jnp.where(lane < ROT_HALF, -sinv, 0.0)
    sin_hi = jnp.where((lane >= ROT_HALF) & (lane < ROT_DIM), sinv, 0.0)

    streams = [jnp.dot(h, w_ref[:, j * STREAM_W:(j + 1) * STREAM_W], preferred_element_type=F32)
               for j in range(N_STREAMS)]

    def proj(j):
        return streams[j]

    def rope(t):
        outs = []
        for hb in range(HEADS):
            tc = t[:, hb * HEAD_W:(hb + 1) * HEAD_W]
            outs.append(tc * cosv
                        + pltpu.roll(tc, LANES - ROT_HALF, 1) * sin_lo
                        + pltpu.roll(tc, ROT_HALF, 1) * sin_hi)
        return jnp.concatenate(outs, axis=1)

    q_ref[0] = (rope(proj(0)) * (DA_QK_DIM ** -0.5 * math.log2(math.e))).T.astype(BF16)
    k_ref[...] = rope(proj(1)).astype(BF16)
    v_ref[0] = proj(2).astype(BF16).T

    gq = proj(3)
    hq_ref[...] = gq * _sigmoid(gq)

    a = lbr_ref[...]
    amax = jnp.max(a, axis=0, keepdims=True)
    ea = jnp.exp(a - amax)
    lb = ea[0:1, :] / jnp.sum(ea, axis=0, keepdims=True)
    gf = proj(4)
    f = lb + (1.0 - lb) * _sigmoid(gf)
    lf_ref[...] = jnp.log(f) * math.log2(math.e)
    kf_ref[...] = 1.0 - f

    gi_ref[...] = proj(5).astype(BF16)
    gg = proj(6)
    sg_ref[...] = (gg * _sigmoid(gg)).astype(BF16)


def _inproj(x2, mod3, norm_w, w_f32, pos_rows, invf, spread, lb_raw, seq):
    t, d = x2.shape
    tm = ROW_TILE
    per_b = seq // tm
    row = lambda i: (i, 0)
    full = lambda i: (0, 0)
    out_bf = jax.ShapeDtypeStruct((t, STREAM_W), BF16)
    out_f = jax.ShapeDtypeStruct((t, STREAM_W), F32)
    out_t = jax.ShapeDtypeStruct((t // seq, STREAM_W, seq), BF16)
    stream = pl.BlockSpec((tm, STREAM_W), row)
    stream_t = pl.BlockSpec((1, STREAM_W, tm), lambda i: (i // per_b, 0, i % per_b))
    return pl.pallas_call(
        _inproj_body,
        grid=(t // tm,),
        in_specs=[pl.BlockSpec((tm, d), row),
                  pl.BlockSpec((1, 6, d), lambda i: (i // per_b, 0, 0)),
                  pl.BlockSpec((1, d), full),
                  pl.BlockSpec((d, N_STREAMS * STREAM_W), full, pipeline_mode=pl.Buffered(1)),
                  pl.BlockSpec((1, 1, tm), lambda i: (i, 0, 0)),
                  pl.BlockSpec((ROT_HALF, 1), full),
                  pl.BlockSpec((2 * ROT_HALF, LANES), full),
                  pl.BlockSpec(lb_raw.shape, full)],
        out_specs=[stream_t, stream, stream_t] + [stream] * 5,
        out_shape=[out_t, out_bf, out_t, out_f, out_f, out_f, out_bf, out_bf],
        scratch_shapes=[pltpu.VMEM((d, N_STREAMS * STREAM_W), BF16)],
        compiler_params=pltpu.CompilerParams(dimension_semantics=("arbitrary",),
                                             vmem_limit_bytes=56 * 1024 * 1024),
        name="inproj",
    )(x2, mod3, norm_w, w_f32, pos_rows, invf, spread, lb_raw)


ONES_ROWS = 16


def _attn_body(qt_ref, k_ref, vt_ref, lam_ref, sw_ref, o_ref, s_scr, m_scr, a_scr, *, tile, ktile):
    qi = pl.program_id(2)
    qt = qt_ref[0]
    feat = lax.broadcasted_iota(jnp.int32, (HEAD_W, 1), 0)
    zero = jnp.zeros_like(qt)
    qmaps = (jnp.where(feat < DA_QK_DIM, qt, zero), jnp.where(feat >= DA_QK_DIM, qt, zero))

    m_scr[...] = jnp.full(m_scr.shape, NEG_BIG, F32)
    a_scr[...] = jnp.zeros(a_scr.shape, F32)

    def score_block(start, nkeys, q_lo, masked):
        kb = k_ref[0, pl.ds(start, nkeys), :]
        scores = [jnp.dot(kb, qmaps[mp][:, q_lo:], preferred_element_type=F32) for mp in range(2)]
        for mp, s in enumerate(scores):
            if masked:
                key = lax.broadcasted_iota(jnp.int32, s.shape, 0)
                qry = lax.broadcasted_iota(jnp.int32, s.shape, 1)
                s = jnp.where(key <= qry, s, NEG_BIG)
            grouped = s.reshape(nkeys // SUBLANES, SUBLANES, tile - q_lo)
            m_scr[mp, :, q_lo:] = jnp.maximum(m_scr[mp, :, q_lo:], jnp.max(grouped, axis=0))
            s_scr[mp, pl.ds(start, nkeys), q_lo:] = s

    def value_block(start, nkeys, q_lo, col_max):
        vb = jnp.concatenate([vt_ref[0, :, pl.ds(start, nkeys)], jnp.ones((ONES_ROWS, nkeys), BF16)],
                             axis=0)
        probs = [jnp.exp2(s_scr[mp, pl.ds(start, nkeys), q_lo:] - col_max[mp][:, q_lo:]).astype(BF16)
                 for mp in range(2)]
        for mp, p in enumerate(probs):
            a_scr[mp, :, q_lo:] += jnp.dot(vb, p, preferred_element_type=F32)

    diag = pl.multiple_of(qi * tile, tile)
    pieces = [(diag + j * ktile, ktile, j * ktile) for j in range(tile // ktile)]

    def score_body(ki, carry):
        score_block(pl.multiple_of(ki * tile, tile), tile, 0, False)
        return carry

    lax.fori_loop(0, qi, score_body, 0)
    for start, nkeys, q_lo in pieces:
        score_block(start, nkeys, q_lo, True)
    col_max = [jnp.max(m_scr[mp], axis=0, keepdims=True) for mp in range(2)]

    def value_body(ki, carry):
        value_block(pl.multiple_of(ki * tile, tile), tile, 0, col_max)
        return carry

    lax.fori_loop(0, qi, value_body, 0)
    for start, nkeys, q_lo in pieces:
        value_block(start, nkeys, q_lo, col_max)

    lp = lam_ref[...]
    lam = (jnp.exp(jnp.sum(lp[0:1] * lp[1:2], axis=-1, keepdims=True))
           - jnp.exp(jnp.sum(lp[2:3] * lp[3:4], axis=-1, keepdims=True)) + LAM_INIT)
    o = (a_scr[0, 0:HEAD_W, :] / a_scr[0, HEAD_W:HEAD_W + 1, :]
         - lam * (a_scr[1, 0:HEAD_W, :] / a_scr[1, HEAD_W:HEAD_W + 1, :]))
    o = o * lax.rsqrt(jnp.mean(o * o, axis=0, keepdims=True) + EPS) * sw_ref[...]
    o_ref[0] = (o * (1.0 - LAM_INIT)).T.astype(BF16)


def _attn(qt3, k3, vt3, lam_p, subln_col):
    bsz, seq, _ = k3.shape
    tile = ATTN_TILE
    return pl.pallas_call(
        functools.partial(_attn_body, tile=tile, ktile=ATTN_KEY_TILE),
        grid=(bsz, HEADS, seq // tile),
        in_specs=[pl.BlockSpec((1, HEAD_W, tile), lambda b, h, i: (b, h, i)),
                  pl.BlockSpec((1, seq, HEAD_W), lambda b, h, i: (b, 0, h)),
                  pl.BlockSpec((1, HEAD_W, seq), lambda b, h, i: (b, h, 0)),
                  pl.BlockSpec(lam_p.shape, lambda b, h, i: (0, 0)),
                  pl.BlockSpec((HEAD_W, 1), lambda b, h, i: (0, 0))],
        out_specs=pl.BlockSpec((1, tile, HEAD_W), lambda b, h, i: (b, i, h)),
        out_shape=jax.ShapeDtypeStruct((bsz, seq, STREAM_W), BF16),
        scratch_shapes=[pltpu.VMEM((2, seq, tile), F32),
                        pltpu.VMEM((2, SUBLANES, tile), F32),
                        pltpu.VMEM((2, HEAD_W + ONES_ROWS, tile), F32)],
        compiler_params=pltpu.CompilerParams(vmem_limit_bytes=56 * 1024 * 1024),
        name="diff_attn",
    )(qt3, k3, vt3, lam_p, subln_col)


def _hgrn_levels(block):
    return [block >> (i + 1) for i in range(block.bit_length() - 1)]


def _hgrn_constants(block):
    t = np.arange(block)[:, None]
    s = np.arange(block)[None, :]
    tril = (s <= t).astype(np.float32)
    lv = np.full((block, block), -1, np.int32)
    halves = _hgrn_levels(block)
    for li, m in enumerate(halves):
        same = (t // (2 * m)) == (s // (2 * m))
        lv[same & ((t & m) != 0) & ((s & m) == 0)] = li
    lv[np.arange(block), np.arange(block)] = len(halves)
    return jnp.asarray(tril, BF16), jnp.asarray(lv)


def _level_operand(b_ref, h, b, q, k, m, block):
    def ref_rows(r, n):
        return jnp.broadcast_to(b_ref[h, pl.ds(r, 1), :], (n, HEAD_W))

    if m >= SUBLANES:
        pieces = []
        for s0 in range(0, block, 2 * m):
            ref = ref_rows(s0 + m - 1, m)
            lo = slice(s0, s0 + m)
            up = slice(s0 + m, s0 + 2 * m)
            pieces.append(k[lo] * jnp.exp2(ref - b[lo]))
            pieces.append(q[up] * jnp.exp2(b[up] - ref))
        return jnp.concatenate(pieces, axis=0)

    sub = lax.broadcasted_iota(jnp.int32, (SUBLANES, HEAD_W), 0)
    refs = []
    for s0 in range(0, block, SUBLANES):
        piece = ref_rows(s0 + m - 1, SUBLANES)
        for j in range(1, SUBLANES // (2 * m)):
            piece = jnp.where(sub >= 2 * m * j, ref_rows(s0 + 2 * m * j + m - 1, SUBLANES), piece)
        refs.append(piece)
    d = b - jnp.concatenate(refs, axis=0)
    row = lax.broadcasted_iota(jnp.int32, (block, 1), 0)
    return jnp.where((row & m) != 0, q, k) * jnp.exp2(jnp.minimum(d, -d))


def _hgrn_body(hq_ref, lf_ref, kf_ref, gi_ref, sg_ref, nw_ref, tril_ref, lv_ref, o_ref,
               st_scr, b_scr, *, block, step):
    @pl.when(pl.program_id(1) == 0)
    def _():
        st_scr[...] = jnp.zeros(st_scr.shape, F32)

    tril = tril_ref[...]
    lv = lv_ref[...]
    halves = _hgrn_levels(block)

    units = [(u, r0, h) for u, (r0, h) in enumerate(
        (r0, h) for r0 in range(0, step, block) for h in range(HEADS))]

    def cols(h):
        return slice(h * HEAD_W, (h + 1) * HEAD_W)

    q, k, v, b, scores = {}, {}, {}, {}, {}
    for u, r0, h in units:
        rows = slice(r0, r0 + block)
        q[u] = hq_ref[0, rows, cols(h)]
        k[u] = kf_ref[0, rows, cols(h)]
        v[u] = gi_ref[0, rows, cols(h)]
        lf = lf_ref[0, rows, cols(h)]
        hi = lf.astype(BF16)
        r1 = lf - hi.astype(F32)
        mid = r1.astype(BF16)
        lo = (r1 - mid.astype(F32)).astype(BF16)
        b[u] = (jnp.dot(tril, hi, preferred_element_type=F32)
                + jnp.dot(tril, mid, preferred_element_type=F32)
                + jnp.dot(tril, lo, preferred_element_type=F32))
        b_scr[u] = b[u]
        scores[u] = jnp.where(lv == len(halves),
                              lax.dot_general(q[u].astype(BF16), k[u].astype(BF16), NT_DIMS,
                                              preferred_element_type=F32), 0.0)

    for li, m in enumerate(halves):
        for u, r0, h in units:
            xl = _level_operand(b_scr, u, b[u], q[u], k[u], m, block).astype(BF16)
            p = lax.dot_general(xl, xl, NT_DIMS, preferred_element_type=F32)
            scores[u] = jnp.where(lv == li, p, scores[u])

    for u, r0, h in units:
        rows = slice(r0, r0 + block)
        o_intra = jnp.dot(scores[u].astype(BF16), v[u], preferred_element_type=F32)
        st = st_scr[h]
        o_inter = lax.dot_general((q[u] * jnp.exp2(b[u])).astype(BF16), st.astype(BF16), NT_DIMS,
                                  preferred_element_type=F32)
        b_last = b[u][block - 1:block, :]
        kdec = (k[u] * jnp.exp2(b_last - b[u])).astype(BF16)
        st_scr[h] = st * jnp.exp2(b_last) + lax.dot_general(v[u], kdec, TN_DIMS,
                                                             preferred_element_type=F32)
        o = o_inter + o_intra
        o = o * lax.rsqrt(jnp.mean(o * o, axis=-1, keepdims=True) + EPS) * nw_ref[...]
        o_ref[0, rows, cols(h)] = (o * sg_ref[0, rows, cols(h)].astype(F32)).astype(BF16)


def _hgrn(hq3, lf3, kf3, gi3, sg3, norm_w):
    bsz, seq, _ = hq3.shape
    block = HGRN_BLOCK
    step = HGRN_STEP
    tril, lv = _hgrn_constants(block)
    blk = pl.BlockSpec((1, step, STREAM_W), lambda b, g: (b, g, 0))
    const = lambda b, g: (0, 0)
    return pl.pallas_call(
        functools.partial(_hgrn_body, block=block, step=step),
        grid=(bsz, seq // step),
        in_specs=[blk, blk, blk, blk, blk,
                  pl.BlockSpec((1, HEAD_W), const),
                  pl.BlockSpec((block, block), const),
                  pl.BlockSpec((block, block), const)],
        out_specs=blk,
        out_shape=jax.ShapeDtypeStruct((bsz, seq, STREAM_W), BF16),
        scratch_shapes=[pltpu.VMEM((HEADS, HEAD_W, HEAD_W), F32),
                        pltpu.VMEM((HEADS * step // block, block, HEAD_W), F32)],
        name="hgrn2",
    )(hq3, lf3, kf3, gi3, sg3, norm_w, tril, lv)


META_E0, META_E1, META_R0, META_R1, META_W0, META_W1 = range(6)
GROUP_LANE0 = N_EXPERTS


OUT_PARTS = 4


def _route(logits, lane):
    far = jnp.int32(LANES)

    def first_max(vals):
        mx = jnp.max(vals, axis=-1, keepdims=True)
        return mx, jnp.min(jnp.where(vals == mx, lane, far), axis=-1, keepdims=True)

    is_g = (lane >= GROUP_LANE0) & (lane < GROUP_LANE0 + N_GROUPS)
    gmax, glane = first_max(jnp.where(is_g, logits, NEG_BIG))
    g_w = 1.0 / jnp.sum(jnp.where(is_g, jnp.exp(logits - gmax), 0.0), axis=-1, keepdims=True)
    gidx = glane - GROUP_LANE0
    in_grp = (lane < N_EXPERTS) & ((lane // EXPERTS_PER_GROUP) == gidx)
    el = jnp.where(in_grp, logits, NEG_BIG)
    m1, i1 = first_max(el)
    m2, i2 = first_max(jnp.where(lane == i1, NEG_BIG, el))
    r = jnp.exp(m2 - m1)
    return i1, i2, g_w / (1.0 + r), g_w * r / (1.0 + r)


def _outproj_body(da_ref, hg_ref, x_ref, mod_ref, wo_ref, nw_ref, wr_ref, br_ref, stril_ref,
                  x1_ref, h2_ref, meta_ref, route_ref, cnt_ref, carry_scr):
    @pl.when(pl.program_id(0) == 0)
    def _():
        carry_scr[...] = jnp.zeros(carry_scr.shape, F32)

    rows = stril_ref.shape[0]
    parts = [slice(p * rows, (p + 1) * rows) for p in range(OUT_PARTS)]
    gate1 = mod_ref[0, 2:3, :]
    shift2 = mod_ref[0, 3:4, :]
    scale2 = mod_ref[0, 4:5, :]

    attn = [jnp.dot(da_ref[r, :], wo_ref[0:STREAM_W, :], preferred_element_type=F32)
            + jnp.dot(hg_ref[r, :], wo_ref[STREAM_W:, :], preferred_element_type=F32) for r in parts]

    h2 = []
    for r, a in zip(parts, attn):
        x1 = x_ref[r, :] + gate1 * a
        x1_ref[r, :] = x1
        h2.append(x1 * lax.rsqrt(jnp.mean(x1 * x1, axis=-1, keepdims=True) + EPS) * nw_ref[...]
                  * (1.0 + scale2) + shift2)

    logits = []
    for p, h in enumerate(h2):
        _slab_store(h2_ref.at[pl.ds(p * rows * SLAB, rows * SLAB)], h)
        h_hi = h.astype(BF16)
        h_lo = (h - h_hi.astype(F32)).astype(BF16)
        terms = jnp.dot(jnp.concatenate([h_hi, h_lo], axis=0), wr_ref[...],
                        preferred_element_type=F32)
        logits.append((terms[:rows, :LANES] + terms[:rows, LANES:])
                      + (terms[rows:, :LANES] + terms[rows:, LANES:]) + br_ref[...])

    lane = lax.broadcasted_iota(jnp.int32, (rows, LANES), 1)
    routed = [_route(lg, lane) for lg in logits]

    carry = carry_scr[...]
    for r, (i1, i2, w0, w1) in zip(parts, routed):
        hot0 = lane == i1
        hot1 = lane == i2
        multi = jnp.where(hot0 | hot1, 1.0, 0.0)
        before = jnp.dot(stril_ref[...], multi.astype(BF16), preferred_element_type=F32) + carry
        rank0 = jnp.sum(jnp.where(hot0, before, 0.0), axis=-1, keepdims=True)
        rank1 = jnp.sum(jnp.where(hot1, before, 0.0), axis=-1, keepdims=True)
        carry = carry + jnp.sum(multi, axis=0, keepdims=True)
        meta = jnp.zeros((rows, LANES), F32)
        for idx, val in ((META_E0, i1.astype(F32)), (META_E1, i2.astype(F32)),
                         (META_R0, rank0), (META_R1, rank1), (META_W0, w0), (META_W1, w1)):
            meta = jnp.where(lane == idx, val, meta)
        meta_ref[r, :] = meta
        route_ref[:, r] = meta.T[0:SUBLANES, :]
    carry_scr[...] = carry
    cnt_ref[...] = carry


def _outproj(da2, hg2, x2, mod3, wo_bf, norm_w, w_route, b_route, seq):
    t, d = x2.shape
    tm = OUT_TILE
    per_b = seq // tm
    row = lambda i: (i, 0)
    full = lambda i: (0, 0)
    part = tm // OUT_PARTS
    stril = jnp.asarray(np.tril(np.ones((part, part), np.float32), -1), BF16)
    return pl.pallas_call(
        _outproj_body,
        grid=(t // tm,),
        in_specs=[pl.BlockSpec((tm, STREAM_W), row),
                  pl.BlockSpec((tm, STREAM_W), row),
                  pl.BlockSpec((tm, d), row),
                  pl.BlockSpec((1, 6, d), lambda i: (i // per_b, 0, 0)),
                  pl.BlockSpec((2 * STREAM_W, d), full),
                  pl.BlockSpec((1, d), full),
                  pl.BlockSpec((d, 2 * LANES), full),
                  pl.BlockSpec((1, LANES), full),
                  pl.BlockSpec((part, part), full)],
        out_specs=[pl.BlockSpec((tm, d), row),
                   pl.BlockSpec((tm * SLAB, LANES), row),
                   pl.BlockSpec((tm, LANES), row),
                   pl.BlockSpec((SUBLANES, tm), lambda i: (0, i)),
                   pl.BlockSpec((1, LANES), full)],
        out_shape=[jax.ShapeDtypeStruct((t, d), F32),
                   jax.ShapeDtypeStruct((t * SLAB, LANES), F32),
                   jax.ShapeDtypeStruct((t, LANES), F32),
                   jax.ShapeDtypeStruct((SUBLANES, t), F32),
                   jax.ShapeDtypeStruct((1, LANES), F32)],
        scratch_shapes=[pltpu.VMEM((1, LANES), F32)],
        compiler_params=pltpu.CompilerParams(dimension_semantics=("arbitrary",)),
        name="outproj_route",
    )(da2, hg2, x2, mod3, wo_bf, norm_w, w_route, b_route, stril)


DMA_UNROLL = 8


ZERO_CHUNK = 128
N_FILL_RANGES = N_EXPERTS + 1


def _scatter_body(fill_ref, pos_ref, h2_ref, xs_hbm, zero_scr, sem, zsem, *, tile):
    rows = tile * SLAB

    @pl.when(pl.program_id(0) == 0)
    def _():
        zero_scr[...] = jnp.zeros(zero_scr.shape, F32)

        def clear(chunk):
            start = pl.multiple_of(chunk * (ZERO_CHUNK * SLAB), SLAB)
            return pltpu.make_async_copy(zero_scr, xs_hbm.at[pl.ds(start, ZERO_CHUNK * SLAB)], zsem)

        def start_one(chunk, carry):
            clear(chunk).start()
            return carry

        def wait_one(chunk, carry):
            clear(chunk).wait()
            return carry

        for fn in (start_one, wait_one):
            for r in range(N_FILL_RANGES):
                lax.fori_loop(fill_ref[r], fill_ref[N_FILL_RANGES + r], fn, 0)

    def issue(t, carry):
        src = h2_ref.at[pl.ds(pl.multiple_of(t * SLAB, SLAB), SLAB)]
        for j in range(2):
            slot = pl.multiple_of(pos_ref[0, 0, j * tile + t] * SLAB, SLAB)
            pltpu.make_async_copy(src, xs_hbm.at[pl.ds(slot, SLAB)], sem).start(priority=j)
        return carry

    lax.fori_loop(0, tile, issue, 0, unroll=DMA_UNROLL)
    whole = pltpu.make_async_copy(h2_ref, xs_hbm.at[pl.ds(0, rows)], sem)
    whole.wait()
    whole.wait()


def _scatter(fill_ranges, pos3, h2s, n_slots):
    tile = GATHER_TILE
    rows = tile * SLAB
    grid_spec = pltpu.PrefetchScalarGridSpec(
        num_scalar_prefetch=1,
        grid=(h2s.shape[0] // rows,),
        in_specs=[pl.BlockSpec((1, 1, 2 * tile), lambda i, lt: (i, 0, 0), memory_space=pltpu.SMEM),
                  pl.BlockSpec((rows, LANES), lambda i, lt: (i, 0))],
        out_specs=pl.BlockSpec(memory_space=pl.ANY),
        scratch_shapes=[pltpu.VMEM((ZERO_CHUNK * SLAB, LANES), F32),
                        pltpu.SemaphoreType.DMA, pltpu.SemaphoreType.DMA],
    )
    return pl.pallas_call(
        functools.partial(_scatter_body, tile=tile),
        grid_spec=grid_spec,
        out_shape=jax.ShapeDtypeStruct((n_slots * SLAB, LANES), F32),
        compiler_params=pltpu.CompilerParams(dimension_semantics=("arbitrary",)),
        name="moe_scatter",
    )(fill_ranges, pos3, h2s)


def _experts_body(te_ref, nv_ref, x_ref, wg_ref, wu_ref, wd_ref, y_ref, wgu_scr, wd_scr, *, tm):
    i = pl.program_id(0)
    e = te_ref[i]
    prev = te_ref[jnp.maximum(i - 1, 0)]

    @pl.when((i == 0) | (e != prev))
    def _():
        wgu_scr[:, 0:EXPERT_FF] = wg_ref[0].astype(BF16)
        wgu_scr[:, EXPERT_FF:] = wu_ref[0].astype(BF16)
        wd_scr[...] = wd_ref[0].astype(BF16)

    @pl.when(i < nv_ref[0])
    def _():
        half = tm // 2
        xr = [x_ref.at[pl.ds(r * half * SLAB, half * SLAB)] for r in range(2)]
        yr = [y_ref.at[pl.ds(r * half * SLAB, half * SLAB)] for r in range(2)]
        xs = [_slab_load(ref, half).astype(BF16) for ref in xr]
        gus = [jnp.dot(x, wgu_scr[...], preferred_element_type=F32) for x in xs]
        acts = [(gu[:, 0:EXPERT_FF] * _sigmoid(gu[:, 0:EXPERT_FF]) * gu[:, EXPERT_FF:]).astype(BF16)
                for gu in gus]
        ys = [jnp.dot(act, wd_scr[...], preferred_element_type=F32) for act in acts]
        for ref, y in zip(yr, ys):
            _slab_store(ref, y)

    @pl.when(i >= nv_ref[0])
    def _():
        y_ref[...] = jnp.zeros(y_ref.shape, F32)


def _experts(tile_expert, n_valid, xs, w_gate, w_up, w_down):
    tm = MOE_TILE
    rows = tm * SLAB
    d = w_gate.shape[1]
    live = lambda i, te, nv: (jnp.minimum(i, nv[0] - 1), 0)
    grid_spec = pltpu.PrefetchScalarGridSpec(
        num_scalar_prefetch=2,
        grid=(xs.shape[0] // rows,),
        in_specs=[pl.BlockSpec((rows, LANES), live),
                  pl.BlockSpec((1, d, EXPERT_FF), lambda i, te, nv: (te[i], 0, 0)),
                  pl.BlockSpec((1, d, EXPERT_FF), lambda i, te, nv: (te[i], 0, 0)),
                  pl.BlockSpec((1, EXPERT_FF, d), lambda i, te, nv: (te[i], 0, 0))],
        out_specs=pl.BlockSpec((rows, LANES), lambda i, te, nv: (i, 0)),
        scratch_shapes=[pltpu.VMEM((d, 2 * EXPERT_FF), BF16),
                        pltpu.VMEM((EXPERT_FF, d), BF16)],
    )
    return pl.pallas_call(
        functools.partial(_experts_body, tm=tm),
        grid_spec=grid_spec,
        out_shape=jax.ShapeDtypeStruct(xs.shape, F32),
        compiler_params=pltpu.CompilerParams(dimension_semantics=("arbitrary",)),
        name="moe_experts",
    )(tile_expert, n_valid, xs, w_gate, w_up, w_down)


def _combine_body(pos_ref, nxt_ref, x1_ref, meta_ref, mod_ref, nw_ref, ys_hbm, o_ref, rows_scr, sems,
                  *, tile):
    i = pl.program_id(0)
    n = pl.num_programs(0)
    rows = tile * SLAB

    def fetch(idx_ref, buf):
        def issue(t, carry):
            dst = pl.ds(pl.multiple_of(t * SLAB, SLAB), SLAB)
            for j in range(2):
                slot = pl.multiple_of(idx_ref[0, 0, j * tile + t] * SLAB, SLAB)
                pltpu.make_async_copy(ys_hbm.at[pl.ds(slot, SLAB)], rows_scr.at[buf, j, dst],
                                      sems.at[buf]).start(priority=j)
            return carry
        lax.fori_loop(0, tile, issue, 0, unroll=DMA_UNROLL)

    @pl.when(i == 0)
    def _():
        fetch(pos_ref, 0)

    for cur in range(2):
        @pl.when((i % 2 == cur) & (i + 1 < n))
        def _():
            fetch(nxt_ref, 1 - cur)

    for cur in range(2):
        @pl.when(i % 2 == cur)
        def _():
            for j in range(2):
                pltpu.make_async_copy(ys_hbm.at[pl.ds(0, rows)], rows_scr.at[cur, j],
                                      sems.at[cur]).wait()
            meta = meta_ref[...]
            w0 = meta[:, META_W0:META_W0 + 1]
            w1 = meta[:, META_W1:META_W1 + 1]
            y = (w0 * _slab_load(rows_scr.at[cur, 0], tile)
                 + w1 * _slab_load(rows_scr.at[cur, 1], tile))
            x2 = x1_ref[...] + mod_ref[0, 5:6, :] * y
            o_ref[...] = (x2 * lax.rsqrt(jnp.mean(x2 * x2, axis=-1, keepdims=True) + EPS)
                          * nw_ref[...])


def _combine(pos3, x1, meta, mod3, norm_w, ys, seq):
    t, d = x1.shape
    tile = GATHER_TILE
    per_b = seq // tile
    n = t // tile
    row = lambda i: (i, 0)
    return pl.pallas_call(
        functools.partial(_combine_body, tile=tile),
        grid=(n,),
        in_specs=[pl.BlockSpec((1, 1, 2 * tile), lambda i: (i, 0, 0), memory_space=pltpu.SMEM),
                  pl.BlockSpec((1, 1, 2 * tile), lambda i: (jnp.minimum(i + 1, n - 1), 0, 0),
                               memory_space=pltpu.SMEM),
                  pl.BlockSpec((tile, d), row),
                  pl.BlockSpec((tile, LANES), row),
                  pl.BlockSpec((1, 6, d), lambda i: (i // per_b, 0, 0)),
                  pl.BlockSpec((1, d), lambda i: (0, 0)),
                  pl.BlockSpec(memory_space=pl.ANY)],
        out_specs=pl.BlockSpec((tile, d), row),
        out_shape=jax.ShapeDtypeStruct((t, d), F32),
        scratch_shapes=[pltpu.VMEM((2, 2, tile * SLAB, LANES), F32), pltpu.SemaphoreType.DMA((2,))],
        compiler_params=pltpu.CompilerParams(dimension_semantics=("arbitrary",)),
        name="moe_combine",
    )(pos3, pos3, x1, meta, mod3, norm_w, ys)


def _rope_constants():
    inv_freq = ROPE_THETA ** (-jnp.arange(ROT_HALF, dtype=F32) / ROT_HALF)
    lane = np.arange(LANES) % DA_QK_DIM
    hit = (lane[None, :] < ROT_DIM) & (lane[None, :] % ROT_HALF == np.arange(ROT_HALF)[:, None])
    spread = np.concatenate([hit, hit], axis=0).astype(np.float32)
    return inv_freq.reshape(ROT_HALF, 1), jnp.asarray(spread, BF16)


def kernel(x, c, positions, norm1_w, norm2_w, final_norm_w, ada_w, ada_b, w_in, w_out, da_lambda_q1, da_lambda_k1, da_lambda_q2, da_lambda_k2, da_subln_w, hg_lower_bound, hg_norm_w, moe_w_group, moe_b_group, moe_w_router, moe_b_router, moe_w_gate, moe_w_up, moe_w_down):
    bsz, seq, d = x.shape
    assert d == D_MODEL and norm1_w.shape[0] == 1, "single-layer model of width 1024 only"
    assert seq % ATTN_TILE == 0 and seq % HGRN_STEP == 0 and seq % OUT_TILE == 0
    t = bsz * seq
    x2 = x.reshape(t, d)

    mod3 = _adaln(c, ada_w[0], ada_b).reshape(bsz, 6, d)

    pos_rows = positions.astype(F32).reshape(t // ROW_TILE, 1, ROW_TILE)
    inv_freq, spread = _rope_constants()
    qt, k, vt, hq, lf, kf, gi, sg = _inproj(x2, mod3, norm1_w, w_in[0], pos_rows,
                                          inv_freq, spread, hg_lower_bound, seq)

    as3 = lambda a: a.reshape(bsz, seq, STREAM_W)
    lam_p = jnp.concatenate([da_lambda_q1, da_lambda_k1, da_lambda_q2, da_lambda_k2], axis=0)
    da = _attn(qt, as3(k), vt, lam_p, da_subln_w.reshape(HEAD_W, 1))
    hg = _hgrn(as3(hq), as3(lf), as3(kf), as3(gi), as3(sg), hg_norm_w)

    pad = jnp.zeros((d, LANES - N_EXPERTS - N_GROUPS), F32)
    w_route = jnp.concatenate([moe_w_router[0], moe_w_group[0], pad], axis=1)
    b_route = jnp.concatenate([moe_b_router[0], moe_b_group[0], pad[0]]).reshape(1, LANES)
    w_route_hi = w_route.astype(BF16)
    w_route_lo = (w_route - w_route_hi.astype(F32)).astype(BF16)
    x1, h2, meta, route, cnt = _outproj(da.reshape(t, STREAM_W), hg.reshape(t, STREAM_W), x2, mod3,
                                 w_out[0].astype(BF16), norm2_w,
                                 jnp.concatenate([w_route_hi, w_route_lo], axis=1), b_route, seq)

    counts = cnt[0, :N_EXPERTS].astype(jnp.int32)
    tiles_e = (counts + MOE_TILE - 1) // MOE_TILE
    tile_end = jnp.cumsum(tiles_e)
    offs = (tile_end - tiles_e) * MOE_TILE
    ids = route[META_E0:META_E1 + 1].astype(jnp.int32)
    ranks = route[META_R0:META_R1 + 1].astype(jnp.int32)
    expert_iota = jnp.arange(N_EXPERTS, dtype=jnp.int32)[:, None, None]
    pos = jnp.sum(jnp.where(ids[None] == expert_iota, offs[:, None, None], 0), axis=0) + ranks
    n_tiles = (2 * t) // MOE_TILE + N_EXPERTS
    n_valid = tile_end[-1:]
    tile_ids = jnp.minimum(jnp.arange(n_tiles, dtype=jnp.int32), n_valid - 1)
    tile_expert = jnp.sum(tile_ids[:, None] >= tile_end[None, :], axis=1).astype(jnp.int32)
    pos3 = (pos.reshape(2, t // GATHER_TILE, GATHER_TILE).transpose(1, 0, 2)
            .reshape(t // GATHER_TILE, 1, 2 * GATHER_TILE))
    fill_lo = jnp.concatenate([(offs + counts) // ZERO_CHUNK, n_valid * (MOE_TILE // ZERO_CHUNK)])
    fill_hi = jnp.concatenate([tile_end * (MOE_TILE // ZERO_CHUNK),
                               jnp.full((1,), n_tiles * (MOE_TILE // ZERO_CHUNK), jnp.int32)])
    fill_ranges = jnp.concatenate([fill_lo, fill_hi]).astype(jnp.int32)

    xs = _scatter(fill_ranges, pos3, h2, n_tiles * MOE_TILE)
    ys = _experts(tile_expert, n_valid.astype(jnp.int32), xs, moe_w_gate[0], moe_w_up[0], moe_w_down[0])
    out = _combine(pos3, x1, meta, mod3, final_norm_w.reshape(1, d), ys, seq)
    return out.reshape(bsz, seq, d)
```

```python
import functools
import math

import numpy as np
import jax
import jax.numpy as jnp
from jax import lax
from jax.experimental import pallas as pl
from jax.experimental.pallas import tpu as pltpu

F32 = jnp.float32
BF16 = jnp.bfloat16
HIGHEST = lax.Precision.HIGHEST

LANES = 128
SUBLANES = 8
D_MODEL = 1024
HEADS = 4
HEAD_W = 128
STREAM_W = HEADS * HEAD_W
N_STREAMS = 7
DA_QK_DIM = 64
ROPE_THETA = 500000.0
ROT_DIM = DA_QK_DIM // 4
ROT_HALF = ROT_DIM // 2
N_GROUPS = 4
EXPERTS_PER_GROUP = 8
N_EXPERTS = N_GROUPS * EXPERTS_PER_GROUP
EXPERT_FF = 256
EPS = 1e-6
LAM_INIT = 0.8 - 0.6 * math.exp(-0.3 * 0)
NEG_BIG = -1e30

ROW_TILE = 512
OUT_TILE = 512
ATTN_TILE = 1024
ATTN_KEY_TILE = 256
HGRN_BLOCK = 128
HGRN_STEP = 256
MOE_TILE = 512
EXPERT_PARTS = 2
GATHER_TILE = 256

NT_DIMS = (((1,), (1,)), ((), ()))
TN_DIMS = (((0,), (0,)), ((), ()))


def _sigmoid(x):
    return 1.0 / (1.0 + jnp.exp(-x))


SLAB = D_MODEL // LANES


def _slab_store(ref, val):
    rows = val.shape[0]
    for c in range(SLAB):
        ref[pl.ds(c, rows, stride=SLAB), :] = val[:, c * LANES:(c + 1) * LANES]


def _slab_load(ref, rows):
    return jnp.concatenate([ref[pl.ds(c, rows, stride=SLAB), :] for c in range(SLAB)], axis=1)


def _adaln_body(c_ref, w_ref, b_ref, o_ref):
    c = c_ref[...]
    ca = c * _sigmoid(c)
    o_ref[...] = jnp.dot(ca, w_ref[...], preferred_element_type=F32, precision=HIGHEST) + b_ref[...]


def _adaln(c, ada_w, ada_b):
    bsz, d = c.shape
    n = ada_w.shape[1]
    tn = 1024
    return pl.pallas_call(
        _adaln_body,
        grid=(n // tn,),
        in_specs=[pl.BlockSpec((bsz, d), lambda j: (0, 0)),
                  pl.BlockSpec((d, tn), lambda j: (0, j)),
                  pl.BlockSpec((1, tn), lambda j: (0, j))],
        out_specs=pl.BlockSpec((bsz, tn), lambda j: (0, j)),
        out_shape=jax.ShapeDtypeStruct((bsz, n), F32),
        name="adaln",
    )(c, ada_w, ada_b)


def _inproj_body(x_ref, mod_ref, nw_ref, w32_ref, pos_ref, invf_ref, spread_ref, lbr_ref,
                 q_ref, k_ref, v_ref, hq_ref, lf_ref, kf_ref, gi_ref, sg_ref, w_ref):
    @pl.when(pl.program_id(0) == 0)
    def _():
        for j in range(N_STREAMS):
            cols = slice(j * STREAM_W, (j + 1) * STREAM_W)
            w_ref[:, cols] = w32_ref[:, cols].astype(BF16)

    x = x_ref[...]
    ms = jnp.mean(x * x, axis=-1, keepdims=True)
    y = x * lax.rsqrt(ms + EPS) * nw_ref[...]
    shift = mod_ref[0, 0:1, :]
    scale = mod_ref[0, 1:2, :]
    h = (y * (1.0 + scale) + shift).astype(BF16)

    ang_t = invf_ref[...] * pos_ref[0]

    def spread(table_t):
        hi = table_t.astype(BF16)
        lo = (table_t - hi.astype(F32)).astype(BF16)
        return lax.dot_general(jnp.concatenate([hi, lo], axis=0), spread_ref[...], TN_DIMS,
                               preferred_element_type=F32)

    lane = lax.broadcasted_iota(jnp.int32, (1, LANES), 1) % DA_QK_DIM
    cosv = spread(jnp.cos(ang_t)) + jnp.where(lane < ROT_DIM, 0.0, 1.0)
    sinv = spread(jnp.sin(ang_t))
    sin_lo = jnp.where(lane < ROT_HALF, -sinv, 0.0)
    sin_hi = jnp.where((lane >= ROT_HALF) & (lane < ROT_DIM), sinv, 0.0)

    streams = [jnp.dot(h, w_ref[:, j * STREAM_W:(j + 1) * STREAM_W], preferred_element_type=F32)
               for j in range(N_STREAMS)]

    def proj(j):
        return streams[j]

    def rope(t):
        outs = []
        for hb in range(HEADS):
            tc = t[:, hb * HEAD_W:(hb + 1) * HEAD_W]
            outs.append(tc * cosv
                        + pltpu.roll(tc, LANES - ROT_HALF, 1) * sin_lo
                        + pltpu.roll(tc, ROT_HALF, 1) * sin_hi)
        return jnp.concatenate(outs, axis=1)

    q_ref[0] = (rope(proj(0)) * (DA_QK_DIM ** -0.5 * math.log2(math.e))).T.astype(BF16)
    k_ref[...] = rope(proj(1)).astype(BF16)
    v_ref[0] = proj(2).astype(BF16).T

    gq = proj(3)
    hq_ref[...] = gq * _sigmoid(gq)

    a = lbr_ref[...]
    amax = jnp.max(a, axis=0, keepdims=True)
    ea = jnp.exp(a - amax)
    lb = ea[0:1, :] / jnp.sum(ea, axis=0, keepdims=True)
    gf = proj(4)
    f = lb + (1.0 - lb) * _sigmoid(gf)
    lf_ref[...] = jnp.log(f) * math.log2(math.e)
    kf_ref[...] = 1.0 - f

    gi_ref[...] = proj(5).astype(BF16)
    gg = proj(6)
    sg_ref[...] = (gg * _sigmoid(gg)).astype(BF16)


def _inproj(x2, mod3, norm_w, w_f32, pos_rows, invf, spread, lb_raw, seq):
    t, d = x2.shape
    tm = ROW_TILE
    per_b = seq // tm
    row = lambda i: (i, 0)
    full = lambda i: (0, 0)
    out_bf = jax.ShapeDtypeStruct((t, STREAM_W), BF16)
    out_f = jax.ShapeDtypeStruct((t, STREAM_W), F32)
    out_t = jax.ShapeDtypeStruct((t // seq, STREAM_W, seq), BF16)
    stream = pl.BlockSpec((tm, STREAM_W), row)
    stream_t = pl.BlockSpec((1, STREAM_W, tm), lambda i: (i // per_b, 0, i % per_b))
    return pl.pallas_call(
        _inproj_body,
        grid=(t // tm,),
        in_specs=[pl.BlockSpec((tm, d), row),
                  pl.BlockSpec((1, 6, d), lambda i: (i // per_b, 0, 0)),
                  pl.BlockSpec((1, d), full),
                  pl.BlockSpec((d, N_STREAMS * STREAM_W), full, pipeline_mode=pl.Buffered(1)),
                  pl.BlockSpec((1, 1, tm), lambda i: (i, 0, 0)),
                  pl.BlockSpec((ROT_HALF, 1), full),
                  pl.BlockSpec((2 * ROT_HALF, LANES), full),
                  pl.BlockSpec(lb_raw.shape, full)],
        out_specs=[stream_t, stream, stream_t] + [stream] * 5,
        out_shape=[out_t, out_bf, out_t, out_f, out_f, out_f, out_bf, out_bf],
        scratch_shapes=[pltpu.VMEM((d, N_STREAMS * STREAM_W), BF16)],
        compiler_params=pltpu.CompilerParams(dimension_semantics=("arbitrary",),
                                             vmem_limit_bytes=56 * 1024 * 1024),
        name="inproj",
    )(x2, mod3, norm_w, w_f32, pos_rows, invf, spread, lb_raw)


ONES_ROWS = 16


def _attn_body(qt_ref, k_ref, vt_ref, lam_ref, sw_ref, o_ref, s_scr, m_scr, a_scr, *, tile, ktile):
    qi = pl.program_id(2)
    qt = qt_ref[0]
    feat = lax.broadcasted_iota(jnp.int32, (HEAD_W, 1), 0)
    zero = jnp.zeros_like(qt)
    qmaps = (jnp.where(feat < DA_QK_DIM, qt, zero), jnp.where(feat >= DA_QK_DIM, qt, zero))

    m_scr[...] = jnp.full(m_scr.shape, NEG_BIG, F32)
    a_scr[...] = jnp.zeros(a_scr.shape, F32)

    def score_block(start, nkeys, q_lo, masked):
        kb = k_ref[0, pl.ds(start, nkeys), :]
        scores = [jnp.dot(kb, qmaps[mp][:, q_lo:], preferred_element_type=F32) for mp in range(2)]
        for mp, s in enumerate(scores):
            if masked:
                key = lax.broadcasted_iota(jnp.int32, s.shape, 0)
                qry = lax.broadcasted_iota(jnp.int32, s.shape, 1)
                s = jnp.where(key <= qry, s, NEG_BIG)
            grouped = s.reshape(nkeys // SUBLANES, SUBLANES, tile - q_lo)
            m_scr[mp, :, q_lo:] = jnp.maximum(m_scr[mp, :, q_lo:], jnp.max(grouped, axis=0))
            s_scr[mp, pl.ds(start, nkeys), q_lo:] = s

    def value_block(start, nkeys, q_lo, col_max):
        vb = jnp.concatenate([vt_ref[0, :, pl.ds(start, nkeys)], jnp.ones((ONES_ROWS, nkeys), BF16)],
                             axis=0)
        probs = [jnp.exp2(s_scr[mp, pl.ds(start, nkeys), q_lo:] - col_max[mp][:, q_lo:]).astype(BF16)
                 for mp in range(2)]
        for mp, p in enumerate(probs):
            a_scr[mp, :, q_lo:] += jnp.dot(vb, p, preferred_element_type=F32)

    diag = pl.multiple_of(qi * tile, tile)
    pieces = [(diag + j * ktile, ktile, j * ktile) for j in range(tile // ktile)]

    def score_body(ki, carry):
        score_block(pl.multiple_of(ki * tile, tile), tile, 0, False)
        return carry

    lax.fori_loop(0, qi, score_body, 0)
    for start, nkeys, q_lo in pieces:
        score_block(start, nkeys, q_lo, True)
    col_max = [jnp.max(m_scr[mp], axis=0, keepdims=True) for mp in range(2)]

    def value_body(ki, carry):
        value_block(pl.multiple_of(ki * tile, tile), tile, 0, col_max)
        return carry

    lax.fori_loop(0, qi, value_body, 0)
    for start, nkeys, q_lo in pieces:
        value_block(start, nkeys, q_lo, col_max)

    lp = lam_ref[...]
    lam = (jnp.exp(jnp.sum(lp[0:1] * lp[1:2], axis=-1, keepdims=True))
           - jnp.exp(jnp.sum(lp[2:3] * lp[3:4], axis=-1, keepdims=True)) + LAM_INIT)
    o = (a_scr[0, 0:HEAD_W, :] / a_scr[0, HEAD_W:HEAD_W + 1, :]
         - lam * (a_scr[1, 0:HEAD_W, :] / a_scr[1, HEAD_W:HEAD_W + 1, :]))
    o = o * lax.rsqrt(jnp.mean(o * o, axis=0, keepdims=True) + EPS) * sw_ref[...]
    o_ref[0] = (o * (1.0 - LAM_INIT)).T.astype(BF16)


def _attn(qt3, k3, vt3, lam_p, subln_col):
    bsz, seq, _ = k3.shape
    tile = ATTN_TILE
    return pl.pallas_call(
        functools.partial(_attn_body, tile=tile, ktile=ATTN_KEY_TILE),
        grid=(bsz, HEADS, seq // tile),
        in_specs=[pl.BlockSpec((1, HEAD_W, tile), lambda b, h, i: (b, h, i)),
                  pl.BlockSpec((1, seq, HEAD_W), lambda b, h, i: (b, 0, h)),
                  pl.BlockSpec((1, HEAD_W, seq), lambda b, h, i: (b, h, 0)),
                  pl.BlockSpec(lam_p.shape, lambda b, h, i: (0, 0)),
                  pl.BlockSpec((HEAD_W, 1), lambda b, h, i: (0, 0))],
        out_specs=pl.BlockSpec((1, tile, HEAD_W), lambda b, h, i: (b, i, h)),
        out_shape=jax.ShapeDtypeStruct((bsz, seq, STREAM_W), BF16),
        scratch_shapes=[pltpu.VMEM((2, seq, tile), F32),
                        pltpu.VMEM((2, SUBLANES, tile), F32),
                        pltpu.VMEM((2, HEAD_W + ONES_ROWS, tile), F32)],
        compiler_params=pltpu.CompilerParams(vmem_limit_bytes=56 * 1024 * 1024),
        name="diff_attn",
    )(qt3, k3, vt3, lam_p, subln_col)


def _hgrn_levels(block):
    return [block >> (i + 1) for i in range(block.bit_length() - 1)]


def _hgrn_constants(block):
    t = np.arange(block)[:, None]
    s = np.arange(block)[None, :]
    tril = (s <= t).astype(np.float32)
    lv = np.full((block, block), -1, np.int32)
    halves = _hgrn_levels(block)
    for li, m in enumerate(halves):
        same = (t // (2 * m)) == (s // (2 * m))
        lv[same & ((t & m) != 0) & ((s & m) == 0)] = li
    lv[np.arange(block), np.arange(block)] = len(halves)
    return jnp.asarray(tril, BF16), jnp.asarray(lv)


def _level_operand(b_ref, h, b, q, k, m, block):
    def ref_rows(r, n):
        return jnp.broadcast_to(b_ref[h, pl.ds(r, 1), :], (n, HEAD_W))

    if m >= SUBLANES:
        pieces = []
        for s0 in range(0, block, 2 * m):
            ref = ref_rows(s0 + m - 1, m)
            lo = slice(s0, s0 + m)
            up = slice(s0 + m, s0 + 2 * m)
            pieces.append(k[lo] * jnp.exp2(ref - b[lo]))
            pieces.append(q[up] * jnp.exp2(b[up] - ref))
        return jnp.concatenate(pieces, axis=0)

    sub = lax.broadcasted_iota(jnp.int32, (SUBLANES, HEAD_W), 0)
    refs = []
    for s0 in range(0, block, SUBLANES):
        piece = ref_rows(s0 + m - 1, SUBLANES)
        for j in range(1, SUBLANES // (2 * m)):
            piece = jnp.where(sub >= 2 * m * j, ref_rows(s0 + 2 * m * j + m - 1, SUBLANES), piece)
        refs.append(piece)
    d = b - jnp.concatenate(refs, axis=0)
    row = lax.broadcasted_iota(jnp.int32, (block, 1), 0)
    return jnp.where((row & m) != 0, q, k) * jnp.exp2(jnp.minimum(d, -d))


def _hgrn_body(hq_ref, lf_ref, kf_ref, gi_ref, sg_ref, nw_ref, tril_ref, lv_ref, o_ref,
               st_scr, b_scr, *, block, step):
    @pl.when(pl.program_id(1) == 0)
    def _():
        st_scr[...] = jnp.zeros(st_scr.shape, F32)

    tril = tril_ref[...]
    lv = lv_ref[...]
    halves = _hgrn_levels(block)

    units = [(u, r0, h) for u, (r0, h) in enumerate(
        (r0, h) for r0 in range(0, step, block) for h in range(HEADS))]

    def cols(h):
        return slice(h * HEAD_W, (h + 1) * HEAD_W)

    q, k, v, b, scores = {}, {}, {}, {}, {}
    for u, r0, h in units:
        rows = slice(r0, r0 + block)
        q[u] = hq_ref[0, rows, cols(h)]
        k[u] = kf_ref[0, rows, cols(h)]
        v[u] = gi_ref[0, rows, cols(h)]
        lf = lf_ref[0, rows, cols(h)]
        hi = lf.astype(BF16)
        r1 = lf - hi.astype(F32)
        mid = r1.astype(BF16)
        lo = (r1 - mid.astype(F32)).astype(BF16)
        b[u] = (jnp.dot(tril, hi, preferred_element_type=F32)
                + jnp.dot(tril, mid, preferred_element_type=F32)
                + jnp.dot(tril, lo, preferred_element_type=F32))
        b_scr[u] = b[u]
        scores[u] = jnp.where(lv == len(halves),
                              lax.dot_general(q[u].astype(BF16), k[u].astype(BF16), NT_DIMS,
                                              preferred_element_type=F32), 0.0)

    for li, m in enumerate(halves):
        for u, r0, h in units:
            xl = _level_operand(b_scr, u, b[u], q[u], k[u], m, block).astype(BF16)
            p = lax.dot_general(xl, xl, NT_DIMS, preferred_element_type=F32)
            scores[u] = jnp.where(lv == li, p, scores[u])

    for u, r0, h in units:
        rows = slice(r0, r0 + block)
        o_intra = jnp.dot(scores[u].astype(BF16), v[u], preferred_element_type=F32)
        st = st_scr[h]
        o_inter = lax.dot_general((q[u] * jnp.exp2(b[u])).astype(BF16), st.astype(BF16), NT_DIMS,
                                  preferred_element_type=F32)
        b_last = b[u][block - 1:block, :]
        kdec = (k[u] * jnp.exp2(b_last - b[u])).astype(BF16)
        st_scr[h] = st * jnp.exp2(b_last) + lax.dot_general(v[u], kdec, TN_DIMS,
                                                             preferred_element_type=F32)
        o = o_inter + o_intra
        o = o * lax.rsqrt(jnp.mean(o * o, axis=-1, keepdims=True) + EPS) * nw_ref[...]
        o_ref[0, rows, cols(h)] = (o * sg_ref[0, rows, cols(h)].astype(F32)).astype(BF16)


def _hgrn(hq3, lf3, kf3, gi3, sg3, norm_w):
    bsz, seq, _ = hq3.shape
    block = HGRN_BLOCK
    step = HGRN_STEP
    tril, lv = _hgrn_constants(block)
    blk = pl.BlockSpec((1, step, STREAM_W), lambda b, g: (b, g, 0))
    const = lambda b, g: (0, 0)
    return pl.pallas_call(
        functools.partial(_hgrn_body, block=block, step=step),
        grid=(bsz, seq // step),
        in_specs=[blk, blk, blk, blk, blk,
                  pl.BlockSpec((1, HEAD_W), const),
                  pl.BlockSpec((block, block), const),
                  pl.BlockSpec((block, block), const)],
        out_specs=blk,
        out_shape=jax.ShapeDtypeStruct((bsz, seq, STREAM_W), BF16),
        scratch_shapes=[pltpu.VMEM((HEADS, HEAD_W, HEAD_W), F32),
                        pltpu.VMEM((HEADS * step // block, block, HEAD_W), F32)],
        name="hgrn2",
    )(hq3, lf3, kf3, gi3, sg3, norm_w, tril, lv)


META_E0, META_E1, META_R0, META_R1, META_W0, META_W1 = range(6)
GROUP_LANE0 = N_EXPERTS


OUT_PARTS = 4


def _route(logits, lane):
    far = jnp.int32(LANES)

    def first_max(vals):
        mx = jnp.max(vals, axis=-1, keepdims=True)
        return mx, jnp.min(jnp.where(vals == mx, lane, far), axis=-1, keepdims=True)

    is_g = (lane >= GROUP_LANE0) & (lane < GROUP_LANE0 + N_GROUPS)
    gmax, glane = first_max(jnp.where(is_g, logits, NEG_BIG))
    g_w = 1.0 / jnp.sum(jnp.where(is_g, jnp.exp(logits - gmax), 0.0), axis=-1, keepdims=True)
    gidx = glane - GROUP_LANE0
    in_grp = (lane < N_EXPERTS) & ((lane // EXPERTS_PER_GROUP) == gidx)
    el = jnp.where(in_grp, logits, NEG_BIG)
    m1, i1 = first_max(el)
    m2, i2 = first_max(jnp.where(lane == i1, NEG_BIG, el))
    r = jnp.exp(m2 - m1)
    return i1, i2, g_w / (1.0 + r), g_w * r / (1.0 + r)


def _outproj_body(da_ref, hg_ref, x_ref, mod_ref, wo_ref, nw_ref, wr_ref, br_ref, stril_ref,
                  x1_ref, h2_ref, meta_ref, route_ref, cnt_ref, carry_scr):
    @pl.when(pl.program_id(0) == 0)
    def _():
        carry_scr[...] = jnp.zeros(carry_scr.shape, F32)

    rows = stril_ref.shape[0]
    parts = [slice(p * rows, (p + 1) * rows) for p in range(OUT_PARTS)]
    gate1 = mod_ref[0, 2:3, :]
    shift2 = mod_ref[0, 3:4, :]
    scale2 = mod_ref[0, 4:5, :]

    attn = [jnp.dot(da_ref[r, :], wo_ref[0:STREAM_W, :], preferred_element_type=F32)
            + jnp.dot(hg_ref[r, :], wo_ref[STREAM_W:, :], preferred_element_type=F32) for r in parts]

    h2 = []
    for r, a in zip(parts, attn):
        x1 = x_ref[r, :] + gate1 * a
        x1_ref[r, :] = x1
        h2.append(x1 * lax.rsqrt(jnp.mean(x1 * x1, axis=-1, keepdims=True) + EPS) * nw_ref[...]
                  * (1.0 + scale2) + shift2)

    logits = []
    for p, h in enumerate(h2):
        _slab_store(h2_ref.at[pl.ds(p * rows * SLAB, rows * SLAB)], h)
        h_hi = h.astype(BF16)
        h_lo = (h - h_hi.astype(F32)).astype(BF16)
        terms = jnp.dot(jnp.concatenate([h_hi, h_lo], axis=0), wr_ref[...],
                        preferred_element_type=F32)
        logits.append((terms[:rows, :LANES] + terms[:rows, LANES:])
                      + (terms[rows:, :LANES] + terms[rows:, LANES:]) + br_ref[...])

    lane = lax.broadcasted_iota(jnp.int32, (rows, LANES), 1)
    routed = [_route(lg, lane) for lg in logits]

    carry = carry_scr[...]
    for r, (i1, i2, w0, w1) in zip(parts, routed):
        hot0 = lane == i1
        hot1 = lane == i2
        multi = jnp.where(hot0 | hot1, 1.0, 0.0)
        before = jnp.dot(stril_ref[...], multi.astype(BF16), preferred_element_type=F32) + carry
        rank0 = jnp.sum(jnp.where(hot0, before, 0.0), axis=-1, keepdims=True)
        rank1 = jnp.sum(jnp.where(hot1, before, 0.0), axis=-1, keepdims=True)
        carry = carry + jnp.sum(multi, axis=0, keepdims=True)
        meta = jnp.zeros((rows, LANES), F32)
        for idx, val in ((META_E0, i1.astype(F32)), (META_E1, i2.astype(F32)),
                         (META_R0, rank0), (META_R1, rank1), (META_W0, w0), (META_W1, w1)):
            meta = jnp.where(lane == idx, val, meta)
        meta_ref[r, :] = meta
        route_ref[:, r] = meta.T[0:SUBLANES, :]
    carry_scr[...] = carry
    cnt_ref[...] = carry


def _outproj(da2, hg2, x2, mod3, wo_bf, norm_w, w_route, b_route, seq):
    t, d = x2.shape
    tm = OUT_TILE
    per_b = seq // tm
    row = lambda i: (i, 0)
    full = lambda i: (0, 0)
    part = tm // OUT_PARTS
    stril = jnp.asarray(np.tril(np.ones((part, part), np.float32), -1), BF16)
    return pl.pallas_call(
        _outproj_body,
        grid=(t // tm,),
        in_specs=[pl.BlockSpec((tm, STREAM_W), row),
                  pl.BlockSpec((tm, STREAM_W), row),
                  pl.BlockSpec((tm, d), row),
                  pl.BlockSpec((1, 6, d), lambda i: (i // per_b, 0, 0)),
                  pl.BlockSpec((2 * STREAM_W, d), full),
                  pl.BlockSpec((1, d), full),
                  pl.BlockSpec((d, 2 * LANES), full),
                  pl.BlockSpec((1, LANES), full),
                  pl.BlockSpec((part, part), full)],
        out_specs=[pl.BlockSpec((tm, d), row),
                   pl.BlockSpec((tm * SLAB, LANES), row),
                   pl.BlockSpec((tm, LANES), row),
                   pl.BlockSpec((SUBLANES, tm), lambda i: (0, i)),
                   pl.BlockSpec((1, LANES), full)],
        out_shape=[jax.ShapeDtypeStruct((t, d), F32),
                   jax.ShapeDtypeStruct((t * SLAB, LANES), F32),
                   jax.ShapeDtypeStruct((t, LANES), F32),
                   jax.ShapeDtypeStruct((SUBLANES, t), F32),
                   jax.ShapeDtypeStruct((1, LANES), F32)],
        scratch_shapes=[pltpu.VMEM((1, LANES), F32)],
        compiler_params=pltpu.CompilerParams(dimension_semantics=("arbitrary",)),
        name="outproj_route",
    )(da2, hg2, x2, mod3, wo_bf, norm_w, w_route, b_route, stril)


DMA_UNROLL = 8


ZERO_CHUNK = 128
N_FILL_RANGES = N_EXPERTS + 1


def _scatter_body(fill_ref, pos_ref, h2_ref, xs_hbm, zero_scr, sem, zsem, *, tile):
    rows = tile * SLAB

    @pl.when(pl.program_id(0) == 0)
    def _():
        zero_scr[...] = jnp.zeros(zero_scr.shape, F32)

        def clear(chunk):
            start = pl.multiple_of(chunk * (ZERO_CHUNK * SLAB), SLAB)
            return pltpu.make_async_copy(zero_scr, xs_hbm.at[pl.ds(start, ZERO_CHUNK * SLAB)], zsem)

        def start_one(chunk, carry):
            clear(chunk).start()
            return carry

        def wait_one(chunk, carry):
            clear(chunk).wait()
            return carry

        for fn in (start_one, wait_one):
            for r in range(N_FILL_RANGES):
                lax.fori_loop(fill_ref[r], fill_ref[N_FILL_RANGES + r], fn, 0)

    def issue(t, carry):
        src = h2_ref.at[pl.ds(pl.multiple_of(t * SLAB, SLAB), SLAB)]
        for j in range(2):
            slot = pl.multiple_of(pos_ref[0, 0, j * tile + t] * SLAB, SLAB)
            pltpu.make_async_copy(src, xs_hbm.at[pl.ds(slot, SLAB)], sem).start(priority=j)
        return carry

    lax.fori_loop(0, tile, issue, 0, unroll=DMA_UNROLL)
    whole = pltpu.make_async_copy(h2_ref, xs_hbm.at[pl.ds(0, rows)], sem)
    whole.wait()
    whole.wait()


def _scatter(fill_ranges, pos3, h2s, n_slots):
    tile = GATHER_TILE
    rows = tile * SLAB
    grid_spec = pltpu.PrefetchScalarGridSpec(
        num_scalar_prefetch=1,
        grid=(h2s.shape[0] // rows,),
        in_specs=[pl.BlockSpec((1, 1, 2 * tile), lambda i, lt: (i, 0, 0), memory_space=pltpu.SMEM),
                  pl.BlockSpec((rows, LANES), lambda i, lt: (i, 0))],
        out_specs=pl.BlockSpec(memory_space=pl.ANY),
        scratch_shapes=[pltpu.VMEM((ZERO_CHUNK * SLAB, LANES), F32),
                        pltpu.SemaphoreType.DMA, pltpu.SemaphoreType.DMA],
    )
    return pl.pallas_call(
        functools.partial(_scatter_body, tile=tile),
        grid_spec=grid_spec,
        out_shape=jax.ShapeDtypeStruct((n_slots * SLAB, LANES), F32),
        compiler_params=pltpu.CompilerParams(dimension_semantics=("arbitrary",)),
        name="moe_scatter",
    )(fill_ranges, pos3, h2s)


def _experts_body(te_ref, nv_ref, x_ref, wg_ref, wu_ref, wd_ref, y_ref, wgu_scr, wd_scr, *, tm):
    i = pl.program_id(0)
    e = te_ref[i]
    prev = te_ref[jnp.maximum(i - 1, 0)]

    @pl.when((i == 0) | (e != prev))
    def _():
        wgu_scr[:, 0:EXPERT_FF] = wg_ref[0].astype(BF16)
        wgu_scr[:, EXPERT_FF:] = wu_ref[0].astype(BF16)
        wd_scr[...] = wd_ref[0].astype(BF16)

    @pl.when(i < nv_ref[0])
    def _():
        part = tm // EXPERT_PARTS
        xr = [x_ref.at[pl.ds(r * part * SLAB, part * SLAB)] for r in range(EXPERT_PARTS)]
        yr = [y_ref.at[pl.ds(r * part * SLAB, part * SLAB)] for r in range(EXPERT_PARTS)]
        xs = [_slab_load(ref, part).astype(BF16) for ref in xr]
        gus = [jnp.dot(x, wgu_scr[...], preferred_element_type=F32) for x in xs]
        acts = [(gu[:, 0:EXPERT_FF] * _sigmoid(gu[:, 0:EXPERT_FF]) * gu[:, EXPERT_FF:]).astype(BF16)
                for gu in gus]
        ys = [jnp.dot(act, wd_scr[...], preferred_element_type=F32) for act in acts]
        for ref, y in zip(yr, ys):
            _slab_store(ref, y)

    @pl.when(i >= nv_ref[0])
    def _():
        y_ref[...] = jnp.zeros(y_ref.shape, F32)


def _experts(tile_expert, n_valid, xs, w_gate, w_up, w_down):
    tm = MOE_TILE
    rows = tm * SLAB
    d = w_gate.shape[1]
    live = lambda i, te, nv: (jnp.minimum(i, nv[0] - 1), 0)
    grid_spec = pltpu.PrefetchScalarGridSpec(
        num_scalar_prefetch=2,
        grid=(xs.shape[0] // rows,),
        in_specs=[pl.BlockSpec((rows, LANES), live),
                  pl.BlockSpec((1, d, EXPERT_FF), lambda i, te, nv: (te[i], 0, 0)),
                  pl.BlockSpec((1, d, EXPERT_FF), lambda i, te, nv: (te[i], 0, 0)),
                  pl.BlockSpec((1, EXPERT_FF, d), lambda i, te, nv: (te[i], 0, 0))],
        out_specs=pl.BlockSpec((rows, LANES), lambda i, te, nv: (i, 0)),
        scratch_shapes=[pltpu.VMEM((d, 2 * EXPERT_FF), BF16),
                        pltpu.VMEM((EXPERT_FF, d), BF16)],
    )
    return pl.pallas_call(
        functools.partial(_experts_body, tm=tm),
        grid_spec=grid_spec,
        out_shape=jax.ShapeDtypeStruct(xs.shape, F32),
        compiler_params=pltpu.CompilerParams(dimension_semantics=("arbitrary",)),
        name="moe_experts",
    )(tile_expert, n_valid, xs, w_gate, w_up, w_down)


def _combine_body(pos_ref, nxt_ref, x1_ref, meta_ref, mod_ref, nw_ref, ys_hbm, o_ref, rows_scr, sems,
                  *, tile):
    i = pl.program_id(0)
    n = pl.num_programs(0)
    rows = tile * SLAB

    def fetch(idx_ref, buf):
        def issue(t, carry):
            dst = pl.ds(pl.multiple_of(t * SLAB, SLAB), SLAB)
            for j in range(2):
                slot = pl.multiple_of(idx_ref[0, 0, j * tile + t] * SLAB, SLAB)
                pltpu.make_async_copy(ys_hbm.at[pl.ds(slot, SLAB)], rows_scr.at[buf, j, dst],
                                      sems.at[buf]).start(priority=j)
            return carry
        lax.fori_loop(0, tile, issue, 0, unroll=DMA_UNROLL)

    @pl.when(i == 0)
    def _():
        fetch(pos_ref, 0)

    for cur in range(2):
        @pl.when((i % 2 == cur) & (i + 1 < n))
        def _():
            fetch(nxt_ref, 1 - cur)

    for cur in range(2):
        @pl.when(i % 2 == cur)
        def _():
            for j in range(2):
                pltpu.make_async_copy(ys_hbm.at[pl.ds(0, rows)], rows_scr.at[cur, j],
                                      sems.at[cur]).wait()
            meta = meta_ref[...]
            w0 = meta[:, META_W0:META_W0 + 1]
            w1 = meta[:, META_W1:META_W1 + 1]
            y = (w0 * _slab_load(rows_scr.at[cur, 0], tile)
                 + w1 * _slab_load(rows_scr.at[cur, 1], tile))
            x2 = x1_ref[...] + mod_ref[0, 5:6, :] * y
            o_ref[...] = (x2 * lax.rsqrt(jnp.mean(x2 * x2, axis=-1, keepdims=True) + EPS)
                          * nw_ref[...])


def _combine(pos3, x1, meta, mod3, norm_w, ys, seq):
    t, d = x1.shape
    tile = GATHER_TILE
    per_b = seq // tile
    n = t // tile
    row = lambda i: (i, 0)
    return pl.pallas_call(
        functools.partial(_combine_body, tile=tile),
        grid=(n,),
        in_specs=[pl.BlockSpec((1, 1, 2 * tile), lambda i: (i, 0, 0), memory_space=pltpu.SMEM),
                  pl.BlockSpec((1, 1, 2 * tile), lambda i: (jnp.minimum(i + 1, n - 1), 0, 0),
                               memory_space=pltpu.SMEM),
                  pl.BlockSpec((tile, d), row),
                  pl.BlockSpec((tile, LANES), row),
                  pl.BlockSpec((1, 6, d), lambda i: (i // per_b, 0, 0)),
                  pl.BlockSpec((1, d), lambda i: (0, 0)),
                  pl.BlockSpec(memory_space=pl.ANY)],
        out_specs=pl.BlockSpec((tile, d), row),
        out_shape=jax.ShapeDtypeStruct((t, d), F32),
        scratch_shapes=[pltpu.VMEM((2, 2, tile * SLAB, LANES), F32), pltpu.SemaphoreType.DMA((2,))],
        compiler_params=pltpu.CompilerParams(dimension_semantics=("arbitrary",)),
        name="moe_combine",
    )(pos3, pos3, x1, meta, mod3, norm_w, ys)


def _rope_constants():
    inv_freq = ROPE_THETA ** (-jnp.arange(ROT_HALF, dtype=F32) / ROT_HALF)
    lane = np.arange(LANES) % DA_QK_DIM
    hit = (lane[None, :] < ROT_DIM) & (lane[None, :] % ROT_HALF == np.arange(ROT_HALF)[:, None])
    spread = np.concatenate([hit, hit], axis=0).astype(np.float32)
    return inv_freq.reshape(ROT_HALF, 1), jnp.asarray(spread, BF16)


def kernel(x, c, positions, norm1_w, norm2_w, final_norm_w, ada_w, ada_b, w_in, w_out, da_lambda_q1, da_lambda_k1, da_lambda_q2, da_lambda_k2, da_subln_w, hg_lower_bound, hg_norm_w, moe_w_group, moe_b_group, moe_w_router, moe_b_router, moe_w_gate, moe_w_up, moe_w_down):
    bsz, seq, d = x.shape
    assert d == D_MODEL and norm1_w.shape[0] == 1, "single-layer model of width 1024 only"
    assert seq % ATTN_TILE == 0 and seq % HGRN_STEP == 0 and seq % OUT_TILE == 0
    t = bsz * seq
    x2 = x.reshape(t, d)

    mod3 = _adaln(c, ada_w[0], ada_b).reshape(bsz, 6, d)

    pos_rows = positions.astype(F32).reshape(t // ROW_TILE, 1, ROW_TILE)
    inv_freq, spread = _rope_constants()
    qt, k, vt, hq, lf, kf, gi, sg = _inproj(x2, mod3, norm1_w, w_in[0], pos_rows,
                                          inv_freq, spread, hg_lower_bound, seq)

    as3 = lambda a: a.reshape(bsz, seq, STREAM_W)
    lam_p = jnp.concatenate([da_lambda_q1, da_lambda_k1, da_lambda_q2, da_lambda_k2], axis=0)
    da = _attn(qt, as3(k), vt, lam_p, da_subln_w.reshape(HEAD_W, 1))
    hg = _hgrn(as3(hq), as3(lf), as3(kf), as3(gi), as3(sg), hg_norm_w)

    pad = jnp.zeros((d, LANES - N_EXPERTS - N_GROUPS), F32)
    w_route = jnp.concatenate([moe_w_router[0], moe_w_group[0], pad], axis=1)
    b_route = jnp.concatenate([moe_b_router[0], moe_b_group[0], pad[0]]).reshape(1, LANES)
    w_route_hi = w_route.astype(BF16)
    w_route_lo = (w_route - w_route_hi.astype(F32)).astype(BF16)
    x1, h2, meta, route, cnt = _outproj(da.reshape(t, STREAM_W), hg.reshape(t, STREAM_W), x2, mod3,
                                 w_out[0].astype(BF16), norm2_w,
                                 jnp.concatenate([w_route_hi, w_route_lo], axis=1), b_route, seq)

    counts = cnt[0, :N_EXPERTS].astype(jnp.int32)
    tiles_e = (counts + MOE_TILE - 1) // MOE_TILE
    tile_end = jnp.cumsum(tiles_e)
    offs = (tile_end - tiles_e) * MOE_TILE
    ids = route[META_E0:META_E1 + 1].astype(jnp.int32)
    ranks = route[META_R0:META_R1 + 1].astype(jnp.int32)
    expert_iota = jnp.arange(N_EXPERTS, dtype=jnp.int32)[:, None, None]
    pos = jnp.sum(jnp.where(ids[None] == expert_iota, offs[:, None, None], 0), axis=0) + ranks
    n_tiles = (2 * t) // MOE_TILE + N_EXPERTS
    n_valid = tile_end[-1:]
    tile_ids = jnp.minimum(jnp.arange(n_tiles, dtype=jnp.int32), n_valid - 1)
    tile_expert = jnp.sum(tile_ids[:, None] >= tile_end[None, :], axis=1).astype(jnp.int32)
    pos3 = (pos.reshape(2, t // GATHER_TILE, GATHER_TILE).transpose(1, 0, 2)
            .reshape(t // GATHER_TILE, 1, 2 * GATHER_TILE))
    fill_lo = jnp.concatenate([(offs + counts) // ZERO_CHUNK, n_valid * (MOE_TILE // ZERO_CHUNK)])
    fill_hi = jnp.concatenate([tile_end * (MOE_TILE // ZERO_CHUNK),
                               jnp.full((1,), n_tiles * (MOE_TILE // ZERO_CHUNK), jnp.int32)])
    fill_ranges = jnp.concatenate([fill_lo, fill_hi]).astype(jnp.int32)

    xs = _scatter(fill_ranges, pos3, h2, n_tiles * MOE_TILE)
    ys = _experts(tile_expert, n_valid.astype(jnp.int32), xs, moe_w_gate[0], moe_w_up[0], moe_w_down[0])
    out = _combine(pos3, x1, meta, mod3, final_norm_w.reshape(1, d), ys, seq)
    return out.reshape(bsz, seq, d)
```

```python
import functools
import math

import numpy as np
import jax
import jax.numpy as jnp
from jax import lax
from jax.experimental import pallas as pl
from jax.experimental.pallas import tpu as pltpu

F32 = jnp.float32
BF16 = jnp.bfloat16
HIGHEST = lax.Precision.HIGHEST

LANES = 128
SUBLANES = 8
D_MODEL = 1024
HEADS = 4
HEAD_W = 128
STREAM_W = HEADS * HEAD_W
N_STREAMS = 7
DA_QK_DIM = 64
ROPE_THETA = 500000.0
ROT_DIM = DA_QK_DIM // 4
ROT_HALF = ROT_DIM // 2
N_GROUPS = 4
EXPERTS_PER_GROUP = 8
N_EXPERTS = N_GROUPS * EXPERTS_PER_GROUP
EXPERT_FF = 256
EPS = 1e-6
LAM_INIT = 0.8 - 0.6 * math.exp(-0.3 * 0)
NEG_BIG = -1e30

ROW_TILE = 512
OUT_TILE = 512
ATTN_TILE = 1024
ATTN_KEY_TILE = 256
HGRN_BLOCK = 128
HGRN_STEP = 256
MOE_TILE = 512
EXPERT_PARTS = 2
GATHER_TILE = 512

NT_DIMS = (((1,), (1,)), ((), ()))
TN_DIMS = (((0,), (0,)), ((), ()))


def _sigmoid(x):
    return 1.0 / (1.0 + jnp.exp(-x))


SLAB = D_MODEL // LANES


def _slab_store(ref, val):
    rows = val.shape[0]
    for c in range(SLAB):
        ref[pl.ds(c, rows, stride=SLAB), :] = val[:, c * LANES:(c + 1) * LANES]


def _slab_load(ref, rows):
    return jnp.concatenate([ref[pl.ds(c, rows, stride=SLAB), :] for c in range(SLAB)], axis=1)


def _adaln_body(c_ref, w_ref, b_ref, o_ref):
    c = c_ref[...]
    ca = c * _sigmoid(c)
    o_ref[...] = jnp.dot(ca, w_ref[...], preferred_element_type=F32, precision=HIGHEST) + b_ref[...]


def _adaln(c, ada_w, ada_b):
    bsz, d = c.shape
    n = ada_w.shape[1]
    tn = 1024
    return pl.pallas_call(
        _adaln_body,
        grid=(n // tn,),
        in_specs=[pl.BlockSpec((bsz, d), lambda j: (0, 0)),
                  pl.BlockSpec((d, tn), lambda j: (0, j)),
                  pl.BlockSpec((1, tn), lambda j: (0, j))],
        out_specs=pl.BlockSpec((bsz, tn), lambda j: (0, j)),
        out_shape=jax.ShapeDtypeStruct((bsz, n), F32),
        name="adaln",
    )(c, ada_w, ada_b)


def _inproj_body(x_ref, mod_ref, nw_ref, w32_ref, pos_ref, invf_ref, spread_ref, lbr_ref,
                 q_ref, k_ref, v_ref, hq_ref, lf_ref, kf_ref, gi_ref, sg_ref, w_ref):
    @pl.when(pl.program_id(0) == 0)
    def _():
        for j in range(N_STREAMS):
            cols = slice(j * STREAM_W, (j + 1) * STREAM_W)
            w_ref[:, cols] = w32_ref[:, cols].astype(BF16)

    x = x_ref[...]
    ms = jnp.mean(x * x, axis=-1, keepdims=True)
    y = x * lax.rsqrt(ms + EPS) * nw_ref[...]
    shift = mod_ref[0, 0:1, :]
    scale = mod_ref[0, 1:2, :]
    h = (y * (1.0 + scale) + shift).astype(BF16)

    ang_t = invf_ref[...] * pos_ref[0]

    def spread(table_t):
        hi = table_t.astype(BF16)
        lo = (table_t - hi.astype(F32)).astype(BF16)
        return lax.dot_general(jnp.concatenate([hi, lo], axis=0), spread_ref[...], TN_DIMS,
                               preferred_element_type=F32)

    lane = lax.broadcasted_iota(jnp.int32, (1, LANES), 1) % DA_QK_DIM
    cosv = spread(jnp.cos(ang_t)) + jnp.where(lane < ROT_DIM, 0.0, 1.0)
    sinv = spread(jnp.sin(ang_t))
    sin_lo = jnp.where(lane < ROT_HALF, -sinv, 0.0)
    sin_hi = jnp.where((lane >= ROT_HALF) & (lane < ROT_DIM), sinv, 0.0)

    streams = [jnp.dot(h, w_ref[:, j * STREAM_W:(j + 1) * STREAM_W], preferred_element_type=F32)
               for j in range(N_STREAMS)]

    def proj(j):
        return streams[j]

    def rope(t):
        outs = []
        for hb in range(HEADS):
            tc = t[:, hb * HEAD_W:(hb + 1) * HEAD_W]
            outs.append(tc * cosv
                        + pltpu.roll(tc, LANES - ROT_HALF, 1) * sin_lo
                        + pltpu.roll(tc, ROT_HALF, 1) * sin_hi)
        return jnp.concatenate(outs, axis=1)

    q_ref[0] = (rope(proj(0)) * (DA_QK_DIM ** -0.5 * math.log2(math.e))).T.astype(BF16)
    k_ref[...] = rope(proj(1)).astype(BF16)
    v_ref[0] = proj(2).astype(BF16).T

    gq = proj(3)
    hq_ref[...] = gq * _sigmoid(gq)

    a = lbr_ref[...]
    amax = jnp.max(a, axis=0, keepdims=True)
    ea = jnp.exp(a - amax)
    lb = ea[0:1, :] / jnp.sum(ea, axis=0, keepdims=True)
    gf = proj(4)
    f = lb + (1.0 - lb) * _sigmoid(gf)
    lf_ref[...] = jnp.log(f) * math.log2(math.e)
    kf_ref[...] = 1.0 - f

    gi_ref[...] = proj(5).astype(BF16)
    gg = proj(6)
    sg_ref[...] = (gg * _sigmoid(gg)).astype(BF16)


def _inproj(x2, mod3, norm_w, w_f32, pos_rows, invf, spread, lb_raw, seq):
    t, d = x2.shape
    tm = ROW_TILE
    per_b = seq // tm
    row = lambda i: (i, 0)
    full = lambda i: (0, 0)
    out_bf = jax.ShapeDtypeStruct((t, STREAM_W), BF16)
    out_f = jax.ShapeDtypeStruct((t, STREAM_W), F32)
    out_t = jax.ShapeDtypeStruct((t // seq, STREAM_W, seq), BF16)
    stream = pl.BlockSpec((tm, STREAM_W), row)
    stream_t = pl.BlockSpec((1, STREAM_W, tm), lambda i: (i // per_b, 0, i % per_b))
    return pl.pallas_call(
        _inproj_body,
        grid=(t // tm,),
        in_specs=[pl.BlockSpec((tm, d), row),
                  pl.BlockSpec((1, 6, d), lambda i: (i // per_b, 0, 0)),
                  pl.BlockSpec((1, d), full),
                  pl.BlockSpec((d, N_STREAMS * STREAM_W), full, pipeline_mode=pl.Buffered(1)),
                  pl.BlockSpec((1, 1, tm), lambda i: (i, 0, 0)),
                  pl.BlockSpec((ROT_HALF, 1), full),
                  pl.BlockSpec((2 * ROT_HALF, LANES), full),
                  pl.BlockSpec(lb_raw.shape, full)],
        out_specs=[stream_t, stream, stream_t] + [stream] * 5,
        out_shape=[out_t, out_bf, out_t, out_f, out_f, out_f, out_bf, out_bf],
        scratch_shapes=[pltpu.VMEM((d, N_STREAMS * STREAM_W), BF16)],
        compiler_params=pltpu.CompilerParams(dimension_semantics=("arbitrary",),
                                             vmem_limit_bytes=56 * 1024 * 1024),
        name="inproj",
    )(x2, mod3, norm_w, w_f32, pos_rows, invf, spread, lb_raw)


ONES_ROWS = 16


def _attn_body(qt_ref, k_ref, vt_ref, lam_ref, sw_ref, o_ref, s_scr, m_scr, a_scr, *, tile, ktile):
    qi = pl.program_id(2)
    qt = qt_ref[0]
    feat = lax.broadcasted_iota(jnp.int32, (HEAD_W, 1), 0)
    zero = jnp.zeros_like(qt)
    qmaps = (jnp.where(feat < DA_QK_DIM, qt, zero), jnp.where(feat >= DA_QK_DIM, qt, zero))

    m_scr[...] = jnp.full(m_scr.shape, NEG_BIG, F32)
    a_scr[...] = jnp.zeros(a_scr.shape, F32)

    def score_block(start, nkeys, q_lo, masked):
        kb = k_ref[0, pl.ds(start, nkeys), :]
        scores = [jnp.dot(kb, qmaps[mp][:, q_lo:], preferred_element_type=F32) for mp in range(2)]
        for mp, s in enumerate(scores):
            if masked:
                key = lax.broadcasted_iota(jnp.int32, s.shape, 0)
                qry = lax.broadcasted_iota(jnp.int32, s.shape, 1)
                s = jnp.where(key <= qry, s, NEG_BIG)
            grouped = s.reshape(nkeys // SUBLANES, SUBLANES, tile - q_lo)
            m_scr[mp, :, q_lo:] = jnp.maximum(m_scr[mp, :, q_lo:], jnp.max(grouped, axis=0))
            s_scr[mp, pl.ds(start, nkeys), q_lo:] = s

    def value_block(start, nkeys, q_lo, col_max):
        vb = jnp.concatenate([vt_ref[0, :, pl.ds(start, nkeys)], jnp.ones((ONES_ROWS, nkeys), BF16)],
                             axis=0)
        probs = [jnp.exp2(s_scr[mp, pl.ds(start, nkeys), q_lo:] - col_max[mp][:, q_lo:]).astype(BF16)
                 for mp in range(2)]
        for mp, p in enumerate(probs):
            a_scr[mp, :, q_lo:] += jnp.dot(vb, p, preferred_element_type=F32)

    diag = pl.multiple_of(qi * tile, tile)
    pieces = [(diag + j * ktile, ktile, j * ktile) for j in range(tile // ktile)]

    def score_body(ki, carry):
        score_block(pl.multiple_of(ki * tile, tile), tile, 0, False)
        return carry

    lax.fori_loop(0, qi, score_body, 0)
    for start, nkeys, q_lo in pieces:
        score_block(start, nkeys, q_lo, True)
    col_max = [jnp.max(m_scr[mp], axis=0, keepdims=True) for mp in range(2)]

    def value_body(ki, carry):
        value_block(pl.multiple_of(ki * tile, tile), tile, 0, col_max)
        return carry

    lax.fori_loop(0, qi, value_body, 0)
    for start, nkeys, q_lo in pieces:
        value_block(start, nkeys, q_lo, col_max)

    lp = lam_ref[...]
    lam = (jnp.exp(jnp.sum(lp[0:1] * lp[1:2], axis=-1, keepdims=True))
           - jnp.exp(jnp.sum(lp[2:3] * lp[3:4], axis=-1, keepdims=True)) + LAM_INIT)
    o = (a_scr[0, 0:HEAD_W, :] / a_scr[0, HEAD_W:HEAD_W + 1, :]
         - lam * (a_scr[1, 0:HEAD_W, :] / a_scr[1, HEAD_W:HEAD_W + 1, :]))
    o = o * lax.rsqrt(jnp.mean(o * o, axis=0, keepdims=True) + EPS) * sw_ref[...]
    o_ref[0] = (o * (1.0 - LAM_INIT)).T.astype(BF16)


def _attn(qt3, k3, vt3, lam_p, subln_col):
    bsz, seq, _ = k3.shape
    tile = ATTN_TILE
    return pl.pallas_call(
        functools.partial(_attn_body, tile=tile, ktile=ATTN_KEY_TILE),
        grid=(bsz, HEADS, seq // tile),
        in_specs=[pl.BlockSpec((1, HEAD_W, tile), lambda b, h, i: (b, h, i)),
                  pl.BlockSpec((1, seq, HEAD_W), lambda b, h, i: (b, 0, h)),
                  pl.BlockSpec((1, HEAD_W, seq), lambda b, h, i: (b, h, 0)),
                  pl.BlockSpec(lam_p.shape, lambda b, h, i: (0, 0)),
                  pl.BlockSpec((HEAD_W, 1), lambda b, h, i: (0, 0))],
        out_specs=pl.BlockSpec((1, tile, HEAD_W), lambda b, h, i: (b, i, h)),
        out_shape=jax.ShapeDtypeStruct((bsz, seq, STREAM_W), BF16),
        scratch_shapes=[pltpu.VMEM((2, seq, tile), F32),
                        pltpu.VMEM((2, SUBLANES, tile), F32),
                        pltpu.VMEM((2, HEAD_W + ONES_ROWS, tile), F32)],
        compiler_params=pltpu.CompilerParams(vmem_limit_bytes=56 * 1024 * 1024),
        name="diff_attn",
    )(qt3, k3, vt3, lam_p, subln_col)


def _hgrn_levels(block):
    return [block >> (i + 1) for i in range(block.bit_length() - 1)]


def _hgrn_constants(block):
    t = np.arange(block)[:, None]
    s = np.arange(block)[None, :]
    tril = (s <= t).astype(np.float32)
    lv = np.full((block, block), -1, np.int32)
    halves = _hgrn_levels(block)
    for li, m in enumerate(halves):
        same = (t // (2 * m)) == (s // (2 * m))
        lv[same & ((t & m) != 0) & ((s & m) == 0)] = li
    lv[np.arange(block), np.arange(block)] = len(halves)
    return jnp.asarray(tril, BF16), jnp.asarray(lv)


def _level_operand(b_ref, h, b, q, k, m, block):
    def ref_rows(r, n):
        return jnp.broadcast_to(b_ref[h, pl.ds(r, 1), :], (n, HEAD_W))

    if m >= SUBLANES:
        pieces = []
        for s0 in range(0, block, 2 * m):
            ref = ref_rows(s0 + m - 1, m)
            lo = slice(s0, s0 + m)
            up = slice(s0 + m, s0 + 2 * m)
            pieces.append(k[lo] * jnp.exp2(ref - b[lo]))
            pieces.append(q[up] * jnp.exp2(b[up] - ref))
        return jnp.concatenate(pieces, axis=0)

    sub = lax.broadcasted_iota(jnp.int32, (SUBLANES, HEAD_W), 0)
    refs = []
    for s0 in range(0, block, SUBLANES):
        piece = ref_rows(s0 + m - 1, SUBLANES)
        for j in range(1, SUBLANES // (2 * m)):
            piece = jnp.where(sub >= 2 * m * j, ref_rows(s0 + 2 * m * j + m - 1, SUBLANES), piece)
        refs.append(piece)
    d = b - jnp.concatenate(refs, axis=0)
    row = lax.broadcasted_iota(jnp.int32, (block, 1), 0)
    return jnp.where((row & m) != 0, q, k) * jnp.exp2(jnp.minimum(d, -d))


def _hgrn_body(hq_ref, lf_ref, kf_ref, gi_ref, sg_ref, nw_ref, tril_ref, lv_ref, o_ref,
               st_scr, b_scr, *, block, step):
    @pl.when(pl.program_id(1) == 0)
    def _():
        st_scr[...] = jnp.zeros(st_scr.shape, F32)

    tril = tril_ref[...]
    lv = lv_ref[...]
    halves = _hgrn_levels(block)

    units = [(u, r0, h) for u, (r0, h) in enumerate(
        (r0, h) for r0 in range(0, step, block) for h in range(HEADS))]

    def cols(h):
        return slice(h * HEAD_W, (h + 1) * HEAD_W)

    q, k, v, b, scores = {}, {}, {}, {}, {}
    for u, r0, h in units:
        rows = slice(r0, r0 + block)
        q[u] = hq_ref[0, rows, cols(h)]
        k[u] = kf_ref[0, rows, cols(h)]
        v[u] = gi_ref[0, rows, cols(h)]
        lf = lf_ref[0, rows, cols(h)]
        hi = lf.astype(BF16)
        r1 = lf - hi.astype(F32)
        mid = r1.astype(BF16)
        lo = (r1 - mid.astype(F32)).astype(BF16)
        b[u] = (jnp.dot(tril, hi, preferred_element_type=F32)
                + jnp.dot(tril, mid, preferred_element_type=F32)
                + jnp.dot(tril, lo, preferred_element_type=F32))
        b_scr[u] = b[u]
        scores[u] = jnp.where(lv == len(halves),
                              lax.dot_general(q[u].astype(BF16), k[u].astype(BF16), NT_DIMS,
                                              preferred_element_type=F32), 0.0)

    for li, m in enumerate(halves):
        for u, r0, h in units:
            xl = _level_operand(b_scr, u, b[u], q[u], k[u], m, block).astype(BF16)
            p = lax.dot_general(xl, xl, NT_DIMS, preferred_element_type=F32)
            scores[u] = jnp.where(lv == li, p, scores[u])

    for u, r0, h in units:
        rows = slice(r0, r0 + block)
        o_intra = jnp.dot(scores[u].astype(BF16), v[u], preferred_element_type=F32)
        st = st_scr[h]
        o_inter = lax.dot_general((q[u] * jnp.exp2(b[u])).astype(BF16), st.astype(BF16), NT_DIMS,
                                  preferred_element_type=F32)
        b_last = b[u][block - 1:block, :]
        kdec = (k[u] * jnp.exp2(b_last - b[u])).astype(BF16)
        st_scr[h] = st * jnp.exp2(b_last) + lax.dot_general(v[u], kdec, TN_DIMS,
                                                             preferred_element_type=F32)
        o = o_inter + o_intra
        o = o * lax.rsqrt(jnp.mean(o * o, axis=-1, keepdims=True) + EPS) * nw_ref[...]
        o_ref[0, rows, cols(h)] = (o * sg_ref[0, rows, cols(h)].astype(F32)).astype(BF16)


def _hgrn(hq3, lf3, kf3, gi3, sg3, norm_w):
    bsz, seq, _ = hq3.shape
    block = HGRN_BLOCK
    step = HGRN_STEP
    tril, lv = _hgrn_constants(block)
    blk = pl.BlockSpec((1, step, STREAM_W), lambda b, g: (b, g, 0))
    const = lambda b, g: (0, 0)
    return pl.pallas_call(
        functools.partial(_hgrn_body, block=block, step=step),
        grid=(bsz, seq // step),
        in_specs=[blk, blk, blk, blk, blk,
                  pl.BlockSpec((1, HEAD_W), const),
                  pl.BlockSpec((block, block), const),
                  pl.BlockSpec((block, block), const)],
        out_specs=blk,
        out_shape=jax.ShapeDtypeStruct((bsz, seq, STREAM_W), BF16),
        scratch_shapes=[pltpu.VMEM((HEADS, HEAD_W, HEAD_W), F32),
                        pltpu.VMEM((HEADS * step // block, block, HEAD_W), F32)],
        name="hgrn2",
    )(hq3, lf3, kf3, gi3, sg3, norm_w, tril, lv)


META_E0, META_E1, META_R0, META_R1, META_W0, META_W1 = range(6)
GROUP_LANE0 = N_EXPERTS


OUT_PARTS = 4


def _route(logits, lane):
    far = jnp.int32(LANES)

    def first_max(vals):
        mx = jnp.max(vals, axis=-1, keepdims=True)
        return mx, jnp.min(jnp.where(vals == mx, lane, far), axis=-1, keepdims=True)

    is_g = (lane >= GROUP_LANE0) & (lane < GROUP_LANE0 + N_GROUPS)
    gmax, glane = first_max(jnp.where(is_g, logits, NEG_BIG))
    g_w = 1.0 / jnp.sum(jnp.where(is_g, jnp.exp(logits - gmax), 0.0), axis=-1, keepdims=True)
    gidx = glane - GROUP_LANE0
    in_grp = (lane < N_EXPERTS) & ((lane // EXPERTS_PER_GROUP) == gidx)
    el = jnp.where(in_grp, logits, NEG_BIG)
    m1, i1 = first_max(el)
    m2, i2 = first_max(jnp.where(lane == i1, NEG_BIG, el))
    r = jnp.exp(m2 - m1)
    return i1, i2, g_w / (1.0 + r), g_w * r / (1.0 + r)


def _outproj_body(da_ref, hg_ref, x_ref, mod_ref, wo_ref, nw_ref, wr_ref, br_ref, stril_ref,
                  x1_ref, h2_ref, meta_ref, route_ref, cnt_ref, carry_scr):
    @pl.when(pl.program_id(0) == 0)
    def _():
        carry_scr[...] = jnp.zeros(carry_scr.shape, F32)

    rows = stril_ref.shape[0]
    parts = [slice(p * rows, (p + 1) * rows) for p in range(OUT_PARTS)]
    gate1 = mod_ref[0, 2:3, :]
    shift2 = mod_ref[0, 3:4, :]
    scale2 = mod_ref[0, 4:5, :]

    attn = [jnp.dot(da_ref[r, :], wo_ref[0:STREAM_W, :], preferred_element_type=F32)
            + jnp.dot(hg_ref[r, :], wo_ref[STREAM_W:, :], preferred_element_type=F32) for r in parts]

    h2 = []
    for r, a in zip(parts, attn):
        x1 = x_ref[r, :] + gate1 * a
        x1_ref[r, :] = x1
        h2.append(x1 * lax.rsqrt(jnp.mean(x1 * x1, axis=-1, keepdims=True) + EPS) * nw_ref[...]
                  * (1.0 + scale2) + shift2)

    logits = []
    for p, h in enumerate(h2):
        _slab_store(h2_ref.at[pl.ds(p * rows * SLAB, rows * SLAB)], h)
        h_hi = h.astype(BF16)
        h_lo = (h - h_hi.astype(F32)).astype(BF16)
        terms = jnp.dot(jnp.concatenate([h_hi, h_lo], axis=0), wr_ref[...],
                        preferred_element_type=F32)
        logits.append((terms[:rows, :LANES] + terms[:rows, LANES:])
                      + (terms[rows:, :LANES] + terms[rows:, LANES:]) + br_ref[...])

    lane = lax.broadcasted_iota(jnp.int32, (rows, LANES), 1)
    routed = [_route(lg, lane) for lg in logits]

    carry = carry_scr[...]
    for r, (i1, i2, w0, w1) in zip(parts, routed):
        hot0 = lane == i1
        hot1 = lane == i2
        multi = jnp.where(hot0 | hot1, 1.0, 0.0)
        before = jnp.dot(stril_ref[...], multi.astype(BF16), preferred_element_type=F32) + carry
        rank0 = jnp.sum(jnp.where(hot0, before, 0.0), axis=-1, keepdims=True)
        rank1 = jnp.sum(jnp.where(hot1, before, 0.0), axis=-1, keepdims=True)
        carry = carry + jnp.sum(multi, axis=0, keepdims=True)
        meta = jnp.zeros((rows, LANES), F32)
        for idx, val in ((META_E0, i1.astype(F32)), (META_E1, i2.astype(F32)),
                         (META_R0, rank0), (META_R1, rank1), (META_W0, w0), (META_W1, w1)):
            meta = jnp.where(lane == idx, val, meta)
        meta_ref[r, :] = meta
        route_ref[:, r] = meta.T[0:SUBLANES, :]
    carry_scr[...] = carry
    cnt_ref[...] = carry


def _outproj(da2, hg2, x2, mod3, wo_bf, norm_w, w_route, b_route, seq):
    t, d = x2.shape
    tm = OUT_TILE
    per_b = seq // tm
    row = lambda i: (i, 0)
    full = lambda i: (0, 0)
    part = tm // OUT_PARTS
    stril = jnp.asarray(np.tril(np.ones((part, part), np.float32), -1), BF16)
    return pl.pallas_call(
        _outproj_body,
        grid=(t // tm,),
        in_specs=[pl.BlockSpec((tm, STREAM_W), row),
                  pl.BlockSpec((tm, STREAM_W), row),
                  pl.BlockSpec((tm, d), row),
                  pl.BlockSpec((1, 6, d), lambda i: (i // per_b, 0, 0)),
                  pl.BlockSpec((2 * STREAM_W, d), full),
                  pl.BlockSpec((1, d), full),
                  pl.BlockSpec((d, 2 * LANES), full),
                  pl.BlockSpec((1, LANES), full),
                  pl.BlockSpec((part, part), full)],
        out_specs=[pl.BlockSpec((tm, d), row),
                   pl.BlockSpec((tm * SLAB, LANES), row),
                   pl.BlockSpec((tm, LANES), row),
                   pl.BlockSpec((SUBLANES, tm), lambda i: (0, i)),
                   pl.BlockSpec((1, LANES), full)],
        out_shape=[jax.ShapeDtypeStruct((t, d), F32),
                   jax.ShapeDtypeStruct((t * SLAB, LANES), F32),
                   jax.ShapeDtypeStruct((t, LANES), F32),
                   jax.ShapeDtypeStruct((SUBLANES, t), F32),
                   jax.ShapeDtypeStruct((1, LANES), F32)],
        scratch_shapes=[pltpu.VMEM((1, LANES), F32)],
        compiler_params=pltpu.CompilerParams(dimension_semantics=("arbitrary",)),
        name="outproj_route",
    )(da2, hg2, x2, mod3, wo_bf, norm_w, w_route, b_route, stril)


DMA_UNROLL = 8


ZERO_CHUNK = 128
N_FILL_RANGES = N_EXPERTS + 1


def _scatter_body(fill_ref, pos_ref, h2_ref, xs_hbm, zero_scr, sem, zsem, *, tile):
    rows = tile * SLAB

    @pl.when(pl.program_id(0) == 0)
    def _():
        zero_scr[...] = jnp.zeros(zero_scr.shape, F32)

        def clear(chunk):
            start = pl.multiple_of(chunk * (ZERO_CHUNK * SLAB), SLAB)
            return pltpu.make_async_copy(zero_scr, xs_hbm.at[pl.ds(start, ZERO_CHUNK * SLAB)], zsem)

        def start_one(chunk, carry):
            clear(chunk).start()
            return carry

        def wait_one(chunk, carry):
            clear(chunk).wait()
            return carry

        for fn in (start_one, wait_one):
            for r in range(N_FILL_RANGES):
                lax.fori_loop(fill_ref[r], fill_ref[N_FILL_RANGES + r], fn, 0)

    def issue(t, carry):
        src = h2_ref.at[pl.ds(pl.multiple_of(t * SLAB, SLAB), SLAB)]
        for j in range(2):
            slot = pl.multiple_of(pos_ref[0, 0, j * tile + t] * SLAB, SLAB)
            pltpu.make_async_copy(src, xs_hbm.at[pl.ds(slot, SLAB)], sem).start(priority=j)
        return carry

    lax.fori_loop(0, tile, issue, 0, unroll=DMA_UNROLL)
    whole = pltpu.make_async_copy(h2_ref, xs_hbm.at[pl.ds(0, rows)], sem)
    whole.wait()
    whole.wait()


def _scatter(fill_ranges, pos3, h2s, n_slots):
    tile = GATHER_TILE
    rows = tile * SLAB
    grid_spec = pltpu.PrefetchScalarGridSpec(
        num_scalar_prefetch=1,
        grid=(h2s.shape[0] // rows,),
        in_specs=[pl.BlockSpec((1, 1, 2 * tile), lambda i, lt: (i, 0, 0), memory_space=pltpu.SMEM),
                  pl.BlockSpec((rows, LANES), lambda i, lt: (i, 0))],
        out_specs=pl.BlockSpec(memory_space=pl.ANY),
        scratch_shapes=[pltpu.VMEM((ZERO_CHUNK * SLAB, LANES), F32),
                        pltpu.SemaphoreType.DMA, pltpu.SemaphoreType.DMA],
    )
    return pl.pallas_call(
        functools.partial(_scatter_body, tile=tile),
        grid_spec=grid_spec,
        out_shape=jax.ShapeDtypeStruct((n_slots * SLAB, LANES), F32),
        compiler_params=pltpu.CompilerParams(dimension_semantics=("arbitrary",)),
        name="moe_scatter",
    )(fill_ranges, pos3, h2s)


def _experts_body(te_ref, nv_ref, x_ref, wg_ref, wu_ref, wd_ref, y_ref, wgu_scr, wd_scr, *, tm):
    i = pl.program_id(0)
    e = te_ref[i]
    prev = te_ref[jnp.maximum(i - 1, 0)]

    @pl.when((i == 0) | (e != prev))
    def _():
        wgu_scr[:, 0:EXPERT_FF] = wg_ref[0].astype(BF16)
        wgu_scr[:, EXPERT_FF:] = wu_ref[0].astype(BF16)
        wd_scr[...] = wd_ref[0].astype(BF16)

    @pl.when(i < nv_ref[0])
    def _():
        part = tm // EXPERT_PARTS
        xr = [x_ref.at[pl.ds(r * part * SLAB, part * SLAB)] for r in range(EXPERT_PARTS)]
        yr = [y_ref.at[pl.ds(r * part * SLAB, part * SLAB)] for r in range(EXPERT_PARTS)]
        xs = [_slab_load(ref, part).astype(BF16) for ref in xr]
        gus = [jnp.dot(x, wgu_scr[...], preferred_element_type=F32) for x in xs]
        acts = [(gu[:, 0:EXPERT_FF] * _sigmoid(gu[:, 0:EXPERT_FF]) * gu[:, EXPERT_FF:]).astype(BF16)
                for gu in gus]
        ys = [jnp.dot(act, wd_scr[...], preferred_element_type=F32) for act in acts]
        for ref, y in zip(yr, ys):
            _slab_store(ref, y)

    @pl.when(i >= nv_ref[0])
    def _():
        y_ref[...] = jnp.zeros(y_ref.shape, F32)


def _experts(tile_expert, n_valid, xs, w_gate, w_up, w_down):
    tm = MOE_TILE
    rows = tm * SLAB
    d = w_gate.shape[1]
    live = lambda i, te, nv: (jnp.minimum(i, nv[0] - 1), 0)
    grid_spec = pltpu.PrefetchScalarGridSpec(
        num_scalar_prefetch=2,
        grid=(xs.shape[0] // rows,),
        in_specs=[pl.BlockSpec((rows, LANES), live),
                  pl.BlockSpec((1, d, EXPERT_FF), lambda i, te, nv: (te[i], 0, 0)),
                  pl.BlockSpec((1, d, EXPERT_FF), lambda i, te, nv: (te[i], 0, 0)),
                  pl.BlockSpec((1, EXPERT_FF, d), lambda i, te, nv: (te[i], 0, 0))],
        out_specs=pl.BlockSpec((rows, LANES), lambda i, te, nv: (i, 0)),
        scratch_shapes=[pltpu.VMEM((d, 2 * EXPERT_FF), BF16),
                        pltpu.VMEM((EXPERT_FF, d), BF16)],
    )
    return pl.pallas_call(
        functools.partial(_experts_body, tm=tm),
        grid_spec=grid_spec,
        out_shape=jax.ShapeDtypeStruct(xs.shape, F32),
        compiler_params=pltpu.CompilerParams(dimension_semantics=("arbitrary",)),
        name="moe_experts",
    )(tile_expert, n_valid, xs, w_gate, w_up, w_down)


def _combine_body(pos_ref, nxt_ref, x1_ref, meta_ref, mod_ref, nw_ref, ys_hbm, o_ref, rows_scr, sems,
                  *, tile):
    i = pl.program_id(0)
    n = pl.num_programs(0)
    rows = tile * SLAB

    def fetch(idx_ref, buf):
        def issue(t, carry):
            dst = pl.ds(pl.multiple_of(t * SLAB, SLAB), SLAB)
            for j in range(2):
                slot = pl.multiple_of(idx_ref[0, 0, j * tile + t] * SLAB, SLAB)
                pltpu.make_async_copy(ys_hbm.at[pl.ds(slot, SLAB)], rows_scr.at[buf, j, dst],
                                      sems.at[buf]).start(priority=j)
            return carry
        lax.fori_loop(0, tile, issue, 0, unroll=DMA_UNROLL)

    @pl.when(i == 0)
    def _():
        fetch(pos_ref, 0)

    for cur in range(2):
        @pl.when((i % 2 == cur) & (i + 1 < n))
        def _():
            fetch(nxt_ref, 1 - cur)

    for cur in range(2):
        @pl.when(i % 2 == cur)
        def _():
            for j in range(2):
                pltpu.make_async_copy(ys_hbm.at[pl.ds(0, rows)], rows_scr.at[cur, j],
                                      sems.at[cur]).wait()
            meta = meta_ref[...]
            w0 = meta[:, META_W0:META_W0 + 1]
            w1 = meta[:, META_W1:META_W1 + 1]
            y = (w0 * _slab_load(rows_scr.at[cur, 0], tile)
                 + w1 * _slab_load(rows_scr.at[cur, 1], tile))
            x2 = x1_ref[...] + mod_ref[0, 5:6, :] * y
            o_ref[...] = (x2 * lax.rsqrt(jnp.mean(x2 * x2, axis=-1, keepdims=True) + EPS)
                          * nw_ref[...])


def _combine(pos3, x1, meta, mod3, norm_w, ys, seq):
    t, d = x1.shape
    tile = GATHER_TILE
    per_b = seq // tile
    n = t // tile
    row = lambda i: (i, 0)
    return pl.pallas_call(
        functools.partial(_combine_body, tile=tile),
        grid=(n,),
        in_specs=[pl.BlockSpec((1, 1, 2 * tile), lambda i: (i, 0, 0), memory_space=pltpu.SMEM),
                  pl.BlockSpec((1, 1, 2 * tile), lambda i: (jnp.minimum(i + 1, n - 1), 0, 0),
                               memory_space=pltpu.SMEM),
                  pl.BlockSpec((tile, d), row),
                  pl.BlockSpec((tile, LANES), row),
                  pl.BlockSpec((1, 6, d), lambda i: (i // per_b, 0, 0)),
                  pl.BlockSpec((1, d), lambda i: (0, 0)),
                  pl.BlockSpec(memory_space=pl.ANY)],
        out_specs=pl.BlockSpec((tile, d), row),
        out_shape=jax.ShapeDtypeStruct((t, d), F32),
        scratch_shapes=[pltpu.VMEM((2, 2, tile * SLAB, LANES), F32), pltpu.SemaphoreType.DMA((2,))],
        compiler_params=pltpu.CompilerParams(dimension_semantics=("arbitrary",)),
        name="moe_combine",
    )(pos3, pos3, x1, meta, mod3, norm_w, ys)


def _rope_constants():
    inv_freq = ROPE_THETA ** (-jnp.arange(ROT_HALF, dtype=F32) / ROT_HALF)
    lane = np.arange(LANES) % DA_QK_DIM
    hit = (lane[None, :] < ROT_DIM) & (lane[None, :] % ROT_HALF == np.arange(ROT_HALF)[:, None])
    spread = np.concatenate([hit, hit], axis=0).astype(np.float32)
    return inv_freq.reshape(ROT_HALF, 1), jnp.asarray(spread, BF16)


def kernel(x, c, positions, norm1_w, norm2_w, final_norm_w, ada_w, ada_b, w_in, w_out, da_lambda_q1, da_lambda_k1, da_lambda_q2, da_lambda_k2, da_subln_w, hg_lower_bound, hg_norm_w, moe_w_group, moe_b_group, moe_w_router, moe_b_router, moe_w_gate, moe_w_up, moe_w_down):
    bsz, seq, d = x.shape
    assert d == D_MODEL and norm1_w.shape[0] == 1, "single-layer model of width 1024 only"
    assert seq % ATTN_TILE == 0 and seq % HGRN_STEP == 0 and seq % OUT_TILE == 0
    t = bsz * seq
    x2 = x.reshape(t, d)

    mod3 = _adaln(c, ada_w[0], ada_b).reshape(bsz, 6, d)

    pos_rows = positions.astype(F32).reshape(t // ROW_TILE, 1, ROW_TILE)
    inv_freq, spread = _rope_constants()
    qt, k, vt, hq, lf, kf, gi, sg = _inproj(x2, mod3, norm1_w, w_in[0], pos_rows,
                                          inv_freq, spread, hg_lower_bound, seq)

    as3 = lambda a: a.reshape(bsz, seq, STREAM_W)
    lam_p = jnp.concatenate([da_lambda_q1, da_lambda_k1, da_lambda_q2, da_lambda_k2], axis=0)
    da = _attn(qt, as3(k), vt, lam_p, da_subln_w.reshape(HEAD_W, 1))
    hg = _hgrn(as3(hq), as3(lf), as3(kf), as3(gi), as3(sg), hg_norm_w)

    pad = jnp.zeros((d, LANES - N_EXPERTS - N_GROUPS), F32)
    w_route = jnp.concatenate([moe_w_router[0], moe_w_group[0], pad], axis=1)
    b_route = jnp.concatenate([moe_b_router[0], moe_b_group[0], pad[0]]).reshape(1, LANES)
    w_route_hi = w_route.astype(BF16)
    w_route_lo = (w_route - w_route_hi.astype(F32)).astype(BF16)
    x1, h2, meta, route, cnt = _outproj(da.reshape(t, STREAM_W), hg.reshape(t, STREAM_W), x2, mod3,
                                 w_out[0].astype(BF16), norm2_w,
                                 jnp.concatenate([w_route_hi, w_route_lo], axis=1), b_route, seq)

    counts = cnt[0, :N_EXPERTS].astype(jnp.int32)
    tiles_e = (counts + MOE_TILE - 1) // MOE_TILE
    tile_end = jnp.cumsum(tiles_e)
    offs = (tile_end - tiles_e) * MOE_TILE
    ids = route[META_E0:META_E1 + 1].astype(jnp.int32)
    ranks = route[META_R0:META_R1 + 1].astype(jnp.int32)
    expert_iota = jnp.arange(N_EXPERTS, dtype=jnp.int32)[:, None, None]
    pos = jnp.sum(jnp.where(ids[None] == expert_iota, offs[:, None, None], 0), axis=0) + ranks
    n_tiles = (2 * t) // MOE_TILE + N_EXPERTS
    n_valid = tile_end[-1:]
    tile_ids = jnp.minimum(jnp.arange(n_tiles, dtype=jnp.int32), n_valid - 1)
    tile_expert = jnp.sum(tile_ids[:, None] >= tile_end[None, :], axis=1).astype(jnp.int32)
    pos3 = (pos.reshape(2, t // GATHER_TILE, GATHER_TILE).transpose(1, 0, 2)
            .reshape(t // GATHER_TILE, 1, 2 * GATHER_TILE))
    fill_lo = jnp.concatenate([(offs + counts) // ZERO_CHUNK, n_valid * (MOE_TILE // ZERO_CHUNK)])
    fill_hi = jnp.concatenate([tile_end * (MOE_TILE // ZERO_CHUNK),
                               jnp.full((1,), n_tiles * (MOE_TILE // ZERO_CHUNK), jnp.int32)])
    fill_ranges = jnp.concatenate([fill_lo, fill_hi]).astype(jnp.int32)

    xs = _scatter(fill_ranges, pos3, h2, n_tiles * MOE_TILE)
    ys = _experts(tile_expert, n_valid.astype(jnp.int32), xs, moe_w_gate[0], moe_w_up[0], moe_w_down[0])
    out = _combine(pos3, x1, meta, mod3, final_norm_w.reshape(1, d), ys, seq)
    return out.reshape(bsz, seq, d)
```

```python
import functools
import math

import numpy as np
import jax
import jax.numpy as jnp
from jax import lax
from jax.experimental import pallas as pl
from jax.experimental.pallas import tpu as pltpu

F32 = jnp.float32
BF16 = jnp.bfloat16
HIGHEST = lax.Precision.HIGHEST

LANES = 128
SUBLANES = 8
D_MODEL = 1024
HEADS = 4
HEAD_W = 128
STREAM_W = HEADS * HEAD_W
N_STREAMS = 7
DA_QK_DIM = 64
ROPE_THETA = 500000.0
ROT_DIM = DA_QK_DIM // 4
ROT_HALF = ROT_DIM // 2
N_GROUPS = 4
EXPERTS_PER_GROUP = 8
N_EXPERTS = N_GROUPS * EXPERTS_PER_GROUP
EXPERT_FF = 256
EPS = 1e-6
LAM_INIT = 0.8 - 0.6 * math.exp(-0.3 * 0)
NEG_BIG = -1e30

ROW_TILE = 512
OUT_TILE = 512
ATTN_TILE = 1024
ATTN_KEY_TILE = 256
HGRN_BLOCK = 128
HGRN_STEP = 256
MOE_TILE = 512
EXPERT_PARTS = 2
SCATTER_TILE = 1024
COMBINE_TILE = 256

NT_DIMS = (((1,), (1,)), ((), ()))
TN_DIMS = (((0,), (0,)), ((), ()))


def _sigmoid(x):
    return 1.0 / (1.0 + jnp.exp(-x))


SLAB = D_MODEL // LANES


def _slab_store(ref, val):
    rows = val.shape[0]
    for c in range(SLAB):
        ref[pl.ds(c, rows, stride=SLAB), :] = val[:, c * LANES:(c + 1) * LANES]


def _slab_load(ref, rows):
    return jnp.concatenate([ref[pl.ds(c, rows, stride=SLAB), :] for c in range(SLAB)], axis=1)


def _adaln_body(c_ref, w_ref, b_ref, o_ref):
    c = c_ref[...]
    ca = c * _sigmoid(c)
    o_ref[...] = jnp.dot(ca, w_ref[...], preferred_element_type=F32, precision=HIGHEST) + b_ref[...]


def _adaln(c, ada_w, ada_b):
    bsz, d = c.shape
    n = ada_w.shape[1]
    tn = 1024
    return pl.pallas_call(
        _adaln_body,
        grid=(n // tn,),
        in_specs=[pl.BlockSpec((bsz, d), lambda j: (0, 0)),
                  pl.BlockSpec((d, tn), lambda j: (0, j)),
                  pl.BlockSpec((1, tn), lambda j: (0, j))],
        out_specs=pl.BlockSpec((bsz, tn), lambda j: (0, j)),
        out_shape=jax.ShapeDtypeStruct((bsz, n), F32),
        name="adaln",
    )(c, ada_w, ada_b)


def _inproj_body(x_ref, mod_ref, nw_ref, w32_ref, pos_ref, invf_ref, spread_ref, lbr_ref,
                 q_ref, k_ref, v_ref, hq_ref, lf_ref, kf_ref, gi_ref, sg_ref, w_ref):
    @pl.when(pl.program_id(0) == 0)
    def _():
        for j in range(N_STREAMS):
            cols = slice(j * STREAM_W, (j + 1) * STREAM_W)
            w_ref[:, cols] = w32_ref[:, cols].astype(BF16)

    x = x_ref[...]
    ms = jnp.mean(x * x, axis=-1, keepdims=True)
    y = x * lax.rsqrt(ms + EPS) * nw_ref[...]
    shift = mod_ref[0, 0:1, :]
    scale = mod_ref[0, 1:2, :]
    h = (y * (1.0 + scale) + shift).astype(BF16)

    ang_t = invf_ref[...] * pos_ref[0]

    def spread(table_t):
        hi = table_t.astype(BF16)
        lo = (table_t - hi.astype(F32)).astype(BF16)
        return lax.dot_general(jnp.concatenate([hi, lo], axis=0), spread_ref[...], TN_DIMS,
                               preferred_element_type=F32)

    lane = lax.broadcasted_iota(jnp.int32, (1, LANES), 1) % DA_QK_DIM
    cosv = spread(jnp.cos(ang_t)) + jnp.where(lane < ROT_DIM, 0.0, 1.0)
    sinv = spread(jnp.sin(ang_t))
    sin_lo = jnp.where(lane < ROT_HALF, -sinv, 0.0)
    sin_hi = jnp.where((lane >= ROT_HALF) & (lane < ROT_DIM), sinv, 0.0)

    streams = [jnp.dot(h, w_ref[:, j * STREAM_W:(j + 1) * STREAM_W], preferred_element_type=F32)
               for j in range(N_STREAMS)]

    def proj(j):
        return streams[j]

    def rope(t):
        outs = []
        for hb in range(HEADS):
            tc = t[:, hb * HEAD_W:(hb + 1) * HEAD_W]
            outs.append(tc * cosv
                        + pltpu.roll(tc, LANES - ROT_HALF, 1) * sin_lo
                        + pltpu.roll(tc, ROT_HALF, 1) * sin_hi)
        return jnp.concatenate(outs, axis=1)

    q_ref[0] = (rope(proj(0)) * (DA_QK_DIM ** -0.5 * math.log2(math.e))).T.astype(BF16)
    k_ref[...] = rope(proj(1)).astype(BF16)
    v_ref[0] = proj(2).astype(BF16).T

    gq = proj(3)
    hq_ref[...] = gq * _sigmoid(gq)

    a = lbr_ref[...]
    amax = jnp.max(a, axis=0, keepdims=True)
    ea = jnp.exp(a - amax)
    lb = ea[0:1, :] / jnp.sum(ea, axis=0, keepdims=True)
    gf = proj(4)
    f = lb + (1.0 - lb) * _sigmoid(gf)
    lf_ref[...] = jnp.log(f) * math.log2(math.e)
    kf_ref[...] = 1.0 - f

    gi_ref[...] = proj(5).astype(BF16)
    gg = proj(6)
    sg_ref[...] = (gg * _sigmoid(gg)).astype(BF16)


def _inproj(x2, mod3, norm_w, w_f32, pos_rows, invf, spread, lb_raw, seq):
    t, d = x2.shape
    tm = ROW_TILE
    per_b = seq // tm
    row = lambda i: (i, 0)
    full = lambda i: (0, 0)
    out_bf = jax.ShapeDtypeStruct((t, STREAM_W), BF16)
    out_f = jax.ShapeDtypeStruct((t, STREAM_W), F32)
    out_t = jax.ShapeDtypeStruct((t // seq, STREAM_W, seq), BF16)
    stream = pl.BlockSpec((tm, STREAM_W), row)
    stream_t = pl.BlockSpec((1, STREAM_W, tm), lambda i: (i // per_b, 0, i % per_b))
    return pl.pallas_call(
        _inproj_body,
        grid=(t // tm,),
        in_specs=[pl.BlockSpec((tm, d), row),
                  pl.BlockSpec((1, 6, d), lambda i: (i // per_b, 0, 0)),
                  pl.BlockSpec((1, d), full),
                  pl.BlockSpec((d, N_STREAMS * STREAM_W), full, pipeline_mode=pl.Buffered(1)),
                  pl.BlockSpec((1, 1, tm), lambda i: (i, 0, 0)),
                  pl.BlockSpec((ROT_HALF, 1), full),
                  pl.BlockSpec((2 * ROT_HALF, LANES), full),
                  pl.BlockSpec(lb_raw.shape, full)],
        out_specs=[stream_t, stream, stream_t] + [stream] * 5,
        out_shape=[out_t, out_bf, out_t, out_f, out_f, out_f, out_bf, out_bf],
        scratch_shapes=[pltpu.VMEM((d, N_STREAMS * STREAM_W), BF16)],
        compiler_params=pltpu.CompilerParams(dimension_semantics=("arbitrary",),
                                             vmem_limit_bytes=56 * 1024 * 1024),
        name="inproj",
    )(x2, mod3, norm_w, w_f32, pos_rows, invf, spread, lb_raw)


ONES_ROWS = 16


def _attn_body(qt_ref, k_ref, vt_ref, lam_ref, sw_ref, o_ref, s_scr, m_scr, a_scr, *, tile, ktile):
    qi = pl.program_id(2)
    qt = qt_ref[0]
    feat = lax.broadcasted_iota(jnp.int32, (HEAD_W, 1), 0)
    zero = jnp.zeros_like(qt)
    qmaps = (jnp.where(feat < DA_QK_DIM, qt, zero), jnp.where(feat >= DA_QK_DIM, qt, zero))

    m_scr[...] = jnp.full(m_scr.shape, NEG_BIG, F32)
    a_scr[...] = jnp.zeros(a_scr.shape, F32)

    def score_block(start, nkeys, q_lo, masked):
        kb = k_ref[0, pl.ds(start, nkeys), :]
        scores = [jnp.dot(kb, qmaps[mp][:, q_lo:], preferred_element_type=F32) for mp in range(2)]
        for mp, s in enumerate(scores):
            if masked:
                key = lax.broadcasted_iota(jnp.int32, s.shape, 0)
                qry = lax.broadcasted_iota(jnp.int32, s.shape, 1)
                s = jnp.where(key <= qry, s, NEG_BIG)
            grouped = s.reshape(nkeys // SUBLANES, SUBLANES, tile - q_lo)
            m_scr[mp, :, q_lo:] = jnp.maximum(m_scr[mp, :, q_lo:], jnp.max(grouped, axis=0))
            s_scr[mp, pl.ds(start, nkeys), q_lo:] = s

    def value_block(start, nkeys, q_lo, col_max):
        vb = jnp.concatenate([vt_ref[0, :, pl.ds(start, nkeys)], jnp.ones((ONES_ROWS, nkeys), BF16)],
                             axis=0)
        probs = [jnp.exp2(s_scr[mp, pl.ds(start, nkeys), q_lo:] - col_max[mp][:, q_lo:]).astype(BF16)
                 for mp in range(2)]
        for mp, p in enumerate(probs):
            a_scr[mp, :, q_lo:] += jnp.dot(vb, p, preferred_element_type=F32)

    diag = pl.multiple_of(qi * tile, tile)
    pieces = [(diag + j * ktile, ktile, j * ktile) for j in range(tile // ktile)]

    def score_body(ki, carry):
        score_block(pl.multiple_of(ki * tile, tile), tile, 0, False)
        return carry

    lax.fori_loop(0, qi, score_body, 0)
    for start, nkeys, q_lo in pieces:
        score_block(start, nkeys, q_lo, True)
    col_max = [jnp.max(m_scr[mp], axis=0, keepdims=True) for mp in range(2)]

    def value_body(ki, carry):
        value_block(pl.multiple_of(ki * tile, tile), tile, 0, col_max)
        return carry

    lax.fori_loop(0, qi, value_body, 0)
    for start, nkeys, q_lo in pieces:
        value_block(start, nkeys, q_lo, col_max)

    lp = lam_ref[...]
    lam = (jnp.exp(jnp.sum(lp[0:1] * lp[1:2], axis=-1, keepdims=True))
           - jnp.exp(jnp.sum(lp[2:3] * lp[3:4], axis=-1, keepdims=True)) + LAM_INIT)
    o = (a_scr[0, 0:HEAD_W, :] / a_scr[0, HEAD_W:HEAD_W + 1, :]
         - lam * (a_scr[1, 0:HEAD_W, :] / a_scr[1, HEAD_W:HEAD_W + 1, :]))
    o = o * lax.rsqrt(jnp.mean(o * o, axis=0, keepdims=True) + EPS) * sw_ref[...]
    o_ref[0] = (o * (1.0 - LAM_INIT)).T.astype(BF16)


def _attn(qt3, k3, vt3, lam_p, subln_col):
    bsz, seq, _ = k3.shape
    tile = ATTN_TILE
    return pl.pallas_call(
        functools.partial(_attn_body, tile=tile, ktile=ATTN_KEY_TILE),
        grid=(bsz, HEADS, seq // tile),
        in_specs=[pl.BlockSpec((1, HEAD_W, tile), lambda b, h, i: (b, h, i)),
                  pl.BlockSpec((1, seq, HEAD_W), lambda b, h, i: (b, 0, h)),
                  pl.BlockSpec((1, HEAD_W, seq), lambda b, h, i: (b, h, 0)),
                  pl.BlockSpec(lam_p.shape, lambda b, h, i: (0, 0)),
                  pl.BlockSpec((HEAD_W, 1), lambda b, h, i: (0, 0))],
        out_specs=pl.BlockSpec((1, tile, HEAD_W), lambda b, h, i: (b, i, h)),
        out_shape=jax.ShapeDtypeStruct((bsz, seq, STREAM_W), BF16),
        scratch_shapes=[pltpu.VMEM((2, seq, tile), F32),
                        pltpu.VMEM((2, SUBLANES, tile), F32),
                        pltpu.VMEM((2, HEAD_W + ONES_ROWS, tile), F32)],
        compiler_params=pltpu.CompilerParams(vmem_limit_bytes=56 * 1024 * 1024),
        name="diff_attn",
    )(qt3, k3, vt3, lam_p, subln_col)


def _hgrn_levels(block):
    return [block >> (i + 1) for i in range(block.bit_length() - 1)]


def _hgrn_constants(block):
    t = np.arange(block)[:, None]
    s = np.arange(block)[None, :]
    tril = (s <= t).astype(np.float32)
    lv = np.full((block, block), -1, np.int32)
    halves = _hgrn_levels(block)
    for li, m in enumerate(halves):
        same = (t // (2 * m)) == (s // (2 * m))
        lv[same & ((t & m) != 0) & ((s & m) == 0)] = li
    lv[np.arange(block), np.arange(block)] = len(halves)
    return jnp.asarray(tril, BF16), jnp.asarray(lv)


def _level_operand(b_ref, h, b, q, k, m, block):
    def ref_rows(r, n):
        return jnp.broadcast_to(b_ref[h, pl.ds(r, 1), :], (n, HEAD_W))

    if m >= SUBLANES:
        pieces = []
        for s0 in range(0, block, 2 * m):
            ref = ref_rows(s0 + m - 1, m)
            lo = slice(s0, s0 + m)
            up = slice(s0 + m, s0 + 2 * m)
            pieces.append(k[lo] * jnp.exp2(ref - b[lo]))
            pieces.append(q[up] * jnp.exp2(b[up] - ref))
        return jnp.concatenate(pieces, axis=0)

    sub = lax.broadcasted_iota(jnp.int32, (SUBLANES, HEAD_W), 0)
    refs = []
    for s0 in range(0, block, SUBLANES):
        piece = ref_rows(s0 + m - 1, SUBLANES)
        for j in range(1, SUBLANES // (2 * m)):
            piece = jnp.where(sub >= 2 * m * j, ref_rows(s0 + 2 * m * j + m - 1, SUBLANES), piece)
        refs.append(piece)
    d = b - jnp.concatenate(refs, axis=0)
    row = lax.broadcasted_iota(jnp.int32, (block, 1), 0)
    return jnp.where((row & m) != 0, q, k) * jnp.exp2(jnp.minimum(d, -d))


def _hgrn_body(hq_ref, lf_ref, kf_ref, gi_ref, sg_ref, nw_ref, tril_ref, lv_ref, o_ref,
               st_scr, b_scr, *, block, step):
    @pl.when(pl.program_id(1) == 0)
    def _():
        st_scr[...] = jnp.zeros(st_scr.shape, F32)

    tril = tril_ref[...]
    lv = lv_ref[...]
    halves = _hgrn_levels(block)

    units = [(u, r0, h) for u, (r0, h) in enumerate(
        (r0, h) for r0 in range(0, step, block) for h in range(HEADS))]

    def cols(h):
        return slice(h * HEAD_W, (h + 1) * HEAD_W)

    q, k, v, b, scores = {}, {}, {}, {}, {}
    for u, r0, h in units:
        rows = slice(r0, r0 + block)
        q[u] = hq_ref[0, rows, cols(h)]
        k[u] = kf_ref[0, rows, cols(h)]
        v[u] = gi_ref[0, rows, cols(h)]
        lf = lf_ref[0, rows, cols(h)]
        hi = lf.astype(BF16)
        r1 = lf - hi.astype(F32)
        mid = r1.astype(BF16)
        lo = (r1 - mid.astype(F32)).astype(BF16)
        b[u] = (jnp.dot(tril, hi, preferred_element_type=F32)
                + jnp.dot(tril, mid, preferred_element_type=F32)
                + jnp.dot(tril, lo, preferred_element_type=F32))
        b_scr[u] = b[u]
        scores[u] = jnp.where(lv == len(halves),
                              lax.dot_general(q[u].astype(BF16), k[u].astype(BF16), NT_DIMS,
                                              preferred_element_type=F32), 0.0)

    for li, m in enumerate(halves):
        for u, r0, h in units:
            xl = _level_operand(b_scr, u, b[u], q[u], k[u], m, block).astype(BF16)
            p = lax.dot_general(xl, xl, NT_DIMS, preferred_element_type=F32)
            scores[u] = jnp.where(lv == li, p, scores[u])

    for u, r0, h in units:
        rows = slice(r0, r0 + block)
        o_intra = jnp.dot(scores[u].astype(BF16), v[u], preferred_element_type=F32)
        st = st_scr[h]
        o_inter = lax.dot_general((q[u] * jnp.exp2(b[u])).astype(BF16), st.astype(BF16), NT_DIMS,
                                  preferred_element_type=F32)
        b_last = b[u][block - 1:block, :]
        kdec = (k[u] * jnp.exp2(b_last - b[u])).astype(BF16)
        st_scr[h] = st * jnp.exp2(b_last) + lax.dot_general(v[u], kdec, TN_DIMS,
                                                             preferred_element_type=F32)
        o = o_inter + o_intra
        o = o * lax.rsqrt(jnp.mean(o * o, axis=-1, keepdims=True) + EPS) * nw_ref[...]
        o_ref[0, rows, cols(h)] = (o * sg_ref[0, rows, cols(h)].astype(F32)).astype(BF16)


def _hgrn(hq3, lf3, kf3, gi3, sg3, norm_w):
    bsz, seq, _ = hq3.shape
    block = HGRN_BLOCK
    step = HGRN_STEP
    tril, lv = _hgrn_constants(block)
    blk = pl.BlockSpec((1, step, STREAM_W), lambda b, g: (b, g, 0))
    const = lambda b, g: (0, 0)
    return pl.pallas_call(
        functools.partial(_hgrn_body, block=block, step=step),
        grid=(bsz, seq // step),
        in_specs=[blk, blk, blk, blk, blk,
                  pl.BlockSpec((1, HEAD_W), const),
                  pl.BlockSpec((block, block), const),
                  pl.BlockSpec((block, block), const)],
        out_specs=blk,
        out_shape=jax.ShapeDtypeStruct((bsz, seq, STREAM_W), BF16),
        scratch_shapes=[pltpu.VMEM((HEADS, HEAD_W, HEAD_W), F32),
                        pltpu.VMEM((HEADS * step // block, block, HEAD_W), F32)],
        name="hgrn2",
    )(hq3, lf3, kf3, gi3, sg3, norm_w, tril, lv)


META_E0, META_E1, META_R0, META_R1, META_W0, META_W1 = range(6)
GROUP_LANE0 = N_EXPERTS


OUT_PARTS = 4


def _route(logits, lane):
    far = jnp.int32(LANES)

    def first_max(vals):
        mx = jnp.max(vals, axis=-1, keepdims=True)
        return mx, jnp.min(jnp.where(vals == mx, lane, far), axis=-1, keepdims=True)

    is_g = (lane >= GROUP_LANE0) & (lane < GROUP_LANE0 + N_GROUPS)
    gmax, glane = first_max(jnp.where(is_g, logits, NEG_BIG))
    g_w = 1.0 / jnp.sum(jnp.where(is_g, jnp.exp(logits - gmax), 0.0), axis=-1, keepdims=True)
    gidx = glane - GROUP_LANE0
    in_grp = (lane < N_EXPERTS) & ((lane // EXPERTS_PER_GROUP) == gidx)
    el = jnp.where(in_grp, logits, NEG_BIG)
    m1, i1 = first_max(el)
    m2, i2 = first_max(jnp.where(lane == i1, NEG_BIG, el))
    r = jnp.exp(m2 - m1)
    return i1, i2, g_w / (1.0 + r), g_w * r / (1.0 + r)


def _outproj_body(da_ref, hg_ref, x_ref, mod_ref, wo_ref, nw_ref, wr_ref, br_ref, stril_ref,
                  x1_ref, h2_ref, meta_ref, route_ref, cnt_ref, carry_scr):
    @pl.when(pl.program_id(0) == 0)
    def _():
        carry_scr[...] = jnp.zeros(carry_scr.shape, F32)

    rows = stril_ref.shape[0]
    parts = [slice(p * rows, (p + 1) * rows) for p in range(OUT_PARTS)]
    gate1 = mod_ref[0, 2:3, :]
    shift2 = mod_ref[0, 3:4, :]
    scale2 = mod_ref[0, 4:5, :]

    attn = [jnp.dot(da_ref[r, :], wo_ref[0:STREAM_W, :], preferred_element_type=F32)
            + jnp.dot(hg_ref[r, :], wo_ref[STREAM_W:, :], preferred_element_type=F32) for r in parts]

    h2 = []
    for r, a in zip(parts, attn):
        x1 = x_ref[r, :] + gate1 * a
        x1_ref[r, :] = x1
        h2.append(x1 * lax.rsqrt(jnp.mean(x1 * x1, axis=-1, keepdims=True) + EPS) * nw_ref[...]
                  * (1.0 + scale2) + shift2)

    logits = []
    for p, h in enumerate(h2):
        _slab_store(h2_ref.at[pl.ds(p * rows * SLAB, rows * SLAB)], h)
        h_hi = h.astype(BF16)
        h_lo = (h - h_hi.astype(F32)).astype(BF16)
        terms = jnp.dot(jnp.concatenate([h_hi, h_lo], axis=0), wr_ref[...],
                        preferred_element_type=F32)
        logits.append((terms[:rows, :LANES] + terms[:rows, LANES:])
                      + (terms[rows:, :LANES] + terms[rows:, LANES:]) + br_ref[...])

    lane = lax.broadcasted_iota(jnp.int32, (rows, LANES), 1)
    routed = [_route(lg, lane) for lg in logits]

    carry = carry_scr[...]
    for r, (i1, i2, w0, w1) in zip(parts, routed):
        hot0 = lane == i1
        hot1 = lane == i2
        multi = jnp.where(hot0 | hot1, 1.0, 0.0)
        before = jnp.dot(stril_ref[...], multi.astype(BF16), preferred_element_type=F32) + carry
        rank0 = jnp.sum(jnp.where(hot0, before, 0.0), axis=-1, keepdims=True)
        rank1 = jnp.sum(jnp.where(hot1, before, 0.0), axis=-1, keepdims=True)
        carry = carry + jnp.sum(multi, axis=0, keepdims=True)
        meta = jnp.zeros((rows, LANES), F32)
        for idx, val in ((META_E0, i1.astype(F32)), (META_E1, i2.astype(F32)),
                         (META_R0, rank0), (META_R1, rank1), (META_W0, w0), (META_W1, w1)):
            meta = jnp.where(lane == idx, val, meta)
        meta_ref[r, :] = meta
        route_ref[:, r] = meta.T[0:SUBLANES, :]
    carry_scr[...] = carry
    cnt_ref[...] = carry


def _outproj(da2, hg2, x2, mod3, wo_bf, norm_w, w_route, b_route, seq):
    t, d = x2.shape
    tm = OUT_TILE
    per_b = seq // tm
    row = lambda i: (i, 0)
    full = lambda i: (0, 0)
    part = tm // OUT_PARTS
    stril = jnp.asarray(np.tril(np.ones((part, part), np.float32), -1), BF16)
    return pl.pallas_call(
        _outproj_body,
        grid=(t // tm,),
        in_specs=[pl.BlockSpec((tm, STREAM_W), row),
                  pl.BlockSpec((tm, STREAM_W), row),
                  pl.BlockSpec((tm, d), row),
                  pl.BlockSpec((1, 6, d), lambda i: (i // per_b, 0, 0)),
                  pl.BlockSpec((2 * STREAM_W, d), full),
                  pl.BlockSpec((1, d), full),
                  pl.BlockSpec((d, 2 * LANES), full),
                  pl.BlockSpec((1, LANES), full),
                  pl.BlockSpec((part, part), full)],
        out_specs=[pl.BlockSpec((tm, d), row),
                   pl.BlockSpec((tm * SLAB, LANES), row),
                   pl.BlockSpec((tm, LANES), row),
                   pl.BlockSpec((SUBLANES, tm), lambda i: (0, i)),
                   pl.BlockSpec((1, LANES), full)],
        out_shape=[jax.ShapeDtypeStruct((t, d), F32),
                   jax.ShapeDtypeStruct((t * SLAB, LANES), F32),
                   jax.ShapeDtypeStruct((t, LANES), F32),
                   jax.ShapeDtypeStruct((SUBLANES, t), F32),
                   jax.ShapeDtypeStruct((1, LANES), F32)],
        scratch_shapes=[pltpu.VMEM((1, LANES), F32)],
        compiler_params=pltpu.CompilerParams(dimension_semantics=("arbitrary",)),
        name="outproj_route",
    )(da2, hg2, x2, mod3, wo_bf, norm_w, w_route, b_route, stril)


DMA_UNROLL = 8


ZERO_CHUNK = 128
N_FILL_RANGES = N_EXPERTS + 1


def _scatter_body(fill_ref, pos_ref, h2_ref, xs_hbm, zero_scr, sem, zsem, *, tile):
    rows = tile * SLAB

    @pl.when(pl.program_id(0) == 0)
    def _():
        zero_scr[...] = jnp.zeros(zero_scr.shape, F32)

        def clear(chunk):
            start = pl.multiple_of(chunk * (ZERO_CHUNK * SLAB), SLAB)
            return pltpu.make_async_copy(zero_scr, xs_hbm.at[pl.ds(start, ZERO_CHUNK * SLAB)], zsem)

        def start_one(chunk, carry):
            clear(chunk).start()
            return carry

        def wait_one(chunk, carry):
            clear(chunk).wait()
            return carry

        for fn in (start_one, wait_one):
            for r in range(N_FILL_RANGES):
                lax.fori_loop(fill_ref[r], fill_ref[N_FILL_RANGES + r], fn, 0)

    def issue(t, carry):
        src = h2_ref.at[pl.ds(pl.multiple_of(t * SLAB, SLAB), SLAB)]
        for j in range(2):
            slot = pl.multiple_of(pos_ref[0, 0, j * tile + t] * SLAB, SLAB)
            pltpu.make_async_copy(src, xs_hbm.at[pl.ds(slot, SLAB)], sem).start(priority=j)
        return carry

    lax.fori_loop(0, tile, issue, 0, unroll=DMA_UNROLL)
    whole = pltpu.make_async_copy(h2_ref, xs_hbm.at[pl.ds(0, rows)], sem)
    whole.wait()
    whole.wait()


def _scatter(fill_ranges, pos3, h2s, n_slots):
    tile = SCATTER_TILE
    rows = tile * SLAB
    grid_spec = pltpu.PrefetchScalarGridSpec(
        num_scalar_prefetch=1,
        grid=(h2s.shape[0] // rows,),
        in_specs=[pl.BlockSpec((1, 1, 2 * tile), lambda i, lt: (i, 0, 0), memory_space=pltpu.SMEM),
                  pl.BlockSpec((rows, LANES), lambda i, lt: (i, 0))],
        out_specs=pl.BlockSpec(memory_space=pl.ANY),
        scratch_shapes=[pltpu.VMEM((ZERO_CHUNK * SLAB, LANES), F32),
                        pltpu.SemaphoreType.DMA, pltpu.SemaphoreType.DMA],
    )
    return pl.pallas_call(
        functools.partial(_scatter_body, tile=tile),
        grid_spec=grid_spec,
        out_shape=jax.ShapeDtypeStruct((n_slots * SLAB, LANES), F32),
        compiler_params=pltpu.CompilerParams(dimension_semantics=("arbitrary",)),
        name="moe_scatter",
    )(fill_ranges, pos3, h2s)


def _experts_body(te_ref, nv_ref, x_ref, wg_ref, wu_ref, wd_ref, y_ref, wgu_scr, wd_scr, *, tm):
    i = pl.program_id(0)
    e = te_ref[i]
    prev = te_ref[jnp.maximum(i - 1, 0)]

    @pl.when((i == 0) | (e != prev))
    def _():
        wgu_scr[:, 0:EXPERT_FF] = wg_ref[0].astype(BF16)
        wgu_scr[:, EXPERT_FF:] = wu_ref[0].astype(BF16)
        wd_scr[...] = wd_ref[0].astype(BF16)

    @pl.when(i < nv_ref[0])
    def _():
        part = tm // EXPERT_PARTS
        xr = [x_ref.at[pl.ds(r * part * SLAB, part * SLAB)] for r in range(EXPERT_PARTS)]
        yr = [y_ref.at[pl.ds(r * part * SLAB, part * SLAB)] for r in range(EXPERT_PARTS)]
        xs = [_slab_load(ref, part).astype(BF16) for ref in xr]
        gus = [jnp.dot(x, wgu_scr[...], preferred_element_type=F32) for x in xs]
        acts = [(gu[:, 0:EXPERT_FF] * _sigmoid(gu[:, 0:EXPERT_FF]) * gu[:, EXPERT_FF:]).astype(BF16)
                for gu in gus]
        ys = [jnp.dot(act, wd_scr[...], preferred_element_type=F32) for act in acts]
        for ref, y in zip(yr, ys):
            _slab_store(ref, y)

    @pl.when(i >= nv_ref[0])
    def _():
        y_ref[...] = jnp.zeros(y_ref.shape, F32)


def _experts(tile_expert, n_valid, xs, w_gate, w_up, w_down):
    tm = MOE_TILE
    rows = tm * SLAB
    d = w_gate.shape[1]
    live = lambda i, te, nv: (jnp.minimum(i, nv[0] - 1), 0)
    grid_spec = pltpu.PrefetchScalarGridSpec(
        num_scalar_prefetch=2,
        grid=(xs.shape[0] // rows,),
        in_specs=[pl.BlockSpec((rows, LANES), live),
                  pl.BlockSpec((1, d, EXPERT_FF), lambda i, te, nv: (te[i], 0, 0)),
                  pl.BlockSpec((1, d, EXPERT_FF), lambda i, te, nv: (te[i], 0, 0)),
                  pl.BlockSpec((1, EXPERT_FF, d), lambda i, te, nv: (te[i], 0, 0))],
        out_specs=pl.BlockSpec((rows, LANES), lambda i, te, nv: (i, 0)),
        scratch_shapes=[pltpu.VMEM((d, 2 * EXPERT_FF), BF16),
                        pltpu.VMEM((EXPERT_FF, d), BF16)],
    )
    return pl.pallas_call(
        functools.partial(_experts_body, tm=tm),
        grid_spec=grid_spec,
        out_shape=jax.ShapeDtypeStruct(xs.shape, F32),
        compiler_params=pltpu.CompilerParams(dimension_semantics=("arbitrary",)),
        name="moe_experts",
    )(tile_expert, n_valid, xs, w_gate, w_up, w_down)


def _combine_body(pos_ref, nxt_ref, x1_ref, meta_ref, mod_ref, nw_ref, ys_hbm, o_ref, rows_scr, sems,
                  *, tile):
    i = pl.program_id(0)
    n = pl.num_programs(0)
    rows = tile * SLAB

    def fetch(idx_ref, buf):
        def issue(t, carry):
            dst = pl.ds(pl.multiple_of(t * SLAB, SLAB), SLAB)
            for j in range(2):
                slot = pl.multiple_of(idx_ref[0, 0, j * tile + t] * SLAB, SLAB)
                pltpu.make_async_copy(ys_hbm.at[pl.ds(slot, SLAB)], rows_scr.at[buf, j, dst],
                                      sems.at[buf]).start(priority=j)
            return carry
        lax.fori_loop(0, tile, issue, 0, unroll=DMA_UNROLL)

    @pl.when(i == 0)
    def _():
        fetch(pos_ref, 0)

    for cur in range(2):
        @pl.when((i % 2 == cur) & (i + 1 < n))
        def _():
            fetch(nxt_ref, 1 - cur)

    for cur in range(2):
        @pl.when(i % 2 == cur)
        def _():
            for j in range(2):
                pltpu.make_async_copy(ys_hbm.at[pl.ds(0, rows)], rows_scr.at[cur, j],
                                      sems.at[cur]).wait()
            meta = meta_ref[...]
            w0 = meta[:, META_W0:META_W0 + 1]
            w1 = meta[:, META_W1:META_W1 + 1]
            y = (w0 * _slab_load(rows_scr.at[cur, 0], tile)
                 + w1 * _slab_load(rows_scr.at[cur, 1], tile))
            x2 = x1_ref[...] + mod_ref[0, 5:6, :] * y
            o_ref[...] = (x2 * lax.rsqrt(jnp.mean(x2 * x2, axis=-1, keepdims=True) + EPS)
                          * nw_ref[...])


def _combine(pos3, x1, meta, mod3, norm_w, ys, seq):
    t, d = x1.shape
    tile = COMBINE_TILE
    per_b = seq // tile
    n = t // tile
    row = lambda i: (i, 0)
    return pl.pallas_call(
        functools.partial(_combine_body, tile=tile),
        grid=(n,),
        in_specs=[pl.BlockSpec((1, 1, 2 * tile), lambda i: (i, 0, 0), memory_space=pltpu.SMEM),
                  pl.BlockSpec((1, 1, 2 * tile), lambda i: (jnp.minimum(i + 1, n - 1), 0, 0),
                               memory_space=pltpu.SMEM),
                  pl.BlockSpec((tile, d), row),
                  pl.BlockSpec((tile, LANES), row),
                  pl.BlockSpec((1, 6, d), lambda i: (i // per_b, 0, 0)),
                  pl.BlockSpec((1, d), lambda i: (0, 0)),
                  pl.BlockSpec(memory_space=pl.ANY)],
        out_specs=pl.BlockSpec((tile, d), row),
        out_shape=jax.ShapeDtypeStruct((t, d), F32),
        scratch_shapes=[pltpu.VMEM((2, 2, tile * SLAB, LANES), F32), pltpu.SemaphoreType.DMA((2,))],
        compiler_params=pltpu.CompilerParams(dimension_semantics=("arbitrary",)),
        name="moe_combine",
    )(pos3, pos3, x1, meta, mod3, norm_w, ys)


def _rope_constants():
    inv_freq = ROPE_THETA ** (-jnp.arange(ROT_HALF, dtype=F32) / ROT_HALF)
    lane = np.arange(LANES) % DA_QK_DIM
    hit = (lane[None, :] < ROT_DIM) & (lane[None, :] % ROT_HALF == np.arange(ROT_HALF)[:, None])
    spread = np.concatenate([hit, hit], axis=0).astype(np.float32)
    return inv_freq.reshape(ROT_HALF, 1), jnp.asarray(spread, BF16)


def kernel(x, c, positions, norm1_w, norm2_w, final_norm_w, ada_w, ada_b, w_in, w_out, da_lambda_q1, da_lambda_k1, da_lambda_q2, da_lambda_k2, da_subln_w, hg_lower_bound, hg_norm_w, moe_w_group, moe_b_group, moe_w_router, moe_b_router, moe_w_gate, moe_w_up, moe_w_down):
    bsz, seq, d = x.shape
    assert d == D_MODEL and norm1_w.shape[0] == 1, "single-layer model of width 1024 only"
    assert seq % ATTN_TILE == 0 and seq % HGRN_STEP == 0 and seq % OUT_TILE == 0
    t = bsz * seq
    x2 = x.reshape(t, d)

    mod3 = _adaln(c, ada_w[0], ada_b).reshape(bsz, 6, d)

    pos_rows = positions.astype(F32).reshape(t // ROW_TILE, 1, ROW_TILE)
    inv_freq, spread = _rope_constants()
    qt, k, vt, hq, lf, kf, gi, sg = _inproj(x2, mod3, norm1_w, w_in[0], pos_rows,
                                          inv_freq, spread, hg_lower_bound, seq)

    as3 = lambda a: a.reshape(bsz, seq, STREAM_W)
    lam_p = jnp.concatenate([da_lambda_q1, da_lambda_k1, da_lambda_q2, da_lambda_k2], axis=0)
    da = _attn(qt, as3(k), vt, lam_p, da_subln_w.reshape(HEAD_W, 1))
    hg = _hgrn(as3(hq), as3(lf), as3(kf), as3(gi), as3(sg), hg_norm_w)

    pad = jnp.zeros((d, LANES - N_EXPERTS - N_GROUPS), F32)
    w_route = jnp.concatenate([moe_w_router[0], moe_w_group[0], pad], axis=1)
    b_route = jnp.concatenate([moe_b_router[0], moe_b_group[0], pad[0]]).reshape(1, LANES)
    w_route_hi = w_route.astype(BF16)
    w_route_lo = (w_route - w_route_hi.astype(F32)).astype(BF16)
    x1, h2, meta, route, cnt = _outproj(da.reshape(t, STREAM_W), hg.reshape(t, STREAM_W), x2, mod3,
                                 w_out[0].astype(BF16), norm2_w,
                                 jnp.concatenate([w_route_hi, w_route_lo], axis=1), b_route, seq)

    counts = cnt[0, :N_EXPERTS].astype(jnp.int32)
    tiles_e = (counts + MOE_TILE - 1) // MOE_TILE
    tile_end = jnp.cumsum(tiles_e)
    offs = (tile_end - tiles_e) * MOE_TILE
    ids = route[META_E0:META_E1 + 1].astype(jnp.int32)
    ranks = route[META_R0:META_R1 + 1].astype(jnp.int32)
    expert_iota = jnp.arange(N_EXPERTS, dtype=jnp.int32)[:, None, None]
    pos = jnp.sum(jnp.where(ids[None] == expert_iota, offs[:, None, None], 0), axis=0) + ranks
    n_tiles = (2 * t) // MOE_TILE + N_EXPERTS
    n_valid = tile_end[-1:]
    tile_ids = jnp.minimum(jnp.arange(n_tiles, dtype=jnp.int32), n_valid - 1)
    tile_expert = jnp.sum(tile_ids[:, None] >= tile_end[None, :], axis=1).astype(jnp.int32)
    def slot_table(tile):
        return pos.reshape(2, t // tile, tile).transpose(1, 0, 2).reshape(t // tile, 1, 2 * tile)

    fill_lo = jnp.concatenate([(offs + counts) // ZERO_CHUNK, n_valid * (MOE_TILE // ZERO_CHUNK)])
    fill_hi = jnp.concatenate([tile_end * (MOE_TILE // ZERO_CHUNK),
                               jnp.full((1,), n_tiles * (MOE_TILE // ZERO_CHUNK), jnp.int32)])
    fill_ranges = jnp.concatenate([fill_lo, fill_hi]).astype(jnp.int32)

    xs = _scatter(fill_ranges, slot_table(SCATTER_TILE), h2, n_tiles * MOE_TILE)
    ys = _experts(tile_expert, n_valid.astype(jnp.int32), xs, moe_w_gate[0], moe_w_up[0], moe_w_down[0])
    out = _combine(slot_table(COMBINE_TILE), x1, meta, mod3, final_norm_w.reshape(1, d), ys, seq)
    return out.reshape(bsz, seq, d)
```

```python
import functools
import math

import numpy as np
import jax
import jax.numpy as jnp
from jax import lax
from jax.experimental import pallas as pl
from jax.experimental.pallas import tpu as pltpu

F32 = jnp.float32
BF16 = jnp.bfloat16
HIGHEST = lax.Precision.HIGHEST

LANES = 128
SUBLANES = 8
D_MODEL = 1024
HEADS = 4
HEAD_W = 128
STREAM_W = HEADS * HEAD_W
N_STREAMS = 7
DA_QK_DIM = 64
ROPE_THETA = 500000.0
ROT_DIM = DA_QK_DIM // 4
ROT_HALF = ROT_DIM // 2
N_GROUPS = 4
EXPERTS_PER_GROUP = 8
N_EXPERTS = N_GROUPS * EXPERTS_PER_GROUP
EXPERT_FF = 256
EPS = 1e-6
LAM_INIT = 0.8 - 0.6 * math.exp(-0.3 * 0)
NEG_BIG = -1e30

ROW_TILE = 512
OUT_TILE = 1024
ATTN_TILE = 1024
ATTN_KEY_TILE = 256
HGRN_BLOCK = 128
HGRN_STEP = 256
MOE_TILE = 512
EXPERT_PARTS = 2
SCATTER_TILE = 1024
COMBINE_TILE = 256

NT_DIMS = (((1,), (1,)), ((), ()))
TN_DIMS = (((0,), (0,)), ((), ()))


def _sigmoid(x):
    return 1.0 / (1.0 + jnp.exp(-x))


SLAB = D_MODEL // LANES


def _slab_store(ref, val):
    rows = val.shape[0]
    for c in range(SLAB):
        ref[pl.ds(c, rows, stride=SLAB), :] = val[:, c * LANES:(c + 1) * LANES]


def _slab_load(ref, rows):
    return jnp.concatenate([ref[pl.ds(c, rows, stride=SLAB), :] for c in range(SLAB)], axis=1)


def _adaln_body(c_ref, w_ref, b_ref, o_ref):
    c = c_ref[...]
    ca = c * _sigmoid(c)
    o_ref[...] = jnp.dot(ca, w_ref[...], preferred_element_type=F32, precision=HIGHEST) + b_ref[...]


def _adaln(c, ada_w, ada_b):
    bsz, d = c.shape
    n = ada_w.shape[1]
    tn = 1024
    return pl.pallas_call(
        _adaln_body,
        grid=(n // tn,),
        in_specs=[pl.BlockSpec((bsz, d), lambda j: (0, 0)),
                  pl.BlockSpec((d, tn), lambda j: (0, j)),
                  pl.BlockSpec((1, tn), lambda j: (0, j))],
        out_specs=pl.BlockSpec((bsz, tn), lambda j: (0, j)),
        out_shape=jax.ShapeDtypeStruct((bsz, n), F32),
        name="adaln",
    )(c, ada_w, ada_b)


def _inproj_body(x_ref, mod_ref, nw_ref, w32_ref, pos_ref, invf_ref, spread_ref, lbr_ref,
                 q_ref, k_ref, v_ref, hq_ref, lf_ref, kf_ref, gi_ref, sg_ref, w_ref):
    @pl.when(pl.program_id(0) == 0)
    def _():
        for j in range(N_STREAMS):
            cols = slice(j * STREAM_W, (j + 1) * STREAM_W)
            w_ref[:, cols] = w32_ref[:, cols].astype(BF16)

    x = x_ref[...]
    ms = jnp.mean(x * x, axis=-1, keepdims=True)
    y = x * lax.rsqrt(ms + EPS) * nw_ref[...]
    shift = mod_ref[0, 0:1, :]
    scale = mod_ref[0, 1:2, :]
    h = (y * (1.0 + scale) + shift).astype(BF16)

    ang_t = invf_ref[...] * pos_ref[0]

    def spread(table_t):
        hi = table_t.astype(BF16)
        lo = (table_t - hi.astype(F32)).astype(BF16)
        return lax.dot_general(jnp.concatenate([hi, lo], axis=0), spread_ref[...], TN_DIMS,
                               preferred_element_type=F32)

    lane = lax.broadcasted_iota(jnp.int32, (1, LANES), 1) % DA_QK_DIM
    cosv = spread(jnp.cos(ang_t)) + jnp.where(lane < ROT_DIM, 0.0, 1.0)
    sinv = spread(jnp.sin(ang_t))
    sin_lo = jnp.where(lane < ROT_HALF, -sinv, 0.0)
    sin_hi = jnp.where((lane >= ROT_HALF) & (lane < ROT_DIM), sinv, 0.0)

    streams = [jnp.dot(h, w_ref[:, j * STREAM_W:(j + 1) * STREAM_W], preferred_element_type=F32)
               for j in range(N_STREAMS)]

    def proj(j):
        return streams[j]

    def rope(t):
        outs = []
        for hb in range(HEADS):
            tc = t[:, hb * HEAD_W:(hb + 1) * HEAD_W]
            outs.append(tc * cosv
                        + pltpu.roll(tc, LANES - ROT_HALF, 1) * sin_lo
                        + pltpu.roll(tc, ROT_HALF, 1) * sin_hi)
        return jnp.concatenate(outs, axis=1)

    q_ref[0] = (rope(proj(0)) * (DA_QK_DIM ** -0.5 * math.log2(math.e))).T.astype(BF16)
    k_ref[...] = rope(proj(1)).astype(BF16)
    v_ref[0] = proj(2).astype(BF16).T

    gq = proj(3)
    hq_ref[...] = gq * _sigmoid(gq)

    a = lbr_ref[...]
    amax = jnp.max(a, axis=0, keepdims=True)
    ea = jnp.exp(a - amax)
    lb = ea[0:1, :] / jnp.sum(ea, axis=0, keepdims=True)
    gf = proj(4)
    f = lb + (1.0 - lb) * _sigmoid(gf)
    lf_ref[...] = jnp.log(f) * math.log2(math.e)
    kf_ref[...] = 1.0 - f

    gi_ref[...] = proj(5).astype(BF16)
    gg = proj(6)
    sg_ref[...] = (gg * _sigmoid(gg)).astype(BF16)


def _inproj(x2, mod3, norm_w, w_f32, pos_rows, invf, spread, lb_raw, seq):
    t, d = x2.shape
    tm = ROW_TILE
    per_b = seq // tm
    row = lambda i: (i, 0)
    full = lambda i: (0, 0)
    out_bf = jax.ShapeDtypeStruct((t, STREAM_W), BF16)
    out_f = jax.ShapeDtypeStruct((t, STREAM_W), F32)
    out_t = jax.ShapeDtypeStruct((t // seq, STREAM_W, seq), BF16)
    stream = pl.BlockSpec((tm, STREAM_W), row)
    stream_t = pl.BlockSpec((1, STREAM_W, tm), lambda i: (i // per_b, 0, i % per_b))
    return pl.pallas_call(
        _inproj_body,
        grid=(t // tm,),
        in_specs=[pl.BlockSpec((tm, d), row),
                  pl.BlockSpec((1, 6, d), lambda i: (i // per_b, 0, 0)),
                  pl.BlockSpec((1, d), full),
                  pl.BlockSpec((d, N_STREAMS * STREAM_W), full, pipeline_mode=pl.Buffered(1)),
                  pl.BlockSpec((1, 1, tm), lambda i: (i, 0, 0)),
                  pl.BlockSpec((ROT_HALF, 1), full),
                  pl.BlockSpec((2 * ROT_HALF, LANES), full),
                  pl.BlockSpec(lb_raw.shape, full)],
        out_specs=[stream_t, stream, stream_t] + [stream] * 5,
        out_shape=[out_t, out_bf, out_t, out_f, out_f, out_f, out_bf, out_bf],
        scratch_shapes=[pltpu.VMEM((d, N_STREAMS * STREAM_W), BF16)],
        compiler_params=pltpu.CompilerParams(dimension_semantics=("arbitrary",),
                                             vmem_limit_bytes=56 * 1024 * 1024),
        name="inproj",
    )(x2, mod3, norm_w, w_f32, pos_rows, invf, spread, lb_raw)


ONES_ROWS = 16


def _attn_body(qt_ref, k_ref, vt_ref, lam_ref, sw_ref, o_ref, s_scr, m_scr, a_scr, *, tile, ktile):
    qi = pl.program_id(2)
    qt = qt_ref[0]
    feat = lax.broadcasted_iota(jnp.int32, (HEAD_W, 1), 0)
    zero = jnp.zeros_like(qt)
    qmaps = (jnp.where(feat < DA_QK_DIM, qt, zero), jnp.where(feat >= DA_QK_DIM, qt, zero))

    m_scr[...] = jnp.full(m_scr.shape, NEG_BIG, F32)
    a_scr[...] = jnp.zeros(a_scr.shape, F32)

    def score_block(start, nkeys, q_lo, masked):
        kb = k_ref[0, pl.ds(start, nkeys), :]
        scores = [jnp.dot(kb, qmaps[mp][:, q_lo:], preferred_element_type=F32) for mp in range(2)]
        for mp, s in enumerate(scores):
            if masked:
                key = lax.broadcasted_iota(jnp.int32, s.shape, 0)
                qry = lax.broadcasted_iota(jnp.int32, s.shape, 1)
                s = jnp.where(key <= qry, s, NEG_BIG)
            grouped = s.reshape(nkeys // SUBLANES, SUBLANES, tile - q_lo)
            m_scr[mp, :, q_lo:] = jnp.maximum(m_scr[mp, :, q_lo:], jnp.max(grouped, axis=0))
            s_scr[mp, pl.ds(start, nkeys), q_lo:] = s

    def value_block(start, nkeys, q_lo, col_max):
        vb = jnp.concatenate([vt_ref[0, :, pl.ds(start, nkeys)], jnp.ones((ONES_ROWS, nkeys), BF16)],
                             axis=0)
        probs = [jnp.exp2(s_scr[mp, pl.ds(start, nkeys), q_lo:] - col_max[mp][:, q_lo:]).astype(BF16)
                 for mp in range(2)]
        for mp, p in enumerate(probs):
            a_scr[mp, :, q_lo:] += jnp.dot(vb, p, preferred_element_type=F32)

    diag = pl.multiple_of(qi * tile, tile)
    pieces = [(diag + j * ktile, ktile, j * ktile) for j in range(tile // ktile)]

    def score_body(ki, carry):
        score_block(pl.multiple_of(ki * tile, tile), tile, 0, False)
        return carry

    lax.fori_loop(0, qi, score_body, 0)
    for start, nkeys, q_lo in pieces:
        score_block(start, nkeys, q_lo, True)
    col_max = [jnp.max(m_scr[mp], axis=0, keepdims=True) for mp in range(2)]

    def value_body(ki, carry):
        value_block(pl.multiple_of(ki * tile, tile), tile, 0, col_max)
        return carry

    lax.fori_loop(0, qi, value_body, 0)
    for start, nkeys, q_lo in pieces:
        value_block(start, nkeys, q_lo, col_max)

    lp = lam_ref[...]
    lam = (jnp.exp(jnp.sum(lp[0:1] * lp[1:2], axis=-1, keepdims=True))
           - jnp.exp(jnp.sum(lp[2:3] * lp[3:4], axis=-1, keepdims=True)) + LAM_INIT)
    o = (a_scr[0, 0:HEAD_W, :] / a_scr[0, HEAD_W:HEAD_W + 1, :]
         - lam * (a_scr[1, 0:HEAD_W, :] / a_scr[1, HEAD_W:HEAD_W + 1, :]))
    o = o * lax.rsqrt(jnp.mean(o * o, axis=0, keepdims=True) + EPS) * sw_ref[...]
    o_ref[0] = (o * (1.0 - LAM_INIT)).T.astype(BF16)


def _attn(qt3, k3, vt3, lam_p, subln_col):
    bsz, seq, _ = k3.shape
    tile = ATTN_TILE
    return pl.pallas_call(
        functools.partial(_attn_body, tile=tile, ktile=ATTN_KEY_TILE),
        grid=(bsz, HEADS, seq // tile),
        in_specs=[pl.BlockSpec((1, HEAD_W, tile), lambda b, h, i: (b, h, i)),
                  pl.BlockSpec((1, seq, HEAD_W), lambda b, h, i: (b, 0, h)),
                  pl.BlockSpec((1, HEAD_W, seq), lambda b, h, i: (b, h, 0)),
                  pl.BlockSpec(lam_p.shape, lambda b, h, i: (0, 0)),
                  pl.BlockSpec((HEAD_W, 1), lambda b, h, i: (0, 0))],
        out_specs=pl.BlockSpec((1, tile, HEAD_W), lambda b, h, i: (b, i, h)),
        out_shape=jax.ShapeDtypeStruct((bsz, seq, STREAM_W), BF16),
        scratch_shapes=[pltpu.VMEM((2, seq, tile), F32),
                        pltpu.VMEM((2, SUBLANES, tile), F32),
                        pltpu.VMEM((2, HEAD_W + ONES_ROWS, tile), F32)],
        compiler_params=pltpu.CompilerParams(vmem_limit_bytes=56 * 1024 * 1024),
        name="diff_attn",
    )(qt3, k3, vt3, lam_p, subln_col)


def _hgrn_levels(block):
    return [block >> (i + 1) for i in range(block.bit_length() - 1)]


def _hgrn_constants(block):
    t = np.arange(block)[:, None]
    s = np.arange(block)[None, :]
    tril = (s <= t).astype(np.float32)
    lv = np.full((block, block), -1, np.int32)
    halves = _hgrn_levels(block)
    for li, m in enumerate(halves):
        same = (t // (2 * m)) == (s // (2 * m))
        lv[same & ((t & m) != 0) & ((s & m) == 0)] = li
    lv[np.arange(block), np.arange(block)] = len(halves)
    return jnp.asarray(tril, BF16), jnp.asarray(lv)


def _level_operand(b_ref, h, b, q, k, m, block):
    def ref_rows(r, n):
        return jnp.broadcast_to(b_ref[h, pl.ds(r, 1), :], (n, HEAD_W))

    if m >= SUBLANES:
        pieces = []
        for s0 in range(0, block, 2 * m):
            ref = ref_rows(s0 + m - 1, m)
            lo = slice(s0, s0 + m)
            up = slice(s0 + m, s0 + 2 * m)
            pieces.append(k[lo] * jnp.exp2(ref - b[lo]))
            pieces.append(q[up] * jnp.exp2(b[up] - ref))
        return jnp.concatenate(pieces, axis=0)

    sub = lax.broadcasted_iota(jnp.int32, (SUBLANES, HEAD_W), 0)
    refs = []
    for s0 in range(0, block, SUBLANES):
        piece = ref_rows(s0 + m - 1, SUBLANES)
        for j in range(1, SUBLANES // (2 * m)):
            piece = jnp.where(sub >= 2 * m * j, ref_rows(s0 + 2 * m * j + m - 1, SUBLANES), piece)
        refs.append(piece)
    d = b - jnp.concatenate(refs, axis=0)
    row = lax.broadcasted_iota(jnp.int32, (block, 1), 0)
    return jnp.where((row & m) != 0, q, k) * jnp.exp2(jnp.minimum(d, -d))


def _hgrn_body(hq_ref, lf_ref, kf_ref, gi_ref, sg_ref, nw_ref, tril_ref, lv_ref, o_ref,
               st_scr, b_scr, *, block, step):
    @pl.when(pl.program_id(1) == 0)
    def _():
        st_scr[...] = jnp.zeros(st_scr.shape, F32)

    tril = tril_ref[...]
    lv = lv_ref[...]
    halves = _hgrn_levels(block)

    units = [(u, r0, h) for u, (r0, h) in enumerate(
        (r0, h) for r0 in range(0, step, block) for h in range(HEADS))]

    def cols(h):
        return slice(h * HEAD_W, (h + 1) * HEAD_W)

    q, k, v, b, scores = {}, {}, {}, {}, {}
    for u, r0, h in units:
        rows = slice(r0, r0 + block)
        q[u] = hq_ref[0, rows, cols(h)]
        k[u] = kf_ref[0, rows, cols(h)]
        v[u] = gi_ref[0, rows, cols(h)]
        lf = lf_ref[0, rows, cols(h)]
        hi = lf.astype(BF16)
        r1 = lf - hi.astype(F32)
        mid = r1.astype(BF16)
        lo = (r1 - mid.astype(F32)).astype(BF16)
        b[u] = (jnp.dot(tril, hi, preferred_element_type=F32)
                + jnp.dot(tril, mid, preferred_element_type=F32)
                + jnp.dot(tril, lo, preferred_element_type=F32))
        b_scr[u] = b[u]
        scores[u] = jnp.where(lv == len(halves),
                              lax.dot_general(q[u].astype(BF16), k[u].astype(BF16), NT_DIMS,
                                              preferred_element_type=F32), 0.0)

    for li, m in enumerate(halves):
        for u, r0, h in units:
            xl = _level_operand(b_scr, u, b[u], q[u], k[u], m, block).astype(BF16)
            p = lax.dot_general(xl, xl, NT_DIMS, preferred_element_type=F32)
            scores[u] = jnp.where(lv == li, p, scores[u])

    for u, r0, h in units:
        rows = slice(r0, r0 + block)
        o_intra = jnp.dot(scores[u].astype(BF16), v[u], preferred_element_type=F32)
        st = st_scr[h]
        o_inter = lax.dot_general((q[u] * jnp.exp2(b[u])).astype(BF16), st.astype(BF16), NT_DIMS,
                                  preferred_element_type=F32)
        b_last = b[u][block - 1:block, :]
        kdec = (k[u] * jnp.exp2(b_last - b[u])).astype(BF16)
        st_scr[h] = st * jnp.exp2(b_last) + lax.dot_general(v[u], kdec, TN_DIMS,
                                                             preferred_element_type=F32)
        o = o_inter + o_intra
        o = o * lax.rsqrt(jnp.mean(o * o, axis=-1, keepdims=True) + EPS) * nw_ref[...]
        o_ref[0, rows, cols(h)] = (o * sg_ref[0, rows, cols(h)].astype(F32)).astype(BF16)


def _hgrn(hq3, lf3, kf3, gi3, sg3, norm_w):
    bsz, seq, _ = hq3.shape
    block = HGRN_BLOCK
    step = HGRN_STEP
    tril, lv = _hgrn_constants(block)
    blk = pl.BlockSpec((1, step, STREAM_W), lambda b, g: (b, g, 0))
    const = lambda b, g: (0, 0)
    return pl.pallas_call(
        functools.partial(_hgrn_body, block=block, step=step),
        grid=(bsz, seq // step),
        in_specs=[blk, blk, blk, blk, blk,
                  pl.BlockSpec((1, HEAD_W), const),
                  pl.BlockSpec((block, block), const),
                  pl.BlockSpec((block, block), const)],
        out_specs=blk,
        out_shape=jax.ShapeDtypeStruct((bsz, seq, STREAM_W), BF16),
        scratch_shapes=[pltpu.VMEM((HEADS, HEAD_W, HEAD_W), F32),
                        pltpu.VMEM((HEADS * step // block, block, HEAD_W), F32)],
        name="hgrn2",
    )(hq3, lf3, kf3, gi3, sg3, norm_w, tril, lv)


META_E0, META_E1, META_R0, META_R1, META_W0, META_W1 = range(6)
GROUP_LANE0 = N_EXPERTS


OUT_PARTS = 8


def _route(logits, lane):
    far = jnp.int32(LANES)

    def first_max(vals):
        mx = jnp.max(vals, axis=-1, keepdims=True)
        return mx, jnp.min(jnp.where(vals == mx, lane, far), axis=-1, keepdims=True)

    is_g = (lane >= GROUP_LANE0) & (lane < GROUP_LANE0 + N_GROUPS)
    gmax, glane = first_max(jnp.where(is_g, logits, NEG_BIG))
    g_w = 1.0 / jnp.sum(jnp.where(is_g, jnp.exp(logits - gmax), 0.0), axis=-1, keepdims=True)
    gidx = glane - GROUP_LANE0
    in_grp = (lane < N_EXPERTS) & ((lane // EXPERTS_PER_GROUP) == gidx)
    el = jnp.where(in_grp, logits, NEG_BIG)
    m1, i1 = first_max(el)
    m2, i2 = first_max(jnp.where(lane == i1, NEG_BIG, el))
    r = jnp.exp(m2 - m1)
    return i1, i2, g_w / (1.0 + r), g_w * r / (1.0 + r)


def _outproj_body(da_ref, hg_ref, x_ref, mod_ref, wo_ref, nw_ref, wr_ref, br_ref, stril_ref,
                  x1_ref, h2_ref, meta_ref, route_ref, cnt_ref, carry_scr):
    @pl.when(pl.program_id(0) == 0)
    def _():
        carry_scr[...] = jnp.zeros(carry_scr.shape, F32)

    rows = stril_ref.shape[0]
    parts = [slice(p * rows, (p + 1) * rows) for p in range(OUT_PARTS)]
    gate1 = mod_ref[0, 2:3, :]
    shift2 = mod_ref[0, 3:4, :]
    scale2 = mod_ref[0, 4:5, :]

    attn = [jnp.dot(da_ref[r, :], wo_ref[0:STREAM_W, :], preferred_element_type=F32)
            + jnp.dot(hg_ref[r, :], wo_ref[STREAM_W:, :], preferred_element_type=F32) for r in parts]

    h2 = []
    for r, a in zip(parts, attn):
        x1 = x_ref[r, :] + gate1 * a
        x1_ref[r, :] = x1
        h2.append(x1 * lax.rsqrt(jnp.mean(x1 * x1, axis=-1, keepdims=True) + EPS) * nw_ref[...]
                  * (1.0 + scale2) + shift2)

    logits = []
    for p, h in enumerate(h2):
        _slab_store(h2_ref.at[pl.ds(p * rows * SLAB, rows * SLAB)], h)
        h_hi = h.astype(BF16)
        h_lo = (h - h_hi.astype(F32)).astype(BF16)
        terms = jnp.dot(jnp.concatenate([h_hi, h_lo], axis=0), wr_ref[...],
                        preferred_element_type=F32)
        logits.append((terms[:rows, :LANES] + terms[:rows, LANES:])
                      + (terms[rows:, :LANES] + terms[rows:, LANES:]) + br_ref[...])

    lane = lax.broadcasted_iota(jnp.int32, (rows, LANES), 1)
    routed = [_route(lg, lane) for lg in logits]

    carry = carry_scr[...]
    for r, (i1, i2, w0, w1) in zip(parts, routed):
        hot0 = lane == i1
        hot1 = lane == i2
        multi = jnp.where(hot0 | hot1, 1.0, 0.0)
        before = jnp.dot(stril_ref[...], multi.astype(BF16), preferred_element_type=F32) + carry
        rank0 = jnp.sum(jnp.where(hot0, before, 0.0), axis=-1, keepdims=True)
        rank1 = jnp.sum(jnp.where(hot1, before, 0.0), axis=-1, keepdims=True)
        carry = carry + jnp.sum(multi, axis=0, keepdims=True)
        meta = jnp.zeros((rows, LANES), F32)
        for idx, val in ((META_E0, i1.astype(F32)), (META_E1, i2.astype(F32)),
                         (META_R0, rank0), (META_R1, rank1), (META_W0, w0), (META_W1, w1)):
            meta = jnp.where(lane == idx, val, meta)
        meta_ref[r, :] = meta
        route_ref[:, r] = meta.T[0:SUBLANES, :]
    carry_scr[...] = carry
    cnt_ref[...] = carry


def _outproj(da2, hg2, x2, mod3, wo_bf, norm_w, w_route, b_route, seq):
    t, d = x2.shape
    tm = OUT_TILE
    per_b = seq // tm
    row = lambda i: (i, 0)
    full = lambda i: (0, 0)
    part = tm // OUT_PARTS
    stril = jnp.asarray(np.tril(np.ones((part, part), np.float32), -1), BF16)
    return pl.pallas_call(
        _outproj_body,
        grid=(t // tm,),
        in_specs=[pl.BlockSpec((tm, STREAM_W), row),
                  pl.BlockSpec((tm, STREAM_W), row),
                  pl.BlockSpec((tm, d), row),
                  pl.BlockSpec((1, 6, d), lambda i: (i // per_b, 0, 0)),
                  pl.BlockSpec((2 * STREAM_W, d), full),
                  pl.BlockSpec((1, d), full),
                  pl.BlockSpec((d, 2 * LANES), full),
                  pl.BlockSpec((1, LANES), full),
                  pl.BlockSpec((part, part), full)],
        out_specs=[pl.BlockSpec((tm, d), row),
                   pl.BlockSpec((tm * SLAB, LANES), row),
                   pl.BlockSpec((tm, LANES), row),
                   pl.BlockSpec((SUBLANES, tm), lambda i: (0, i)),
                   pl.BlockSpec((1, LANES), full)],
        out_shape=[jax.ShapeDtypeStruct((t, d), F32),
                   jax.ShapeDtypeStruct((t * SLAB, LANES), F32),
                   jax.ShapeDtypeStruct((t, LANES), F32),
                   jax.ShapeDtypeStruct((SUBLANES, t), F32),
                   jax.ShapeDtypeStruct((1, LANES), F32)],
        scratch_shapes=[pltpu.VMEM((1, LANES), F32)],
        compiler_params=pltpu.CompilerParams(dimension_semantics=("arbitrary",)),
        name="outproj_route",
    )(da2, hg2, x2, mod3, wo_bf, norm_w, w_route, b_route, stril)


DMA_UNROLL = 8


ZERO_CHUNK = 128
N_FILL_RANGES = N_EXPERTS + 1


def _scatter_body(fill_ref, pos_ref, h2_ref, xs_hbm, zero_scr, sem, zsem, *, tile):
    rows = tile * SLAB

    @pl.when(pl.program_id(0) == 0)
    def _():
        zero_scr[...] = jnp.zeros(zero_scr.shape, F32)

        def clear(chunk):
            start = pl.multiple_of(chunk * (ZERO_CHUNK * SLAB), SLAB)
            return pltpu.make_async_copy(zero_scr, xs_hbm.at[pl.ds(start, ZERO_CHUNK * SLAB)], zsem)

        def start_one(chunk, carry):
            clear(chunk).start()
            return carry

        def wait_one(chunk, carry):
            clear(chunk).wait()
            return carry

        for fn in (start_one, wait_one):
            for r in range(N_FILL_RANGES):
                lax.fori_loop(fill_ref[r], fill_ref[N_FILL_RANGES + r], fn, 0)

    def issue(t, carry):
        src = h2_ref.at[pl.ds(pl.multiple_of(t * SLAB, SLAB), SLAB)]
        for j in range(2):
            slot = pl.multiple_of(pos_ref[0, 0, j * tile + t] * SLAB, SLAB)
            pltpu.make_async_copy(src, xs_hbm.at[pl.ds(slot, SLAB)], sem).start(priority=j)
        return carry

    lax.fori_loop(0, tile, issue, 0, unroll=DMA_UNROLL)
    whole = pltpu.make_async_copy(h2_ref, xs_hbm.at[pl.ds(0, rows)], sem)
    whole.wait()
    whole.wait()


def _scatter(fill_ranges, pos3, h2s, n_slots):
    tile = SCATTER_TILE
    rows = tile * SLAB
    grid_spec = pltpu.PrefetchScalarGridSpec(
        num_scalar_prefetch=1,
        grid=(h2s.shape[0] // rows,),
        in_specs=[pl.BlockSpec((1, 1, 2 * tile), lambda i, lt: (i, 0, 0), memory_space=pltpu.SMEM),
                  pl.BlockSpec((rows, LANES), lambda i, lt: (i, 0))],
        out_specs=pl.BlockSpec(memory_space=pl.ANY),
        scratch_shapes=[pltpu.VMEM((ZERO_CHUNK * SLAB, LANES), F32),
                        pltpu.SemaphoreType.DMA, pltpu.SemaphoreType.DMA],
    )
    return pl.pallas_call(
        functools.partial(_scatter_body, tile=tile),
        grid_spec=grid_spec,
        out_shape=jax.ShapeDtypeStruct((n_slots * SLAB, LANES), F32),
        compiler_params=pltpu.CompilerParams(dimension_semantics=("arbitrary",)),
        name="moe_scatter",
    )(fill_ranges, pos3, h2s)


def _experts_body(te_ref, nv_ref, x_ref, wg_ref, wu_ref, wd_ref, y_ref, wgu_scr, wd_scr, *, tm):
    i = pl.program_id(0)
    e = te_ref[i]
    prev = te_ref[jnp.maximum(i - 1, 0)]

    @pl.when((i == 0) | (e != prev))
    def _():
        wgu_scr[:, 0:EXPERT_FF] = wg_ref[0].astype(BF16)
        wgu_scr[:, EXPERT_FF:] = wu_ref[0].astype(BF16)
        wd_scr[...] = wd_ref[0].astype(BF16)

    @pl.when(i < nv_ref[0])
    def _():
        part = tm // EXPERT_PARTS
        xr = [x_ref.at[pl.ds(r * part * SLAB, part * SLAB)] for r in range(EXPERT_PARTS)]
        yr = [y_ref.at[pl.ds(r * part * SLAB, part * SLAB)] for r in range(EXPERT_PARTS)]
        xs = [_slab_load(ref, part).astype(BF16) for ref in xr]
        gus = [jnp.dot(x, wgu_scr[...], preferred_element_type=F32) for x in xs]
        acts = [(gu[:, 0:EXPERT_FF] * _sigmoid(gu[:, 0:EXPERT_FF]) * gu[:, EXPERT_FF:]).astype(BF16)
                for gu in gus]
        ys = [jnp.dot(act, wd_scr[...], preferred_element_type=F32) for act in acts]
        for ref, y in zip(yr, ys):
            _slab_store(ref, y)

    @pl.when(i >= nv_ref[0])
    def _():
        y_ref[...] = jnp.zeros(y_ref.shape, F32)


def _experts(tile_expert, n_valid, xs, w_gate, w_up, w_down):
    tm = MOE_TILE
    rows = tm * SLAB
    d = w_gate.shape[1]
    live = lambda i, te, nv: (jnp.minimum(i, nv[0] - 1), 0)
    grid_spec = pltpu.PrefetchScalarGridSpec(
        num_scalar_prefetch=2,
        grid=(xs.shape[0] // rows,),
        in_specs=[pl.BlockSpec((rows, LANES), live),
                  pl.BlockSpec((1, d, EXPERT_FF), lambda i, te, nv: (te[i], 0, 0)),
                  pl.BlockSpec((1, d, EXPERT_FF), lambda i, te, nv: (te[i], 0, 0)),
                  pl.BlockSpec((1, EXPERT_FF, d), lambda i, te, nv: (te[i], 0, 0))],
        out_specs=pl.BlockSpec((rows, LANES), lambda i, te, nv: (i, 0)),
        scratch_shapes=[pltpu.VMEM((d, 2 * EXPERT_FF), BF16),
                        pltpu.VMEM((EXPERT_FF, d), BF16)],
    )
    return pl.pallas_call(
        functools.partial(_experts_body, tm=tm),
        grid_spec=grid_spec,
        out_shape=jax.ShapeDtypeStruct(xs.shape, F32),
        compiler_params=pltpu.CompilerParams(dimension_semantics=("arbitrary",)),
        name="moe_experts",
    )(tile_expert, n_valid, xs, w_gate, w_up, w_down)


def _combine_body(pos_ref, nxt_ref, x1_ref, meta_ref, mod_ref, nw_ref, ys_hbm, o_ref, rows_scr, sems,
                  *, tile):
    i = pl.program_id(0)
    n = pl.num_programs(0)
    rows = tile * SLAB

    def fetch(idx_ref, buf):
        def issue(t, carry):
            dst = pl.ds(pl.multiple_of(t * SLAB, SLAB), SLAB)
            for j in range(2):
                slot = pl.multiple_of(idx_ref[0, 0, j * tile + t] * SLAB, SLAB)
                pltpu.make_async_copy(ys_hbm.at[pl.ds(slot, SLAB)], rows_scr.at[buf, j, dst],
                                      sems.at[buf]).start(priority=j)
            return carry
        lax.fori_loop(0, tile, issue, 0, unroll=DMA_UNROLL)

    @pl.when(i == 0)
    def _():
        fetch(pos_ref, 0)

    for cur in range(2):
        @pl.when((i % 2 == cur) & (i + 1 < n))
        def _():
            fetch(nxt_ref, 1 - cur)

    for cur in range(2):
        @pl.when(i % 2 == cur)
        def _():
            for j in range(2):
                pltpu.make_async_copy(ys_hbm.at[pl.ds(0, rows)], rows_scr.at[cur, j],
                                      sems.at[cur]).wait()
            meta = meta_ref[...]
            w0 = meta[:, META_W0:META_W0 + 1]
            w1 = meta[:, META_W1:META_W1 + 1]
            y = (w0 * _slab_load(rows_scr.at[cur, 0], tile)
                 + w1 * _slab_load(rows_scr.at[cur, 1], tile))
            x2 = x1_ref[...] + mod_ref[0, 5:6, :] * y
            o_ref[...] = (x2 * lax.rsqrt(jnp.mean(x2 * x2, axis=-1, keepdims=True) + EPS)
                          * nw_ref[...])


def _combine(pos3, x1, meta, mod3, norm_w, ys, seq):
    t, d = x1.shape
    tile = COMBINE_TILE
    per_b = seq // tile
    n = t // tile
    row = lambda i: (i, 0)
    return pl.pallas_call(
        functools.partial(_combine_body, tile=tile),
        grid=(n,),
        in_specs=[pl.BlockSpec((1, 1, 2 * tile), lambda i: (i, 0, 0), memory_space=pltpu.SMEM),
                  pl.BlockSpec((1, 1, 2 * tile), lambda i: (jnp.minimum(i + 1, n - 1), 0, 0),
                               memory_space=pltpu.SMEM),
                  pl.BlockSpec((tile, d), row),
                  pl.BlockSpec((tile, LANES), row),
                  pl.BlockSpec((1, 6, d), lambda i: (i // per_b, 0, 0)),
                  pl.BlockSpec((1, d), lambda i: (0, 0)),
                  pl.BlockSpec(memory_space=pl.ANY)],
        out_specs=pl.BlockSpec((tile, d), row),
        out_shape=jax.ShapeDtypeStruct((t, d), F32),
        scratch_shapes=[pltpu.VMEM((2, 2, tile * SLAB, LANES), F32), pltpu.SemaphoreType.DMA((2,))],
        compiler_params=pltpu.CompilerParams(dimension_semantics=("arbitrary",)),
        name="moe_combine",
    )(pos3, pos3, x1, meta, mod3, norm_w, ys)


def _rope_constants():
    inv_freq = ROPE_THETA ** (-jnp.arange(ROT_HALF, dtype=F32) / ROT_HALF)
    lane = np.arange(LANES) % DA_QK_DIM
    hit = (lane[None, :] < ROT_DIM) & (lane[None, :] % ROT_HALF == np.arange(ROT_HALF)[:, None])
    spread = np.concatenate([hit, hit], axis=0).astype(np.float32)
    return inv_freq.reshape(ROT_HALF, 1), jnp.asarray(spread, BF16)


def kernel(x, c, positions, norm1_w, norm2_w, final_norm_w, ada_w, ada_b, w_in, w_out, da_lambda_q1, da_lambda_k1, da_lambda_q2, da_lambda_k2, da_subln_w, hg_lower_bound, hg_norm_w, moe_w_group, moe_b_group, moe_w_router, moe_b_router, moe_w_gate, moe_w_up, moe_w_down):
    bsz, seq, d = x.shape
    assert d == D_MODEL and norm1_w.shape[0] == 1, "single-layer model of width 1024 only"
    assert seq % ATTN_TILE == 0 and seq % HGRN_STEP == 0 and seq % OUT_TILE == 0
    t = bsz * seq
    x2 = x.reshape(t, d)

    mod3 = _adaln(c, ada_w[0], ada_b).reshape(bsz, 6, d)

    pos_rows = positions.astype(F32).reshape(t // ROW_TILE, 1, ROW_TILE)
    inv_freq, spread = _rope_constants()
    qt, k, vt, hq, lf, kf, gi, sg = _inproj(x2, mod3, norm1_w, w_in[0], pos_rows,
                                          inv_freq, spread, hg_lower_bound, seq)

    as3 = lambda a: a.reshape(bsz, seq, STREAM_W)
    lam_p = jnp.concatenate([da_lambda_q1, da_lambda_k1, da_lambda_q2, da_lambda_k2], axis=0)
    da = _attn(qt, as3(k), vt, lam_p, da_subln_w.reshape(HEAD_W, 1))
    hg = _hgrn(as3(hq), as3(lf), as3(kf), as3(gi), as3(sg), hg_norm_w)

    pad = jnp.zeros((d, LANES - N_EXPERTS - N_GROUPS), F32)
    w_route = jnp.concatenate([moe_w_router[0], moe_w_group[0], pad], axis=1)
    b_route = jnp.concatenate([moe_b_router[0], moe_b_group[0], pad[0]]).reshape(1, LANES)
    w_route_hi = w_route.astype(BF16)
    w_route_lo = (w_route - w_route_hi.astype(F32)).astype(BF16)
    x1, h2, meta, route, cnt = _outproj(da.reshape(t, STREAM_W), hg.reshape(t, STREAM_W), x2, mod3,
                                 w_out[0].astype(BF16), norm2_w,
                                 jnp.concatenate([w_route_hi, w_route_lo], axis=1), b_route, seq)

    counts = cnt[0, :N_EXPERTS].astype(jnp.int32)
    tiles_e = (counts + MOE_TILE - 1) // MOE_TILE
    tile_end = jnp.cumsum(tiles_e)
    offs = (tile_end - tiles_e) * MOE_TILE
    ids = route[META_E0:META_E1 + 1].astype(jnp.int32)
    ranks = route[META_R0:META_R1 + 1].astype(jnp.int32)
    expert_iota = jnp.arange(N_EXPERTS, dtype=jnp.int32)[:, None, None]
    pos = jnp.sum(jnp.where(ids[None] == expert_iota, offs[:, None, None], 0), axis=0) + ranks
    n_tiles = (2 * t) // MOE_TILE + N_EXPERTS
    n_valid = tile_end[-1:]
    tile_ids = jnp.minimum(jnp.arange(n_tiles, dtype=jnp.int32), n_valid - 1)
    tile_expert = jnp.sum(tile_ids[:, None] >= tile_end[None, :], axis=1).astype(jnp.int32)
    def slot_table(tile):
        return pos.reshape(2, t // tile, tile).transpose(1, 0, 2).reshape(t // tile, 1, 2 * tile)

    fill_lo = jnp.concatenate([(offs + counts) // ZERO_CHUNK, n_valid * (MOE_TILE // ZERO_CHUNK)])
    fill_hi = jnp.concatenate([tile_end * (MOE_TILE // ZERO_CHUNK),
                               jnp.full((1,), n_tiles * (MOE_TILE // ZERO_CHUNK), jnp.int32)])
    fill_ranges = jnp.concatenate([fill_lo, fill_hi]).astype(jnp.int32)

    xs = _scatter(fill_ranges, slot_table(SCATTER_TILE), h2, n_tiles * MOE_TILE)
    ys = _experts(tile_expert, n_valid.astype(jnp.int32), xs, moe_w_gate[0], moe_w_up[0], moe_w_down[0])
    out = _combine(slot_table(COMBINE_TILE), x1, meta, mod3, final_norm_w.reshape(1, d), ys, seq)
    return out.reshape(bsz, seq, d)
```

```python
import functools
import math

import numpy as np
import jax
import jax.numpy as jnp
from jax import lax
from jax.experimental import pallas as pl
from jax.experimental.pallas import tpu as pltpu

F32 = jnp.float32
BF16 = jnp.bfloat16
HIGHEST = lax.Precision.HIGHEST

LANES = 128
SUBLANES = 8
D_MODEL = 1024
HEADS = 4
HEAD_W = 128
STREAM_W = HEADS * HEAD_W
N_STREAMS = 7
DA_QK_DIM = 64
ROPE_THETA = 500000.0
ROT_DIM = DA_QK_DIM // 4
ROT_HALF = ROT_DIM // 2
N_GROUPS = 4
EXPERTS_PER_GROUP = 8
N_EXPERTS = N_GROUPS * EXPERTS_PER_GROUP
EXPERT_FF = 256
EPS = 1e-6
LAM_INIT = 0.8 - 0.6 * math.exp(-0.3 * 0)
NEG_BIG = -1e30

ROW_TILE = 512
OUT_TILE = 1024
ATTN_TILE = 1024
ATTN_KEY_TILE = 256
HGRN_BLOCK = 128
HGRN_STEP = 256
MOE_TILE = 512
EXPERT_PARTS = 2
SCATTER_TILE = 1024
COMBINE_TILE = 256

NT_DIMS = (((1,), (1,)), ((), ()))
TN_DIMS = (((0,), (0,)), ((), ()))


def _sigmoid(x):
    return 1.0 / (1.0 + jnp.exp(-x))


SLAB = D_MODEL // LANES


def _slab_store(ref, val):
    rows = val.shape[0]
    for c in range(SLAB):
        ref[pl.ds(c, rows, stride=SLAB), :] = val[:, c * LANES:(c + 1) * LANES]


def _slab_load(ref, rows):
    return jnp.concatenate([ref[pl.ds(c, rows, stride=SLAB), :] for c in range(SLAB)], axis=1)


def _slab_store_bf16(ref, val, piece_scr):
    _slab_store(piece_scr, val)
    ref[...] = piece_scr[...].astype(BF16)


def _slab_load_bf16(ref, rows, piece_scr):
    piece_scr[...] = ref[...].astype(F32)
    return _slab_load(piece_scr, rows)


def _adaln_body(c_ref, w_ref, b_ref, o_ref):
    c = c_ref[...]
    ca = c * _sigmoid(c)
    o_ref[...] = jnp.dot(ca, w_ref[...], preferred_element_type=F32, precision=HIGHEST) + b_ref[...]


def _adaln(c, ada_w, ada_b):
    bsz, d = c.shape
    n = ada_w.shape[1]
    tn = 1024
    return pl.pallas_call(
        _adaln_body,
        grid=(n // tn,),
        in_specs=[pl.BlockSpec((bsz, d), lambda j: (0, 0)),
                  pl.BlockSpec((d, tn), lambda j: (0, j)),
                  pl.BlockSpec((1, tn), lambda j: (0, j))],
        out_specs=pl.BlockSpec((bsz, tn), lambda j: (0, j)),
        out_shape=jax.ShapeDtypeStruct((bsz, n), F32),
        name="adaln",
    )(c, ada_w, ada_b)


def _inproj_body(x_ref, mod_ref, nw_ref, w32_ref, pos_ref, invf_ref, spread_ref, lbr_ref,
                 q_ref, k_ref, v_ref, hq_ref, lf_ref, kf_ref, gi_ref, sg_ref, w_ref):
    @pl.when(pl.program_id(0) == 0)
    def _():
        for j in range(N_STREAMS):
            cols = slice(j * STREAM_W, (j + 1) * STREAM_W)
            w_ref[:, cols] = w32_ref[:, cols].astype(BF16)

    x = x_ref[...]
    ms = jnp.mean(x * x, axis=-1, keepdims=True)
    y = x * lax.rsqrt(ms + EPS) * nw_ref[...]
    shift = mod_ref[0, 0:1, :]
    scale = mod_ref[0, 1:2, :]
    h = (y * (1.0 + scale) + shift).astype(BF16)

    ang_t = invf_ref[...] * pos_ref[0]

    def spread(table_t):
        hi = table_t.astype(BF16)
        lo = (table_t - hi.astype(F32)).astype(BF16)
        return lax.dot_general(jnp.concatenate([hi, lo], axis=0), spread_ref[...], TN_DIMS,
                               preferred_element_type=F32)

    lane = lax.broadcasted_iota(jnp.int32, (1, LANES), 1) % DA_QK_DIM
    cosv = spread(jnp.cos(ang_t)) + jnp.where(lane < ROT_DIM, 0.0, 1.0)
    sinv = spread(jnp.sin(ang_t))
    sin_lo = jnp.where(lane < ROT_HALF, -sinv, 0.0)
    sin_hi = jnp.where((lane >= ROT_HALF) & (lane < ROT_DIM), sinv, 0.0)

    streams = [jnp.dot(h, w_ref[:, j * STREAM_W:(j + 1) * STREAM_W], preferred_element_type=F32)
               for j in range(N_STREAMS)]

    def proj(j):
        return streams[j]

    def rope(t):
        outs = []
        for hb in range(HEADS):
            tc = t[:, hb * HEAD_W:(hb + 1) * HEAD_W]
            outs.append(tc * cosv
                        + pltpu.roll(tc, LANES - ROT_HALF, 1) * sin_lo
                        + pltpu.roll(tc, ROT_HALF, 1) * sin_hi)
        return jnp.concatenate(outs, axis=1)

    q_ref[0] = (rope(proj(0)) * (DA_QK_DIM ** -0.5 * math.log2(math.e))).T.astype(BF16)
    k_ref[...] = rope(proj(1)).astype(BF16)
    v_ref[0] = proj(2).astype(BF16).T

    gq = proj(3)
    hq_ref[...] = gq * _sigmoid(gq)

    a = lbr_ref[...]
    amax = jnp.max(a, axis=0, keepdims=True)
    ea = jnp.exp(a - amax)
    lb = ea[0:1, :] / jnp.sum(ea, axis=0, keepdims=True)
    gf = proj(4)
    f = lb + (1.0 - lb) * _sigmoid(gf)
    lf_ref[...] = jnp.log(f) * math.log2(math.e)
    kf_ref[...] = 1.0 - f

    gi_ref[...] = proj(5).astype(BF16)
    gg = proj(6)
    sg_ref[...] = (gg * _sigmoid(gg)).astype(BF16)


def _inproj(x2, mod3, norm_w, w_f32, pos_rows, invf, spread, lb_raw, seq):
    t, d = x2.shape
    tm = ROW_TILE
    per_b = seq // tm
    row = lambda i: (i, 0)
    full = lambda i: (0, 0)
    out_bf = jax.ShapeDtypeStruct((t, STREAM_W), BF16)
    out_f = jax.ShapeDtypeStruct((t, STREAM_W), F32)
    out_t = jax.ShapeDtypeStruct((t // seq, STREAM_W, seq), BF16)
    stream = pl.BlockSpec((tm, STREAM_W), row)
    stream_t = pl.BlockSpec((1, STREAM_W, tm), lambda i: (i // per_b, 0, i % per_b))
    return pl.pallas_call(
        _inproj_body,
        grid=(t // tm,),
        in_specs=[pl.BlockSpec((tm, d), row),
                  pl.BlockSpec((1, 6, d), lambda i: (i // per_b, 0, 0)),
                  pl.BlockSpec((1, d), full),
                  pl.BlockSpec((d, N_STREAMS * STREAM_W), full, pipeline_mode=pl.Buffered(1)),
                  pl.BlockSpec((1, 1, tm), lambda i: (i, 0, 0)),
                  pl.BlockSpec((ROT_HALF, 1), full),
                  pl.BlockSpec((2 * ROT_HALF, LANES), full),
                  pl.BlockSpec(lb_raw.shape, full)],
        out_specs=[stream_t, stream, stream_t] + [stream] * 5,
        out_shape=[out_t, out_bf, out_t, out_f, out_f, out_f, out_bf, out_bf],
        scratch_shapes=[pltpu.VMEM((d, N_STREAMS * STREAM_W), BF16)],
        compiler_params=pltpu.CompilerParams(dimension_semantics=("arbitrary",),
                                             vmem_limit_bytes=56 * 1024 * 1024),
        name="inproj",
    )(x2, mod3, norm_w, w_f32, pos_rows, invf, spread, lb_raw)


ONES_ROWS = 16


def _attn_body(qt_ref, k_ref, vt_ref, lam_ref, sw_ref, o_ref, s_scr, m_scr, a_scr, *, tile, ktile):
    qi = pl.program_id(2)
    qt = qt_ref[0]
    feat = lax.broadcasted_iota(jnp.int32, (HEAD_W, 1), 0)
    zero = jnp.zeros_like(qt)
    qmaps = (jnp.where(feat < DA_QK_DIM, qt, zero), jnp.where(feat >= DA_QK_DIM, qt, zero))

    m_scr[...] = jnp.full(m_scr.shape, NEG_BIG, F32)
    a_scr[...] = jnp.zeros(a_scr.shape, F32)

    def score_block(start, nkeys, q_lo, masked):
        kb = k_ref[0, pl.ds(start, nkeys), :]
        scores = [jnp.dot(kb, qmaps[mp][:, q_lo:], preferred_element_type=F32) for mp in range(2)]
        for mp, s in enumerate(scores):
            if masked:
                key = lax.broadcasted_iota(jnp.int32, s.shape, 0)
                qry = lax.broadcasted_iota(jnp.int32, s.shape, 1)
                s = jnp.where(key <= qry, s, NEG_BIG)
            grouped = s.reshape(nkeys // SUBLANES, SUBLANES, tile - q_lo)
            m_scr[mp, :, q_lo:] = jnp.maximum(m_scr[mp, :, q_lo:], jnp.max(grouped, axis=0))
            s_scr[mp, pl.ds(start, nkeys), q_lo:] = s

    def value_block(start, nkeys, q_lo, col_max):
        vb = jnp.concatenate([vt_ref[0, :, pl.ds(start, nkeys)], jnp.ones((ONES_ROWS, nkeys), BF16)],
                             axis=0)
        probs = [jnp.exp2(s_scr[mp, pl.ds(start, nkeys), q_lo:] - col_max[mp][:, q_lo:]).astype(BF16)
                 for mp in range(2)]
        for mp, p in enumerate(probs):
            a_scr[mp, :, q_lo:] += jnp.dot(vb, p, preferred_element_type=F32)

    diag = pl.multiple_of(qi * tile, tile)
    pieces = [(diag + j * ktile, ktile, j * ktile) for j in range(tile // ktile)]

    def score_body(ki, carry):
        score_block(pl.multiple_of(ki * tile, tile), tile, 0, False)
        return carry

    lax.fori_loop(0, qi, score_body, 0)
    for start, nkeys, q_lo in pieces:
        score_block(start, nkeys, q_lo, True)
    col_max = [jnp.max(m_scr[mp], axis=0, keepdims=True) for mp in range(2)]

    def value_body(ki, carry):
        value_block(pl.multiple_of(ki * tile, tile), tile, 0, col_max)
        return carry

    lax.fori_loop(0, qi, value_body, 0)
    for start, nkeys, q_lo in pieces:
        value_block(start, nkeys, q_lo, col_max)

    lp = lam_ref[...]
    lam = (jnp.exp(jnp.sum(lp[0:1] * lp[1:2], axis=-1, keepdims=True))
           - jnp.exp(jnp.sum(lp[2:3] * lp[3:4], axis=-1, keepdims=True)) + LAM_INIT)
    o = (a_scr[0, 0:HEAD_W, :] / a_scr[0, HEAD_W:HEAD_W + 1, :]
         - lam * (a_scr[1, 0:HEAD_W, :] / a_scr[1, HEAD_W:HEAD_W + 1, :]))
    o = o * lax.rsqrt(jnp.mean(o * o, axis=0, keepdims=True) + EPS) * sw_ref[...]
    o_ref[0] = (o * (1.0 - LAM_INIT)).T.astype(BF16)


def _attn(qt3, k3, vt3, lam_p, subln_col):
    bsz, seq, _ = k3.shape
    tile = ATTN_TILE
    return pl.pallas_call(
        functools.partial(_attn_body, tile=tile, ktile=ATTN_KEY_TILE),
        grid=(bsz, HEADS, seq // tile),
        in_specs=[pl.BlockSpec((1, HEAD_W, tile), lambda b, h, i: (b, h, i)),
                  pl.BlockSpec((1, seq, HEAD_W), lambda b, h, i: (b, 0, h)),
                  pl.BlockSpec((1, HEAD_W, seq), lambda b, h, i: (b, h, 0)),
                  pl.BlockSpec(lam_p.shape, lambda b, h, i: (0, 0)),
                  pl.BlockSpec((HEAD_W, 1), lambda b, h, i: (0, 0))],
        out_specs=pl.BlockSpec((1, tile, HEAD_W), lambda b, h, i: (b, i, h)),
        out_shape=jax.ShapeDtypeStruct((bsz, seq, STREAM_W), BF16),
        scratch_shapes=[pltpu.VMEM((2, seq, tile), F32),
                        pltpu.VMEM((2, SUBLANES, tile), F32),
                        pltpu.VMEM((2, HEAD_W + ONES_ROWS, tile), F32)],
        compiler_params=pltpu.CompilerParams(vmem_limit_bytes=56 * 1024 * 1024),
        name="diff_attn",
    )(qt3, k3, vt3, lam_p, subln_col)


def _hgrn_levels(block):
    return [block >> (i + 1) for i in range(block.bit_length() - 1)]


def _hgrn_constants(block):
    t = np.arange(block)[:, None]
    s = np.arange(block)[None, :]
    tril = (s <= t).astype(np.float32)
    lv = np.full((block, block), -1, np.int32)
    halves = _hgrn_levels(block)
    for li, m in enumerate(halves):
        same = (t // (2 * m)) == (s // (2 * m))
        lv[same & ((t & m) != 0) & ((s & m) == 0)] = li
    lv[np.arange(block), np.arange(block)] = len(halves)
    return jnp.asarray(tril, BF16), jnp.asarray(lv)


def _level_operand(b_ref, h, b, q, k, m, block):
    def ref_rows(r, n):
        return jnp.broadcast_to(b_ref[h, pl.ds(r, 1), :], (n, HEAD_W))

    if m >= SUBLANES:
        pieces = []
        for s0 in range(0, block, 2 * m):
            ref = ref_rows(s0 + m - 1, m)
            lo = slice(s0, s0 + m)
            up = slice(s0 + m, s0 + 2 * m)
            pieces.append(k[lo] * jnp.exp2(ref - b[lo]))
            pieces.append(q[up] * jnp.exp2(b[up] - ref))
        return jnp.concatenate(pieces, axis=0)

    sub = lax.broadcasted_iota(jnp.int32, (SUBLANES, HEAD_W), 0)
    refs = []
    for s0 in range(0, block, SUBLANES):
        piece = ref_rows(s0 + m - 1, SUBLANES)
        for j in range(1, SUBLANES // (2 * m)):
            piece = jnp.where(sub >= 2 * m * j, ref_rows(s0 + 2 * m * j + m - 1, SUBLANES), piece)
        refs.append(piece)
    d = b - jnp.concatenate(refs, axis=0)
    row = lax.broadcasted_iota(jnp.int32, (block, 1), 0)
    return jnp.where((row & m) != 0, q, k) * jnp.exp2(jnp.minimum(d, -d))


def _hgrn_body(hq_ref, lf_ref, kf_ref, gi_ref, sg_ref, nw_ref, tril_ref, lv_ref, o_ref,
               st_scr, b_scr, *, block, step):
    @pl.when(pl.program_id(1) == 0)
    def _():
        st_scr[...] = jnp.zeros(st_scr.shape, F32)

    tril = tril_ref[...]
    lv = lv_ref[...]
    halves = _hgrn_levels(block)

    units = [(u, r0, h) for u, (r0, h) in enumerate(
        (r0, h) for r0 in range(0, step, block) for h in range(HEADS))]

    def cols(h):
        return slice(h * HEAD_W, (h + 1) * HEAD_W)

    q, k, v, b, scores = {}, {}, {}, {}, {}
    for u, r0, h in units:
        rows = slice(r0, r0 + block)
        q[u] = hq_ref[0, rows, cols(h)]
        k[u] = kf_ref[0, rows, cols(h)]
        v[u] = gi_ref[0, rows, cols(h)]
        lf = lf_ref[0, rows, cols(h)]
        hi = lf.astype(BF16)
        r1 = lf - hi.astype(F32)
        mid = r1.astype(BF16)
        lo = (r1 - mid.astype(F32)).astype(BF16)
        b[u] = (jnp.dot(tril, hi, preferred_element_type=F32)
                + jnp.dot(tril, mid, preferred_element_type=F32)
                + jnp.dot(tril, lo, preferred_element_type=F32))
        b_scr[u] = b[u]
        scores[u] = jnp.where(lv == len(halves),
                              lax.dot_general(q[u].astype(BF16), k[u].astype(BF16), NT_DIMS,
                                              preferred_element_type=F32), 0.0)

    for li, m in enumerate(halves):
        for u, r0, h in units:
            xl = _level_operand(b_scr, u, b[u], q[u], k[u], m, block).astype(BF16)
            p = lax.dot_general(xl, xl, NT_DIMS, preferred_element_type=F32)
            scores[u] = jnp.where(lv == li, p, scores[u])

    for u, r0, h in units:
        rows = slice(r0, r0 + block)
        o_intra = jnp.dot(scores[u].astype(BF16), v[u], preferred_element_type=F32)
        st = st_scr[h]
        o_inter = lax.dot_general((q[u] * jnp.exp2(b[u])).astype(BF16), st.astype(BF16), NT_DIMS,
                                  preferred_element_type=F32)
        b_last = b[u][block - 1:block, :]
        kdec = (k[u] * jnp.exp2(b_last - b[u])).astype(BF16)
        st_scr[h] = st * jnp.exp2(b_last) + lax.dot_general(v[u], kdec, TN_DIMS,
                                                             preferred_element_type=F32)
        o = o_inter + o_intra
        o = o * lax.rsqrt(jnp.mean(o * o, axis=-1, keepdims=True) + EPS) * nw_ref[...]
        o_ref[0, rows, cols(h)] = (o * sg_ref[0, rows, cols(h)].astype(F32)).astype(BF16)


def _hgrn(hq3, lf3, kf3, gi3, sg3, norm_w):
    bsz, seq, _ = hq3.shape
    block = HGRN_BLOCK
    step = HGRN_STEP
    tril, lv = _hgrn_constants(block)
    blk = pl.BlockSpec((1, step, STREAM_W), lambda b, g: (b, g, 0))
    const = lambda b, g: (0, 0)
    return pl.pallas_call(
        functools.partial(_hgrn_body, block=block, step=step),
        grid=(bsz, seq // step),
        in_specs=[blk, blk, blk, blk, blk,
                  pl.BlockSpec((1, HEAD_W), const),
                  pl.BlockSpec((block, block), const),
                  pl.BlockSpec((block, block), const)],
        out_specs=blk,
        out_shape=jax.ShapeDtypeStruct((bsz, seq, STREAM_W), BF16),
        scratch_shapes=[pltpu.VMEM((HEADS, HEAD_W, HEAD_W), F32),
                        pltpu.VMEM((HEADS * step // block, block, HEAD_W), F32)],
        name="hgrn2",
    )(hq3, lf3, kf3, gi3, sg3, norm_w, tril, lv)


META_E0, META_E1, META_R0, META_R1, META_W0, META_W1 = range(6)
GROUP_LANE0 = N_EXPERTS


OUT_PARTS = 8


def _route(logits, lane):
    far = jnp.int32(LANES)

    def first_max(vals):
        mx = jnp.max(vals, axis=-1, keepdims=True)
        return mx, jnp.min(jnp.where(vals == mx, lane, far), axis=-1, keepdims=True)

    is_g = (lane >= GROUP_LANE0) & (lane < GROUP_LANE0 + N_GROUPS)
    gmax, glane = first_max(jnp.where(is_g, logits, NEG_BIG))
    g_w = 1.0 / jnp.sum(jnp.where(is_g, jnp.exp(logits - gmax), 0.0), axis=-1, keepdims=True)
    gidx = glane - GROUP_LANE0
    in_grp = (lane < N_EXPERTS) & ((lane // EXPERTS_PER_GROUP) == gidx)
    el = jnp.where(in_grp, logits, NEG_BIG)
    m1, i1 = first_max(el)
    m2, i2 = first_max(jnp.where(lane == i1, NEG_BIG, el))
    r = jnp.exp(m2 - m1)
    return i1, i2, g_w / (1.0 + r), g_w * r / (1.0 + r)


def _outproj_body(da_ref, hg_ref, x_ref, mod_ref, wo_ref, nw_ref, wr_ref, br_ref, stril_ref,
                  x1_ref, h2_ref, meta_ref, route_ref, cnt_ref, carry_scr, piece_scr):
    @pl.when(pl.program_id(0) == 0)
    def _():
        carry_scr[...] = jnp.zeros(carry_scr.shape, F32)

    rows = stril_ref.shape[0]
    parts = [slice(p * rows, (p + 1) * rows) for p in range(OUT_PARTS)]
    gate1 = mod_ref[0, 2:3, :]
    shift2 = mod_ref[0, 3:4, :]
    scale2 = mod_ref[0, 4:5, :]

    attn = [jnp.dot(da_ref[r, :], wo_ref[0:STREAM_W, :], preferred_element_type=F32)
            + jnp.dot(hg_ref[r, :], wo_ref[STREAM_W:, :], preferred_element_type=F32) for r in parts]

    h2 = []
    for r, a in zip(parts, attn):
        x1 = x_ref[r, :] + gate1 * a
        x1_ref[r, :] = x1
        h2.append(x1 * lax.rsqrt(jnp.mean(x1 * x1, axis=-1, keepdims=True) + EPS) * nw_ref[...]
                  * (1.0 + scale2) + shift2)

    logits = []
    for p, h in enumerate(h2):
        _slab_store_bf16(h2_ref.at[pl.ds(p * rows * SLAB, rows * SLAB)], h, piece_scr.at[p])
        h_hi = h.astype(BF16)
        h_lo = (h - h_hi.astype(F32)).astype(BF16)
        terms = jnp.dot(jnp.concatenate([h_hi, h_lo], axis=0), wr_ref[...],
                        preferred_element_type=F32)
        logits.append((terms[:rows, :LANES] + terms[:rows, LANES:])
                      + (terms[rows:, :LANES] + terms[rows:, LANES:]) + br_ref[...])

    lane = lax.broadcasted_iota(jnp.int32, (rows, LANES), 1)
    routed = [_route(lg, lane) for lg in logits]

    carry = carry_scr[...]
    for r, (i1, i2, w0, w1) in zip(parts, routed):
        hot0 = lane == i1
        hot1 = lane == i2
        multi = jnp.where(hot0 | hot1, 1.0, 0.0)
        before = jnp.dot(stril_ref[...], multi.astype(BF16), preferred_element_type=F32) + carry
        rank0 = jnp.sum(jnp.where(hot0, before, 0.0), axis=-1, keepdims=True)
        rank1 = jnp.sum(jnp.where(hot1, before, 0.0), axis=-1, keepdims=True)
        carry = carry + jnp.sum(multi, axis=0, keepdims=True)
        meta = jnp.zeros((rows, LANES), F32)
        for idx, val in ((META_E0, i1.astype(F32)), (META_E1, i2.astype(F32)),
                         (META_R0, rank0), (META_R1, rank1), (META_W0, w0), (META_W1, w1)):
            meta = jnp.where(lane == idx, val, meta)
        meta_ref[r, :] = meta
        route_ref[:, r] = meta.T[0:SUBLANES, :]
    carry_scr[...] = carry
    cnt_ref[...] = carry


def _outproj(da2, hg2, x2, mod3, wo_bf, norm_w, w_route, b_route, seq):
    t, d = x2.shape
    tm = OUT_TILE
    per_b = seq // tm
    row = lambda i: (i, 0)
    full = lambda i: (0, 0)
    part = tm // OUT_PARTS
    stril = jnp.asarray(np.tril(np.ones((part, part), np.float32), -1), BF16)
    return pl.pallas_call(
        _outproj_body,
        grid=(t // tm,),
        in_specs=[pl.BlockSpec((tm, STREAM_W), row),
                  pl.BlockSpec((tm, STREAM_W), row),
                  pl.BlockSpec((tm, d), row),
                  pl.BlockSpec((1, 6, d), lambda i: (i // per_b, 0, 0)),
                  pl.BlockSpec((2 * STREAM_W, d), full),
                  pl.BlockSpec((1, d), full),
                  pl.BlockSpec((d, 2 * LANES), full),
                  pl.BlockSpec((1, LANES), full),
                  pl.BlockSpec((part, part), full)],
        out_specs=[pl.BlockSpec((tm, d), row),
                   pl.BlockSpec((tm * SLAB, LANES), row),
                   pl.BlockSpec((tm, LANES), row),
                   pl.BlockSpec((SUBLANES, tm), lambda i: (0, i)),
                   pl.BlockSpec((1, LANES), full)],
        out_shape=[jax.ShapeDtypeStruct((t, d), F32),
                   jax.ShapeDtypeStruct((t * SLAB, LANES), BF16),
                   jax.ShapeDtypeStruct((t, LANES), F32),
                   jax.ShapeDtypeStruct((SUBLANES, t), F32),
                   jax.ShapeDtypeStruct((1, LANES), F32)],
        scratch_shapes=[pltpu.VMEM((1, LANES), F32),
                        pltpu.VMEM((OUT_PARTS, part * SLAB, LANES), F32)],
        compiler_params=pltpu.CompilerParams(dimension_semantics=("arbitrary",)),
        name="outproj_route",
    )(da2, hg2, x2, mod3, wo_bf, norm_w, w_route, b_route, stril)


DMA_UNROLL = 8


ZERO_CHUNK = 128
N_FILL_RANGES = N_EXPERTS + 1


def _scatter_body(fill_ref, pos_ref, h2_ref, xs_hbm, zero_scr, sem, zsem, *, tile):
    rows = tile * SLAB

    @pl.when(pl.program_id(0) == 0)
    def _():
        zero_scr[...] = jnp.zeros(zero_scr.shape, BF16)

        def clear(chunk):
            start = pl.multiple_of(chunk * (ZERO_CHUNK * SLAB), SLAB)
            return pltpu.make_async_copy(zero_scr, xs_hbm.at[pl.ds(start, ZERO_CHUNK * SLAB)], zsem)

        def start_one(chunk, carry):
            clear(chunk).start()
            return carry

        def wait_one(chunk, carry):
            clear(chunk).wait()
            return carry

        for fn in (start_one, wait_one):
            for r in range(N_FILL_RANGES):
                lax.fori_loop(fill_ref[r], fill_ref[N_FILL_RANGES + r], fn, 0)

    def issue(t, carry):
        src = h2_ref.at[pl.ds(pl.multiple_of(t * SLAB, SLAB), SLAB)]
        for j in range(2):
            slot = pl.multiple_of(pos_ref[0, 0, j * tile + t] * SLAB, SLAB)
            pltpu.make_async_copy(src, xs_hbm.at[pl.ds(slot, SLAB)], sem).start(priority=j)
        return carry

    lax.fori_loop(0, tile, issue, 0, unroll=DMA_UNROLL)
    whole = pltpu.make_async_copy(h2_ref, xs_hbm.at[pl.ds(0, rows)], sem)
    whole.wait()
    whole.wait()


def _scatter(fill_ranges, pos3, h2s, n_slots):
    tile = SCATTER_TILE
    rows = tile * SLAB
    grid_spec = pltpu.PrefetchScalarGridSpec(
        num_scalar_prefetch=1,
        grid=(h2s.shape[0] // rows,),
        in_specs=[pl.BlockSpec((1, 1, 2 * tile), lambda i, lt: (i, 0, 0), memory_space=pltpu.SMEM),
                  pl.BlockSpec((rows, LANES), lambda i, lt: (i, 0))],
        out_specs=pl.BlockSpec(memory_space=pl.ANY),
        scratch_shapes=[pltpu.VMEM((ZERO_CHUNK * SLAB, LANES), BF16),
                        pltpu.SemaphoreType.DMA, pltpu.SemaphoreType.DMA],
    )
    return pl.pallas_call(
        functools.partial(_scatter_body, tile=tile),
        grid_spec=grid_spec,
        out_shape=jax.ShapeDtypeStruct((n_slots * SLAB, LANES), BF16),
        compiler_params=pltpu.CompilerParams(dimension_semantics=("arbitrary",)),
        name="moe_scatter",
    )(fill_ranges, pos3, h2s)


def _experts_body(te_ref, nv_ref, x_ref, wg_ref, wu_ref, wd_ref, y_ref, wgu_scr, wd_scr, piece_scr,
                  *, tm):
    i = pl.program_id(0)
    e = te_ref[i]
    prev = te_ref[jnp.maximum(i - 1, 0)]

    @pl.when((i == 0) | (e != prev))
    def _():
        wgu_scr[:, 0:EXPERT_FF] = wg_ref[0].astype(BF16)
        wgu_scr[:, EXPERT_FF:] = wu_ref[0].astype(BF16)
        wd_scr[...] = wd_ref[0].astype(BF16)

    @pl.when(i < nv_ref[0])
    def _():
        part = tm // EXPERT_PARTS
        xr = [x_ref.at[pl.ds(r * part * SLAB, part * SLAB)] for r in range(EXPERT_PARTS)]
        yr = [y_ref.at[pl.ds(r * part * SLAB, part * SLAB)] for r in range(EXPERT_PARTS)]
        xs = [_slab_load_bf16(ref, part, piece_scr.at[r]).astype(BF16) for r, ref in enumerate(xr)]
        gus = [jnp.dot(x, wgu_scr[...], preferred_element_type=F32) for x in xs]
        acts = [(gu[:, 0:EXPERT_FF] * _sigmoid(gu[:, 0:EXPERT_FF]) * gu[:, EXPERT_FF:]).astype(BF16)
                for gu in gus]
        ys = [jnp.dot(act, wd_scr[...], preferred_element_type=F32) for act in acts]
        for ref, y in zip(yr, ys):
            _slab_store(ref, y)

    @pl.when(i >= nv_ref[0])
    def _():
        y_ref[...] = jnp.zeros(y_ref.shape, F32)


def _experts(tile_expert, n_valid, xs, w_gate, w_up, w_down):
    tm = MOE_TILE
    rows = tm * SLAB
    d = w_gate.shape[1]
    live = lambda i, te, nv: (jnp.minimum(i, nv[0] - 1), 0)
    grid_spec = pltpu.PrefetchScalarGridSpec(
        num_scalar_prefetch=2,
        grid=(xs.shape[0] // rows,),
        in_specs=[pl.BlockSpec((rows, LANES), live),
                  pl.BlockSpec((1, d, EXPERT_FF), lambda i, te, nv: (te[i], 0, 0)),
                  pl.BlockSpec((1, d, EXPERT_FF), lambda i, te, nv: (te[i], 0, 0)),
                  pl.BlockSpec((1, EXPERT_FF, d), lambda i, te, nv: (te[i], 0, 0))],
        out_specs=pl.BlockSpec((rows, LANES), lambda i, te, nv: (i, 0)),
        scratch_shapes=[pltpu.VMEM((d, 2 * EXPERT_FF), BF16),
                        pltpu.VMEM((EXPERT_FF, d), BF16),
                        pltpu.VMEM((EXPERT_PARTS, rows // EXPERT_PARTS, LANES), F32)],
    )
    return pl.pallas_call(
        functools.partial(_experts_body, tm=tm),
        grid_spec=grid_spec,
        out_shape=jax.ShapeDtypeStruct(xs.shape, F32),
        compiler_params=pltpu.CompilerParams(dimension_semantics=("arbitrary",)),
        name="moe_experts",
    )(tile_expert, n_valid, xs, w_gate, w_up, w_down)


def _combine_body(pos_ref, nxt_ref, x1_ref, meta_ref, mod_ref, nw_ref, ys_hbm, o_ref, rows_scr, sems,
                  *, tile):
    i = pl.program_id(0)
    n = pl.num_programs(0)
    rows = tile * SLAB

    def fetch(idx_ref, buf):
        def issue(t, carry):
            dst = pl.ds(pl.multiple_of(t * SLAB, SLAB), SLAB)
            for j in range(2):
                slot = pl.multiple_of(idx_ref[0, 0, j * tile + t] * SLAB, SLAB)
                pltpu.make_async_copy(ys_hbm.at[pl.ds(slot, SLAB)], rows_scr.at[buf, j, dst],
                                      sems.at[buf]).start(priority=j)
            return carry
        lax.fori_loop(0, tile, issue, 0, unroll=DMA_UNROLL)

    @pl.when(i == 0)
    def _():
        fetch(pos_ref, 0)

    for cur in range(2):
        @pl.when((i % 2 == cur) & (i + 1 < n))
        def _():
            fetch(nxt_ref, 1 - cur)

    for cur in range(2):
        @pl.when(i % 2 == cur)
        def _():
            for j in range(2):
                pltpu.make_async_copy(ys_hbm.at[pl.ds(0, rows)], rows_scr.at[cur, j],
                                      sems.at[cur]).wait()
            meta = meta_ref[...]
            w0 = meta[:, META_W0:META_W0 + 1]
            w1 = meta[:, META_W1:META_W1 + 1]
            y = (w0 * _slab_load(rows_scr.at[cur, 0], tile)
                 + w1 * _slab_load(rows_scr.at[cur, 1], tile))
            x2 = x1_ref[...] + mod_ref[0, 5:6, :] * y
            o_ref[...] = (x2 * lax.rsqrt(jnp.mean(x2 * x2, axis=-1, keepdims=True) + EPS)
                          * nw_ref[...])


def _combine(pos3, x1, meta, mod3, norm_w, ys, seq):
    t, d = x1.shape
    tile = COMBINE_TILE
    per_b = seq // tile
    n = t // tile
    row = lambda i: (i, 0)
    return pl.pallas_call(
        functools.partial(_combine_body, tile=tile),
        grid=(n,),
        in_specs=[pl.BlockSpec((1, 1, 2 * tile), lambda i: (i, 0, 0), memory_space=pltpu.SMEM),
                  pl.BlockSpec((1, 1, 2 * tile), lambda i: (jnp.minimum(i + 1, n - 1), 0, 0),
                               memory_space=pltpu.SMEM),
                  pl.BlockSpec((tile, d), row),
                  pl.BlockSpec((tile, LANES), row),
                  pl.BlockSpec((1, 6, d), lambda i: (i // per_b, 0, 0)),
                  pl.BlockSpec((1, d), lambda i: (0, 0)),
                  pl.BlockSpec(memory_space=pl.ANY)],
        out_specs=pl.BlockSpec((tile, d), row),
        out_shape=jax.ShapeDtypeStruct((t, d), F32),
        scratch_shapes=[pltpu.VMEM((2, 2, tile * SLAB, LANES), F32), pltpu.SemaphoreType.DMA((2,))],
        compiler_params=pltpu.CompilerParams(dimension_semantics=("arbitrary",)),
        name="moe_combine",
    )(pos3, pos3, x1, meta, mod3, norm_w, ys)


def _rope_constants():
    inv_freq = ROPE_THETA ** (-jnp.arange(ROT_HALF, dtype=F32) / ROT_HALF)
    lane = np.arange(LANES) % DA_QK_DIM
    hit = (lane[None, :] < ROT_DIM) & (lane[None, :] % ROT_HALF == np.arange(ROT_HALF)[:, None])
    spread = np.concatenate([hit, hit], axis=0).astype(np.float32)
    return inv_freq.reshape(ROT_HALF, 1), jnp.asarray(spread, BF16)


def kernel(x, c, positions, norm1_w, norm2_w, final_norm_w, ada_w, ada_b, w_in, w_out, da_lambda_q1, da_lambda_k1, da_lambda_q2, da_lambda_k2, da_subln_w, hg_lower_bound, hg_norm_w, moe_w_group, moe_b_group, moe_w_router, moe_b_router, moe_w_gate, moe_w_up, moe_w_down):
    bsz, seq, d = x.shape
    assert d == D_MODEL and norm1_w.shape[0] == 1, "single-layer model of width 1024 only"
    assert seq % ATTN_TILE == 0 and seq % HGRN_STEP == 0 and seq % OUT_TILE == 0
    t = bsz * seq
    x2 = x.reshape(t, d)

    mod3 = _adaln(c, ada_w[0], ada_b).reshape(bsz, 6, d)

    pos_rows = positions.astype(F32).reshape(t // ROW_TILE, 1, ROW_TILE)
    inv_freq, spread = _rope_constants()
    qt, k, vt, hq, lf, kf, gi, sg = _inproj(x2, mod3, norm1_w, w_in[0], pos_rows,
                                          inv_freq, spread, hg_lower_bound, seq)

    as3 = lambda a: a.reshape(bsz, seq, STREAM_W)
    lam_p = jnp.concatenate([da_lambda_q1, da_lambda_k1, da_lambda_q2, da_lambda_k2], axis=0)
    da = _attn(qt, as3(k), vt, lam_p, da_subln_w.reshape(HEAD_W, 1))
    hg = _hgrn(as3(hq), as3(lf), as3(kf), as3(gi), as3(sg), hg_norm_w)

    pad = jnp.zeros((d, LANES - N_EXPERTS - N_GROUPS), F32)
    w_route = jnp.concatenate([moe_w_router[0], moe_w_group[0], pad], axis=1)
    b_route = jnp.concatenate([moe_b_router[0], moe_b_group[0], pad[0]]).reshape(1, LANES)
    w_route_hi = w_route.astype(BF16)
    w_route_lo = (w_route - w_route_hi.astype(F32)).astype(BF16)
    x1, h2, meta, route, cnt = _outproj(da.reshape(t, STREAM_W), hg.reshape(t, STREAM_W), x2, mod3,
                                 w_out[0].astype(BF16), norm2_w,
                                 jnp.concatenate([w_route_hi, w_route_lo], axis=1), b_route, seq)

    counts = cnt[0, :N_EXPERTS].astype(jnp.int32)
    tiles_e = (counts + MOE_TILE - 1) // MOE_TILE
    tile_end = jnp.cumsum(tiles_e)
    offs = (tile_end - tiles_e) * MOE_TILE
    ids = route[META_E0:META_E1 + 1].astype(jnp.int32)
    ranks = route[META_R0:META_R1 + 1].astype(jnp.int32)
    expert_iota = jnp.arange(N_EXPERTS, dtype=jnp.int32)[:, None, None]
    pos = jnp.sum(jnp.where(ids[None] == expert_iota, offs[:, None, None], 0), axis=0) + ranks
    n_tiles = (2 * t) // MOE_TILE + N_EXPERTS
    n_valid = tile_end[-1:]
    tile_ids = jnp.minimum(jnp.arange(n_tiles, dtype=jnp.int32), n_valid - 1)
    tile_expert = jnp.sum(tile_ids[:, None] >= tile_end[None, :], axis=1).astype(jnp.int32)
    def slot_table(tile):
        return pos.reshape(2, t // tile, tile).transpose(1, 0, 2).reshape(t // tile, 1, 2 * tile)

    fill_lo = jnp.concatenate([(offs + counts) // ZERO_CHUNK, n_valid * (MOE_TILE // ZERO_CHUNK)])
    fill_hi = jnp.concatenate([tile_end * (MOE_TILE // ZERO_CHUNK),
                               jnp.full((1,), n_tiles * (MOE_TILE // ZERO_CHUNK), jnp.int32)])
    fill_ranges = jnp.concatenate([fill_lo, fill_hi]).astype(jnp.int32)

    xs = _scatter(fill_ranges, slot_table(SCATTER_TILE), h2, n_tiles * MOE_TILE)
    ys = _experts(tile_expert, n_valid.astype(jnp.int32), xs, moe_w_gate[0], moe_w_up[0], moe_w_down[0])
    out = _combine(slot_table(COMBINE_TILE), x1, meta, mod3, final_norm_w.reshape(1, d), ys, seq)
    return out.reshape(bsz, seq, d)
```

```python
import functools
import math

import numpy as np
import jax
import jax.numpy as jnp
from jax import lax
from jax.experimental import pallas as pl
from jax.experimental.pallas import tpu as pltpu

F32 = jnp.float32
BF16 = jnp.bfloat16
HIGHEST = lax.Precision.HIGHEST

LANES = 128
SUBLANES = 8
D_MODEL = 1024
HEADS = 4
HEAD_W = 128
STREAM_W = HEADS * HEAD_W
N_STREAMS = 7
DA_QK_DIM = 64
ROPE_THETA = 500000.0
ROT_DIM = DA_QK_DIM // 4
ROT_HALF = ROT_DIM // 2
N_GROUPS = 4
EXPERTS_PER_GROUP = 8
N_EXPERTS = N_GROUPS * EXPERTS_PER_GROUP
EXPERT_FF = 256
EPS = 1e-6
LAM_INIT = 0.8 - 0.6 * math.exp(-0.3 * 0)
NEG_BIG = -1e30

ROW_TILE = 512
OUT_TILE = 1024
ATTN_TILE = 1024
ATTN_KEY_TILE = 256
HGRN_BLOCK = 128
HGRN_STEP = 256
MOE_TILE = 512
EXPERT_PARTS = 2
SCATTER_TILE = 1024
COMBINE_TILE = 256

NT_DIMS = (((1,), (1,)), ((), ()))
TN_DIMS = (((0,), (0,)), ((), ()))


def _sigmoid(x):
    return 1.0 / (1.0 + jnp.exp(-x))


SLAB = D_MODEL // LANES


def _slab_store(ref, val):
    rows = val.shape[0]
    for c in range(SLAB):
        ref[pl.ds(c, rows, stride=SLAB), :] = val[:, c * LANES:(c + 1) * LANES]


def _slab_load(ref, rows):
    return jnp.concatenate([ref[pl.ds(c, rows, stride=SLAB), :] for c in range(SLAB)], axis=1)


def _slab_store_bf16(ref, val, piece_scr):
    _slab_store(piece_scr, val)
    ref[...] = piece_scr[...].astype(BF16)


def _slab_load_bf16(ref, rows, piece_scr):
    piece_scr[...] = ref[...].astype(F32)
    return _slab_load(piece_scr, rows)


def _adaln_body(c_ref, w_ref, b_ref, o_ref):
    c = c_ref[...]
    ca = c * _sigmoid(c)
    o_ref[...] = jnp.dot(ca, w_ref[...], preferred_element_type=F32, precision=HIGHEST) + b_ref[...]


def _adaln(c, ada_w, ada_b):
    bsz, d = c.shape
    n = ada_w.shape[1]
    tn = 1024
    return pl.pallas_call(
        _adaln_body,
        grid=(n // tn,),
        in_specs=[pl.BlockSpec((bsz, d), lambda j: (0, 0)),
                  pl.BlockSpec((d, tn), lambda j: (0, j)),
                  pl.BlockSpec((1, tn), lambda j: (0, j))],
        out_specs=pl.BlockSpec((bsz, tn), lambda j: (0, j)),
        out_shape=jax.ShapeDtypeStruct((bsz, n), F32),
        name="adaln",
    )(c, ada_w, ada_b)


def _inproj_body(x_ref, mod_ref, nw_ref, w32_ref, pos_ref, invf_ref, spread_ref, lbr_ref,
                 q_ref, k_ref, v_ref, hq_ref, lf_ref, kf_ref, gi_ref, sg_ref, w_ref):
    @pl.when(pl.program_id(0) == 0)
    def _():
        for j in range(N_STREAMS):
            cols = slice(j * STREAM_W, (j + 1) * STREAM_W)
            w_ref[:, cols] = w32_ref[:, cols].astype(BF16)

    x = x_ref[...]
    ms = jnp.mean(x * x, axis=-1, keepdims=True)
    y = x * lax.rsqrt(ms + EPS) * nw_ref[...]
    shift = mod_ref[0, 0:1, :]
    scale = mod_ref[0, 1:2, :]
    h = (y * (1.0 + scale) + shift).astype(BF16)

    ang_t = invf_ref[...] * pos_ref[0]

    def spread(table_t):
        hi = table_t.astype(BF16)
        lo = (table_t - hi.astype(F32)).astype(BF16)
        return lax.dot_general(jnp.concatenate([hi, lo], axis=0), spread_ref[...], TN_DIMS,
                               preferred_element_type=F32)

    lane = lax.broadcasted_iota(jnp.int32, (1, LANES), 1) % DA_QK_DIM
    cosv = spread(jnp.cos(ang_t)) + jnp.where(lane < ROT_DIM, 0.0, 1.0)
    sinv = spread(jnp.sin(ang_t))
    sin_lo = jnp.where(lane < ROT_HALF, -sinv, 0.0)
    sin_hi = jnp.where((lane >= ROT_HALF) & (lane < ROT_DIM), sinv, 0.0)

    streams = [jnp.dot(h, w_ref[:, j * STREAM_W:(j + 1) * STREAM_W], preferred_element_type=F32)
               for j in range(N_STREAMS)]

    def proj(j):
        return streams[j]

    def rope(t):
        outs = []
        for hb in range(HEADS):
            tc = t[:, hb * HEAD_W:(hb + 1) * HEAD_W]
            outs.append(tc * cosv
                        + pltpu.roll(tc, LANES - ROT_HALF, 1) * sin_lo
                        + pltpu.roll(tc, ROT_HALF, 1) * sin_hi)
        return jnp.concatenate(outs, axis=1)

    q_ref[0] = (rope(proj(0)) * (DA_QK_DIM ** -0.5 * math.log2(math.e))).T.astype(BF16)
    k_ref[...] = rope(proj(1)).astype(BF16)
    v_ref[0] = proj(2).astype(BF16).T

    gq = proj(3)
    hq_ref[...] = gq * _sigmoid(gq)

    a = lbr_ref[...]
    amax = jnp.max(a, axis=0, keepdims=True)
    ea = jnp.exp(a - amax)
    lb = ea[0:1, :] / jnp.sum(ea, axis=0, keepdims=True)
    gf = proj(4)
    f = lb + (1.0 - lb) * _sigmoid(gf)
    lf_ref[...] = jnp.log(f) * math.log2(math.e)
    kf_ref[...] = 1.0 - f

    gi_ref[...] = proj(5).astype(BF16)
    gg = proj(6)
    sg_ref[...] = (gg * _sigmoid(gg)).astype(BF16)


def _inproj(x2, mod3, norm_w, w_f32, pos_rows, invf, spread, lb_raw, seq):
    t, d = x2.shape
    tm = ROW_TILE
    per_b = seq // tm
    row = lambda i: (i, 0)
    full = lambda i: (0, 0)
    out_bf = jax.ShapeDtypeStruct((t, STREAM_W), BF16)
    out_f = jax.ShapeDtypeStruct((t, STREAM_W), F32)
    out_t = jax.ShapeDtypeStruct((t // seq, STREAM_W, seq), BF16)
    stream = pl.BlockSpec((tm, STREAM_W), row)
    stream_t = pl.BlockSpec((1, STREAM_W, tm), lambda i: (i // per_b, 0, i % per_b))
    return pl.pallas_call(
        _inproj_body,
        grid=(t // tm,),
        in_specs=[pl.BlockSpec((tm, d), row),
                  pl.BlockSpec((1, 6, d), lambda i: (i // per_b, 0, 0)),
                  pl.BlockSpec((1, d), full),
                  pl.BlockSpec((d, N_STREAMS * STREAM_W), full, pipeline_mode=pl.Buffered(1)),
                  pl.BlockSpec((1, 1, tm), lambda i: (i, 0, 0)),
                  pl.BlockSpec((ROT_HALF, 1), full),
                  pl.BlockSpec((2 * ROT_HALF, LANES), full),
                  pl.BlockSpec(lb_raw.shape, full)],
        out_specs=[stream_t, stream, stream_t] + [stream] * 5,
        out_shape=[out_t, out_bf, out_t, out_f, out_f, out_f, out_bf, out_bf],
        scratch_shapes=[pltpu.VMEM((d, N_STREAMS * STREAM_W), BF16)],
        compiler_params=pltpu.CompilerParams(dimension_semantics=("arbitrary",),
                                             vmem_limit_bytes=56 * 1024 * 1024),
        name="inproj",
    )(x2, mod3, norm_w, w_f32, pos_rows, invf, spread, lb_raw)


ONES_ROWS = 16


def _attn_body(qt_ref, k_ref, vt_ref, lam_ref, sw_ref, o_ref, s_scr, m_scr, a_scr, *, tile, ktile):
    qi = pl.program_id(2)
    qt = qt_ref[0]
    feat = lax.broadcasted_iota(jnp.int32, (HEAD_W, 1), 0)
    zero = jnp.zeros_like(qt)
    qmaps = (jnp.where(feat < DA_QK_DIM, qt, zero), jnp.where(feat >= DA_QK_DIM, qt, zero))

    m_scr[...] = jnp.full(m_scr.shape, NEG_BIG, F32)
    a_scr[...] = jnp.zeros(a_scr.shape, F32)

    def score_block(start, nkeys, q_lo, masked):
        kb = k_ref[0, pl.ds(start, nkeys), :]
        scores = [jnp.dot(kb, qmaps[mp][:, q_lo:], preferred_element_type=F32) for mp in range(2)]
        for mp, s in enumerate(scores):
            if masked:
                key = lax.broadcasted_iota(jnp.int32, s.shape, 0)
                qry = lax.broadcasted_iota(jnp.int32, s.shape, 1)
                s = jnp.where(key <= qry, s, NEG_BIG)
            grouped = s.reshape(nkeys // SUBLANES, SUBLANES, tile - q_lo)
            m_scr[mp, :, q_lo:] = jnp.maximum(m_scr[mp, :, q_lo:], jnp.max(grouped, axis=0))
            s_scr[mp, pl.ds(start, nkeys), q_lo:] = s

    def value_block(start, nkeys, q_lo, col_max):
        vb = jnp.concatenate([vt_ref[0, :, pl.ds(start, nkeys)], jnp.ones((ONES_ROWS, nkeys), BF16)],
                             axis=0)
        probs = [jnp.exp2(s_scr[mp, pl.ds(start, nkeys), q_lo:] - col_max[mp][:, q_lo:]).astype(BF16)
                 for mp in range(2)]
        for mp, p in enumerate(probs):
            a_scr[mp, :, q_lo:] += jnp.dot(vb, p, preferred_element_type=F32)

    diag = pl.multiple_of(qi * tile, tile)
    pieces = [(diag + j * ktile, ktile, j * ktile) for j in range(tile // ktile)]

    def score_body(ki, carry):
        score_block(pl.multiple_of(ki * tile, tile), tile, 0, False)
        return carry

    lax.fori_loop(0, qi, score_body, 0)
    for start, nkeys, q_lo in pieces:
        score_block(start, nkeys, q_lo, True)
    col_max = [jnp.max(m_scr[mp], axis=0, keepdims=True) for mp in range(2)]

    def value_body(ki, carry):
        value_block(pl.multiple_of(ki * tile, tile), tile, 0, col_max)
        return carry

    lax.fori_loop(0, qi, value_body, 0)
    for start, nkeys, q_lo in pieces:
        value_block(start, nkeys, q_lo, col_max)

    lp = lam_ref[...]
    lam = (jnp.exp(jnp.sum(lp[0:1] * lp[1:2], axis=-1, keepdims=True))
           - jnp.exp(jnp.sum(lp[2:3] * lp[3:4], axis=-1, keepdims=True)) + LAM_INIT)
    o = (a_scr[0, 0:HEAD_W, :] / a_scr[0, HEAD_W:HEAD_W + 1, :]
         - lam * (a_scr[1, 0:HEAD_W, :] / a_scr[1, HEAD_W:HEAD_W + 1, :]))
    o = o * lax.rsqrt(jnp.mean(o * o, axis=0, keepdims=True) + EPS) * sw_ref[...]
    o_ref[0] = (o * (1.0 - LAM_INIT)).T.astype(BF16)


def _attn(qt3, k3, vt3, lam_p, subln_col):
    bsz, seq, _ = k3.shape
    tile = ATTN_TILE
    return pl.pallas_call(
        functools.partial(_attn_body, tile=tile, ktile=ATTN_KEY_TILE),
        grid=(bsz, HEADS, seq // tile),
        in_specs=[pl.BlockSpec((1, HEAD_W, tile), lambda b, h, i: (b, h, i)),
                  pl.BlockSpec((1, seq, HEAD_W), lambda b, h, i: (b, 0, h)),
                  pl.BlockSpec((1, HEAD_W, seq), lambda b, h, i: (b, h, 0)),
                  pl.BlockSpec(lam_p.shape, lambda b, h, i: (0, 0)),
                  pl.BlockSpec((HEAD_W, 1), lambda b, h, i: (0, 0))],
        out_specs=pl.BlockSpec((1, tile, HEAD_W), lambda b, h, i: (b, i, h)),
        out_shape=jax.ShapeDtypeStruct((bsz, seq, STREAM_W), BF16),
        scratch_shapes=[pltpu.VMEM((2, seq, tile), F32),
                        pltpu.VMEM((2, SUBLANES, tile), F32),
                        pltpu.VMEM((2, HEAD_W + ONES_ROWS, tile), F32)],
        compiler_params=pltpu.CompilerParams(vmem_limit_bytes=56 * 1024 * 1024),
        name="diff_attn",
    )(qt3, k3, vt3, lam_p, subln_col)


def _hgrn_levels(block):
    return [block >> (i + 1) for i in range(block.bit_length() - 1)]


def _hgrn_constants(block):
    t = np.arange(block)[:, None]
    s = np.arange(block)[None, :]
    tril = (s <= t).astype(np.float32)
    lv = np.full((block, block), -1, np.int32)
    halves = _hgrn_levels(block)
    for li, m in enumerate(halves):
        same = (t // (2 * m)) == (s // (2 * m))
        lv[same & ((t & m) != 0) & ((s & m) == 0)] = li
    lv[np.arange(block), np.arange(block)] = len(halves)
    return jnp.asarray(tril, BF16), jnp.asarray(lv)


def _level_operand(b_ref, h, b, q, k, m, block):
    def ref_rows(r, n):
        return jnp.broadcast_to(b_ref[h, pl.ds(r, 1), :], (n, HEAD_W))

    if m >= SUBLANES:
        pieces = []
        for s0 in range(0, block, 2 * m):
            ref = ref_rows(s0 + m - 1, m)
            lo = slice(s0, s0 + m)
            up = slice(s0 + m, s0 + 2 * m)
            pieces.append(k[lo] * jnp.exp2(ref - b[lo]))
            pieces.append(q[up] * jnp.exp2(b[up] - ref))
        return jnp.concatenate(pieces, axis=0)

    sub = lax.broadcasted_iota(jnp.int32, (SUBLANES, HEAD_W), 0)
    refs = []
    for s0 in range(0, block, SUBLANES):
        piece = ref_rows(s0 + m - 1, SUBLANES)
        for j in range(1, SUBLANES // (2 * m)):
            piece = jnp.where(sub >= 2 * m * j, ref_rows(s0 + 2 * m * j + m - 1, SUBLANES), piece)
        refs.append(piece)
    d = b - jnp.concatenate(refs, axis=0)
    row = lax.broadcasted_iota(jnp.int32, (block, 1), 0)
    return jnp.where((row & m) != 0, q, k) * jnp.exp2(jnp.minimum(d, -d))


def _hgrn_body(hq_ref, lf_ref, kf_ref, gi_ref, sg_ref, nw_ref, tril_ref, lv_ref, o_ref,
               st_scr, b_scr, *, block, step):
    @pl.when(pl.program_id(1) == 0)
    def _():
        st_scr[...] = jnp.zeros(st_scr.shape, F32)

    tril = tril_ref[...]
    lv = lv_ref[...]
    halves = _hgrn_levels(block)

    units = [(u, r0, h) for u, (r0, h) in enumerate(
        (r0, h) for r0 in range(0, step, block) for h in range(HEADS))]

    def cols(h):
        return slice(h * HEAD_W, (h + 1) * HEAD_W)

    q, k, v, b, scores = {}, {}, {}, {}, {}
    for u, r0, h in units:
        rows = slice(r0, r0 + block)
        q[u] = hq_ref[0, rows, cols(h)]
        k[u] = kf_ref[0, rows, cols(h)]
        v[u] = gi_ref[0, rows, cols(h)]
        lf = lf_ref[0, rows, cols(h)]
        hi = lf.astype(BF16)
        r1 = lf - hi.astype(F32)
        mid = r1.astype(BF16)
        lo = (r1 - mid.astype(F32)).astype(BF16)
        b[u] = (jnp.dot(tril, hi, preferred_element_type=F32)
                + jnp.dot(tril, mid, preferred_element_type=F32)
                + jnp.dot(tril, lo, preferred_element_type=F32))
        b_scr[u] = b[u]
        scores[u] = jnp.where(lv == len(halves),
                              lax.dot_general(q[u].astype(BF16), k[u].astype(BF16), NT_DIMS,
                                              preferred_element_type=F32), 0.0)

    for li, m in enumerate(halves):
        for u, r0, h in units:
            xl = _level_operand(b_scr, u, b[u], q[u], k[u], m, block).astype(BF16)
            p = lax.dot_general(xl, xl, NT_DIMS, preferred_element_type=F32)
            scores[u] = jnp.where(lv == li, p, scores[u])

    for u, r0, h in units:
        rows = slice(r0, r0 + block)
        o_intra = jnp.dot(scores[u].astype(BF16), v[u], preferred_element_type=F32)
        st = st_scr[h]
        o_inter = lax.dot_general((q[u] * jnp.exp2(b[u])).astype(BF16), st.astype(BF16), NT_DIMS,
                                  preferred_element_type=F32)
        b_last = b[u][block - 1:block, :]
        kdec = (k[u] * jnp.exp2(b_last - b[u])).astype(BF16)
        st_scr[h] = st * jnp.exp2(b_last) + lax.dot_general(v[u], kdec, TN_DIMS,
                                                             preferred_element_type=F32)
        o = o_inter + o_intra
        o = o * lax.rsqrt(jnp.mean(o * o, axis=-1, keepdims=True) + EPS) * nw_ref[...]
        o_ref[0, rows, cols(h)] = (o * sg_ref[0, rows, cols(h)].astype(F32)).astype(BF16)


def _hgrn(hq3, lf3, kf3, gi3, sg3, norm_w):
    bsz, seq, _ = hq3.shape
    block = HGRN_BLOCK
    step = HGRN_STEP
    tril, lv = _hgrn_constants(block)
    blk = pl.BlockSpec((1, step, STREAM_W), lambda b, g: (b, g, 0))
    const = lambda b, g: (0, 0)
    return pl.pallas_call(
        functools.partial(_hgrn_body, block=block, step=step),
        grid=(bsz, seq // step),
        in_specs=[blk, blk, blk, blk, blk,
                  pl.BlockSpec((1, HEAD_W), const),
                  pl.BlockSpec((block, block), const),
                  pl.BlockSpec((block, block), const)],
        out_specs=blk,
        out_shape=jax.ShapeDtypeStruct((bsz, seq, STREAM_W), BF16),
        scratch_shapes=[pltpu.VMEM((HEADS, HEAD_W, HEAD_W), F32),
                        pltpu.VMEM((HEADS * step // block, block, HEAD_W), F32)],
        name="hgrn2",
    )(hq3, lf3, kf3, gi3, sg3, norm_w, tril, lv)


META_E0, META_E1, META_R0, META_R1, META_W0, META_W1 = range(6)
GROUP_LANE0 = N_EXPERTS


OUT_PARTS = 8


def _route(logits, lane):
    far = jnp.int32(LANES)

    def first_max(vals):
        mx = jnp.max(vals, axis=-1, keepdims=True)
        return mx, jnp.min(jnp.where(vals == mx, lane, far), axis=-1, keepdims=True)

    is_g = (lane >= GROUP_LANE0) & (lane < GROUP_LANE0 + N_GROUPS)
    gmax, glane = first_max(jnp.where(is_g, logits, NEG_BIG))
    g_w = 1.0 / jnp.sum(jnp.where(is_g, jnp.exp(logits - gmax), 0.0), axis=-1, keepdims=True)
    gidx = glane - GROUP_LANE0
    in_grp = (lane < N_EXPERTS) & ((lane // EXPERTS_PER_GROUP) == gidx)
    el = jnp.where(in_grp, logits, NEG_BIG)
    m1, i1 = first_max(el)
    m2, i2 = first_max(jnp.where(lane == i1, NEG_BIG, el))
    r = jnp.exp(m2 - m1)
    return i1, i2, g_w / (1.0 + r), g_w * r / (1.0 + r)


def _outproj_body(da_ref, hg_ref, x_ref, mod_ref, wo_ref, nw_ref, wr_ref, br_ref, stril_ref,
                  x1_ref, h2_ref, meta_ref, route_ref, cnt_ref, carry_scr, piece_scr):
    @pl.when(pl.program_id(0) == 0)
    def _():
        carry_scr[...] = jnp.zeros(carry_scr.shape, F32)

    rows = stril_ref.shape[0]
    parts = [slice(p * rows, (p + 1) * rows) for p in range(OUT_PARTS)]
    gate1 = mod_ref[0, 2:3, :]
    shift2 = mod_ref[0, 3:4, :]
    scale2 = mod_ref[0, 4:5, :]

    attn = [jnp.dot(da_ref[r, :], wo_ref[0:STREAM_W, :], preferred_element_type=F32)
            + jnp.dot(hg_ref[r, :], wo_ref[STREAM_W:, :], preferred_element_type=F32) for r in parts]

    h2 = []
    for r, a in zip(parts, attn):
        x1 = x_ref[r, :] + gate1 * a
        x1_ref[r, :] = x1
        h2.append(x1 * lax.rsqrt(jnp.mean(x1 * x1, axis=-1, keepdims=True) + EPS) * nw_ref[...]
                  * (1.0 + scale2) + shift2)

    logits = []
    for p, h in enumerate(h2):
        _slab_store_bf16(h2_ref.at[pl.ds(p * rows * SLAB, rows * SLAB)], h, piece_scr.at[p])
        h_hi = h.astype(BF16)
        h_lo = (h - h_hi.astype(F32)).astype(BF16)
        terms = jnp.dot(jnp.concatenate([h_hi, h_lo], axis=0), wr_ref[...],
                        preferred_element_type=F32)
        logits.append((terms[:rows, :LANES] + terms[:rows, LANES:])
                      + (terms[rows:, :LANES] + terms[rows:, LANES:]) + br_ref[...])

    lane = lax.broadcasted_iota(jnp.int32, (rows, LANES), 1)
    routed = [_route(lg, lane) for lg in logits]

    carry = carry_scr[...]
    for r, (i1, i2, w0, w1) in zip(parts, routed):
        hot0 = lane == i1
        hot1 = lane == i2
        multi = jnp.where(hot0 | hot1, 1.0, 0.0)
        before = jnp.dot(stril_ref[...], multi.astype(BF16), preferred_element_type=F32) + carry
        rank0 = jnp.sum(jnp.where(hot0, before, 0.0), axis=-1, keepdims=True)
        rank1 = jnp.sum(jnp.where(hot1, before, 0.0), axis=-1, keepdims=True)
        carry = carry + jnp.sum(multi, axis=0, keepdims=True)
        meta = jnp.zeros((rows, LANES), F32)
        for idx, val in ((META_E0, i1.astype(F32)), (META_E1, i2.astype(F32)),
                         (META_R0, rank0), (META_R1, rank1), (META_W0, w0), (META_W1, w1)):
            meta = jnp.where(lane == idx, val, meta)
        meta_ref[r, :] = meta
        route_ref[:, r] = meta.T[0:SUBLANES, :]
    carry_scr[...] = carry
    cnt_ref[...] = carry


def _outproj(da2, hg2, x2, mod3, wo_bf, norm_w, w_route, b_route, seq):
    t, d = x2.shape
    tm = OUT_TILE
    per_b = seq // tm
    row = lambda i: (i, 0)
    full = lambda i: (0, 0)
    part = tm // OUT_PARTS
    stril = jnp.asarray(np.tril(np.ones((part, part), np.float32), -1), BF16)
    return pl.pallas_call(
        _outproj_body,
        grid=(t // tm,),
        in_specs=[pl.BlockSpec((tm, STREAM_W), row),
                  pl.BlockSpec((tm, STREAM_W), row),
                  pl.BlockSpec((tm, d), row),
                  pl.BlockSpec((1, 6, d), lambda i: (i // per_b, 0, 0)),
                  pl.BlockSpec((2 * STREAM_W, d), full),
                  pl.BlockSpec((1, d), full),
                  pl.BlockSpec((d, 2 * LANES), full),
                  pl.BlockSpec((1, LANES), full),
                  pl.BlockSpec((part, part), full)],
        out_specs=[pl.BlockSpec((tm, d), row),
                   pl.BlockSpec((tm * SLAB, LANES), row),
                   pl.BlockSpec((tm, LANES), row),
                   pl.BlockSpec((SUBLANES, tm), lambda i: (0, i)),
                   pl.BlockSpec((1, LANES), full)],
        out_shape=[jax.ShapeDtypeStruct((t, d), F32),
                   jax.ShapeDtypeStruct((t * SLAB, LANES), BF16),
                   jax.ShapeDtypeStruct((t, LANES), F32),
                   jax.ShapeDtypeStruct((SUBLANES, t), F32),
                   jax.ShapeDtypeStruct((1, LANES), F32)],
        scratch_shapes=[pltpu.VMEM((1, LANES), F32),
                        pltpu.VMEM((OUT_PARTS, part * SLAB, LANES), F32)],
        compiler_params=pltpu.CompilerParams(dimension_semantics=("arbitrary",)),
        name="outproj_route",
    )(da2, hg2, x2, mod3, wo_bf, norm_w, w_route, b_route, stril)


DMA_UNROLL = 8


ZERO_CHUNK = 128
N_FILL_RANGES = N_EXPERTS + 1


def _scatter_body(fill_ref, pos_ref, h2_ref, wg_ref, wu_ref, wd_ref, xs_hbm, wgu_ref, wdb_ref,
                  zero_scr, sem, zsem, *, tile):
    rows = tile * SLAB

    wgu_ref[:, :, 0:EXPERT_FF] = wg_ref[...].astype(BF16)
    wgu_ref[:, :, EXPERT_FF:] = wu_ref[...].astype(BF16)
    wdb_ref[...] = wd_ref[...].astype(BF16)

    @pl.when(pl.program_id(0) == 0)
    def _():
        zero_scr[...] = jnp.zeros(zero_scr.shape, BF16)

        def clear(chunk):
            start = pl.multiple_of(chunk * (ZERO_CHUNK * SLAB), SLAB)
            return pltpu.make_async_copy(zero_scr, xs_hbm.at[pl.ds(start, ZERO_CHUNK * SLAB)], zsem)

        def start_one(chunk, carry):
            clear(chunk).start()
            return carry

        def wait_one(chunk, carry):
            clear(chunk).wait()
            return carry

        for fn in (start_one, wait_one):
            for r in range(N_FILL_RANGES):
                lax.fori_loop(fill_ref[r], fill_ref[N_FILL_RANGES + r], fn, 0)

    def issue(t, carry):
        src = h2_ref.at[pl.ds(pl.multiple_of(t * SLAB, SLAB), SLAB)]
        for j in range(2):
            slot = pl.multiple_of(pos_ref[0, 0, j * tile + t] * SLAB, SLAB)
            pltpu.make_async_copy(src, xs_hbm.at[pl.ds(slot, SLAB)], sem).start(priority=j)
        return carry

    lax.fori_loop(0, tile, issue, 0, unroll=DMA_UNROLL)
    whole = pltpu.make_async_copy(h2_ref, xs_hbm.at[pl.ds(0, rows)], sem)
    whole.wait()
    whole.wait()


def _scatter(fill_ranges, pos3, h2s, n_slots, w_gate, w_up, w_down):
    tile = SCATTER_TILE
    rows = tile * SLAB
    n_steps = h2s.shape[0] // rows
    n_exp, d, ff = w_gate.shape
    assert n_exp % n_steps == 0, "expert weights are cast in equal shares per scatter step"
    share = n_exp // n_steps
    w_in = lambda i, lt: (i, 0, 0)
    grid_spec = pltpu.PrefetchScalarGridSpec(
        num_scalar_prefetch=1,
        grid=(n_steps,),
        in_specs=[pl.BlockSpec((1, 1, 2 * tile), lambda i, lt: (i, 0, 0), memory_space=pltpu.SMEM),
                  pl.BlockSpec((rows, LANES), lambda i, lt: (i, 0)),
                  pl.BlockSpec((share, d, ff), w_in),
                  pl.BlockSpec((share, d, ff), w_in),
                  pl.BlockSpec((share, ff, d), w_in)],
        out_specs=[pl.BlockSpec(memory_space=pl.ANY),
                   pl.BlockSpec((share, d, 2 * ff), w_in),
                   pl.BlockSpec((share, ff, d), w_in)],
        scratch_shapes=[pltpu.VMEM((ZERO_CHUNK * SLAB, LANES), BF16),
                        pltpu.SemaphoreType.DMA, pltpu.SemaphoreType.DMA],
    )
    return pl.pallas_call(
        functools.partial(_scatter_body, tile=tile),
        grid_spec=grid_spec,
        out_shape=[jax.ShapeDtypeStruct((n_slots * SLAB, LANES), BF16),
                   jax.ShapeDtypeStruct((n_exp, d, 2 * ff), BF16),
                   jax.ShapeDtypeStruct((n_exp, ff, d), BF16)],
        compiler_params=pltpu.CompilerParams(dimension_semantics=("arbitrary",)),
        name="moe_scatter",
    )(fill_ranges, pos3, h2s, w_gate, w_up, w_down)


def _experts_body(te_ref, nv_ref, x_ref, wgu_ref, wd_ref, y_ref, piece_scr, *, tm):
    del te_ref
    i = pl.program_id(0)

    @pl.when(i < nv_ref[0])
    def _():
        part = tm // EXPERT_PARTS
        xr = [x_ref.at[pl.ds(r * part * SLAB, part * SLAB)] for r in range(EXPERT_PARTS)]
        yr = [y_ref.at[pl.ds(r * part * SLAB, part * SLAB)] for r in range(EXPERT_PARTS)]
        xs = [_slab_load_bf16(ref, part, piece_scr.at[r]).astype(BF16) for r, ref in enumerate(xr)]
        gus = [jnp.dot(x, wgu_ref[0], preferred_element_type=F32) for x in xs]
        acts = [(gu[:, 0:EXPERT_FF] * _sigmoid(gu[:, 0:EXPERT_FF]) * gu[:, EXPERT_FF:]).astype(BF16)
                for gu in gus]
        ys = [jnp.dot(act, wd_ref[0], preferred_element_type=F32) for act in acts]
        for ref, y in zip(yr, ys):
            _slab_store(ref, y)

    @pl.when(i >= nv_ref[0])
    def _():
        y_ref[...] = jnp.zeros(y_ref.shape, F32)


def _experts(tile_expert, n_valid, xs, w_gate_up, w_down):
    tm = MOE_TILE
    rows = tm * SLAB
    d = w_gate_up.shape[1]
    live = lambda i, te, nv: (jnp.minimum(i, nv[0] - 1), 0)
    grid_spec = pltpu.PrefetchScalarGridSpec(
        num_scalar_prefetch=2,
        grid=(xs.shape[0] // rows,),
        in_specs=[pl.BlockSpec((rows, LANES), live),
                  pl.BlockSpec((1, d, 2 * EXPERT_FF), lambda i, te, nv: (te[i], 0, 0)),
                  pl.BlockSpec((1, EXPERT_FF, d), lambda i, te, nv: (te[i], 0, 0))],
        out_specs=pl.BlockSpec((rows, LANES), lambda i, te, nv: (i, 0)),
        scratch_shapes=[pltpu.VMEM((EXPERT_PARTS, rows // EXPERT_PARTS, LANES), F32)],
    )
    return pl.pallas_call(
        functools.partial(_experts_body, tm=tm),
        grid_spec=grid_spec,
        out_shape=jax.ShapeDtypeStruct(xs.shape, F32),
        compiler_params=pltpu.CompilerParams(dimension_semantics=("arbitrary",)),
        name="moe_experts",
    )(tile_expert, n_valid, xs, w_gate_up, w_down)


def _combine_body(pos_ref, nxt_ref, x1_ref, meta_ref, mod_ref, nw_ref, ys_hbm, o_ref, rows_scr, sems,
                  *, tile):
    i = pl.program_id(0)
    n = pl.num_programs(0)
    rows = tile * SLAB

    def fetch(idx_ref, buf):
        def issue(t, carry):
            dst = pl.ds(pl.multiple_of(t * SLAB, SLAB), SLAB)
            for j in range(2):
                slot = pl.multiple_of(idx_ref[0, 0, j * tile + t] * SLAB, SLAB)
                pltpu.make_async_copy(ys_hbm.at[pl.ds(slot, SLAB)], rows_scr.at[buf, j, dst],
                                      sems.at[buf]).start(priority=j)
            return carry
        lax.fori_loop(0, tile, issue, 0, unroll=DMA_UNROLL)

    @pl.when(i == 0)
    def _():
        fetch(pos_ref, 0)

    for cur in range(2):
        @pl.when((i % 2 == cur) & (i + 1 < n))
        def _():
            fetch(nxt_ref, 1 - cur)

    for cur in range(2):
        @pl.when(i % 2 == cur)
        def _():
            for j in range(2):
                pltpu.make_async_copy(ys_hbm.at[pl.ds(0, rows)], rows_scr.at[cur, j],
                                      sems.at[cur]).wait()
            meta = meta_ref[...]
            w0 = meta[:, META_W0:META_W0 + 1]
            w1 = meta[:, META_W1:META_W1 + 1]
            y = (w0 * _slab_load(rows_scr.at[cur, 0], tile)
                 + w1 * _slab_load(rows_scr.at[cur, 1], tile))
            x2 = x1_ref[...] + mod_ref[0, 5:6, :] * y
            o_ref[...] = (x2 * lax.rsqrt(jnp.mean(x2 * x2, axis=-1, keepdims=True) + EPS)
                          * nw_ref[...])


def _combine(pos3, x1, meta, mod3, norm_w, ys, seq):
    t, d = x1.shape
    tile = COMBINE_TILE
    per_b = seq // tile
    n = t // tile
    row = lambda i: (i, 0)
    return pl.pallas_call(
        functools.partial(_combine_body, tile=tile),
        grid=(n,),
        in_specs=[pl.BlockSpec((1, 1, 2 * tile), lambda i: (i, 0, 0), memory_space=pltpu.SMEM),
                  pl.BlockSpec((1, 1, 2 * tile), lambda i: (jnp.minimum(i + 1, n - 1), 0, 0),
                               memory_space=pltpu.SMEM),
                  pl.BlockSpec((tile, d), row),
                  pl.BlockSpec((tile, LANES), row),
                  pl.BlockSpec((1, 6, d), lambda i: (i // per_b, 0, 0)),
                  pl.BlockSpec((1, d), lambda i: (0, 0)),
                  pl.BlockSpec(memory_space=pl.ANY)],
        out_specs=pl.BlockSpec((tile, d), row),
        out_shape=jax.ShapeDtypeStruct((t, d), F32),
        scratch_shapes=[pltpu.VMEM((2, 2, tile * SLAB, LANES), F32), pltpu.SemaphoreType.DMA((2,))],
        compiler_params=pltpu.CompilerParams(dimension_semantics=("arbitrary",)),
        name="moe_combine",
    )(pos3, pos3, x1, meta, mod3, norm_w, ys)


def _rope_constants():
    inv_freq = ROPE_THETA ** (-jnp.arange(ROT_HALF, dtype=F32) / ROT_HALF)
    lane = np.arange(LANES) % DA_QK_DIM
    hit = (lane[None, :] < ROT_DIM) & (lane[None, :] % ROT_HALF == np.arange(ROT_HALF)[:, None])
    spread = np.concatenate([hit, hit], axis=0).astype(np.float32)
    return inv_freq.reshape(ROT_HALF, 1), jnp.asarray(spread, BF16)


def kernel(x, c, positions, norm1_w, norm2_w, final_norm_w, ada_w, ada_b, w_in, w_out, da_lambda_q1, da_lambda_k1, da_lambda_q2, da_lambda_k2, da_subln_w, hg_lower_bound, hg_norm_w, moe_w_group, moe_b_group, moe_w_router, moe_b_router, moe_w_gate, moe_w_up, moe_w_down):
    bsz, seq, d = x.shape
    assert d == D_MODEL and norm1_w.shape[0] == 1, "single-layer model of width 1024 only"
    assert seq % ATTN_TILE == 0 and seq % HGRN_STEP == 0 and seq % OUT_TILE == 0
    t = bsz * seq
    x2 = x.reshape(t, d)

    mod3 = _adaln(c, ada_w[0], ada_b).reshape(bsz, 6, d)

    pos_rows = positions.astype(F32).reshape(t // ROW_TILE, 1, ROW_TILE)
    inv_freq, spread = _rope_constants()
    qt, k, vt, hq, lf, kf, gi, sg = _inproj(x2, mod3, norm1_w, w_in[0], pos_rows,
                                          inv_freq, spread, hg_lower_bound, seq)

    as3 = lambda a: a.reshape(bsz, seq, STREAM_W)
    lam_p = jnp.concatenate([da_lambda_q1, da_lambda_k1, da_lambda_q2, da_lambda_k2], axis=0)
    da = _attn(qt, as3(k), vt, lam_p, da_subln_w.reshape(HEAD_W, 1))
    hg = _hgrn(as3(hq), as3(lf), as3(kf), as3(gi), as3(sg), hg_norm_w)

    pad = jnp.zeros((d, LANES - N_EXPERTS - N_GROUPS), F32)
    w_route = jnp.concatenate([moe_w_router[0], moe_w_group[0], pad], axis=1)
    b_route = jnp.concatenate([moe_b_router[0], moe_b_group[0], pad[0]]).reshape(1, LANES)
    w_route_hi = w_route.astype(BF16)
    w_route_lo = (w_route - w_route_hi.astype(F32)).astype(BF16)
    x1, h2, meta, route, cnt = _outproj(da.reshape(t, STREAM_W), hg.reshape(t, STREAM_W), x2, mod3,
                                 w_out[0].astype(BF16), norm2_w,
                                 jnp.concatenate([w_route_hi, w_route_lo], axis=1), b_route, seq)

    counts = cnt[0, :N_EXPERTS].astype(jnp.int32)
    tiles_e = (counts + MOE_TILE - 1) // MOE_TILE
    tile_end = jnp.cumsum(tiles_e)
    offs = (tile_end - tiles_e) * MOE_TILE
    ids = route[META_E0:META_E1 + 1].astype(jnp.int32)
    ranks = route[META_R0:META_R1 + 1].astype(jnp.int32)
    expert_iota = jnp.arange(N_EXPERTS, dtype=jnp.int32)[:, None, None]
    pos = jnp.sum(jnp.where(ids[None] == expert_iota, offs[:, None, None], 0), axis=0) + ranks
    n_tiles = (2 * t) // MOE_TILE + N_EXPERTS
    n_valid = tile_end[-1:]
    tile_ids = jnp.minimum(jnp.arange(n_tiles, dtype=jnp.int32), n_valid - 1)
    tile_expert = jnp.sum(tile_ids[:, None] >= tile_end[None, :], axis=1).astype(jnp.int32)
    def slot_table(tile):
        return pos.reshape(2, t // tile, tile).transpose(1, 0, 2).reshape(t // tile, 1, 2 * tile)

    fill_lo = jnp.concatenate([(offs + counts) // ZERO_CHUNK, n_valid * (MOE_TILE // ZERO_CHUNK)])
    fill_hi = jnp.concatenate([tile_end * (MOE_TILE // ZERO_CHUNK),
                               jnp.full((1,), n_tiles * (MOE_TILE // ZERO_CHUNK), jnp.int32)])
    fill_ranges = jnp.concatenate([fill_lo, fill_hi]).astype(jnp.int32)

    xs, w_gate_up, w_down_bf = _scatter(fill_ranges, slot_table(SCATTER_TILE), h2, n_tiles * MOE_TILE,
                                        moe_w_gate[0], moe_w_up[0], moe_w_down[0])
    ys = _experts(tile_expert, n_valid.astype(jnp.int32), xs, w_gate_up, w_down_bf)
    out = _combine(slot_table(COMBINE_TILE), x1, meta, mod3, final_norm_w.reshape(1, d), ys, seq)
    return out.reshape(bsz, seq, d)
```

```python
import functools
import math

import numpy as np
import jax
import jax.numpy as jnp
from jax import lax
from jax.experimental import pallas as pl
from jax.experimental.pallas import tpu as pltpu

F32 = jnp.float32
BF16 = jnp.bfloat16
HIGHEST = lax.Precision.HIGHEST

LANES = 128
SUBLANES = 8
D_MODEL = 1024
HEADS = 4
HEAD_W = 128
STREAM_W = HEADS * HEAD_W
N_STREAMS = 7
DA_QK_DIM = 64
ROPE_THETA = 500000.0
ROT_DIM = DA_QK_DIM // 4
ROT_HALF = ROT_DIM // 2
N_GROUPS = 4
EXPERTS_PER_GROUP = 8
N_EXPERTS = N_GROUPS * EXPERTS_PER_GROUP
EXPERT_FF = 256
EPS = 1e-6
LAM_INIT = 0.8 - 0.6 * math.exp(-0.3 * 0)
NEG_BIG = -1e30

ROW_TILE = 512
OUT_TILE = 1024
ATTN_TILE = 1024
ATTN_KEY_TILE = 256
HGRN_BLOCK = 128
HGRN_STEP = 256
MOE_TILE = 512
EXPERT_PARTS = 2
SCATTER_TILE = 2048
COMBINE_TILE = 256

NT_DIMS = (((1,), (1,)), ((), ()))
TN_DIMS = (((0,), (0,)), ((), ()))


def _sigmoid(x):
    return 1.0 / (1.0 + jnp.exp(-x))


SLAB = D_MODEL // LANES


def _slab_store(ref, val):
    rows = val.shape[0]
    for c in range(SLAB):
        ref[pl.ds(c, rows, stride=SLAB), :] = val[:, c * LANES:(c + 1) * LANES]


def _slab_load(ref, rows):
    return jnp.concatenate([ref[pl.ds(c, rows, stride=SLAB), :] for c in range(SLAB)], axis=1)


def _slab_store_bf16(ref, val, piece_scr):
    _slab_store(piece_scr, val)
    ref[...] = piece_scr[...].astype(BF16)


def _slab_load_bf16(ref, rows, piece_scr):
    piece_scr[...] = ref[...].astype(F32)
    return _slab_load(piece_scr, rows)


def _adaln_body(c_ref, w_ref, b_ref, o_ref):
    c = c_ref[...]
    ca = c * _sigmoid(c)
    o_ref[...] = jnp.dot(ca, w_ref[...], preferred_element_type=F32, precision=HIGHEST) + b_ref[...]


def _adaln(c, ada_w, ada_b):
    bsz, d = c.shape
    n = ada_w.shape[1]
    tn = 1024
    return pl.pallas_call(
        _adaln_body,
        grid=(n // tn,),
        in_specs=[pl.BlockSpec((bsz, d), lambda j: (0, 0)),
                  pl.BlockSpec((d, tn), lambda j: (0, j)),
                  pl.BlockSpec((1, tn), lambda j: (0, j))],
        out_specs=pl.BlockSpec((bsz, tn), lambda j: (0, j)),
        out_shape=jax.ShapeDtypeStruct((bsz, n), F32),
        name="adaln",
    )(c, ada_w, ada_b)


def _inproj_body(x_ref, mod_ref, nw_ref, w32_ref, pos_ref, invf_ref, spread_ref, lbr_ref,
                 q_ref, k_ref, v_ref, hq_ref, lf_ref, kf_ref, gi_ref, sg_ref, w_ref):
    @pl.when(pl.program_id(0) == 0)
    def _():
        for j in range(N_STREAMS):
            cols = slice(j * STREAM_W, (j + 1) * STREAM_W)
            w_ref[:, cols] = w32_ref[:, cols].astype(BF16)

    x = x_ref[...]
    ms = jnp.mean(x * x, axis=-1, keepdims=True)
    y = x * lax.rsqrt(ms + EPS) * nw_ref[...]
    shift = mod_ref[0, 0:1, :]
    scale = mod_ref[0, 1:2, :]
    h = (y * (1.0 + scale) + shift).astype(BF16)

    ang_t = invf_ref[...] * pos_ref[0]

    def spread(table_t):
        hi = table_t.astype(BF16)
        lo = (table_t - hi.astype(F32)).astype(BF16)
        return lax.dot_general(jnp.concatenate([hi, lo], axis=0), spread_ref[...], TN_DIMS,
                               preferred_element_type=F32)

    lane = lax.broadcasted_iota(jnp.int32, (1, LANES), 1) % DA_QK_DIM
    cosv = spread(jnp.cos(ang_t)) + jnp.where(lane < ROT_DIM, 0.0, 1.0)
    sinv = spread(jnp.sin(ang_t))
    sin_lo = jnp.where(lane < ROT_HALF, -sinv, 0.0)
    sin_hi = jnp.where((lane >= ROT_HALF) & (lane < ROT_DIM), sinv, 0.0)

    streams = [jnp.dot(h, w_ref[:, j * STREAM_W:(j + 1) * STREAM_W], preferred_element_type=F32)
               for j in range(N_STREAMS)]

    def proj(j):
        return streams[j]

    def rope(t):
        outs = []
        for hb in range(HEADS):
            tc = t[:, hb * HEAD_W:(hb + 1) * HEAD_W]
            outs.append(tc * cosv
                        + pltpu.roll(tc, LANES - ROT_HALF, 1) * sin_lo
                        + pltpu.roll(tc, ROT_HALF, 1) * sin_hi)
        return jnp.concatenate(outs, axis=1)

    q_ref[0] = (rope(proj(0)) * (DA_QK_DIM ** -0.5 * math.log2(math.e))).T.astype(BF16)
    k_ref[...] = rope(proj(1)).astype(BF16)
    v_ref[0] = proj(2).astype(BF16).T

    gq = proj(3)
    hq_ref[...] = gq * _sigmoid(gq)

    a = lbr_ref[...]
    amax = jnp.max(a, axis=0, keepdims=True)
    ea = jnp.exp(a - amax)
    lb = ea[0:1, :] / jnp.sum(ea, axis=0, keepdims=True)
    gf = proj(4)
    f = lb + (1.0 - lb) * _sigmoid(gf)
    lf_ref[...] = jnp.log(f) * math.log2(math.e)
    kf_ref[...] = 1.0 - f

    gi_ref[...] = proj(5).astype(BF16)
    gg = proj(6)
    sg_ref[...] = (gg * _sigmoid(gg)).astype(BF16)


def _inproj(x2, mod3, norm_w, w_f32, pos_rows, invf, spread, lb_raw, seq):
    t, d = x2.shape
    tm = ROW_TILE
    per_b = seq // tm
    row = lambda i: (i, 0)
    full = lambda i: (0, 0)
    out_bf = jax.ShapeDtypeStruct((t, STREAM_W), BF16)
    out_f = jax.ShapeDtypeStruct((t, STREAM_W), F32)
    out_t = jax.ShapeDtypeStruct((t // seq, STREAM_W, seq), BF16)
    stream = pl.BlockSpec((tm, STREAM_W), row)
    stream_t = pl.BlockSpec((1, STREAM_W, tm), lambda i: (i // per_b, 0, i % per_b))
    return pl.pallas_call(
        _inproj_body,
        grid=(t // tm,),
        in_specs=[pl.BlockSpec((tm, d), row),
                  pl.BlockSpec((1, 6, d), lambda i: (i // per_b, 0, 0)),
                  pl.BlockSpec((1, d), full),
                  pl.BlockSpec((d, N_STREAMS * STREAM_W), full, pipeline_mode=pl.Buffered(1)),
                  pl.BlockSpec((1, 1, tm), lambda i: (i, 0, 0)),
                  pl.BlockSpec((ROT_HALF, 1), full),
                  pl.BlockSpec((2 * ROT_HALF, LANES), full),
                  pl.BlockSpec(lb_raw.shape, full)],
        out_specs=[stream_t, stream, stream_t] + [stream] * 5,
        out_shape=[out_t, out_bf, out_t, out_f, out_f, out_f, out_bf, out_bf],
        scratch_shapes=[pltpu.VMEM((d, N_STREAMS * STREAM_W), BF16)],
        compiler_params=pltpu.CompilerParams(dimension_semantics=("arbitrary",),
                                             vmem_limit_bytes=56 * 1024 * 1024),
        name="inproj",
    )(x2, mod3, norm_w, w_f32, pos_rows, invf, spread, lb_raw)


ONES_ROWS = 16


def _attn_body(qt_ref, k_ref, vt_ref, lam_ref, sw_ref, o_ref, s_scr, m_scr, a_scr, *, tile, ktile):
    qi = pl.program_id(2)
    qt = qt_ref[0]
    feat = lax.broadcasted_iota(jnp.int32, (HEAD_W, 1), 0)
    zero = jnp.zeros_like(qt)
    qmaps = (jnp.where(feat < DA_QK_DIM, qt, zero), jnp.where(feat >= DA_QK_DIM, qt, zero))

    m_scr[...] = jnp.full(m_scr.shape, NEG_BIG, F32)
    a_scr[...] = jnp.zeros(a_scr.shape, F32)

    def score_block(start, nkeys, q_lo, masked):
        kb = k_ref[0, pl.ds(start, nkeys), :]
        scores = [jnp.dot(kb, qmaps[mp][:, q_lo:], preferred_element_type=F32) for mp in range(2)]
        for mp, s in enumerate(scores):
            if masked:
                key = lax.broadcasted_iota(jnp.int32, s.shape, 0)
                qry = lax.broadcasted_iota(jnp.int32, s.shape, 1)
                s = jnp.where(key <= qry, s, NEG_BIG)
            grouped = s.reshape(nkeys // SUBLANES, SUBLANES, tile - q_lo)
            m_scr[mp, :, q_lo:] = jnp.maximum(m_scr[mp, :, q_lo:], jnp.max(grouped, axis=0))
            s_scr[mp, pl.ds(start, nkeys), q_lo:] = s

    def value_block(start, nkeys, q_lo, col_max):
        vb = jnp.concatenate([vt_ref[0, :, pl.ds(start, nkeys)], jnp.ones((ONES_ROWS, nkeys), BF16)],
                             axis=0)
        probs = [jnp.exp2(s_scr[mp, pl.ds(start, nkeys), q_lo:] - col_max[mp][:, q_lo:]).astype(BF16)
                 for mp in range(2)]
        for mp, p in enumerate(probs):
            a_scr[mp, :, q_lo:] += jnp.dot(vb, p, preferred_element_type=F32)

    diag = pl.multiple_of(qi * tile, tile)
    pieces = [(diag + j * ktile, ktile, j * ktile) for j in range(tile // ktile)]

    def score_body(ki, carry):
        score_block(pl.multiple_of(ki * tile, tile), tile, 0, False)
        return carry

    lax.fori_loop(0, qi, score_body, 0)
    for start, nkeys, q_lo in pieces:
        score_block(start, nkeys, q_lo, True)
    col_max = [jnp.max(m_scr[mp], axis=0, keepdims=True) for mp in range(2)]

    def value_body(ki, carry):
        value_block(pl.multiple_of(ki * tile, tile), tile, 0, col_max)
        return carry

    lax.fori_loop(0, qi, value_body, 0)
    for start, nkeys, q_lo in pieces:
        value_block(start, nkeys, q_lo, col_max)

    lp = lam_ref[...]
    lam = (jnp.exp(jnp.sum(lp[0:1] * lp[1:2], axis=-1, keepdims=True))
           - jnp.exp(jnp.sum(lp[2:3] * lp[3:4], axis=-1, keepdims=True)) + LAM_INIT)
    o = (a_scr[0, 0:HEAD_W, :] / a_scr[0, HEAD_W:HEAD_W + 1, :]
         - lam * (a_scr[1, 0:HEAD_W, :] / a_scr[1, HEAD_W:HEAD_W + 1, :]))
    o = o * lax.rsqrt(jnp.mean(o * o, axis=0, keepdims=True) + EPS) * sw_ref[...]
    o_ref[0] = (o * (1.0 - LAM_INIT)).T.astype(BF16)


def _attn(qt3, k3, vt3, lam_p, subln_col):
    bsz, seq, _ = k3.shape
    tile = ATTN_TILE
    return pl.pallas_call(
        functools.partial(_attn_body, tile=tile, ktile=ATTN_KEY_TILE),
        grid=(bsz, HEADS, seq // tile),
        in_specs=[pl.BlockSpec((1, HEAD_W, tile), lambda b, h, i: (b, h, i)),
                  pl.BlockSpec((1, seq, HEAD_W), lambda b, h, i: (b, 0, h)),
                  pl.BlockSpec((1, HEAD_W, seq), lambda b, h, i: (b, h, 0)),
                  pl.BlockSpec(lam_p.shape, lambda b, h, i: (0, 0)),
                  pl.BlockSpec((HEAD_W, 1), lambda b, h, i: (0, 0))],
        out_specs=pl.BlockSpec((1, tile, HEAD_W), lambda b, h, i: (b, i, h)),
        out_shape=jax.ShapeDtypeStruct((bsz, seq, STREAM_W), BF16),
        scratch_shapes=[pltpu.VMEM((2, seq, tile), F32),
                        pltpu.VMEM((2, SUBLANES, tile), F32),
                        pltpu.VMEM((2, HEAD_W + ONES_ROWS, tile), F32)],
        compiler_params=pltpu.CompilerParams(vmem_limit_bytes=56 * 1024 * 1024),
        name="diff_attn",
    )(qt3, k3, vt3, lam_p, subln_col)


def _hgrn_levels(block):
    return [block >> (i + 1) for i in range(block.bit_length() - 1)]


def _hgrn_constants(block):
    t = np.arange(block)[:, None]
    s = np.arange(block)[None, :]
    tril = (s <= t).astype(np.float32)
    lv = np.full((block, block), -1, np.int32)
    halves = _hgrn_levels(block)
    for li, m in enumerate(halves):
        same = (t // (2 * m)) == (s // (2 * m))
        lv[same & ((t & m) != 0) & ((s & m) == 0)] = li
    lv[np.arange(block), np.arange(block)] = len(halves)
    return jnp.asarray(tril, BF16), jnp.asarray(lv)


def _level_operand(b_ref, h, b, q, k, m, block):
    def ref_rows(r, n):
        return jnp.broadcast_to(b_ref[h, pl.ds(r, 1), :], (n, HEAD_W))

    if m >= SUBLANES:
        pieces = []
        for s0 in range(0, block, 2 * m):
            ref = ref_rows(s0 + m - 1, m)
            lo = slice(s0, s0 + m)
            up = slice(s0 + m, s0 + 2 * m)
            pieces.append(k[lo] * jnp.exp2(ref - b[lo]))
            pieces.append(q[up] * jnp.exp2(b[up] - ref))
        return jnp.concatenate(pieces, axis=0)

    sub = lax.broadcasted_iota(jnp.int32, (SUBLANES, HEAD_W), 0)
    refs = []
    for s0 in range(0, block, SUBLANES):
        piece = ref_rows(s0 + m - 1, SUBLANES)
        for j in range(1, SUBLANES // (2 * m)):
            piece = jnp.where(sub >= 2 * m * j, ref_rows(s0 + 2 * m * j + m - 1, SUBLANES), piece)
        refs.append(piece)
    d = b - jnp.concatenate(refs, axis=0)
    row = lax.broadcasted_iota(jnp.int32, (block, 1), 0)
    return jnp.where((row & m) != 0, q, k) * jnp.exp2(jnp.minimum(d, -d))


def _hgrn_body(hq_ref, lf_ref, kf_ref, gi_ref, sg_ref, nw_ref, tril_ref, lv_ref, o_ref,
               st_scr, b_scr, *, block, step):
    @pl.when(pl.program_id(1) == 0)
    def _():
        st_scr[...] = jnp.zeros(st_scr.shape, F32)

    tril = tril_ref[...]
    lv = lv_ref[...]
    halves = _hgrn_levels(block)

    units = [(u, r0, h) for u, (r0, h) in enumerate(
        (r0, h) for r0 in range(0, step, block) for h in range(HEADS))]

    def cols(h):
        return slice(h * HEAD_W, (h + 1) * HEAD_W)

    q, k, v, b, scores = {}, {}, {}, {}, {}
    for u, r0, h in units:
        rows = slice(r0, r0 + block)
        q[u] = hq_ref[0, rows, cols(h)]
        k[u] = kf_ref[0, rows, cols(h)]
        v[u] = gi_ref[0, rows, cols(h)]
        lf = lf_ref[0, rows, cols(h)]
        hi = lf.astype(BF16)
        r1 = lf - hi.astype(F32)
        mid = r1.astype(BF16)
        lo = (r1 - mid.astype(F32)).astype(BF16)
        b[u] = (jnp.dot(tril, hi, preferred_element_type=F32)
                + jnp.dot(tril, mid, preferred_element_type=F32)
                + jnp.dot(tril, lo, preferred_element_type=F32))
        b_scr[u] = b[u]
        scores[u] = jnp.where(lv == len(halves),
                              lax.dot_general(q[u].astype(BF16), k[u].astype(BF16), NT_DIMS,
                                              preferred_element_type=F32), 0.0)

    for li, m in enumerate(halves):
        for u, r0, h in units:
            xl = _level_operand(b_scr, u, b[u], q[u], k[u], m, block).astype(BF16)
            p = lax.dot_general(xl, xl, NT_DIMS, preferred_element_type=F32)
            scores[u] = jnp.where(lv == li, p, scores[u])

    for u, r0, h in units:
        rows = slice(r0, r0 + block)
        o_intra = jnp.dot(scores[u].astype(BF16), v[u], preferred_element_type=F32)
        st = st_scr[h]
        o_inter = lax.dot_general((q[u] * jnp.exp2(b[u])).astype(BF16), st.astype(BF16), NT_DIMS,
                                  preferred_element_type=F32)
        b_last = b[u][block - 1:block, :]
        kdec = (k[u] * jnp.exp2(b_last - b[u])).astype(BF16)
        st_scr[h] = st * jnp.exp2(b_last) + lax.dot_general(v[u], kdec, TN_DIMS,
                                                             preferred_element_type=F32)
        o = o_inter + o_intra
        o = o * lax.rsqrt(jnp.mean(o * o, axis=-1, keepdims=True) + EPS) * nw_ref[...]
        o_ref[0, rows, cols(h)] = (o * sg_ref[0, rows, cols(h)].astype(F32)).astype(BF16)


def _hgrn(hq3, lf3, kf3, gi3, sg3, norm_w):
    bsz, seq, _ = hq3.shape
    block = HGRN_BLOCK
    step = HGRN_STEP
    tril, lv = _hgrn_constants(block)
    blk = pl.BlockSpec((1, step, STREAM_W), lambda b, g: (b, g, 0))
    const = lambda b, g: (0, 0)
    return pl.pallas_call(
        functools.partial(_hgrn_body, block=block, step=step),
        grid=(bsz, seq // step),
        in_specs=[blk, blk, blk, blk, blk,
                  pl.BlockSpec((1, HEAD_W), const),
                  pl.BlockSpec((block, block), const),
                  pl.BlockSpec((block, block), const)],
        out_specs=blk,
        out_shape=jax.ShapeDtypeStruct((bsz, seq, STREAM_W), BF16),
        scratch_shapes=[pltpu.VMEM((HEADS, HEAD_W, HEAD_W), F32),
                        pltpu.VMEM((HEADS * step // block, block, HEAD_W), F32)],
        name="hgrn2",
    )(hq3, lf3, kf3, gi3, sg3, norm_w, tril, lv)


META_E0, META_E1, META_R0, META_R1, META_W0, META_W1 = range(6)
GROUP_LANE0 = N_EXPERTS


OUT_PARTS = 8


def _route(logits, lane):
    far = jnp.int32(LANES)

    def first_max(vals):
        mx = jnp.max(vals, axis=-1, keepdims=True)
        return mx, jnp.min(jnp.where(vals == mx, lane, far), axis=-1, keepdims=True)

    is_g = (lane >= GROUP_LANE0) & (lane < GROUP_LANE0 + N_GROUPS)
    gmax, glane = first_max(jnp.where(is_g, logits, NEG_BIG))
    g_w = 1.0 / jnp.sum(jnp.where(is_g, jnp.exp(logits - gmax), 0.0), axis=-1, keepdims=True)
    gidx = glane - GROUP_LANE0
    in_grp = (lane < N_EXPERTS) & ((lane // EXPERTS_PER_GROUP) == gidx)
    el = jnp.where(in_grp, logits, NEG_BIG)
    m1, i1 = first_max(el)
    m2, i2 = first_max(jnp.where(lane == i1, NEG_BIG, el))
    r = jnp.exp(m2 - m1)
    return i1, i2, g_w / (1.0 + r), g_w * r / (1.0 + r)


def _outproj_body(da_ref, hg_ref, x_ref, mod_ref, wo_ref, nw_ref, wr_ref, br_ref, stril_ref,
                  x1_ref, h2_ref, meta_ref, route_ref, cnt_ref, carry_scr, piece_scr):
    @pl.when(pl.program_id(0) == 0)
    def _():
        carry_scr[...] = jnp.zeros(carry_scr.shape, F32)

    rows = stril_ref.shape[0]
    parts = [slice(p * rows, (p + 1) * rows) for p in range(OUT_PARTS)]
    gate1 = mod_ref[0, 2:3, :]
    shift2 = mod_ref[0, 3:4, :]
    scale2 = mod_ref[0, 4:5, :]

    attn = [jnp.dot(da_ref[r, :], wo_ref[0:STREAM_W, :], preferred_element_type=F32)
            + jnp.dot(hg_ref[r, :], wo_ref[STREAM_W:, :], preferred_element_type=F32) for r in parts]

    h2 = []
    for r, a in zip(parts, attn):
        x1 = x_ref[r, :] + gate1 * a
        x1_ref[r, :] = x1
        h2.append(x1 * lax.rsqrt(jnp.mean(x1 * x1, axis=-1, keepdims=True) + EPS) * nw_ref[...]
                  * (1.0 + scale2) + shift2)

    logits = []
    for p, h in enumerate(h2):
        _slab_store_bf16(h2_ref.at[pl.ds(p * rows * SLAB, rows * SLAB)], h, piece_scr.at[p])
        h_hi = h.astype(BF16)
        h_lo = (h - h_hi.astype(F32)).astype(BF16)
        terms = jnp.dot(jnp.concatenate([h_hi, h_lo], axis=0), wr_ref[...],
                        preferred_element_type=F32)
        logits.append((terms[:rows, :LANES] + terms[:rows, LANES:])
                      + (terms[rows:, :LANES] + terms[rows:, LANES:]) + br_ref[...])

    lane = lax.broadcasted_iota(jnp.int32, (rows, LANES), 1)
    routed = [_route(lg, lane) for lg in logits]

    carry = carry_scr[...]
    for r, (i1, i2, w0, w1) in zip(parts, routed):
        hot0 = lane == i1
        hot1 = lane == i2
        multi = jnp.where(hot0 | hot1, 1.0, 0.0)
        before = jnp.dot(stril_ref[...], multi.astype(BF16), preferred_element_type=F32) + carry
        rank0 = jnp.sum(jnp.where(hot0, before, 0.0), axis=-1, keepdims=True)
        rank1 = jnp.sum(jnp.where(hot1, before, 0.0), axis=-1, keepdims=True)
        carry = carry + jnp.sum(multi, axis=0, keepdims=True)
        meta = jnp.zeros((rows, LANES), F32)
        for idx, val in ((META_E0, i1.astype(F32)), (META_E1, i2.astype(F32)),
                         (META_R0, rank0), (META_R1, rank1), (META_W0, w0), (META_W1, w1)):
            meta = jnp.where(lane == idx, val, meta)
        meta_ref[r, :] = meta
        route_ref[:, r] = meta.T[0:SUBLANES, :]
    carry_scr[...] = carry
    cnt_ref[...] = carry


def _outproj(da2, hg2, x2, mod3, wo_bf, norm_w, w_route, b_route, seq):
    t, d = x2.shape
    tm = OUT_TILE
    per_b = seq // tm
    row = lambda i: (i, 0)
    full = lambda i: (0, 0)
    part = tm // OUT_PARTS
    stril = jnp.asarray(np.tril(np.ones((part, part), np.float32), -1), BF16)
    return pl.pallas_call(
        _outproj_body,
        grid=(t // tm,),
        in_specs=[pl.BlockSpec((tm, STREAM_W), row),
                  pl.BlockSpec((tm, STREAM_W), row),
                  pl.BlockSpec((tm, d), row),
                  pl.BlockSpec((1, 6, d), lambda i: (i // per_b, 0, 0)),
                  pl.BlockSpec((2 * STREAM_W, d), full),
                  pl.BlockSpec((1, d), full),
                  pl.BlockSpec((d, 2 * LANES), full),
                  pl.BlockSpec((1, LANES), full),
                  pl.BlockSpec((part, part), full)],
        out_specs=[pl.BlockSpec((tm, d), row),
                   pl.BlockSpec((tm * SLAB, LANES), row),
                   pl.BlockSpec((tm, LANES), row),
                   pl.BlockSpec((SUBLANES, tm), lambda i: (0, i)),
                   pl.BlockSpec((1, LANES), full)],
        out_shape=[jax.ShapeDtypeStruct((t, d), F32),
                   jax.ShapeDtypeStruct((t * SLAB, LANES), BF16),
                   jax.ShapeDtypeStruct((t, LANES), F32),
                   jax.ShapeDtypeStruct((SUBLANES, t), F32),
                   jax.ShapeDtypeStruct((1, LANES), F32)],
        scratch_shapes=[pltpu.VMEM((1, LANES), F32),
                        pltpu.VMEM((OUT_PARTS, part * SLAB, LANES), F32)],
        compiler_params=pltpu.CompilerParams(dimension_semantics=("arbitrary",)),
        name="outproj_route",
    )(da2, hg2, x2, mod3, wo_bf, norm_w, w_route, b_route, stril)


DMA_UNROLL = 8


ZERO_CHUNK = 128
N_FILL_RANGES = N_EXPERTS + 1


def _scatter_body(fill_ref, pos_ref, h2_ref, wg_ref, wu_ref, wd_ref, xs_hbm, wgu_ref, wdb_ref,
                  zero_scr, sem, zsem, *, tile):
    rows = tile * SLAB

    wgu_ref[:, :, 0:EXPERT_FF] = wg_ref[...].astype(BF16)
    wgu_ref[:, :, EXPERT_FF:] = wu_ref[...].astype(BF16)
    wdb_ref[...] = wd_ref[...].astype(BF16)

    @pl.when(pl.program_id(0) == 0)
    def _():
        zero_scr[...] = jnp.zeros(zero_scr.shape, BF16)

        def clear(chunk):
            start = pl.multiple_of(chunk * (ZERO_CHUNK * SLAB), SLAB)
            return pltpu.make_async_copy(zero_scr, xs_hbm.at[pl.ds(start, ZERO_CHUNK * SLAB)], zsem)

        def start_one(chunk, carry):
            clear(chunk).start()
            return carry

        def wait_one(chunk, carry):
            clear(chunk).wait()
            return carry

        for fn in (start_one, wait_one):
            for r in range(N_FILL_RANGES):
                lax.fori_loop(fill_ref[r], fill_ref[N_FILL_RANGES + r], fn, 0)

    def issue(t, carry):
        src = h2_ref.at[pl.ds(pl.multiple_of(t * SLAB, SLAB), SLAB)]
        for j in range(2):
            slot = pl.multiple_of(pos_ref[0, 0, j * tile + t] * SLAB, SLAB)
            pltpu.make_async_copy(src, xs_hbm.at[pl.ds(slot, SLAB)], sem).start(priority=j)
        return carry

    lax.fori_loop(0, tile, issue, 0, unroll=DMA_UNROLL)
    whole = pltpu.make_async_copy(h2_ref, xs_hbm.at[pl.ds(0, rows)], sem)
    whole.wait()
    whole.wait()


def _scatter(fill_ranges, pos3, h2s, n_slots, w_gate, w_up, w_down):
    tile = SCATTER_TILE
    rows = tile * SLAB
    n_steps = h2s.shape[0] // rows
    n_exp, d, ff = w_gate.shape
    assert n_exp % n_steps == 0, "expert weights are cast in equal shares per scatter step"
    share = n_exp // n_steps
    w_in = lambda i, lt: (i, 0, 0)
    grid_spec = pltpu.PrefetchScalarGridSpec(
        num_scalar_prefetch=1,
        grid=(n_steps,),
        in_specs=[pl.BlockSpec((1, 1, 2 * tile), lambda i, lt: (i, 0, 0), memory_space=pltpu.SMEM),
                  pl.BlockSpec((rows, LANES), lambda i, lt: (i, 0)),
                  pl.BlockSpec((share, d, ff), w_in),
                  pl.BlockSpec((share, d, ff), w_in),
                  pl.BlockSpec((share, ff, d), w_in)],
        out_specs=[pl.BlockSpec(memory_space=pl.ANY),
                   pl.BlockSpec((share, d, 2 * ff), w_in),
                   pl.BlockSpec((share, ff, d), w_in)],
        scratch_shapes=[pltpu.VMEM((ZERO_CHUNK * SLAB, LANES), BF16),
                        pltpu.SemaphoreType.DMA, pltpu.SemaphoreType.DMA],
    )
    return pl.pallas_call(
        functools.partial(_scatter_body, tile=tile),
        grid_spec=grid_spec,
        out_shape=[jax.ShapeDtypeStruct((n_slots * SLAB, LANES), BF16),
                   jax.ShapeDtypeStruct((n_exp, d, 2 * ff), BF16),
                   jax.ShapeDtypeStruct((n_exp, ff, d), BF16)],
        compiler_params=pltpu.CompilerParams(dimension_semantics=("arbitrary",)),
        name="moe_scatter",
    )(fill_ranges, pos3, h2s, w_gate, w_up, w_down)


def _experts_body(te_ref, nv_ref, x_ref, wgu_ref, wd_ref, y_ref, piece_scr, *, tm):
    del te_ref
    i = pl.program_id(0)

    @pl.when(i < nv_ref[0])
    def _():
        part = tm // EXPERT_PARTS
        xr = [x_ref.at[pl.ds(r * part * SLAB, part * SLAB)] for r in range(EXPERT_PARTS)]
        yr = [y_ref.at[pl.ds(r * part * SLAB, part * SLAB)] for r in range(EXPERT_PARTS)]
        xs = [_slab_load_bf16(ref, part, piece_scr.at[r]).astype(BF16) for r, ref in enumerate(xr)]
        gus = [jnp.dot(x, wgu_ref[0], preferred_element_type=F32) for x in xs]
        acts = [(gu[:, 0:EXPERT_FF] * _sigmoid(gu[:, 0:EXPERT_FF]) * gu[:, EXPERT_FF:]).astype(BF16)
                for gu in gus]
        ys = [jnp.dot(act, wd_ref[0], preferred_element_type=F32) for act in acts]
        for ref, y in zip(yr, ys):
            _slab_store(ref, y)

    @pl.when(i >= nv_ref[0])
    def _():
        y_ref[...] = jnp.zeros(y_ref.shape, F32)


def _experts(tile_expert, n_valid, xs, w_gate_up, w_down):
    tm = MOE_TILE
    rows = tm * SLAB
    d = w_gate_up.shape[1]
    live = lambda i, te, nv: (jnp.minimum(i, nv[0] - 1), 0)
    grid_spec = pltpu.PrefetchScalarGridSpec(
        num_scalar_prefetch=2,
        grid=(xs.shape[0] // rows,),
        in_specs=[pl.BlockSpec((rows, LANES), live),
                  pl.BlockSpec((1, d, 2 * EXPERT_FF), lambda i, te, nv: (te[i], 0, 0)),
                  pl.BlockSpec((1, EXPERT_FF, d), lambda i, te, nv: (te[i], 0, 0))],
        out_specs=pl.BlockSpec((rows, LANES), lambda i, te, nv: (i, 0)),
        scratch_shapes=[pltpu.VMEM((EXPERT_PARTS, rows // EXPERT_PARTS, LANES), F32)],
    )
    return pl.pallas_call(
        functools.partial(_experts_body, tm=tm),
        grid_spec=grid_spec,
        out_shape=jax.ShapeDtypeStruct(xs.shape, F32),
        compiler_params=pltpu.CompilerParams(dimension_semantics=("arbitrary",)),
        name="moe_experts",
    )(tile_expert, n_valid, xs, w_gate_up, w_down)


def _combine_body(pos_ref, nxt_ref, x1_ref, meta_ref, mod_ref, nw_ref, ys_hbm, o_ref, rows_scr, sems,
                  *, tile):
    i = pl.program_id(0)
    n = pl.num_programs(0)
    rows = tile * SLAB

    def fetch(idx_ref, buf):
        def issue(t, carry):
            dst = pl.ds(pl.multiple_of(t * SLAB, SLAB), SLAB)
            for j in range(2):
                slot = pl.multiple_of(idx_ref[0, 0, j * tile + t] * SLAB, SLAB)
                pltpu.make_async_copy(ys_hbm.at[pl.ds(slot, SLAB)], rows_scr.at[buf, j, dst],
                                      sems.at[buf]).start(priority=j)
            return carry
        lax.fori_loop(0, tile, issue, 0, unroll=DMA_UNROLL)

    @pl.when(i == 0)
    def _():
        fetch(pos_ref, 0)

    for cur in range(2):
        @pl.when((i % 2 == cur) & (i + 1 < n))
        def _():
            fetch(nxt_ref, 1 - cur)

    for cur in range(2):
        @pl.when(i % 2 == cur)
        def _():
            for j in range(2):
                pltpu.make_async_copy(ys_hbm.at[pl.ds(0, rows)], rows_scr.at[cur, j],
                                      sems.at[cur]).wait()
            meta = meta_ref[...]
            w0 = meta[:, META_W0:META_W0 + 1]
            w1 = meta[:, META_W1:META_W1 + 1]
            y = (w0 * _slab_load(rows_scr.at[cur, 0], tile)
                 + w1 * _slab_load(rows_scr.at[cur, 1], tile))
            x2 = x1_ref[...] + mod_ref[0, 5:6, :] * y
            o_ref[...] = (x2 * lax.rsqrt(jnp.mean(x2 * x2, axis=-1, keepdims=True) + EPS)
                          * nw_ref[...])


def _combine(pos3, x1, meta, mod3, norm_w, ys, seq):
    t, d = x1.shape
    tile = COMBINE_TILE
    per_b = seq // tile
    n = t // tile
    row = lambda i: (i, 0)
    return pl.pallas_call(
        functools.partial(_combine_body, tile=tile),
        grid=(n,),
        in_specs=[pl.BlockSpec((1, 1, 2 * tile), lambda i: (i, 0, 0), memory_space=pltpu.SMEM),
                  pl.BlockSpec((1, 1, 2 * tile), lambda i: (jnp.minimum(i + 1, n - 1), 0, 0),
                               memory_space=pltpu.SMEM),
                  pl.BlockSpec((tile, d), row),
                  pl.BlockSpec((tile, LANES), row),
                  pl.BlockSpec((1, 6, d), lambda i: (i // per_b, 0, 0)),
                  pl.BlockSpec((1, d), lambda i: (0, 0)),
                  pl.BlockSpec(memory_space=pl.ANY)],
        out_specs=pl.BlockSpec((tile, d), row),
        out_shape=jax.ShapeDtypeStruct((t, d), F32),
        scratch_shapes=[pltpu.VMEM((2, 2, tile * SLAB, LANES), F32), pltpu.SemaphoreType.DMA((2,))],
        compiler_params=pltpu.CompilerParams(dimension_semantics=("arbitrary",)),
        name="moe_combine",
    )(pos3, pos3, x1, meta, mod3, norm_w, ys)


def _rope_constants():
    inv_freq = ROPE_THETA ** (-jnp.arange(ROT_HALF, dtype=F32) / ROT_HALF)
    lane = np.arange(LANES) % DA_QK_DIM
    hit = (lane[None, :] < ROT_DIM) & (lane[None, :] % ROT_HALF == np.arange(ROT_HALF)[:, None])
    spread = np.concatenate([hit, hit], axis=0).astype(np.float32)
    return inv_freq.reshape(ROT_HALF, 1), jnp.asarray(spread, BF16)


def kernel(x, c, positions, norm1_w, norm2_w, final_norm_w, ada_w, ada_b, w_in, w_out, da_lambda_q1, da_lambda_k1, da_lambda_q2, da_lambda_k2, da_subln_w, hg_lower_bound, hg_norm_w, moe_w_group, moe_b_group, moe_w_router, moe_b_router, moe_w_gate, moe_w_up, moe_w_down):
    bsz, seq, d = x.shape
    assert d == D_MODEL and norm1_w.shape[0] == 1, "single-layer model of width 1024 only"
    assert seq % ATTN_TILE == 0 and seq % HGRN_STEP == 0 and seq % OUT_TILE == 0
    t = bsz * seq
    x2 = x.reshape(t, d)

    mod3 = _adaln(c, ada_w[0], ada_b).reshape(bsz, 6, d)

    pos_rows = positions.astype(F32).reshape(t // ROW_TILE, 1, ROW_TILE)
    inv_freq, spread = _rope_constants()
    qt, k, vt, hq, lf, kf, gi, sg = _inproj(x2, mod3, norm1_w, w_in[0], pos_rows,
                                          inv_freq, spread, hg_lower_bound, seq)

    as3 = lambda a: a.reshape(bsz, seq, STREAM_W)
    lam_p = jnp.concatenate([da_lambda_q1, da_lambda_k1, da_lambda_q2, da_lambda_k2], axis=0)
    da = _attn(qt, as3(k), vt, lam_p, da_subln_w.reshape(HEAD_W, 1))
    hg = _hgrn(as3(hq), as3(lf), as3(kf), as3(gi), as3(sg), hg_norm_w)

    pad = jnp.zeros((d, LANES - N_EXPERTS - N_GROUPS), F32)
    w_route = jnp.concatenate([moe_w_router[0], moe_w_group[0], pad], axis=1)
    b_route = jnp.concatenate([moe_b_router[0], moe_b_group[0], pad[0]]).reshape(1, LANES)
    w_route_hi = w_route.astype(BF16)
    w_route_lo = (w_route - w_route_hi.astype(F32)).astype(BF16)
    x1, h2, meta, route, cnt = _outproj(da.reshape(t, STREAM_W), hg.reshape(t, STREAM_W), x2, mod3,
                                 w_out[0].astype(BF16), norm2_w,
                                 jnp.concatenate([w_route_hi, w_route_lo], axis=1), b_route, seq)

    counts = cnt[0, :N_EXPERTS].astype(jnp.int32)
    tiles_e = (counts + MOE_TILE - 1) // MOE_TILE
    tile_end = jnp.cumsum(tiles_e)
    offs = (tile_end - tiles_e) * MOE_TILE
    ids = route[META_E0:META_E1 + 1].astype(jnp.int32)
    ranks = route[META_R0:META_R1 + 1].astype(jnp.int32)
    expert_iota = jnp.arange(N_EXPERTS, dtype=jnp.int32)[:, None, None]
    pos = jnp.sum(jnp.where(ids[None] == expert_iota, offs[:, None, None], 0), axis=0) + ranks
    n_tiles = (2 * t) // MOE_TILE + N_EXPERTS
    n_valid = tile_end[-1:]
    tile_ids = jnp.minimum(jnp.arange(n_tiles, dtype=jnp.int32), n_valid - 1)
    tile_expert = jnp.sum(tile_ids[:, None] >= tile_end[None, :], axis=1).astype(jnp.int32)
    def slot_table(tile):
        return pos.reshape(2, t // tile, tile).transpose(1, 0, 2).reshape(t // tile, 1, 2 * tile)

    fill_lo = jnp.concatenate([(offs + counts) // ZERO_CHUNK, n_valid * (MOE_TILE // ZERO_CHUNK)])
    fill_hi = jnp.concatenate([tile_end * (MOE_TILE // ZERO_CHUNK),
                               jnp.full((1,), n_tiles * (MOE_TILE // ZERO_CHUNK), jnp.int32)])
    fill_ranges = jnp.concatenate([fill_lo, fill_hi]).astype(jnp.int32)

    xs, w_gate_up, w_down_bf = _scatter(fill_ranges, slot_table(SCATTER_TILE), h2, n_tiles * MOE_TILE,
                                        moe_w_gate[0], moe_w_up[0], moe_w_down[0])
    ys = _experts(tile_expert, n_valid.astype(jnp.int32), xs, w_gate_up, w_down_bf)
    out = _combine(slot_table(COMBINE_TILE), x1, meta, mod3, final_norm_w.reshape(1, d), ys, seq)
    return out.reshape(bsz, seq, d)
```

```python
import functools
import math

import numpy as np
import jax
import jax.numpy as jnp
from jax import lax
from jax.experimental import pallas as pl
from jax.experimental.pallas import tpu as pltpu

F32 = jnp.float32
BF16 = jnp.bfloat16
HIGHEST = lax.Precision.HIGHEST

LANES = 128
SUBLANES = 8
D_MODEL = 1024
HEADS = 4
HEAD_W = 128
STREAM_W = HEADS * HEAD_W
N_STREAMS = 7
DA_QK_DIM = 64
ROPE_THETA = 500000.0
ROT_DIM = DA_QK_DIM // 4
ROT_HALF = ROT_DIM // 2
N_GROUPS = 4
EXPERTS_PER_GROUP = 8
N_EXPERTS = N_GROUPS * EXPERTS_PER_GROUP
EXPERT_FF = 256
EPS = 1e-6
LAM_INIT = 0.8 - 0.6 * math.exp(-0.3 * 0)
NEG_BIG = -1e30

ADALN_TILE = 2048
ROW_TILE = 512
OUT_TILE = 1024
ATTN_TILE = 1024
ATTN_KEY_TILE = 256
HGRN_BLOCK = 128
HGRN_STEP = 256
MOE_TILE = 512
EXPERT_PARTS = 2
SCATTER_TILE = 1024
COMBINE_TILE = 256

NT_DIMS = (((1,), (1,)), ((), ()))
TN_DIMS = (((0,), (0,)), ((), ()))


def _sigmoid(x):
    return 1.0 / (1.0 + jnp.exp(-x))


SLAB = D_MODEL // LANES


def _slab_store(ref, val):
    rows = val.shape[0]
    for c in range(SLAB):
        ref[pl.ds(c, rows, stride=SLAB), :] = val[:, c * LANES:(c + 1) * LANES]


def _slab_load(ref, rows):
    return jnp.concatenate([ref[pl.ds(c, rows, stride=SLAB), :] for c in range(SLAB)], axis=1)


def _slab_store_bf16(ref, val, piece_scr):
    _slab_store(piece_scr, val)
    ref[...] = piece_scr[...].astype(BF16)


def _slab_load_bf16(ref, rows, piece_scr):
    piece_scr[...] = ref[...].astype(F32)
    return _slab_load(piece_scr, rows)


def _adaln_body(c_ref, w_ref, b_ref, o_ref):
    c = c_ref[...]
    ca = c * _sigmoid(c)
    o_ref[...] = jnp.dot(ca, w_ref[...], preferred_element_type=F32, precision=HIGHEST) + b_ref[...]


def _adaln(c, ada_w, ada_b):
    bsz, d = c.shape
    n = ada_w.shape[1]
    tn = ADALN_TILE
    return pl.pallas_call(
        _adaln_body,
        grid=(n // tn,),
        in_specs=[pl.BlockSpec((bsz, d), lambda j: (0, 0)),
                  pl.BlockSpec((d, tn), lambda j: (0, j)),
                  pl.BlockSpec((1, tn), lambda j: (0, j))],
        out_specs=pl.BlockSpec((bsz, tn), lambda j: (0, j)),
        out_shape=jax.ShapeDtypeStruct((bsz, n), F32),
        name="adaln",
    )(c, ada_w, ada_b)


def _inproj_body(x_ref, mod_ref, nw_ref, w32_ref, pos_ref, invf_ref, spread_ref, lbr_ref,
                 q_ref, k_ref, v_ref, hq_ref, lf_ref, kf_ref, gi_ref, sg_ref, w_ref):
    @pl.when(pl.program_id(0) == 0)
    def _():
        for j in range(N_STREAMS):
            cols = slice(j * STREAM_W, (j + 1) * STREAM_W)
            w_ref[:, cols] = w32_ref[:, cols].astype(BF16)

    x = x_ref[...]
    ms = jnp.mean(x * x, axis=-1, keepdims=True)
    y = x * lax.rsqrt(ms + EPS) * nw_ref[...]
    shift = mod_ref[0, 0:1, :]
    scale = mod_ref[0, 1:2, :]
    h = (y * (1.0 + scale) + shift).astype(BF16)

    ang_t = invf_ref[...] * pos_ref[0]

    def spread(table_t):
        hi = table_t.astype(BF16)
        lo = (table_t - hi.astype(F32)).astype(BF16)
        return lax.dot_general(jnp.concatenate([hi, lo], axis=0), spread_ref[...], TN_DIMS,
                               preferred_element_type=F32)

    lane = lax.broadcasted_iota(jnp.int32, (1, LANES), 1) % DA_QK_DIM
    cosv = spread(jnp.cos(ang_t)) + jnp.where(lane < ROT_DIM, 0.0, 1.0)
    sinv = spread(jnp.sin(ang_t))
    sin_lo = jnp.where(lane < ROT_HALF, -sinv, 0.0)
    sin_hi = jnp.where((lane >= ROT_HALF) & (lane < ROT_DIM), sinv, 0.0)

    streams = [jnp.dot(h, w_ref[:, j * STREAM_W:(j + 1) * STREAM_W], preferred_element_type=F32)
               for j in range(N_STREAMS)]

    def proj(j):
        return streams[j]

    def rope(t):
        outs = []
        for hb in range(HEADS):
            tc = t[:, hb * HEAD_W:(hb + 1) * HEAD_W]
            outs.append(tc * cosv
                        + pltpu.roll(tc, LANES - ROT_HALF, 1) * sin_lo
                        + pltpu.roll(tc, ROT_HALF, 1) * sin_hi)
        return jnp.concatenate(outs, axis=1)

    q_ref[0] = (rope(proj(0)) * (DA_QK_DIM ** -0.5 * math.log2(math.e))).T.astype(BF16)
    k_ref[...] = rope(proj(1)).astype(BF16)
    v_ref[0] = proj(2).astype(BF16).T

    gq = proj(3)
    hq_ref[...] = gq * _sigmoid(gq)

    a = lbr_ref[...]
    amax = jnp.max(a, axis=0, keepdims=True)
    ea = jnp.exp(a - amax)
    lb = ea[0:1, :] / jnp.sum(ea, axis=0, keepdims=True)
    gf = proj(4)
    f = lb + (1.0 - lb) * _sigmoid(gf)
    lf_ref[...] = jnp.log(f) * math.log2(math.e)
    kf_ref[...] = 1.0 - f

    gi_ref[...] = proj(5).astype(BF16)
    gg = proj(6)
    sg_ref[...] = (gg * _sigmoid(gg)).astype(BF16)


def _inproj(x2, mod3, norm_w, w_f32, pos_rows, invf, spread, lb_raw, seq):
    t, d = x2.shape
    tm = ROW_TILE
    per_b = seq // tm
    row = lambda i: (i, 0)
    full = lambda i: (0, 0)
    out_bf = jax.ShapeDtypeStruct((t, STREAM_W), BF16)
    out_f = jax.ShapeDtypeStruct((t, STREAM_W), F32)
    out_t = jax.ShapeDtypeStruct((t // seq, STREAM_W, seq), BF16)
    stream = pl.BlockSpec((tm, STREAM_W), row)
    stream_t = pl.BlockSpec((1, STREAM_W, tm), lambda i: (i // per_b, 0, i % per_b))
    return pl.pallas_call(
        _inproj_body,
        grid=(t // tm,),
        in_specs=[pl.BlockSpec((tm, d), row),
                  pl.BlockSpec((1, 6, d), lambda i: (i // per_b, 0, 0)),
                  pl.BlockSpec((1, d), full),
                  pl.BlockSpec((d, N_STREAMS * STREAM_W), full, pipeline_mode=pl.Buffered(1)),
                  pl.BlockSpec((1, 1, tm), lambda i: (i, 0, 0)),
                  pl.BlockSpec((ROT_HALF, 1), full),
                  pl.BlockSpec((2 * ROT_HALF, LANES), full),
                  pl.BlockSpec(lb_raw.shape, full)],
        out_specs=[stream_t, stream, stream_t] + [stream] * 5,
        out_shape=[out_t, out_bf, out_t, out_f, out_f, out_f, out_bf, out_bf],
        scratch_shapes=[pltpu.VMEM((d, N_STREAMS * STREAM_W), BF16)],
        compiler_params=pltpu.CompilerParams(dimension_semantics=("arbitrary",),
                                             vmem_limit_bytes=56 * 1024 * 1024),
        name="inproj",
    )(x2, mod3, norm_w, w_f32, pos_rows, invf, spread, lb_raw)


ONES_ROWS = 16


def _attn_body(qt_ref, k_ref, vt_ref, lam_ref, sw_ref, o_ref, s_scr, m_scr, a_scr, *, tile, ktile):
    qi = pl.program_id(2)
    qt = qt_ref[0]
    feat = lax.broadcasted_iota(jnp.int32, (HEAD_W, 1), 0)
    zero = jnp.zeros_like(qt)
    qmaps = (jnp.where(feat < DA_QK_DIM, qt, zero), jnp.where(feat >= DA_QK_DIM, qt, zero))

    m_scr[...] = jnp.full(m_scr.shape, NEG_BIG, F32)
    a_scr[...] = jnp.zeros(a_scr.shape, F32)

    def score_block(start, nkeys, q_lo, masked):
        kb = k_ref[0, pl.ds(start, nkeys), :]
        scores = [jnp.dot(kb, qmaps[mp][:, q_lo:], preferred_element_type=F32) for mp in range(2)]
        for mp, s in enumerate(scores):
            if masked:
                key = lax.broadcasted_iota(jnp.int32, s.shape, 0)
                qry = lax.broadcasted_iota(jnp.int32, s.shape, 1)
                s = jnp.where(key <= qry, s, NEG_BIG)
            grouped = s.reshape(nkeys // SUBLANES, SUBLANES, tile - q_lo)
            m_scr[mp, :, q_lo:] = jnp.maximum(m_scr[mp, :, q_lo:], jnp.max(grouped, axis=0))
            s_scr[mp, pl.ds(start, nkeys), q_lo:] = s

    def value_block(start, nkeys, q_lo, col_max):
        vb = jnp.concatenate([vt_ref[0, :, pl.ds(start, nkeys)], jnp.ones((ONES_ROWS, nkeys), BF16)],
                             axis=0)
        probs = [jnp.exp2(s_scr[mp, pl.ds(start, nkeys), q_lo:] - col_max[mp][:, q_lo:]).astype(BF16)
                 for mp in range(2)]
        for mp, p in enumerate(probs):
            a_scr[mp, :, q_lo:] += jnp.dot(vb, p, preferred_element_type=F32)

    diag = pl.multiple_of(qi * tile, tile)
    pieces = [(diag + j * ktile, ktile, j * ktile) for j in range(tile // ktile)]

    def score_body(ki, carry):
        score_block(pl.multiple_of(ki * tile, tile), tile, 0, False)
        return carry

    lax.fori_loop(0, qi, score_body, 0)
    for start, nkeys, q_lo in pieces:
        score_block(start, nkeys, q_lo, True)
    col_max = [jnp.max(m_scr[mp], axis=0, keepdims=True) for mp in range(2)]

    def value_body(ki, carry):
        value_block(pl.multiple_of(ki * tile, tile), tile, 0, col_max)
        return carry

    lax.fori_loop(0, qi, value_body, 0)
    for start, nkeys, q_lo in pieces:
        value_block(start, nkeys, q_lo, col_max)

    lp = lam_ref[...]
    lam = (jnp.exp(jnp.sum(lp[0:1] * lp[1:2], axis=-1, keepdims=True))
           - jnp.exp(jnp.sum(lp[2:3] * lp[3:4], axis=-1, keepdims=True)) + LAM_INIT)
    o = (a_scr[0, 0:HEAD_W, :] / a_scr[0, HEAD_W:HEAD_W + 1, :]
         - lam * (a_scr[1, 0:HEAD_W, :] / a_scr[1, HEAD_W:HEAD_W + 1, :]))
    o = o * lax.rsqrt(jnp.mean(o * o, axis=0, keepdims=True) + EPS) * sw_ref[...]
    o_ref[0] = (o * (1.0 - LAM_INIT)).T.astype(BF16)


def _attn(qt3, k3, vt3, lam_p, subln_col):
    bsz, seq, _ = k3.shape
    tile = ATTN_TILE
    return pl.pallas_call(
        functools.partial(_attn_body, tile=tile, ktile=ATTN_KEY_TILE),
        grid=(bsz, HEADS, seq // tile),
        in_specs=[pl.BlockSpec((1, HEAD_W, tile), lambda b, h, i: (b, h, i)),
                  pl.BlockSpec((1, seq, HEAD_W), lambda b, h, i: (b, 0, h)),
                  pl.BlockSpec((1, HEAD_W, seq), lambda b, h, i: (b, h, 0)),
                  pl.BlockSpec(lam_p.shape, lambda b, h, i: (0, 0)),
                  pl.BlockSpec((HEAD_W, 1), lambda b, h, i: (0, 0))],
        out_specs=pl.BlockSpec((1, tile, HEAD_W), lambda b, h, i: (b, i, h)),
        out_shape=jax.ShapeDtypeStruct((bsz, seq, STREAM_W), BF16),
        scratch_shapes=[pltpu.VMEM((2, seq, tile), F32),
                        pltpu.VMEM((2, SUBLANES, tile), F32),
                        pltpu.VMEM((2, HEAD_W + ONES_ROWS, tile), F32)],
        compiler_params=pltpu.CompilerParams(vmem_limit_bytes=56 * 1024 * 1024),
        name="diff_attn",
    )(qt3, k3, vt3, lam_p, subln_col)


def _hgrn_levels(block):
    return [block >> (i + 1) for i in range(block.bit_length() - 1)]


def _hgrn_constants(block):
    t = np.arange(block)[:, None]
    s = np.arange(block)[None, :]
    tril = (s <= t).astype(np.float32)
    lv = np.full((block, block), -1, np.int32)
    halves = _hgrn_levels(block)
    for li, m in enumerate(halves):
        same = (t // (2 * m)) == (s // (2 * m))
        lv[same & ((t & m) != 0) & ((s & m) == 0)] = li
    lv[np.arange(block), np.arange(block)] = len(halves)
    return jnp.asarray(tril, BF16), jnp.asarray(lv)


def _level_operand(b_ref, h, b, q, k, m, block):
    def ref_rows(r, n):
        return jnp.broadcast_to(b_ref[h, pl.ds(r, 1), :], (n, HEAD_W))

    if m >= SUBLANES:
        pieces = []
        for s0 in range(0, block, 2 * m):
            ref = ref_rows(s0 + m - 1, m)
            lo = slice(s0, s0 + m)
            up = slice(s0 + m, s0 + 2 * m)
            pieces.append(k[lo] * jnp.exp2(ref - b[lo]))
            pieces.append(q[up] * jnp.exp2(b[up] - ref))
        return jnp.concatenate(pieces, axis=0)

    sub = lax.broadcasted_iota(jnp.int32, (SUBLANES, HEAD_W), 0)
    refs = []
    for s0 in range(0, block, SUBLANES):
        piece = ref_rows(s0 + m - 1, SUBLANES)
        for j in range(1, SUBLANES // (2 * m)):
            piece = jnp.where(sub >= 2 * m * j, ref_rows(s0 + 2 * m * j + m - 1, SUBLANES), piece)
        refs.append(piece)
    d = b - jnp.concatenate(refs, axis=0)
    row = lax.broadcasted_iota(jnp.int32, (block, 1), 0)
    return jnp.where((row & m) != 0, q, k) * jnp.exp2(jnp.minimum(d, -d))


def _hgrn_body(hq_ref, lf_ref, kf_ref, gi_ref, sg_ref, nw_ref, tril_ref, lv_ref, o_ref,
               st_scr, b_scr, *, block, step):
    @pl.when(pl.program_id(1) == 0)
    def _():
        st_scr[...] = jnp.zeros(st_scr.shape, F32)

    tril = tril_ref[...]
    lv = lv_ref[...]
    halves = _hgrn_levels(block)

    units = [(u, r0, h) for u, (r0, h) in enumerate(
        (r0, h) for r0 in range(0, step, block) for h in range(HEADS))]

    def cols(h):
        return slice(h * HEAD_W, (h + 1) * HEAD_W)

    q, k, v, b, scores = {}, {}, {}, {}, {}
    for u, r0, h in units:
        rows = slice(r0, r0 + block)
        q[u] = hq_ref[0, rows, cols(h)]
        k[u] = kf_ref[0, rows, cols(h)]
        v[u] = gi_ref[0, rows, cols(h)]
        lf = lf_ref[0, rows, cols(h)]
        hi = lf.astype(BF16)
        lo = (lf - hi.astype(F32)).astype(BF16)
        b[u] = (jnp.dot(tril, hi, preferred_element_type=F32)
                + jnp.dot(tril, lo, preferred_element_type=F32))
        b_scr[u] = b[u]
        scores[u] = jnp.where(lv == len(halves),
                              lax.dot_general(q[u].astype(BF16), k[u].astype(BF16), NT_DIMS,
                                              preferred_element_type=F32), 0.0)

    for li, m in enumerate(halves):
        for u, r0, h in units:
            xl = _level_operand(b_scr, u, b[u], q[u], k[u], m, block).astype(BF16)
            p = lax.dot_general(xl, xl, NT_DIMS, preferred_element_type=F32)
            scores[u] = jnp.where(lv == li, p, scores[u])

    for u, r0, h in units:
        rows = slice(r0, r0 + block)
        o_intra = jnp.dot(scores[u].astype(BF16), v[u], preferred_element_type=F32)
        st = st_scr[h]
        o_inter = lax.dot_general((q[u] * jnp.exp2(b[u])).astype(BF16), st.astype(BF16), NT_DIMS,
                                  preferred_element_type=F32)
        b_last = b[u][block - 1:block, :]
        kdec = (k[u] * jnp.exp2(b_last - b[u])).astype(BF16)
        st_scr[h] = st * jnp.exp2(b_last) + lax.dot_general(v[u], kdec, TN_DIMS,
                                                             preferred_element_type=F32)
        o = o_inter + o_intra
        o = o * lax.rsqrt(jnp.mean(o * o, axis=-1, keepdims=True) + EPS) * nw_ref[...]
        o_ref[0, rows, cols(h)] = (o * sg_ref[0, rows, cols(h)].astype(F32)).astype(BF16)


def _hgrn(hq3, lf3, kf3, gi3, sg3, norm_w):
    bsz, seq, _ = hq3.shape
    block = HGRN_BLOCK
    step = HGRN_STEP
    tril, lv = _hgrn_constants(block)
    blk = pl.BlockSpec((1, step, STREAM_W), lambda b, g: (b, g, 0))
    const = lambda b, g: (0, 0)
    return pl.pallas_call(
        functools.partial(_hgrn_body, block=block, step=step),
        grid=(bsz, seq // step),
        in_specs=[blk, blk, blk, blk, blk,
                  pl.BlockSpec((1, HEAD_W), const),
                  pl.BlockSpec((block, block), const),
                  pl.BlockSpec((block, block), const)],
        out_specs=blk,
        out_shape=jax.ShapeDtypeStruct((bsz, seq, STREAM_W), BF16),
        scratch_shapes=[pltpu.VMEM((HEADS, HEAD_W, HEAD_W), F32),
                        pltpu.VMEM((HEADS * step // block, block, HEAD_W), F32)],
        name="hgrn2",
    )(hq3, lf3, kf3, gi3, sg3, norm_w, tril, lv)


META_E0, META_E1, META_R0, META_R1, META_W0, META_W1 = range(6)
GROUP_LANE0 = N_EXPERTS


OUT_PARTS = 8


def _route(logits, lane):
    far = jnp.int32(LANES)

    def first_max(vals):
        mx = jnp.max(vals, axis=-1, keepdims=True)
        return mx, jnp.min(jnp.where(vals == mx, lane, far), axis=-1, keepdims=True)

    is_g = (lane >= GROUP_LANE0) & (lane < GROUP_LANE0 + N_GROUPS)
    gmax, glane = first_max(jnp.where(is_g, logits, NEG_BIG))
    g_w = 1.0 / jnp.sum(jnp.where(is_g, jnp.exp(logits - gmax), 0.0), axis=-1, keepdims=True)
    gidx = glane - GROUP_LANE0
    in_grp = (lane < N_EXPERTS) & ((lane // EXPERTS_PER_GROUP) == gidx)
    el = jnp.where(in_grp, logits, NEG_BIG)
    m1, i1 = first_max(el)
    m2, i2 = first_max(jnp.where(lane == i1, NEG_BIG, el))
    r = jnp.exp(m2 - m1)
    return i1, i2, g_w / (1.0 + r), g_w * r / (1.0 + r)


def _outproj_body(da_ref, hg_ref, x_ref, mod_ref, wo_ref, nw_ref, wr_ref, br_ref, stril_ref,
                  x1_ref, h2_ref, meta_ref, route_ref, cnt_ref, carry_scr, piece_scr):
    @pl.when(pl.program_id(0) == 0)
    def _():
        carry_scr[...] = jnp.zeros(carry_scr.shape, F32)

    rows = stril_ref.shape[0]
    parts = [slice(p * rows, (p + 1) * rows) for p in range(OUT_PARTS)]
    gate1 = mod_ref[0, 2:3, :]
    shift2 = mod_ref[0, 3:4, :]
    scale2 = mod_ref[0, 4:5, :]

    attn = [jnp.dot(da_ref[r, :], wo_ref[0:STREAM_W, :], preferred_element_type=F32)
            + jnp.dot(hg_ref[r, :], wo_ref[STREAM_W:, :], preferred_element_type=F32) for r in parts]

    h2 = []
    for r, a in zip(parts, attn):
        x1 = x_ref[r, :] + gate1 * a
        x1_ref[r, :] = x1
        h2.append(x1 * lax.rsqrt(jnp.mean(x1 * x1, axis=-1, keepdims=True) + EPS) * nw_ref[...]
                  * (1.0 + scale2) + shift2)

    logits = []
    for p, h in enumerate(h2):
        _slab_store_bf16(h2_ref.at[pl.ds(p * rows * SLAB, rows * SLAB)], h, piece_scr.at[p])
        h_hi = h.astype(BF16)
        h_lo = (h - h_hi.astype(F32)).astype(BF16)
        terms = jnp.dot(jnp.concatenate([h_hi, h_lo], axis=0), wr_ref[...],
                        preferred_element_type=F32)
        logits.append((terms[:rows, :LANES] + terms[:rows, LANES:])
                      + (terms[rows:, :LANES] + terms[rows:, LANES:]) + br_ref[...])

    lane = lax.broadcasted_iota(jnp.int32, (rows, LANES), 1)
    routed = [_route(lg, lane) for lg in logits]

    carry = carry_scr[...]
    for r, (i1, i2, w0, w1) in zip(parts, routed):
        hot0 = lane == i1
        hot1 = lane == i2
        multi = jnp.where(hot0 | hot1, 1.0, 0.0)
        before = jnp.dot(stril_ref[...], multi.astype(BF16), preferred_element_type=F32) + carry
        rank0 = jnp.sum(jnp.where(hot0, before, 0.0), axis=-1, keepdims=True)
        rank1 = jnp.sum(jnp.where(hot1, before, 0.0), axis=-1, keepdims=True)
        carry = carry + jnp.sum(multi, axis=0, keepdims=True)
        meta = jnp.zeros((rows, LANES), F32)
        for idx, val in ((META_E0, i1.astype(F32)), (META_E1, i2.astype(F32)),
                         (META_R0, rank0), (META_R1, rank1), (META_W0, w0), (META_W1, w1)):
            meta = jnp.where(lane == idx, val, meta)
        meta_ref[r, :] = meta
        route_ref[:, r] = meta.T[0:SUBLANES, :]
    carry_scr[...] = carry
    cnt_ref[...] = carry


def _outproj(da2, hg2, x2, mod3, wo_bf, norm_w, w_route, b_route, seq):
    t, d = x2.shape
    tm = OUT_TILE
    per_b = seq // tm
    row = lambda i: (i, 0)
    full = lambda i: (0, 0)
    part = tm // OUT_PARTS
    stril = jnp.asarray(np.tril(np.ones((part, part), np.float32), -1), BF16)
    return pl.pallas_call(
        _outproj_body,
        grid=(t // tm,),
        in_specs=[pl.BlockSpec((tm, STREAM_W), row),
                  pl.BlockSpec((tm, STREAM_W), row),
                  pl.BlockSpec((tm, d), row),
                  pl.BlockSpec((1, 6, d), lambda i: (i // per_b, 0, 0)),
                  pl.BlockSpec((2 * STREAM_W, d), full),
                  pl.BlockSpec((1, d), full),
                  pl.BlockSpec((d, 2 * LANES), full),
                  pl.BlockSpec((1, LANES), full),
                  pl.BlockSpec((part, part), full)],
        out_specs=[pl.BlockSpec((tm, d), row),
                   pl.BlockSpec((tm * SLAB, LANES), row),
                   pl.BlockSpec((tm, LANES), row),
                   pl.BlockSpec((SUBLANES, tm), lambda i: (0, i)),
                   pl.BlockSpec((1, LANES), full)],
        out_shape=[jax.ShapeDtypeStruct((t, d), F32),
                   jax.ShapeDtypeStruct((t * SLAB, LANES), BF16),
                   jax.ShapeDtypeStruct((t, LANES), F32),
                   jax.ShapeDtypeStruct((SUBLANES, t), F32),
                   jax.ShapeDtypeStruct((1, LANES), F32)],
        scratch_shapes=[pltpu.VMEM((1, LANES), F32),
                        pltpu.VMEM((OUT_PARTS, part * SLAB, LANES), F32)],
        compiler_params=pltpu.CompilerParams(dimension_semantics=("arbitrary",)),
        name="outproj_route",
    )(da2, hg2, x2, mod3, wo_bf, norm_w, w_route, b_route, stril)


DMA_UNROLL = 8


ZERO_CHUNK = 128
N_FILL_RANGES = N_EXPERTS + 1


def _scatter_body(fill_ref, pos_ref, h2_ref, wg_ref, wu_ref, wd_ref, xs_hbm, wgu_ref, wdb_ref,
                  zero_scr, sem, zsem, *, tile):
    rows = tile * SLAB

    wgu_ref[:, :, 0:EXPERT_FF] = wg_ref[...].astype(BF16)
    wgu_ref[:, :, EXPERT_FF:] = wu_ref[...].astype(BF16)
    wdb_ref[...] = wd_ref[...].astype(BF16)

    @pl.when(pl.program_id(0) == 0)
    def _():
        zero_scr[...] = jnp.zeros(zero_scr.shape, BF16)

        def clear(chunk):
            start = pl.multiple_of(chunk * (ZERO_CHUNK * SLAB), SLAB)
            return pltpu.make_async_copy(zero_scr, xs_hbm.at[pl.ds(start, ZERO_CHUNK * SLAB)], zsem)

        def start_one(chunk, carry):
            clear(chunk).start()
            return carry

        def wait_one(chunk, carry):
            clear(chunk).wait()
            return carry

        for fn in (start_one, wait_one):
            for r in range(N_FILL_RANGES):
                lax.fori_loop(fill_ref[r], fill_ref[N_FILL_RANGES + r], fn, 0)

    def issue(t, carry):
        src = h2_ref.at[pl.ds(pl.multiple_of(t * SLAB, SLAB), SLAB)]
        for j in range(2):
            slot = pl.multiple_of(pos_ref[0, 0, j * tile + t] * SLAB, SLAB)
            pltpu.make_async_copy(src, xs_hbm.at[pl.ds(slot, SLAB)], sem).start(priority=j)
        return carry

    lax.fori_loop(0, tile, issue, 0, unroll=DMA_UNROLL)
    whole = pltpu.make_async_copy(h2_ref, xs_hbm.at[pl.ds(0, rows)], sem)
    whole.wait()
    whole.wait()


def _scatter(fill_ranges, pos3, h2s, n_slots, w_gate, w_up, w_down):
    tile = SCATTER_TILE
    rows = tile * SLAB
    n_steps = h2s.shape[0] // rows
    n_exp, d, ff = w_gate.shape
    assert n_exp % n_steps == 0, "expert weights are cast in equal shares per scatter step"
    share = n_exp // n_steps
    w_in = lambda i, lt: (i, 0, 0)
    grid_spec = pltpu.PrefetchScalarGridSpec(
        num_scalar_prefetch=1,
        grid=(n_steps,),
        in_specs=[pl.BlockSpec((1, 1, 2 * tile), lambda i, lt: (i, 0, 0), memory_space=pltpu.SMEM),
                  pl.BlockSpec((rows, LANES), lambda i, lt: (i, 0)),
                  pl.BlockSpec((share, d, ff), w_in),
                  pl.BlockSpec((share, d, ff), w_in),
                  pl.BlockSpec((share, ff, d), w_in)],
        out_specs=[pl.BlockSpec(memory_space=pl.ANY),
                   pl.BlockSpec((share, d, 2 * ff), w_in),
                   pl.BlockSpec((share, ff, d), w_in)],
        scratch_shapes=[pltpu.VMEM((ZERO_CHUNK * SLAB, LANES), BF16),
                        pltpu.SemaphoreType.DMA, pltpu.SemaphoreType.DMA],
    )
    return pl.pallas_call(
        functools.partial(_scatter_body, tile=tile),
        grid_spec=grid_spec,
        out_shape=[jax.ShapeDtypeStruct((n_slots * SLAB, LANES), BF16),
                   jax.ShapeDtypeStruct((n_exp, d, 2 * ff), BF16),
                   jax.ShapeDtypeStruct((n_exp, ff, d), BF16)],
        compiler_params=pltpu.CompilerParams(dimension_semantics=("arbitrary",)),
        name="moe_scatter",
    )(fill_ranges, pos3, h2s, w_gate, w_up, w_down)


def _experts_body(te_ref, nv_ref, x_ref, wgu_ref, wd_ref, y_ref, piece_scr, *, tm):
    del te_ref
    i = pl.program_id(0)

    @pl.when(i < nv_ref[0])
    def _():
        part = tm // EXPERT_PARTS
        xr = [x_ref.at[pl.ds(r * part * SLAB, part * SLAB)] for r in range(EXPERT_PARTS)]
        yr = [y_ref.at[pl.ds(r * part * SLAB, part * SLAB)] for r in range(EXPERT_PARTS)]
        xs = [_slab_load_bf16(ref, part, piece_scr.at[r]).astype(BF16) for r, ref in enumerate(xr)]
        gus = [jnp.dot(x, wgu_ref[0], preferred_element_type=F32) for x in xs]
        acts = [(gu[:, 0:EXPERT_FF] * _sigmoid(gu[:, 0:EXPERT_FF]) * gu[:, EXPERT_FF:]).astype(BF16)
                for gu in gus]
        ys = [jnp.dot(act, wd_ref[0], preferred_element_type=F32) for act in acts]
        for ref, y in zip(yr, ys):
            _slab_store(ref, y)

    @pl.when(i >= nv_ref[0])
    def _():
        y_ref[...] = jnp.zeros(y_ref.shape, F32)


def _experts(tile_expert, n_valid, xs, w_gate_up, w_down):
    tm = MOE_TILE
    rows = tm * SLAB
    d = w_gate_up.shape[1]
    live = lambda i, te, nv: (jnp.minimum(i, nv[0] - 1), 0)
    grid_spec = pltpu.PrefetchScalarGridSpec(
        num_scalar_prefetch=2,
        grid=(xs.shape[0] // rows,),
        in_specs=[pl.BlockSpec((rows, LANES), live),
                  pl.BlockSpec((1, d, 2 * EXPERT_FF), lambda i, te, nv: (te[i], 0, 0)),
                  pl.BlockSpec((1, EXPERT_FF, d), lambda i, te, nv: (te[i], 0, 0))],
        out_specs=pl.BlockSpec((rows, LANES), lambda i, te, nv: (i, 0)),
        scratch_shapes=[pltpu.VMEM((EXPERT_PARTS, rows // EXPERT_PARTS, LANES), F32)],
    )
    return pl.pallas_call(
        functools.partial(_experts_body, tm=tm),
        grid_spec=grid_spec,
        out_shape=jax.ShapeDtypeStruct(xs.shape, F32),
        compiler_params=pltpu.CompilerParams(dimension_semantics=("arbitrary",)),
        name="moe_experts",
    )(tile_expert, n_valid, xs, w_gate_up, w_down)


def _combine_body(pos_ref, nxt_ref, x1_ref, meta_ref, mod_ref, nw_ref, ys_hbm, o_ref, rows_scr, sems,
                  *, tile):
    i = pl.program_id(0)
    n = pl.num_programs(0)
    rows = tile * SLAB

    def fetch(idx_ref, buf):
        def issue(t, carry):
            dst = pl.ds(pl.multiple_of(t * SLAB, SLAB), SLAB)
            for j in range(2):
                slot = pl.multiple_of(idx_ref[0, 0, j * tile + t] * SLAB, SLAB)
                pltpu.make_async_copy(ys_hbm.at[pl.ds(slot, SLAB)], rows_scr.at[buf, j, dst],
                                      sems.at[buf]).start(priority=j)
            return carry
        lax.fori_loop(0, tile, issue, 0, unroll=DMA_UNROLL)

    @pl.when(i == 0)
    def _():
        fetch(pos_ref, 0)

    for cur in range(2):
        @pl.when((i % 2 == cur) & (i + 1 < n))
        def _():
            fetch(nxt_ref, 1 - cur)

    for cur in range(2):
        @pl.when(i % 2 == cur)
        def _():
            for j in range(2):
                pltpu.make_async_copy(ys_hbm.at[pl.ds(0, rows)], rows_scr.at[cur, j],
                                      sems.at[cur]).wait()
            meta = meta_ref[...]
            w0 = meta[:, META_W0:META_W0 + 1]
            w1 = meta[:, META_W1:META_W1 + 1]
            y = (w0 * _slab_load(rows_scr.at[cur, 0], tile)
                 + w1 * _slab_load(rows_scr.at[cur, 1], tile))
            x2 = x1_ref[...] + mod_ref[0, 5:6, :] * y
            o_ref[...] = (x2 * lax.rsqrt(jnp.mean(x2 * x2, axis=-1, keepdims=True) + EPS)
                          * nw_ref[...])


def _combine(pos3, x1, meta, mod3, norm_w, ys, seq):
    t, d = x1.shape
    tile = COMBINE_TILE
    per_b = seq // tile
    n = t // tile
    row = lambda i: (i, 0)
    return pl.pallas_call(
        functools.partial(_combine_body, tile=tile),
        grid=(n,),
        in_specs=[pl.BlockSpec((1, 1, 2 * tile), lambda i: (i, 0, 0), memory_space=pltpu.SMEM),
                  pl.BlockSpec((1, 1, 2 * tile), lambda i: (jnp.minimum(i + 1, n - 1), 0, 0),
                               memory_space=pltpu.SMEM),
                  pl.BlockSpec((tile, d), row),
                  pl.BlockSpec((tile, LANES), row),
                  pl.BlockSpec((1, 6, d), lambda i: (i // per_b, 0, 0)),
                  pl.BlockSpec((1, d), lambda i: (0, 0)),
                  pl.BlockSpec(memory_space=pl.ANY)],
        out_specs=pl.BlockSpec((tile, d), row),
        out_shape=jax.ShapeDtypeStruct((t, d), F32),
        scratch_shapes=[pltpu.VMEM((2, 2, tile * SLAB, LANES), F32), pltpu.SemaphoreType.DMA((2,))],
        compiler_params=pltpu.CompilerParams(dimension_semantics=("arbitrary",)),
        name="moe_combine",
    )(pos3, pos3, x1, meta, mod3, norm_w, ys)


def _rope_constants():
    inv_freq = ROPE_THETA ** (-jnp.arange(ROT_HALF, dtype=F32) / ROT_HALF)
    lane = np.arange(LANES) % DA_QK_DIM
    hit = (lane[None, :] < ROT_DIM) & (lane[None, :] % ROT_HALF == np.arange(ROT_HALF)[:, None])
    spread = np.concatenate([hit, hit], axis=0).astype(np.float32)
    return inv_freq.reshape(ROT_HALF, 1), jnp.asarray(spread, BF16)


def kernel(x, c, positions, norm1_w, norm2_w, final_norm_w, ada_w, ada_b, w_in, w_out, da_lambda_q1, da_lambda_k1, da_lambda_q2, da_lambda_k2, da_subln_w, hg_lower_bound, hg_norm_w, moe_w_group, moe_b_group, moe_w_router, moe_b_router, moe_w_gate, moe_w_up, moe_w_down):
    bsz, seq, d = x.shape
    assert d == D_MODEL and norm1_w.shape[0] == 1, "single-layer model of width 1024 only"
    assert seq % ATTN_TILE == 0 and seq % HGRN_STEP == 0 and seq % OUT_TILE == 0
    t = bsz * seq
    x2 = x.reshape(t, d)

    mod3 = _adaln(c, ada_w[0], ada_b).reshape(bsz, 6, d)

    pos_rows = positions.astype(F32).reshape(t // ROW_TILE, 1, ROW_TILE)
    inv_freq, spread = _rope_constants()
    qt, k, vt, hq, lf, kf, gi, sg = _inproj(x2, mod3, norm1_w, w_in[0], pos_rows,
                                          inv_freq, spread, hg_lower_bound, seq)

    as3 = lambda a: a.reshape(bsz, seq, STREAM_W)
    lam_p = jnp.concatenate([da_lambda_q1, da_lambda_k1, da_lambda_q2, da_lambda_k2], axis=0)
    da = _attn(qt, as3(k), vt, lam_p, da_subln_w.reshape(HEAD_W, 1))
    hg = _hgrn(as3(hq), as3(lf), as3(kf), as3(gi), as3(sg), hg_norm_w)

    pad = jnp.zeros((d, LANES - N_EXPERTS - N_GROUPS), F32)
    w_route = jnp.concatenate([moe_w_router[0], moe_w_group[0], pad], axis=1)
    b_route = jnp.concatenate([moe_b_router[0], moe_b_group[0], pad[0]]).reshape(1, LANES)
    w_route_hi = w_route.astype(BF16)
    w_route_lo = (w_route - w_route_hi.astype(F32)).astype(BF16)
    x1, h2, meta, route, cnt = _outproj(da.reshape(t, STREAM_W), hg.reshape(t, STREAM_W), x2, mod3,
                                 w_out[0].astype(BF16), norm2_w,
                                 jnp.concatenate([w_route_hi, w_route_lo], axis=1), b_route, seq)

    counts = cnt[0, :N_EXPERTS].astype(jnp.int32)
    tiles_e = (counts + MOE_TILE - 1) // MOE_TILE
    tile_end = jnp.cumsum(tiles_e)
    offs = (tile_end - tiles_e) * MOE_TILE
    ids = route[META_E0:META_E1 + 1].astype(jnp.int32)
    ranks = route[META_R0:META_R1 + 1].astype(jnp.int32)
    expert_iota = jnp.arange(N_EXPERTS, dtype=jnp.int32)[:, None, None]
    pos = jnp.sum(jnp.where(ids[None] == expert_iota, offs[:, None, None], 0), axis=0) + ranks
    n_tiles = (2 * t) // MOE_TILE + N_EXPERTS
    n_valid = tile_end[-1:]
    tile_ids = jnp.minimum(jnp.arange(n_tiles, dtype=jnp.int32), n_valid - 1)
    tile_expert = jnp.sum(tile_ids[:, None] >= tile_end[None, :], axis=1).astype(jnp.int32)
    def slot_table(tile):
        return pos.reshape(2, t // tile, tile).transpose(1, 0, 2).reshape(t // tile, 1, 2 * tile)

    fill_lo = jnp.concatenate([(offs + counts) // ZERO_CHUNK, n_valid * (MOE_TILE // ZERO_CHUNK)])
    fill_hi = jnp.concatenate([tile_end * (MOE_TILE // ZERO_CHUNK),
                               jnp.full((1,), n_tiles * (MOE_TILE // ZERO_CHUNK), jnp.int32)])
    fill_ranges = jnp.concatenate([fill_lo, fill_hi]).astype(jnp.int32)

    xs, w_gate_up, w_down_bf = _scatter(fill_ranges, slot_table(SCATTER_TILE), h2, n_tiles * MOE_TILE,
                                        moe_w_gate[0], moe_w_up[0], moe_w_down[0])
    ys = _experts(tile_expert, n_valid.astype(jnp.int32), xs, w_gate_up, w_down_bf)
    out = _combine(slot_table(COMBINE_TILE), x1, meta, mod3, final_norm_w.reshape(1, d), ys, seq)
    return out.reshape(bsz, seq, d)
```

```python
import functools
import math

import numpy as np
import jax
import jax.numpy as jnp
from jax import lax
from jax.experimental import pallas as pl
from jax.experimental.pallas import tpu as pltpu

F32 = jnp.float32
BF16 = jnp.bfloat16
HIGHEST = lax.Precision.HIGHEST

LANES = 128
SUBLANES = 8
D_MODEL = 1024
HEADS = 4
HEAD_W = 128
STREAM_W = HEADS * HEAD_W
N_STREAMS = 7
DA_QK_DIM = 64
ROPE_THETA = 500000.0
ROT_DIM = DA_QK_DIM // 4
ROT_HALF = ROT_DIM // 2
N_GROUPS = 4
EXPERTS_PER_GROUP = 8
N_EXPERTS = N_GROUPS * EXPERTS_PER_GROUP
EXPERT_FF = 256
EPS = 1e-6
LAM_INIT = 0.8 - 0.6 * math.exp(-0.3 * 0)
NEG_BIG = -1e30

ROW_TILE = 512
OUT_TILE = 1024
ATTN_TILE = 1024
ATTN_KEY_TILE = 256
HGRN_BLOCK = 128
MOE_TILE = 512
EXPERT_PARTS = 2
SCATTER_TILE = 1024
COMBINE_TILE = 256

NT_DIMS = (((1,), (1,)), ((), ()))
TN_DIMS = (((0,), (0,)), ((), ()))


def _sigmoid(x):
    return 1.0 / (1.0 + jnp.exp(-x))


SLAB = D_MODEL // LANES


def _slab_store(ref, val):
    rows = val.shape[0]
    for c in range(SLAB):
        ref[pl.ds(c, rows, stride=SLAB), :] = val[:, c * LANES:(c + 1) * LANES]


def _slab_load(ref, rows):
    return jnp.concatenate([ref[pl.ds(c, rows, stride=SLAB), :] for c in range(SLAB)], axis=1)


def _slab_store_bf16(ref, val, piece_scr):
    _slab_store(piece_scr, val)
    ref[...] = piece_scr[...].astype(BF16)


def _slab_load_bf16(ref, rows, piece_scr):
    piece_scr[...] = ref[...].astype(F32)
    return _slab_load(piece_scr, rows)


def _adaln_body(c_ref, w_ref, b_ref, o_ref):
    c = c_ref[...]
    ca = c * _sigmoid(c)
    o_ref[...] = jnp.dot(ca, w_ref[...], preferred_element_type=F32, precision=HIGHEST) + b_ref[...]


def _adaln(c, ada_w, ada_b):
    bsz, d = c.shape
    n = ada_w.shape[1]
    tn = 1024
    return pl.pallas_call(
        _adaln_body,
        grid=(n // tn,),
        in_specs=[pl.BlockSpec((bsz, d), lambda j: (0, 0)),
                  pl.BlockSpec((d, tn), lambda j: (0, j)),
                  pl.BlockSpec((1, tn), lambda j: (0, j))],
        out_specs=pl.BlockSpec((bsz, tn), lambda j: (0, j)),
        out_shape=jax.ShapeDtypeStruct((bsz, n), F32),
        name="adaln",
    )(c, ada_w, ada_b)


def _inproj_body(x_ref, mod_ref, nw_ref, w32_ref, pos_ref, invf_ref, spread_ref, lbr_ref,
                 hgnw_ref, tril_ref, lv_ref, q_ref, k_ref, v_ref, hg_ref, w_ref, st_scr, b_scr,
                 *, steps_per_seq):
    @pl.when(pl.program_id(0) == 0)
    def _():
        for j in range(N_STREAMS):
            cols = slice(j * STREAM_W, (j + 1) * STREAM_W)
            w_ref[:, cols] = w32_ref[:, cols].astype(BF16)

    x = x_ref[...]
    ms = jnp.mean(x * x, axis=-1, keepdims=True)
    y = x * lax.rsqrt(ms + EPS) * nw_ref[...]
    shift = mod_ref[0, 0:1, :]
    scale = mod_ref[0, 1:2, :]
    h = (y * (1.0 + scale) + shift).astype(BF16)

    ang_t = invf_ref[...] * pos_ref[0]

    def spread(table_t):
        hi = table_t.astype(BF16)
        lo = (table_t - hi.astype(F32)).astype(BF16)
        return lax.dot_general(jnp.concatenate([hi, lo], axis=0), spread_ref[...], TN_DIMS,
                               preferred_element_type=F32)

    lane = lax.broadcasted_iota(jnp.int32, (1, LANES), 1) % DA_QK_DIM
    cosv = spread(jnp.cos(ang_t)) + jnp.where(lane < ROT_DIM, 0.0, 1.0)
    sinv = spread(jnp.sin(ang_t))
    sin_lo = jnp.where(lane < ROT_HALF, -sinv, 0.0)
    sin_hi = jnp.where((lane >= ROT_HALF) & (lane < ROT_DIM), sinv, 0.0)

    streams = [jnp.dot(h, w_ref[:, j * STREAM_W:(j + 1) * STREAM_W], preferred_element_type=F32)
               for j in range(N_STREAMS)]

    def proj(j):
        return streams[j]

    def rope(t):
        outs = []
        for hb in range(HEADS):
            tc = t[:, hb * HEAD_W:(hb + 1) * HEAD_W]
            outs.append(tc * cosv
                        + pltpu.roll(tc, LANES - ROT_HALF, 1) * sin_lo
                        + pltpu.roll(tc, ROT_HALF, 1) * sin_hi)
        return jnp.concatenate(outs, axis=1)

    q_ref[0] = (rope(proj(0)) * (DA_QK_DIM ** -0.5 * math.log2(math.e))).T.astype(BF16)
    k_ref[...] = rope(proj(1)).astype(BF16)
    v_ref[0] = proj(2).astype(BF16).T

    gq = proj(3)
    hq = gq * _sigmoid(gq)

    a = lbr_ref[...]
    amax = jnp.max(a, axis=0, keepdims=True)
    ea = jnp.exp(a - amax)
    lb = ea[0:1, :] / jnp.sum(ea, axis=0, keepdims=True)
    gf = proj(4)
    f = lb + (1.0 - lb) * _sigmoid(gf)
    lf = jnp.log(f) * math.log2(math.e)
    gi = proj(5).astype(BF16)
    gg = proj(6)

    @pl.when(pl.program_id(0) % steps_per_seq == 0)
    def _():
        st_scr[...] = jnp.zeros(st_scr.shape, F32)

    _hgrn_tile(hq, lf, 1.0 - f, gi, gg * _sigmoid(gg), hgnw_ref, tril_ref[...], lv_ref[...], hg_ref,
               st_scr, b_scr)


def _inproj(x2, mod3, norm_w, w_f32, pos_rows, invf, spread, lb_raw, hg_norm_w, seq):
    t, d = x2.shape
    tm = ROW_TILE
    per_b = seq // tm
    row = lambda i: (i, 0)
    full = lambda i: (0, 0)
    out_bf = jax.ShapeDtypeStruct((t, STREAM_W), BF16)
    out_t = jax.ShapeDtypeStruct((t // seq, STREAM_W, seq), BF16)
    stream = pl.BlockSpec((tm, STREAM_W), row)
    stream_t = pl.BlockSpec((1, STREAM_W, tm), lambda i: (i // per_b, 0, i % per_b))
    tril, lv = _hgrn_constants(HGRN_BLOCK)
    return pl.pallas_call(
        functools.partial(_inproj_body, steps_per_seq=per_b),
        grid=(t // tm,),
        in_specs=[pl.BlockSpec((tm, d), row),
                  pl.BlockSpec((1, 6, d), lambda i: (i // per_b, 0, 0)),
                  pl.BlockSpec((1, d), full),
                  pl.BlockSpec((d, N_STREAMS * STREAM_W), full, pipeline_mode=pl.Buffered(1)),
                  pl.BlockSpec((1, 1, tm), lambda i: (i, 0, 0)),
                  pl.BlockSpec((ROT_HALF, 1), full),
                  pl.BlockSpec((2 * ROT_HALF, LANES), full),
                  pl.BlockSpec(lb_raw.shape, full),
                  pl.BlockSpec((1, HEAD_W), full),
                  pl.BlockSpec((HGRN_BLOCK, HGRN_BLOCK), full),
                  pl.BlockSpec((HGRN_BLOCK, HGRN_BLOCK), full)],
        out_specs=[stream_t, stream, stream_t, stream],
        out_shape=[out_t, out_bf, out_t, out_bf],
        scratch_shapes=[pltpu.VMEM((d, N_STREAMS * STREAM_W), BF16),
                        pltpu.VMEM((HEADS, HEAD_W, HEAD_W), F32),
                        pltpu.VMEM((HEADS * tm // HGRN_BLOCK, HGRN_BLOCK, HEAD_W), F32)],
        compiler_params=pltpu.CompilerParams(dimension_semantics=("arbitrary",),
                                             vmem_limit_bytes=56 * 1024 * 1024),
        name="inproj_hgrn2",
    )(x2, mod3, norm_w, w_f32, pos_rows, invf, spread, lb_raw, hg_norm_w, tril, lv)


ONES_ROWS = 16


def _attn_body(qt_ref, k_ref, vt_ref, lam_ref, sw_ref, o_ref, s_scr, m_scr, a_scr, *, tile, ktile):
    qi = pl.program_id(2)
    qt = qt_ref[0]
    feat = lax.broadcasted_iota(jnp.int32, (HEAD_W, 1), 0)
    zero = jnp.zeros_like(qt)
    qmaps = (jnp.where(feat < DA_QK_DIM, qt, zero), jnp.where(feat >= DA_QK_DIM, qt, zero))

    m_scr[...] = jnp.full(m_scr.shape, NEG_BIG, F32)
    a_scr[...] = jnp.zeros(a_scr.shape, F32)

    def score_block(start, nkeys, q_lo, masked):
        kb = k_ref[0, pl.ds(start, nkeys), :]
        scores = [jnp.dot(kb, qmaps[mp][:, q_lo:], preferred_element_type=F32) for mp in range(2)]
        for mp, s in enumerate(scores):
            if masked:
                key = lax.broadcasted_iota(jnp.int32, s.shape, 0)
                qry = lax.broadcasted_iota(jnp.int32, s.shape, 1)
                s = jnp.where(key <= qry, s, NEG_BIG)
            grouped = s.reshape(nkeys // SUBLANES, SUBLANES, tile - q_lo)
            m_scr[mp, :, q_lo:] = jnp.maximum(m_scr[mp, :, q_lo:], jnp.max(grouped, axis=0))
            s_scr[mp, pl.ds(start, nkeys), q_lo:] = s

    def value_block(start, nkeys, q_lo, col_max):
        vb = jnp.concatenate([vt_ref[0, :, pl.ds(start, nkeys)], jnp.ones((ONES_ROWS, nkeys), BF16)],
                             axis=0)
        probs = [jnp.exp2(s_scr[mp, pl.ds(start, nkeys), q_lo:] - col_max[mp][:, q_lo:]).astype(BF16)
                 for mp in range(2)]
        for mp, p in enumerate(probs):
            a_scr[mp, :, q_lo:] += jnp.dot(vb, p, preferred_element_type=F32)

    diag = pl.multiple_of(qi * tile, tile)
    pieces = [(diag + j * ktile, ktile, j * ktile) for j in range(tile // ktile)]

    def score_body(ki, carry):
        score_block(pl.multiple_of(ki * tile, tile), tile, 0, False)
        return carry

    lax.fori_loop(0, qi, score_body, 0)
    for start, nkeys, q_lo in pieces:
        score_block(start, nkeys, q_lo, True)
    col_max = [jnp.max(m_scr[mp], axis=0, keepdims=True) for mp in range(2)]

    def value_body(ki, carry):
        value_block(pl.multiple_of(ki * tile, tile), tile, 0, col_max)
        return carry

    lax.fori_loop(0, qi, value_body, 0)
    for start, nkeys, q_lo in pieces:
        value_block(start, nkeys, q_lo, col_max)

    lp = lam_ref[...]
    lam = (jnp.exp(jnp.sum(lp[0:1] * lp[1:2], axis=-1, keepdims=True))
           - jnp.exp(jnp.sum(lp[2:3] * lp[3:4], axis=-1, keepdims=True)) + LAM_INIT)
    o = (a_scr[0, 0:HEAD_W, :] / a_scr[0, HEAD_W:HEAD_W + 1, :]
         - lam * (a_scr[1, 0:HEAD_W, :] / a_scr[1, HEAD_W:HEAD_W + 1, :]))
    o = o * lax.rsqrt(jnp.mean(o * o, axis=0, keepdims=True) + EPS) * sw_ref[...]
    o_ref[0] = (o * (1.0 - LAM_INIT)).T.astype(BF16)


def _attn(qt3, k3, vt3, lam_p, subln_col):
    bsz, seq, _ = k3.shape
    tile = ATTN_TILE
    return pl.pallas_call(
        functools.partial(_attn_body, tile=tile, ktile=ATTN_KEY_TILE),
        grid=(bsz, HEADS, seq // tile),
        in_specs=[pl.BlockSpec((1, HEAD_W, tile), lambda b, h, i: (b, h, i)),
                  pl.BlockSpec((1, seq, HEAD_W), lambda b, h, i: (b, 0, h)),
                  pl.BlockSpec((1, HEAD_W, seq), lambda b, h, i: (b, h, 0)),
                  pl.BlockSpec(lam_p.shape, lambda b, h, i: (0, 0)),
                  pl.BlockSpec((HEAD_W, 1), lambda b, h, i: (0, 0))],
        out_specs=pl.BlockSpec((1, tile, HEAD_W), lambda b, h, i: (b, i, h)),
        out_shape=jax.ShapeDtypeStruct((bsz, seq, STREAM_W), BF16),
        scratch_shapes=[pltpu.VMEM((2, seq, tile), F32),
                        pltpu.VMEM((2, SUBLANES, tile), F32),
                        pltpu.VMEM((2, HEAD_W + ONES_ROWS, tile), F32)],
        compiler_params=pltpu.CompilerParams(vmem_limit_bytes=56 * 1024 * 1024),
        name="diff_attn",
    )(qt3, k3, vt3, lam_p, subln_col)


def _hgrn_levels(block):
    return [block >> (i + 1) for i in range(block.bit_length() - 1)]


def _hgrn_constants(block):
    t = np.arange(block)[:, None]
    s = np.arange(block)[None, :]
    tril = (s <= t).astype(np.float32)
    lv = np.full((block, block), -1, np.int32)
    halves = _hgrn_levels(block)
    for li, m in enumerate(halves):
        same = (t // (2 * m)) == (s // (2 * m))
        lv[same & ((t & m) != 0) & ((s & m) == 0)] = li
    lv[np.arange(block), np.arange(block)] = len(halves)
    return jnp.asarray(tril, BF16), jnp.asarray(lv)


def _level_operand(b_ref, h, b, q, k, m, block):
    def ref_rows(r, n):
        return jnp.broadcast_to(b_ref[h, pl.ds(r, 1), :], (n, HEAD_W))

    if m >= SUBLANES:
        pieces = []
        for s0 in range(0, block, 2 * m):
            ref = ref_rows(s0 + m - 1, m)
            lo = slice(s0, s0 + m)
            up = slice(s0 + m, s0 + 2 * m)
            pieces.append(k[lo] * jnp.exp2(ref - b[lo]))
            pieces.append(q[up] * jnp.exp2(b[up] - ref))
        return jnp.concatenate(pieces, axis=0)

    sub = lax.broadcasted_iota(jnp.int32, (SUBLANES, HEAD_W), 0)
    refs = []
    for s0 in range(0, block, SUBLANES):
        piece = ref_rows(s0 + m - 1, SUBLANES)
        for j in range(1, SUBLANES // (2 * m)):
            piece = jnp.where(sub >= 2 * m * j, ref_rows(s0 + 2 * m * j + m - 1, SUBLANES), piece)
        refs.append(piece)
    d = b - jnp.concatenate(refs, axis=0)
    row = lax.broadcasted_iota(jnp.int32, (block, 1), 0)
    return jnp.where((row & m) != 0, q, k) * jnp.exp2(jnp.minimum(d, -d))


def _hgrn_tile(hq, lf, kf, gi, sg, nw_ref, tril, lv, o_ref, st_scr, b_scr):
    block = HGRN_BLOCK
    halves = _hgrn_levels(block)

    units = [(u, r0, h) for u, (r0, h) in enumerate(
        (r0, h) for r0 in range(0, hq.shape[0], block) for h in range(HEADS))]

    def cols(h):
        return slice(h * HEAD_W, (h + 1) * HEAD_W)

    q, k, v, b, scores = {}, {}, {}, {}, {}
    for u, r0, h in units:
        rows = slice(r0, r0 + block)
        q[u] = hq[rows, cols(h)]
        k[u] = kf[rows, cols(h)]
        v[u] = gi[rows, cols(h)]
        lfu = lf[rows, cols(h)]
        hi = lfu.astype(BF16)
        r1 = lfu - hi.astype(F32)
        mid = r1.astype(BF16)
        lo = (r1 - mid.astype(F32)).astype(BF16)
        b[u] = (jnp.dot(tril, hi, preferred_element_type=F32)
                + jnp.dot(tril, mid, preferred_element_type=F32)
                + jnp.dot(tril, lo, preferred_element_type=F32))
        b_scr[u] = b[u]
        scores[u] = jnp.where(lv == len(halves),
                              lax.dot_general(q[u].astype(BF16), k[u].astype(BF16), NT_DIMS,
                                              preferred_element_type=F32), 0.0)

    for li, m in enumerate(halves):
        for u, r0, h in units:
            xl = _level_operand(b_scr, u, b[u], q[u], k[u], m, block).astype(BF16)
            p = lax.dot_general(xl, xl, NT_DIMS, preferred_element_type=F32)
            scores[u] = jnp.where(lv == li, p, scores[u])

    for u, r0, h in units:
        rows = slice(r0, r0 + block)
        o_intra = jnp.dot(scores[u].astype(BF16), v[u], preferred_element_type=F32)
        st = st_scr[h]
        o_inter = lax.dot_general((q[u] * jnp.exp2(b[u])).astype(BF16), st.astype(BF16), NT_DIMS,
                                  preferred_element_type=F32)
        b_last = b[u][block - 1:block, :]
        kdec = (k[u] * jnp.exp2(b_last - b[u])).astype(BF16)
        st_scr[h] = st * jnp.exp2(b_last) + lax.dot_general(v[u], kdec, TN_DIMS,
                                                             preferred_element_type=F32)
        o = o_inter + o_intra
        o = o * lax.rsqrt(jnp.mean(o * o, axis=-1, keepdims=True) + EPS) * nw_ref[...]
        o_ref[rows, cols(h)] = (o * sg[rows, cols(h)]).astype(BF16)


META_E0, META_E1, META_R0, META_R1, META_W0, META_W1 = range(6)
GROUP_LANE0 = N_EXPERTS


OUT_PARTS = 8


def _route(logits, lane):
    far = jnp.int32(LANES)

    def first_max(vals):
        mx = jnp.max(vals, axis=-1, keepdims=True)
        return mx, jnp.min(jnp.where(vals == mx, lane, far), axis=-1, keepdims=True)

    is_g = (lane >= GROUP_LANE0) & (lane < GROUP_LANE0 + N_GROUPS)
    gmax, glane = first_max(jnp.where(is_g, logits, NEG_BIG))
    g_w = 1.0 / jnp.sum(jnp.where(is_g, jnp.exp(logits - gmax), 0.0), axis=-1, keepdims=True)
    gidx = glane - GROUP_LANE0
    in_grp = (lane < N_EXPERTS) & ((lane // EXPERTS_PER_GROUP) == gidx)
    el = jnp.where(in_grp, logits, NEG_BIG)
    m1, i1 = first_max(el)
    m2, i2 = first_max(jnp.where(lane == i1, NEG_BIG, el))
    r = jnp.exp(m2 - m1)
    return i1, i2, g_w / (1.0 + r), g_w * r / (1.0 + r)


def _outproj_body(da_ref, hg_ref, x_ref, mod_ref, wo_ref, nw_ref, wr_ref, br_ref, stril_ref,
                  x1_ref, h2_ref, meta_ref, route_ref, cnt_ref, carry_scr, piece_scr):
    @pl.when(pl.program_id(0) == 0)
    def _():
        carry_scr[...] = jnp.zeros(carry_scr.shape, F32)

    rows = stril_ref.shape[0]
    parts = [slice(p * rows, (p + 1) * rows) for p in range(OUT_PARTS)]
    gate1 = mod_ref[0, 2:3, :]
    shift2 = mod_ref[0, 3:4, :]
    scale2 = mod_ref[0, 4:5, :]

    attn = [jnp.dot(da_ref[r, :], wo_ref[0:STREAM_W, :], preferred_element_type=F32)
            + jnp.dot(hg_ref[r, :], wo_ref[STREAM_W:, :], preferred_element_type=F32) for r in parts]

    h2 = []
    for r, a in zip(parts, attn):
        x1 = x_ref[r, :] + gate1 * a
        x1_ref[r, :] = x1
        h2.append(x1 * lax.rsqrt(jnp.mean(x1 * x1, axis=-1, keepdims=True) + EPS) * nw_ref[...]
                  * (1.0 + scale2) + shift2)

    logits = []
    for p, h in enumerate(h2):
        _slab_store_bf16(h2_ref.at[pl.ds(p * rows * SLAB, rows * SLAB)], h, piece_scr.at[p])
        h_hi = h.astype(BF16)
        h_lo = (h - h_hi.astype(F32)).astype(BF16)
        terms = jnp.dot(jnp.concatenate([h_hi, h_lo], axis=0), wr_ref[...],
                        preferred_element_type=F32)
        logits.append((terms[:rows, :LANES] + terms[:rows, LANES:])
                      + (terms[rows:, :LANES] + terms[rows:, LANES:]) + br_ref[...])

    lane = lax.broadcasted_iota(jnp.int32, (rows, LANES), 1)
    routed = [_route(lg, lane) for lg in logits]

    carry = carry_scr[...]
    for r, (i1, i2, w0, w1) in zip(parts, routed):
        hot0 = lane == i1
        hot1 = lane == i2
        multi = jnp.where(hot0 | hot1, 1.0, 0.0)
        before = jnp.dot(stril_ref[...], multi.astype(BF16), preferred_element_type=F32) + carry
        rank0 = jnp.sum(jnp.where(hot0, before, 0.0), axis=-1, keepdims=True)
        rank1 = jnp.sum(jnp.where(hot1, before, 0.0), axis=-1, keepdims=True)
        carry = carry + jnp.sum(multi, axis=0, keepdims=True)
        meta = jnp.zeros((rows, LANES), F32)
        for idx, val in ((META_E0, i1.astype(F32)), (META_E1, i2.astype(F32)),
                         (META_R0, rank0), (META_R1, rank1), (META_W0, w0), (META_W1, w1)):
            meta = jnp.where(lane == idx, val, meta)
        meta_ref[r, :] = meta
        route_ref[:, r] = meta.T[0:SUBLANES, :]
    carry_scr[...] = carry
    cnt_ref[...] = carry


def _outproj(da2, hg2, x2, mod3, wo_bf, norm_w, w_route, b_route, seq):
    t, d = x2.shape
    tm = OUT_TILE
    per_b = seq // tm
    row = lambda i: (i, 0)
    full = lambda i: (0, 0)
    part = tm // OUT_PARTS
    stril = jnp.asarray(np.tril(np.ones((part, part), np.float32), -1), BF16)
    return pl.pallas_call(
        _outproj_body,
        grid=(t // tm,),
        in_specs=[pl.BlockSpec((tm, STREAM_W), row),
                  pl.BlockSpec((tm, STREAM_W), row),
                  pl.BlockSpec((tm, d), row),
                  pl.BlockSpec((1, 6, d), lambda i: (i // per_b, 0, 0)),
                  pl.BlockSpec((2 * STREAM_W, d), full),
                  pl.BlockSpec((1, d), full),
                  pl.BlockSpec((d, 2 * LANES), full),
                  pl.BlockSpec((1, LANES), full),
                  pl.BlockSpec((part, part), full)],
        out_specs=[pl.BlockSpec((tm, d), row),
                   pl.BlockSpec((tm * SLAB, LANES), row),
                   pl.BlockSpec((tm, LANES), row),
                   pl.BlockSpec((SUBLANES, tm), lambda i: (0, i)),
                   pl.BlockSpec((1, LANES), full)],
        out_shape=[jax.ShapeDtypeStruct((t, d), F32),
                   jax.ShapeDtypeStruct((t * SLAB, LANES), BF16),
                   jax.ShapeDtypeStruct((t, LANES), F32),
                   jax.ShapeDtypeStruct((SUBLANES, t), F32),
                   jax.ShapeDtypeStruct((1, LANES), F32)],
        scratch_shapes=[pltpu.VMEM((1, LANES), F32),
                        pltpu.VMEM((OUT_PARTS, part * SLAB, LANES), F32)],
        compiler_params=pltpu.CompilerParams(dimension_semantics=("arbitrary",)),
        name="outproj_route",
    )(da2, hg2, x2, mod3, wo_bf, norm_w, w_route, b_route, stril)


DMA_UNROLL = 8


ZERO_CHUNK = 128
N_FILL_RANGES = N_EXPERTS + 1


def _scatter_body(fill_ref, pos_ref, h2_ref, wg_ref, wu_ref, wd_ref, xs_hbm, wgu_ref, wdb_ref,
                  zero_scr, sem, zsem, *, tile):
    rows = tile * SLAB

    wgu_ref[:, :, 0:EXPERT_FF] = wg_ref[...].astype(BF16)
    wgu_ref[:, :, EXPERT_FF:] = wu_ref[...].astype(BF16)
    wdb_ref[...] = wd_ref[...].astype(BF16)

    @pl.when(pl.program_id(0) == 0)
    def _():
        zero_scr[...] = jnp.zeros(zero_scr.shape, BF16)

        def clear(chunk):
            start = pl.multiple_of(chunk * (ZERO_CHUNK * SLAB), SLAB)
            return pltpu.make_async_copy(zero_scr, xs_hbm.at[pl.ds(start, ZERO_CHUNK * SLAB)], zsem)

        def start_one(chunk, carry):
            clear(chunk).start()
            return carry

        def wait_one(chunk, carry):
            clear(chunk).wait()
            return carry

        for fn in (start_one, wait_one):
            for r in range(N_FILL_RANGES):
                lax.fori_loop(fill_ref[r], fill_ref[N_FILL_RANGES + r], fn, 0)

    def issue(t, carry):
        src = h2_ref.at[pl.ds(pl.multiple_of(t * SLAB, SLAB), SLAB)]
        for j in range(2):
            slot = pl.multiple_of(pos_ref[0, 0, j * tile + t] * SLAB, SLAB)
            pltpu.make_async_copy(src, xs_hbm.at[pl.ds(slot, SLAB)], sem).start(priority=j)
        return carry

    lax.fori_loop(0, tile, issue, 0, unroll=DMA_UNROLL)
    whole = pltpu.make_async_copy(h2_ref, xs_hbm.at[pl.ds(0, rows)], sem)
    whole.wait()
    whole.wait()


def _scatter(fill_ranges, pos3, h2s, n_slots, w_gate, w_up, w_down):
    tile = SCATTER_TILE
    rows = tile * SLAB
    n_steps = h2s.shape[0] // rows
    n_exp, d, ff = w_gate.shape
    assert n_exp % n_steps == 0, "expert weights are cast in equal shares per scatter step"
    share = n_exp // n_steps
    w_in = lambda i, lt: (i, 0, 0)
    grid_spec = pltpu.PrefetchScalarGridSpec(
        num_scalar_prefetch=1,
        grid=(n_steps,),
        in_specs=[pl.BlockSpec((1, 1, 2 * tile), lambda i, lt: (i, 0, 0), memory_space=pltpu.SMEM),
                  pl.BlockSpec((rows, LANES), lambda i, lt: (i, 0)),
                  pl.BlockSpec((share, d, ff), w_in),
                  pl.BlockSpec((share, d, ff), w_in),
                  pl.BlockSpec((share, ff, d), w_in)],
        out_specs=[pl.BlockSpec(memory_space=pl.ANY),
                   pl.BlockSpec((share, d, 2 * ff), w_in),
                   pl.BlockSpec((share, ff, d), w_in)],
        scratch_shapes=[pltpu.VMEM((ZERO_CHUNK * SLAB, LANES), BF16),
                        pltpu.SemaphoreType.DMA, pltpu.SemaphoreType.DMA],
    )
    return pl.pallas_call(
        functools.partial(_scatter_body, tile=tile),
        grid_spec=grid_spec,
        out_shape=[jax.ShapeDtypeStruct((n_slots * SLAB, LANES), BF16),
                   jax.ShapeDtypeStruct((n_exp, d, 2 * ff), BF16),
                   jax.ShapeDtypeStruct((n_exp, ff, d), BF16)],
        compiler_params=pltpu.CompilerParams(dimension_semantics=("arbitrary",)),
        name="moe_scatter",
    )(fill_ranges, pos3, h2s, w_gate, w_up, w_down)


def _experts_body(te_ref, nv_ref, x_ref, wgu_ref, wd_ref, y_ref, piece_scr, *, tm):
    del te_ref
    i = pl.program_id(0)

    @pl.when(i < nv_ref[0])
    def _():
        part = tm // EXPERT_PARTS
        xr = [x_ref.at[pl.ds(r * part * SLAB, part * SLAB)] for r in range(EXPERT_PARTS)]
        yr = [y_ref.at[pl.ds(r * part * SLAB, part * SLAB)] for r in range(EXPERT_PARTS)]
        xs = [_slab_load_bf16(ref, part, piece_scr.at[r]).astype(BF16) for r, ref in enumerate(xr)]
        gus = [jnp.dot(x, wgu_ref[0], preferred_element_type=F32) for x in xs]
        acts = [(gu[:, 0:EXPERT_FF] * _sigmoid(gu[:, 0:EXPERT_FF]) * gu[:, EXPERT_FF:]).astype(BF16)
                for gu in gus]
        ys = [jnp.dot(act, wd_ref[0], preferred_element_type=F32) for act in acts]
        for ref, y in zip(yr, ys):
            _slab_store(ref, y)

    @pl.when(i >= nv_ref[0])
    def _():
        y_ref[...] = jnp.zeros(y_ref.shape, F32)


def _experts(tile_expert, n_valid, xs, w_gate_up, w_down):
    tm = MOE_TILE
    rows = tm * SLAB
    d = w_gate_up.shape[1]
    live = lambda i, te, nv: (jnp.minimum(i, nv[0] - 1), 0)
    grid_spec = pltpu.PrefetchScalarGridSpec(
        num_scalar_prefetch=2,
        grid=(xs.shape[0] // rows,),
        in_specs=[pl.BlockSpec((rows, LANES), live),
                  pl.BlockSpec((1, d, 2 * EXPERT_FF), lambda i, te, nv: (te[i], 0, 0)),
                  pl.BlockSpec((1, EXPERT_FF, d), lambda i, te, nv: (te[i], 0, 0))],
        out_specs=pl.BlockSpec((rows, LANES), lambda i, te, nv: (i, 0)),
        scratch_shapes=[pltpu.VMEM((EXPERT_PARTS, rows // EXPERT_PARTS, LANES), F32)],
    )
    return pl.pallas_call(
        functools.partial(_experts_body, tm=tm),
        grid_spec=grid_spec,
        out_shape=jax.ShapeDtypeStruct(xs.shape, F32),
        compiler_params=pltpu.CompilerParams(dimension_semantics=("arbitrary",)),
        name="moe_experts",
    )(tile_expert, n_valid, xs, w_gate_up, w_down)


def _combine_body(pos_ref, nxt_ref, x1_ref, meta_ref, mod_ref, nw_ref, ys_hbm, o_ref, rows_scr, sems,
                  *, tile):
    i = pl.program_id(0)
    n = pl.num_programs(0)
    rows = tile * SLAB

    def fetch(idx_ref, buf):
        def issue(t, carry):
            dst = pl.ds(pl.multiple_of(t * SLAB, SLAB), SLAB)
            for j in range(2):
                slot = pl.multiple_of(idx_ref[0, 0, j * tile + t] * SLAB, SLAB)
                pltpu.make_async_copy(ys_hbm.at[pl.ds(slot, SLAB)], rows_scr.at[buf, j, dst],
                                      sems.at[buf]).start(priority=j)
            return carry
        lax.fori_loop(0, tile, issue, 0, unroll=DMA_UNROLL)

    @pl.when(i == 0)
    def _():
        fetch(pos_ref, 0)

    for cur in range(2):
        @pl.when((i % 2 == cur) & (i + 1 < n))
        def _():
            fetch(nxt_ref, 1 - cur)

    for cur in range(2):
        @pl.when(i % 2 == cur)
        def _():
            for j in range(2):
                pltpu.make_async_copy(ys_hbm.at[pl.ds(0, rows)], rows_scr.at[cur, j],
                                      sems.at[cur]).wait()
            meta = meta_ref[...]
            w0 = meta[:, META_W0:META_W0 + 1]
            w1 = meta[:, META_W1:META_W1 + 1]
            y = (w0 * _slab_load(rows_scr.at[cur, 0], tile)
                 + w1 * _slab_load(rows_scr.at[cur, 1], tile))
            x2 = x1_ref[...] + mod_ref[0, 5:6, :] * y
            o_ref[...] = (x2 * lax.rsqrt(jnp.mean(x2 * x2, axis=-1, keepdims=True) + EPS)
                          * nw_ref[...])


def _combine(pos3, x1, meta, mod3, norm_w, ys, seq):
    t, d = x1.shape
    tile = COMBINE_TILE
    per_b = seq // tile
    n = t // tile
    row = lambda i: (i, 0)
    return pl.pallas_call(
        functools.partial(_combine_body, tile=tile),
        grid=(n,),
        in_specs=[pl.BlockSpec((1, 1, 2 * tile), lambda i: (i, 0, 0), memory_space=pltpu.SMEM),
                  pl.BlockSpec((1, 1, 2 * tile), lambda i: (jnp.minimum(i + 1, n - 1), 0, 0),
                               memory_space=pltpu.SMEM),
                  pl.BlockSpec((tile, d), row),
                  pl.BlockSpec((tile, LANES), row),
                  pl.BlockSpec((1, 6, d), lambda i: (i // per_b, 0, 0)),
                  pl.BlockSpec((1, d), lambda i: (0, 0)),
                  pl.BlockSpec(memory_space=pl.ANY)],
        out_specs=pl.BlockSpec((tile, d), row),
        out_shape=jax.ShapeDtypeStruct((t, d), F32),
        scratch_shapes=[pltpu.VMEM((2, 2, tile * SLAB, LANES), F32), pltpu.SemaphoreType.DMA((2,))],
        compiler_params=pltpu.CompilerParams(dimension_semantics=("arbitrary",)),
        name="moe_combine",
    )(pos3, pos3, x1, meta, mod3, norm_w, ys)


def _rope_constants():
    inv_freq = ROPE_THETA ** (-jnp.arange(ROT_HALF, dtype=F32) / ROT_HALF)
    lane = np.arange(LANES) % DA_QK_DIM
    hit = (lane[None, :] < ROT_DIM) & (lane[None, :] % ROT_HALF == np.arange(ROT_HALF)[:, None])
    spread = np.concatenate([hit, hit], axis=0).astype(np.float32)
    return inv_freq.reshape(ROT_HALF, 1), jnp.asarray(spread, BF16)


def kernel(x, c, positions, norm1_w, norm2_w, final_norm_w, ada_w, ada_b, w_in, w_out, da_lambda_q1, da_lambda_k1, da_lambda_q2, da_lambda_k2, da_subln_w, hg_lower_bound, hg_norm_w, moe_w_group, moe_b_group, moe_w_router, moe_b_router, moe_w_gate, moe_w_up, moe_w_down):
    bsz, seq, d = x.shape
    assert d == D_MODEL and norm1_w.shape[0] == 1, "single-layer model of width 1024 only"
    assert seq % ATTN_TILE == 0 and seq % ROW_TILE == 0 and seq % OUT_TILE == 0
    t = bsz * seq
    x2 = x.reshape(t, d)

    mod3 = _adaln(c, ada_w[0], ada_b).reshape(bsz, 6, d)

    pos_rows = positions.astype(F32).reshape(t // ROW_TILE, 1, ROW_TILE)
    inv_freq, spread = _rope_constants()
    qt, k, vt, hg = _inproj(x2, mod3, norm1_w, w_in[0], pos_rows, inv_freq, spread,
                            hg_lower_bound, hg_norm_w, seq)

    as3 = lambda a: a.reshape(bsz, seq, STREAM_W)
    lam_p = jnp.concatenate([da_lambda_q1, da_lambda_k1, da_lambda_q2, da_lambda_k2], axis=0)
    da = _attn(qt, as3(k), vt, lam_p, da_subln_w.reshape(HEAD_W, 1))

    pad = jnp.zeros((d, LANES - N_EXPERTS - N_GROUPS), F32)
    w_route = jnp.concatenate([moe_w_router[0], moe_w_group[0], pad], axis=1)
    b_route = jnp.concatenate([moe_b_router[0], moe_b_group[0], pad[0]]).reshape(1, LANES)
    w_route_hi = w_route.astype(BF16)
    w_route_lo = (w_route - w_route_hi.astype(F32)).astype(BF16)
    x1, h2, meta, route, cnt = _outproj(da.reshape(t, STREAM_W), hg.reshape(t, STREAM_W), x2, mod3,
                                 w_out[0].astype(BF16), norm2_w,
                                 jnp.concatenate([w_route_hi, w_route_lo], axis=1), b_route, seq)

    counts = cnt[0, :N_EXPERTS].astype(jnp.int32)
    tiles_e = (counts + MOE_TILE - 1) // MOE_TILE
    tile_end = jnp.cumsum(tiles_e)
    offs = (tile_end - tiles_e) * MOE_TILE
    ids = route[META_E0:META_E1 + 1].astype(jnp.int32)
    ranks = route[META_R0:META_R1 + 1].astype(jnp.int32)
    expert_iota = jnp.arange(N_EXPERTS, dtype=jnp.int32)[:, None, None]
    pos = jnp.sum(jnp.where(ids[None] == expert_iota, offs[:, None, None], 0), axis=0) + ranks
    n_tiles = (2 * t) // MOE_TILE + N_EXPERTS
    n_valid = tile_end[-1:]
    tile_ids = jnp.minimum(jnp.arange(n_tiles, dtype=jnp.int32), n_valid - 1)
    tile_expert = jnp.sum(tile_ids[:, None] >= tile_end[None, :], axis=1).astype(jnp.int32)
    def slot_table(tile):
        return pos.reshape(2, t // tile, tile).transpose(1, 0, 2).reshape(t // tile, 1, 2 * tile)

    fill_lo = jnp.concatenate([(offs + counts) // ZERO_CHUNK, n_valid * (MOE_TILE // ZERO_CHUNK)])
    fill_hi = jnp.concatenate([tile_end * (MOE_TILE // ZERO_CHUNK),
                               jnp.full((1,), n_tiles * (MOE_TILE // ZERO_CHUNK), jnp.int32)])
    fill_ranges = jnp.concatenate([fill_lo, fill_hi]).astype(jnp.int32)

    xs, w_gate_up, w_down_bf = _scatter(fill_ranges, slot_table(SCATTER_TILE), h2, n_tiles * MOE_TILE,
                                        moe_w_gate[0], moe_w_up[0], moe_w_down[0])
    ys = _experts(tile_expert, n_valid.astype(jnp.int32), xs, w_gate_up, w_down_bf)
    out = _combine(slot_table(COMBINE_TILE), x1, meta, mod3, final_norm_w.reshape(1, d), ys, seq)
    return out.reshape(bsz, seq, d)
```
